```python
import math
import jax, jax.numpy as jnp
from jax import lax
import numpy as np

D_MODEL = 1024
BATCH = 16
SEQ = 2048
DEPTH = 2

N_META = 16
HEAD_DIM = 64
N_Q_HEADS = 8
N_KV_HEADS = 2
GQA_GROUP = N_Q_HEADS // N_KV_HEADS
ATTN_WIDTH = N_Q_HEADS * HEAD_DIM
KV_WIDTH = N_KV_HEADS * HEAD_DIM
WINDOW = 128
BLOCK = 128
CONV_WIDTH = D_MODEL // 2
CONV_TAPS = 31
N_BRANCHES = 2
REL_BUCKETS = 32
REL_MAX_DIST = 128
D_FF = 2816
FFN_TAPS = 3
LN_EPS = 1e-5
DEEPNORM_ALPHA = (2.0 * DEPTH) ** 0.25
DEEPNORM_BETA = (8.0 * DEPTH) ** -0.25
IN_COLS = ATTN_WIDTH + 2 * KV_WIDTH + 2 * CONV_WIDTH + N_BRANCHES * D_MODEL

kernel_name = "hybrid_conformer_swa_sink_gated"


def layer_norm(x, g, b):
    xf = x.astype(jnp.float32)
    mu = xf.mean(-1, keepdims=True)
    var = jnp.square(xf - mu).mean(-1, keepdims=True)
    return ((xf - mu) * lax.rsqrt(var + LN_EPS) * g + b).astype(x.dtype)


def causal_dwconv(x, w, b):
    taps, c = w.shape
    y = lax.conv_general_dilated(x, w[:, None, :].astype(x.dtype), window_strides=(1,),
                                 padding=[(taps - 1, 0)],
                                 dimension_numbers=('NWC', 'WIO', 'NWC'),
                                 feature_group_count=c)
    return y + b.astype(x.dtype)


def t5_bucket(d):
    n = jnp.maximum(d, 0)
    max_exact = REL_BUCKETS // 2
    nf = jnp.maximum(n, 1).astype(jnp.float32)
    large = max_exact + (jnp.log(nf / max_exact) / math.log(REL_MAX_DIST / max_exact)
                         * (REL_BUCKETS - max_exact)).astype(jnp.int32)
    large = jnp.minimum(large, REL_BUCKETS - 1)
    return jnp.where(n < max_exact, n, large)


def swa_sink_attention(q, k, v, sinks, rel_bias):
    B, T = q.shape[0], q.shape[1]
    pad = BLOCK - N_META
    nblk = (T + pad) // BLOCK
    padw = ((0, 0), (pad, 0), (0, 0), (0, 0))
    qp = jnp.pad(q, padw).reshape(B, nblk, BLOCK, N_KV_HEADS, GQA_GROUP, HEAD_DIM)
    kp = jnp.pad(k, padw).reshape(B, nblk, BLOCK, N_KV_HEADS, HEAD_DIM)
    vp = jnp.pad(v, padw).reshape(B, nblk, BLOCK, N_KV_HEADS, HEAD_DIM)
    shift = ((0, 0), (1, 0), (0, 0), (0, 0), (0, 0))
    kb = jnp.concatenate([jnp.pad(kp[:, :-1], shift), kp], axis=2)
    vb = jnp.concatenate([jnp.pad(vp[:, :-1], shift), vp], axis=2)
    k_meta, v_meta = k[:, :N_META], v[:, :N_META]

    blk = jnp.arange(nblk)[:, None]
    q_pos = blk * BLOCK + jnp.arange(BLOCK)[None, :] - pad
    k_pos = (blk - 1) * BLOCK + jnp.arange(2 * BLOCK)[None, :] - pad
    d_band = q_pos[:, :, None] - k_pos[:, None, :]
    mask_band = (d_band >= 0) & (d_band < WINDOW) & (k_pos >= N_META)[:, None, :]
    d_meta = q_pos[:, :, None] - jnp.arange(N_META)[None, None, :]
    mask_meta = d_meta >= 0

    rb = rel_bias.astype(jnp.float32)
    def bias_of(d):
        bsh = jnp.moveaxis(rb[t5_bucket(d)], -1, 0)
        return bsh.reshape((N_KV_HEADS, GQA_GROUP) + d.shape)

    scale = HEAD_DIM ** -0.5
    neg = jnp.finfo(jnp.float32).min
    s_band = jnp.einsum('bnqhgd,bnkhd->bhgnqk', qp, kb).astype(jnp.float32)
    s_band = jnp.where(mask_band, s_band * scale + bias_of(d_band), neg)
    s_meta = jnp.einsum('bnqhgd,bmhd->bhgnqm', qp, k_meta).astype(jnp.float32)
    s_meta = jnp.where(mask_meta, s_meta * scale + bias_of(d_meta), neg)

    sink = sinks.astype(jnp.float32).reshape(N_KV_HEADS, GQA_GROUP)[None, :, :, None, None, None]
    m = jnp.maximum(jnp.maximum(s_band.max(-1, keepdims=True), s_meta.max(-1, keepdims=True)), sink)
    p_band = jnp.exp(s_band - m)
    p_meta = jnp.exp(s_meta - m)
    denom = p_band.sum(-1, keepdims=True) + p_meta.sum(-1, keepdims=True) + jnp.exp(sink - m)
    p_band = (p_band / denom).astype(v.dtype)
    p_meta = (p_meta / denom).astype(v.dtype)
    o = (jnp.einsum('bhgnqk,bnkhd->bnqhgd', p_band, vb)
         + jnp.einsum('bhgnqm,bmhd->bnqhgd', p_meta, v_meta))
    o = o.reshape(B, nblk * BLOCK, ATTN_WIDTH)
    return o[:, pad:]


def hybrid_layer(h, rel_bias, w_in, b_in, attn_sinks, w_attn_proj, conv_dw, conv_dw_b,
                 conv_ln_g, conv_ln_b, w_conv_proj, w_out, ln1_g, ln1_b,
                 ffn_w_up, ffn_dw, ffn_dw_b, ffn_w_down, ln2_g, ln2_b):
    B, T, _ = h.shape
    z = h @ w_in + b_in
    q, k, v, c_in, gates = jnp.split(
        z, [ATTN_WIDTH, ATTN_WIDTH + KV_WIDTH, ATTN_WIDTH + 2 * KV_WIDTH,
            ATTN_WIDTH + 2 * KV_WIDTH + 2 * CONV_WIDTH], axis=-1)
    a = swa_sink_attention(q.reshape(B, T, N_Q_HEADS, HEAD_DIM),
                           k.reshape(B, T, N_KV_HEADS, HEAD_DIM),
                           v.reshape(B, T, N_KV_HEADS, HEAD_DIM), attn_sinks, rel_bias)
    y_attn = a @ w_attn_proj
    c_val, c_gate = jnp.split(c_in, 2, axis=-1)
    c = c_val * jax.nn.sigmoid(c_gate)
    c = causal_dwconv(c, conv_dw, conv_dw_b)
    c = jax.nn.silu(layer_norm(c, conv_ln_g, conv_ln_b))
    y_conv = c @ w_conv_proj
    g_attn, g_conv = jnp.split(gates, 2, axis=-1)
    mixed = jax.nn.sigmoid(g_attn) * y_attn + jax.nn.sigmoid(g_conv) * y_conv
    h = layer_norm(DEEPNORM_ALPHA * h + mixed @ w_out, ln1_g, ln1_b)
    up = causal_dwconv(h @ ffn_w_up, ffn_dw, ffn_dw_b)
    u, g = jnp.split(up, 2, axis=-1)
    f = (jax.nn.gelu(g, approximate=False) * u) @ ffn_w_down
    return layer_norm(DEEPNORM_ALPHA * h + f, ln2_g, ln2_b)


def _fwd_setup_inputs(seed: int = 0) -> dict:
    key = jax.random.key(seed)
    ks = jax.random.split(key, 24)
    def nrm(k, shape, s):
        return jax.random.normal(k, shape, jnp.float32) * s
    col_scale = jnp.ones((IN_COLS,), jnp.float32).at[
        ATTN_WIDTH + KV_WIDTH:ATTN_WIDTH + 2 * KV_WIDTH].set(DEEPNORM_BETA)
    return {
        "x": nrm(ks[0], (BATCH, SEQ, D_MODEL), 1.0),
        "meta_tokens": nrm(ks[1], (N_META, D_MODEL), 1.0),
        "in_ln_g": 1.0 + nrm(ks[2], (D_MODEL,), 0.02),
        "in_ln_b": nrm(ks[3], (D_MODEL,), 0.02),
        "rel_bias": nrm(ks[4], (REL_BUCKETS, N_Q_HEADS), 0.1),
        "w_in": nrm(ks[5], (DEPTH, D_MODEL, IN_COLS), D_MODEL ** -0.5) * col_scale,
        "b_in": nrm(ks[6], (DEPTH, IN_COLS), 0.01),
        "attn_sinks": nrm(ks[7], (DEPTH, N_Q_HEADS), 0.5),
        "w_attn_proj": nrm(ks[8], (DEPTH, ATTN_WIDTH, D_MODEL), ATTN_WIDTH ** -0.5 * DEEPNORM_BETA),
        "conv_dw": nrm(ks[9], (DEPTH, CONV_TAPS, CONV_WIDTH), CONV_TAPS ** -0.5),
        "conv_dw_b": nrm(ks[10], (DEPTH, CONV_WIDTH), 0.01),
        "conv_ln_g": 1.0 + nrm(ks[11], (DEPTH, CONV_WIDTH), 0.02),
        "conv_ln_b": nrm(ks[12], (DEPTH, CONV_WIDTH), 0.02),
        "w_conv_proj": nrm(ks[13], (DEPTH, CONV_WIDTH, D_MODEL), CONV_WIDTH ** -0.5 * DEEPNORM_BETA),
        "w_out": nrm(ks[14], (DEPTH, D_MODEL, D_MODEL), D_MODEL ** -0.5 * DEEPNORM_BETA),
        "ln1_g": 1.0 + nrm(ks[15], (DEPTH, D_MODEL), 0.02),
        "ln1_b": nrm(ks[16], (DEPTH, D_MODEL), 0.02),
        "ffn_w_up": nrm(ks[17], (DEPTH, D_MODEL, 2 * D_FF), D_MODEL ** -0.5 * DEEPNORM_BETA),
        "ffn_dw": nrm(ks[18], (DEPTH, FFN_TAPS, 2 * D_FF), FFN_TAPS ** -0.5),
        "ffn_dw_b": nrm(ks[19], (DEPTH, 2 * D_FF), 0.01),
        "ffn_w_down": nrm(ks[20], (DEPTH, D_FF, D_MODEL), D_FF ** -0.5 * DEEPNORM_BETA),
        "ln2_g": 1.0 + nrm(ks[21], (DEPTH, D_MODEL), 0.02),
        "ln2_b": nrm(ks[22], (DEPTH, D_MODEL), 0.02),
    }


def _fwd_reference(x, meta_tokens, in_ln_g, in_ln_b, rel_bias, w_in, b_in, attn_sinks,
              w_attn_proj, conv_dw, conv_dw_b, conv_ln_g, conv_ln_b, w_conv_proj, w_out,
              ln1_g, ln1_b, ffn_w_up, ffn_dw, ffn_dw_b, ffn_w_down, ln2_g, ln2_b):
    B = x.shape[0]
    meta = jnp.broadcast_to(meta_tokens[None].astype(x.dtype), (B, N_META, D_MODEL))
    h = jnp.concatenate([meta, x], axis=1)
    h = layer_norm(h, in_ln_g, in_ln_b)
    for i in range(DEPTH):
        h = hybrid_layer(h, rel_bias, w_in[i], b_in[i], attn_sinks[i], w_attn_proj[i],
                         conv_dw[i], conv_dw_b[i], conv_ln_g[i], conv_ln_b[i], w_conv_proj[i],
                         w_out[i], ln1_g[i], ln1_b[i], ffn_w_up[i], ffn_dw[i], ffn_dw_b[i],
                         ffn_w_down[i], ln2_g[i], ln2_b[i])
    return h[:, N_META:]


import jax as _jax
import jax.numpy as _jnp

TWIN_FORMAT = 'train_step'
FWD_PARAMS = ['x', 'meta_tokens', 'in_ln_g', 'in_ln_b', 'rel_bias', 'w_in', 'b_in', 'attn_sinks', 'w_attn_proj', 'conv_dw', 'conv_dw_b', 'conv_ln_g', 'conv_ln_b', 'w_conv_proj', 'w_out', 'ln1_g', 'ln1_b', 'ffn_w_up', 'ffn_dw', 'ffn_dw_b', 'ffn_w_down', 'ln2_g', 'ln2_b']
TWIN_WEIGHTS = ['meta_tokens', 'in_ln_g', 'in_ln_b', 'rel_bias', 'w_in', 'b_in', 'attn_sinks', 'w_attn_proj', 'conv_dw', 'conv_dw_b', 'conv_ln_g', 'conv_ln_b', 'w_conv_proj', 'w_out', 'ln1_g', 'ln1_b', 'ffn_w_up', 'ffn_dw', 'ffn_dw_b', 'ffn_w_down', 'ln2_g', 'ln2_b']
TWIN_DIFF_INPUT = 'x'
TWIN_INPUTS = ['x', 'meta_tokens', 'in_ln_g', 'in_ln_b', 'rel_bias', 'w_in', 'b_in', 'attn_sinks', 'w_attn_proj', 'conv_dw', 'conv_dw_b', 'conv_ln_g', 'conv_ln_b', 'w_conv_proj', 'w_out', 'ln1_g', 'ln1_b', 'ffn_w_up', 'ffn_dw', 'ffn_dw_b', 'ffn_w_down', 'ln2_g', 'ln2_b', 'loss_target', 'm_meta_tokens', 'm_in_ln_g', 'm_in_ln_b', 'm_rel_bias', 'm_w_in', 'm_b_in', 'm_attn_sinks', 'm_w_attn_proj', 'm_conv_dw', 'm_conv_dw_b', 'm_conv_ln_g', 'm_conv_ln_b', 'm_w_conv_proj', 'm_w_out', 'm_ln1_g', 'm_ln1_b', 'm_ffn_w_up', 'm_ffn_dw', 'm_ffn_dw_b', 'm_ffn_w_down', 'm_ln2_g', 'm_ln2_b', 'v_meta_tokens', 'v_in_ln_g', 'v_in_ln_b', 'v_rel_bias', 'v_w_in', 'v_b_in', 'v_attn_sinks', 'v_w_attn_proj', 'v_conv_dw', 'v_conv_dw_b', 'v_conv_ln_g', 'v_conv_ln_b', 'v_w_conv_proj', 'v_w_out', 'v_ln1_g', 'v_ln1_b', 'v_ffn_w_up', 'v_ffn_dw', 'v_ffn_dw_b', 'v_ffn_w_down', 'v_ln2_g', 'v_ln2_b']
TWIN_OUTPUTS = ['loss', 'grad_x', 'grad_meta_tokens', 'grad_in_ln_g', 'grad_in_ln_b', 'grad_rel_bias', 'grad_w_in', 'grad_b_in', 'grad_attn_sinks', 'grad_w_attn_proj', 'grad_conv_dw', 'grad_conv_dw_b', 'grad_conv_ln_g', 'grad_conv_ln_b', 'grad_w_conv_proj', 'grad_w_out', 'grad_ln1_g', 'grad_ln1_b', 'grad_ffn_w_up', 'grad_ffn_dw', 'grad_ffn_dw_b', 'grad_ffn_w_down', 'grad_ln2_g', 'grad_ln2_b', 'delta_meta_tokens', 'delta_in_ln_g', 'delta_in_ln_b', 'delta_rel_bias', 'delta_w_in', 'delta_b_in', 'delta_attn_sinks', 'delta_w_attn_proj', 'delta_conv_dw', 'delta_conv_dw_b', 'delta_conv_ln_g', 'delta_conv_ln_b', 'delta_w_conv_proj', 'delta_w_out', 'delta_ln1_g', 'delta_ln1_b', 'delta_ffn_w_up', 'delta_ffn_dw', 'delta_ffn_dw_b', 'delta_ffn_w_down', 'delta_ln2_g', 'delta_ln2_b', 'new_m_meta_tokens', 'new_m_in_ln_g', 'new_m_in_ln_b', 'new_m_rel_bias', 'new_m_w_in', 'new_m_b_in', 'new_m_attn_sinks', 'new_m_w_attn_proj', 'new_m_conv_dw', 'new_m_conv_dw_b', 'new_m_conv_ln_g', 'new_m_conv_ln_b', 'new_m_w_conv_proj', 'new_m_w_out', 'new_m_ln1_g', 'new_m_ln1_b', 'new_m_ffn_w_up', 'new_m_ffn_dw', 'new_m_ffn_dw_b', 'new_m_ffn_w_down', 'new_m_ln2_g', 'new_m_ln2_b', 'new_v_meta_tokens', 'new_v_in_ln_g', 'new_v_in_ln_b', 'new_v_rel_bias', 'new_v_w_in', 'new_v_b_in', 'new_v_attn_sinks', 'new_v_w_attn_proj', 'new_v_conv_dw', 'new_v_conv_dw_b', 'new_v_conv_ln_g', 'new_v_conv_ln_b', 'new_v_w_conv_proj', 'new_v_w_out', 'new_v_ln1_g', 'new_v_ln1_b', 'new_v_ffn_w_up', 'new_v_ffn_dw', 'new_v_ffn_dw_b', 'new_v_ffn_w_down', 'new_v_ln2_g', 'new_v_ln2_b']
TWIN_LEAF_KINDS = {'loss': 'loss', 'grad_x': 'grad_x', 'grad_meta_tokens': 'grad_w', 'grad_in_ln_g': 'grad_w', 'grad_in_ln_b': 'grad_w', 'grad_rel_bias': 'grad_w', 'grad_w_in': 'grad_w', 'grad_b_in': 'grad_w', 'grad_attn_sinks': 'grad_w', 'grad_w_attn_proj': 'grad_w', 'grad_conv_dw': 'grad_w', 'grad_conv_dw_b': 'grad_w', 'grad_conv_ln_g': 'grad_w', 'grad_conv_ln_b': 'grad_w', 'grad_w_conv_proj': 'grad_w', 'grad_w_out': 'grad_w', 'grad_ln1_g': 'grad_w', 'grad_ln1_b': 'grad_w', 'grad_ffn_w_up': 'grad_w', 'grad_ffn_dw': 'grad_w', 'grad_ffn_dw_b': 'grad_w', 'grad_ffn_w_down': 'grad_w', 'grad_ln2_g': 'grad_w', 'grad_ln2_b': 'grad_w', 'delta_meta_tokens': 'delta_w', 'delta_in_ln_g': 'delta_w', 'delta_in_ln_b': 'delta_w', 'delta_rel_bias': 'delta_w', 'delta_w_in': 'delta_w', 'delta_b_in': 'delta_w', 'delta_attn_sinks': 'delta_w', 'delta_w_attn_proj': 'delta_w', 'delta_conv_dw': 'delta_w', 'delta_conv_dw_b': 'delta_w', 'delta_conv_ln_g': 'delta_w', 'delta_conv_ln_b': 'delta_w', 'delta_w_conv_proj': 'delta_w', 'delta_w_out': 'delta_w', 'delta_ln1_g': 'delta_w', 'delta_ln1_b': 'delta_w', 'delta_ffn_w_up': 'delta_w', 'delta_ffn_dw': 'delta_w', 'delta_ffn_dw_b': 'delta_w', 'delta_ffn_w_down': 'delta_w', 'delta_ln2_g': 'delta_w', 'delta_ln2_b': 'delta_w', 'new_m_meta_tokens': 'new_m', 'new_m_in_ln_g': 'new_m', 'new_m_in_ln_b': 'new_m', 'new_m_rel_bias': 'new_m', 'new_m_w_in': 'new_m', 'new_m_b_in': 'new_m', 'new_m_attn_sinks': 'new_m', 'new_m_w_attn_proj': 'new_m', 'new_m_conv_dw': 'new_m', 'new_m_conv_dw_b': 'new_m', 'new_m_conv_ln_g': 'new_m', 'new_m_conv_ln_b': 'new_m', 'new_m_w_conv_proj': 'new_m', 'new_m_w_out': 'new_m', 'new_m_ln1_g': 'new_m', 'new_m_ln1_b': 'new_m', 'new_m_ffn_w_up': 'new_m', 'new_m_ffn_dw': 'new_m', 'new_m_ffn_dw_b': 'new_m', 'new_m_ffn_w_down': 'new_m', 'new_m_ln2_g': 'new_m', 'new_m_ln2_b': 'new_m', 'new_v_meta_tokens': 'new_v', 'new_v_in_ln_g': 'new_v', 'new_v_in_ln_b': 'new_v', 'new_v_rel_bias': 'new_v', 'new_v_w_in': 'new_v', 'new_v_b_in': 'new_v', 'new_v_attn_sinks': 'new_v', 'new_v_w_attn_proj': 'new_v', 'new_v_conv_dw': 'new_v', 'new_v_conv_dw_b': 'new_v', 'new_v_conv_ln_g': 'new_v', 'new_v_conv_ln_b': 'new_v', 'new_v_w_conv_proj': 'new_v', 'new_v_w_out': 'new_v', 'new_v_ln1_g': 'new_v', 'new_v_ln1_b': 'new_v', 'new_v_ffn_w_up': 'new_v', 'new_v_ffn_dw': 'new_v', 'new_v_ffn_dw_b': 'new_v', 'new_v_ffn_w_down': 'new_v', 'new_v_ln2_g': 'new_v', 'new_v_ln2_b': 'new_v'}


def _forward(args):
    return _fwd_reference(*[args[k] for k in FWD_PARAMS])


def _output_shape():
    out = _jax.eval_shape(lambda: _forward(_fwd_setup_inputs(0)))
    return out.shape, out.dtype

N_MICROBATCH = 1
ADAM_LR = 0.001
ADAM_B1 = 0.9
ADAM_B2 = 0.999
ADAM_EPS = 1e-08
ADAM_WD = 0.01
ADAM_STEP = 10
PER_EXAMPLE_BATCH_AXIS = {'x': 0, 'loss_target': 0}
SHARED_INPUTS = []
_WEIGHT_DTYPES = {'meta_tokens': _jnp.float32, 'in_ln_g': _jnp.float32, 'in_ln_b': _jnp.float32, 'rel_bias': _jnp.float32, 'w_in': _jnp.float32, 'b_in': _jnp.float32, 'attn_sinks': _jnp.float32, 'w_attn_proj': _jnp.float32, 'conv_dw': _jnp.float32, 'conv_dw_b': _jnp.float32, 'conv_ln_g': _jnp.float32, 'conv_ln_b': _jnp.float32, 'w_conv_proj': _jnp.float32, 'w_out': _jnp.float32, 'ln1_g': _jnp.float32, 'ln1_b': _jnp.float32, 'ffn_w_up': _jnp.float32, 'ffn_dw': _jnp.float32, 'ffn_dw_b': _jnp.float32, 'ffn_w_down': _jnp.float32, 'ln2_g': _jnp.float32, 'ln2_b': _jnp.float32}
MOMENT_SCALE = {'meta_tokens': 1.081381e-03, 'in_ln_g': 1.115779e+00, 'in_ln_b': 4.676728e-01, 'rel_bias': 3.672164e-03, 'w_in': 6.503397e-03, 'b_in': 2.462990e-02, 'attn_sinks': 2.761703e-04, 'w_attn_proj': 3.125400e-03, 'conv_dw': 1.533905e-02, 'conv_dw_b': 5.570749e-02, 'conv_ln_g': 2.359962e-02, 'conv_ln_b': 3.001466e-02, 'w_conv_proj': 2.428503e-02, 'w_out': 2.455084e-02, 'ln1_g': 1.129631e+00, 'ln1_b': 4.715024e-01, 'ffn_w_up': 1.125544e-02, 'ffn_dw': 5.667480e-03, 'ffn_dw_b': 1.258316e-02, 'ffn_w_down': 1.838425e-02, 'ln2_g': 2.269070e+01, 'ln2_b': 9.569134e-01}


def _to_microbatches(a, axis):
    t = _jnp.moveaxis(a, axis, 0)
    t = t.reshape((N_MICROBATCH, t.shape[0] // N_MICROBATCH) + t.shape[1:])
    return _jnp.moveaxis(t, 1, axis + 1)


def setup_inputs(seed: int = 0) -> dict:
    inp = _fwd_setup_inputs(seed)
    key = _jax.random.fold_in(_jax.random.key(seed), 7919)
    shape, _ = _output_shape()
    out = dict(inp)
    out["loss_target"] = _jax.random.normal(_jax.random.fold_in(key, 0), shape, _jnp.float32)
    for i, name in enumerate(TWIN_WEIGHTS):
        w = inp[name].astype(_jnp.float32)
        if MOMENT_SCALE is None:
            s = _jnp.sqrt(_jnp.mean(_jnp.square(w)) + 1e-30)
        else:
            s = MOMENT_SCALE[name]
        km, kv = _jax.random.split(_jax.random.fold_in(key, i + 1))
        out[name] = w
        out["m_" + name] = s * _jax.random.normal(km, w.shape, _jnp.float32)
        out["v_" + name] = (s * s) * _jax.random.uniform(kv, w.shape, _jnp.float32, 0.5, 1.5)
    if N_MICROBATCH > 1:
        for name, axis in PER_EXAMPLE_BATCH_AXIS.items():
            out[name] = _to_microbatches(out[name], axis)
    return {'x': out['x'], 'meta_tokens': out['meta_tokens'], 'in_ln_g': out['in_ln_g'], 'in_ln_b': out['in_ln_b'], 'rel_bias': out['rel_bias'], 'w_in': out['w_in'], 'b_in': out['b_in'], 'attn_sinks': out['attn_sinks'], 'w_attn_proj': out['w_attn_proj'], 'conv_dw': out['conv_dw'], 'conv_dw_b': out['conv_dw_b'], 'conv_ln_g': out['conv_ln_g'], 'conv_ln_b': out['conv_ln_b'], 'w_conv_proj': out['w_conv_proj'], 'w_out': out['w_out'], 'ln1_g': out['ln1_g'], 'ln1_b': out['ln1_b'], 'ffn_w_up': out['ffn_w_up'], 'ffn_dw': out['ffn_dw'], 'ffn_dw_b': out['ffn_dw_b'], 'ffn_w_down': out['ffn_w_down'], 'ln2_g': out['ln2_g'], 'ln2_b': out['ln2_b'], 'loss_target': out['loss_target'], 'm_meta_tokens': out['m_meta_tokens'], 'm_in_ln_g': out['m_in_ln_g'], 'm_in_ln_b': out['m_in_ln_b'], 'm_rel_bias': out['m_rel_bias'], 'm_w_in': out['m_w_in'], 'm_b_in': out['m_b_in'], 'm_attn_sinks': out['m_attn_sinks'], 'm_w_attn_proj': out['m_w_attn_proj'], 'm_conv_dw': out['m_conv_dw'], 'm_conv_dw_b': out['m_conv_dw_b'], 'm_conv_ln_g': out['m_conv_ln_g'], 'm_conv_ln_b': out['m_conv_ln_b'], 'm_w_conv_proj': out['m_w_conv_proj'], 'm_w_out': out['m_w_out'], 'm_ln1_g': out['m_ln1_g'], 'm_ln1_b': out['m_ln1_b'], 'm_ffn_w_up': out['m_ffn_w_up'], 'm_ffn_dw': out['m_ffn_dw'], 'm_ffn_dw_b': out['m_ffn_dw_b'], 'm_ffn_w_down': out['m_ffn_w_down'], 'm_ln2_g': out['m_ln2_g'], 'm_ln2_b': out['m_ln2_b'], 'v_meta_tokens': out['v_meta_tokens'], 'v_in_ln_g': out['v_in_ln_g'], 'v_in_ln_b': out['v_in_ln_b'], 'v_rel_bias': out['v_rel_bias'], 'v_w_in': out['v_w_in'], 'v_b_in': out['v_b_in'], 'v_attn_sinks': out['v_attn_sinks'], 'v_w_attn_proj': out['v_w_attn_proj'], 'v_conv_dw': out['v_conv_dw'], 'v_conv_dw_b': out['v_conv_dw_b'], 'v_conv_ln_g': out['v_conv_ln_g'], 'v_conv_ln_b': out['v_conv_ln_b'], 'v_w_conv_proj': out['v_w_conv_proj'], 'v_w_out': out['v_w_out'], 'v_ln1_g': out['v_ln1_g'], 'v_ln1_b': out['v_ln1_b'], 'v_ffn_w_up': out['v_ffn_w_up'], 'v_ffn_dw': out['v_ffn_dw'], 'v_ffn_dw_b': out['v_ffn_dw_b'], 'v_ffn_w_down': out['v_ffn_w_down'], 'v_ln2_g': out['v_ln2_g'], 'v_ln2_b': out['v_ln2_b']}


def _loss(weights, diff, rest, loss_target):
    with _jax.named_scope("forward"):
        args = {**rest, TWIN_DIFF_INPUT: diff, **{k: w.astype(_WEIGHT_DTYPES[k]) for k, w in weights.items()}}
        y = _forward(args)
    with _jax.named_scope("loss_head"):
        err = _jnp.square(y.astype(_jnp.float32) - loss_target)
        return 0.5 * _jnp.sum(_jnp.mean(err, axis=-1)) if err.ndim else 0.5 * err


def _adamw(w, g, m, v):
    m = ADAM_B1 * m + (1.0 - ADAM_B1) * g
    v = ADAM_B2 * v + (1.0 - ADAM_B2) * _jnp.square(g)
    m_hat = m / (1.0 - ADAM_B1 ** ADAM_STEP)
    v_hat = v / (1.0 - ADAM_B2 ** ADAM_STEP)
    delta = -ADAM_LR * (m_hat / (_jnp.sqrt(v_hat) + ADAM_EPS) + ADAM_WD * w)
    return delta, m, v


def reference(x, meta_tokens, in_ln_g, in_ln_b, rel_bias, w_in, b_in, attn_sinks, w_attn_proj, conv_dw, conv_dw_b, conv_ln_g, conv_ln_b, w_conv_proj, w_out, ln1_g, ln1_b, ffn_w_up, ffn_dw, ffn_dw_b, ffn_w_down, ln2_g, ln2_b, loss_target, m_meta_tokens, m_in_ln_g, m_in_ln_b, m_rel_bias, m_w_in, m_b_in, m_attn_sinks, m_w_attn_proj, m_conv_dw, m_conv_dw_b, m_conv_ln_g, m_conv_ln_b, m_w_conv_proj, m_w_out, m_ln1_g, m_ln1_b, m_ffn_w_up, m_ffn_dw, m_ffn_dw_b, m_ffn_w_down, m_ln2_g, m_ln2_b, v_meta_tokens, v_in_ln_g, v_in_ln_b, v_rel_bias, v_w_in, v_b_in, v_attn_sinks, v_w_attn_proj, v_conv_dw, v_conv_dw_b, v_conv_ln_g, v_conv_ln_b, v_w_conv_proj, v_w_out, v_ln1_g, v_ln1_b, v_ffn_w_up, v_ffn_dw, v_ffn_dw_b, v_ffn_w_down, v_ln2_g, v_ln2_b):
    given = dict(x=x, meta_tokens=meta_tokens, in_ln_g=in_ln_g, in_ln_b=in_ln_b, rel_bias=rel_bias, w_in=w_in, b_in=b_in, attn_sinks=attn_sinks, w_attn_proj=w_attn_proj, conv_dw=conv_dw, conv_dw_b=conv_dw_b, conv_ln_g=conv_ln_g, conv_ln_b=conv_ln_b, w_conv_proj=w_conv_proj, w_out=w_out, ln1_g=ln1_g, ln1_b=ln1_b, ffn_w_up=ffn_w_up, ffn_dw=ffn_dw, ffn_dw_b=ffn_dw_b, ffn_w_down=ffn_w_down, ln2_g=ln2_g, ln2_b=ln2_b, loss_target=loss_target, m_meta_tokens=m_meta_tokens, m_in_ln_g=m_in_ln_g, m_in_ln_b=m_in_ln_b, m_rel_bias=m_rel_bias, m_w_in=m_w_in, m_b_in=m_b_in, m_attn_sinks=m_attn_sinks, m_w_attn_proj=m_w_attn_proj, m_conv_dw=m_conv_dw, m_conv_dw_b=m_conv_dw_b, m_conv_ln_g=m_conv_ln_g, m_conv_ln_b=m_conv_ln_b, m_w_conv_proj=m_w_conv_proj, m_w_out=m_w_out, m_ln1_g=m_ln1_g, m_ln1_b=m_ln1_b, m_ffn_w_up=m_ffn_w_up, m_ffn_dw=m_ffn_dw, m_ffn_dw_b=m_ffn_dw_b, m_ffn_w_down=m_ffn_w_down, m_ln2_g=m_ln2_g, m_ln2_b=m_ln2_b, v_meta_tokens=v_meta_tokens, v_in_ln_g=v_in_ln_g, v_in_ln_b=v_in_ln_b, v_rel_bias=v_rel_bias, v_w_in=v_w_in, v_b_in=v_b_in, v_attn_sinks=v_attn_sinks, v_w_attn_proj=v_w_attn_proj, v_conv_dw=v_conv_dw, v_conv_dw_b=v_conv_dw_b, v_conv_ln_g=v_conv_ln_g, v_conv_ln_b=v_conv_ln_b, v_w_conv_proj=v_w_conv_proj, v_w_out=v_w_out, v_ln1_g=v_ln1_g, v_ln1_b=v_ln1_b, v_ffn_w_up=v_ffn_w_up, v_ffn_dw=v_ffn_dw, v_ffn_dw_b=v_ffn_dw_b, v_ffn_w_down=v_ffn_w_down, v_ln2_g=v_ln2_g, v_ln2_b=v_ln2_b)
    weights = {n: given[n] for n in TWIN_WEIGHTS}
    shared = {n: given[n] for n in SHARED_INPUTS}
    per_example = {n: given[n] for n in ['x']}
    grad_fn = _jax.value_and_grad(_loss, argnums=(0, 1))

    def one_microbatch(ex, loss_target):
        ex = dict(ex)
        diff = ex.pop(TWIN_DIFF_INPUT)
        return grad_fn(weights, diff, {**shared, **ex}, loss_target)

    if N_MICROBATCH == 1:
        loss, (grad_w, grad_x) = one_microbatch(per_example, given["loss_target"])
    else:
        def body(carry, xs):
            loss_sum, grad_sum = carry
            l_k, (gw_k, gx_k) = one_microbatch(xs[0], xs[1])
            with _jax.named_scope("update"):
                return (loss_sum + l_k, _jax.tree.map(_jnp.add, grad_sum, gw_k)), gx_k

        init = (_jnp.zeros((), _jnp.float32), _jax.tree.map(_jnp.zeros_like, weights))
        (loss, grad_w), grad_x = _jax.lax.scan(body, init, (per_example, given["loss_target"]))
    with _jax.named_scope("update"):
        delta_w, new_m, new_v = {}, {}, {}
        for n in TWIN_WEIGHTS:
            delta_w[n], new_m[n], new_v[n] = _adamw(weights[n], grad_w[n], given["m_" + n], given["v_" + n])
    return (loss, grad_x, *[grad_w[n] for n in TWIN_WEIGHTS], *[delta_w[n] for n in TWIN_WEIGHTS],
            *[new_m[n] for n in TWIN_WEIGHTS], *[new_v[n] for n in TWIN_WEIGHTS])
```

```python
import functools
import math

import numpy as np
import jax
import jax.numpy as jnp
from jax import lax
from jax.experimental import pallas as pl
from jax.experimental.pallas import tpu as pltpu

F32 = jnp.float32
BF16 = jnp.bfloat16
MESH = pl.DeviceIdType.MESH

D = 1024
N_META = 16
BLK = 128
PAD = BLK - N_META
HD = 64
NQ = 8
NKV = 2
GRP = NQ // NKV
AW = NQ * HD
KVW = NKV * HD
CW = D // 2
CTAPS = 31
FTAPS = 3
NBUCKET = 32
MAXDIST = 128
EPS = 1e-5
DEPTH = 2
ALPHA = (2.0 * DEPTH) ** 0.25
NCHIP = 4
NKEY = 3 * BLK
NEG = -1e30
CHALO = 32
FHALO = 8
IN_COLS = AW + 2 * KVW + 2 * CW + 2 * D
_OLD = dict(q=(0, AW), k=(AW, AW + KVW), v=(AW + KVW, AW + 2 * KVW), cv=(AW + 2 * KVW, AW + 2 * KVW + CW),
            cg=(AW + 2 * KVW + CW, AW + 2 * KVW + 2 * CW), ga=(AW + 2 * KVW + 2 * CW, AW + 2 * KVW + 2 * CW + D),
            gc=(AW + 2 * KVW + 2 * CW + D, IN_COLS))
_NEW_ORDER = ("ga", "gc", "cv", "cg", "q", "k", "v")
C_GATES, C_CONV, C_QKV = 0, 2 * D, 2 * D + 2 * CW

ADAM_LR, ADAM_B1, ADAM_B2, ADAM_EPS, ADAM_WD, ADAM_STEP = 0.001, 0.9, 0.999, 1e-08, 0.01, 10


def _to_new(a):
    return jnp.concatenate([a[..., _OLD[n][0]:_OLD[n][1]] for n in _NEW_ORDER], axis=-1)


def _to_old(a):
    offs, o = {}, 0
    for n in _NEW_ORDER:
        w = _OLD[n][1] - _OLD[n][0]
        offs[n] = (o, o + w)
        o += w
    return jnp.concatenate([a[..., offs[n][0]:offs[n][1]] for n in ("q", "k", "v", "cv", "cg", "ga", "gc")], axis=-1)


def _call(body, name, out_shape, grid, in_specs, out_specs, scratch=(), prefetch=0):
    if prefetch:
        gs = pltpu.PrefetchScalarGridSpec(num_scalar_prefetch=prefetch, grid=grid, in_specs=in_specs,
                                          out_specs=out_specs, scratch_shapes=list(scratch))
        return pl.pallas_call(body, name=name, out_shape=out_shape, grid_spec=gs,
                              compiler_params=pltpu.CompilerParams(dimension_semantics=("arbitrary",) * len(grid)))
    return pl.pallas_call(body, name=name, out_shape=out_shape, grid=grid, in_specs=in_specs, out_specs=out_specs,
                          scratch_shapes=list(scratch),
                          compiler_params=pltpu.CompilerParams(dimension_semantics=("arbitrary",) * len(grid)))


def _row_tile(m):
    best = 32
    for t in range(32, 641, 32):
        if m % t == 0:
            best = t
    return best


def _pad_rows(rows, nex, tp):
    m = rows < PAD
    for b in range(1, nex):
        m = m | ((rows >= b * tp) & (rows < b * tp + PAD))
    return m


def _ln_stats(x):
    mu = jnp.mean(x, axis=-1, keepdims=True)
    xc = x - mu
    var = jnp.mean(xc * xc, axis=-1, keepdims=True)
    rstd = lax.rsqrt(var + EPS)
    return xc * rstd, rstd


def _ln_bwd(dy, xhat, rstd, g):
    dxh = dy * g
    m1 = jnp.mean(dxh, axis=-1, keepdims=True)
    m2 = jnp.mean(dxh * xhat, axis=-1, keepdims=True)
    return rstd * (dxh - m1 - xhat * m2)


def _dot(a, b):
    return jnp.dot(a, b, preferred_element_type=F32)


def _dot_nt(a, b):
    return lax.dot_general(a, b, (((1,), (1,)), ((), ())), preferred_element_type=F32)


def _dot_tn(a, b):
    return lax.dot_general(a, b, (((0,), (0,)), ((), ())), preferred_element_type=F32)


def _sigmoid(x):
    return 1.0 / (1.0 + jnp.exp(-x))


def _gelu_parts(g):
    cdf = 0.5 * (1.0 + lax.erf(g * (1.0 / math.sqrt(2.0))))
    pdf = jnp.exp(-0.5 * g * g) * (1.0 / math.sqrt(2.0 * math.pi))
    return g * cdf, cdf + g * pdf


def _bucket_np(d):
    n = np.maximum(d, 0)
    max_exact = NBUCKET // 2
    nf = np.maximum(n, 1).astype(np.float32)
    large = max_exact + (np.log(nf / np.float32(max_exact)) / np.float32(math.log(MAXDIST / max_exact))
                         * np.float32(NBUCKET - max_exact)).astype(np.int32)
    large = np.minimum(large, NBUCKET - 1)
    return np.where(n < max_exact, n, large).astype(np.int32)


def _bias_index():
    i = np.arange(BLK)[:, None]
    j = np.arange(2 * BLK)[None, :]
    d = BLK + i - j
    band_ok = (d >= 0) & (d < BLK)
    band = _bucket_np(d)
    idx = np.full((3, BLK, NKEY), -1, np.int32)
    m = np.arange(N_META)[None, :]
    d0 = (i - PAD) - m
    idx[0, :, 2 * BLK:2 * BLK + N_META] = np.where(d0 >= 0, _bucket_np(d0), -1)
    ok1 = band_ok & (j >= BLK)
    idx[1, :, :2 * BLK] = np.where(ok1, band, -1)
    idx[1, :, 2 * BLK:2 * BLK + N_META] = _bucket_np((N_META + i) - m)
    idx[2, :, :2 * BLK] = np.where(band_ok, band, -1)
    idx[2, :, 2 * BLK:2 * BLK + N_META] = NBUCKET - 1
    return idx


def _bias_build(rel_bias):
    idx = jnp.asarray(_bias_index())

    def body(idx_ref, rb_ref, o_ref):
        ix = idx_ref[...]
        for h in range(NQ):
            acc = jnp.full(ix.shape, NEG, F32)
            for b in range(NBUCKET):
                acc = jnp.where(ix == b, rb_ref[b, h], acc)
            o_ref[:, h, :, :] = acc

    return pl.pallas_call(
        body, name="bias_build", out_shape=jax.ShapeDtypeStruct((3, NQ, BLK, NKEY), F32),
        in_specs=[pl.BlockSpec(memory_space=pltpu.VMEM), pl.BlockSpec(memory_space=pltpu.SMEM)],
        out_specs=pl.BlockSpec(memory_space=pltpu.VMEM))(idx, rel_bias)


def _bias_grad(dbias):
    idx = jnp.asarray(_bias_index())

    def body(idx_ref, d_ref, o_ref):
        d = jnp.sum(d_ref[...], axis=0)
        for b in range(NBUCKET):
            acc = jnp.zeros((NQ, NKEY), F32)
            for case in range(3):
                hit = (idx_ref[case] == b)[None, :, :]
                acc = acc + jnp.sum(jnp.where(hit, d[case], 0.0), axis=1)
            o_ref[b] = jnp.sum(acc, axis=-1, keepdims=True)

    out = pl.pallas_call(
        body, name="bias_grad", out_shape=jax.ShapeDtypeStruct((NBUCKET, NQ, 1), F32),
        in_specs=[pl.BlockSpec(memory_space=pltpu.VMEM), pl.BlockSpec(memory_space=pltpu.VMEM)],
        out_specs=pl.BlockSpec(memory_space=pltpu.VMEM))(idx, dbias)
    return out.reshape(NBUCKET, NQ)


def _embed_ln(x, meta, g, b, nblk):
    nex, seq, _ = x.shape
    m = nex * nblk * BLK

    def body(x_ref, meta_ref, g_ref, b_ref, raw_ref, h_ref, hb_ref):
        j = pl.program_id(1)

        @pl.when(j == 0)
        def _():
            raw_ref[0:PAD, :] = jnp.zeros((PAD, D), F32)
            raw_ref[PAD:BLK, :] = meta_ref[...]

        @pl.when(j > 0)
        def _():
            raw_ref[...] = x_ref[...]

        xhat, _ = _ln_stats(raw_ref[...])
        y = xhat * g_ref[...] + b_ref[...]
        h_ref[...] = y
        hb_ref[...] = y.astype(BF16)

    row = lambda bb, j: (bb * nblk + j, 0)
    return _call(
        body, "embed_ln",
        (jax.ShapeDtypeStruct((m, D), F32), jax.ShapeDtypeStruct((m, D), F32), jax.ShapeDtypeStruct((m, D), BF16)),
        (nex, nblk),
        [pl.BlockSpec((None, BLK, D), lambda bb, j: (bb, jnp.maximum(j - 1, 0), 0)),
         pl.BlockSpec((N_META, D), lambda bb, j: (0, 0)),
         pl.BlockSpec((1, D), lambda bb, j: (0, 0)), pl.BlockSpec((1, D), lambda bb, j: (0, 0))],
        (pl.BlockSpec((BLK, D), row), pl.BlockSpec((BLK, D), row), pl.BlockSpec((BLK, D), row)),
    )(x, meta, g.reshape(1, D), b.reshape(1, D))


def _mm_bias(name, a, w, bias, tn, tm):
    m, k = a.shape
    n = w.shape[1]

    def body(a_ref, w_ref, b_ref, o_ref):
        o_ref[...] = _dot(a_ref[...], w_ref[...]) + b_ref[...]

    return _call(body, name, jax.ShapeDtypeStruct((m, n), F32), (n // tn, m // tm),
                 [pl.BlockSpec((tm, k), lambda j, i: (i, 0)), pl.BlockSpec((k, tn), lambda j, i: (0, j)),
                  pl.BlockSpec((1, tn), lambda j, i: (0, j))],
                 pl.BlockSpec((tm, tn), lambda j, i: (i, j)))(a, w, bias)


def _ffn_up(a, w4, tm, nex, tp):
    m, k = a.shape
    ffs = w4.shape[2]

    def body(a_ref, w_ref, o_ref):
        i = pl.program_id(1)
        rows = i * tm + lax.broadcasted_iota(jnp.int32, (tm, 1), 0)
        acc = _dot(a_ref[...], w_ref[...])
        o_ref[...] = jnp.where(_pad_rows(rows, nex, tp), 0.0, acc)

    return _call(body, "ffn_up", jax.ShapeDtypeStruct((2, m, 2 * ffs), F32), (NCHIP, m // tm),
                 [pl.BlockSpec((tm, k), lambda j, i: (i, 0)), pl.BlockSpec((None, k, ffs), lambda j, i: (j, 0, 0))],
                 pl.BlockSpec((None, tm, ffs), lambda j, i: (j // 2, i, j % 2)))(a, w4)


def _fill_kv(ks, vs, prev_ref, cur_ref, meta_ref):
    for piece, lo, n in ((prev_ref, 0, BLK), (cur_ref, BLK, BLK), (meta_ref, 2 * BLK, N_META)):
        val = piece[...]
        for hk in range(NKV):
            ks[hk, lo:lo + n, :] = val[:, hk * HD:(hk + 1) * HD].astype(BF16)
            vs[hk, lo:lo + n, :] = val[:, KVW + hk * HD:KVW + (hk + 1) * HD].astype(BF16)
    for hk in range(NKV):
        ks[hk, 2 * BLK + N_META:NKEY, :] = jnp.zeros((BLK - N_META, HD), BF16)
        vs[hk, 2 * BLK + N_META:NKEY, :] = jnp.zeros((BLK - N_META, HD), BF16)


def _softmax_group(q, ks_hk, bias_ref, sink_ref, hk):
    qg = jnp.concatenate([q[:, (hk * GRP + g) * HD:(hk * GRP + g + 1) * HD] for g in range(GRP)], axis=0).astype(BF16)
    s = _dot_nt(qg, ks_hk) * (HD ** -0.5) + bias_ref[hk * GRP:(hk + 1) * GRP].reshape(GRP * BLK, NKEY)
    sink = jnp.concatenate([jnp.full((BLK, 1), sink_ref[0, hk * GRP + g], F32) for g in range(GRP)], axis=0)
    mx = jnp.maximum(jnp.max(s, axis=-1, keepdims=True), sink)
    p = jnp.exp(s - mx)
    es = jnp.exp(sink - mx)
    inv = 1.0 / (jnp.sum(p, axis=-1, keepdims=True) + es)
    return qg, p * inv, es * inv


def _attn_specs(nblk, tp, blk_of):
    qcol, kvcol = (C_QKV) // AW, (C_QKV + AW) // (2 * KVW)
    return [
        pl.BlockSpec((BLK, AW), lambda bb, j: (bb * nblk + blk_of(j), qcol)),
        pl.BlockSpec((BLK, 2 * KVW), lambda bb, j: (bb * nblk + blk_of(j), kvcol)),
        pl.BlockSpec((BLK, 2 * KVW), lambda bb, j: (bb * nblk + jnp.maximum(blk_of(j) - 1, 0), kvcol)),
        pl.BlockSpec((N_META, 2 * KVW), lambda bb, j: (bb * (tp // N_META) + PAD // N_META, kvcol)),
        pl.BlockSpec((None, NQ, BLK, NKEY), lambda bb, j: (jnp.minimum(blk_of(j), 2), 0, 0, 0)),
        pl.BlockSpec(memory_space=pltpu.SMEM),
    ]


def _attn_fwd(z, bias, sinks, nex, nblk):
    m = z.shape[0]
    tp = nblk * BLK

    def body(q_ref, cur_ref, prev_ref, meta_ref, bias_ref, sink_ref, o_ref, ks, vs, oacc):
        _fill_kv(ks, vs, prev_ref, cur_ref, meta_ref)
        q = q_ref[...]
        for hk in range(NKV):
            _, pn, _ = _softmax_group(q, ks[hk], bias_ref, sink_ref, hk)
            o = _dot(pn.astype(BF16), vs[hk])
            for g in range(GRP):
                h = hk * GRP + g
                oacc[:, h * HD:(h + 1) * HD] = o[g * BLK:(g + 1) * BLK, :]
        o_ref[...] = oacc[...].astype(BF16)

    return _call(body, "attn_fwd", jax.ShapeDtypeStruct((m, AW), BF16), (nex, nblk),
                 _attn_specs(nblk, tp, lambda j: j),
                 pl.BlockSpec((BLK, AW), lambda bb, j: (bb * nblk + j, 0)),
                 scratch=[pltpu.VMEM((NKV, NKEY, HD), BF16), pltpu.VMEM((NKV, NKEY, HD), BF16),
                          pltpu.VMEM((BLK, AW), F32)])(z, z, z, z, bias, sinks.reshape(1, NQ))


def _cgate(cv, cg, rows, nex, tp):
    return jnp.where(_pad_rows(rows, nex, tp), 0.0, cv * _sigmoid(cg))


def _conv_fwd(z, w, wb, g, b, nex, tp):
    m = z.shape[0]
    tm = BLK
    sub = tm // CHALO
    cvc, cgc = C_CONV // CW, C_CONV // CW + 1

    def body(cv_ref, cg_ref, cvh_ref, cgh_ref, w_ref, wb_ref, g_ref, b_ref, cc_ref, cs_ref, win):
        i = pl.program_id(0)
        rows = i * tm + lax.broadcasted_iota(jnp.int32, (tm, 1), 0)
        hrows = i * tm - CHALO + lax.broadcasted_iota(jnp.int32, (CHALO, 1), 0)
        win[0:CHALO, :] = _cgate(cvh_ref[...], cgh_ref[...], hrows, nex, tp)
        win[CHALO:CHALO + tm, :] = _cgate(cv_ref[...], cg_ref[...], rows, nex, tp)
        for sb in range(sub):
            acc = jnp.zeros((CHALO, CW), F32) + wb_ref[...]
            for k in range(CTAPS):
                acc = acc + w_ref[k:k + 1, :] * win[sb * CHALO + 2 + k:sb * CHALO + 2 + k + CHALO, :]
            cc_ref[sb * CHALO:(sb + 1) * CHALO, :] = acc
        xhat, _ = _ln_stats(cc_ref[...])
        cl = xhat * g_ref[...] + b_ref[...]
        cs_ref[...] = (cl * _sigmoid(cl)).astype(BF16)

    halo = lambda i: jnp.maximum(i * sub - 1, 0)
    vec = pl.BlockSpec((1, CW), lambda i: (0, 0))
    return _call(body, "conv_fwd", (jax.ShapeDtypeStruct((m, CW), F32), jax.ShapeDtypeStruct((m, CW), BF16)),
                 (m // tm,),
                 [pl.BlockSpec((tm, CW), lambda i: (i, cvc)), pl.BlockSpec((tm, CW), lambda i: (i, cgc)),
                  pl.BlockSpec((CHALO, CW), lambda i: (halo(i), cvc)), pl.BlockSpec((CHALO, CW), lambda i: (halo(i), cgc)),
                  pl.BlockSpec((CTAPS, CW), lambda i: (0, 0)), vec, vec, vec],
                 (pl.BlockSpec((tm, CW), lambda i: (i, 0)), pl.BlockSpec((tm, CW), lambda i: (i, 0))),
                 scratch=[pltpu.VMEM((CHALO + tm, CW), F32)])(z, z, z, z, w, wb.reshape(1, CW), g.reshape(1, CW), b.reshape(1, CW))


def _mix_fwd(a, cs, wap4, wcp4, z, tm):
    m = a.shape[0]
    ns = wap4.shape[2]

    def body(a_ref, cs_ref, wa_ref, wc_ref, ga_ref, gc_ref, ya_ref, yc_ref, mix_ref):
        av, cv = a_ref[...], cs_ref[...]
        for j in range(NCHIP):
            ya_ref[:, j * ns:(j + 1) * ns] = _dot(av, wa_ref[j])
            yc_ref[:, j * ns:(j + 1) * ns] = _dot(cv, wc_ref[j])
        mix_ref[...] = (_sigmoid(ga_ref[...]) * ya_ref[...] + _sigmoid(gc_ref[...]) * yc_ref[...]).astype(BF16)

    wspec = pl.BlockSpec((NCHIP, AW, ns), lambda i: (0, 0, 0))
    row = lambda i: (i, 0)
    return _call(body, "mix_fwd",
                 (jax.ShapeDtypeStruct((m, D), F32), jax.ShapeDtypeStruct((m, D), F32), jax.ShapeDtypeStruct((m, D), BF16)),
                 (m // tm,),
                 [pl.BlockSpec((tm, AW), row), pl.BlockSpec((tm, CW), row), wspec, wspec,
                  pl.BlockSpec((tm, D), lambda i: (i, 0)), pl.BlockSpec((tm, D), lambda i: (i, 1))],
                 (pl.BlockSpec((tm, D), row), pl.BlockSpec((tm, D), row), pl.BlockSpec((tm, D), row)))(a, cs, wap4, wcp4, z, z)


def _mm_res_ln(name, a, w, res, g, b, tm):
    m, k = a.shape

    def body(a_ref, w_ref, res_ref, g_ref, b_ref, r_ref, h_ref, hb_ref):
        r = ALPHA * res_ref[...] + _dot(a_ref[...], w_ref[...])
        r_ref[...] = r
        xhat, _ = _ln_stats(r)
        y = xhat * g_ref[...] + b_ref[...]
        h_ref[...] = y
        hb_ref[...] = y.astype(BF16)

    row = lambda i: (i, 0)
    vec = pl.BlockSpec((1, D), lambda i: (0, 0))
    return _call(body, name,
                 (jax.ShapeDtypeStruct((m, D), F32), jax.ShapeDtypeStruct((m, D), F32), jax.ShapeDtypeStruct((m, D), BF16)),
                 (m // tm,),
                 [pl.BlockSpec((tm, k), row), pl.BlockSpec((k, D), lambda i: (0, 0)), pl.BlockSpec((tm, D), row), vec, vec],
                 (pl.BlockSpec((tm, D), row), pl.BlockSpec((tm, D), row), pl.BlockSpec((tm, D), row)))(a, w, res, g.reshape(1, D), b.reshape(1, D))


def _conv3(win, w_ref, b_ref, tm):
    return (b_ref[...] + w_ref[0:1, :] * win[FHALO - 2:FHALO - 2 + tm, :] + w_ref[1:2, :] * win[FHALO - 1:FHALO - 1 + tm, :]
            + w_ref[2:3, :] * win[FHALO:FHALO + tm, :])


def _ffn_specs(tm, ffs, m):
    sub = tm // FHALO
    return [
        pl.BlockSpec((2, tm, ffs), lambda c, i: (0, i, c)),
        pl.BlockSpec((2, FHALO, ffs), lambda c, i: (0, jnp.maximum(i * sub - 1, 0), c)),
        pl.BlockSpec((FTAPS, ffs), lambda c, i: (0, c)), pl.BlockSpec((FTAPS, ffs), lambda c, i: (0, c + 2)),
        pl.BlockSpec((1, ffs), lambda c, i: (0, c)), pl.BlockSpec((1, ffs), lambda c, i: (0, c + 2)),
    ]


def _ffn_act(up3, w, wb, tm):
    _, m, dff = up3.shape
    ffs = dff // 2

    def body(x_ref, xh_ref, wu_ref, wg_ref, bu_ref, bg_ref, o_ref, win):
        win[:, 0:FHALO, :] = xh_ref[...]
        win[:, FHALO:FHALO + tm, :] = x_ref[...]
        u = _conv3(win.at[0], wu_ref, bu_ref, tm)
        g = _conv3(win.at[1], wg_ref, bg_ref, tm)
        o_ref[...] = (_gelu_parts(g)[0] * u).astype(BF16)

    return _call(body, "ffn_act", jax.ShapeDtypeStruct((m, dff), BF16), (2, m // tm),
                 _ffn_specs(tm, ffs, m), pl.BlockSpec((tm, ffs), lambda c, i: (i, c)),
                 scratch=[pltpu.VMEM((2, FHALO + tm, ffs), F32)])(up3, up3, w, w, wb, wb)


def _loss_grad(y, target, nblk):
    nex = target.shape[0]
    m = y.shape[0]

    def body(y_ref, t_ref, dy_ref, acc_ref):
        bb, j = pl.program_id(0), pl.program_id(1)

        @pl.when((bb == 0) & (j == 0))
        def _():
            acc_ref[...] = jnp.zeros_like(acc_ref)

        @pl.when(j == 0)
        def _():
            dy_ref[...] = jnp.zeros_like(dy_ref)

        @pl.when(j > 0)
        def _():
            e = y_ref[...] - t_ref[...]
            dy_ref[...] = e * (1.0 / D)
            acc_ref[...] += jnp.sum((e * e).reshape(BLK // 8, 8, D), axis=0)

    return _call(body, "loss_grad", (jax.ShapeDtypeStruct((m, D), F32), jax.ShapeDtypeStruct((8, D), F32)), (nex, nblk),
                 [pl.BlockSpec((BLK, D), lambda bb, j: (bb * nblk + j, 0)),
                  pl.BlockSpec((None, BLK, D), lambda bb, j: (bb, jnp.maximum(j - 1, 0), 0))],
                 (pl.BlockSpec((BLK, D), lambda bb, j: (bb * nblk + j, 0)), pl.BlockSpec((8, D), lambda bb, j: (0, 0))))(y, target)


def _ln_bwd_call(name, dy, r, g, tm, a_list=(), w=None, cols=None):
    m = dy.shape[0]
    na = len(a_list)

    def body(*refs):
        dy_ref, r_ref, g_ref = refs[0:3]
        a_refs = refs[3:3 + na]
        w_ref = refs[3 + na] if na else None
        dr_ref, drb_ref, dg_ref, db_ref = refs[-4:]
        i = pl.program_id(0)
        dh = dy_ref[...]
        if na:
            dh = ALPHA * dh
            if cols is None:
                ns = w.shape[2]
                for j in range(NCHIP):
                    dh = dh + _dot_nt(a_refs[0][j // 2, :, (j % 2) * ns:(j % 2 + 1) * ns], w_ref[j])
            else:
                for a_ref, (c0, c1) in zip(a_refs, cols):
                    dh = dh + _dot_nt(a_ref[...], w_ref[:, c0:c1])
        xhat, rstd = _ln_stats(r_ref[...])
        dr = _ln_bwd(dh, xhat, rstd, g_ref[...])
        dr_ref[...] = dr
        drb_ref[...] = dr.astype(BF16)

        @pl.when(i == 0)
        def _():
            dg_ref[...] = jnp.zeros_like(dg_ref)
            db_ref[...] = jnp.zeros_like(db_ref)

        dg_ref[...] += jnp.sum(dh * xhat, axis=0, keepdims=True)
        db_ref[...] += jnp.sum(dh, axis=0, keepdims=True)

    row = lambda i: (i, 0)
    vec = pl.BlockSpec((1, D), lambda i: (0, 0))
    in_specs = [pl.BlockSpec((tm, D), row), pl.BlockSpec((tm, D), row), vec]
    for a in a_list:
        in_specs.append(pl.BlockSpec((2, tm, a.shape[2]), lambda i: (0, i, 0)) if a.ndim == 3 else pl.BlockSpec((tm, a.shape[1]), row))
    if na:
        in_specs.append(pl.BlockSpec(w.shape, (lambda i: (0, 0, 0)) if w.ndim == 3 else (lambda i: (0, 0))))
    return _call(body, name,
                 (jax.ShapeDtypeStruct((m, D), F32), jax.ShapeDtypeStruct((m, D), BF16),
                  jax.ShapeDtypeStruct((1, D), F32), jax.ShapeDtypeStruct((1, D), F32)),
                 (m // tm,), in_specs,
                 (pl.BlockSpec((tm, D), row), pl.BlockSpec((tm, D), row), vec, vec),
                 )(dy, r, g.reshape(1, D), *a_list, *([w] if na else []))


def _ffn_act_bwd(drb, wdown, up3, w, wb, tm):
    _, m, dff = up3.shape
    ffs = dff // 2

    def body(dr_ref, wd_ref, x_ref, xh_ref, wu_ref, wg_ref, bu_ref, bg_ref, o_ref, win):
        win[:, 0:FHALO, :] = xh_ref[...]
        win[:, FHALO:FHALO + tm, :] = x_ref[...]
        u = _conv3(win.at[0], wu_ref, bu_ref, tm)
        g = _conv3(win.at[1], wg_ref, bg_ref, tm)
        dact = _dot_nt(dr_ref[...], wd_ref[...])
        gel, dgel = _gelu_parts(g)
        o_ref[0] = dact * gel
        o_ref[1] = dact * u * dgel

    specs = [pl.BlockSpec((tm, D), lambda c, i: (i, 0)), pl.BlockSpec((ffs, D), lambda c, i: (c, 0))] + _ffn_specs(tm, ffs, m)
    return _call(body, "ffn_act_bwd", jax.ShapeDtypeStruct((2, m, dff), F32), (2, m // tm), specs,
                 pl.BlockSpec((2, tm, ffs), lambda c, i: (0, i, c)),
                 scratch=[pltpu.VMEM((2, FHALO + tm, ffs), F32)])(drb, wdown, up3, up3, w, w, wb, wb)


def _ffn_conv_bwd(dup3, up3, w, tm, nex, tp):
    _, m, dff = up3.shape
    ffs = dff // 2
    sub = tm // FHALO
    nt = m // tm

    def body(d_ref, dh_ref, x_ref, xh_ref, wu_ref, wg_ref, o_ref, dw_ref, db_ref, dwin, xwin):
        i = pl.program_id(1)
        dwin[:, 0:tm, :] = d_ref[...]
        dwin[:, tm:tm + FHALO, :] = jnp.where(i == nt - 1, 0.0, dh_ref[...])
        xwin[:, 0:FHALO, :] = xh_ref[...]
        xwin[:, FHALO:FHALO + tm, :] = x_ref[...]
        rows = i * tm + lax.broadcasted_iota(jnp.int32, (tm, 1), 0)
        pad = _pad_rows(rows, nex, tp)

        @pl.when(i == 0)
        def _():
            dw_ref[...] = jnp.zeros_like(dw_ref)
            db_ref[...] = jnp.zeros_like(db_ref)

        for p, w_ref in ((0, wu_ref), (1, wg_ref)):
            d0 = dwin[p, 0:tm, :]
            dpre = w_ref[2:3, :] * d0 + w_ref[1:2, :] * dwin[p, 1:1 + tm, :] + w_ref[0:1, :] * dwin[p, 2:2 + tm, :]
            o_ref[p] = jnp.where(pad, 0.0, dpre).astype(BF16)
            for k in range(FTAPS):
                dw_ref[p, k:k + 1, :] += jnp.sum(d0 * xwin[p, FHALO - 2 + k:FHALO - 2 + k + tm, :], axis=0, keepdims=True)
            db_ref[p] += jnp.sum(d0, axis=0, keepdims=True)

    nxt = lambda i: jnp.minimum((i + 1) * sub, m // FHALO - 1)
    prv = lambda i: jnp.maximum(i * sub - 1, 0)
    return _call(body, "ffn_conv_bwd",
                 (jax.ShapeDtypeStruct((2, m, dff), BF16), jax.ShapeDtypeStruct((2, FTAPS, dff), F32),
                  jax.ShapeDtypeStruct((2, 1, dff), F32)),
                 (2, nt),
                 [pl.BlockSpec((2, tm, ffs), lambda c, i: (0, i, c)), pl.BlockSpec((2, FHALO, ffs), lambda c, i: (0, nxt(i), c)),
                  pl.BlockSpec((2, tm, ffs), lambda c, i: (0, i, c)), pl.BlockSpec((2, FHALO, ffs), lambda c, i: (0, prv(i), c)),
                  pl.BlockSpec((FTAPS, ffs), lambda c, i: (0, c)), pl.BlockSpec((FTAPS, ffs), lambda c, i: (0, c + 2))],
                 (pl.BlockSpec((2, tm, ffs), lambda c, i: (0, i, c)), pl.BlockSpec((2, FTAPS, ffs), lambda c, i: (0, 0, c)),
                  pl.BlockSpec((2, 1, ffs), lambda c, i: (0, 0, c))),
                 scratch=[pltpu.VMEM((2, tm + FHALO, ffs), F32), pltpu.VMEM((2, FHALO + tm, ffs), F32)])(dup3, dup3, up3, up3, w, w)


def _mm_tn(name, a, b, tk, tn, b_cols=None, chip_out=False):
    m, k = a.shape
    n = b.shape[-1] * (2 if b.ndim == 3 else 1)

    def body(a_ref, b_ref, o_ref):
        o_ref[...] = _dot_tn(a_ref[...], b_ref[...])

    if b.ndim == 3:
        bspec = pl.BlockSpec((None, m, tn), lambda kk, j: (b_cols(j)[0], 0, b_cols(j)[1]))
    else:
        bspec = pl.BlockSpec((m, tn), lambda kk, j: (0, j))
    if chip_out:
        oshape, ospec = (n // tn, k, tn), pl.BlockSpec((None, tk, tn), lambda kk, j: (j, kk, 0))
    else:
        oshape, ospec = (k, n), pl.BlockSpec((tk, tn), lambda kk, j: (kk, j))
    return _call(body, name, jax.ShapeDtypeStruct(oshape, F32), (k // tk, n // tn),
                 [pl.BlockSpec((m, tk), lambda kk, j: (0, kk)), bspec], ospec)(a, b)


def _gate_bwd(drb, wout, ya, yc, z, tm):
    m = drb.shape[0]

    def body(dr_ref, w_ref, ya_ref, yc_ref, ga_ref, gc_ref, dya_ref, dyc_ref, dz_ref, cs_ref):
        i = pl.program_id(0)
        dmix = _dot_nt(dr_ref[...], w_ref[...])
        sa, sc = _sigmoid(ga_ref[...]), _sigmoid(gc_ref[...])
        dya_ref[...] = (dmix * sa).astype(BF16)
        dyc_ref[...] = (dmix * sc).astype(BF16)
        dga = dmix * ya_ref[...] * sa * (1.0 - sa)
        dgc = dmix * yc_ref[...] * sc * (1.0 - sc)
        dz_ref[:, 0:D] = dga.astype(BF16)
        dz_ref[:, D:2 * D] = dgc.astype(BF16)

        @pl.when(i == 0)
        def _():
            cs_ref[...] = jnp.zeros_like(cs_ref)

        cs_ref[:, 0:D] += jnp.sum(dga, axis=0, keepdims=True)
        cs_ref[:, D:2 * D] += jnp.sum(dgc, axis=0, keepdims=True)

    row = lambda i: (i, 0)
    return _call(body, "gate_bwd",
                 (jax.ShapeDtypeStruct((m, D), BF16), jax.ShapeDtypeStruct((m, D), BF16),
                  jax.ShapeDtypeStruct((m, 2 * D), BF16), jax.ShapeDtypeStruct((1, 2 * D), F32)),
                 (m // tm,),
                 [pl.BlockSpec((tm, D), row), pl.BlockSpec((D, D), lambda i: (0, 0)), pl.BlockSpec((tm, D), row),
                  pl.BlockSpec((tm, D), row), pl.BlockSpec((tm, D), lambda i: (i, 0)), pl.BlockSpec((tm, D), lambda i: (i, 1))],
                 (pl.BlockSpec((tm, D), row), pl.BlockSpec((tm, D), row), pl.BlockSpec((tm, 2 * D), row),
                  pl.BlockSpec((1, 2 * D), lambda i: (0, 0))))(drb, wout, ya, yc, z, z)


def _conv_bwd_a(dyc, wcp4, cc, g, b, tm):
    m = cc.shape[0]
    ns = wcp4.shape[2]

    def body(dy_ref, w_ref, cc_ref, g_ref, b_ref, dcc_ref, dg_ref, db_ref, dwb_ref):
        i = pl.program_id(0)
        dcs = jnp.zeros((tm, CW), F32)
        for j in range(NCHIP):
            dcs = dcs + _dot_nt(dy_ref[:, j * ns:(j + 1) * ns], w_ref[j])
        xhat, rstd = _ln_stats(cc_ref[...])
        cl = xhat * g_ref[...] + b_ref[...]
        sg = _sigmoid(cl)
        dcl = dcs * sg * (1.0 + cl * (1.0 - sg))
        dcc = _ln_bwd(dcl, xhat, rstd, g_ref[...])
        dcc_ref[...] = dcc

        @pl.when(i == 0)
        def _():
            dg_ref[...] = jnp.zeros_like(dg_ref)
            db_ref[...] = jnp.zeros_like(db_ref)
            dwb_ref[...] = jnp.zeros_like(dwb_ref)

        dg_ref[...] += jnp.sum(dcl * xhat, axis=0, keepdims=True)
        db_ref[...] += jnp.sum(dcl, axis=0, keepdims=True)
        dwb_ref[...] += jnp.sum(dcc, axis=0, keepdims=True)

    row = lambda i: (i, 0)
    vec = pl.BlockSpec((1, CW), lambda i: (0, 0))
    v = jax.ShapeDtypeStruct((1, CW), F32)
    return _call(body, "conv_bwd_a", (jax.ShapeDtypeStruct((m, CW), F32), v, v, v), (m // tm,),
                 [pl.BlockSpec((tm, D), row), pl.BlockSpec((NCHIP, CW, ns), lambda i: (0, 0, 0)), pl.BlockSpec((tm, CW), row), vec, vec],
                 (pl.BlockSpec((tm, CW), row), vec, vec, vec))(dyc, wcp4, cc, g.reshape(1, CW), b.reshape(1, CW))


def _conv_bwd_b(dcc, z, w, nex, tp):
    m = dcc.shape[0]
    tm = BLK
    sub = tm // CHALO
    nt = m // tm
    cvc, cgc = C_CONV // CW, C_CONV // CW + 1

    def body(d_ref, dh_ref, cv_ref, cg_ref, cvh_ref, cgh_ref, w_ref, dz_ref, dw_ref, cs_ref, dwin, cwin):
        i = pl.program_id(0)
        rows = i * tm + lax.broadcasted_iota(jnp.int32, (tm, 1), 0)
        hrows = i * tm - CHALO + lax.broadcasted_iota(jnp.int32, (CHALO, 1), 0)
        dwin[0:tm, :] = d_ref[...]
        dwin[tm:tm + CHALO, :] = jnp.where(i == nt - 1, 0.0, dh_ref[...])
        cwin[0:CHALO, :] = _cgate(cvh_ref[...], cgh_ref[...], hrows, nex, tp)
        cwin[CHALO:CHALO + tm, :] = _cgate(cv_ref[...], cg_ref[...], rows, nex, tp)

        @pl.when(i == 0)
        def _():
            dw_ref[...] = jnp.zeros_like(dw_ref)
            cs_ref[...] = jnp.zeros_like(cs_ref)

        for sb in range(sub):
            lo = sb * CHALO
            acc = jnp.zeros((CHALO, CW), F32)
            for k in range(CTAPS):
                acc = acc + w_ref[k:k + 1, :] * dwin[lo + CTAPS - 1 - k:lo + CTAPS - 1 - k + CHALO, :]
            dsb = dwin[lo:lo + CHALO, :]
            for k in range(CTAPS):
                dw_ref[k:k + 1, :] += jnp.sum(dsb * cwin[lo + 2 + k:lo + 2 + k + CHALO, :], axis=0, keepdims=True)
            srows = rows[lo:lo + CHALO]
            dcg = jnp.where(_pad_rows(srows, nex, tp), 0.0, acc)
            cv = cv_ref[lo:lo + CHALO, :]
            sg = _sigmoid(cg_ref[lo:lo + CHALO, :])
            dcv = dcg * sg
            dgt = dcg * cv * sg * (1.0 - sg)
            dz_ref[lo:lo + CHALO, 0:CW] = dcv.astype(BF16)
            dz_ref[lo:lo + CHALO, CW:2 * CW] = dgt.astype(BF16)
            cs_ref[:, 0:CW] += jnp.sum(dcv, axis=0, keepdims=True)
            cs_ref[:, CW:2 * CW] += jnp.sum(dgt, axis=0, keepdims=True)

    nxt = lambda i: jnp.minimum((i + 1) * sub, m // CHALO - 1)
    prv = lambda i: jnp.maximum(i * sub - 1, 0)
    return _call(body, "conv_bwd_b",
                 (jax.ShapeDtypeStruct((m, 2 * CW), BF16), jax.ShapeDtypeStruct((CHALO, CW), F32),
                  jax.ShapeDtypeStruct((1, 2 * CW), F32)),
                 (nt,),
                 [pl.BlockSpec((tm, CW), lambda i: (i, 0)), pl.BlockSpec((CHALO, CW), lambda i: (nxt(i), 0)),
                  pl.BlockSpec((tm, CW), lambda i: (i, cvc)), pl.BlockSpec((tm, CW), lambda i: (i, cgc)),
                  pl.BlockSpec((CHALO, CW), lambda i: (prv(i), cvc)), pl.BlockSpec((CHALO, CW), lambda i: (prv(i), cgc)),
                  pl.BlockSpec((CTAPS, CW), lambda i: (0, 0))],
                 (pl.BlockSpec((tm, 2 * CW), lambda i: (i, 0)), pl.BlockSpec((CHALO, CW), lambda i: (0, 0)),
                  pl.BlockSpec((1, 2 * CW), lambda i: (0, 0))),
                 scratch=[pltpu.VMEM((tm + CHALO, CW), F32), pltpu.VMEM((CHALO + tm, CW), F32)])(dcc, dcc, z, z, z, z, w)


def _attn_bwd(z, bias, sinks, dya, wap4, nex, nblk):
    m = z.shape[0]
    tp = nblk * BLK
    ns = wap4.shape[2]
    blk_of = lambda j: nblk - 1 - j

    def body(q_ref, cur_ref, prev_ref, meta_ref, bias_ref, sink_ref, dy_ref, w_ref,
             dz_ref, cs_ref, dsk_ref, dbias_ref, ks, vs, carry, macc, dqacc, dkv, okv):
        bb, j = pl.program_id(0), pl.program_id(1)
        n = nblk - 1 - j
        _fill_kv(ks, vs, prev_ref, cur_ref, meta_ref)
        q = q_ref[...]
        da = jnp.zeros((BLK, AW), F32)
        for jj in range(NCHIP):
            da = da + _dot_nt(dy_ref[:, jj * ns:(jj + 1) * ns], w_ref[jj])

        @pl.when(j == 0)
        def _():
            carry[...] = jnp.zeros_like(carry)
            macc[...] = jnp.zeros_like(macc)

        @pl.when((bb == 0) & (j == 0))
        def _():
            cs_ref[...] = jnp.zeros_like(cs_ref)
            dsk_ref[...] = jnp.zeros_like(dsk_ref)

        @pl.when((j == 0) | (n <= 1))
        def _():
            dbias_ref[...] = jnp.zeros_like(dbias_ref)

        lane = lax.broadcasted_iota(jnp.int32, (1, BLK), 1)
        dsk = jnp.zeros((1, BLK), F32)
        for hk in range(NKV):
            qg, pn, psink = _softmax_group(q, ks[hk], bias_ref, sink_ref, hk)
            dog = jnp.concatenate([da[:, (hk * GRP + g) * HD:(hk * GRP + g + 1) * HD] for g in range(GRP)], axis=0).astype(BF16)
            dp = _dot_nt(dog, vs[hk])
            dl = jnp.sum(pn * dp, axis=-1, keepdims=True)
            ds = pn * (dp - dl)
            dbias_ref[hk * GRP:(hk + 1) * GRP] += ds.reshape(GRP, BLK, NKEY)
            dsr = -psink * dl
            for g in range(GRP):
                dsk = dsk + jnp.where(lane == hk * GRP + g, jnp.sum(dsr[g * BLK:(g + 1) * BLK]), 0.0)
            dsb = (ds * (HD ** -0.5)).astype(BF16)
            dqg = _dot(dsb, ks[hk])
            for g in range(GRP):
                h = hk * GRP + g
                dqacc[:, h * HD:(h + 1) * HD] = dqg[g * BLK:(g + 1) * BLK, :]
            dkv[:, hk * HD:(hk + 1) * HD] = _dot_tn(dsb, qg)
            dkv[:, KVW + hk * HD:KVW + (hk + 1) * HD] = _dot_tn(pn.astype(BF16), dog)
        dsk_ref[...] += dsk
        macc[...] += dkv[2 * BLK:2 * BLK + N_META, :]
        okv[...] = dkv[BLK:2 * BLK, :] + carry[...]
        carry[...] = dkv[0:BLK, :]

        @pl.when(n == 0)
        def _():
            okv[PAD:BLK, :] += macc[...]

        dq = dqacc[...]
        ok = okv[...]
        dz_ref[:, 0:AW] = dq.astype(BF16)
        dz_ref[:, AW:AW + 2 * KVW] = ok.astype(BF16)
        cs_ref[:, 0:AW] += jnp.sum(dq, axis=0, keepdims=True)
        cs_ref[:, AW:AW + 2 * KVW] += jnp.sum(ok, axis=0, keepdims=True)

    wz = AW + 2 * KVW
    specs = _attn_specs(nblk, tp, blk_of) + [
        pl.BlockSpec((BLK, D), lambda bb, j: (bb * nblk + blk_of(j), 0)),
        pl.BlockSpec((NCHIP, AW, ns), lambda bb, j: (0, 0, 0))]
    return _call(body, "attn_bwd",
                 (jax.ShapeDtypeStruct((m, wz), BF16), jax.ShapeDtypeStruct((1, wz), F32), jax.ShapeDtypeStruct((1, BLK), F32),
                  jax.ShapeDtypeStruct((nex, 3, NQ, BLK, NKEY), F32)),
                 (nex, nblk), specs,
                 (pl.BlockSpec((BLK, wz), lambda bb, j: (bb * nblk + blk_of(j), 0)), pl.BlockSpec((1, wz), lambda bb, j: (0, 0)),
                  pl.BlockSpec((1, BLK), lambda bb, j: (0, 0)),
                  pl.BlockSpec((None, None, NQ, BLK, NKEY), lambda bb, j: (bb, jnp.minimum(blk_of(j), 2), 0, 0, 0))),
                 scratch=[pltpu.VMEM((NKV, NKEY, HD), BF16), pltpu.VMEM((NKV, NKEY, HD), BF16),
                          pltpu.VMEM((BLK, 2 * KVW), F32), pltpu.VMEM((N_META, 2 * KVW), F32),
                          pltpu.VMEM((BLK, AW), F32), pltpu.VMEM((NKEY, 2 * KVW), F32), pltpu.VMEM((BLK, 2 * KVW), F32)],
                 )(z, z, z, z, bias, sinks.reshape(1, NQ), dya, wap4)


def _tile_rows(rows, cols, target_bytes=1 << 20):
    best = None
    for t in range(8, rows + 1, 8):
        if rows % t == 0 and t * cols * 4 <= target_bytes:
            best = t
    return best or rows


def _sum0(name, x):
    n, r, c = x.shape
    tr = _tile_rows(r, c * n)

    def body(x_ref, o_ref):
        acc = x_ref[0]
        for k in range(1, n):
            acc = acc + x_ref[k]
        o_ref[...] = acc

    return _call(body, name, jax.ShapeDtypeStruct((r, c), F32), (r // tr,),
                 [pl.BlockSpec((n, tr, c), lambda i: (0, i, 0))], pl.BlockSpec((tr, c), lambda i: (i, 0)))(x)


def _adamw(name, w, g, mom, vel):
    r, c = w.shape
    tr = _tile_rows(r, c)
    c1 = 1.0 / (1.0 - ADAM_B1 ** ADAM_STEP)
    c2 = 1.0 / (1.0 - ADAM_B2 ** ADAM_STEP)

    def body(w_ref, g_ref, m_ref, v_ref, d_ref, mo_ref, vo_ref):
        gg = g_ref[...]
        mn = ADAM_B1 * m_ref[...] + (1.0 - ADAM_B1) * gg
        vn = ADAM_B2 * v_ref[...] + (1.0 - ADAM_B2) * (gg * gg)
        mo_ref[...] = mn
        vo_ref[...] = vn
        d_ref[...] = -ADAM_LR * ((mn * c1) / (jnp.sqrt(vn * c2) + ADAM_EPS) + ADAM_WD * w_ref[...])

    spec = pl.BlockSpec((tr, c), lambda i: (i, 0))
    o = jax.ShapeDtypeStruct((r, c), F32)
    return _call(body, name, (o, o, o), (r // tr,), [spec] * 4, (spec, spec, spec))(w, g, mom, vel)


def _place():
    x, y, c = lax.axis_index("x"), lax.axis_index("y"), lax.axis_index("c")
    others = [(1 - x, y), (x, 1 - y), (1 - x, 1 - y)]
    return x, y, c, others


_ANY = pl.BlockSpec(memory_space=pl.ANY)


def _gather_weights(shards):
    nw = len(shards)

    def body(*refs):
        s_refs, g_refs = refs[:nw], refs[nw:2 * nw]
        send, recv, loc = refs[2 * nw:]
        x, y, c, others = _place()
        chip = 2 * x + y
        me, sib = (x, y, c), (x, y, 1 - c)

        def rcopy(w, k, src, dst, to):
            return pltpu.make_async_remote_copy(src_ref=src, dst_ref=dst, send_sem=send.at[w, k], recv_sem=recv.at[w, k],
                                                device_id=to, device_id_type=MESH)

        locals_, firsts, passed = [], [], []
        for w in range(nw):
            cp = pltpu.make_async_copy(s_refs[w], g_refs[w].at[chip], loc.at[w])
            cp.start()
            locals_.append(cp)
            for k, (px, py) in enumerate(others):
                cp = rcopy(w, k, s_refs[w].at[c], g_refs[w].at[chip, c], (px, py, c))
                cp.start()
                firsts.append(cp)
        for w in range(nw):
            for k, (px, py) in enumerate(others):
                blk = g_refs[w].at[2 * px + py, c]
                rcopy(w, k, blk, blk, me).wait_recv()
                cp = rcopy(w, 3 + k, blk, blk, sib)
                cp.start()
                passed.append(cp)
        for w in range(nw):
            for k, (px, py) in enumerate(others):
                blk = g_refs[w].at[2 * px + py, 1 - c]
                rcopy(w, 3 + k, blk, blk, me).wait_recv()
        for cp in firsts + passed:
            cp.wait_send()
        for cp in locals_:
            cp.wait()

    outs = tuple(jax.ShapeDtypeStruct((NCHIP,) + s.shape, s.dtype) for s in shards)
    return pl.pallas_call(
        body, name="gather_weights", out_shape=outs, in_specs=[_ANY] * nw, out_specs=tuple([_ANY] * nw),
        scratch_shapes=[pltpu.SemaphoreType.DMA((nw, 6)), pltpu.SemaphoreType.DMA((nw, 6)), pltpu.SemaphoreType.DMA((nw,))],
    )(*shards)


def _sibling_swap_halves(grads):
    ng = len(grads)

    def body(*refs):
        d_refs, a_refs = refs[:ng], refs[ng:2 * ng]
        send, recv = refs[2 * ng:]
        x, y, c, _ = _place()
        cps = []
        for w in range(ng):
            h = d_refs[w].shape[1] // 2
            cp = pltpu.make_async_remote_copy(src_ref=d_refs[w].at[:, pl.ds((1 - c) * h, h), :], dst_ref=a_refs[w],
                                              send_sem=send.at[w], recv_sem=recv.at[w], device_id=(x, y, 1 - c), device_id_type=MESH)
            cp.start()
            cps.append(cp)
        for cp in cps:
            cp.wait()

    outs = tuple(jax.ShapeDtypeStruct((NCHIP, g.shape[1] // 2, g.shape[2]), g.dtype) for g in grads)
    return pl.pallas_call(
        body, name="grad_swap_halves", out_shape=outs, in_specs=[_ANY] * ng, out_specs=tuple([_ANY] * ng),
        scratch_shapes=[pltpu.SemaphoreType.DMA((ng,)), pltpu.SemaphoreType.DMA((ng,))])(*grads)


def _chip_exchange(parts):
    ng = len(parts)

    def body(*refs):
        q_refs, b_refs = refs[:ng], refs[ng:2 * ng]
        send, recv = refs[2 * ng:]
        x, y, c, others = _place()
        cps = []
        for w in range(ng):
            for k, (px, py) in enumerate(others):
                cp = pltpu.make_async_remote_copy(src_ref=q_refs[w].at[2 * px + py], dst_ref=b_refs[w].at[k],
                                                  send_sem=send.at[w, k], recv_sem=recv.at[w, k],
                                                  device_id=(px, py, c), device_id_type=MESH)
                cp.start()
                cps.append(cp)
        for cp in cps:
            cp.wait()

    outs = tuple(jax.ShapeDtypeStruct((3,) + p.shape[1:], p.dtype) for p in parts)
    return pl.pallas_call(
        body, name="grad_chip_exchange", out_shape=outs, in_specs=[_ANY] * ng, out_specs=tuple([_ANY] * ng),
        scratch_shapes=[pltpu.SemaphoreType.DMA((ng, 3)), pltpu.SemaphoreType.DMA((ng, 3))])(*parts)


def _sibling_join(halves):
    nw = len(halves)

    def body(*refs):
        h_refs = refs[:2 * nw]
        f_refs = refs[2 * nw:3 * nw]
        send, recv, loc = refs[3 * nw:]
        x, y, c, _ = _place()
        cps, lcs = [], []
        for w in range(nw):
            for l in range(2):
                src = h_refs[2 * w + l]
                h = src.shape[0]
                dst = f_refs[w].at[l, pl.ds(c * h, h), :]
                lc = pltpu.make_async_copy(src, dst, loc.at[w, l])
                lc.start()
                lcs.append(lc)
                cp = pltpu.make_async_remote_copy(src_ref=src, dst_ref=dst, send_sem=send.at[w, l], recv_sem=recv.at[w, l],
                                                  device_id=(x, y, 1 - c), device_id_type=MESH)
                cp.start()
                cps.append(cp)
        for w in range(nw):
            for l in range(2):
                src = h_refs[2 * w + l]
                h = src.shape[0]
                other = f_refs[w].at[l, pl.ds((1 - c) * h, h), :]
                pltpu.make_async_remote_copy(src_ref=src, dst_ref=other, send_sem=send.at[w, l], recv_sem=recv.at[w, l],
                                             device_id=(x, y, c), device_id_type=MESH).wait_recv()
        for cp in cps:
            cp.wait_send()
        for lc in lcs:
            lc.wait()

    flat = [a for pair in halves for a in pair]
    outs = tuple(jax.ShapeDtypeStruct((2, 2 * pair[0].shape[0], pair[0].shape[1]), F32) for pair in halves)
    return pl.pallas_call(
        body, name="grad_sibling_join", out_shape=outs, in_specs=[_ANY] * (2 * nw), out_specs=tuple([_ANY] * nw),
        scratch_shapes=[pltpu.SemaphoreType.DMA((nw, 2)), pltpu.SemaphoreType.DMA((nw, 2)), pltpu.SemaphoreType.DMA((nw, 2))])(*flat)


def _allgather_small(v):
    r = v.shape[0]

    def body(x_ref, out_ref, send_sems, recv_sems, local_sem):
        x, y, c, chips = _place()
        me, sibling = (x, y, c), (x, y, 1 - c)

        def slab(px, py, pc):
            return out_ref.at[4 * px + 2 * py + pc]

        def copy(k, block, to, src=None):
            return pltpu.make_async_remote_copy(src_ref=slab(*block) if src is None else src, dst_ref=slab(*block),
                                                send_sem=send_sems.at[k], recv_sem=recv_sems.at[k],
                                                device_id=to, device_id_type=MESH)

        mine = pltpu.make_async_copy(x_ref, slab(*me), local_sem)
        mine.start()
        first = [copy(0, me, sibling, src=x_ref)]
        first += [copy(1 + j, me, (*chip, c), src=x_ref) for j, chip in enumerate(chips)]
        for cp in first:
            cp.start()
        passed = [copy(4 + j, (*chip, c), sibling) for j, chip in enumerate(chips)]
        for j, chip in enumerate(chips):
            copy(1 + j, (*chip, c), me).wait_recv()
            passed[j].start()
        copy(0, sibling, me).wait_recv()
        for j, chip in enumerate(chips):
            copy(4 + j, (*chip, 1 - c), me).wait_recv()
        for cp in first + passed:
            cp.wait_send()
        mine.wait()

    return pl.pallas_call(
        body, name="allgather_small", out_shape=jax.ShapeDtypeStruct((8, r, 128), F32),
        in_specs=[pl.BlockSpec(memory_space=pltpu.VMEM)], out_specs=pl.BlockSpec(memory_space=pltpu.VMEM),
        scratch_shapes=[pltpu.SemaphoreType.DMA((7,)), pltpu.SemaphoreType.DMA((7,)), pltpu.SemaphoreType.DMA],
    )(v)


def _add_half(name, d, a, c):
    _, h, cols = a.shape

    def body(c_ref, d_ref, a_ref, o_ref):
        o_ref[...] = d_ref[...] + a_ref[...]

    return _call(body, name, jax.ShapeDtypeStruct(a.shape, F32), (NCHIP,),
                 [pl.BlockSpec((None, h, cols), lambda p, cr: (p, cr[0], 0)), pl.BlockSpec((None, h, cols), lambda p, cr: (p, 0, 0))],
                 pl.BlockSpec((None, h, cols), lambda p, cr: (p, 0, 0)), prefetch=1)(c, d, a)


def _add_chips(name, q, b, chip):
    _, h, cols = q.shape
    th = h // 4 if (h % 32 == 0) else h

    def body(c_ref, q_ref, b_ref, o_ref):
        o_ref[...] = ((q_ref[...] + b_ref[0]) + b_ref[1]) + b_ref[2]

    return _call(body, name, jax.ShapeDtypeStruct((h, cols), F32), (h // th,),
                 [pl.BlockSpec((None, th, cols), lambda i, cr: (cr[0], i, 0)), pl.BlockSpec((3, th, cols), lambda i, cr: (0, i, 0))],
                 pl.BlockSpec((th, cols), lambda i, cr: (i, 0)), prefetch=1)(chip, q, b)


def _pack(arrs):
    pieces = []
    for a in arrs:
        f = a.reshape(-1)
        n = -(-f.shape[0] // 1024) * 1024
        pieces.append(jnp.pad(f, (0, n - f.shape[0])).reshape(-1, 128))
    return jnp.concatenate(pieces, axis=0)


def _unpack(buf, shapes):
    out, r = [], 0
    for s in shapes:
        n = int(np.prod(s))
        rows = -(-n // 1024) * 8
        out.append(buf[r:r + rows].reshape(-1)[:n].reshape(s))
        r += rows
    return out


def kernel(x, meta_tokens, in_ln_g, in_ln_b, rel_bias, w_in, b_in, attn_sinks, w_attn_proj, conv_dw, conv_dw_b, conv_ln_g, conv_ln_b, w_conv_proj, w_out, ln1_g, ln1_b, ffn_w_up, ffn_dw, ffn_dw_b, ffn_w_down, ln2_g, ln2_b, loss_target, m_meta_tokens, m_in_ln_g, m_in_ln_b, m_rel_bias, m_w_in, m_b_in, m_attn_sinks, m_w_attn_proj, m_conv_dw, m_conv_dw_b, m_conv_ln_g, m_conv_ln_b, m_w_conv_proj, m_w_out, m_ln1_g, m_ln1_b, m_ffn_w_up, m_ffn_dw, m_ffn_dw_b, m_ffn_w_down, m_ln2_g, m_ln2_b, v_meta_tokens, v_in_ln_g, v_in_ln_b, v_rel_bias, v_w_in, v_b_in, v_attn_sinks, v_w_attn_proj, v_conv_dw, v_conv_dw_b, v_conv_ln_g, v_conv_ln_b, v_w_conv_proj, v_w_out, v_ln1_g, v_ln1_b, v_ffn_w_up, v_ffn_dw, v_ffn_dw_b, v_ffn_w_down, v_ln2_g, v_ln2_b):
    nex, seq, _ = x.shape
    nblk = seq // BLK + 1
    tp = nblk * BLK
    m = nex * tp
    tm = _row_tile(m)
    ffs = ffn_w_up.shape[2]
    dff = 2 * ffs
    cx, cy, cc = lax.axis_index("x"), lax.axis_index("y"), lax.axis_index("c")
    chip = (2 * cx + cy).astype(jnp.int32)
    core = cc.astype(jnp.int32)

    big = [w_in, w_attn_proj, w_conv_proj, w_out, ffn_w_up, ffn_w_down]
    shards = [w.astype(BF16) for w in big]
    shards += [meta_tokens.reshape(2, N_META // 2, -1), conv_dw, ffn_dw]
    g_in, g_ap, g_cp, g_out, g_up, g_down, g_meta, g_cdw, g_fdw = _gather_weights(shards)
    meta_full = jnp.transpose(g_meta, (1, 2, 0, 3)).reshape(N_META, D)
    bias_tab = _bias_build(rel_bias)

    def layer_weights(l):
        win_old = jnp.transpose(g_in[:, l], (1, 0, 2)).reshape(D, IN_COLS)
        return dict(
            win=_to_new(win_old), bin=_to_new(b_in[l]).reshape(1, IN_COLS),
            wap=g_ap[:, l], wcp=g_cp[:, l], wout=g_out[:, l].reshape(D, D), wup=g_up[:, l],
            wdown=g_down[:, l].reshape(dff, D),
            cdw=jnp.transpose(g_cdw[:, l], (1, 0, 2)).reshape(CTAPS, CW),
            fdw=jnp.transpose(g_fdw[:, l], (1, 0, 2)).reshape(FTAPS, 2 * dff),
            fdwb=ffn_dw_b[l].reshape(1, 2 * dff))

    lw = [layer_weights(l) for l in range(DEPTH)]

    raw, h, hb = _embed_ln(x, meta_full, in_ln_g, in_ln_b, nblk)
    saved = []
    for l in range(DEPTH):
        p = lw[l]
        z = _mm_bias("in_proj", hb, p["win"], p["bin"], IN_COLS // 3, tm)
        a = _attn_fwd(z, bias_tab, attn_sinks[l], nex, nblk)
        ccv, cs = _conv_fwd(z, p["cdw"], conv_dw_b[l], conv_ln_g[l], conv_ln_b[l], nex, tp)
        ya, yc, mixed = _mix_fwd(a, cs, p["wap"], p["wcp"], z, tm)
        r1, h1, h1b = _mm_res_ln("out_proj_ln", mixed, p["wout"], h, ln1_g[l], ln1_b[l], tm)
        up3 = _ffn_up(h1b, p["wup"], tm, nex, tp)
        act = _ffn_act(up3, p["fdw"], p["fdwb"], tm)
        r2, h2, h2b = _mm_res_ln("ffn_down_ln", act, p["wdown"], h1, ln2_g[l], ln2_b[l], tm)
        saved.append(dict(hb=hb, z=z, a=a, cc=ccv, cs=cs, ya=ya, yc=yc, mixed=mixed, r1=r1, h1b=h1b, up3=up3, act=act, r2=r2))
        h, hb = h2, h2b

    dy, sq = _loss_grad(h, loss_target, nblk)
    loss = lax.psum(0.5 / D * jnp.sum(sq), ("x", "y", "c"))

    big_grads = [None] * DEPTH
    small = {}
    prev_a, prev_w, prev_cols = (), None, None
    dprev = dy
    for l in reversed(range(DEPTH)):
        p, s = lw[l], saved[l]
        dr2, dr2b, dg2, db2 = _ln_bwd_call("ln2_bwd", dprev, s["r2"], ln2_g[l], tm, prev_a, prev_w, prev_cols)
        dup3 = _ffn_act_bwd(dr2b, p["wdown"], s["up3"], p["fdw"], p["fdwb"], tm)
        dpre3, dfdw, dfdwb = _ffn_conv_bwd(dup3, s["up3"], p["fdw"], tm, nex, tp)
        gw_down = _mm_tn("dw_down", s["act"], dr2b, ffs, D // 2)
        gw_up = _mm_tn("dw_up", s["h1b"], dpre3, D, ffs, b_cols=lambda j: (j // 2, j % 2), chip_out=True)
        dr1, dr1b, dg1, db1 = _ln_bwd_call("ln1_bwd", dr2, s["r1"], ln1_g[l], tm // 2, (dpre3,), p["wup"])
        dya, dyc, dzg, csg = _gate_bwd(dr1b, p["wout"], s["ya"], s["yc"], s["z"], tm)
        gw_out = _mm_tn("dw_out", s["mixed"], dr1b, D, D // 2)
        gw_ap = _mm_tn("dw_attn_proj", s["a"], dya, AW, D // NCHIP, chip_out=True)
        gw_cp = _mm_tn("dw_conv_proj", s["cs"], dyc, CW, D // NCHIP, chip_out=True)
        dcc, dclg, dclb, dcwb = _conv_bwd_a(dyc, p["wcp"], s["cc"], conv_ln_g[l], conv_ln_b[l], tm)
        dzc, dcdw, csc = _conv_bwd_b(dcc, s["z"], p["cdw"], nex, tp)
        dzq, csq, dsk, dbias = _attn_bwd(s["z"], bias_tab, attn_sinks[l], dya, p["wap"], nex, nblk)
        gin = [_mm_tn("dw_in_gates", s["hb"], dzg, D, D // 2), _mm_tn("dw_in_conv", s["hb"], dzc, D, CW),
               _mm_tn("dw_in_qkv", s["hb"], dzq, D, 2 * KVW)]
        gin_old = _to_old(jnp.concatenate(gin, axis=1))
        gw_in = jnp.transpose(gin_old.reshape(D, NCHIP, IN_COLS // NCHIP), (1, 0, 2))
        big_grads[l] = [gw_in, gw_ap, gw_cp, gw_out.reshape(NCHIP, D // NCHIP, D), gw_up,
                        gw_down.reshape(NCHIP, dff // NCHIP, D)]
        small[l] = dict(
            b_in=_to_old(jnp.concatenate([csg, csc, csq], axis=1)).reshape(IN_COLS), attn_sinks=dsk[0, :NQ],
            conv_dw=dcdw[:CTAPS], conv_dw_b=dcwb.reshape(CW), conv_ln_g=dclg.reshape(CW), conv_ln_b=dclb.reshape(CW),
            ln1_g=dg1.reshape(D), ln1_b=db1.reshape(D),
            ffn_dw=jnp.transpose(dfdw, (1, 0, 2)).reshape(FTAPS, 2 * dff), ffn_dw_b=jnp.transpose(dfdwb, (1, 0, 2)).reshape(2 * dff),
            ln2_g=dg2.reshape(D), ln2_b=db2.reshape(D), bias=_bias_grad(dbias))
        dprev = dr1
        prev_a, prev_w, prev_cols = (dzg, dzc, dzq), p["win"], [(C_GATES, C_CONV), (C_CONV, C_QKV), (C_QKV, IN_COLS)]
    draw, _, dg0, db0 = _ln_bwd_call("in_ln_bwd", dprev, raw, in_ln_g, tm, prev_a, prev_w, prev_cols)
    draw3 = draw.reshape(nex, tp, D)
    grad_x = draw3[:, BLK:, :]
    dmeta = _sum0("meta_grad_sum", draw3[:, PAD:BLK, :])

    names_l = ["b_in", "attn_sinks", "conv_dw", "conv_dw_b", "conv_ln_g", "conv_ln_b", "ln1_g", "ln1_b", "ffn_dw", "ffn_dw_b", "ln2_g", "ln2_b"]
    part_list = [dmeta, dg0.reshape(D), db0.reshape(D), small[0]["bias"] + small[1]["bias"]]
    part_list += [jnp.stack([small[0][n], small[1][n]]) for n in names_l]
    shapes_small = [tuple(a.shape) for a in part_list]
    tot = _sum0("small_grad_sum", _allgather_small(_pack(part_list)))
    (g_meta_f, g_inlg, g_inlb, g_biasp, g_bin, g_sinks, g_cdw_f, g_cdwb, g_clg, g_clb, g_l1g, g_l1b, g_fdw_f, g_fdwb,
     g_l2g, g_l2b) = _unpack(tot, shapes_small)
    g_relb = g_biasp
    csh = D // NCHIP
    g_meta_s = lax.dynamic_slice_in_dim(g_meta_f, chip * csh, csh, axis=1)
    g_cdw_s = lax.dynamic_slice_in_dim(g_cdw_f, chip * (CW // NCHIP), CW // NCHIP, axis=2)
    g_fdw_s = lax.dynamic_slice_in_dim(g_fdw_f, chip * ffs, ffs, axis=2)

    flat = [big_grads[l][w] for w in range(6) for l in range(DEPTH)]
    sib = _sibling_swap_halves(flat)
    cvec, chipvec = core.reshape(1), chip.reshape(1)
    q = [_add_half("grad_add_sibling", d, a, cvec) for d, a in zip(flat, sib)]
    got = _chip_exchange(q)
    red = [_add_chips("grad_add_chips", qq, bb, chipvec) for qq, bb in zip(q, got)]
    full = _sibling_join([[red[2 * w], red[2 * w + 1]] for w in range(6)])

    moms = [m_w_in, m_w_attn_proj, m_w_conv_proj, m_w_out, m_ffn_w_up, m_ffn_w_down]
    vels = [v_w_in, v_w_attn_proj, v_w_conv_proj, v_w_out, v_ffn_w_up, v_ffn_w_down]
    big_out = []
    for w, g, mo, ve in zip(big, full, moms, vels):
        sh = w.shape
        two = lambda t: t.reshape(sh[0] * sh[1], sh[2])
        d_, m_, v_ = _adamw("adamw_matrix", two(w), two(g), two(mo), two(ve))
        big_out.append((g.reshape(sh), d_.reshape(sh), m_.reshape(sh), v_.reshape(sh)))

    sm_w = [meta_tokens, in_ln_g, in_ln_b, rel_bias, b_in, attn_sinks, conv_dw, conv_dw_b, conv_ln_g, conv_ln_b, ln1_g, ln1_b,
            ffn_dw, ffn_dw_b, ln2_g, ln2_b]
    sm_m = [m_meta_tokens, m_in_ln_g, m_in_ln_b, m_rel_bias, m_b_in, m_attn_sinks, m_conv_dw, m_conv_dw_b, m_conv_ln_g, m_conv_ln_b,
            m_ln1_g, m_ln1_b, m_ffn_dw, m_ffn_dw_b, m_ln2_g, m_ln2_b]
    sm_v = [v_meta_tokens, v_in_ln_g, v_in_ln_b, v_rel_bias, v_b_in, v_attn_sinks, v_conv_dw, v_conv_dw_b, v_conv_ln_g, v_conv_ln_b,
            v_ln1_g, v_ln1_b, v_ffn_dw, v_ffn_dw_b, v_ln2_g, v_ln2_b]
    sm_g = [g_meta_s, g_inlg, g_inlb, g_relb, g_bin, g_sinks, g_cdw_s, g_cdwb, g_clg, g_clb, g_l1g, g_l1b, g_fdw_s, g_fdwb, g_l2g, g_l2b]
    sm_shapes = [tuple(a.shape) for a in sm_w]
    sd, smn, svn = _adamw("adamw_small", _pack(sm_w), _pack(sm_g), _pack(sm_m), _pack(sm_v))
    sd, smn, svn = _unpack(sd, sm_shapes), _unpack(smn, sm_shapes), _unpack(svn, sm_shapes)

    order = ["meta_tokens", "in_ln_g", "in_ln_b", "rel_bias", "w_in", "b_in", "attn_sinks", "w_attn_proj", "conv_dw", "conv_dw_b",
             "conv_ln_g", "conv_ln_b", "w_conv_proj", "w_out", "ln1_g", "ln1_b", "ffn_w_up", "ffn_dw", "ffn_dw_b", "ffn_w_down",
             "ln2_g", "ln2_b"]
    small_names = ["meta_tokens", "in_ln_g", "in_ln_b", "rel_bias", "b_in", "attn_sinks", "conv_dw", "conv_dw_b", "conv_ln_g",
                   "conv_ln_b", "ln1_g", "ln1_b", "ffn_dw", "ffn_dw_b", "ln2_g", "ln2_b"]
    big_names = ["w_in", "w_attn_proj", "w_conv_proj", "w_out", "ffn_w_up", "ffn_w_down"]
    res = {}
    for i, n in enumerate(small_names):
        res[n] = (sm_g[i], sd[i], smn[i], svn[i])
    for i, n in enumerate(big_names):
        res[n] = big_out[i]
    outs = [loss, grad_x]
    for k in range(4):
        outs += [res[n][k] for n in order]
    return tuple(outs)
```

```python
import functools
import math

import numpy as np
import jax
import jax.numpy as jnp
from jax import lax
from jax.experimental import pallas as pl
from jax.experimental.pallas import tpu as pltpu

F32 = jnp.float32
BF16 = jnp.bfloat16
MESH = pl.DeviceIdType.MESH

D = 1024
N_META = 16
BLK = 128
PAD = BLK - N_META
HD = 64
NQ = 8
NKV = 2
GRP = NQ // NKV
AW = NQ * HD
KVW = NKV * HD
CW = D // 2
CTAPS = 31
FTAPS = 3
NBUCKET = 32
MAXDIST = 128
EPS = 1e-5
DEPTH = 2
ALPHA = (2.0 * DEPTH) ** 0.25
NCHIP = 4
NKEY = 3 * BLK
NEG = -1e30
CHALO = 32
FHALO = 8
IN_COLS = AW + 2 * KVW + 2 * CW + 2 * D
_OLD = dict(q=(0, AW), k=(AW, AW + KVW), v=(AW + KVW, AW + 2 * KVW), cv=(AW + 2 * KVW, AW + 2 * KVW + CW),
            cg=(AW + 2 * KVW + CW, AW + 2 * KVW + 2 * CW), ga=(AW + 2 * KVW + 2 * CW, AW + 2 * KVW + 2 * CW + D),
            gc=(AW + 2 * KVW + 2 * CW + D, IN_COLS))
_NEW_ORDER = ("ga", "gc", "cv", "cg", "q", "k", "v")
C_GATES, C_CONV, C_QKV = 0, 2 * D, 2 * D + 2 * CW

ADAM_LR, ADAM_B1, ADAM_B2, ADAM_EPS, ADAM_WD, ADAM_STEP = 0.001, 0.9, 0.999, 1e-08, 0.01, 10


def _to_new(a):
    return jnp.concatenate([a[..., _OLD[n][0]:_OLD[n][1]] for n in _NEW_ORDER], axis=-1)


def _to_old(a):
    offs, o = {}, 0
    for n in _NEW_ORDER:
        w = _OLD[n][1] - _OLD[n][0]
        offs[n] = (o, o + w)
        o += w
    return jnp.concatenate([a[..., offs[n][0]:offs[n][1]] for n in ("q", "k", "v", "cv", "cg", "ga", "gc")], axis=-1)


def _call(body, name, out_shape, grid, in_specs, out_specs, scratch=(), prefetch=0):
    if prefetch:
        gs = pltpu.PrefetchScalarGridSpec(num_scalar_prefetch=prefetch, grid=grid, in_specs=in_specs,
                                          out_specs=out_specs, scratch_shapes=list(scratch))
        return pl.pallas_call(body, name=name, out_shape=out_shape, grid_spec=gs,
                              compiler_params=pltpu.CompilerParams(dimension_semantics=("arbitrary",) * len(grid)))
    return pl.pallas_call(body, name=name, out_shape=out_shape, grid=grid, in_specs=in_specs, out_specs=out_specs,
                          scratch_shapes=list(scratch),
                          compiler_params=pltpu.CompilerParams(dimension_semantics=("arbitrary",) * len(grid)))


def _row_tile(m):
    best = 32
    for t in range(32, 641, 32):
        if m % t == 0:
            best = t
    return best


def _pad_rows(rows, nex, tp):
    m = rows < PAD
    for b in range(1, nex):
        m = m | ((rows >= b * tp) & (rows < b * tp + PAD))
    return m


def _ln_stats(x):
    mu = jnp.mean(x, axis=-1, keepdims=True)
    xc = x - mu
    var = jnp.mean(xc * xc, axis=-1, keepdims=True)
    rstd = lax.rsqrt(var + EPS)
    return xc * rstd, rstd


def _ln_bwd(dy, xhat, rstd, g):
    dxh = dy * g
    m1 = jnp.mean(dxh, axis=-1, keepdims=True)
    m2 = jnp.mean(dxh * xhat, axis=-1, keepdims=True)
    return rstd * (dxh - m1 - xhat * m2)


def _dot(a, b):
    return jnp.dot(a, b, preferred_element_type=F32)


def _dot_nt(a, b):
    return lax.dot_general(a, b, (((1,), (1,)), ((), ())), preferred_element_type=F32)


def _dot_tn(a, b):
    return lax.dot_general(a, b, (((0,), (0,)), ((), ())), preferred_element_type=F32)


def _sigmoid(x):
    return 1.0 / (1.0 + jnp.exp(-x))


def _gelu_parts(g):
    cdf = 0.5 * (1.0 + lax.erf(g * (1.0 / math.sqrt(2.0))))
    pdf = jnp.exp(-0.5 * g * g) * (1.0 / math.sqrt(2.0 * math.pi))
    return g * cdf, cdf + g * pdf


def _bucket_np(d):
    n = np.maximum(d, 0)
    max_exact = NBUCKET // 2
    nf = np.maximum(n, 1).astype(np.float32)
    large = max_exact + (np.log(nf / np.float32(max_exact)) / np.float32(math.log(MAXDIST / max_exact))
                         * np.float32(NBUCKET - max_exact)).astype(np.int32)
    large = np.minimum(large, NBUCKET - 1)
    return np.where(n < max_exact, n, large).astype(np.int32)


def _bias_index():
    i = np.arange(BLK)[:, None]
    j = np.arange(2 * BLK)[None, :]
    d = BLK + i - j
    band_ok = (d >= 0) & (d < BLK)
    band = _bucket_np(d)
    idx = np.full((3, BLK, NKEY), -1, np.int32)
    m = np.arange(N_META)[None, :]
    d0 = (i - PAD) - m
    idx[0, :, 2 * BLK:2 * BLK + N_META] = np.where(d0 >= 0, _bucket_np(d0), -1)
    ok1 = band_ok & (j >= BLK)
    idx[1, :, :2 * BLK] = np.where(ok1, band, -1)
    idx[1, :, 2 * BLK:2 * BLK + N_META] = _bucket_np((N_META + i) - m)
    idx[2, :, :2 * BLK] = np.where(band_ok, band, -1)
    idx[2, :, 2 * BLK:2 * BLK + N_META] = NBUCKET - 1
    return idx


def _bias_build(rel_bias):
    idx = jnp.asarray(_bias_index())

    def body(idx_ref, rb_ref, o_ref):
        ix = idx_ref[...]
        for h in range(NQ):
            acc = jnp.full(ix.shape, NEG, F32)
            for b in range(NBUCKET):
                acc = jnp.where(ix == b, rb_ref[b, h], acc)
            o_ref[:, h, :, :] = acc

    return pl.pallas_call(
        body, name="bias_build", out_shape=jax.ShapeDtypeStruct((3, NQ, BLK, NKEY), F32),
        in_specs=[pl.BlockSpec(memory_space=pltpu.VMEM), pl.BlockSpec(memory_space=pltpu.SMEM)],
        out_specs=pl.BlockSpec(memory_space=pltpu.VMEM))(idx, rel_bias)


def _bias_grad(dbias):
    idx = jnp.asarray(_bias_index())

    def body(idx_ref, d_ref, o_ref):
        d = jnp.sum(d_ref[...], axis=0)
        for b in range(NBUCKET):
            acc = jnp.zeros((NQ, NKEY), F32)
            for case in range(3):
                hit = (idx_ref[case] == b)[None, :, :]
                acc = acc + jnp.sum(jnp.where(hit, d[case], 0.0), axis=1)
            o_ref[b] = jnp.sum(acc, axis=-1, keepdims=True)

    out = pl.pallas_call(
        body, name="bias_grad", out_shape=jax.ShapeDtypeStruct((NBUCKET, NQ, 1), F32),
        in_specs=[pl.BlockSpec(memory_space=pltpu.VMEM), pl.BlockSpec(memory_space=pltpu.VMEM)],
        out_specs=pl.BlockSpec(memory_space=pltpu.VMEM))(idx, dbias)
    return out.reshape(NBUCKET, NQ)


def _embed_ln(x, meta, g, b, nblk):
    nex, seq, _ = x.shape
    m = nex * nblk * BLK

    def body(x_ref, meta_ref, g_ref, b_ref, raw_ref, h_ref, hb_ref):
        j = pl.program_id(1)

        @pl.when(j == 0)
        def _():
            raw_ref[0:PAD, :] = jnp.zeros((PAD, D), F32)
            raw_ref[PAD:BLK, :] = meta_ref[...]

        @pl.when(j > 0)
        def _():
            raw_ref[...] = x_ref[...]

        xhat, _ = _ln_stats(raw_ref[...])
        y = xhat * g_ref[...] + b_ref[...]
        h_ref[...] = y
        hb_ref[...] = y.astype(BF16)

    row = lambda bb, j: (bb * nblk + j, 0)
    return _call(
        body, "embed_ln",
        (jax.ShapeDtypeStruct((m, D), F32), jax.ShapeDtypeStruct((m, D), F32), jax.ShapeDtypeStruct((m, D), BF16)),
        (nex, nblk),
        [pl.BlockSpec((None, BLK, D), lambda bb, j: (bb, jnp.maximum(j - 1, 0), 0)),
         pl.BlockSpec((N_META, D), lambda bb, j: (0, 0)),
         pl.BlockSpec((1, D), lambda bb, j: (0, 0)), pl.BlockSpec((1, D), lambda bb, j: (0, 0))],
        (pl.BlockSpec((BLK, D), row), pl.BlockSpec((BLK, D), row), pl.BlockSpec((BLK, D), row)),
    )(x, meta, g.reshape(1, D), b.reshape(1, D))


def _mm_bias(name, a, w, bias, tn, tm):
    m, k = a.shape
    n = w.shape[1]

    def body(a_ref, w_ref, b_ref, o_ref):
        o_ref[...] = _dot(a_ref[...], w_ref[...]) + b_ref[...]

    return _call(body, name, jax.ShapeDtypeStruct((m, n), F32), (n // tn, m // tm),
                 [pl.BlockSpec((tm, k), lambda j, i: (i, 0)), pl.BlockSpec((k, tn), lambda j, i: (0, j)),
                  pl.BlockSpec((1, tn), lambda j, i: (0, j))],
                 pl.BlockSpec((tm, tn), lambda j, i: (i, j)))(a, w, bias)


def _ffn_up(a, w4, tm, nex, tp):
    m, k = a.shape
    ffs = w4.shape[2]

    def body(a_ref, w_ref, o_ref):
        i = pl.program_id(1)
        rows = i * tm + lax.broadcasted_iota(jnp.int32, (tm, 1), 0)
        acc = _dot(a_ref[...], w_ref[...])
        o_ref[...] = jnp.where(_pad_rows(rows, nex, tp), 0.0, acc)

    return _call(body, "ffn_up", jax.ShapeDtypeStruct((2, m, 2 * ffs), F32), (NCHIP, m // tm),
                 [pl.BlockSpec((tm, k), lambda j, i: (i, 0)), pl.BlockSpec((None, k, ffs), lambda j, i: (j, 0, 0))],
                 pl.BlockSpec((None, tm, ffs), lambda j, i: (j // 2, i, j % 2)))(a, w4)


def _fill_kv(ks, vs, prev_ref, cur_ref, meta_ref):
    for piece, lo, n in ((prev_ref, 0, BLK), (cur_ref, BLK, BLK), (meta_ref, 2 * BLK, N_META)):
        val = piece[...]
        for hk in range(NKV):
            ks[hk, lo:lo + n, :] = val[:, hk * HD:(hk + 1) * HD].astype(BF16)
            vs[hk, lo:lo + n, :] = val[:, KVW + hk * HD:KVW + (hk + 1) * HD].astype(BF16)
    for hk in range(NKV):
        ks[hk, 2 * BLK + N_META:NKEY, :] = jnp.zeros((BLK - N_META, HD), BF16)
        vs[hk, 2 * BLK + N_META:NKEY, :] = jnp.zeros((BLK - N_META, HD), BF16)


def _softmax_group(q, ks_hk, bias_ref, sink_ref, hk):
    qg = jnp.concatenate([q[:, (hk * GRP + g) * HD:(hk * GRP + g + 1) * HD] for g in range(GRP)], axis=0).astype(BF16)
    s = _dot_nt(qg, ks_hk) * (HD ** -0.5) + bias_ref[hk * GRP:(hk + 1) * GRP].reshape(GRP * BLK, NKEY)
    sink = jnp.concatenate([jnp.full((BLK, 1), sink_ref[0, hk * GRP + g], F32) for g in range(GRP)], axis=0)
    mx = jnp.maximum(jnp.max(s, axis=-1, keepdims=True), sink)
    p = jnp.exp(s - mx)
    es = jnp.exp(sink - mx)
    inv = 1.0 / (jnp.sum(p, axis=-1, keepdims=True) + es)
    return qg, p * inv, es * inv


def _attn_specs(nblk, tp, blk_of):
    qcol, kvcol = (C_QKV) // AW, (C_QKV + AW) // (2 * KVW)
    return [
        pl.BlockSpec((BLK, AW), lambda bb, j: (bb * nblk + blk_of(j), qcol)),
        pl.BlockSpec((BLK, 2 * KVW), lambda bb, j: (bb * nblk + blk_of(j), kvcol)),
        pl.BlockSpec((BLK, 2 * KVW), lambda bb, j: (bb * nblk + jnp.maximum(blk_of(j) - 1, 0), kvcol)),
        pl.BlockSpec((N_META, 2 * KVW), lambda bb, j: (bb * (tp // N_META) + PAD // N_META, kvcol)),
        pl.BlockSpec((None, NQ, BLK, NKEY), lambda bb, j: (jnp.minimum(blk_of(j), 2), 0, 0, 0)),
        pl.BlockSpec(memory_space=pltpu.SMEM),
    ]


def _attn_fwd(z, bias, sinks, nex, nblk):
    m = z.shape[0]
    tp = nblk * BLK

    def body(q_ref, cur_ref, prev_ref, meta_ref, bias_ref, sink_ref, o_ref, ks, vs, oacc):
        _fill_kv(ks, vs, prev_ref, cur_ref, meta_ref)
        q = q_ref[...]
        for hk in range(NKV):
            _, pn, _ = _softmax_group(q, ks[hk], bias_ref, sink_ref, hk)
            o = _dot(pn.astype(BF16), vs[hk])
            for g in range(GRP):
                h = hk * GRP + g
                oacc[:, h * HD:(h + 1) * HD] = o[g * BLK:(g + 1) * BLK, :]
        o_ref[...] = oacc[...].astype(BF16)

    return _call(body, "attn_fwd", jax.ShapeDtypeStruct((m, AW), BF16), (nex, nblk),
                 _attn_specs(nblk, tp, lambda j: j),
                 pl.BlockSpec((BLK, AW), lambda bb, j: (bb * nblk + j, 0)),
                 scratch=[pltpu.VMEM((NKV, NKEY, HD), BF16), pltpu.VMEM((NKV, NKEY, HD), BF16),
                          pltpu.VMEM((BLK, AW), F32)])(z, z, z, z, bias, sinks.reshape(1, NQ))


def _cgate(cv, cg, rows, nex, tp):
    return jnp.where(_pad_rows(rows, nex, tp), 0.0, cv * _sigmoid(cg))


def _conv_fwd(z, w, wb, g, b, nex, tp):
    m = z.shape[0]
    tm = BLK
    sub = tm // CHALO
    cvc, cgc = C_CONV // CW, C_CONV // CW + 1

    def body(cv_ref, cg_ref, cvh_ref, cgh_ref, w_ref, wb_ref, g_ref, b_ref, cc_ref, cs_ref, win):
        i = pl.program_id(0)
        rows = i * tm + lax.broadcasted_iota(jnp.int32, (tm, 1), 0)
        hrows = i * tm - CHALO + lax.broadcasted_iota(jnp.int32, (CHALO, 1), 0)
        win[0:CHALO, :] = _cgate(cvh_ref[...], cgh_ref[...], hrows, nex, tp)
        win[CHALO:CHALO + tm, :] = _cgate(cv_ref[...], cg_ref[...], rows, nex, tp)
        for sb in range(sub):
            acc = jnp.zeros((CHALO, CW), F32) + wb_ref[...]
            for k in range(CTAPS):
                acc = acc + w_ref[k:k + 1, :] * win[sb * CHALO + 2 + k:sb * CHALO + 2 + k + CHALO, :]
            cc_ref[sb * CHALO:(sb + 1) * CHALO, :] = acc
        xhat, _ = _ln_stats(cc_ref[...])
        cl = xhat * g_ref[...] + b_ref[...]
        cs_ref[...] = (cl * _sigmoid(cl)).astype(BF16)

    halo = lambda i: jnp.maximum(i * sub - 1, 0)
    vec = pl.BlockSpec((1, CW), lambda i: (0, 0))
    return _call(body, "conv_fwd", (jax.ShapeDtypeStruct((m, CW), F32), jax.ShapeDtypeStruct((m, CW), BF16)),
                 (m // tm,),
                 [pl.BlockSpec((tm, CW), lambda i: (i, cvc)), pl.BlockSpec((tm, CW), lambda i: (i, cgc)),
                  pl.BlockSpec((CHALO, CW), lambda i: (halo(i), cvc)), pl.BlockSpec((CHALO, CW), lambda i: (halo(i), cgc)),
                  pl.BlockSpec((CTAPS, CW), lambda i: (0, 0)), vec, vec, vec],
                 (pl.BlockSpec((tm, CW), lambda i: (i, 0)), pl.BlockSpec((tm, CW), lambda i: (i, 0))),
                 scratch=[pltpu.VMEM((CHALO + tm, CW), F32)])(z, z, z, z, w, wb.reshape(1, CW), g.reshape(1, CW), b.reshape(1, CW))


def _mix_fwd(a, cs, wap4, wcp4, z, tm):
    m = a.shape[0]
    ns = wap4.shape[2]

    def body(a_ref, cs_ref, wa_ref, wc_ref, ga_ref, gc_ref, ya_ref, yc_ref, mix_ref):
        av, cv = a_ref[...], cs_ref[...]
        for j in range(NCHIP):
            ya_ref[:, j * ns:(j + 1) * ns] = _dot(av, wa_ref[j])
            yc_ref[:, j * ns:(j + 1) * ns] = _dot(cv, wc_ref[j])
        mix_ref[...] = (_sigmoid(ga_ref[...]) * ya_ref[...] + _sigmoid(gc_ref[...]) * yc_ref[...]).astype(BF16)

    wspec = pl.BlockSpec((NCHIP, AW, ns), lambda i: (0, 0, 0))
    row = lambda i: (i, 0)
    return _call(body, "mix_fwd",
                 (jax.ShapeDtypeStruct((m, D), F32), jax.ShapeDtypeStruct((m, D), F32), jax.ShapeDtypeStruct((m, D), BF16)),
                 (m // tm,),
                 [pl.BlockSpec((tm, AW), row), pl.BlockSpec((tm, CW), row), wspec, wspec,
                  pl.BlockSpec((tm, D), lambda i: (i, 0)), pl.BlockSpec((tm, D), lambda i: (i, 1))],
                 (pl.BlockSpec((tm, D), row), pl.BlockSpec((tm, D), row), pl.BlockSpec((tm, D), row)))(a, cs, wap4, wcp4, z, z)


def _mm_res_ln(name, a, w, res, g, b, tm):
    m, k = a.shape

    def body(a_ref, w_ref, res_ref, g_ref, b_ref, r_ref, h_ref, hb_ref):
        r = ALPHA * res_ref[...] + _dot(a_ref[...], w_ref[...])
        r_ref[...] = r
        xhat, _ = _ln_stats(r)
        y = xhat * g_ref[...] + b_ref[...]
        h_ref[...] = y
        hb_ref[...] = y.astype(BF16)

    row = lambda i: (i, 0)
    vec = pl.BlockSpec((1, D), lambda i: (0, 0))
    return _call(body, name,
                 (jax.ShapeDtypeStruct((m, D), F32), jax.ShapeDtypeStruct((m, D), F32), jax.ShapeDtypeStruct((m, D), BF16)),
                 (m // tm,),
                 [pl.BlockSpec((tm, k), row), pl.BlockSpec((k, D), lambda i: (0, 0)), pl.BlockSpec((tm, D), row), vec, vec],
                 (pl.BlockSpec((tm, D), row), pl.BlockSpec((tm, D), row), pl.BlockSpec((tm, D), row)))(a, w, res, g.reshape(1, D), b.reshape(1, D))


def _conv3(win, w_ref, b_ref, tm):
    return (b_ref[...] + w_ref[0:1, :] * win[FHALO - 2:FHALO - 2 + tm, :] + w_ref[1:2, :] * win[FHALO - 1:FHALO - 1 + tm, :]
            + w_ref[2:3, :] * win[FHALO:FHALO + tm, :])


def _ffn_specs(tm, ffs, m):
    sub = tm // FHALO
    return [
        pl.BlockSpec((2, tm, ffs), lambda c, i: (0, i, c)),
        pl.BlockSpec((2, FHALO, ffs), lambda c, i: (0, jnp.maximum(i * sub - 1, 0), c)),
        pl.BlockSpec((FTAPS, ffs), lambda c, i: (0, c)), pl.BlockSpec((FTAPS, ffs), lambda c, i: (0, c + 2)),
        pl.BlockSpec((1, ffs), lambda c, i: (0, c)), pl.BlockSpec((1, ffs), lambda c, i: (0, c + 2)),
    ]


def _ffn_act(up3, w, wb, tm):
    _, m, dff = up3.shape
    ffs = dff // 2

    def body(x_ref, xh_ref, wu_ref, wg_ref, bu_ref, bg_ref, o_ref, win):
        win[:, 0:FHALO, :] = xh_ref[...]
        win[:, FHALO:FHALO + tm, :] = x_ref[...]
        u = _conv3(win.at[0], wu_ref, bu_ref, tm)
        g = _conv3(win.at[1], wg_ref, bg_ref, tm)
        o_ref[...] = (_gelu_parts(g)[0] * u).astype(BF16)

    return _call(body, "ffn_act", jax.ShapeDtypeStruct((m, dff), BF16), (2, m // tm),
                 _ffn_specs(tm, ffs, m), pl.BlockSpec((tm, ffs), lambda c, i: (i, c)),
                 scratch=[pltpu.VMEM((2, FHALO + tm, ffs), F32)])(up3, up3, w, w, wb, wb)


def _loss_grad(y, target, nblk):
    nex = target.shape[0]
    m = y.shape[0]

    def body(y_ref, t_ref, dy_ref, acc_ref):
        bb, j = pl.program_id(0), pl.program_id(1)

        @pl.when((bb == 0) & (j == 0))
        def _():
            acc_ref[...] = jnp.zeros_like(acc_ref)

        @pl.when(j == 0)
        def _():
            dy_ref[...] = jnp.zeros_like(dy_ref)

        @pl.when(j > 0)
        def _():
            e = y_ref[...] - t_ref[...]
            dy_ref[...] = e * (1.0 / D)
            acc_ref[...] += jnp.sum((e * e).reshape(BLK // 8, 8, D), axis=0)

    return _call(body, "loss_grad", (jax.ShapeDtypeStruct((m, D), F32), jax.ShapeDtypeStruct((8, D), F32)), (nex, nblk),
                 [pl.BlockSpec((BLK, D), lambda bb, j: (bb * nblk + j, 0)),
                  pl.BlockSpec((None, BLK, D), lambda bb, j: (bb, jnp.maximum(j - 1, 0), 0))],
                 (pl.BlockSpec((BLK, D), lambda bb, j: (bb * nblk + j, 0)), pl.BlockSpec((8, D), lambda bb, j: (0, 0))))(y, target)


def _ln_bwd_call(name, dy, r, g, tm, a_list=(), w=None, cols=None):
    m = dy.shape[0]
    na = len(a_list)

    def body(*refs):
        dy_ref, r_ref, g_ref = refs[0:3]
        a_refs = refs[3:3 + na]
        w_ref = refs[3 + na] if na else None
        dr_ref, drb_ref, dg_ref, db_ref = refs[-4:]
        i = pl.program_id(0)
        dh = dy_ref[...]
        if na:
            dh = ALPHA * dh
            if cols is None:
                ns = w.shape[2]
                for j in range(NCHIP):
                    dh = dh + _dot_nt(a_refs[0][j // 2, :, (j % 2) * ns:(j % 2 + 1) * ns], w_ref[j])
            else:
                for a_ref, (c0, c1) in zip(a_refs, cols):
                    dh = dh + _dot_nt(a_ref[...], w_ref[:, c0:c1])
        xhat, rstd = _ln_stats(r_ref[...])
        dr = _ln_bwd(dh, xhat, rstd, g_ref[...])
        dr_ref[...] = dr
        drb_ref[...] = dr.astype(BF16)

        @pl.when(i == 0)
        def _():
            dg_ref[...] = jnp.zeros_like(dg_ref)
            db_ref[...] = jnp.zeros_like(db_ref)

        dg_ref[...] += jnp.sum(dh * xhat, axis=0, keepdims=True)
        db_ref[...] += jnp.sum(dh, axis=0, keepdims=True)

    row = lambda i: (i, 0)
    vec = pl.BlockSpec((1, D), lambda i: (0, 0))
    in_specs = [pl.BlockSpec((tm, D), row), pl.BlockSpec((tm, D), row), vec]
    for a in a_list:
        in_specs.append(pl.BlockSpec((2, tm, a.shape[2]), lambda i: (0, i, 0)) if a.ndim == 3 else pl.BlockSpec((tm, a.shape[1]), row))
    if na:
        in_specs.append(pl.BlockSpec(w.shape, (lambda i: (0, 0, 0)) if w.ndim == 3 else (lambda i: (0, 0))))
    return _call(body, name,
                 (jax.ShapeDtypeStruct((m, D), F32), jax.ShapeDtypeStruct((m, D), BF16),
                  jax.ShapeDtypeStruct((1, D), F32), jax.ShapeDtypeStruct((1, D), F32)),
                 (m // tm,), in_specs,
                 (pl.BlockSpec((tm, D), row), pl.BlockSpec((tm, D), row), vec, vec),
                 )(dy, r, g.reshape(1, D), *a_list, *([w] if na else []))


def _ffn_act_bwd(drb, wdown, up3, w, wb, tm):
    _, m, dff = up3.shape
    ffs = dff // 2

    def body(dr_ref, wd_ref, x_ref, xh_ref, wu_ref, wg_ref, bu_ref, bg_ref, o_ref, win):
        win[:, 0:FHALO, :] = xh_ref[...]
        win[:, FHALO:FHALO + tm, :] = x_ref[...]
        u = _conv3(win.at[0], wu_ref, bu_ref, tm)
        g = _conv3(win.at[1], wg_ref, bg_ref, tm)
        dact = _dot_nt(dr_ref[...], wd_ref[...])
        gel, dgel = _gelu_parts(g)
        o_ref[0] = dact * gel
        o_ref[1] = dact * u * dgel

    specs = [pl.BlockSpec((tm, D), lambda c, i: (i, 0)), pl.BlockSpec((ffs, D), lambda c, i: (c, 0))] + _ffn_specs(tm, ffs, m)
    return _call(body, "ffn_act_bwd", jax.ShapeDtypeStruct((2, m, dff), F32), (2, m // tm), specs,
                 pl.BlockSpec((2, tm, ffs), lambda c, i: (0, i, c)),
                 scratch=[pltpu.VMEM((2, FHALO + tm, ffs), F32)])(drb, wdown, up3, up3, w, w, wb, wb)


def _ffn_conv_bwd(dup3, up3, w, tm, nex, tp):
    _, m, dff = up3.shape
    ffs = dff // 2
    sub = tm // FHALO
    nt = m // tm

    def body(d_ref, dh_ref, x_ref, xh_ref, wu_ref, wg_ref, o_ref, dw_ref, db_ref, dwin, xwin):
        i = pl.program_id(1)
        dwin[:, 0:tm, :] = d_ref[...]
        dwin[:, tm:tm + FHALO, :] = jnp.where(i == nt - 1, 0.0, dh_ref[...])
        xwin[:, 0:FHALO, :] = xh_ref[...]
        xwin[:, FHALO:FHALO + tm, :] = x_ref[...]
        rows = i * tm + lax.broadcasted_iota(jnp.int32, (tm, 1), 0)
        pad = _pad_rows(rows, nex, tp)

        @pl.when(i == 0)
        def _():
            dw_ref[...] = jnp.zeros_like(dw_ref)
            db_ref[...] = jnp.zeros_like(db_ref)

        for p, w_ref in ((0, wu_ref), (1, wg_ref)):
            d0 = dwin[p, 0:tm, :]
            dpre = w_ref[2:3, :] * d0 + w_ref[1:2, :] * dwin[p, 1:1 + tm, :] + w_ref[0:1, :] * dwin[p, 2:2 + tm, :]
            o_ref[p] = jnp.where(pad, 0.0, dpre).astype(BF16)
            for k in range(FTAPS):
                dw_ref[p, k:k + 1, :] += jnp.sum(d0 * xwin[p, FHALO - 2 + k:FHALO - 2 + k + tm, :], axis=0, keepdims=True)
            db_ref[p] += jnp.sum(d0, axis=0, keepdims=True)

    nxt = lambda i: jnp.minimum((i + 1) * sub, m // FHALO - 1)
    prv = lambda i: jnp.maximum(i * sub - 1, 0)
    return _call(body, "ffn_conv_bwd",
                 (jax.ShapeDtypeStruct((2, m, dff), BF16), jax.ShapeDtypeStruct((2, FTAPS, dff), F32),
                  jax.ShapeDtypeStruct((2, 1, dff), F32)),
                 (2, nt),
                 [pl.BlockSpec((2, tm, ffs), lambda c, i: (0, i, c)), pl.BlockSpec((2, FHALO, ffs), lambda c, i: (0, nxt(i), c)),
                  pl.BlockSpec((2, tm, ffs), lambda c, i: (0, i, c)), pl.BlockSpec((2, FHALO, ffs), lambda c, i: (0, prv(i), c)),
                  pl.BlockSpec((FTAPS, ffs), lambda c, i: (0, c)), pl.BlockSpec((FTAPS, ffs), lambda c, i: (0, c + 2))],
                 (pl.BlockSpec((2, tm, ffs), lambda c, i: (0, i, c)), pl.BlockSpec((2, FTAPS, ffs), lambda c, i: (0, 0, c)),
                  pl.BlockSpec((2, 1, ffs), lambda c, i: (0, 0, c))),
                 scratch=[pltpu.VMEM((2, tm + FHALO, ffs), F32), pltpu.VMEM((2, FHALO + tm, ffs), F32)])(dup3, dup3, up3, up3, w, w)


def _mm_tn(name, a, b, tk, tn, b_cols=None, chip_out=False):
    m, k = a.shape
    n = b.shape[-1] * (2 if b.ndim == 3 else 1)

    def body(a_ref, b_ref, o_ref):
        o_ref[...] = _dot_tn(a_ref[...], b_ref[...])

    if b.ndim == 3:
        bspec = pl.BlockSpec((None, m, tn), lambda kk, j: (b_cols(j)[0], 0, b_cols(j)[1]))
    else:
        bspec = pl.BlockSpec((m, tn), lambda kk, j: (0, j))
    if chip_out:
        oshape, ospec = (n // tn, k, tn), pl.BlockSpec((None, tk, tn), lambda kk, j: (j, kk, 0))
    else:
        oshape, ospec = (k, n), pl.BlockSpec((tk, tn), lambda kk, j: (kk, j))
    return _call(body, name, jax.ShapeDtypeStruct(oshape, F32), (k // tk, n // tn),
                 [pl.BlockSpec((m, tk), lambda kk, j: (0, kk)), bspec], ospec)(a, b)


def _gate_bwd(drb, wout, ya, yc, z, tm):
    m = drb.shape[0]

    def body(dr_ref, w_ref, ya_ref, yc_ref, ga_ref, gc_ref, dya_ref, dyc_ref, dz_ref, cs_ref):
        i = pl.program_id(0)
        dmix = _dot_nt(dr_ref[...], w_ref[...])
        sa, sc = _sigmoid(ga_ref[...]), _sigmoid(gc_ref[...])
        dya_ref[...] = (dmix * sa).astype(BF16)
        dyc_ref[...] = (dmix * sc).astype(BF16)
        dga = dmix * ya_ref[...] * sa * (1.0 - sa)
        dgc = dmix * yc_ref[...] * sc * (1.0 - sc)
        dz_ref[:, 0:D] = dga.astype(BF16)
        dz_ref[:, D:2 * D] = dgc.astype(BF16)

        @pl.when(i == 0)
        def _():
            cs_ref[...] = jnp.zeros_like(cs_ref)

        cs_ref[:, 0:D] += jnp.sum(dga, axis=0, keepdims=True)
        cs_ref[:, D:2 * D] += jnp.sum(dgc, axis=0, keepdims=True)

    row = lambda i: (i, 0)
    return _call(body, "gate_bwd",
                 (jax.ShapeDtypeStruct((m, D), BF16), jax.ShapeDtypeStruct((m, D), BF16),
                  jax.ShapeDtypeStruct((m, 2 * D), BF16), jax.ShapeDtypeStruct((1, 2 * D), F32)),
                 (m // tm,),
                 [pl.BlockSpec((tm, D), row), pl.BlockSpec((D, D), lambda i: (0, 0)), pl.BlockSpec((tm, D), row),
                  pl.BlockSpec((tm, D), row), pl.BlockSpec((tm, D), lambda i: (i, 0)), pl.BlockSpec((tm, D), lambda i: (i, 1))],
                 (pl.BlockSpec((tm, D), row), pl.BlockSpec((tm, D), row), pl.BlockSpec((tm, 2 * D), row),
                  pl.BlockSpec((1, 2 * D), lambda i: (0, 0))))(drb, wout, ya, yc, z, z)


def _conv_bwd_a(dyc, wcp4, cc, g, b, tm):
    m = cc.shape[0]
    ns = wcp4.shape[2]

    def body(dy_ref, w_ref, cc_ref, g_ref, b_ref, dcc_ref, dg_ref, db_ref, dwb_ref):
        i = pl.program_id(0)
        dcs = jnp.zeros((tm, CW), F32)
        for j in range(NCHIP):
            dcs = dcs + _dot_nt(dy_ref[:, j * ns:(j + 1) * ns], w_ref[j])
        xhat, rstd = _ln_stats(cc_ref[...])
        cl = xhat * g_ref[...] + b_ref[...]
        sg = _sigmoid(cl)
        dcl = dcs * sg * (1.0 + cl * (1.0 - sg))
        dcc = _ln_bwd(dcl, xhat, rstd, g_ref[...])
        dcc_ref[...] = dcc

        @pl.when(i == 0)
        def _():
            dg_ref[...] = jnp.zeros_like(dg_ref)
            db_ref[...] = jnp.zeros_like(db_ref)
            dwb_ref[...] = jnp.zeros_like(dwb_ref)

        dg_ref[...] += jnp.sum(dcl * xhat, axis=0, keepdims=True)
        db_ref[...] += jnp.sum(dcl, axis=0, keepdims=True)
        dwb_ref[...] += jnp.sum(dcc, axis=0, keepdims=True)

    row = lambda i: (i, 0)
    vec = pl.BlockSpec((1, CW), lambda i: (0, 0))
    v = jax.ShapeDtypeStruct((1, CW), F32)
    return _call(body, "conv_bwd_a", (jax.ShapeDtypeStruct((m, CW), F32), v, v, v), (m // tm,),
                 [pl.BlockSpec((tm, D), row), pl.BlockSpec((NCHIP, CW, ns), lambda i: (0, 0, 0)), pl.BlockSpec((tm, CW), row), vec, vec],
                 (pl.BlockSpec((tm, CW), row), vec, vec, vec))(dyc, wcp4, cc, g.reshape(1, CW), b.reshape(1, CW))


def _conv_bwd_b(dcc, z, w, nex, tp):
    m = dcc.shape[0]
    tm = BLK
    sub = tm // CHALO
    nt = m // tm
    cvc, cgc = C_CONV // CW, C_CONV // CW + 1

    def body(d_ref, dh_ref, cv_ref, cg_ref, cvh_ref, cgh_ref, w_ref, dz_ref, dw_ref, cs_ref, dwin, cwin):
        i = pl.program_id(0)
        rows = i * tm + lax.broadcasted_iota(jnp.int32, (tm, 1), 0)
        hrows = i * tm - CHALO + lax.broadcasted_iota(jnp.int32, (CHALO, 1), 0)
        dwin[0:tm, :] = d_ref[...]
        dwin[tm:tm + CHALO, :] = jnp.where(i == nt - 1, 0.0, dh_ref[...])
        cwin[0:CHALO, :] = _cgate(cvh_ref[...], cgh_ref[...], hrows, nex, tp)
        cwin[CHALO:CHALO + tm, :] = _cgate(cv_ref[...], cg_ref[...], rows, nex, tp)

        @pl.when(i == 0)
        def _():
            dw_ref[...] = jnp.zeros_like(dw_ref)
            cs_ref[...] = jnp.zeros_like(cs_ref)

        for sb in range(sub):
            lo = sb * CHALO
            acc = jnp.zeros((CHALO, CW), F32)
            for k in range(CTAPS):
                acc = acc + w_ref[k:k + 1, :] * dwin[lo + CTAPS - 1 - k:lo + CTAPS - 1 - k + CHALO, :]
            dsb = dwin[lo:lo + CHALO, :]
            for k in range(CTAPS):
                dw_ref[k:k + 1, :] += jnp.sum(dsb * cwin[lo + 2 + k:lo + 2 + k + CHALO, :], axis=0, keepdims=True)
            srows = rows[lo:lo + CHALO]
            dcg = jnp.where(_pad_rows(srows, nex, tp), 0.0, acc)
            cv = cv_ref[lo:lo + CHALO, :]
            sg = _sigmoid(cg_ref[lo:lo + CHALO, :])
            dcv = dcg * sg
            dgt = dcg * cv * sg * (1.0 - sg)
            dz_ref[lo:lo + CHALO, 0:CW] = dcv.astype(BF16)
            dz_ref[lo:lo + CHALO, CW:2 * CW] = dgt.astype(BF16)
            cs_ref[:, 0:CW] += jnp.sum(dcv, axis=0, keepdims=True)
            cs_ref[:, CW:2 * CW] += jnp.sum(dgt, axis=0, keepdims=True)

    nxt = lambda i: jnp.minimum((i + 1) * sub, m // CHALO - 1)
    prv = lambda i: jnp.maximum(i * sub - 1, 0)
    return _call(body, "conv_bwd_b",
                 (jax.ShapeDtypeStruct((m, 2 * CW), BF16), jax.ShapeDtypeStruct((CHALO, CW), F32),
                  jax.ShapeDtypeStruct((1, 2 * CW), F32)),
                 (nt,),
                 [pl.BlockSpec((tm, CW), lambda i: (i, 0)), pl.BlockSpec((CHALO, CW), lambda i: (nxt(i), 0)),
                  pl.BlockSpec((tm, CW), lambda i: (i, cvc)), pl.BlockSpec((tm, CW), lambda i: (i, cgc)),
                  pl.BlockSpec((CHALO, CW), lambda i: (prv(i), cvc)), pl.BlockSpec((CHALO, CW), lambda i: (prv(i), cgc)),
                  pl.BlockSpec((CTAPS, CW), lambda i: (0, 0))],
                 (pl.BlockSpec((tm, 2 * CW), lambda i: (i, 0)), pl.BlockSpec((CHALO, CW), lambda i: (0, 0)),
                  pl.BlockSpec((1, 2 * CW), lambda i: (0, 0))),
                 scratch=[pltpu.VMEM((tm + CHALO, CW), F32), pltpu.VMEM((CHALO + tm, CW), F32)])(dcc, dcc, z, z, z, z, w)


def _attn_bwd(z, bias, sinks, dya, wap4, nex, nblk):
    m = z.shape[0]
    tp = nblk * BLK
    ns = wap4.shape[2]
    blk_of = lambda j: nblk - 1 - j

    def body(q_ref, cur_ref, prev_ref, meta_ref, bias_ref, sink_ref, dy_ref, w_ref,
             dz_ref, cs_ref, dsk_ref, dbias_ref, ks, vs, carry, macc, dqacc, dkv, okv):
        bb, j = pl.program_id(0), pl.program_id(1)
        n = nblk - 1 - j
        _fill_kv(ks, vs, prev_ref, cur_ref, meta_ref)
        q = q_ref[...]
        da = jnp.zeros((BLK, AW), F32)
        for jj in range(NCHIP):
            da = da + _dot_nt(dy_ref[:, jj * ns:(jj + 1) * ns], w_ref[jj])

        @pl.when(j == 0)
        def _():
            carry[...] = jnp.zeros_like(carry)
            macc[...] = jnp.zeros_like(macc)

        @pl.when((bb == 0) & (j == 0))
        def _():
            cs_ref[...] = jnp.zeros_like(cs_ref)
            dsk_ref[...] = jnp.zeros_like(dsk_ref)

        @pl.when((j == 0) | (n <= 1))
        def _():
            dbias_ref[...] = jnp.zeros_like(dbias_ref)

        lane = lax.broadcasted_iota(jnp.int32, (1, BLK), 1)
        dsk = jnp.zeros((1, BLK), F32)
        for hk in range(NKV):
            qg, pn, psink = _softmax_group(q, ks[hk], bias_ref, sink_ref, hk)
            dog = jnp.concatenate([da[:, (hk * GRP + g) * HD:(hk * GRP + g + 1) * HD] for g in range(GRP)], axis=0).astype(BF16)
            dp = _dot_nt(dog, vs[hk])
            dl = jnp.sum(pn * dp, axis=-1, keepdims=True)
            ds = pn * (dp - dl)
            dbias_ref[hk * GRP:(hk + 1) * GRP] += ds.reshape(GRP, BLK, NKEY)
            dsr = -psink * dl
            for g in range(GRP):
                dsk = dsk + jnp.where(lane == hk * GRP + g, jnp.sum(dsr[g * BLK:(g + 1) * BLK]), 0.0)
            dsb = (ds * (HD ** -0.5)).astype(BF16)
            dqg = _dot(dsb, ks[hk])
            for g in range(GRP):
                h = hk * GRP + g
                dqacc[:, h * HD:(h + 1) * HD] = dqg[g * BLK:(g + 1) * BLK, :]
            dkv[:, hk * HD:(hk + 1) * HD] = _dot_tn(dsb, qg)
            dkv[:, KVW + hk * HD:KVW + (hk + 1) * HD] = _dot_tn(pn.astype(BF16), dog)
        dsk_ref[...] += dsk
        macc[...] += dkv[2 * BLK:2 * BLK + N_META, :]
        okv[...] = dkv[BLK:2 * BLK, :] + carry[...]
        carry[...] = dkv[0:BLK, :]

        @pl.when(n == 0)
        def _():
            okv[PAD:BLK, :] += macc[...]

        dq = dqacc[...]
        ok = okv[...]
        dz_ref[:, 0:AW] = dq.astype(BF16)
        dz_ref[:, AW:AW + 2 * KVW] = ok.astype(BF16)
        cs_ref[:, 0:AW] += jnp.sum(dq, axis=0, keepdims=True)
        cs_ref[:, AW:AW + 2 * KVW] += jnp.sum(ok, axis=0, keepdims=True)

    wz = AW + 2 * KVW
    specs = _attn_specs(nblk, tp, blk_of) + [
        pl.BlockSpec((BLK, D), lambda bb, j: (bb * nblk + blk_of(j), 0)),
        pl.BlockSpec((NCHIP, AW, ns), lambda bb, j: (0, 0, 0))]
    return _call(body, "attn_bwd",
                 (jax.ShapeDtypeStruct((m, wz), BF16), jax.ShapeDtypeStruct((1, wz), F32), jax.ShapeDtypeStruct((1, BLK), F32),
                  jax.ShapeDtypeStruct((nex, 3, NQ, BLK, NKEY), F32)),
                 (nex, nblk), specs,
                 (pl.BlockSpec((BLK, wz), lambda bb, j: (bb * nblk + blk_of(j), 0)), pl.BlockSpec((1, wz), lambda bb, j: (0, 0)),
                  pl.BlockSpec((1, BLK), lambda bb, j: (0, 0)),
                  pl.BlockSpec((None, None, NQ, BLK, NKEY), lambda bb, j: (bb, jnp.minimum(blk_of(j), 2), 0, 0, 0))),
                 scratch=[pltpu.VMEM((NKV, NKEY, HD), BF16), pltpu.VMEM((NKV, NKEY, HD), BF16),
                          pltpu.VMEM((BLK, 2 * KVW), F32), pltpu.VMEM((N_META, 2 * KVW), F32),
                          pltpu.VMEM((BLK, AW), F32), pltpu.VMEM((NKEY, 2 * KVW), F32), pltpu.VMEM((BLK, 2 * KVW), F32)],
                 )(z, z, z, z, bias, sinks.reshape(1, NQ), dya, wap4)


def _tile_rows(rows, cols, target_bytes=1 << 20):
    best = None
    for t in range(8, rows + 1, 8):
        if rows % t == 0 and t * cols * 4 <= target_bytes:
            best = t
    return best or rows


def _sum0(name, x):
    n, r, c = x.shape
    tr = _tile_rows(r, c * n)

    def body(x_ref, o_ref):
        acc = x_ref[0]
        for k in range(1, n):
            acc = acc + x_ref[k]
        o_ref[...] = acc

    return _call(body, name, jax.ShapeDtypeStruct((r, c), F32), (r // tr,),
                 [pl.BlockSpec((n, tr, c), lambda i: (0, i, 0))], pl.BlockSpec((tr, c), lambda i: (i, 0)))(x)


def _adamw(name, w, g, mom, vel):
    r, c = w.shape
    tr = _tile_rows(r, c)
    c1 = 1.0 / (1.0 - ADAM_B1 ** ADAM_STEP)
    c2 = 1.0 / (1.0 - ADAM_B2 ** ADAM_STEP)

    def body(w_ref, g_ref, m_ref, v_ref, d_ref, mo_ref, vo_ref):
        gg = g_ref[...]
        mn = ADAM_B1 * m_ref[...] + (1.0 - ADAM_B1) * gg
        vn = ADAM_B2 * v_ref[...] + (1.0 - ADAM_B2) * (gg * gg)
        mo_ref[...] = mn
        vo_ref[...] = vn
        d_ref[...] = -ADAM_LR * ((mn * c1) / (jnp.sqrt(vn * c2) + ADAM_EPS) + ADAM_WD * w_ref[...])

    spec = pl.BlockSpec((tr, c), lambda i: (i, 0))
    o = jax.ShapeDtypeStruct((r, c), F32)
    return _call(body, name, (o, o, o), (r // tr,), [spec] * 4, (spec, spec, spec))(w, g, mom, vel)


def _place():
    x, y, c = lax.axis_index("x"), lax.axis_index("y"), lax.axis_index("c")
    others = [(1 - x, y), (x, 1 - y), (1 - x, 1 - y)]
    return x, y, c, others


_ANY = pl.BlockSpec(memory_space=pl.ANY)


def _gather_weights(shards):
    nw = len(shards)

    def body(*refs):
        s_refs, g_refs = refs[:nw], refs[nw:2 * nw]
        send, recv = refs[2 * nw:]
        x, y, c, others = _place()
        chip = 2 * x + y
        me, sib = (x, y, c), (x, y, 1 - c)

        def rcopy(w, k, src, dst, to):
            return pltpu.make_async_remote_copy(src_ref=src, dst_ref=dst, send_sem=send.at[w, k], recv_sem=recv.at[w, k],
                                                device_id=to, device_id_type=MESH)

        firsts, passed = [], []
        for w in range(nw):
            for k, (px, py) in enumerate(others):
                cp = rcopy(w, k, s_refs[w].at[c], g_refs[w].at[chip, c], (px, py, c))
                cp.start()
                firsts.append(cp)
        for w in range(nw):
            for k, (px, py) in enumerate(others):
                blk = g_refs[w].at[2 * px + py, c]
                rcopy(w, k, blk, blk, me).wait_recv()
                cp = rcopy(w, 3 + k, blk, blk, sib)
                cp.start()
                passed.append(cp)
        for w in range(nw):
            for k, (px, py) in enumerate(others):
                blk = g_refs[w].at[2 * px + py, 1 - c]
                rcopy(w, 3 + k, blk, blk, me).wait_recv()
        for cp in firsts + passed:
            cp.wait_send()

    outs = tuple(jax.ShapeDtypeStruct((NCHIP,) + s.shape, s.dtype) for s in shards)
    return pl.pallas_call(
        body, name="gather_weights", out_shape=outs, in_specs=[_ANY] * nw, out_specs=tuple([_ANY] * nw),
        scratch_shapes=[pltpu.SemaphoreType.DMA((nw, 6)), pltpu.SemaphoreType.DMA((nw, 6))],
    )(*shards)


def _sibling_swap_halves(grads):
    ng = len(grads)

    def body(*refs):
        d_refs, a_refs = refs[:ng], refs[ng:2 * ng]
        send, recv = refs[2 * ng:]
        x, y, c, _ = _place()
        cps = []
        for w in range(ng):
            h = d_refs[w].shape[1] // 2
            cp = pltpu.make_async_remote_copy(src_ref=d_refs[w].at[:, pl.ds((1 - c) * h, h), :], dst_ref=a_refs[w],
                                              send_sem=send.at[w], recv_sem=recv.at[w], device_id=(x, y, 1 - c), device_id_type=MESH)
            cp.start()
            cps.append(cp)
        for cp in cps:
            cp.wait()

    outs = tuple(jax.ShapeDtypeStruct((NCHIP, g.shape[1] // 2, g.shape[2]), g.dtype) for g in grads)
    return pl.pallas_call(
        body, name="grad_swap_halves", out_shape=outs, in_specs=[_ANY] * ng, out_specs=tuple([_ANY] * ng),
        scratch_shapes=[pltpu.SemaphoreType.DMA((ng,)), pltpu.SemaphoreType.DMA((ng,))])(*grads)


def _chip_exchange(parts):
    ng = len(parts)

    def body(*refs):
        q_refs, b_refs = refs[:ng], refs[ng:2 * ng]
        send, recv = refs[2 * ng:]
        x, y, c, others = _place()
        cps = []
        for w in range(ng):
            for k, (px, py) in enumerate(others):
                cp = pltpu.make_async_remote_copy(src_ref=q_refs[w].at[2 * px + py], dst_ref=b_refs[w].at[k],
                                                  send_sem=send.at[w, k], recv_sem=recv.at[w, k],
                                                  device_id=(px, py, c), device_id_type=MESH)
                cp.start()
                cps.append(cp)
        for cp in cps:
            cp.wait()

    outs = tuple(jax.ShapeDtypeStruct((3,) + p.shape[1:], p.dtype) for p in parts)
    return pl.pallas_call(
        body, name="grad_chip_exchange", out_shape=outs, in_specs=[_ANY] * ng, out_specs=tuple([_ANY] * ng),
        scratch_shapes=[pltpu.SemaphoreType.DMA((ng, 3)), pltpu.SemaphoreType.DMA((ng, 3))])(*parts)


def _sibling_join(halves):
    nw = len(halves)

    def body(*refs):
        h_refs = refs[:2 * nw]
        f_refs = refs[2 * nw:3 * nw]
        send, recv = refs[3 * nw:]
        x, y, c, _ = _place()
        cps = []
        for w in range(nw):
            for l in range(2):
                src = h_refs[2 * w + l]
                h = src.shape[0]
                dst = f_refs[w].at[l, pl.ds(c * h, h), :]
                cp = pltpu.make_async_remote_copy(src_ref=src, dst_ref=dst, send_sem=send.at[w, l], recv_sem=recv.at[w, l],
                                                  device_id=(x, y, 1 - c), device_id_type=MESH)
                cp.start()
                cps.append(cp)
        for w in range(nw):
            for l in range(2):
                src = h_refs[2 * w + l]
                h = src.shape[0]
                other = f_refs[w].at[l, pl.ds((1 - c) * h, h), :]
                pltpu.make_async_remote_copy(src_ref=src, dst_ref=other, send_sem=send.at[w, l], recv_sem=recv.at[w, l],
                                             device_id=(x, y, c), device_id_type=MESH).wait_recv()
        for cp in cps:
            cp.wait_send()

    flat = [a for pair in halves for a in pair]
    outs = tuple(jax.ShapeDtypeStruct((2, 2 * pair[0].shape[0], pair[0].shape[1]), F32) for pair in halves)
    return pl.pallas_call(
        body, name="grad_sibling_join", out_shape=outs, in_specs=[_ANY] * (2 * nw), out_specs=tuple([_ANY] * nw),
        scratch_shapes=[pltpu.SemaphoreType.DMA((nw, 2)), pltpu.SemaphoreType.DMA((nw, 2))])(*flat)


def _allgather_small(v):
    r = v.shape[0]

    def body(x_ref, out_ref, send_sems, recv_sems, local_sem):
        x, y, c, chips = _place()
        me, sibling = (x, y, c), (x, y, 1 - c)

        def slab(px, py, pc):
            return out_ref.at[4 * px + 2 * py + pc]

        def copy(k, block, to, src=None):
            return pltpu.make_async_remote_copy(src_ref=slab(*block) if src is None else src, dst_ref=slab(*block),
                                                send_sem=send_sems.at[k], recv_sem=recv_sems.at[k],
                                                device_id=to, device_id_type=MESH)

        mine = pltpu.make_async_copy(x_ref, slab(*me), local_sem)
        mine.start()
        first = [copy(0, me, sibling, src=x_ref)]
        first += [copy(1 + j, me, (*chip, c), src=x_ref) for j, chip in enumerate(chips)]
        for cp in first:
            cp.start()
        passed = [copy(4 + j, (*chip, c), sibling) for j, chip in enumerate(chips)]
        for j, chip in enumerate(chips):
            copy(1 + j, (*chip, c), me).wait_recv()
            passed[j].start()
        copy(0, sibling, me).wait_recv()
        for j, chip in enumerate(chips):
            copy(4 + j, (*chip, 1 - c), me).wait_recv()
        for cp in first + passed:
            cp.wait_send()
        mine.wait()

    return pl.pallas_call(
        body, name="allgather_small", out_shape=jax.ShapeDtypeStruct((8, r, 128), F32),
        in_specs=[pl.BlockSpec(memory_space=pltpu.VMEM)], out_specs=pl.BlockSpec(memory_space=pltpu.VMEM),
        scratch_shapes=[pltpu.SemaphoreType.DMA((7,)), pltpu.SemaphoreType.DMA((7,)), pltpu.SemaphoreType.DMA],
    )(v)


def _add_half(name, d, a, c):
    _, h, cols = a.shape

    def body(c_ref, d_ref, a_ref, o_ref):
        o_ref[...] = (d_ref[...] + a_ref[...]).astype(BF16)

    return _call(body, name, jax.ShapeDtypeStruct(a.shape, BF16), (NCHIP,),
                 [pl.BlockSpec((None, h, cols), lambda p, cr: (p, cr[0], 0)), pl.BlockSpec((None, h, cols), lambda p, cr: (p, 0, 0))],
                 pl.BlockSpec((None, h, cols), lambda p, cr: (p, 0, 0)), prefetch=1)(c, d, a)


def _add_chips(name, d, a, b, where):
    _, h, cols = a.shape
    th = h // 4 if (h % 64 == 0) else h
    nt = h // th

    def body(w_ref, d_ref, a_ref, b_ref, o_ref):
        own = d_ref[...] + a_ref[...]
        o_ref[...] = ((own + b_ref[0].astype(F32)) + b_ref[1].astype(F32)) + b_ref[2].astype(F32)

    return _call(body, name, jax.ShapeDtypeStruct((h, cols), F32), (nt,),
                 [pl.BlockSpec((None, th, cols), lambda i, wr: (wr[0], wr[1] * nt + i, 0)),
                  pl.BlockSpec((None, th, cols), lambda i, wr: (wr[0], i, 0)),
                  pl.BlockSpec((3, th, cols), lambda i, wr: (0, i, 0))],
                 pl.BlockSpec((th, cols), lambda i, wr: (i, 0)), prefetch=1)(where, d, a, b)


def _pack(arrs):
    pieces = []
    for a in arrs:
        f = a.reshape(-1)
        n = -(-f.shape[0] // 1024) * 1024
        pieces.append(jnp.pad(f, (0, n - f.shape[0])).reshape(-1, 128))
    return jnp.concatenate(pieces, axis=0)


def _unpack(buf, shapes):
    out, r = [], 0
    for s in shapes:
        n = int(np.prod(s))
        rows = -(-n // 1024) * 8
        out.append(buf[r:r + rows].reshape(-1)[:n].reshape(s))
        r += rows
    return out


def kernel(x, meta_tokens, in_ln_g, in_ln_b, rel_bias, w_in, b_in, attn_sinks, w_attn_proj, conv_dw, conv_dw_b, conv_ln_g, conv_ln_b, w_conv_proj, w_out, ln1_g, ln1_b, ffn_w_up, ffn_dw, ffn_dw_b, ffn_w_down, ln2_g, ln2_b, loss_target, m_meta_tokens, m_in_ln_g, m_in_ln_b, m_rel_bias, m_w_in, m_b_in, m_attn_sinks, m_w_attn_proj, m_conv_dw, m_conv_dw_b, m_conv_ln_g, m_conv_ln_b, m_w_conv_proj, m_w_out, m_ln1_g, m_ln1_b, m_ffn_w_up, m_ffn_dw, m_ffn_dw_b, m_ffn_w_down, m_ln2_g, m_ln2_b, v_meta_tokens, v_in_ln_g, v_in_ln_b, v_rel_bias, v_w_in, v_b_in, v_attn_sinks, v_w_attn_proj, v_conv_dw, v_conv_dw_b, v_conv_ln_g, v_conv_ln_b, v_w_conv_proj, v_w_out, v_ln1_g, v_ln1_b, v_ffn_w_up, v_ffn_dw, v_ffn_dw_b, v_ffn_w_down, v_ln2_g, v_ln2_b):
    nex, seq, _ = x.shape
    nblk = seq // BLK + 1
    tp = nblk * BLK
    m = nex * tp
    tm = _row_tile(m)
    ffs = ffn_w_up.shape[2]
    dff = 2 * ffs
    cx, cy, cc = lax.axis_index("x"), lax.axis_index("y"), lax.axis_index("c")
    chip = (2 * cx + cy).astype(jnp.int32)
    core = cc.astype(jnp.int32)

    big = [w_in, w_attn_proj, w_conv_proj, w_out, ffn_w_up, ffn_w_down]
    shards = [w.astype(BF16) for w in big]
    shards += [meta_tokens.reshape(2, N_META // 2, -1), conv_dw, ffn_dw]
    gathered = _gather_weights(shards)
    g_in, g_ap, g_cp, g_out, g_up, g_down, g_meta, g_cdw, g_fdw = [
        lax.dynamic_update_index_in_dim(g, s, chip, 0) for g, s in zip(gathered, shards)]
    meta_full = jnp.transpose(g_meta, (1, 2, 0, 3)).reshape(N_META, D)
    bias_tab = _bias_build(rel_bias)

    def layer_weights(l):
        win_old = jnp.transpose(g_in[:, l], (1, 0, 2)).reshape(D, IN_COLS)
        return dict(
            win=_to_new(win_old), bin=_to_new(b_in[l]).reshape(1, IN_COLS),
            wap=g_ap[:, l], wcp=g_cp[:, l], wout=g_out[:, l].reshape(D, D), wup=g_up[:, l],
            wdown=g_down[:, l].reshape(dff, D),
            cdw=jnp.transpose(g_cdw[:, l], (1, 0, 2)).reshape(CTAPS, CW),
            fdw=jnp.transpose(g_fdw[:, l], (1, 0, 2)).reshape(FTAPS, 2 * dff),
            fdwb=ffn_dw_b[l].reshape(1, 2 * dff))

    lw = [layer_weights(l) for l in range(DEPTH)]

    raw, h, hb = _embed_ln(x, meta_full, in_ln_g, in_ln_b, nblk)
    saved = []
    for l in range(DEPTH):
        p = lw[l]
        z = _mm_bias("in_proj", hb, p["win"], p["bin"], IN_COLS // 3, tm)
        a = _attn_fwd(z, bias_tab, attn_sinks[l], nex, nblk)
        ccv, cs = _conv_fwd(z, p["cdw"], conv_dw_b[l], conv_ln_g[l], conv_ln_b[l], nex, tp)
        ya, yc, mixed = _mix_fwd(a, cs, p["wap"], p["wcp"], z, tm)
        r1, h1, h1b = _mm_res_ln("out_proj_ln", mixed, p["wout"], h, ln1_g[l], ln1_b[l], tm)
        up3 = _ffn_up(h1b, p["wup"], tm, nex, tp)
        act = _ffn_act(up3, p["fdw"], p["fdwb"], tm)
        r2, h2, h2b = _mm_res_ln("ffn_down_ln", act, p["wdown"], h1, ln2_g[l], ln2_b[l], tm)
        saved.append(dict(hb=hb, z=z, a=a, cc=ccv, cs=cs, ya=ya, yc=yc, mixed=mixed, r1=r1, h1b=h1b, up3=up3, act=act, r2=r2))
        h, hb = h2, h2b

    dy, sq = _loss_grad(h, loss_target, nblk)
    loss = lax.psum(0.5 / D * jnp.sum(sq), ("x", "y", "c"))

    big_grads = [None] * DEPTH
    small = {}
    prev_a, prev_w, prev_cols = (), None, None
    dprev = dy
    for l in reversed(range(DEPTH)):
        p, s = lw[l], saved[l]
        dr2, dr2b, dg2, db2 = _ln_bwd_call("ln2_bwd", dprev, s["r2"], ln2_g[l], tm, prev_a, prev_w, prev_cols)
        dup3 = _ffn_act_bwd(dr2b, p["wdown"], s["up3"], p["fdw"], p["fdwb"], tm)
        dpre3, dfdw, dfdwb = _ffn_conv_bwd(dup3, s["up3"], p["fdw"], tm, nex, tp)
        gw_down = _mm_tn("dw_down", s["act"], dr2b, ffs, D // 2)
        gw_up = _mm_tn("dw_up", s["h1b"], dpre3, D, ffs, b_cols=lambda j: (j // 2, j % 2), chip_out=True)
        dr1, dr1b, dg1, db1 = _ln_bwd_call("ln1_bwd", dr2, s["r1"], ln1_g[l], tm // 2, (dpre3,), p["wup"])
        dya, dyc, dzg, csg = _gate_bwd(dr1b, p["wout"], s["ya"], s["yc"], s["z"], tm)
        gw_out = _mm_tn("dw_out", s["mixed"], dr1b, D, D // 2)
        gw_ap = _mm_tn("dw_attn_proj", s["a"], dya, AW, D // NCHIP, chip_out=True)
        gw_cp = _mm_tn("dw_conv_proj", s["cs"], dyc, CW, D // NCHIP, chip_out=True)
        dcc, dclg, dclb, dcwb = _conv_bwd_a(dyc, p["wcp"], s["cc"], conv_ln_g[l], conv_ln_b[l], tm)
        dzc, dcdw, csc = _conv_bwd_b(dcc, s["z"], p["cdw"], nex, tp)
        dzq, csq, dsk, dbias = _attn_bwd(s["z"], bias_tab, attn_sinks[l], dya, p["wap"], nex, nblk)
        gin = [_mm_tn("dw_in_gates", s["hb"], dzg, D, D // 2), _mm_tn("dw_in_conv", s["hb"], dzc, D, CW),
               _mm_tn("dw_in_qkv", s["hb"], dzq, D, 2 * KVW)]
        gin_old = _to_old(jnp.concatenate(gin, axis=1))
        gw_in = jnp.transpose(gin_old.reshape(D, NCHIP, IN_COLS // NCHIP), (1, 0, 2))
        big_grads[l] = [gw_in, gw_ap, gw_cp, gw_out.reshape(NCHIP, D // NCHIP, D), gw_up,
                        gw_down.reshape(NCHIP, dff // NCHIP, D)]
        small[l] = dict(
            b_in=_to_old(jnp.concatenate([csg, csc, csq], axis=1)).reshape(IN_COLS), attn_sinks=dsk[0, :NQ],
            conv_dw=dcdw[:CTAPS], conv_dw_b=dcwb.reshape(CW), conv_ln_g=dclg.reshape(CW), conv_ln_b=dclb.reshape(CW),
            ln1_g=dg1.reshape(D), ln1_b=db1.reshape(D),
            ffn_dw=jnp.transpose(dfdw, (1, 0, 2)).reshape(FTAPS, 2 * dff), ffn_dw_b=jnp.transpose(dfdwb, (1, 0, 2)).reshape(2 * dff),
            ln2_g=dg2.reshape(D), ln2_b=db2.reshape(D), bias=_bias_grad(dbias))
        dprev = dr1
        prev_a, prev_w, prev_cols = (dzg, dzc, dzq), p["win"], [(C_GATES, C_CONV), (C_CONV, C_QKV), (C_QKV, IN_COLS)]
    draw, _, dg0, db0 = _ln_bwd_call("in_ln_bwd", dprev, raw, in_ln_g, tm, prev_a, prev_w, prev_cols)
    draw3 = draw.reshape(nex, tp, D)
    grad_x = draw3[:, BLK:, :]
    dmeta = _sum0("meta_grad_sum", draw3[:, PAD:BLK, :])

    names_l = ["b_in", "attn_sinks", "conv_dw", "conv_dw_b", "conv_ln_g", "conv_ln_b", "ln1_g", "ln1_b", "ffn_dw", "ffn_dw_b", "ln2_g", "ln2_b"]
    part_list = [dmeta, dg0.reshape(D), db0.reshape(D), small[0]["bias"] + small[1]["bias"]]
    part_list += [jnp.stack([small[0][n], small[1][n]]) for n in names_l]
    shapes_small = [tuple(a.shape) for a in part_list]
    tot = _sum0("small_grad_sum", _allgather_small(_pack(part_list)))
    (g_meta_f, g_inlg, g_inlb, g_biasp, g_bin, g_sinks, g_cdw_f, g_cdwb, g_clg, g_clb, g_l1g, g_l1b, g_fdw_f, g_fdwb,
     g_l2g, g_l2b) = _unpack(tot, shapes_small)
    g_relb = g_biasp
    csh = D // NCHIP
    g_meta_s = lax.dynamic_slice_in_dim(g_meta_f, chip * csh, csh, axis=1)
    g_cdw_s = lax.dynamic_slice_in_dim(g_cdw_f, chip * (CW // NCHIP), CW // NCHIP, axis=2)
    g_fdw_s = lax.dynamic_slice_in_dim(g_fdw_f, chip * ffs, ffs, axis=2)

    flat = [big_grads[l][w] for w in range(6) for l in range(DEPTH)]
    sib = _sibling_swap_halves(flat)
    cvec, where = core.reshape(1), jnp.stack([chip, core])
    q = [_add_half("grad_add_sibling", d, a, cvec) for d, a in zip(flat, sib)]
    got = _chip_exchange(q)
    red = [_add_chips("grad_add_chips", d, a, bb, where) for d, a, bb in zip(flat, sib, got)]
    joined = _sibling_join([[red[2 * w], red[2 * w + 1]] for w in range(6)])
    full = []
    for w, f in enumerate(joined):
        hh = red[2 * w].shape[0]
        for l in range(DEPTH):
            f = lax.dynamic_update_slice(f, red[2 * w + l][None], (l, core * hh, 0))
        full.append(f)

    moms = [m_w_in, m_w_attn_proj, m_w_conv_proj, m_w_out, m_ffn_w_up, m_ffn_w_down]
    vels = [v_w_in, v_w_attn_proj, v_w_conv_proj, v_w_out, v_ffn_w_up, v_ffn_w_down]
    big_out = []
    for w, g, mo, ve in zip(big, full, moms, vels):
        sh = w.shape
        two = lambda t: t.reshape(sh[0] * sh[1], sh[2])
        d_, m_, v_ = _adamw("adamw_matrix", two(w), two(g), two(mo), two(ve))
        big_out.append((g.reshape(sh), d_.reshape(sh), m_.reshape(sh), v_.reshape(sh)))

    sm_w = [meta_tokens, in_ln_g, in_ln_b, rel_bias, b_in, attn_sinks, conv_dw, conv_dw_b, conv_ln_g, conv_ln_b, ln1_g, ln1_b,
            ffn_dw, ffn_dw_b, ln2_g, ln2_b]
    sm_m = [m_meta_tokens, m_in_ln_g, m_in_ln_b, m_rel_bias, m_b_in, m_attn_sinks, m_conv_dw, m_conv_dw_b, m_conv_ln_g, m_conv_ln_b,
            m_ln1_g, m_ln1_b, m_ffn_dw, m_ffn_dw_b, m_ln2_g, m_ln2_b]
    sm_v = [v_meta_tokens, v_in_ln_g, v_in_ln_b, v_rel_bias, v_b_in, v_attn_sinks, v_conv_dw, v_conv_dw_b, v_conv_ln_g, v_conv_ln_b,
            v_ln1_g, v_ln1_b, v_ffn_dw, v_ffn_dw_b, v_ln2_g, v_ln2_b]
    sm_g = [g_meta_s, g_inlg, g_inlb, g_relb, g_bin, g_sinks, g_cdw_s, g_cdwb, g_clg, g_clb, g_l1g, g_l1b, g_fdw_s, g_fdwb, g_l2g, g_l2b]
    sm_shapes = [tuple(a.shape) for a in sm_w]
    sd, smn, svn = _adamw("adamw_small", _pack(sm_w), _pack(sm_g), _pack(sm_m), _pack(sm_v))
    sd, smn, svn = _unpack(sd, sm_shapes), _unpack(smn, sm_shapes), _unpack(svn, sm_shapes)

    order = ["meta_tokens", "in_ln_g", "in_ln_b", "rel_bias", "w_in", "b_in", "attn_sinks", "w_attn_proj", "conv_dw", "conv_dw_b",
             "conv_ln_g", "conv_ln_b", "w_conv_proj", "w_out", "ln1_g", "ln1_b", "ffn_w_up", "ffn_dw", "ffn_dw_b", "ffn_w_down",
             "ln2_g", "ln2_b"]
    small_names = ["meta_tokens", "in_ln_g", "in_ln_b", "rel_bias", "b_in", "attn_sinks", "conv_dw", "conv_dw_b", "conv_ln_g",
                   "conv_ln_b", "ln1_g", "ln1_b", "ffn_dw", "ffn_dw_b", "ln2_g", "ln2_b"]
    big_names = ["w_in", "w_attn_proj", "w_conv_proj", "w_out", "ffn_w_up", "ffn_w_down"]
    res = {}
    for i, n in enumerate(small_names):
        res[n] = (sm_g[i], sd[i], smn[i], svn[i])
    for i, n in enumerate(big_names):
        res[n] = big_out[i]
    outs = [loss, grad_x]
    for k in range(4):
        outs += [res[n][k] for n in order]
    return tuple(outs)
```

```python
import functools
import math
from typing import Any, Callable, NamedTuple, Sequence

import numpy as np
import jax
import jax.numpy as jnp
from jax import lax
from jax.experimental import pallas as pl
from jax.experimental.pallas import tpu as pltpu

F32 = jnp.float32
BF16 = jnp.bfloat16
MESH = pl.DeviceIdType.MESH

D = 1024
N_META = 16
BLK = 128
PAD = BLK - N_META
HD = 64
NQ = 8
NKV = 2
GRP = NQ // NKV
AW = NQ * HD
KVW = NKV * HD
CW = D // 2
CTAPS = 31
FTAPS = 3
NBUCKET = 32
MAXDIST = 128
EPS = 1e-5
DEPTH = 2
ALPHA = (2.0 * DEPTH) ** 0.25
NCHIP = 4
NKEY = 3 * BLK
NEG = -1e30
CHALO = 32
FHALO = 8
IN_COLS = AW + 2 * KVW + 2 * CW + 2 * D
_OLD = dict(q=(0, AW), k=(AW, AW + KVW), v=(AW + KVW, AW + 2 * KVW), cv=(AW + 2 * KVW, AW + 2 * KVW + CW),
            cg=(AW + 2 * KVW + CW, AW + 2 * KVW + 2 * CW), ga=(AW + 2 * KVW + 2 * CW, AW + 2 * KVW + 2 * CW + D),
            gc=(AW + 2 * KVW + 2 * CW + D, IN_COLS))
_NEW_ORDER = ("ga", "gc", "cv", "cg", "q", "k", "v")
C_GATES, C_CONV, C_QKV = 0, 2 * D, 2 * D + 2 * CW

ADAM_LR, ADAM_B1, ADAM_B2, ADAM_EPS, ADAM_WD, ADAM_STEP = 0.001, 0.9, 0.999, 1e-08, 0.01, 10


def _to_new(a):
    return jnp.concatenate([a[..., _OLD[n][0]:_OLD[n][1]] for n in _NEW_ORDER], axis=-1)


def _to_old(a):
    offs, o = {}, 0
    for n in _NEW_ORDER:
        w = _OLD[n][1] - _OLD[n][0]
        offs[n] = (o, o + w)
        o += w
    return jnp.concatenate([a[..., offs[n][0]:offs[n][1]] for n in ("q", "k", "v", "cv", "cg", "ga", "gc")], axis=-1)


class _Job(NamedTuple):
    ins: Sequence[Any]
    outs: Sequence[Any]
    sems: tuple
    start: Callable
    finish: Callable


_ANY = pl.BlockSpec(memory_space=pl.ANY)


def _call(body, name, out_shape, grid, in_specs, out_specs, scratch=(), prefetch=0, side=None):
    params = pltpu.CompilerParams(dimension_semantics=("arbitrary",) * len(grid))
    if side is None:
        if prefetch:
            gs = pltpu.PrefetchScalarGridSpec(num_scalar_prefetch=prefetch, grid=grid, in_specs=in_specs,
                                              out_specs=out_specs, scratch_shapes=list(scratch))
            return pl.pallas_call(body, name=name, out_shape=out_shape, grid_spec=gs, compiler_params=params)
        return pl.pallas_call(body, name=name, out_shape=out_shape, grid=grid, in_specs=in_specs, out_specs=out_specs,
                              scratch_shapes=list(scratch), compiler_params=params)
    assert not prefetch
    single = not isinstance(out_shape, (tuple, list))
    main_shapes = (out_shape,) if single else tuple(out_shape)
    main_specs = (out_specs,) if single else tuple(out_specs)
    n_in, n_sin, n_out, n_sout, n_scr = len(in_specs), len(side.ins), len(main_shapes), len(side.outs), len(scratch)

    def wrapped(*refs):
        main_in, sin = refs[:n_in], refs[n_in:n_in + n_sin]
        o0 = n_in + n_sin
        main_out, sout = refs[o0:o0 + n_out], refs[o0 + n_out:o0 + n_out + n_sout]
        s0 = o0 + n_out + n_sout
        main_scr, (send, recv) = refs[s0:s0 + n_scr], refs[s0 + n_scr:]
        first = functools.reduce(lambda a, b: a & b, [pl.program_id(k) == 0 for k in range(len(grid))])
        last = functools.reduce(lambda a, b: a & b, [pl.program_id(k) == grid[k] - 1 for k in range(len(grid))])

        @pl.when(first)
        def _():
            side.start(sin, sout, send, recv)

        body(*main_in, *main_out, *main_scr)

        @pl.when(last)
        def _():
            side.finish(sin, sout, send, recv)

    call = pl.pallas_call(
        wrapped, name=name, out_shape=main_shapes + tuple(side.outs), grid=grid,
        in_specs=list(in_specs) + [_ANY] * n_sin, out_specs=main_specs + tuple([_ANY] * n_sout),
        scratch_shapes=list(scratch) + [pltpu.SemaphoreType.DMA(side.sems), pltpu.SemaphoreType.DMA(side.sems)],
        compiler_params=params)
    return lambda *args: call(*args, *side.ins)


def _row_tile(m):
    best = 32
    for t in range(32, 641, 32):
        if m % t == 0:
            best = t
    return best


def _pad_rows(rows, nex, tp):
    m = rows < PAD
    for b in range(1, nex):
        m = m | ((rows >= b * tp) & (rows < b * tp + PAD))
    return m


def _ln_stats(x):
    mu = jnp.mean(x, axis=-1, keepdims=True)
    xc = x - mu
    var = jnp.mean(xc * xc, axis=-1, keepdims=True)
    rstd = lax.rsqrt(var + EPS)
    return xc * rstd, rstd


def _ln_bwd(dy, xhat, rstd, g):
    dxh = dy * g
    m1 = jnp.mean(dxh, axis=-1, keepdims=True)
    m2 = jnp.mean(dxh * xhat, axis=-1, keepdims=True)
    return rstd * (dxh - m1 - xhat * m2)


def _dot(a, b):
    return jnp.dot(a, b, preferred_element_type=F32)


def _dot_nt(a, b):
    return lax.dot_general(a, b, (((1,), (1,)), ((), ())), preferred_element_type=F32)


def _dot_tn(a, b):
    return lax.dot_general(a, b, (((0,), (0,)), ((), ())), preferred_element_type=F32)


def _sigmoid(x):
    return 1.0 / (1.0 + jnp.exp(-x))


def _gelu_parts(g):
    cdf = 0.5 * (1.0 + lax.erf(g * (1.0 / math.sqrt(2.0))))
    pdf = jnp.exp(-0.5 * g * g) * (1.0 / math.sqrt(2.0 * math.pi))
    return g * cdf, cdf + g * pdf


def _bucket_np(d):
    n = np.maximum(d, 0)
    max_exact = NBUCKET // 2
    nf = np.maximum(n, 1).astype(np.float32)
    large = max_exact + (np.log(nf / np.float32(max_exact)) / np.float32(math.log(MAXDIST / max_exact))
                         * np.float32(NBUCKET - max_exact)).astype(np.int32)
    large = np.minimum(large, NBUCKET - 1)
    return np.where(n < max_exact, n, large).astype(np.int32)


def _bias_index():
    i = np.arange(BLK)[:, None]
    j = np.arange(2 * BLK)[None, :]
    d = BLK + i - j
    band_ok = (d >= 0) & (d < BLK)
    band = _bucket_np(d)
    idx = np.full((3, BLK, NKEY), -1, np.int32)
    m = np.arange(N_META)[None, :]
    d0 = (i - PAD) - m
    idx[0, :, 2 * BLK:2 * BLK + N_META] = np.where(d0 >= 0, _bucket_np(d0), -1)
    ok1 = band_ok & (j >= BLK)
    idx[1, :, :2 * BLK] = np.where(ok1, band, -1)
    idx[1, :, 2 * BLK:2 * BLK + N_META] = _bucket_np((N_META + i) - m)
    idx[2, :, :2 * BLK] = np.where(band_ok, band, -1)
    idx[2, :, 2 * BLK:2 * BLK + N_META] = NBUCKET - 1
    return idx


def _bias_build(rel_bias):
    idx = jnp.asarray(_bias_index())

    def body(idx_ref, rb_ref, o_ref):
        ix = idx_ref[...]
        for h in range(NQ):
            acc = jnp.full(ix.shape, NEG, F32)
            for b in range(NBUCKET):
                acc = jnp.where(ix == b, rb_ref[b, h], acc)
            o_ref[:, h, :, :] = acc

    return pl.pallas_call(
        body, name="bias_build", out_shape=jax.ShapeDtypeStruct((3, NQ, BLK, NKEY), F32),
        in_specs=[pl.BlockSpec(memory_space=pltpu.VMEM), pl.BlockSpec(memory_space=pltpu.SMEM)],
        out_specs=pl.BlockSpec(memory_space=pltpu.VMEM))(idx, rel_bias)


def _bias_grad(dbias):
    idx = jnp.asarray(_bias_index())

    def body(idx_ref, d_ref, o_ref):
        d = jnp.sum(d_ref[...], axis=0)
        for b in range(NBUCKET):
            acc = jnp.zeros((NQ, NKEY), F32)
            for case in range(3):
                hit = (idx_ref[case] == b)[None, :, :]
                acc = acc + jnp.sum(jnp.where(hit, d[case], 0.0), axis=1)
            o_ref[b] = jnp.sum(acc, axis=-1, keepdims=True)

    out = pl.pallas_call(
        body, name="bias_grad", out_shape=jax.ShapeDtypeStruct((NBUCKET, NQ, 1), F32),
        in_specs=[pl.BlockSpec(memory_space=pltpu.VMEM), pl.BlockSpec(memory_space=pltpu.VMEM)],
        out_specs=pl.BlockSpec(memory_space=pltpu.VMEM))(idx, dbias)
    return out.reshape(NBUCKET, NQ)


def _embed_ln(x, meta, g, b, nblk):
    nex, seq, _ = x.shape
    m = nex * nblk * BLK

    def body(x_ref, meta_ref, g_ref, b_ref, raw_ref, h_ref, hb_ref):
        j = pl.program_id(1)

        @pl.when(j == 0)
        def _():
            raw_ref[0:PAD, :] = jnp.zeros((PAD, D), F32)
            raw_ref[PAD:BLK, :] = meta_ref[...]

        @pl.when(j > 0)
        def _():
            raw_ref[...] = x_ref[...]

        xhat, _ = _ln_stats(raw_ref[...])
        y = xhat * g_ref[...] + b_ref[...]
        h_ref[...] = y
        hb_ref[...] = y.astype(BF16)

    row = lambda bb, j: (bb * nblk + j, 0)
    return _call(
        body, "embed_ln",
        (jax.ShapeDtypeStruct((m, D), F32), jax.ShapeDtypeStruct((m, D), F32), jax.ShapeDtypeStruct((m, D), BF16)),
        (nex, nblk),
        [pl.BlockSpec((None, BLK, D), lambda bb, j: (bb, jnp.maximum(j - 1, 0), 0)),
         pl.BlockSpec((N_META, D), lambda bb, j: (0, 0)),
         pl.BlockSpec((1, D), lambda bb, j: (0, 0)), pl.BlockSpec((1, D), lambda bb, j: (0, 0))],
        (pl.BlockSpec((BLK, D), row), pl.BlockSpec((BLK, D), row), pl.BlockSpec((BLK, D), row)),
    )(x, meta, g.reshape(1, D), b.reshape(1, D))


def _mm_bias(name, a, w, bias, tn, tm, side=None):
    m, k = a.shape
    n = w.shape[1]

    def body(a_ref, w_ref, b_ref, o_ref):
        o_ref[...] = _dot(a_ref[...], w_ref[...]) + b_ref[...]

    return _call(body, name, jax.ShapeDtypeStruct((m, n), F32), (n // tn, m // tm),
                 [pl.BlockSpec((tm, k), lambda j, i: (i, 0)), pl.BlockSpec((k, tn), lambda j, i: (0, j)),
                  pl.BlockSpec((1, tn), lambda j, i: (0, j))],
                 pl.BlockSpec((tm, tn), lambda j, i: (i, j)), side=side)(a, w, bias)


def _ffn_up(a, w4, tm, nex, tp, side=None):
    m, k = a.shape
    ffs = w4.shape[2]

    def body(a_ref, w_ref, o_ref):
        i = pl.program_id(1)
        rows = i * tm + lax.broadcasted_iota(jnp.int32, (tm, 1), 0)
        acc = _dot(a_ref[...], w_ref[...])
        o_ref[...] = jnp.where(_pad_rows(rows, nex, tp), 0.0, acc)

    return _call(body, "ffn_up", jax.ShapeDtypeStruct((2, m, 2 * ffs), F32), (NCHIP, m // tm),
                 [pl.BlockSpec((tm, k), lambda j, i: (i, 0)), pl.BlockSpec((None, k, ffs), lambda j, i: (j, 0, 0))],
                 pl.BlockSpec((None, tm, ffs), lambda j, i: (j // 2, i, j % 2)), side=side)(a, w4)


def _fill_kv(ks, vs, prev_ref, cur_ref, meta_ref):
    for piece, lo, n in ((prev_ref, 0, BLK), (cur_ref, BLK, BLK), (meta_ref, 2 * BLK, N_META)):
        val = piece[...]
        for hk in range(NKV):
            ks[hk, lo:lo + n, :] = val[:, hk * HD:(hk + 1) * HD].astype(BF16)
            vs[hk, lo:lo + n, :] = val[:, KVW + hk * HD:KVW + (hk + 1) * HD].astype(BF16)
    for hk in range(NKV):
        ks[hk, 2 * BLK + N_META:NKEY, :] = jnp.zeros((BLK - N_META, HD), BF16)
        vs[hk, 2 * BLK + N_META:NKEY, :] = jnp.zeros((BLK - N_META, HD), BF16)


def _softmax_group(q, ks_hk, bias_ref, sink_ref, hk):
    qg = jnp.concatenate([q[:, (hk * GRP + g) * HD:(hk * GRP + g + 1) * HD] for g in range(GRP)], axis=0).astype(BF16)
    s = _dot_nt(qg, ks_hk) * (HD ** -0.5) + bias_ref[hk * GRP:(hk + 1) * GRP].reshape(GRP * BLK, NKEY)
    sink = jnp.concatenate([jnp.full((BLK, 1), sink_ref[0, hk * GRP + g], F32) for g in range(GRP)], axis=0)
    mx = jnp.maximum(jnp.max(s, axis=-1, keepdims=True), sink)
    p = jnp.exp(s - mx)
    es = jnp.exp(sink - mx)
    inv = 1.0 / (jnp.sum(p, axis=-1, keepdims=True) + es)
    return qg, p * inv, es * inv


def _attn_specs(nblk, tp, blk_of):
    qcol, kvcol = (C_QKV) // AW, (C_QKV + AW) // (2 * KVW)
    return [
        pl.BlockSpec((BLK, AW), lambda bb, j: (bb * nblk + blk_of(j), qcol)),
        pl.BlockSpec((BLK, 2 * KVW), lambda bb, j: (bb * nblk + blk_of(j), kvcol)),
        pl.BlockSpec((BLK, 2 * KVW), lambda bb, j: (bb * nblk + jnp.maximum(blk_of(j) - 1, 0), kvcol)),
        pl.BlockSpec((N_META, 2 * KVW), lambda bb, j: (bb * (tp // N_META) + PAD // N_META, kvcol)),
        pl.BlockSpec((None, NQ, BLK, NKEY), lambda bb, j: (jnp.minimum(blk_of(j), 2), 0, 0, 0)),
        pl.BlockSpec(memory_space=pltpu.SMEM),
    ]


def _attn_fwd(z, bias, sinks, nex, nblk, side=None):
    m = z.shape[0]
    tp = nblk * BLK

    def body(q_ref, cur_ref, prev_ref, meta_ref, bias_ref, sink_ref, o_ref, ks, vs, oacc):
        _fill_kv(ks, vs, prev_ref, cur_ref, meta_ref)
        q = q_ref[...]
        for hk in range(NKV):
            _, pn, _ = _softmax_group(q, ks[hk], bias_ref, sink_ref, hk)
            o = _dot(pn.astype(BF16), vs[hk])
            for g in range(GRP):
                h = hk * GRP + g
                oacc[:, h * HD:(h + 1) * HD] = o[g * BLK:(g + 1) * BLK, :]
        o_ref[...] = oacc[...].astype(BF16)

    return _call(body, "attn_fwd", jax.ShapeDtypeStruct((m, AW), BF16), (nex, nblk),
                 _attn_specs(nblk, tp, lambda j: j),
                 pl.BlockSpec((BLK, AW), lambda bb, j: (bb * nblk + j, 0)),
                 scratch=[pltpu.VMEM((NKV, NKEY, HD), BF16), pltpu.VMEM((NKV, NKEY, HD), BF16),
                          pltpu.VMEM((BLK, AW), F32)], side=side)(z, z, z, z, bias, sinks.reshape(1, NQ))


def _cgate(cv, cg, rows, nex, tp):
    return jnp.where(_pad_rows(rows, nex, tp), 0.0, cv * _sigmoid(cg))


def _conv_fwd(z, w, wb, g, b, nex, tp, side=None):
    m = z.shape[0]
    tm = BLK
    sub = tm // CHALO
    cvc, cgc = C_CONV // CW, C_CONV // CW + 1

    def body(cv_ref, cg_ref, cvh_ref, cgh_ref, w_ref, wb_ref, g_ref, b_ref, cc_ref, cs_ref, win):
        i = pl.program_id(0)
        rows = i * tm + lax.broadcasted_iota(jnp.int32, (tm, 1), 0)
        hrows = i * tm - CHALO + lax.broadcasted_iota(jnp.int32, (CHALO, 1), 0)
        win[0:CHALO, :] = _cgate(cvh_ref[...], cgh_ref[...], hrows, nex, tp)
        win[CHALO:CHALO + tm, :] = _cgate(cv_ref[...], cg_ref[...], rows, nex, tp)
        for sb in range(sub):
            acc = jnp.zeros((CHALO, CW), F32) + wb_ref[...]
            for k in range(CTAPS):
                acc = acc + w_ref[k:k + 1, :] * win[sb * CHALO + 2 + k:sb * CHALO + 2 + k + CHALO, :]
            cc_ref[sb * CHALO:(sb + 1) * CHALO, :] = acc
        xhat, _ = _ln_stats(cc_ref[...])
        cl = xhat * g_ref[...] + b_ref[...]
        cs_ref[...] = (cl * _sigmoid(cl)).astype(BF16)

    halo = lambda i: jnp.maximum(i * sub - 1, 0)
    vec = pl.BlockSpec((1, CW), lambda i: (0, 0))
    return _call(body, "conv_fwd", (jax.ShapeDtypeStruct((m, CW), F32), jax.ShapeDtypeStruct((m, CW), BF16)),
                 (m // tm,),
                 [pl.BlockSpec((tm, CW), lambda i: (i, cvc)), pl.BlockSpec((tm, CW), lambda i: (i, cgc)),
                  pl.BlockSpec((CHALO, CW), lambda i: (halo(i), cvc)), pl.BlockSpec((CHALO, CW), lambda i: (halo(i), cgc)),
                  pl.BlockSpec((CTAPS, CW), lambda i: (0, 0)), vec, vec, vec],
                 (pl.BlockSpec((tm, CW), lambda i: (i, 0)), pl.BlockSpec((tm, CW), lambda i: (i, 0))),
                 scratch=[pltpu.VMEM((CHALO + tm, CW), F32)], side=side)(z, z, z, z, w, wb.reshape(1, CW), g.reshape(1, CW), b.reshape(1, CW))


def _mix_fwd(a, cs, wap4, wcp4, z, tm, side=None):
    m = a.shape[0]
    ns = wap4.shape[2]

    def body(a_ref, cs_ref, wa_ref, wc_ref, ga_ref, gc_ref, ya_ref, yc_ref, mix_ref):
        av, cv = a_ref[...], cs_ref[...]
        for j in range(NCHIP):
            ya_ref[:, j * ns:(j + 1) * ns] = _dot(av, wa_ref[j])
            yc_ref[:, j * ns:(j + 1) * ns] = _dot(cv, wc_ref[j])
        mix_ref[...] = (_sigmoid(ga_ref[...]) * ya_ref[...] + _sigmoid(gc_ref[...]) * yc_ref[...]).astype(BF16)

    wspec = pl.BlockSpec((NCHIP, AW, ns), lambda i: (0, 0, 0))
    row = lambda i: (i, 0)
    return _call(body, "mix_fwd",
                 (jax.ShapeDtypeStruct((m, D), F32), jax.ShapeDtypeStruct((m, D), F32), jax.ShapeDtypeStruct((m, D), BF16)),
                 (m // tm,),
                 [pl.BlockSpec((tm, AW), row), pl.BlockSpec((tm, CW), row), wspec, wspec,
                  pl.BlockSpec((tm, D), lambda i: (i, 0)), pl.BlockSpec((tm, D), lambda i: (i, 1))],
                 (pl.BlockSpec((tm, D), row), pl.BlockSpec((tm, D), row), pl.BlockSpec((tm, D), row)), side=side)(a, cs, wap4, wcp4, z, z)


def _mm_res_ln(name, a, w, res, g, b, tm, side=None):
    m, k = a.shape

    def body(a_ref, w_ref, res_ref, g_ref, b_ref, r_ref, h_ref, hb_ref):
        r = ALPHA * res_ref[...] + _dot(a_ref[...], w_ref[...])
        r_ref[...] = r
        xhat, _ = _ln_stats(r)
        y = xhat * g_ref[...] + b_ref[...]
        h_ref[...] = y
        hb_ref[...] = y.astype(BF16)

    row = lambda i: (i, 0)
    vec = pl.BlockSpec((1, D), lambda i: (0, 0))
    return _call(body, name,
                 (jax.ShapeDtypeStruct((m, D), F32), jax.ShapeDtypeStruct((m, D), F32), jax.ShapeDtypeStruct((m, D), BF16)),
                 (m // tm,),
                 [pl.BlockSpec((tm, k), row), pl.BlockSpec((k, D), lambda i: (0, 0)), pl.BlockSpec((tm, D), row), vec, vec],
                 (pl.BlockSpec((tm, D), row), pl.BlockSpec((tm, D), row), pl.BlockSpec((tm, D), row)), side=side)(a, w, res, g.reshape(1, D), b.reshape(1, D))


def _conv3(win, w_ref, b_ref, tm):
    return (b_ref[...] + w_ref[0:1, :] * win[FHALO - 2:FHALO - 2 + tm, :] + w_ref[1:2, :] * win[FHALO - 1:FHALO - 1 + tm, :]
            + w_ref[2:3, :] * win[FHALO:FHALO + tm, :])


def _ffn_specs(tm, ffs, m):
    sub = tm // FHALO
    return [
        pl.BlockSpec((2, tm, ffs), lambda c, i: (0, i, c)),
        pl.BlockSpec((2, FHALO, ffs), lambda c, i: (0, jnp.maximum(i * sub - 1, 0), c)),
        pl.BlockSpec((FTAPS, ffs), lambda c, i: (0, c)), pl.BlockSpec((FTAPS, ffs), lambda c, i: (0, c + 2)),
        pl.BlockSpec((1, ffs), lambda c, i: (0, c)), pl.BlockSpec((1, ffs), lambda c, i: (0, c + 2)),
    ]


def _ffn_act(up3, w, wb, tm, side=None):
    _, m, dff = up3.shape
    ffs = dff // 2

    def body(x_ref, xh_ref, wu_ref, wg_ref, bu_ref, bg_ref, o_ref, win):
        win[:, 0:FHALO, :] = xh_ref[...]
        win[:, FHALO:FHALO + tm, :] = x_ref[...]
        u = _conv3(win.at[0], wu_ref, bu_ref, tm)
        g = _conv3(win.at[1], wg_ref, bg_ref, tm)
        o_ref[...] = (_gelu_parts(g)[0] * u).astype(BF16)

    return _call(body, "ffn_act", jax.ShapeDtypeStruct((m, dff), BF16), (2, m // tm),
                 _ffn_specs(tm, ffs, m), pl.BlockSpec((tm, ffs), lambda c, i: (i, c)),
                 scratch=[pltpu.VMEM((2, FHALO + tm, ffs), F32)], side=side)(up3, up3, w, w, wb, wb)


def _loss_grad(y, target, nblk):
    nex = target.shape[0]
    m = y.shape[0]

    def body(y_ref, t_ref, dy_ref, acc_ref):
        bb, j = pl.program_id(0), pl.program_id(1)

        @pl.when((bb == 0) & (j == 0))
        def _():
            acc_ref[...] = jnp.zeros_like(acc_ref)

        @pl.when(j == 0)
        def _():
            dy_ref[...] = jnp.zeros_like(dy_ref)

        @pl.when(j > 0)
        def _():
            e = y_ref[...] - t_ref[...]
            dy_ref[...] = e * (1.0 / D)
            acc_ref[...] += jnp.sum((e * e).reshape(BLK // 8, 8, D), axis=0)

    return _call(body, "loss_grad", (jax.ShapeDtypeStruct((m, D), F32), jax.ShapeDtypeStruct((8, D), F32)), (nex, nblk),
                 [pl.BlockSpec((BLK, D), lambda bb, j: (bb * nblk + j, 0)),
                  pl.BlockSpec((None, BLK, D), lambda bb, j: (bb, jnp.maximum(j - 1, 0), 0))],
                 (pl.BlockSpec((BLK, D), lambda bb, j: (bb * nblk + j, 0)), pl.BlockSpec((8, D), lambda bb, j: (0, 0))))(y, target)


def _ln_bwd_call(name, dy, r, g, tm, a_list=(), w=None, cols=None, side=None):
    m = dy.shape[0]
    na = len(a_list)

    def body(*refs):
        dy_ref, r_ref, g_ref = refs[0:3]
        a_refs = refs[3:3 + na]
        w_ref = refs[3 + na] if na else None
        dr_ref, drb_ref, dg_ref, db_ref = refs[-4:]
        i = pl.program_id(0)
        dh = dy_ref[...]
        if na:
            dh = ALPHA * dh
            if cols is None:
                ns = w.shape[2]
                for j in range(NCHIP):
                    dh = dh + _dot_nt(a_refs[0][j // 2, :, (j % 2) * ns:(j % 2 + 1) * ns], w_ref[j])
            else:
                for a_ref, (c0, c1) in zip(a_refs, cols):
                    dh = dh + _dot_nt(a_ref[...], w_ref[:, c0:c1])
        xhat, rstd = _ln_stats(r_ref[...])
        dr = _ln_bwd(dh, xhat, rstd, g_ref[...])
        dr_ref[...] = dr
        drb_ref[...] = dr.astype(BF16)

        @pl.when(i == 0)
        def _():
            dg_ref[...] = jnp.zeros_like(dg_ref)
            db_ref[...] = jnp.zeros_like(db_ref)

        dg_ref[...] += jnp.sum(dh * xhat, axis=0, keepdims=True)
        db_ref[...] += jnp.sum(dh, axis=0, keepdims=True)

    row = lambda i: (i, 0)
    vec = pl.BlockSpec((1, D), lambda i: (0, 0))
    in_specs = [pl.BlockSpec((tm, D), row), pl.BlockSpec((tm, D), row), vec]
    for a in a_list:
        in_specs.append(pl.BlockSpec((2, tm, a.shape[2]), lambda i: (0, i, 0)) if a.ndim == 3 else pl.BlockSpec((tm, a.shape[1]), row))
    if na:
        in_specs.append(pl.BlockSpec(w.shape, (lambda i: (0, 0, 0)) if w.ndim == 3 else (lambda i: (0, 0))))
    return _call(body, name,
                 (jax.ShapeDtypeStruct((m, D), F32), jax.ShapeDtypeStruct((m, D), BF16),
                  jax.ShapeDtypeStruct((1, D), F32), jax.ShapeDtypeStruct((1, D), F32)),
                 (m // tm,), in_specs,
                 (pl.BlockSpec((tm, D), row), pl.BlockSpec((tm, D), row), vec, vec),
                 side=side)(dy, r, g.reshape(1, D), *a_list, *([w] if na else []))


def _ffn_act_bwd(drb, wdown, up3, w, wb, tm, side=None):
    _, m, dff = up3.shape
    ffs = dff // 2

    def body(dr_ref, wd_ref, x_ref, xh_ref, wu_ref, wg_ref, bu_ref, bg_ref, o_ref, win):
        win[:, 0:FHALO, :] = xh_ref[...]
        win[:, FHALO:FHALO + tm, :] = x_ref[...]
        u = _conv3(win.at[0], wu_ref, bu_ref, tm)
        g = _conv3(win.at[1], wg_ref, bg_ref, tm)
        dact = _dot_nt(dr_ref[...], wd_ref[...])
        gel, dgel = _gelu_parts(g)
        o_ref[0] = dact * gel
        o_ref[1] = dact * u * dgel

    specs = [pl.BlockSpec((tm, D), lambda c, i: (i, 0)), pl.BlockSpec((ffs, D), lambda c, i: (c, 0))] + _ffn_specs(tm, ffs, m)
    return _call(body, "ffn_act_bwd", jax.ShapeDtypeStruct((2, m, dff), F32), (2, m // tm), specs,
                 pl.BlockSpec((2, tm, ffs), lambda c, i: (0, i, c)),
                 scratch=[pltpu.VMEM((2, FHALO + tm, ffs), F32)], side=side)(drb, wdown, up3, up3, w, w, wb, wb)


def _ffn_conv_bwd(dup3, up3, w, tm, nex, tp, side=None):
    _, m, dff = up3.shape
    ffs = dff // 2
    sub = tm // FHALO
    nt = m // tm

    def body(d_ref, dh_ref, x_ref, xh_ref, wu_ref, wg_ref, o_ref, dw_ref, db_ref, dwin, xwin):
        i = pl.program_id(1)
        dwin[:, 0:tm, :] = d_ref[...]
        dwin[:, tm:tm + FHALO, :] = jnp.where(i == nt - 1, 0.0, dh_ref[...])
        xwin[:, 0:FHALO, :] = xh_ref[...]
        xwin[:, FHALO:FHALO + tm, :] = x_ref[...]
        rows = i * tm + lax.broadcasted_iota(jnp.int32, (tm, 1), 0)
        pad = _pad_rows(rows, nex, tp)

        @pl.when(i == 0)
        def _():
            dw_ref[...] = jnp.zeros_like(dw_ref)
            db_ref[...] = jnp.zeros_like(db_ref)

        for p, w_ref in ((0, wu_ref), (1, wg_ref)):
            d0 = dwin[p, 0:tm, :]
            dpre = w_ref[2:3, :] * d0 + w_ref[1:2, :] * dwin[p, 1:1 + tm, :] + w_ref[0:1, :] * dwin[p, 2:2 + tm, :]
            o_ref[p] = jnp.where(pad, 0.0, dpre).astype(BF16)
            for k in range(FTAPS):
                dw_ref[p, k:k + 1, :] += jnp.sum(d0 * xwin[p, FHALO - 2 + k:FHALO - 2 + k + tm, :], axis=0, keepdims=True)
            db_ref[p] += jnp.sum(d0, axis=0, keepdims=True)

    nxt = lambda i: jnp.minimum((i + 1) * sub, m // FHALO - 1)
    prv = lambda i: jnp.maximum(i * sub - 1, 0)
    return _call(body, "ffn_conv_bwd",
                 (jax.ShapeDtypeStruct((2, m, dff), BF16), jax.ShapeDtypeStruct((2, FTAPS, dff), F32),
                  jax.ShapeDtypeStruct((2, 1, dff), F32)),
                 (2, nt),
                 [pl.BlockSpec((2, tm, ffs), lambda c, i: (0, i, c)), pl.BlockSpec((2, FHALO, ffs), lambda c, i: (0, nxt(i), c)),
                  pl.BlockSpec((2, tm, ffs), lambda c, i: (0, i, c)), pl.BlockSpec((2, FHALO, ffs), lambda c, i: (0, prv(i), c)),
                  pl.BlockSpec((FTAPS, ffs), lambda c, i: (0, c)), pl.BlockSpec((FTAPS, ffs), lambda c, i: (0, c + 2))],
                 (pl.BlockSpec((2, tm, ffs), lambda c, i: (0, i, c)), pl.BlockSpec((2, FTAPS, ffs), lambda c, i: (0, 0, c)),
                  pl.BlockSpec((2, 1, ffs), lambda c, i: (0, 0, c))),
                 scratch=[pltpu.VMEM((2, tm + FHALO, ffs), F32), pltpu.VMEM((2, FHALO + tm, ffs), F32)], side=side)(dup3, dup3, up3, up3, w, w)


def _mm_tn(name, a, b, tk, tn, b_cols=None, chip_out=False):
    m, k = a.shape
    n = b.shape[-1] * (2 if b.ndim == 3 else 1)

    def body(a_ref, b_ref, o_ref):
        o_ref[...] = _dot_tn(a_ref[...], b_ref[...])

    if b.ndim == 3:
        bspec = pl.BlockSpec((None, m, tn), lambda kk, j: (b_cols(j)[0], 0, b_cols(j)[1]))
    else:
        bspec = pl.BlockSpec((m, tn), lambda kk, j: (0, j))
    if chip_out:
        oshape, ospec = (n // tn, k, tn), pl.BlockSpec((None, tk, tn), lambda kk, j: (j, kk, 0))
    else:
        oshape, ospec = (k, n), pl.BlockSpec((tk, tn), lambda kk, j: (kk, j))
    return _call(body, name, jax.ShapeDtypeStruct(oshape, F32), (k // tk, n // tn),
                 [pl.BlockSpec((m, tk), lambda kk, j: (0, kk)), bspec], ospec)(a, b)


def _gate_bwd(drb, wout, ya, yc, z, tm):
    m = drb.shape[0]

    def body(dr_ref, w_ref, ya_ref, yc_ref, ga_ref, gc_ref, dya_ref, dyc_ref, dz_ref, cs_ref):
        i = pl.program_id(0)
        dmix = _dot_nt(dr_ref[...], w_ref[...])
        sa, sc = _sigmoid(ga_ref[...]), _sigmoid(gc_ref[...])
        dya_ref[...] = (dmix * sa).astype(BF16)
        dyc_ref[...] = (dmix * sc).astype(BF16)
        dga = dmix * ya_ref[...] * sa * (1.0 - sa)
        dgc = dmix * yc_ref[...] * sc * (1.0 - sc)
        dz_ref[:, 0:D] = dga.astype(BF16)
        dz_ref[:, D:2 * D] = dgc.astype(BF16)

        @pl.when(i == 0)
        def _():
            cs_ref[...] = jnp.zeros_like(cs_ref)

        cs_ref[:, 0:D] += jnp.sum(dga, axis=0, keepdims=True)
        cs_ref[:, D:2 * D] += jnp.sum(dgc, axis=0, keepdims=True)

    row = lambda i: (i, 0)
    return _call(body, "gate_bwd",
                 (jax.ShapeDtypeStruct((m, D), BF16), jax.ShapeDtypeStruct((m, D), BF16),
                  jax.ShapeDtypeStruct((m, 2 * D), BF16), jax.ShapeDtypeStruct((1, 2 * D), F32)),
                 (m // tm,),
                 [pl.BlockSpec((tm, D), row), pl.BlockSpec((D, D), lambda i: (0, 0)), pl.BlockSpec((tm, D), row),
                  pl.BlockSpec((tm, D), row), pl.BlockSpec((tm, D), lambda i: (i, 0)), pl.BlockSpec((tm, D), lambda i: (i, 1))],
                 (pl.BlockSpec((tm, D), row), pl.BlockSpec((tm, D), row), pl.BlockSpec((tm, 2 * D), row),
                  pl.BlockSpec((1, 2 * D), lambda i: (0, 0))))(drb, wout, ya, yc, z, z)


def _conv_bwd_a(dyc, wcp4, cc, g, b, tm):
    m = cc.shape[0]
    ns = wcp4.shape[2]

    def body(dy_ref, w_ref, cc_ref, g_ref, b_ref, dcc_ref, dg_ref, db_ref, dwb_ref):
        i = pl.program_id(0)
        dcs = jnp.zeros((tm, CW), F32)
        for j in range(NCHIP):
            dcs = dcs + _dot_nt(dy_ref[:, j * ns:(j + 1) * ns], w_ref[j])
        xhat, rstd = _ln_stats(cc_ref[...])
        cl = xhat * g_ref[...] + b_ref[...]
        sg = _sigmoid(cl)
        dcl = dcs * sg * (1.0 + cl * (1.0 - sg))
        dcc = _ln_bwd(dcl, xhat, rstd, g_ref[...])
        dcc_ref[...] = dcc

        @pl.when(i == 0)
        def _():
            dg_ref[...] = jnp.zeros_like(dg_ref)
            db_ref[...] = jnp.zeros_like(db_ref)
            dwb_ref[...] = jnp.zeros_like(dwb_ref)

        dg_ref[...] += jnp.sum(dcl * xhat, axis=0, keepdims=True)
        db_ref[...] += jnp.sum(dcl, axis=0, keepdims=True)
        dwb_ref[...] += jnp.sum(dcc, axis=0, keepdims=True)

    row = lambda i: (i, 0)
    vec = pl.BlockSpec((1, CW), lambda i: (0, 0))
    v = jax.ShapeDtypeStruct((1, CW), F32)
    return _call(body, "conv_bwd_a", (jax.ShapeDtypeStruct((m, CW), F32), v, v, v), (m // tm,),
                 [pl.BlockSpec((tm, D), row), pl.BlockSpec((NCHIP, CW, ns), lambda i: (0, 0, 0)), pl.BlockSpec((tm, CW), row), vec, vec],
                 (pl.BlockSpec((tm, CW), row), vec, vec, vec))(dyc, wcp4, cc, g.reshape(1, CW), b.reshape(1, CW))


def _conv_bwd_b(dcc, z, w, nex, tp, side=None):
    m = dcc.shape[0]
    tm = BLK
    sub = tm // CHALO
    nt = m // tm
    cvc, cgc = C_CONV // CW, C_CONV // CW + 1

    def body(d_ref, dh_ref, cv_ref, cg_ref, cvh_ref, cgh_ref, w_ref, dz_ref, dw_ref, cs_ref, dwin, cwin):
        i = pl.program_id(0)
        rows = i * tm + lax.broadcasted_iota(jnp.int32, (tm, 1), 0)
        hrows = i * tm - CHALO + lax.broadcasted_iota(jnp.int32, (CHALO, 1), 0)
        dwin[0:tm, :] = d_ref[...]
        dwin[tm:tm + CHALO, :] = jnp.where(i == nt - 1, 0.0, dh_ref[...])
        cwin[0:CHALO, :] = _cgate(cvh_ref[...], cgh_ref[...], hrows, nex, tp)
        cwin[CHALO:CHALO + tm, :] = _cgate(cv_ref[...], cg_ref[...], rows, nex, tp)

        @pl.when(i == 0)
        def _():
            dw_ref[...] = jnp.zeros_like(dw_ref)
            cs_ref[...] = jnp.zeros_like(cs_ref)

        for sb in range(sub):
            lo = sb * CHALO
            acc = jnp.zeros((CHALO, CW), F32)
            for k in range(CTAPS):
                acc = acc + w_ref[k:k + 1, :] * dwin[lo + CTAPS - 1 - k:lo + CTAPS - 1 - k + CHALO, :]
            dsb = dwin[lo:lo + CHALO, :]
            for k in range(CTAPS):
                dw_ref[k:k + 1, :] += jnp.sum(dsb * cwin[lo + 2 + k:lo + 2 + k + CHALO, :], axis=0, keepdims=True)
            srows = rows[lo:lo + CHALO]
            dcg = jnp.where(_pad_rows(srows, nex, tp), 0.0, acc)
            cv = cv_ref[lo:lo + CHALO, :]
            sg = _sigmoid(cg_ref[lo:lo + CHALO, :])
            dcv = dcg * sg
            dgt = dcg * cv * sg * (1.0 - sg)
            dz_ref[lo:lo + CHALO, 0:CW] = dcv.astype(BF16)
            dz_ref[lo:lo + CHALO, CW:2 * CW] = dgt.astype(BF16)
            cs_ref[:, 0:CW] += jnp.sum(dcv, axis=0, keepdims=True)
            cs_ref[:, CW:2 * CW] += jnp.sum(dgt, axis=0, keepdims=True)

    nxt = lambda i: jnp.minimum((i + 1) * sub, m // CHALO - 1)
    prv = lambda i: jnp.maximum(i * sub - 1, 0)
    return _call(body, "conv_bwd_b",
                 (jax.ShapeDtypeStruct((m, 2 * CW), BF16), jax.ShapeDtypeStruct((CHALO, CW), F32),
                  jax.ShapeDtypeStruct((1, 2 * CW), F32)),
                 (nt,),
                 [pl.BlockSpec((tm, CW), lambda i: (i, 0)), pl.BlockSpec((CHALO, CW), lambda i: (nxt(i), 0)),
                  pl.BlockSpec((tm, CW), lambda i: (i, cvc)), pl.BlockSpec((tm, CW), lambda i: (i, cgc)),
                  pl.BlockSpec((CHALO, CW), lambda i: (prv(i), cvc)), pl.BlockSpec((CHALO, CW), lambda i: (prv(i), cgc)),
                  pl.BlockSpec((CTAPS, CW), lambda i: (0, 0))],
                 (pl.BlockSpec((tm, 2 * CW), lambda i: (i, 0)), pl.BlockSpec((CHALO, CW), lambda i: (0, 0)),
                  pl.BlockSpec((1, 2 * CW), lambda i: (0, 0))),
                 scratch=[pltpu.VMEM((tm + CHALO, CW), F32), pltpu.VMEM((CHALO + tm, CW), F32)], side=side)(dcc, dcc, z, z, z, z, w)


def _attn_bwd(z, bias, sinks, dya, wap4, nex, nblk, side=None):
    m = z.shape[0]
    tp = nblk * BLK
    ns = wap4.shape[2]
    blk_of = lambda j: nblk - 1 - j

    def body(q_ref, cur_ref, prev_ref, meta_ref, bias_ref, sink_ref, dy_ref, w_ref,
             dz_ref, cs_ref, dsk_ref, dbias_ref, ks, vs, carry, macc, dqacc, dkv, okv):
        bb, j = pl.program_id(0), pl.program_id(1)
        n = nblk - 1 - j
        _fill_kv(ks, vs, prev_ref, cur_ref, meta_ref)
        q = q_ref[...]
        da = jnp.zeros((BLK, AW), F32)
        for jj in range(NCHIP):
            da = da + _dot_nt(dy_ref[:, jj * ns:(jj + 1) * ns], w_ref[jj])

        @pl.when(j == 0)
        def _():
            carry[...] = jnp.zeros_like(carry)
            macc[...] = jnp.zeros_like(macc)

        @pl.when((bb == 0) & (j == 0))
        def _():
            cs_ref[...] = jnp.zeros_like(cs_ref)
            dsk_ref[...] = jnp.zeros_like(dsk_ref)

        @pl.when((j == 0) | (n <= 1))
        def _():
            dbias_ref[...] = jnp.zeros_like(dbias_ref)

        lane = lax.broadcasted_iota(jnp.int32, (1, BLK), 1)
        dsk = jnp.zeros((1, BLK), F32)
        for hk in range(NKV):
            qg, pn, psink = _softmax_group(q, ks[hk], bias_ref, sink_ref, hk)
            dog = jnp.concatenate([da[:, (hk * GRP + g) * HD:(hk * GRP + g + 1) * HD] for g in range(GRP)], axis=0).astype(BF16)
            dp = _dot_nt(dog, vs[hk])
            dl = jnp.sum(pn * dp, axis=-1, keepdims=True)
            ds = pn * (dp - dl)
            dbias_ref[hk * GRP:(hk + 1) * GRP] += ds.reshape(GRP, BLK, NKEY)
            dsr = -psink * dl
            for g in range(GRP):
                dsk = dsk + jnp.where(lane == hk * GRP + g, jnp.sum(dsr[g * BLK:(g + 1) * BLK]), 0.0)
            dsb = (ds * (HD ** -0.5)).astype(BF16)
            dqg = _dot(dsb, ks[hk])
            for g in range(GRP):
                h = hk * GRP + g
                dqacc[:, h * HD:(h + 1) * HD] = dqg[g * BLK:(g + 1) * BLK, :]
            dkv[:, hk * HD:(hk + 1) * HD] = _dot_tn(dsb, qg)
            dkv[:, KVW + hk * HD:KVW + (hk + 1) * HD] = _dot_tn(pn.astype(BF16), dog)
        dsk_ref[...] += dsk
        macc[...] += dkv[2 * BLK:2 * BLK + N_META, :]
        okv[...] = dkv[BLK:2 * BLK, :] + carry[...]
        carry[...] = dkv[0:BLK, :]

        @pl.when(n == 0)
        def _():
            okv[PAD:BLK, :] += macc[...]

        dq = dqacc[...]
        ok = okv[...]
        dz_ref[:, 0:AW] = dq.astype(BF16)
        dz_ref[:, AW:AW + 2 * KVW] = ok.astype(BF16)
        cs_ref[:, 0:AW] += jnp.sum(dq, axis=0, keepdims=True)
        cs_ref[:, AW:AW + 2 * KVW] += jnp.sum(ok, axis=0, keepdims=True)

    wz = AW + 2 * KVW
    specs = _attn_specs(nblk, tp, blk_of) + [
        pl.BlockSpec((BLK, D), lambda bb, j: (bb * nblk + blk_of(j), 0)),
        pl.BlockSpec((NCHIP, AW, ns), lambda bb, j: (0, 0, 0))]
    return _call(body, "attn_bwd",
                 (jax.ShapeDtypeStruct((m, wz), BF16), jax.ShapeDtypeStruct((1, wz), F32), jax.ShapeDtypeStruct((1, BLK), F32),
                  jax.ShapeDtypeStruct((nex, 3, NQ, BLK, NKEY), F32)),
                 (nex, nblk), specs,
                 (pl.BlockSpec((BLK, wz), lambda bb, j: (bb * nblk + blk_of(j), 0)), pl.BlockSpec((1, wz), lambda bb, j: (0, 0)),
                  pl.BlockSpec((1, BLK), lambda bb, j: (0, 0)),
                  pl.BlockSpec((None, None, NQ, BLK, NKEY), lambda bb, j: (bb, jnp.minimum(blk_of(j), 2), 0, 0, 0))),
                 scratch=[pltpu.VMEM((NKV, NKEY, HD), BF16), pltpu.VMEM((NKV, NKEY, HD), BF16),
                          pltpu.VMEM((BLK, 2 * KVW), F32), pltpu.VMEM((N_META, 2 * KVW), F32),
                          pltpu.VMEM((BLK, AW), F32), pltpu.VMEM((NKEY, 2 * KVW), F32), pltpu.VMEM((BLK, 2 * KVW), F32)],
                 side=side)(z, z, z, z, bias, sinks.reshape(1, NQ), dya, wap4)


def _tile_rows(rows, cols, target_bytes=1 << 20):
    best = None
    for t in range(8, rows + 1, 8):
        if rows % t == 0 and t * cols * 4 <= target_bytes:
            best = t
    return best or rows


def _sum0(name, x):
    n, r, c = x.shape
    tr = _tile_rows(r, c * n)

    def body(x_ref, o_ref):
        acc = x_ref[0]
        for k in range(1, n):
            acc = acc + x_ref[k]
        o_ref[...] = acc

    return _call(body, name, jax.ShapeDtypeStruct((r, c), F32), (r // tr,),
                 [pl.BlockSpec((n, tr, c), lambda i: (0, i, 0))], pl.BlockSpec((tr, c), lambda i: (i, 0)))(x)


def _adamw(name, w, g, mom, vel):
    r, c = w.shape
    tr = _tile_rows(r, c)
    c1 = 1.0 / (1.0 - ADAM_B1 ** ADAM_STEP)
    c2 = 1.0 / (1.0 - ADAM_B2 ** ADAM_STEP)

    def body(w_ref, g_ref, m_ref, v_ref, d_ref, mo_ref, vo_ref):
        gg = g_ref[...]
        mn = ADAM_B1 * m_ref[...] + (1.0 - ADAM_B1) * gg
        vn = ADAM_B2 * v_ref[...] + (1.0 - ADAM_B2) * (gg * gg)
        mo_ref[...] = mn
        vo_ref[...] = vn
        d_ref[...] = -ADAM_LR * ((mn * c1) / (jnp.sqrt(vn * c2) + ADAM_EPS) + ADAM_WD * w_ref[...])

    spec = pl.BlockSpec((tr, c), lambda i: (i, 0))
    o = jax.ShapeDtypeStruct((r, c), F32)
    return _call(body, name, (o, o, o), (r // tr,), [spec] * 4, (spec, spec, spec))(w, g, mom, vel)


def _place():
    x, y, c = lax.axis_index("x"), lax.axis_index("y"), lax.axis_index("c")
    others = [(1 - x, y), (x, 1 - y), (1 - x, 1 - y)]
    return x, y, c, others


def _gather_job(items):
    nw = len(items)

    def views(s_ref, g_ref, layer, c):
        if layer is None:
            return s_ref.at[c], lambda chip, cc: g_ref.at[chip, cc]
        hr = s_ref.shape[1] // 2
        return s_ref.at[layer, pl.ds(c * hr, hr)], lambda chip, cc: g_ref.at[chip, pl.ds(cc * hr, hr)]

    def copies(s_refs, g_refs, send, recv):
        x, y, c, others = _place()
        chip = 2 * x + y
        firsts, arrive, passed, arrive2 = [], [], [], []
        for w, (_, layer) in enumerate(items):
            src, dst = views(s_refs[w], g_refs[w], layer, c)
            for k, (px, py) in enumerate(others):
                def rc(kk, s, d, to, w=w):
                    return pltpu.make_async_remote_copy(src_ref=s, dst_ref=d, send_sem=send.at[w, kk], recv_sem=recv.at[w, kk],
                                                        device_id=to, device_id_type=MESH)
                got, got2 = dst(2 * px + py, c), dst(2 * px + py, 1 - c)
                firsts.append(rc(k, src, dst(chip, c), (px, py, c)))
                arrive.append(rc(k, got, got, (x, y, c)))
                passed.append(rc(3 + k, got, got, (x, y, 1 - c)))
                arrive2.append(rc(3 + k, got2, got2, (x, y, c)))
        return firsts, arrive, passed, arrive2

    def start(s_refs, g_refs, send, recv):
        for cp in copies(s_refs, g_refs, send, recv)[0]:
            cp.start()

    def finish(s_refs, g_refs, send, recv):
        firsts, arrive, passed, arrive2 = copies(s_refs, g_refs, send, recv)
        for a, p in zip(arrive, passed):
            a.wait_recv()
            p.start()
        for a in arrive2:
            a.wait_recv()
        for cp in firsts + passed:
            cp.wait_send()

    outs = [jax.ShapeDtypeStruct((NCHIP,) + (s.shape if layer is None else s.shape[1:]), s.dtype) for s, layer in items]
    return _Job([s for s, _ in items], outs, (nw, 6), start, finish)


def _swap_job(grads):
    def copies(d_refs, a_refs, send, recv):
        x, y, c, _ = _place()
        cps = []
        for w in range(len(grads)):
            h = d_refs[w].shape[1] // 2
            cps.append(pltpu.make_async_remote_copy(
                src_ref=d_refs[w].at[:, pl.ds((1 - c) * h, h), :], dst_ref=a_refs[w], send_sem=send.at[w], recv_sem=recv.at[w],
                device_id=(x, y, 1 - c), device_id_type=MESH))
        return cps

    def start(*r):
        for cp in copies(*r):
            cp.start()

    def finish(*r):
        for cp in copies(*r):
            cp.wait()

    outs = [jax.ShapeDtypeStruct((NCHIP, g.shape[1] // 2, g.shape[2]), g.dtype) for g in grads]
    return _Job(list(grads), outs, (len(grads),), start, finish)


def _exchange_job(parts):
    def copies(q_refs, b_refs, send, recv):
        x, y, c, others = _place()
        cps = []
        for w in range(len(parts)):
            for k, (px, py) in enumerate(others):
                cps.append(pltpu.make_async_remote_copy(
                    src_ref=q_refs[w].at[2 * px + py], dst_ref=b_refs[w].at[k], send_sem=send.at[w, k], recv_sem=recv.at[w, k],
                    device_id=(px, py, c), device_id_type=MESH))
        return cps

    def start(*r):
        for cp in copies(*r):
            cp.start()

    def finish(*r):
        for cp in copies(*r):
            cp.wait()

    outs = [jax.ShapeDtypeStruct((3,) + p.shape[1:], p.dtype) for p in parts]
    return _Job(list(parts), outs, (len(parts), 3), start, finish)


def _run_job(name, job):
    n_in, n_out = len(job.ins), len(job.outs)

    def body(*refs):
        ins, outs = refs[:n_in], refs[n_in:n_in + n_out]
        send, recv = refs[n_in + n_out:]
        job.start(ins, outs, send, recv)
        job.finish(ins, outs, send, recv)

    return pl.pallas_call(
        body, name=name, out_shape=tuple(job.outs), in_specs=[_ANY] * n_in, out_specs=tuple([_ANY] * n_out),
        scratch_shapes=[pltpu.SemaphoreType.DMA(job.sems), pltpu.SemaphoreType.DMA(job.sems)])(*job.ins)


def _sibling_join(halves):
    nw = len(halves)

    def body(*refs):
        h_refs = refs[:2 * nw]
        f_refs = refs[2 * nw:3 * nw]
        send, recv = refs[3 * nw:]
        x, y, c, _ = _place()
        cps = []
        for w in range(nw):
            for l in range(2):
                src = h_refs[2 * w + l]
                h = src.shape[0]
                dst = f_refs[w].at[l, pl.ds(c * h, h), :]
                cp = pltpu.make_async_remote_copy(src_ref=src, dst_ref=dst, send_sem=send.at[w, l], recv_sem=recv.at[w, l],
                                                  device_id=(x, y, 1 - c), device_id_type=MESH)
                cp.start()
                cps.append(cp)
        for w in range(nw):
            for l in range(2):
                src = h_refs[2 * w + l]
                h = src.shape[0]
                other = f_refs[w].at[l, pl.ds((1 - c) * h, h), :]
                pltpu.make_async_remote_copy(src_ref=src, dst_ref=other, send_sem=send.at[w, l], recv_sem=recv.at[w, l],
                                             device_id=(x, y, c), device_id_type=MESH).wait_recv()
        for cp in cps:
            cp.wait_send()

    flat = [a for pair in halves for a in pair]
    outs = tuple(jax.ShapeDtypeStruct((2, 2 * pair[0].shape[0], pair[0].shape[1]), F32) for pair in halves)
    return pl.pallas_call(
        body, name="grad_sibling_join", out_shape=outs, in_specs=[_ANY] * (2 * nw), out_specs=tuple([_ANY] * nw),
        scratch_shapes=[pltpu.SemaphoreType.DMA((nw, 2)), pltpu.SemaphoreType.DMA((nw, 2))])(*flat)


def _allgather_small(v):
    r = v.shape[0]

    def body(x_ref, out_ref, send_sems, recv_sems, local_sem):
        x, y, c, chips = _place()
        me, sibling = (x, y, c), (x, y, 1 - c)

        def slab(px, py, pc):
            return out_ref.at[4 * px + 2 * py + pc]

        def copy(k, block, to, src=None):
            return pltpu.make_async_remote_copy(src_ref=slab(*block) if src is None else src, dst_ref=slab(*block),
                                                send_sem=send_sems.at[k], recv_sem=recv_sems.at[k],
                                                device_id=to, device_id_type=MESH)

        mine = pltpu.make_async_copy(x_ref, slab(*me), local_sem)
        mine.start()
        first = [copy(0, me, sibling, src=x_ref)]
        first += [copy(1 + j, me, (*chip, c), src=x_ref) for j, chip in enumerate(chips)]
        for cp in first:
            cp.start()
        passed = [copy(4 + j, (*chip, c), sibling) for j, chip in enumerate(chips)]
        for j, chip in enumerate(chips):
            copy(1 + j, (*chip, c), me).wait_recv()
            passed[j].start()
        copy(0, sibling, me).wait_recv()
        for j, chip in enumerate(chips):
            copy(4 + j, (*chip, 1 - c), me).wait_recv()
        for cp in first + passed:
            cp.wait_send()
        mine.wait()

    return pl.pallas_call(
        body, name="allgather_small", out_shape=jax.ShapeDtypeStruct((8, r, 128), F32),
        in_specs=[pl.BlockSpec(memory_space=pltpu.VMEM)], out_specs=pl.BlockSpec(memory_space=pltpu.VMEM),
        scratch_shapes=[pltpu.SemaphoreType.DMA((7,)), pltpu.SemaphoreType.DMA((7,)), pltpu.SemaphoreType.DMA],
    )(v)


def _add_half(name, d, a, c):
    _, h, cols = a.shape

    def body(c_ref, d_ref, a_ref, o_ref):
        o_ref[...] = (d_ref[...] + a_ref[...]).astype(BF16)

    return _call(body, name, jax.ShapeDtypeStruct(a.shape, BF16), (NCHIP,),
                 [pl.BlockSpec((None, h, cols), lambda p, cr: (p, cr[0], 0)), pl.BlockSpec((None, h, cols), lambda p, cr: (p, 0, 0))],
                 pl.BlockSpec((None, h, cols), lambda p, cr: (p, 0, 0)), prefetch=1)(c, d, a)


def _add_chips(name, d, a, b, where):
    _, h, cols = a.shape
    th = h // 4 if (h % 64 == 0) else h
    nt = h // th

    def body(w_ref, d_ref, a_ref, b_ref, o_ref):
        own = d_ref[...] + a_ref[...]
        o_ref[...] = ((own + b_ref[0].astype(F32)) + b_ref[1].astype(F32)) + b_ref[2].astype(F32)

    return _call(body, name, jax.ShapeDtypeStruct((h, cols), F32), (nt,),
                 [pl.BlockSpec((None, th, cols), lambda i, wr: (wr[0], wr[1] * nt + i, 0)),
                  pl.BlockSpec((None, th, cols), lambda i, wr: (wr[0], i, 0)),
                  pl.BlockSpec((3, th, cols), lambda i, wr: (0, i, 0))],
                 pl.BlockSpec((th, cols), lambda i, wr: (i, 0)), prefetch=1)(where, d, a, b)


def _pack(arrs):
    pieces = []
    for a in arrs:
        f = a.reshape(-1)
        n = -(-f.shape[0] // 1024) * 1024
        pieces.append(jnp.pad(f, (0, n - f.shape[0])).reshape(-1, 128))
    return jnp.concatenate(pieces, axis=0)


def _unpack(buf, shapes):
    out, r = [], 0
    for s in shapes:
        n = int(np.prod(s))
        rows = -(-n // 1024) * 8
        out.append(buf[r:r + rows].reshape(-1)[:n].reshape(s))
        r += rows
    return out


def kernel(x, meta_tokens, in_ln_g, in_ln_b, rel_bias, w_in, b_in, attn_sinks, w_attn_proj, conv_dw, conv_dw_b, conv_ln_g, conv_ln_b, w_conv_proj, w_out, ln1_g, ln1_b, ffn_w_up, ffn_dw, ffn_dw_b, ffn_w_down, ln2_g, ln2_b, loss_target, m_meta_tokens, m_in_ln_g, m_in_ln_b, m_rel_bias, m_w_in, m_b_in, m_attn_sinks, m_w_attn_proj, m_conv_dw, m_conv_dw_b, m_conv_ln_g, m_conv_ln_b, m_w_conv_proj, m_w_out, m_ln1_g, m_ln1_b, m_ffn_w_up, m_ffn_dw, m_ffn_dw_b, m_ffn_w_down, m_ln2_g, m_ln2_b, v_meta_tokens, v_in_ln_g, v_in_ln_b, v_rel_bias, v_w_in, v_b_in, v_attn_sinks, v_w_attn_proj, v_conv_dw, v_conv_dw_b, v_conv_ln_g, v_conv_ln_b, v_w_conv_proj, v_w_out, v_ln1_g, v_ln1_b, v_ffn_w_up, v_ffn_dw, v_ffn_dw_b, v_ffn_w_down, v_ln2_g, v_ln2_b):
    nex, seq, _ = x.shape
    nblk = seq // BLK + 1
    tp = nblk * BLK
    m = nex * tp
    tm = _row_tile(m)
    ffs = ffn_w_up.shape[2]
    dff = 2 * ffs
    cx, cy, cc = lax.axis_index("x"), lax.axis_index("y"), lax.axis_index("c")
    chip = (2 * cx + cy).astype(jnp.int32)
    core = cc.astype(jnp.int32)

    names = ("in", "ap", "cp", "out", "up", "down")
    big = dict(zip(names, [w_in, w_attn_proj, w_conv_proj, w_out, ffn_w_up, ffn_w_down]))
    sb = {k: v.astype(BF16) for k, v in big.items()}
    gathered = {}

    def land(items, outs):
        return [lax.dynamic_update_index_in_dim(g, s if layer is None else s[layer], chip, 0)
                for (s, layer), g in zip(items, outs)]

    first_items = [(sb["in"], 0), (meta_tokens.reshape(2, N_META // 2, -1), None), (conv_dw, None), (ffn_dw, None)]
    gathered[("in", 0)], g_meta, g_cdw, g_fdw = land(first_items, _run_job("gather_first", _gather_job(first_items)))
    meta_full = jnp.transpose(g_meta, (1, 2, 0, 3)).reshape(N_META, D)
    bias_tab = _bias_build(rel_bias)

    fwd_plan = {("in_proj", 0): [("ap", 0), ("cp", 0), ("out", 0)], ("attn_fwd", 0): [("up", 0)],
                ("conv_fwd", 0): [("down", 0)], ("mix_fwd", 0): [("in", 1)],
                ("out_proj_ln", 0): [("ap", 1), ("cp", 1), ("out", 1)], ("ffn_up", 0): [("up", 1)],
                ("ffn_act", 0): [("down", 1)]}

    def fwd(tag, l, fn, *args):
        keys = fwd_plan.get((tag, l))
        if not keys:
            return fn(*args)
        items = [(sb[k], kl) for k, kl in keys]
        res = fn(*args, side=_gather_job(items))
        for key, g in zip(keys, land(items, res[-len(keys):])):
            gathered[key] = g
        main = res[:-len(keys)]
        return main[0] if len(main) == 1 else main

    def layer_weights(l):
        win_old = jnp.transpose(gathered[("in", l)], (1, 0, 2)).reshape(D, IN_COLS)
        return dict(
            win=_to_new(win_old), bin=_to_new(b_in[l]).reshape(1, IN_COLS),
            cdw=jnp.transpose(g_cdw[:, l], (1, 0, 2)).reshape(CTAPS, CW),
            fdw=jnp.transpose(g_fdw[:, l], (1, 0, 2)).reshape(FTAPS, 2 * dff),
            fdwb=ffn_dw_b[l].reshape(1, 2 * dff))

    raw, h, hb = _embed_ln(x, meta_full, in_ln_g, in_ln_b, nblk)
    saved, lw = [], []
    for l in range(DEPTH):
        p = layer_weights(l)
        z = fwd("in_proj", l, functools.partial(_mm_bias, "in_proj"), hb, p["win"], p["bin"], IN_COLS // 3, tm)
        a = fwd("attn_fwd", l, _attn_fwd, z, bias_tab, attn_sinks[l], nex, nblk)
        ccv, cs = fwd("conv_fwd", l, _conv_fwd, z, p["cdw"], conv_dw_b[l], conv_ln_g[l], conv_ln_b[l], nex, tp)
        p["wap"], p["wcp"] = gathered[("ap", l)], gathered[("cp", l)]
        ya, yc, mixed = fwd("mix_fwd", l, _mix_fwd, a, cs, p["wap"], p["wcp"], z, tm)
        p["wout"] = gathered[("out", l)].reshape(D, D)
        r1, h1, h1b = fwd("out_proj_ln", l, functools.partial(_mm_res_ln, "out_proj_ln"), mixed, p["wout"], h, ln1_g[l], ln1_b[l], tm)
        p["wup"] = gathered[("up", l)]
        up3 = fwd("ffn_up", l, _ffn_up, h1b, p["wup"], tm, nex, tp)
        act = fwd("ffn_act", l, _ffn_act, up3, p["fdw"], p["fdwb"], tm)
        p["wdown"] = gathered[("down", l)].reshape(dff, D)
        r2, h2, h2b = _mm_res_ln("ffn_down_ln", act, p["wdown"], h1, ln2_g[l], ln2_b[l], tm)
        saved.append(dict(hb=hb, z=z, a=a, cc=ccv, cs=cs, ya=ya, yc=yc, mixed=mixed, r1=r1, h1b=h1b, up3=up3, act=act, r2=r2))
        lw.append(p)
        h, hb = h2, h2b

    dy, sq = _loss_grad(h, loss_target, nblk)
    loss = lax.psum(0.5 / D * jnp.sum(sq), ("x", "y", "c"))

    grads, swapped, pair_sums, reduced = {}, {}, {}, {}
    cvec, where = core.reshape(1), jnp.stack([chip, core])
    last = [(k, DEPTH - 1) for k in names]
    bwd_plan = {("ln2_bwd", 0): ("swap", last),
                ("ffn_act_bwd", 0): ("exch", [("up", 1), ("down", 1)]),
                ("ffn_conv_bwd", 0): ("exch", [("in", 1), ("ap", 1), ("cp", 1), ("out", 1)]),
                ("ln1_bwd", 0): ("swap", [("down", 0), ("up", 0)]),
                ("conv_bwd_b", 0): ("swap", [("out", 0), ("ap", 0), ("cp", 0)]),
                ("attn_bwd", 0): ("exch", [("down", 0), ("up", 0), ("out", 0), ("ap", 0), ("cp", 0)]),
                ("in_ln_bwd", 0): ("swap", [("in", 0)])}

    def after(kind, keys, outs):
        for key, o in zip(keys, outs):
            if kind == "swap":
                swapped[key] = o
                pair_sums[key] = _add_half("grad_add_sibling", grads[key], o, cvec)
            else:
                reduced[key] = _add_chips("grad_add_chips", grads[key], swapped[key], o, where)

    def bwd(tag, l, fn, *args):
        plan = bwd_plan.get((tag, l))
        if plan is None:
            return fn(*args)
        kind, keys = plan
        job = _swap_job([grads[k] for k in keys]) if kind == "swap" else _exchange_job([pair_sums[k] for k in keys])
        res = fn(*args, side=job)
        after(kind, keys, res[-len(keys):])
        main = res[:-len(keys)]
        return main[0] if len(main) == 1 else main

    small = {}
    prev_a, prev_w, prev_cols = (), None, None
    dprev = dy
    for l in reversed(range(DEPTH)):
        p, s = lw[l], saved[l]
        dr2, dr2b, dg2, db2 = bwd("ln2_bwd", l, functools.partial(_ln_bwd_call, "ln2_bwd"), dprev, s["r2"], ln2_g[l], tm,
                                  prev_a, prev_w, prev_cols)
        dup3 = bwd("ffn_act_bwd", l, _ffn_act_bwd, dr2b, p["wdown"], s["up3"], p["fdw"], p["fdwb"], tm)
        dpre3, dfdw, dfdwb = bwd("ffn_conv_bwd", l, _ffn_conv_bwd, dup3, s["up3"], p["fdw"], tm, nex, tp)
        grads[("down", l)] = _mm_tn("dw_down", s["act"], dr2b, ffs, D // 2).reshape(NCHIP, dff // NCHIP, D)
        grads[("up", l)] = _mm_tn("dw_up", s["h1b"], dpre3, D, ffs, b_cols=lambda j: (j // 2, j % 2), chip_out=True)
        dr1, dr1b, dg1, db1 = bwd("ln1_bwd", l, functools.partial(_ln_bwd_call, "ln1_bwd"), dr2, s["r1"], ln1_g[l], tm // 2,
                                  (dpre3,), p["wup"])
        dya, dyc, dzg, csg = _gate_bwd(dr1b, p["wout"], s["ya"], s["yc"], s["z"], tm)
        grads[("out", l)] = _mm_tn("dw_out", s["mixed"], dr1b, D, D // 2).reshape(NCHIP, D // NCHIP, D)
        grads[("ap", l)] = _mm_tn("dw_attn_proj", s["a"], dya, AW, D // NCHIP, chip_out=True)
        grads[("cp", l)] = _mm_tn("dw_conv_proj", s["cs"], dyc, CW, D // NCHIP, chip_out=True)
        dcc, dclg, dclb, dcwb = _conv_bwd_a(dyc, p["wcp"], s["cc"], conv_ln_g[l], conv_ln_b[l], tm)
        dzc, dcdw, csc = bwd("conv_bwd_b", l, _conv_bwd_b, dcc, s["z"], p["cdw"], nex, tp)
        dzq, csq, dsk, dbias = bwd("attn_bwd", l, _attn_bwd, s["z"], bias_tab, attn_sinks[l], dya, p["wap"], nex, nblk)
        gin = [_mm_tn("dw_in_gates", s["hb"], dzg, D, D // 2), _mm_tn("dw_in_conv", s["hb"], dzc, D, CW),
               _mm_tn("dw_in_qkv", s["hb"], dzq, D, 2 * KVW)]
        gin_old = _to_old(jnp.concatenate(gin, axis=1))
        grads[("in", l)] = jnp.transpose(gin_old.reshape(D, NCHIP, IN_COLS // NCHIP), (1, 0, 2))
        small[l] = dict(
            b_in=_to_old(jnp.concatenate([csg, csc, csq], axis=1)).reshape(IN_COLS), attn_sinks=dsk[0, :NQ],
            conv_dw=dcdw[:CTAPS], conv_dw_b=dcwb.reshape(CW), conv_ln_g=dclg.reshape(CW), conv_ln_b=dclb.reshape(CW),
            ln1_g=dg1.reshape(D), ln1_b=db1.reshape(D),
            ffn_dw=jnp.transpose(dfdw, (1, 0, 2)).reshape(FTAPS, 2 * dff), ffn_dw_b=jnp.transpose(dfdwb, (1, 0, 2)).reshape(2 * dff),
            ln2_g=dg2.reshape(D), ln2_b=db2.reshape(D), bias=_bias_grad(dbias))
        dprev = dr1
        prev_a, prev_w, prev_cols = (dzg, dzc, dzq), p["win"], [(C_GATES, C_CONV), (C_CONV, C_QKV), (C_QKV, IN_COLS)]
    draw, _, dg0, db0 = bwd("in_ln_bwd", 0, functools.partial(_ln_bwd_call, "in_ln_bwd"), dprev, raw, in_ln_g, tm,
                            prev_a, prev_w, prev_cols)
    draw3 = draw.reshape(nex, tp, D)
    grad_x = draw3[:, BLK:, :]
    dmeta = _sum0("meta_grad_sum", draw3[:, PAD:BLK, :])

    names_l = ["b_in", "attn_sinks", "conv_dw", "conv_dw_b", "conv_ln_g", "conv_ln_b", "ln1_g", "ln1_b", "ffn_dw", "ffn_dw_b", "ln2_g", "ln2_b"]
    part_list = [dmeta, dg0.reshape(D), db0.reshape(D), small[0]["bias"] + small[1]["bias"]]
    part_list += [jnp.stack([small[0][n], small[1][n]]) for n in names_l]
    shapes_small = [tuple(a.shape) for a in part_list]
    tot = _sum0("small_grad_sum", _allgather_small(_pack(part_list)))
    (g_meta_f, g_inlg, g_inlb, g_biasp, g_bin, g_sinks, g_cdw_f, g_cdwb, g_clg, g_clb, g_l1g, g_l1b, g_fdw_f, g_fdwb,
     g_l2g, g_l2b) = _unpack(tot, shapes_small)
    g_relb = g_biasp
    csh = D // NCHIP
    g_meta_s = lax.dynamic_slice_in_dim(g_meta_f, chip * csh, csh, axis=1)
    g_cdw_s = lax.dynamic_slice_in_dim(g_cdw_f, chip * (CW // NCHIP), CW // NCHIP, axis=2)
    g_fdw_s = lax.dynamic_slice_in_dim(g_fdw_f, chip * ffs, ffs, axis=2)

    tail = [("in", 0)]
    after("exch", tail, _run_job("grad_chip_exchange", _exchange_job([pair_sums[k] for k in tail])))
    joined = _sibling_join([[reduced[(k, l)] for l in range(DEPTH)] for k in names])
    full = []
    for k, f in zip(names, joined):
        hh = reduced[(k, 0)].shape[0]
        for l in range(DEPTH):
            f = lax.dynamic_update_slice(f, reduced[(k, l)][None], (l, core * hh, 0))
        full.append(f)

    moms = [m_w_in, m_w_attn_proj, m_w_conv_proj, m_w_out, m_ffn_w_up, m_ffn_w_down]
    vels = [v_w_in, v_w_attn_proj, v_w_conv_proj, v_w_out, v_ffn_w_up, v_ffn_w_down]
    big_out = []
    for w, g, mo, ve in zip(big.values(), full, moms, vels):
        sh = w.shape
        two = lambda t: t.reshape(sh[0] * sh[1], sh[2])
        d_, m_, v_ = _adamw("adamw_matrix", two(w), two(g), two(mo), two(ve))
        big_out.append((g.reshape(sh), d_.reshape(sh), m_.reshape(sh), v_.reshape(sh)))

    sm_w = [meta_tokens, in_ln_g, in_ln_b, rel_bias, b_in, attn_sinks, conv_dw, conv_dw_b, conv_ln_g, conv_ln_b, ln1_g, ln1_b,
            ffn_dw, ffn_dw_b, ln2_g, ln2_b]
    sm_m = [m_meta_tokens, m_in_ln_g, m_in_ln_b, m_rel_bias, m_b_in, m_attn_sinks, m_conv_dw, m_conv_dw_b, m_conv_ln_g, m_conv_ln_b,
            m_ln1_g, m_ln1_b, m_ffn_dw, m_ffn_dw_b, m_ln2_g, m_ln2_b]
    sm_v = [v_meta_tokens, v_in_ln_g, v_in_ln_b, v_rel_bias, v_b_in, v_attn_sinks, v_conv_dw, v_conv_dw_b, v_conv_ln_g, v_conv_ln_b,
            v_ln1_g, v_ln1_b, v_ffn_dw, v_ffn_dw_b, v_ln2_g, v_ln2_b]
    sm_g = [g_meta_s, g_inlg, g_inlb, g_relb, g_bin, g_sinks, g_cdw_s, g_cdwb, g_clg, g_clb, g_l1g, g_l1b, g_fdw_s, g_fdwb, g_l2g, g_l2b]
    sm_shapes = [tuple(a.shape) for a in sm_w]
    sd, smn, svn = _adamw("adamw_small", _pack(sm_w), _pack(sm_g), _pack(sm_m), _pack(sm_v))
    sd, smn, svn = _unpack(sd, sm_shapes), _unpack(smn, sm_shapes), _unpack(svn, sm_shapes)

    order = ["meta_tokens", "in_ln_g", "in_ln_b", "rel_bias", "w_in", "b_in", "attn_sinks", "w_attn_proj", "conv_dw", "conv_dw_b",
             "conv_ln_g", "conv_ln_b", "w_conv_proj", "w_out", "ln1_g", "ln1_b", "ffn_w_up", "ffn_dw", "ffn_dw_b", "ffn_w_down",
             "ln2_g", "ln2_b"]
    small_names = ["meta_tokens", "in_ln_g", "in_ln_b", "rel_bias", "b_in", "attn_sinks", "conv_dw", "conv_dw_b", "conv_ln_g",
                   "conv_ln_b", "ln1_g", "ln1_b", "ffn_dw", "ffn_dw_b", "ln2_g", "ln2_b"]
    big_names = ["w_in", "w_attn_proj", "w_conv_proj", "w_out", "ffn_w_up", "ffn_w_down"]
    res = {}
    for i, n in enumerate(small_names):
        res[n] = (sm_g[i], sd[i], smn[i], svn[i])
    for i, n in enumerate(big_names):
        res[n] = big_out[i]
    outs = [loss, grad_x]
    for k in range(4):
        outs += [res[n][k] for n in order]
    return tuple(outs)
```

```python
import functools
import math
from typing import Any, Callable, NamedTuple, Sequence

import numpy as np
import jax
import jax.numpy as jnp
from jax import lax
from jax.experimental import pallas as pl
from jax.experimental.pallas import tpu as pltpu

F32 = jnp.float32
BF16 = jnp.bfloat16
MESH = pl.DeviceIdType.MESH

D = 1024
N_META = 16
BLK = 128
PAD = BLK - N_META
HD = 64
NQ = 8
NKV = 2
GRP = NQ // NKV
AW = NQ * HD
KVW = NKV * HD
CW = D // 2
CTAPS = 31
FTAPS = 3
NBUCKET = 32
MAXDIST = 128
EPS = 1e-5
DEPTH = 2
ALPHA = (2.0 * DEPTH) ** 0.25
NCHIP = 4
NKEY = 3 * BLK
NEG = -1e30
CHALO = 32
FHALO = 8
IN_COLS = AW + 2 * KVW + 2 * CW + 2 * D
_OLD = dict(q=(0, AW), k=(AW, AW + KVW), v=(AW + KVW, AW + 2 * KVW), cv=(AW + 2 * KVW, AW + 2 * KVW + CW),
            cg=(AW + 2 * KVW + CW, AW + 2 * KVW + 2 * CW), ga=(AW + 2 * KVW + 2 * CW, AW + 2 * KVW + 2 * CW + D),
            gc=(AW + 2 * KVW + 2 * CW + D, IN_COLS))
_NEW_ORDER = ("ga", "gc", "cv", "cg", "q", "k", "v")
C_GATES, C_CONV, C_QKV = 0, 2 * D, 2 * D + 2 * CW

ADAM_LR, ADAM_B1, ADAM_B2, ADAM_EPS, ADAM_WD, ADAM_STEP = 0.001, 0.9, 0.999, 1e-08, 0.01, 10


def _to_new(a):
    return jnp.concatenate([a[..., _OLD[n][0]:_OLD[n][1]] for n in _NEW_ORDER], axis=-1)


def _to_old(a):
    offs, o = {}, 0
    for n in _NEW_ORDER:
        w = _OLD[n][1] - _OLD[n][0]
        offs[n] = (o, o + w)
        o += w
    return jnp.concatenate([a[..., offs[n][0]:offs[n][1]] for n in ("q", "k", "v", "cv", "cg", "ga", "gc")], axis=-1)


class _Job(NamedTuple):
    ins: Sequence[Any]
    outs: Sequence[Any]
    sems: tuple
    start: Callable
    finish: Callable


_ANY = pl.BlockSpec(memory_space=pl.ANY)


def _call(body, name, out_shape, grid, in_specs, out_specs, scratch=(), prefetch=0, side=None):
    params = pltpu.CompilerParams(dimension_semantics=("arbitrary",) * len(grid))
    if side is None:
        if prefetch:
            gs = pltpu.PrefetchScalarGridSpec(num_scalar_prefetch=prefetch, grid=grid, in_specs=in_specs,
                                              out_specs=out_specs, scratch_shapes=list(scratch))
            return pl.pallas_call(body, name=name, out_shape=out_shape, grid_spec=gs, compiler_params=params)
        return pl.pallas_call(body, name=name, out_shape=out_shape, grid=grid, in_specs=in_specs, out_specs=out_specs,
                              scratch_shapes=list(scratch), compiler_params=params)
    assert not prefetch
    single = not isinstance(out_shape, (tuple, list))
    main_shapes = (out_shape,) if single else tuple(out_shape)
    main_specs = (out_specs,) if single else tuple(out_specs)
    n_in, n_sin, n_out, n_sout, n_scr = len(in_specs), len(side.ins), len(main_shapes), len(side.outs), len(scratch)

    def wrapped(*refs):
        main_in, sin = refs[:n_in], refs[n_in:n_in + n_sin]
        o0 = n_in + n_sin
        main_out, sout = refs[o0:o0 + n_out], refs[o0 + n_out:o0 + n_out + n_sout]
        s0 = o0 + n_out + n_sout
        main_scr, (send, recv) = refs[s0:s0 + n_scr], refs[s0 + n_scr:]
        first = functools.reduce(lambda a, b: a & b, [pl.program_id(k) == 0 for k in range(len(grid))])
        last = functools.reduce(lambda a, b: a & b, [pl.program_id(k) == grid[k] - 1 for k in range(len(grid))])

        @pl.when(first)
        def _():
            side.start(sin, sout, send, recv)

        body(*main_in, *main_out, *main_scr)

        @pl.when(last)
        def _():
            side.finish(sin, sout, send, recv)

    call = pl.pallas_call(
        wrapped, name=name, out_shape=main_shapes + tuple(side.outs), grid=grid,
        in_specs=list(in_specs) + [_ANY] * n_sin, out_specs=main_specs + tuple([_ANY] * n_sout),
        scratch_shapes=list(scratch) + [pltpu.SemaphoreType.DMA(side.sems), pltpu.SemaphoreType.DMA(side.sems)],
        compiler_params=params)
    return lambda *args: call(*args, *side.ins)


def _row_tile(m):
    best = 32
    for t in range(32, 641, 32):
        if m % t == 0:
            best = t
    return best


def _pad_rows(rows, nex, tp):
    m = rows < PAD
    for b in range(1, nex):
        m = m | ((rows >= b * tp) & (rows < b * tp + PAD))
    return m


def _ln_stats(x):
    mu = jnp.mean(x, axis=-1, keepdims=True)
    xc = x - mu
    var = jnp.mean(xc * xc, axis=-1, keepdims=True)
    rstd = lax.rsqrt(var + EPS)
    return xc * rstd, rstd


def _ln_bwd(dy, xhat, rstd, g):
    dxh = dy * g
    m1 = jnp.mean(dxh, axis=-1, keepdims=True)
    m2 = jnp.mean(dxh * xhat, axis=-1, keepdims=True)
    return rstd * (dxh - m1 - xhat * m2)


def _dot(a, b):
    return jnp.dot(a, b, preferred_element_type=F32)


def _dot_nt(a, b):
    return lax.dot_general(a, b, (((1,), (1,)), ((), ())), preferred_element_type=F32)


def _dot_tn(a, b):
    return lax.dot_general(a, b, (((0,), (0,)), ((), ())), preferred_element_type=F32)


def _sigmoid(x):
    return 1.0 / (1.0 + jnp.exp(-x))


def _gelu_parts(g):
    cdf = 0.5 * (1.0 + lax.erf(g * (1.0 / math.sqrt(2.0))))
    pdf = jnp.exp(-0.5 * g * g) * (1.0 / math.sqrt(2.0 * math.pi))
    return g * cdf, cdf + g * pdf


def _bucket_np(d):
    n = np.maximum(d, 0)
    max_exact = NBUCKET // 2
    nf = np.maximum(n, 1).astype(np.float32)
    large = max_exact + (np.log(nf / np.float32(max_exact)) / np.float32(math.log(MAXDIST / max_exact))
                         * np.float32(NBUCKET - max_exact)).astype(np.int32)
    large = np.minimum(large, NBUCKET - 1)
    return np.where(n < max_exact, n, large).astype(np.int32)


def _bias_index():
    i = np.arange(BLK)[:, None]
    j = np.arange(2 * BLK)[None, :]
    d = BLK + i - j
    band_ok = (d >= 0) & (d < BLK)
    band = _bucket_np(d)
    idx = np.full((3, BLK, NKEY), -1, np.int32)
    m = np.arange(N_META)[None, :]
    d0 = (i - PAD) - m
    idx[0, :, 2 * BLK:2 * BLK + N_META] = np.where(d0 >= 0, _bucket_np(d0), -1)
    ok1 = band_ok & (j >= BLK)
    idx[1, :, :2 * BLK] = np.where(ok1, band, -1)
    idx[1, :, 2 * BLK:2 * BLK + N_META] = _bucket_np((N_META + i) - m)
    idx[2, :, :2 * BLK] = np.where(band_ok, band, -1)
    idx[2, :, 2 * BLK:2 * BLK + N_META] = NBUCKET - 1
    return idx


def _bias_build(rel_bias):
    idx = jnp.asarray(_bias_index())

    def body(idx_ref, rb_ref, o_ref):
        ix = idx_ref[...]
        for h in range(NQ):
            acc = jnp.full(ix.shape, NEG, F32)
            for b in range(NBUCKET):
                acc = jnp.where(ix == b, rb_ref[b, h], acc)
            o_ref[:, h, :, :] = acc

    return pl.pallas_call(
        body, name="bias_build", out_shape=jax.ShapeDtypeStruct((3, NQ, BLK, NKEY), F32),
        in_specs=[pl.BlockSpec(memory_space=pltpu.VMEM), pl.BlockSpec(memory_space=pltpu.SMEM)],
        out_specs=pl.BlockSpec(memory_space=pltpu.VMEM))(idx, rel_bias)


def _bias_grad(dbias):
    idx = jnp.asarray(_bias_index())

    def body(idx_ref, d_ref, o_ref):
        d = jnp.sum(d_ref[...], axis=0)
        for b in range(NBUCKET):
            acc = jnp.zeros((NQ, NKEY), F32)
            for case in range(3):
                hit = (idx_ref[case] == b)[None, :, :]
                acc = acc + jnp.sum(jnp.where(hit, d[case], 0.0), axis=1)
            o_ref[b] = jnp.sum(acc, axis=-1, keepdims=True)

    out = pl.pallas_call(
        body, name="bias_grad", out_shape=jax.ShapeDtypeStruct((NBUCKET, NQ, 1), F32),
        in_specs=[pl.BlockSpec(memory_space=pltpu.VMEM), pl.BlockSpec(memory_space=pltpu.VMEM)],
        out_specs=pl.BlockSpec(memory_space=pltpu.VMEM))(idx, dbias)
    return out.reshape(NBUCKET, NQ)


def _embed_ln(x, meta, g, b, nblk):
    nex, seq, _ = x.shape
    m = nex * nblk * BLK

    def body(x_ref, meta_ref, g_ref, b_ref, raw_ref, h_ref, hb_ref):
        j = pl.program_id(1)

        @pl.when(j == 0)
        def _():
            raw_ref[0:PAD, :] = jnp.zeros((PAD, D), F32)
            raw_ref[PAD:BLK, :] = meta_ref[...]

        @pl.when(j > 0)
        def _():
            raw_ref[...] = x_ref[...]

        xhat, _ = _ln_stats(raw_ref[...])
        y = xhat * g_ref[...] + b_ref[...]
        h_ref[...] = y
        hb_ref[...] = y.astype(BF16)

    row = lambda bb, j: (bb * nblk + j, 0)
    return _call(
        body, "embed_ln",
        (jax.ShapeDtypeStruct((m, D), F32), jax.ShapeDtypeStruct((m, D), F32), jax.ShapeDtypeStruct((m, D), BF16)),
        (nex, nblk),
        [pl.BlockSpec((None, BLK, D), lambda bb, j: (bb, jnp.maximum(j - 1, 0), 0)),
         pl.BlockSpec((N_META, D), lambda bb, j: (0, 0)),
         pl.BlockSpec((1, D), lambda bb, j: (0, 0)), pl.BlockSpec((1, D), lambda bb, j: (0, 0))],
        (pl.BlockSpec((BLK, D), row), pl.BlockSpec((BLK, D), row), pl.BlockSpec((BLK, D), row)),
    )(x, meta, g.reshape(1, D), b.reshape(1, D))


def _mm_bias(name, a, w, bias, tn, tm, side=None):
    m, k = a.shape
    n = w.shape[1]

    def body(a_ref, w_ref, b_ref, o_ref):
        o_ref[...] = _dot(a_ref[...], w_ref[...]) + b_ref[...]

    return _call(body, name, jax.ShapeDtypeStruct((m, n), F32), (n // tn, m // tm),
                 [pl.BlockSpec((tm, k), lambda j, i: (i, 0)), pl.BlockSpec((k, tn), lambda j, i: (0, j)),
                  pl.BlockSpec((1, tn), lambda j, i: (0, j))],
                 pl.BlockSpec((tm, tn), lambda j, i: (i, j)), side=side)(a, w, bias)


def _ffn_up(a, w4, tm, nex, tp, side=None):
    m, k = a.shape
    ffs = w4.shape[2]

    def body(a_ref, w_ref, o_ref):
        i = pl.program_id(1)
        rows = i * tm + lax.broadcasted_iota(jnp.int32, (tm, 1), 0)
        acc = _dot(a_ref[...], w_ref[...])
        o_ref[...] = jnp.where(_pad_rows(rows, nex, tp), 0.0, acc)

    return _call(body, "ffn_up", jax.ShapeDtypeStruct((2, m, 2 * ffs), F32), (NCHIP, m // tm),
                 [pl.BlockSpec((tm, k), lambda j, i: (i, 0)), pl.BlockSpec((None, k, ffs), lambda j, i: (j, 0, 0))],
                 pl.BlockSpec((None, tm, ffs), lambda j, i: (j // 2, i, j % 2)), side=side)(a, w4)


def _fill_kv(ks, vs, prev_ref, cur_ref, meta_ref):
    for piece, lo, n in ((prev_ref, 0, BLK), (cur_ref, BLK, BLK), (meta_ref, 2 * BLK, N_META)):
        val = piece[...]
        for hk in range(NKV):
            ks[hk, lo:lo + n, :] = val[:, hk * HD:(hk + 1) * HD].astype(BF16)
            vs[hk, lo:lo + n, :] = val[:, KVW + hk * HD:KVW + (hk + 1) * HD].astype(BF16)
    for hk in range(NKV):
        ks[hk, 2 * BLK + N_META:NKEY, :] = jnp.zeros((BLK - N_META, HD), BF16)
        vs[hk, 2 * BLK + N_META:NKEY, :] = jnp.zeros((BLK - N_META, HD), BF16)


def _softmax_group(q, ks_hk, bias_ref, sink_ref, hk):
    qg = jnp.concatenate([q[:, (hk * GRP + g) * HD:(hk * GRP + g + 1) * HD] for g in range(GRP)], axis=0).astype(BF16)
    s = _dot_nt(qg, ks_hk) * (HD ** -0.5) + bias_ref[hk * GRP:(hk + 1) * GRP].reshape(GRP * BLK, NKEY)
    sink = jnp.concatenate([jnp.full((BLK, 1), sink_ref[0, hk * GRP + g], F32) for g in range(GRP)], axis=0)
    mx = jnp.maximum(jnp.max(s, axis=-1, keepdims=True), sink)
    p = jnp.exp(s - mx)
    es = jnp.exp(sink - mx)
    inv = 1.0 / (jnp.sum(p, axis=-1, keepdims=True) + es)
    return qg, p * inv, es * inv


def _attn_specs(nblk, tp, blk_of):
    qcol, kvcol = (C_QKV) // AW, (C_QKV + AW) // (2 * KVW)
    return [
        pl.BlockSpec((BLK, AW), lambda bb, j: (bb * nblk + blk_of(j), qcol)),
        pl.BlockSpec((BLK, 2 * KVW), lambda bb, j: (bb * nblk + blk_of(j), kvcol)),
        pl.BlockSpec((BLK, 2 * KVW), lambda bb, j: (bb * nblk + jnp.maximum(blk_of(j) - 1, 0), kvcol)),
        pl.BlockSpec((N_META, 2 * KVW), lambda bb, j: (bb * (tp // N_META) + PAD // N_META, kvcol)),
        pl.BlockSpec((None, NQ, BLK, NKEY), lambda bb, j: (jnp.minimum(blk_of(j), 2), 0, 0, 0)),
        pl.BlockSpec(memory_space=pltpu.SMEM),
    ]


def _attn_fwd(z, bias, sinks, nex, nblk, side=None):
    m = z.shape[0]
    tp = nblk * BLK

    def body(q_ref, cur_ref, prev_ref, meta_ref, bias_ref, sink_ref, o_ref, ks, vs, oacc):
        _fill_kv(ks, vs, prev_ref, cur_ref, meta_ref)
        q = q_ref[...]
        for hk in range(NKV):
            _, pn, _ = _softmax_group(q, ks[hk], bias_ref, sink_ref, hk)
            o = _dot(pn.astype(BF16), vs[hk])
            for g in range(GRP):
                h = hk * GRP + g
                oacc[:, h * HD:(h + 1) * HD] = o[g * BLK:(g + 1) * BLK, :]
        o_ref[...] = oacc[...].astype(BF16)

    return _call(body, "attn_fwd", jax.ShapeDtypeStruct((m, AW), BF16), (nex, nblk),
                 _attn_specs(nblk, tp, lambda j: j),
                 pl.BlockSpec((BLK, AW), lambda bb, j: (bb * nblk + j, 0)),
                 scratch=[pltpu.VMEM((NKV, NKEY, HD), BF16), pltpu.VMEM((NKV, NKEY, HD), BF16),
                          pltpu.VMEM((BLK, AW), F32)], side=side)(z, z, z, z, bias, sinks.reshape(1, NQ))


def _cgate(cv, cg, rows, nex, tp):
    return jnp.where(_pad_rows(rows, nex, tp), 0.0, cv * _sigmoid(cg))


def _conv_fwd(z, w, wb, g, b, nex, tp, side=None):
    m = z.shape[0]
    tm = BLK
    sub = tm // CHALO
    cvc, cgc = C_CONV // CW, C_CONV // CW + 1

    def body(cv_ref, cg_ref, cvh_ref, cgh_ref, w_ref, wb_ref, g_ref, b_ref, cc_ref, cs_ref, win):
        i = pl.program_id(0)
        rows = i * tm + lax.broadcasted_iota(jnp.int32, (tm, 1), 0)
        hrows = i * tm - CHALO + lax.broadcasted_iota(jnp.int32, (CHALO, 1), 0)
        win[0:CHALO, :] = _cgate(cvh_ref[...], cgh_ref[...], hrows, nex, tp)
        win[CHALO:CHALO + tm, :] = _cgate(cv_ref[...], cg_ref[...], rows, nex, tp)
        for sb in range(sub):
            acc = jnp.zeros((CHALO, CW), F32) + wb_ref[...]
            for k in range(CTAPS):
                acc = acc + w_ref[k:k + 1, :] * win[sb * CHALO + 2 + k:sb * CHALO + 2 + k + CHALO, :]
            cc_ref[sb * CHALO:(sb + 1) * CHALO, :] = acc
        xhat, _ = _ln_stats(cc_ref[...])
        cl = xhat * g_ref[...] + b_ref[...]
        cs_ref[...] = (cl * _sigmoid(cl)).astype(BF16)

    halo = lambda i: jnp.maximum(i * sub - 1, 0)
    vec = pl.BlockSpec((1, CW), lambda i: (0, 0))
    return _call(body, "conv_fwd", (jax.ShapeDtypeStruct((m, CW), F32), jax.ShapeDtypeStruct((m, CW), BF16)),
                 (m // tm,),
                 [pl.BlockSpec((tm, CW), lambda i: (i, cvc)), pl.BlockSpec((tm, CW), lambda i: (i, cgc)),
                  pl.BlockSpec((CHALO, CW), lambda i: (halo(i), cvc)), pl.BlockSpec((CHALO, CW), lambda i: (halo(i), cgc)),
                  pl.BlockSpec((CTAPS, CW), lambda i: (0, 0)), vec, vec, vec],
                 (pl.BlockSpec((tm, CW), lambda i: (i, 0)), pl.BlockSpec((tm, CW), lambda i: (i, 0))),
                 scratch=[pltpu.VMEM((CHALO + tm, CW), F32)], side=side)(z, z, z, z, w, wb.reshape(1, CW), g.reshape(1, CW), b.reshape(1, CW))


def _mix_fwd(a, cs, wap4, wcp4, z, tm, side=None):
    m = a.shape[0]
    ns = wap4.shape[2]

    def body(a_ref, cs_ref, wa_ref, wc_ref, ga_ref, gc_ref, ya_ref, yc_ref, mix_ref):
        av, cv = a_ref[...], cs_ref[...]
        for j in range(NCHIP):
            ya_ref[:, j * ns:(j + 1) * ns] = _dot(av, wa_ref[j])
            yc_ref[:, j * ns:(j + 1) * ns] = _dot(cv, wc_ref[j])
        mix_ref[...] = (_sigmoid(ga_ref[...]) * ya_ref[...] + _sigmoid(gc_ref[...]) * yc_ref[...]).astype(BF16)

    wspec = pl.BlockSpec((NCHIP, AW, ns), lambda i: (0, 0, 0))
    row = lambda i: (i, 0)
    return _call(body, "mix_fwd",
                 (jax.ShapeDtypeStruct((m, D), F32), jax.ShapeDtypeStruct((m, D), F32), jax.ShapeDtypeStruct((m, D), BF16)),
                 (m // tm,),
                 [pl.BlockSpec((tm, AW), row), pl.BlockSpec((tm, CW), row), wspec, wspec,
                  pl.BlockSpec((tm, D), lambda i: (i, 0)), pl.BlockSpec((tm, D), lambda i: (i, 1))],
                 (pl.BlockSpec((tm, D), row), pl.BlockSpec((tm, D), row), pl.BlockSpec((tm, D), row)), side=side)(a, cs, wap4, wcp4, z, z)


def _mm_res_ln(name, a, w, res, g, b, tm, side=None):
    m, k = a.shape

    def body(a_ref, w_ref, res_ref, g_ref, b_ref, r_ref, h_ref, hb_ref):
        r = ALPHA * res_ref[...] + _dot(a_ref[...], w_ref[...])
        r_ref[...] = r
        xhat, _ = _ln_stats(r)
        y = xhat * g_ref[...] + b_ref[...]
        h_ref[...] = y
        hb_ref[...] = y.astype(BF16)

    row = lambda i: (i, 0)
    vec = pl.BlockSpec((1, D), lambda i: (0, 0))
    return _call(body, name,
                 (jax.ShapeDtypeStruct((m, D), F32), jax.ShapeDtypeStruct((m, D), F32), jax.ShapeDtypeStruct((m, D), BF16)),
                 (m // tm,),
                 [pl.BlockSpec((tm, k), row), pl.BlockSpec((k, D), lambda i: (0, 0)), pl.BlockSpec((tm, D), row), vec, vec],
                 (pl.BlockSpec((tm, D), row), pl.BlockSpec((tm, D), row), pl.BlockSpec((tm, D), row)), side=side)(a, w, res, g.reshape(1, D), b.reshape(1, D))


RCH = 16


def _lane_groups(width, most=768):
    n = -(-width // most)
    step = -(-width // (128 * n)) * 128
    return [(c, min(c + step, width)) for c in range(0, width, step)]


def _conv3(win, p, r0, c0, c1, w_ref, b_ref):
    blk = win[p, r0:r0 + FHALO + RCH, c0:c1]
    x0, x1, x2 = blk[FHALO:], pltpu.roll(blk, 1, axis=0)[FHALO:], pltpu.roll(blk, 2, axis=0)[FHALO:]
    return b_ref[:, c0:c1] + w_ref[0:1, c0:c1] * x2 + w_ref[1:2, c0:c1] * x1 + w_ref[2:3, c0:c1] * x0


def _ffn_specs(tm, ffs, m):
    sub = tm // FHALO
    return [
        pl.BlockSpec((2, tm, ffs), lambda c, i: (0, i, c)),
        pl.BlockSpec((2, FHALO, ffs), lambda c, i: (0, jnp.maximum(i * sub - 1, 0), c)),
        pl.BlockSpec((FTAPS, ffs), lambda c, i: (0, c)), pl.BlockSpec((FTAPS, ffs), lambda c, i: (0, c + 2)),
        pl.BlockSpec((1, ffs), lambda c, i: (0, c)), pl.BlockSpec((1, ffs), lambda c, i: (0, c + 2)),
    ]


def _ffn_act(up3, w, wb, tm, side=None):
    _, m, dff = up3.shape
    ffs = dff // 2

    def body(x_ref, xh_ref, wu_ref, wg_ref, bu_ref, bg_ref, o_ref, win):
        win[:, 0:FHALO, :] = xh_ref[...]
        win[:, FHALO:FHALO + tm, :] = x_ref[...]
        for r0 in range(0, tm, RCH):
            for c0, c1 in _lane_groups(ffs):
                u = _conv3(win, 0, r0, c0, c1, wu_ref, bu_ref)
                g = _conv3(win, 1, r0, c0, c1, wg_ref, bg_ref)
                o_ref[r0:r0 + RCH, c0:c1] = (g * (0.5 * (1.0 + lax.erf(g * (1.0 / math.sqrt(2.0))))) * u).astype(BF16)

    return _call(body, "ffn_act", jax.ShapeDtypeStruct((m, dff), BF16), (2, m // tm),
                 _ffn_specs(tm, ffs, m), pl.BlockSpec((tm, ffs), lambda c, i: (i, c)),
                 scratch=[pltpu.VMEM((2, FHALO + tm, ffs), F32)], side=side)(up3, up3, w, w, wb, wb)


def _loss_grad(y, target, nblk):
    nex = target.shape[0]
    m = y.shape[0]

    def body(y_ref, t_ref, dy_ref, acc_ref):
        bb, j = pl.program_id(0), pl.program_id(1)

        @pl.when((bb == 0) & (j == 0))
        def _():
            acc_ref[...] = jnp.zeros_like(acc_ref)

        @pl.when(j == 0)
        def _():
            dy_ref[...] = jnp.zeros_like(dy_ref)

        @pl.when(j > 0)
        def _():
            e = y_ref[...] - t_ref[...]
            dy_ref[...] = e * (1.0 / D)
            acc_ref[...] += jnp.sum((e * e).reshape(BLK // 8, 8, D), axis=0)

    return _call(body, "loss_grad", (jax.ShapeDtypeStruct((m, D), F32), jax.ShapeDtypeStruct((8, D), F32)), (nex, nblk),
                 [pl.BlockSpec((BLK, D), lambda bb, j: (bb * nblk + j, 0)),
                  pl.BlockSpec((None, BLK, D), lambda bb, j: (bb, jnp.maximum(j - 1, 0), 0))],
                 (pl.BlockSpec((BLK, D), lambda bb, j: (bb * nblk + j, 0)), pl.BlockSpec((8, D), lambda bb, j: (0, 0))))(y, target)


def _ln_bwd_call(name, dy, r, g, tm, a_list=(), w=None, cols=None, side=None):
    m = dy.shape[0]
    na = len(a_list)

    def body(*refs):
        dy_ref, r_ref, g_ref = refs[0:3]
        a_refs = refs[3:3 + na]
        w_ref = refs[3 + na] if na else None
        dr_ref, drb_ref, dg_ref, db_ref = refs[-4:]
        i = pl.program_id(0)
        dh = dy_ref[...]
        if na:
            dh = ALPHA * dh
            if cols is None:
                ns = w.shape[2]
                for j in range(NCHIP):
                    dh = dh + _dot_nt(a_refs[0][j // 2, :, (j % 2) * ns:(j % 2 + 1) * ns], w_ref[j])
            else:
                for a_ref, (c0, c1) in zip(a_refs, cols):
                    dh = dh + _dot_nt(a_ref[...], w_ref[:, c0:c1])
        xhat, rstd = _ln_stats(r_ref[...])
        dr = _ln_bwd(dh, xhat, rstd, g_ref[...])
        dr_ref[...] = dr
        drb_ref[...] = dr.astype(BF16)

        @pl.when(i == 0)
        def _():
            dg_ref[...] = jnp.zeros_like(dg_ref)
            db_ref[...] = jnp.zeros_like(db_ref)

        dg_ref[...] += jnp.sum(dh * xhat, axis=0, keepdims=True)
        db_ref[...] += jnp.sum(dh, axis=0, keepdims=True)

    row = lambda i: (i, 0)
    vec = pl.BlockSpec((1, D), lambda i: (0, 0))
    in_specs = [pl.BlockSpec((tm, D), row), pl.BlockSpec((tm, D), row), vec]
    for a in a_list:
        in_specs.append(pl.BlockSpec((2, tm, a.shape[2]), lambda i: (0, i, 0)) if a.ndim == 3 else pl.BlockSpec((tm, a.shape[1]), row))
    if na:
        in_specs.append(pl.BlockSpec(w.shape, (lambda i: (0, 0, 0)) if w.ndim == 3 else (lambda i: (0, 0))))
    return _call(body, name,
                 (jax.ShapeDtypeStruct((m, D), F32), jax.ShapeDtypeStruct((m, D), BF16),
                  jax.ShapeDtypeStruct((1, D), F32), jax.ShapeDtypeStruct((1, D), F32)),
                 (m // tm,), in_specs,
                 (pl.BlockSpec((tm, D), row), pl.BlockSpec((tm, D), row), vec, vec),
                 side=side)(dy, r, g.reshape(1, D), *a_list, *([w] if na else []))


def _ffn_act_bwd(drb, wdown, up3, w, wb, tm, side=None):
    _, m, dff = up3.shape
    ffs = dff // 2

    def body(dr_ref, wd_ref, x_ref, xh_ref, wu_ref, wg_ref, bu_ref, bg_ref, o_ref, win, dact):
        win[:, 0:FHALO, :] = xh_ref[...]
        win[:, FHALO:FHALO + tm, :] = x_ref[...]
        dact[...] = _dot_nt(dr_ref[...], wd_ref[...])
        for r0 in range(0, tm, RCH):
            for c0, c1 in _lane_groups(ffs):
                u = _conv3(win, 0, r0, c0, c1, wu_ref, bu_ref)
                g = _conv3(win, 1, r0, c0, c1, wg_ref, bg_ref)
                da = dact[r0:r0 + RCH, c0:c1]
                cdf = 0.5 * (1.0 + lax.erf(g * (1.0 / math.sqrt(2.0))))
                pdf = jnp.exp(-0.5 * g * g) * (1.0 / math.sqrt(2.0 * math.pi))
                o_ref[0, r0:r0 + RCH, c0:c1] = da * (g * cdf)
                o_ref[1, r0:r0 + RCH, c0:c1] = da * u * (cdf + g * pdf)

    specs = [pl.BlockSpec((tm, D), lambda c, i: (i, 0)), pl.BlockSpec((ffs, D), lambda c, i: (c, 0))] + _ffn_specs(tm, ffs, m)
    return _call(body, "ffn_act_bwd", jax.ShapeDtypeStruct((2, m, dff), F32), (2, m // tm), specs,
                 pl.BlockSpec((2, tm, ffs), lambda c, i: (0, i, c)),
                 scratch=[pltpu.VMEM((2, FHALO + tm, ffs), F32), pltpu.VMEM((tm, ffs), F32)],
                 side=side)(drb, wdown, up3, up3, w, w, wb, wb)


def _ffn_conv_bwd(dup3, up3, w, tm, nex, tp, side=None):
    _, m, dff = up3.shape
    ffs = dff // 2
    sub = tm // FHALO
    nt = m // tm

    def body(d_ref, dh_ref, x_ref, wu_ref, wg_ref, o_ref, dw_ref, db_ref, dwin, dwacc, dbacc):
        i = pl.program_id(1)
        dwin[:, 0:tm, :] = d_ref[...]
        dwin[:, tm:tm + FHALO, :] = jnp.where(i == nt - 1, 0.0, dh_ref[...])
        dwacc[...] = jnp.zeros_like(dwacc)
        dbacc[...] = jnp.zeros_like(dbacc)

        @pl.when(i == 0)
        def _():
            dw_ref[...] = jnp.zeros_like(dw_ref)
            db_ref[...] = jnp.zeros_like(db_ref)

        fold = lambda t: t[0:8, :] + t[8:16, :]
        for r0 in range(0, tm, RCH):
            rows = i * tm + r0 + lax.broadcasted_iota(jnp.int32, (RCH, 1), 0)
            pad = _pad_rows(rows, nex, tp)
            for p, w_ref in ((0, wu_ref), (1, wg_ref)):
                for c0, c1 in _lane_groups(ffs):
                    dblk = dwin[p, r0:r0 + RCH + FHALO, c0:c1]
                    d0 = dblk[:RCH]
                    d1 = pltpu.roll(dblk, RCH + FHALO - 1, axis=0)[:RCH]
                    d2 = pltpu.roll(dblk, RCH + FHALO - 2, axis=0)[:RCH]
                    dpre = w_ref[2:3, c0:c1] * d0 + w_ref[1:2, c0:c1] * d1 + w_ref[0:1, c0:c1] * d2
                    o_ref[p, r0:r0 + RCH, c0:c1] = jnp.where(pad, 0.0, dpre).astype(BF16)
                    x0 = x_ref[p, r0:r0 + RCH, c0:c1]
                    dwacc[p, 2, :, c0:c1] += fold(d0 * x0)
                    dwacc[p, 1, :, c0:c1] += fold(d1 * x0)
                    dwacc[p, 0, :, c0:c1] += fold(d2 * x0)
                    dbacc[p, :, c0:c1] += fold(d0)
        for p in range(2):
            for k in range(FTAPS):
                dw_ref[p, k:k + 1, :] += jnp.sum(dwacc[p, k], axis=0, keepdims=True)
            db_ref[p] += jnp.sum(dbacc[p], axis=0, keepdims=True)

    nxt = lambda i: jnp.minimum((i + 1) * sub, m // FHALO - 1)
    return _call(body, "ffn_conv_bwd",
                 (jax.ShapeDtypeStruct((2, m, dff), BF16), jax.ShapeDtypeStruct((2, FTAPS, dff), F32),
                  jax.ShapeDtypeStruct((2, 1, dff), F32)),
                 (2, nt),
                 [pl.BlockSpec((2, tm, ffs), lambda c, i: (0, i, c)), pl.BlockSpec((2, FHALO, ffs), lambda c, i: (0, nxt(i), c)),
                  pl.BlockSpec((2, tm, ffs), lambda c, i: (0, i, c)),
                  pl.BlockSpec((FTAPS, ffs), lambda c, i: (0, c)), pl.BlockSpec((FTAPS, ffs), lambda c, i: (0, c + 2))],
                 (pl.BlockSpec((2, tm, ffs), lambda c, i: (0, i, c)), pl.BlockSpec((2, FTAPS, ffs), lambda c, i: (0, 0, c)),
                  pl.BlockSpec((2, 1, ffs), lambda c, i: (0, 0, c))),
                 scratch=[pltpu.VMEM((2, tm + FHALO, ffs), F32),
                          pltpu.VMEM((2, FTAPS, 8, ffs), F32), pltpu.VMEM((2, 8, ffs), F32)], side=side)(dup3, dup3, up3, w, w)


def _mm_tn(name, a, b, tk, tn, b_cols=None, chip_out=False):
    m, k = a.shape
    n = b.shape[-1] * (2 if b.ndim == 3 else 1)

    def body(a_ref, b_ref, o_ref):
        o_ref[...] = _dot_tn(a_ref[...], b_ref[...])

    if b.ndim == 3:
        bspec = pl.BlockSpec((None, m, tn), lambda kk, j: (b_cols(j)[0], 0, b_cols(j)[1]))
    else:
        bspec = pl.BlockSpec((m, tn), lambda kk, j: (0, j))
    if chip_out:
        oshape, ospec = (n // tn, k, tn), pl.BlockSpec((None, tk, tn), lambda kk, j: (j, kk, 0))
    else:
        oshape, ospec = (k, n), pl.BlockSpec((tk, tn), lambda kk, j: (kk, j))
    return _call(body, name, jax.ShapeDtypeStruct(oshape, F32), (k // tk, n // tn),
                 [pl.BlockSpec((m, tk), lambda kk, j: (0, kk)), bspec], ospec)(a, b)


def _gate_bwd(drb, wout, ya, yc, z, tm):
    m = drb.shape[0]

    def body(dr_ref, w_ref, ya_ref, yc_ref, ga_ref, gc_ref, dya_ref, dyc_ref, dz_ref, cs_ref):
        i = pl.program_id(0)
        dmix = _dot_nt(dr_ref[...], w_ref[...])
        sa, sc = _sigmoid(ga_ref[...]), _sigmoid(gc_ref[...])
        dya_ref[...] = (dmix * sa).astype(BF16)
        dyc_ref[...] = (dmix * sc).astype(BF16)
        dga = dmix * ya_ref[...] * sa * (1.0 - sa)
        dgc = dmix * yc_ref[...] * sc * (1.0 - sc)
        dz_ref[:, 0:D] = dga.astype(BF16)
        dz_ref[:, D:2 * D] = dgc.astype(BF16)

        @pl.when(i == 0)
        def _():
            cs_ref[...] = jnp.zeros_like(cs_ref)

        cs_ref[:, 0:D] += jnp.sum(dga, axis=0, keepdims=True)
        cs_ref[:, D:2 * D] += jnp.sum(dgc, axis=0, keepdims=True)

    row = lambda i: (i, 0)
    return _call(body, "gate_bwd",
                 (jax.ShapeDtypeStruct((m, D), BF16), jax.ShapeDtypeStruct((m, D), BF16),
                  jax.ShapeDtypeStruct((m, 2 * D), BF16), jax.ShapeDtypeStruct((1, 2 * D), F32)),
                 (m // tm,),
                 [pl.BlockSpec((tm, D), row), pl.BlockSpec((D, D), lambda i: (0, 0)), pl.BlockSpec((tm, D), row),
                  pl.BlockSpec((tm, D), row), pl.BlockSpec((tm, D), lambda i: (i, 0)), pl.BlockSpec((tm, D), lambda i: (i, 1))],
                 (pl.BlockSpec((tm, D), row), pl.BlockSpec((tm, D), row), pl.BlockSpec((tm, 2 * D), row),
                  pl.BlockSpec((1, 2 * D), lambda i: (0, 0))))(drb, wout, ya, yc, z, z)


def _conv_bwd_a(dyc, wcp4, cc, g, b, tm):
    m = cc.shape[0]
    ns = wcp4.shape[2]

    def body(dy_ref, w_ref, cc_ref, g_ref, b_ref, dcc_ref, dg_ref, db_ref, dwb_ref):
        i = pl.program_id(0)
        dcs = jnp.zeros((tm, CW), F32)
        for j in range(NCHIP):
            dcs = dcs + _dot_nt(dy_ref[:, j * ns:(j + 1) * ns], w_ref[j])
        xhat, rstd = _ln_stats(cc_ref[...])
        cl = xhat * g_ref[...] + b_ref[...]
        sg = _sigmoid(cl)
        dcl = dcs * sg * (1.0 + cl * (1.0 - sg))
        dcc = _ln_bwd(dcl, xhat, rstd, g_ref[...])
        dcc_ref[...] = dcc

        @pl.when(i == 0)
        def _():
            dg_ref[...] = jnp.zeros_like(dg_ref)
            db_ref[...] = jnp.zeros_like(db_ref)
            dwb_ref[...] = jnp.zeros_like(dwb_ref)

        dg_ref[...] += jnp.sum(dcl * xhat, axis=0, keepdims=True)
        db_ref[...] += jnp.sum(dcl, axis=0, keepdims=True)
        dwb_ref[...] += jnp.sum(dcc, axis=0, keepdims=True)

    row = lambda i: (i, 0)
    vec = pl.BlockSpec((1, CW), lambda i: (0, 0))
    v = jax.ShapeDtypeStruct((1, CW), F32)
    return _call(body, "conv_bwd_a", (jax.ShapeDtypeStruct((m, CW), F32), v, v, v), (m // tm,),
                 [pl.BlockSpec((tm, D), row), pl.BlockSpec((NCHIP, CW, ns), lambda i: (0, 0, 0)), pl.BlockSpec((tm, CW), row), vec, vec],
                 (pl.BlockSpec((tm, CW), row), vec, vec, vec))(dyc, wcp4, cc, g.reshape(1, CW), b.reshape(1, CW))


def _conv_bwd_b(dcc, z, w, nex, tp, side=None):
    m = dcc.shape[0]
    tm = BLK
    sub = tm // CHALO
    nt = m // tm
    cvc, cgc = C_CONV // CW, C_CONV // CW + 1

    def body(d_ref, dh_ref, cv_ref, cg_ref, cvh_ref, cgh_ref, w_ref, dz_ref, dw_ref, cs_ref, dwin, cwin):
        i = pl.program_id(0)
        rows = i * tm + lax.broadcasted_iota(jnp.int32, (tm, 1), 0)
        hrows = i * tm - CHALO + lax.broadcasted_iota(jnp.int32, (CHALO, 1), 0)
        dwin[0:tm, :] = d_ref[...]
        dwin[tm:tm + CHALO, :] = jnp.where(i == nt - 1, 0.0, dh_ref[...])
        cwin[0:CHALO, :] = _cgate(cvh_ref[...], cgh_ref[...], hrows, nex, tp)
        cwin[CHALO:CHALO + tm, :] = _cgate(cv_ref[...], cg_ref[...], rows, nex, tp)

        @pl.when(i == 0)
        def _():
            dw_ref[...] = jnp.zeros_like(dw_ref)
            cs_ref[...] = jnp.zeros_like(cs_ref)

        for sb in range(sub):
            lo = sb * CHALO
            acc = jnp.zeros((CHALO, CW), F32)
            for k in range(CTAPS):
                acc = acc + w_ref[k:k + 1, :] * dwin[lo + CTAPS - 1 - k:lo + CTAPS - 1 - k + CHALO, :]
            dsb = dwin[lo:lo + CHALO, :]
            for k in range(CTAPS):
                dw_ref[k:k + 1, :] += jnp.sum(dsb * cwin[lo + 2 + k:lo + 2 + k + CHALO, :], axis=0, keepdims=True)
            srows = rows[lo:lo + CHALO]
            dcg = jnp.where(_pad_rows(srows, nex, tp), 0.0, acc)
            cv = cv_ref[lo:lo + CHALO, :]
            sg = _sigmoid(cg_ref[lo:lo + CHALO, :])
            dcv = dcg * sg
            dgt = dcg * cv * sg * (1.0 - sg)
            dz_ref[lo:lo + CHALO, 0:CW] = dcv.astype(BF16)
            dz_ref[lo:lo + CHALO, CW:2 * CW] = dgt.astype(BF16)
            cs_ref[:, 0:CW] += jnp.sum(dcv, axis=0, keepdims=True)
            cs_ref[:, CW:2 * CW] += jnp.sum(dgt, axis=0, keepdims=True)

    nxt = lambda i: jnp.minimum((i + 1) * sub, m // CHALO - 1)
    prv = lambda i: jnp.maximum(i * sub - 1, 0)
    return _call(body, "conv_bwd_b",
                 (jax.ShapeDtypeStruct((m, 2 * CW), BF16), jax.ShapeDtypeStruct((CHALO, CW), F32),
                  jax.ShapeDtypeStruct((1, 2 * CW), F32)),
                 (nt,),
                 [pl.BlockSpec((tm, CW), lambda i: (i, 0)), pl.BlockSpec((CHALO, CW), lambda i: (nxt(i), 0)),
                  pl.BlockSpec((tm, CW), lambda i: (i, cvc)), pl.BlockSpec((tm, CW), lambda i: (i, cgc)),
                  pl.BlockSpec((CHALO, CW), lambda i: (prv(i), cvc)), pl.BlockSpec((CHALO, CW), lambda i: (prv(i), cgc)),
                  pl.BlockSpec((CTAPS, CW), lambda i: (0, 0))],
                 (pl.BlockSpec((tm, 2 * CW), lambda i: (i, 0)), pl.BlockSpec((CHALO, CW), lambda i: (0, 0)),
                  pl.BlockSpec((1, 2 * CW), lambda i: (0, 0))),
                 scratch=[pltpu.VMEM((tm + CHALO, CW), F32), pltpu.VMEM((CHALO + tm, CW), F32)], side=side)(dcc, dcc, z, z, z, z, w)


def _attn_bwd(z, bias, sinks, dya, wap4, nex, nblk, side=None):
    m = z.shape[0]
    tp = nblk * BLK
    ns = wap4.shape[2]
    blk_of = lambda j: nblk - 1 - j

    def body(q_ref, cur_ref, prev_ref, meta_ref, bias_ref, sink_ref, dy_ref, w_ref,
             dz_ref, cs_ref, dsk_ref, dbias_ref, ks, vs, carry, macc, dqacc, dkv, okv):
        bb, j = pl.program_id(0), pl.program_id(1)
        n = nblk - 1 - j
        _fill_kv(ks, vs, prev_ref, cur_ref, meta_ref)
        q = q_ref[...]
        da = jnp.zeros((BLK, AW), F32)
        for jj in range(NCHIP):
            da = da + _dot_nt(dy_ref[:, jj * ns:(jj + 1) * ns], w_ref[jj])

        @pl.when(j == 0)
        def _():
            carry[...] = jnp.zeros_like(carry)
            macc[...] = jnp.zeros_like(macc)

        @pl.when((bb == 0) & (j == 0))
        def _():
            cs_ref[...] = jnp.zeros_like(cs_ref)
            dsk_ref[...] = jnp.zeros_like(dsk_ref)

        @pl.when((j == 0) | (n <= 1))
        def _():
            dbias_ref[...] = jnp.zeros_like(dbias_ref)

        lane = lax.broadcasted_iota(jnp.int32, (1, BLK), 1)
        dsk = jnp.zeros((1, BLK), F32)
        for hk in range(NKV):
            qg, pn, psink = _softmax_group(q, ks[hk], bias_ref, sink_ref, hk)
            dog = jnp.concatenate([da[:, (hk * GRP + g) * HD:(hk * GRP + g + 1) * HD] for g in range(GRP)], axis=0).astype(BF16)
            dp = _dot_nt(dog, vs[hk])
            dl = jnp.sum(pn * dp, axis=-1, keepdims=True)
            ds = pn * (dp - dl)
            dbias_ref[hk * GRP:(hk + 1) * GRP] += ds.reshape(GRP, BLK, NKEY)
            dsr = -psink * dl
            for g in range(GRP):
                dsk = dsk + jnp.where(lane == hk * GRP + g, jnp.sum(dsr[g * BLK:(g + 1) * BLK]), 0.0)
            dsb = (ds * (HD ** -0.5)).astype(BF16)
            dqg = _dot(dsb, ks[hk])
            for g in range(GRP):
                h = hk * GRP + g
                dqacc[:, h * HD:(h + 1) * HD] = dqg[g * BLK:(g + 1) * BLK, :]
            dkv[:, hk * HD:(hk + 1) * HD] = _dot_tn(dsb, qg)
            dkv[:, KVW + hk * HD:KVW + (hk + 1) * HD] = _dot_tn(pn.astype(BF16), dog)
        dsk_ref[...] += dsk
        macc[...] += dkv[2 * BLK:2 * BLK + N_META, :]
        okv[...] = dkv[BLK:2 * BLK, :] + carry[...]
        carry[...] = dkv[0:BLK, :]

        @pl.when(n == 0)
        def _():
            okv[PAD:BLK, :] += macc[...]

        dq = dqacc[...]
        ok = okv[...]
        dz_ref[:, 0:AW] = dq.astype(BF16)
        dz_ref[:, AW:AW + 2 * KVW] = ok.astype(BF16)
        cs_ref[:, 0:AW] += jnp.sum(dq, axis=0, keepdims=True)
        cs_ref[:, AW:AW + 2 * KVW] += jnp.sum(ok, axis=0, keepdims=True)

    wz = AW + 2 * KVW
    specs = _attn_specs(nblk, tp, blk_of) + [
        pl.BlockSpec((BLK, D), lambda bb, j: (bb * nblk + blk_of(j), 0)),
        pl.BlockSpec((NCHIP, AW, ns), lambda bb, j: (0, 0, 0))]
    return _call(body, "attn_bwd",
                 (jax.ShapeDtypeStruct((m, wz), BF16), jax.ShapeDtypeStruct((1, wz), F32), jax.ShapeDtypeStruct((1, BLK), F32),
                  jax.ShapeDtypeStruct((nex, 3, NQ, BLK, NKEY), F32)),
                 (nex, nblk), specs,
                 (pl.BlockSpec((BLK, wz), lambda bb, j: (bb * nblk + blk_of(j), 0)), pl.BlockSpec((1, wz), lambda bb, j: (0, 0)),
                  pl.BlockSpec((1, BLK), lambda bb, j: (0, 0)),
                  pl.BlockSpec((None, None, NQ, BLK, NKEY), lambda bb, j: (bb, jnp.minimum(blk_of(j), 2), 0, 0, 0))),
                 scratch=[pltpu.VMEM((NKV, NKEY, HD), BF16), pltpu.VMEM((NKV, NKEY, HD), BF16),
                          pltpu.VMEM((BLK, 2 * KVW), F32), pltpu.VMEM((N_META, 2 * KVW), F32),
                          pltpu.VMEM((BLK, AW), F32), pltpu.VMEM((NKEY, 2 * KVW), F32), pltpu.VMEM((BLK, 2 * KVW), F32)],
                 side=side)(z, z, z, z, bias, sinks.reshape(1, NQ), dya, wap4)


def _tile_rows(rows, cols, target_bytes=1 << 20):
    best = None
    for t in range(8, rows + 1, 8):
        if rows % t == 0 and t * cols * 4 <= target_bytes:
            best = t
    return best or rows


def _sum0(name, x):
    n, r, c = x.shape
    tr = _tile_rows(r, c * n)

    def body(x_ref, o_ref):
        acc = x_ref[0]
        for k in range(1, n):
            acc = acc + x_ref[k]
        o_ref[...] = acc

    return _call(body, name, jax.ShapeDtypeStruct((r, c), F32), (r // tr,),
                 [pl.BlockSpec((n, tr, c), lambda i: (0, i, 0))], pl.BlockSpec((tr, c), lambda i: (i, 0)))(x)


def _adamw(name, w, g, mom, vel):
    r, c = w.shape
    tr = _tile_rows(r, c)
    c1 = 1.0 / (1.0 - ADAM_B1 ** ADAM_STEP)
    c2 = 1.0 / (1.0 - ADAM_B2 ** ADAM_STEP)

    def body(w_ref, g_ref, m_ref, v_ref, d_ref, mo_ref, vo_ref):
        gg = g_ref[...]
        mn = ADAM_B1 * m_ref[...] + (1.0 - ADAM_B1) * gg
        vn = ADAM_B2 * v_ref[...] + (1.0 - ADAM_B2) * (gg * gg)
        mo_ref[...] = mn
        vo_ref[...] = vn
        d_ref[...] = -ADAM_LR * ((mn * c1) / (jnp.sqrt(vn * c2) + ADAM_EPS) + ADAM_WD * w_ref[...])

    spec = pl.BlockSpec((tr, c), lambda i: (i, 0))
    o = jax.ShapeDtypeStruct((r, c), F32)
    return _call(body, name, (o, o, o), (r // tr,), [spec] * 4, (spec, spec, spec))(w, g, mom, vel)


def _place():
    x, y, c = lax.axis_index("x"), lax.axis_index("y"), lax.axis_index("c")
    others = [(1 - x, y), (x, 1 - y), (1 - x, 1 - y)]
    return x, y, c, others


def _gather_job(items):
    nw = len(items)

    def views(s_ref, g_ref, layer, c):
        if layer is None:
            return s_ref.at[c], lambda chip, cc: g_ref.at[chip, cc]
        hr = s_ref.shape[1] // 2
        return s_ref.at[layer, pl.ds(c * hr, hr)], lambda chip, cc: g_ref.at[chip, pl.ds(cc * hr, hr)]

    def copies(s_refs, g_refs, send, recv):
        x, y, c, others = _place()
        chip = 2 * x + y
        firsts, arrive, passed, arrive2 = [], [], [], []
        for w, (_, layer) in enumerate(items):
            src, dst = views(s_refs[w], g_refs[w], layer, c)
            for k, (px, py) in enumerate(others):
                def rc(kk, s, d, to, w=w):
                    return pltpu.make_async_remote_copy(src_ref=s, dst_ref=d, send_sem=send.at[w, kk], recv_sem=recv.at[w, kk],
                                                        device_id=to, device_id_type=MESH)
                got, got2 = dst(2 * px + py, c), dst(2 * px + py, 1 - c)
                firsts.append(rc(k, src, dst(chip, c), (px, py, c)))
                arrive.append(rc(k, got, got, (x, y, c)))
                passed.append(rc(3 + k, got, got, (x, y, 1 - c)))
                arrive2.append(rc(3 + k, got2, got2, (x, y, c)))
        return firsts, arrive, passed, arrive2

    def start(s_refs, g_refs, send, recv):
        for cp in copies(s_refs, g_refs, send, recv)[0]:
            cp.start()

    def finish(s_refs, g_refs, send, recv):
        firsts, arrive, passed, arrive2 = copies(s_refs, g_refs, send, recv)
        for a, p in zip(arrive, passed):
            a.wait_recv()
            p.start()
        for a in arrive2:
            a.wait_recv()
        for cp in firsts + passed:
            cp.wait_send()

    outs = [jax.ShapeDtypeStruct((NCHIP,) + (s.shape if layer is None else s.shape[1:]), s.dtype) for s, layer in items]
    return _Job([s for s, _ in items], outs, (nw, 6), start, finish)


def _swap_job(grads):
    def copies(d_refs, a_refs, send, recv):
        x, y, c, _ = _place()
        cps = []
        for w in range(len(grads)):
            h = d_refs[w].shape[1] // 2
            cps.append(pltpu.make_async_remote_copy(
                src_ref=d_refs[w].at[:, pl.ds((1 - c) * h, h), :], dst_ref=a_refs[w], send_sem=send.at[w], recv_sem=recv.at[w],
                device_id=(x, y, 1 - c), device_id_type=MESH))
        return cps

    def start(*r):
        for cp in copies(*r):
            cp.start()

    def finish(*r):
        for cp in copies(*r):
            cp.wait()

    outs = [jax.ShapeDtypeStruct((NCHIP, g.shape[1] // 2, g.shape[2]), g.dtype) for g in grads]
    return _Job(list(grads), outs, (len(grads),), start, finish)


def _exchange_job(parts):
    def copies(q_refs, b_refs, send, recv):
        x, y, c, others = _place()
        cps = []
        for w in range(len(parts)):
            for k, (px, py) in enumerate(others):
                cps.append(pltpu.make_async_remote_copy(
                    src_ref=q_refs[w].at[2 * px + py], dst_ref=b_refs[w].at[k], send_sem=send.at[w, k], recv_sem=recv.at[w, k],
                    device_id=(px, py, c), device_id_type=MESH))
        return cps

    def start(*r):
        for cp in copies(*r):
            cp.start()

    def finish(*r):
        for cp in copies(*r):
            cp.wait()

    outs = [jax.ShapeDtypeStruct((3,) + p.shape[1:], p.dtype) for p in parts]
    return _Job(list(parts), outs, (len(parts), 3), start, finish)


def _run_job(name, job):
    n_in, n_out = len(job.ins), len(job.outs)

    def body(*refs):
        ins, outs = refs[:n_in], refs[n_in:n_in + n_out]
        send, recv = refs[n_in + n_out:]
        job.start(ins, outs, send, recv)
        job.finish(ins, outs, send, recv)

    return pl.pallas_call(
        body, name=name, out_shape=tuple(job.outs), in_specs=[_ANY] * n_in, out_specs=tuple([_ANY] * n_out),
        scratch_shapes=[pltpu.SemaphoreType.DMA(job.sems), pltpu.SemaphoreType.DMA(job.sems)])(*job.ins)


def _sibling_join(halves):
    nw = len(halves)

    def body(*refs):
        h_refs = refs[:2 * nw]
        f_refs = refs[2 * nw:3 * nw]
        send, recv = refs[3 * nw:]
        x, y, c, _ = _place()
        cps = []
        for w in range(nw):
            for l in range(2):
                src = h_refs[2 * w + l]
                h = src.shape[0]
                dst = f_refs[w].at[l, pl.ds(c * h, h), :]
                cp = pltpu.make_async_remote_copy(src_ref=src, dst_ref=dst, send_sem=send.at[w, l], recv_sem=recv.at[w, l],
                                                  device_id=(x, y, 1 - c), device_id_type=MESH)
                cp.start()
                cps.append(cp)
        for w in range(nw):
            for l in range(2):
                src = h_refs[2 * w + l]
                h = src.shape[0]
                other = f_refs[w].at[l, pl.ds((1 - c) * h, h), :]
                pltpu.make_async_remote_copy(src_ref=src, dst_ref=other, send_sem=send.at[w, l], recv_sem=recv.at[w, l],
                                             device_id=(x, y, c), device_id_type=MESH).wait_recv()
        for cp in cps:
            cp.wait_send()

    flat = [a for pair in halves for a in pair]
    outs = tuple(jax.ShapeDtypeStruct((2, 2 * pair[0].shape[0], pair[0].shape[1]), F32) for pair in halves)
    return pl.pallas_call(
        body, name="grad_sibling_join", out_shape=outs, in_specs=[_ANY] * (2 * nw), out_specs=tuple([_ANY] * nw),
        scratch_shapes=[pltpu.SemaphoreType.DMA((nw, 2)), pltpu.SemaphoreType.DMA((nw, 2))])(*flat)


def _allgather_small(v):
    r = v.shape[0]

    def body(x_ref, out_ref, send_sems, recv_sems, local_sem):
        x, y, c, chips = _place()
        me, sibling = (x, y, c), (x, y, 1 - c)

        def slab(px, py, pc):
            return out_ref.at[4 * px + 2 * py + pc]

        def copy(k, block, to, src=None):
            return pltpu.make_async_remote_copy(src_ref=slab(*block) if src is None else src, dst_ref=slab(*block),
                                                send_sem=send_sems.at[k], recv_sem=recv_sems.at[k],
                                                device_id=to, device_id_type=MESH)

        mine = pltpu.make_async_copy(x_ref, slab(*me), local_sem)
        mine.start()
        first = [copy(0, me, sibling, src=x_ref)]
        first += [copy(1 + j, me, (*chip, c), src=x_ref) for j, chip in enumerate(chips)]
        for cp in first:
            cp.start()
        passed = [copy(4 + j, (*chip, c), sibling) for j, chip in enumerate(chips)]
        for j, chip in enumerate(chips):
            copy(1 + j, (*chip, c), me).wait_recv()
            passed[j].start()
        copy(0, sibling, me).wait_recv()
        for j, chip in enumerate(chips):
            copy(4 + j, (*chip, 1 - c), me).wait_recv()
        for cp in first + passed:
            cp.wait_send()
        mine.wait()

    return pl.pallas_call(
        body, name="allgather_small", out_shape=jax.ShapeDtypeStruct((8, r, 128), F32),
        in_specs=[pl.BlockSpec(memory_space=pltpu.VMEM)], out_specs=pl.BlockSpec(memory_space=pltpu.VMEM),
        scratch_shapes=[pltpu.SemaphoreType.DMA((7,)), pltpu.SemaphoreType.DMA((7,)), pltpu.SemaphoreType.DMA],
    )(v)


def _add_half(name, d, a, c):
    _, h, cols = a.shape

    def body(c_ref, d_ref, a_ref, o_ref):
        o_ref[...] = (d_ref[...] + a_ref[...]).astype(BF16)

    return _call(body, name, jax.ShapeDtypeStruct(a.shape, BF16), (NCHIP,),
                 [pl.BlockSpec((None, h, cols), lambda p, cr: (p, cr[0], 0)), pl.BlockSpec((None, h, cols), lambda p, cr: (p, 0, 0))],
                 pl.BlockSpec((None, h, cols), lambda p, cr: (p, 0, 0)), prefetch=1)(c, d, a)


def _add_chips(name, d, a, b, where):
    _, h, cols = a.shape
    th = h // 4 if (h % 64 == 0) else h
    nt = h // th

    def body(w_ref, d_ref, a_ref, b_ref, o_ref):
        own = d_ref[...] + a_ref[...]
        o_ref[...] = ((own + b_ref[0].astype(F32)) + b_ref[1].astype(F32)) + b_ref[2].astype(F32)

    return _call(body, name, jax.ShapeDtypeStruct((h, cols), F32), (nt,),
                 [pl.BlockSpec((None, th, cols), lambda i, wr: (wr[0], wr[1] * nt + i, 0)),
                  pl.BlockSpec((None, th, cols), lambda i, wr: (wr[0], i, 0)),
                  pl.BlockSpec((3, th, cols), lambda i, wr: (0, i, 0))],
                 pl.BlockSpec((th, cols), lambda i, wr: (i, 0)), prefetch=1)(where, d, a, b)


def _pack(arrs):
    pieces = []
    for a in arrs:
        f = a.reshape(-1)
        n = -(-f.shape[0] // 1024) * 1024
        pieces.append(jnp.pad(f, (0, n - f.shape[0])).reshape(-1, 128))
    return jnp.concatenate(pieces, axis=0)


def _unpack(buf, shapes):
    out, r = [], 0
    for s in shapes:
        n = int(np.prod(s))
        rows = -(-n // 1024) * 8
        out.append(buf[r:r + rows].reshape(-1)[:n].reshape(s))
        r += rows
    return out


def kernel(x, meta_tokens, in_ln_g, in_ln_b, rel_bias, w_in, b_in, attn_sinks, w_attn_proj, conv_dw, conv_dw_b, conv_ln_g, conv_ln_b, w_conv_proj, w_out, ln1_g, ln1_b, ffn_w_up, ffn_dw, ffn_dw_b, ffn_w_down, ln2_g, ln2_b, loss_target, m_meta_tokens, m_in_ln_g, m_in_ln_b, m_rel_bias, m_w_in, m_b_in, m_attn_sinks, m_w_attn_proj, m_conv_dw, m_conv_dw_b, m_conv_ln_g, m_conv_ln_b, m_w_conv_proj, m_w_out, m_ln1_g, m_ln1_b, m_ffn_w_up, m_ffn_dw, m_ffn_dw_b, m_ffn_w_down, m_ln2_g, m_ln2_b, v_meta_tokens, v_in_ln_g, v_in_ln_b, v_rel_bias, v_w_in, v_b_in, v_attn_sinks, v_w_attn_proj, v_conv_dw, v_conv_dw_b, v_conv_ln_g, v_conv_ln_b, v_w_conv_proj, v_w_out, v_ln1_g, v_ln1_b, v_ffn_w_up, v_ffn_dw, v_ffn_dw_b, v_ffn_w_down, v_ln2_g, v_ln2_b):
    nex, seq, _ = x.shape
    nblk = seq // BLK + 1
    tp = nblk * BLK
    m = nex * tp
    tm = _row_tile(m)
    ffs = ffn_w_up.shape[2]
    dff = 2 * ffs
    cx, cy, cc = lax.axis_index("x"), lax.axis_index("y"), lax.axis_index("c")
    chip = (2 * cx + cy).astype(jnp.int32)
    core = cc.astype(jnp.int32)

    names = ("in", "ap", "cp", "out", "up", "down")
    big = dict(zip(names, [w_in, w_attn_proj, w_conv_proj, w_out, ffn_w_up, ffn_w_down]))
    sb = {k: v.astype(BF16) for k, v in big.items()}
    gathered = {}

    def land(items, outs):
        return [lax.dynamic_update_index_in_dim(g, s if layer is None else s[layer], chip, 0)
                for (s, layer), g in zip(items, outs)]

    first_items = [(sb["in"], 0), (meta_tokens.reshape(2, N_META // 2, -1), None), (conv_dw, None), (ffn_dw, None)]
    gathered[("in", 0)], g_meta, g_cdw, g_fdw = land(first_items, _run_job("gather_first", _gather_job(first_items)))
    meta_full = jnp.transpose(g_meta, (1, 2, 0, 3)).reshape(N_META, D)
    bias_tab = _bias_build(rel_bias)

    fwd_plan = {("in_proj", 0): [("ap", 0), ("cp", 0), ("out", 0)], ("attn_fwd", 0): [("up", 0)],
                ("conv_fwd", 0): [("down", 0)], ("mix_fwd", 0): [("in", 1)],
                ("out_proj_ln", 0): [("ap", 1), ("cp", 1), ("out", 1)], ("ffn_up", 0): [("up", 1)],
                ("ffn_act", 0): [("down", 1)]}

    def fwd(tag, l, fn, *args):
        keys = fwd_plan.get((tag, l))
        if not keys:
            return fn(*args)
        items = [(sb[k], kl) for k, kl in keys]
        res = fn(*args, side=_gather_job(items))
        for key, g in zip(keys, land(items, res[-len(keys):])):
            gathered[key] = g
        main = res[:-len(keys)]
        return main[0] if len(main) == 1 else main

    def layer_weights(l):
        win_old = jnp.transpose(gathered[("in", l)], (1, 0, 2)).reshape(D, IN_COLS)
        return dict(
            win=_to_new(win_old), bin=_to_new(b_in[l]).reshape(1, IN_COLS),
            cdw=jnp.transpose(g_cdw[:, l], (1, 0, 2)).reshape(CTAPS, CW),
            fdw=jnp.transpose(g_fdw[:, l], (1, 0, 2)).reshape(FTAPS, 2 * dff),
            fdwb=ffn_dw_b[l].reshape(1, 2 * dff))

    raw, h, hb = _embed_ln(x, meta_full, in_ln_g, in_ln_b, nblk)
    saved, lw = [], []
    for l in range(DEPTH):
        p = layer_weights(l)
        z = fwd("in_proj", l, functools.partial(_mm_bias, "in_proj"), hb, p["win"], p["bin"], IN_COLS // 3, tm)
        a = fwd("attn_fwd", l, _attn_fwd, z, bias_tab, attn_sinks[l], nex, nblk)
        ccv, cs = fwd("conv_fwd", l, _conv_fwd, z, p["cdw"], conv_dw_b[l], conv_ln_g[l], conv_ln_b[l], nex, tp)
        p["wap"], p["wcp"] = gathered[("ap", l)], gathered[("cp", l)]
        ya, yc, mixed = fwd("mix_fwd", l, _mix_fwd, a, cs, p["wap"], p["wcp"], z, tm)
        p["wout"] = gathered[("out", l)].reshape(D, D)
        r1, h1, h1b = fwd("out_proj_ln", l, functools.partial(_mm_res_ln, "out_proj_ln"), mixed, p["wout"], h, ln1_g[l], ln1_b[l], tm)
        p["wup"] = gathered[("up", l)]
        up3 = fwd("ffn_up", l, _ffn_up, h1b, p["wup"], tm, nex, tp)
        act = fwd("ffn_act", l, _ffn_act, up3, p["fdw"], p["fdwb"], tm)
        p["wdown"] = gathered[("down", l)].reshape(dff, D)
        r2, h2, h2b = _mm_res_ln("ffn_down_ln", act, p["wdown"], h1, ln2_g[l], ln2_b[l], tm)
        saved.append(dict(hb=hb, z=z, a=a, cc=ccv, cs=cs, ya=ya, yc=yc, mixed=mixed, r1=r1, h1b=h1b, up3=up3, act=act, r2=r2))
        lw.append(p)
        h, hb = h2, h2b

    dy, sq = _loss_grad(h, loss_target, nblk)
    loss = lax.psum(0.5 / D * jnp.sum(sq), ("x", "y", "c"))

    grads, swapped, pair_sums, reduced = {}, {}, {}, {}
    cvec, where = core.reshape(1), jnp.stack([chip, core])
    last = [(k, DEPTH - 1) for k in names]
    bwd_plan = {("ln2_bwd", 0): ("swap", last),
                ("ffn_act_bwd", 0): ("exch", [("up", 1), ("down", 1)]),
                ("ffn_conv_bwd", 0): ("exch", [("in", 1), ("ap", 1), ("cp", 1), ("out", 1)]),
                ("ln1_bwd", 0): ("swap", [("down", 0), ("up", 0)]),
                ("conv_bwd_b", 0): ("swap", [("out", 0), ("ap", 0), ("cp", 0)]),
                ("attn_bwd", 0): ("exch", [("down", 0), ("up", 0), ("out", 0), ("ap", 0), ("cp", 0)]),
                ("in_ln_bwd", 0): ("swap", [("in", 0)])}

    def after(kind, keys, outs):
        for key, o in zip(keys, outs):
            if kind == "swap":
                swapped[key] = o
                pair_sums[key] = _add_half("grad_add_sibling", grads[key], o, cvec)
            else:
                reduced[key] = _add_chips("grad_add_chips", grads[key], swapped[key], o, where)

    def bwd(tag, l, fn, *args):
        plan = bwd_plan.get((tag, l))
        if plan is None:
            return fn(*args)
        kind, keys = plan
        job = _swap_job([grads[k] for k in keys]) if kind == "swap" else _exchange_job([pair_sums[k] for k in keys])
        res = fn(*args, side=job)
        after(kind, keys, res[-len(keys):])
        main = res[:-len(keys)]
        return main[0] if len(main) == 1 else main

    small = {}
    prev_a, prev_w, prev_cols = (), None, None
    dprev = dy
    for l in reversed(range(DEPTH)):
        p, s = lw[l], saved[l]
        dr2, dr2b, dg2, db2 = bwd("ln2_bwd", l, functools.partial(_ln_bwd_call, "ln2_bwd"), dprev, s["r2"], ln2_g[l], tm,
                                  prev_a, prev_w, prev_cols)
        dup3 = bwd("ffn_act_bwd", l, _ffn_act_bwd, dr2b, p["wdown"], s["up3"], p["fdw"], p["fdwb"], tm)
        dpre3, dfdw, dfdwb = bwd("ffn_conv_bwd", l, _ffn_conv_bwd, dup3, s["up3"], p["fdw"], tm, nex, tp)
        grads[("down", l)] = _mm_tn("dw_down", s["act"], dr2b, ffs, D // 2).reshape(NCHIP, dff // NCHIP, D)
        grads[("up", l)] = _mm_tn("dw_up", s["h1b"], dpre3, D, ffs, b_cols=lambda j: (j // 2, j % 2), chip_out=True)
        dr1, dr1b, dg1, db1 = bwd("ln1_bwd", l, functools.partial(_ln_bwd_call, "ln1_bwd"), dr2, s["r1"], ln1_g[l], tm // 2,
                                  (dpre3,), p["wup"])
        dya, dyc, dzg, csg = _gate_bwd(dr1b, p["wout"], s["ya"], s["yc"], s["z"], tm)
        grads[("out", l)] = _mm_tn("dw_out", s["mixed"], dr1b, D, D // 2).reshape(NCHIP, D // NCHIP, D)
        grads[("ap", l)] = _mm_tn("dw_attn_proj", s["a"], dya, AW, D // NCHIP, chip_out=True)
        grads[("cp", l)] = _mm_tn("dw_conv_proj", s["cs"], dyc, CW, D // NCHIP, chip_out=True)
        dcc, dclg, dclb, dcwb = _conv_bwd_a(dyc, p["wcp"], s["cc"], conv_ln_g[l], conv_ln_b[l], tm)
        dzc, dcdw, csc = bwd("conv_bwd_b", l, _conv_bwd_b, dcc, s["z"], p["cdw"], nex, tp)
        dzq, csq, dsk, dbias = bwd("attn_bwd", l, _attn_bwd, s["z"], bias_tab, attn_sinks[l], dya, p["wap"], nex, nblk)
        gin = [_mm_tn("dw_in_gates", s["hb"], dzg, D, D // 2), _mm_tn("dw_in_conv", s["hb"], dzc, D, CW),
               _mm_tn("dw_in_qkv", s["hb"], dzq, D, 2 * KVW)]
        gin_old = _to_old(jnp.concatenate(gin, axis=1))
        grads[("in", l)] = jnp.transpose(gin_old.reshape(D, NCHIP, IN_COLS // NCHIP), (1, 0, 2))
        small[l] = dict(
            b_in=_to_old(jnp.concatenate([csg, csc, csq], axis=1)).reshape(IN_COLS), attn_sinks=dsk[0, :NQ],
            conv_dw=dcdw[:CTAPS], conv_dw_b=dcwb.reshape(CW), conv_ln_g=dclg.reshape(CW), conv_ln_b=dclb.reshape(CW),
            ln1_g=dg1.reshape(D), ln1_b=db1.reshape(D),
            ffn_dw=jnp.transpose(dfdw, (1, 0, 2)).reshape(FTAPS, 2 * dff), ffn_dw_b=jnp.transpose(dfdwb, (1, 0, 2)).reshape(2 * dff),
            ln2_g=dg2.reshape(D), ln2_b=db2.reshape(D), bias=_bias_grad(dbias))
        dprev = dr1
        prev_a, prev_w, prev_cols = (dzg, dzc, dzq), p["win"], [(C_GATES, C_CONV), (C_CONV, C_QKV), (C_QKV, IN_COLS)]
    draw, _, dg0, db0 = bwd("in_ln_bwd", 0, functools.partial(_ln_bwd_call, "in_ln_bwd"), dprev, raw, in_ln_g, tm,
                            prev_a, prev_w, prev_cols)
    draw3 = draw.reshape(nex, tp, D)
    grad_x = draw3[:, BLK:, :]
    dmeta = _sum0("meta_grad_sum", draw3[:, PAD:BLK, :])

    names_l = ["b_in", "attn_sinks", "conv_dw", "conv_dw_b", "conv_ln_g", "conv_ln_b", "ln1_g", "ln1_b", "ffn_dw", "ffn_dw_b", "ln2_g", "ln2_b"]
    part_list = [dmeta, dg0.reshape(D), db0.reshape(D), small[0]["bias"] + small[1]["bias"]]
    part_list += [jnp.stack([small[0][n], small[1][n]]) for n in names_l]
    shapes_small = [tuple(a.shape) for a in part_list]
    tot = _sum0("small_grad_sum", _allgather_small(_pack(part_list)))
    (g_meta_f, g_inlg, g_inlb, g_biasp, g_bin, g_sinks, g_cdw_f, g_cdwb, g_clg, g_clb, g_l1g, g_l1b, g_fdw_f, g_fdwb,
     g_l2g, g_l2b) = _unpack(tot, shapes_small)
    g_relb = g_biasp
    csh = D // NCHIP
    g_meta_s = lax.dynamic_slice_in_dim(g_meta_f, chip * csh, csh, axis=1)
    g_cdw_s = lax.dynamic_slice_in_dim(g_cdw_f, chip * (CW // NCHIP), CW // NCHIP, axis=2)
    g_fdw_s = lax.dynamic_slice_in_dim(g_fdw_f, chip * ffs, ffs, axis=2)

    tail = [("in", 0)]
    after("exch", tail, _run_job("grad_chip_exchange", _exchange_job([pair_sums[k] for k in tail])))
    joined = _sibling_join([[reduced[(k, l)] for l in range(DEPTH)] for k in names])
    full = []
    for k, f in zip(names, joined):
        hh = reduced[(k, 0)].shape[0]
        for l in range(DEPTH):
            f = lax.dynamic_update_slice(f, reduced[(k, l)][None], (l, core * hh, 0))
        full.append(f)

    moms = [m_w_in, m_w_attn_proj, m_w_conv_proj, m_w_out, m_ffn_w_up, m_ffn_w_down]
    vels = [v_w_in, v_w_attn_proj, v_w_conv_proj, v_w_out, v_ffn_w_up, v_ffn_w_down]
    big_out = []
    for w, g, mo, ve in zip(big.values(), full, moms, vels):
        sh = w.shape
        two = lambda t: t.reshape(sh[0] * sh[1], sh[2])
        d_, m_, v_ = _adamw("adamw_matrix", two(w), two(g), two(mo), two(ve))
        big_out.append((g.reshape(sh), d_.reshape(sh), m_.reshape(sh), v_.reshape(sh)))

    sm_w = [meta_tokens, in_ln_g, in_ln_b, rel_bias, b_in, attn_sinks, conv_dw, conv_dw_b, conv_ln_g, conv_ln_b, ln1_g, ln1_b,
            ffn_dw, ffn_dw_b, ln2_g, ln2_b]
    sm_m = [m_meta_tokens, m_in_ln_g, m_in_ln_b, m_rel_bias, m_b_in, m_attn_sinks, m_conv_dw, m_conv_dw_b, m_conv_ln_g, m_conv_ln_b,
            m_ln1_g, m_ln1_b, m_ffn_dw, m_ffn_dw_b, m_ln2_g, m_ln2_b]
    sm_v = [v_meta_tokens, v_in_ln_g, v_in_ln_b, v_rel_bias, v_b_in, v_attn_sinks, v_conv_dw, v_conv_dw_b, v_conv_ln_g, v_conv_ln_b,
            v_ln1_g, v_ln1_b, v_ffn_dw, v_ffn_dw_b, v_ln2_g, v_ln2_b]
    sm_g = [g_meta_s, g_inlg, g_inlb, g_relb, g_bin, g_sinks, g_cdw_s, g_cdwb, g_clg, g_clb, g_l1g, g_l1b, g_fdw_s, g_fdwb, g_l2g, g_l2b]
    sm_shapes = [tuple(a.shape) for a in sm_w]
    sd, smn, svn = _adamw("adamw_small", _pack(sm_w), _pack(sm_g), _pack(sm_m), _pack(sm_v))
    sd, smn, svn = _unpack(sd, sm_shapes), _unpack(smn, sm_shapes), _unpack(svn, sm_shapes)

    order = ["meta_tokens", "in_ln_g", "in_ln_b", "rel_bias", "w_in", "b_in", "attn_sinks", "w_attn_proj", "conv_dw", "conv_dw_b",
             "conv_ln_g", "conv_ln_b", "w_conv_proj", "w_out", "ln1_g", "ln1_b", "ffn_w_up", "ffn_dw", "ffn_dw_b", "ffn_w_down",
             "ln2_g", "ln2_b"]
    small_names = ["meta_tokens", "in_ln_g", "in_ln_b", "rel_bias", "b_in", "attn_sinks", "conv_dw", "conv_dw_b", "conv_ln_g",
                   "conv_ln_b", "ln1_g", "ln1_b", "ffn_dw", "ffn_dw_b", "ln2_g", "ln2_b"]
    big_names = ["w_in", "w_attn_proj", "w_conv_proj", "w_out", "ffn_w_up", "ffn_w_down"]
    res = {}
    for i, n in enumerate(small_names):
        res[n] = (sm_g[i], sd[i], smn[i], svn[i])
    for i, n in enumerate(big_names):
        res[n] = big_out[i]
    outs = [loss, grad_x]
    for k in range(4):
        outs += [res[n][k] for n in order]
    return tuple(outs)
```

```python
import functools
import math
from typing import Any, Callable, NamedTuple, Sequence

import numpy as np
import jax
import jax.numpy as jnp
from jax import lax
from jax.experimental import pallas as pl
from jax.experimental.pallas import tpu as pltpu

F32 = jnp.float32
BF16 = jnp.bfloat16
MESH = pl.DeviceIdType.MESH

D = 1024
N_META = 16
BLK = 128
PAD = BLK - N_META
HD = 64
NQ = 8
NKV = 2
GRP = NQ // NKV
AW = NQ * HD
KVW = NKV * HD
CW = D // 2
CTAPS = 31
FTAPS = 3
NBUCKET = 32
MAXDIST = 128
EPS = 1e-5
DEPTH = 2
ALPHA = (2.0 * DEPTH) ** 0.25
NCHIP = 4
NKEY = 3 * BLK
NEG = -1e30
CHALO = 32
FHALO = 8
IN_COLS = AW + 2 * KVW + 2 * CW + 2 * D
_OLD = dict(q=(0, AW), k=(AW, AW + KVW), v=(AW + KVW, AW + 2 * KVW), cv=(AW + 2 * KVW, AW + 2 * KVW + CW),
            cg=(AW + 2 * KVW + CW, AW + 2 * KVW + 2 * CW), ga=(AW + 2 * KVW + 2 * CW, AW + 2 * KVW + 2 * CW + D),
            gc=(AW + 2 * KVW + 2 * CW + D, IN_COLS))
_NEW_ORDER = ("ga", "gc", "cv", "cg", "q", "k", "v")
C_GATES, C_CONV, C_QKV = 0, 2 * D, 2 * D + 2 * CW

ADAM_LR, ADAM_B1, ADAM_B2, ADAM_EPS, ADAM_WD, ADAM_STEP = 0.001, 0.9, 0.999, 1e-08, 0.01, 10


def _to_new(a):
    return jnp.concatenate([a[..., _OLD[n][0]:_OLD[n][1]] for n in _NEW_ORDER], axis=-1)


def _to_old(a):
    offs, o = {}, 0
    for n in _NEW_ORDER:
        w = _OLD[n][1] - _OLD[n][0]
        offs[n] = (o, o + w)
        o += w
    return jnp.concatenate([a[..., offs[n][0]:offs[n][1]] for n in ("q", "k", "v", "cv", "cg", "ga", "gc")], axis=-1)


class _Job(NamedTuple):
    ins: Sequence[Any]
    outs: Sequence[Any]
    sems: tuple
    start: Callable
    finish: Callable


_ANY = pl.BlockSpec(memory_space=pl.ANY)


def _call(body, name, out_shape, grid, in_specs, out_specs, scratch=(), prefetch=0, side=None):
    params = pltpu.CompilerParams(dimension_semantics=("arbitrary",) * len(grid))
    if side is None:
        if prefetch:
            gs = pltpu.PrefetchScalarGridSpec(num_scalar_prefetch=prefetch, grid=grid, in_specs=in_specs,
                                              out_specs=out_specs, scratch_shapes=list(scratch))
            return pl.pallas_call(body, name=name, out_shape=out_shape, grid_spec=gs, compiler_params=params)
        return pl.pallas_call(body, name=name, out_shape=out_shape, grid=grid, in_specs=in_specs, out_specs=out_specs,
                              scratch_shapes=list(scratch), compiler_params=params)
    assert not prefetch
    single = not isinstance(out_shape, (tuple, list))
    main_shapes = (out_shape,) if single else tuple(out_shape)
    main_specs = (out_specs,) if single else tuple(out_specs)
    n_in, n_sin, n_out, n_sout, n_scr = len(in_specs), len(side.ins), len(main_shapes), len(side.outs), len(scratch)

    def wrapped(*refs):
        main_in, sin = refs[:n_in], refs[n_in:n_in + n_sin]
        o0 = n_in + n_sin
        main_out, sout = refs[o0:o0 + n_out], refs[o0 + n_out:o0 + n_out + n_sout]
        s0 = o0 + n_out + n_sout
        main_scr, (send, recv) = refs[s0:s0 + n_scr], refs[s0 + n_scr:]
        first = functools.reduce(lambda a, b: a & b, [pl.program_id(k) == 0 for k in range(len(grid))])
        last = functools.reduce(lambda a, b: a & b, [pl.program_id(k) == grid[k] - 1 for k in range(len(grid))])

        @pl.when(first)
        def _():
            side.start(sin, sout, send, recv)

        body(*main_in, *main_out, *main_scr)

        @pl.when(last)
        def _():
            side.finish(sin, sout, send, recv)

    call = pl.pallas_call(
        wrapped, name=name, out_shape=main_shapes + tuple(side.outs), grid=grid,
        in_specs=list(in_specs) + [_ANY] * n_sin, out_specs=main_specs + tuple([_ANY] * n_sout),
        scratch_shapes=list(scratch) + [pltpu.SemaphoreType.DMA(side.sems), pltpu.SemaphoreType.DMA(side.sems)],
        compiler_params=params)
    return lambda *args: call(*args, *side.ins)


def _row_tile(m):
    best = 32
    for t in range(32, 641, 32):
        if m % t == 0:
            best = t
    return best


def _pad_rows(rows, nex, tp):
    m = rows < PAD
    for b in range(1, nex):
        m = m | ((rows >= b * tp) & (rows < b * tp + PAD))
    return m


def _ln_stats(x):
    mu = jnp.mean(x, axis=-1, keepdims=True)
    xc = x - mu
    var = jnp.mean(xc * xc, axis=-1, keepdims=True)
    rstd = lax.rsqrt(var + EPS)
    return xc * rstd, rstd


def _ln_bwd(dy, xhat, rstd, g):
    dxh = dy * g
    m1 = jnp.mean(dxh, axis=-1, keepdims=True)
    m2 = jnp.mean(dxh * xhat, axis=-1, keepdims=True)
    return rstd * (dxh - m1 - xhat * m2)


def _dot(a, b):
    return jnp.dot(a, b, preferred_element_type=F32)


def _dot_nt(a, b):
    return lax.dot_general(a, b, (((1,), (1,)), ((), ())), preferred_element_type=F32)


def _dot_tn(a, b):
    return lax.dot_general(a, b, (((0,), (0,)), ((), ())), preferred_element_type=F32)


def _sigmoid(x):
    return 1.0 / (1.0 + jnp.exp(-x))


def _gelu_parts(g):
    cdf = 0.5 * (1.0 + lax.erf(g * (1.0 / math.sqrt(2.0))))
    pdf = jnp.exp(-0.5 * g * g) * (1.0 / math.sqrt(2.0 * math.pi))
    return g * cdf, cdf + g * pdf


def _bucket_np(d):
    n = np.maximum(d, 0)
    max_exact = NBUCKET // 2
    nf = np.maximum(n, 1).astype(np.float32)
    large = max_exact + (np.log(nf / np.float32(max_exact)) / np.float32(math.log(MAXDIST / max_exact))
                         * np.float32(NBUCKET - max_exact)).astype(np.int32)
    large = np.minimum(large, NBUCKET - 1)
    return np.where(n < max_exact, n, large).astype(np.int32)


def _bias_index():
    i = np.arange(BLK)[:, None]
    j = np.arange(2 * BLK)[None, :]
    d = BLK + i - j
    band_ok = (d >= 0) & (d < BLK)
    band = _bucket_np(d)
    idx = np.full((3, BLK, NKEY), -1, np.int32)
    m = np.arange(N_META)[None, :]
    d0 = (i - PAD) - m
    idx[0, :, 2 * BLK:2 * BLK + N_META] = np.where(d0 >= 0, _bucket_np(d0), -1)
    ok1 = band_ok & (j >= BLK)
    idx[1, :, :2 * BLK] = np.where(ok1, band, -1)
    idx[1, :, 2 * BLK:2 * BLK + N_META] = _bucket_np((N_META + i) - m)
    idx[2, :, :2 * BLK] = np.where(band_ok, band, -1)
    idx[2, :, 2 * BLK:2 * BLK + N_META] = NBUCKET - 1
    return idx


def _bias_build(rel_bias):
    idx = jnp.asarray(_bias_index())

    def body(idx_ref, rb_ref, o_ref):
        ix = idx_ref[...]
        for h in range(NQ):
            acc = jnp.full(ix.shape, NEG, F32)
            for b in range(NBUCKET):
                acc = jnp.where(ix == b, rb_ref[b, h], acc)
            o_ref[:, h, :, :] = acc

    return pl.pallas_call(
        body, name="bias_build", out_shape=jax.ShapeDtypeStruct((3, NQ, BLK, NKEY), F32),
        in_specs=[pl.BlockSpec(memory_space=pltpu.VMEM), pl.BlockSpec(memory_space=pltpu.SMEM)],
        out_specs=pl.BlockSpec(memory_space=pltpu.VMEM))(idx, rel_bias)


def _bias_grad(dbias):
    idx = jnp.asarray(_bias_index())

    def body(idx_ref, d_ref, o_ref):
        d = jnp.sum(d_ref[...], axis=0)
        for b in range(NBUCKET):
            acc = jnp.zeros((NQ, NKEY), F32)
            for case in range(3):
                hit = (idx_ref[case] == b)[None, :, :]
                acc = acc + jnp.sum(jnp.where(hit, d[case], 0.0), axis=1)
            o_ref[b] = jnp.sum(acc, axis=-1, keepdims=True)

    out = pl.pallas_call(
        body, name="bias_grad", out_shape=jax.ShapeDtypeStruct((NBUCKET, NQ, 1), F32),
        in_specs=[pl.BlockSpec(memory_space=pltpu.VMEM), pl.BlockSpec(memory_space=pltpu.VMEM)],
        out_specs=pl.BlockSpec(memory_space=pltpu.VMEM))(idx, dbias)
    return out.reshape(NBUCKET, NQ)


def _embed_ln(x, meta, g, b, nblk):
    nex, seq, _ = x.shape
    m = nex * nblk * BLK

    def body(x_ref, meta_ref, g_ref, b_ref, raw_ref, h_ref, hb_ref):
        j = pl.program_id(1)

        @pl.when(j == 0)
        def _():
            raw_ref[0:PAD, :] = jnp.zeros((PAD, D), F32)
            raw_ref[PAD:BLK, :] = meta_ref[...]

        @pl.when(j > 0)
        def _():
            raw_ref[...] = x_ref[...]

        xhat, _ = _ln_stats(raw_ref[...])
        y = xhat * g_ref[...] + b_ref[...]
        h_ref[...] = y
        hb_ref[...] = y.astype(BF16)

    row = lambda bb, j: (bb * nblk + j, 0)
    return _call(
        body, "embed_ln",
        (jax.ShapeDtypeStruct((m, D), F32), jax.ShapeDtypeStruct((m, D), F32), jax.ShapeDtypeStruct((m, D), BF16)),
        (nex, nblk),
        [pl.BlockSpec((None, BLK, D), lambda bb, j: (bb, jnp.maximum(j - 1, 0), 0)),
         pl.BlockSpec((N_META, D), lambda bb, j: (0, 0)),
         pl.BlockSpec((1, D), lambda bb, j: (0, 0)), pl.BlockSpec((1, D), lambda bb, j: (0, 0))],
        (pl.BlockSpec((BLK, D), row), pl.BlockSpec((BLK, D), row), pl.BlockSpec((BLK, D), row)),
    )(x, meta, g.reshape(1, D), b.reshape(1, D))


def _mm_bias(name, a, w, bias, tn, tm, side=None):
    m, k = a.shape
    n = w.shape[1]

    def body(a_ref, w_ref, b_ref, o_ref):
        o_ref[...] = _dot(a_ref[...], w_ref[...]) + b_ref[...]

    return _call(body, name, jax.ShapeDtypeStruct((m, n), F32), (n // tn, m // tm),
                 [pl.BlockSpec((tm, k), lambda j, i: (i, 0)), pl.BlockSpec((k, tn), lambda j, i: (0, j)),
                  pl.BlockSpec((1, tn), lambda j, i: (0, j))],
                 pl.BlockSpec((tm, tn), lambda j, i: (i, j)), side=side)(a, w, bias)


def _ffn_up(a, w4, tm, nex, tp, side=None):
    m, k = a.shape
    ffs = w4.shape[2]

    def body(a_ref, w_ref, o_ref):
        i = pl.program_id(1)
        rows = i * tm + lax.broadcasted_iota(jnp.int32, (tm, 1), 0)
        acc = _dot(a_ref[...], w_ref[...])
        o_ref[...] = jnp.where(_pad_rows(rows, nex, tp), 0.0, acc)

    return _call(body, "ffn_up", jax.ShapeDtypeStruct((2, m, 2 * ffs), F32), (NCHIP, m // tm),
                 [pl.BlockSpec((tm, k), lambda j, i: (i, 0)), pl.BlockSpec((None, k, ffs), lambda j, i: (j, 0, 0))],
                 pl.BlockSpec((None, tm, ffs), lambda j, i: (j // 2, i, j % 2)), side=side)(a, w4)


def _fill_kv(ks, vs, prev_ref, cur_ref, meta_ref):
    for piece, lo, n in ((prev_ref, 0, BLK), (cur_ref, BLK, BLK), (meta_ref, 2 * BLK, N_META)):
        val = piece[...]
        for hk in range(NKV):
            ks[hk, lo:lo + n, :] = val[:, hk * HD:(hk + 1) * HD].astype(BF16)
            vs[hk, lo:lo + n, :] = val[:, KVW + hk * HD:KVW + (hk + 1) * HD].astype(BF16)
    for hk in range(NKV):
        ks[hk, 2 * BLK + N_META:NKEY, :] = jnp.zeros((BLK - N_META, HD), BF16)
        vs[hk, 2 * BLK + N_META:NKEY, :] = jnp.zeros((BLK - N_META, HD), BF16)


def _softmax_group(q, ks_hk, bias_ref, sink_ref, hk):
    qg = jnp.concatenate([q[:, (hk * GRP + g) * HD:(hk * GRP + g + 1) * HD] for g in range(GRP)], axis=0).astype(BF16)
    s = _dot_nt(qg, ks_hk) * (HD ** -0.5) + bias_ref[hk * GRP:(hk + 1) * GRP].reshape(GRP * BLK, NKEY)
    sink = jnp.concatenate([jnp.full((BLK, 1), sink_ref[0, hk * GRP + g], F32) for g in range(GRP)], axis=0)
    mx = jnp.maximum(jnp.max(s, axis=-1, keepdims=True), sink)
    p = jnp.exp(s - mx)
    es = jnp.exp(sink - mx)
    inv = 1.0 / (jnp.sum(p, axis=-1, keepdims=True) + es)
    return qg, p * inv, es * inv


def _attn_specs(nblk, tp, blk_of):
    qcol, kvcol = (C_QKV) // AW, (C_QKV + AW) // (2 * KVW)
    return [
        pl.BlockSpec((BLK, AW), lambda bb, j: (bb * nblk + blk_of(j), qcol)),
        pl.BlockSpec((BLK, 2 * KVW), lambda bb, j: (bb * nblk + blk_of(j), kvcol)),
        pl.BlockSpec((BLK, 2 * KVW), lambda bb, j: (bb * nblk + jnp.maximum(blk_of(j) - 1, 0), kvcol)),
        pl.BlockSpec((N_META, 2 * KVW), lambda bb, j: (bb * (tp // N_META) + PAD // N_META, kvcol)),
        pl.BlockSpec((None, NQ, BLK, NKEY), lambda bb, j: (jnp.minimum(blk_of(j), 2), 0, 0, 0)),
        pl.BlockSpec(memory_space=pltpu.SMEM),
    ]


def _attn_fwd(z, bias, sinks, nex, nblk, side=None):
    m = z.shape[0]
    tp = nblk * BLK

    def body(q_ref, cur_ref, prev_ref, meta_ref, bias_ref, sink_ref, o_ref, ks, vs, oacc):
        _fill_kv(ks, vs, prev_ref, cur_ref, meta_ref)
        q = q_ref[...]
        for hk in range(NKV):
            _, pn, _ = _softmax_group(q, ks[hk], bias_ref, sink_ref, hk)
            o = _dot(pn.astype(BF16), vs[hk])
            for g in range(GRP):
                h = hk * GRP + g
                oacc[:, h * HD:(h + 1) * HD] = o[g * BLK:(g + 1) * BLK, :]
        o_ref[...] = oacc[...].astype(BF16)

    return _call(body, "attn_fwd", jax.ShapeDtypeStruct((m, AW), BF16), (nex, nblk),
                 _attn_specs(nblk, tp, lambda j: j),
                 pl.BlockSpec((BLK, AW), lambda bb, j: (bb * nblk + j, 0)),
                 scratch=[pltpu.VMEM((NKV, NKEY, HD), BF16), pltpu.VMEM((NKV, NKEY, HD), BF16),
                          pltpu.VMEM((BLK, AW), F32)], side=side)(z, z, z, z, bias, sinks.reshape(1, NQ))


def _cgate(cv, cg, rows, nex, tp):
    return jnp.where(_pad_rows(rows, nex, tp), 0.0, cv * _sigmoid(cg))


CLANES = 256


def _rolled_up(blk, b):
    return blk if b == 0 else pltpu.roll(blk, blk.shape[0] - b, axis=0)


def _conv_fwd(z, w, wb, g, b, nex, tp, side=None):
    m = z.shape[0]
    tm = BLK
    sub = tm // CHALO
    cvc, cgc = C_CONV // CW, C_CONV // CW + 1

    def body(cv_ref, cg_ref, cvh_ref, cgh_ref, w_ref, wb_ref, g_ref, b_ref, cc_ref, cs_ref, win):
        i = pl.program_id(0)
        rows = i * tm + lax.broadcasted_iota(jnp.int32, (tm, 1), 0)
        hrows = i * tm - CHALO + lax.broadcasted_iota(jnp.int32, (CHALO, 1), 0)
        win[0:CHALO, :] = _cgate(cvh_ref[...], cgh_ref[...], hrows, nex, tp)
        win[CHALO:CHALO + tm, :] = _cgate(cv_ref[...], cg_ref[...], rows, nex, tp)
        for sb in range(sub):
            lo = sb * CHALO
            for c0 in range(0, CW, CLANES):
                blk = win[lo:lo + 2 * CHALO, c0:c0 + CLANES]
                acc = jnp.zeros((CHALO, CLANES), F32) + wb_ref[:, c0:c0 + CLANES]
                for b in range(8):
                    rb = _rolled_up(blk, b)
                    for a in range(5):
                        s = 8 * a + b
                        if 2 <= s <= CTAPS + 1:
                            acc = acc + w_ref[s - 2:s - 1, c0:c0 + CLANES] * rb[8 * a:8 * a + CHALO]
                cc_ref[lo:lo + CHALO, c0:c0 + CLANES] = acc
        xhat, _ = _ln_stats(cc_ref[...])
        cl = xhat * g_ref[...] + b_ref[...]
        cs_ref[...] = (cl * _sigmoid(cl)).astype(BF16)

    halo = lambda i: jnp.maximum(i * sub - 1, 0)
    vec = pl.BlockSpec((1, CW), lambda i: (0, 0))
    return _call(body, "conv_fwd", (jax.ShapeDtypeStruct((m, CW), F32), jax.ShapeDtypeStruct((m, CW), BF16)),
                 (m // tm,),
                 [pl.BlockSpec((tm, CW), lambda i: (i, cvc)), pl.BlockSpec((tm, CW), lambda i: (i, cgc)),
                  pl.BlockSpec((CHALO, CW), lambda i: (halo(i), cvc)), pl.BlockSpec((CHALO, CW), lambda i: (halo(i), cgc)),
                  pl.BlockSpec((CTAPS, CW), lambda i: (0, 0)), vec, vec, vec],
                 (pl.BlockSpec((tm, CW), lambda i: (i, 0)), pl.BlockSpec((tm, CW), lambda i: (i, 0))),
                 scratch=[pltpu.VMEM((CHALO + tm, CW), F32)], side=side)(z, z, z, z, w, wb.reshape(1, CW), g.reshape(1, CW), b.reshape(1, CW))


def _mix_fwd(a, cs, wap4, wcp4, z, tm, side=None):
    m = a.shape[0]
    ns = wap4.shape[2]

    def body(a_ref, cs_ref, wa_ref, wc_ref, ga_ref, gc_ref, ya_ref, yc_ref, mix_ref):
        av, cv = a_ref[...], cs_ref[...]
        for j in range(NCHIP):
            ya_ref[:, j * ns:(j + 1) * ns] = _dot(av, wa_ref[j])
            yc_ref[:, j * ns:(j + 1) * ns] = _dot(cv, wc_ref[j])
        mix_ref[...] = (_sigmoid(ga_ref[...]) * ya_ref[...] + _sigmoid(gc_ref[...]) * yc_ref[...]).astype(BF16)

    wspec = pl.BlockSpec((NCHIP, AW, ns), lambda i: (0, 0, 0))
    row = lambda i: (i, 0)
    return _call(body, "mix_fwd",
                 (jax.ShapeDtypeStruct((m, D), F32), jax.ShapeDtypeStruct((m, D), F32), jax.ShapeDtypeStruct((m, D), BF16)),
                 (m // tm,),
                 [pl.BlockSpec((tm, AW), row), pl.BlockSpec((tm, CW), row), wspec, wspec,
                  pl.BlockSpec((tm, D), lambda i: (i, 0)), pl.BlockSpec((tm, D), lambda i: (i, 1))],
                 (pl.BlockSpec((tm, D), row), pl.BlockSpec((tm, D), row), pl.BlockSpec((tm, D), row)), side=side)(a, cs, wap4, wcp4, z, z)


def _mm_res_ln(name, a, w, res, g, b, tm, side=None):
    m, k = a.shape

    def body(a_ref, w_ref, res_ref, g_ref, b_ref, r_ref, h_ref, hb_ref):
        r = ALPHA * res_ref[...] + _dot(a_ref[...], w_ref[...])
        r_ref[...] = r
        xhat, _ = _ln_stats(r)
        y = xhat * g_ref[...] + b_ref[...]
        h_ref[...] = y
        hb_ref[...] = y.astype(BF16)

    row = lambda i: (i, 0)
    vec = pl.BlockSpec((1, D), lambda i: (0, 0))
    return _call(body, name,
                 (jax.ShapeDtypeStruct((m, D), F32), jax.ShapeDtypeStruct((m, D), F32), jax.ShapeDtypeStruct((m, D), BF16)),
                 (m // tm,),
                 [pl.BlockSpec((tm, k), row), pl.BlockSpec((k, D), lambda i: (0, 0)), pl.BlockSpec((tm, D), row), vec, vec],
                 (pl.BlockSpec((tm, D), row), pl.BlockSpec((tm, D), row), pl.BlockSpec((tm, D), row)), side=side)(a, w, res, g.reshape(1, D), b.reshape(1, D))


RCH = 16


def _lane_groups(width, most=768):
    n = -(-width // most)
    step = -(-width // (128 * n)) * 128
    return [(c, min(c + step, width)) for c in range(0, width, step)]


def _conv3(win, p, r0, c0, c1, w_ref, b_ref):
    blk = win[p, r0:r0 + FHALO + RCH, c0:c1]
    x0, x1, x2 = blk[FHALO:], pltpu.roll(blk, 1, axis=0)[FHALO:], pltpu.roll(blk, 2, axis=0)[FHALO:]
    return b_ref[:, c0:c1] + w_ref[0:1, c0:c1] * x2 + w_ref[1:2, c0:c1] * x1 + w_ref[2:3, c0:c1] * x0


def _ffn_specs(tm, ffs, m):
    sub = tm // FHALO
    return [
        pl.BlockSpec((2, tm, ffs), lambda c, i: (0, i, c)),
        pl.BlockSpec((2, FHALO, ffs), lambda c, i: (0, jnp.maximum(i * sub - 1, 0), c)),
        pl.BlockSpec((FTAPS, ffs), lambda c, i: (0, c)), pl.BlockSpec((FTAPS, ffs), lambda c, i: (0, c + 2)),
        pl.BlockSpec((1, ffs), lambda c, i: (0, c)), pl.BlockSpec((1, ffs), lambda c, i: (0, c + 2)),
    ]


def _ffn_act(up3, w, wb, tm, side=None):
    _, m, dff = up3.shape
    ffs = dff // 2

    def body(x_ref, xh_ref, wu_ref, wg_ref, bu_ref, bg_ref, o_ref, win):
        win[:, 0:FHALO, :] = xh_ref[...]
        win[:, FHALO:FHALO + tm, :] = x_ref[...]
        for r0 in range(0, tm, RCH):
            for c0, c1 in _lane_groups(ffs):
                u = _conv3(win, 0, r0, c0, c1, wu_ref, bu_ref)
                g = _conv3(win, 1, r0, c0, c1, wg_ref, bg_ref)
                o_ref[r0:r0 + RCH, c0:c1] = (g * (0.5 * (1.0 + lax.erf(g * (1.0 / math.sqrt(2.0))))) * u).astype(BF16)

    return _call(body, "ffn_act", jax.ShapeDtypeStruct((m, dff), BF16), (2, m // tm),
                 _ffn_specs(tm, ffs, m), pl.BlockSpec((tm, ffs), lambda c, i: (i, c)),
                 scratch=[pltpu.VMEM((2, FHALO + tm, ffs), F32)], side=side)(up3, up3, w, w, wb, wb)


def _loss_grad(y, target, nblk):
    nex = target.shape[0]
    m = y.shape[0]

    def body(y_ref, t_ref, dy_ref, acc_ref):
        bb, j = pl.program_id(0), pl.program_id(1)

        @pl.when((bb == 0) & (j == 0))
        def _():
            acc_ref[...] = jnp.zeros_like(acc_ref)

        @pl.when(j == 0)
        def _():
            dy_ref[...] = jnp.zeros_like(dy_ref)

        @pl.when(j > 0)
        def _():
            e = y_ref[...] - t_ref[...]
            dy_ref[...] = e * (1.0 / D)
            acc_ref[...] += jnp.sum((e * e).reshape(BLK // 8, 8, D), axis=0)

    return _call(body, "loss_grad", (jax.ShapeDtypeStruct((m, D), F32), jax.ShapeDtypeStruct((8, D), F32)), (nex, nblk),
                 [pl.BlockSpec((BLK, D), lambda bb, j: (bb * nblk + j, 0)),
                  pl.BlockSpec((None, BLK, D), lambda bb, j: (bb, jnp.maximum(j - 1, 0), 0))],
                 (pl.BlockSpec((BLK, D), lambda bb, j: (bb * nblk + j, 0)), pl.BlockSpec((8, D), lambda bb, j: (0, 0))))(y, target)


def _ln_bwd_call(name, dy, r, g, tm, a_list=(), w=None, cols=None, side=None):
    m = dy.shape[0]
    na = len(a_list)

    def body(*refs):
        dy_ref, r_ref, g_ref = refs[0:3]
        a_refs = refs[3:3 + na]
        w_ref = refs[3 + na] if na else None
        dr_ref, drb_ref, dg_ref, db_ref = refs[-4:]
        i = pl.program_id(0)
        dh = dy_ref[...]
        if na:
            dh = ALPHA * dh
            if cols is None:
                ns = w.shape[2]
                for j in range(NCHIP):
                    dh = dh + _dot_nt(a_refs[0][j // 2, :, (j % 2) * ns:(j % 2 + 1) * ns], w_ref[j])
            else:
                for a_ref, (c0, c1) in zip(a_refs, cols):
                    dh = dh + _dot_nt(a_ref[...], w_ref[:, c0:c1])
        xhat, rstd = _ln_stats(r_ref[...])
        dr = _ln_bwd(dh, xhat, rstd, g_ref[...])
        dr_ref[...] = dr
        drb_ref[...] = dr.astype(BF16)

        @pl.when(i == 0)
        def _():
            dg_ref[...] = jnp.zeros_like(dg_ref)
            db_ref[...] = jnp.zeros_like(db_ref)

        dg_ref[...] += jnp.sum(dh * xhat, axis=0, keepdims=True)
        db_ref[...] += jnp.sum(dh, axis=0, keepdims=True)

    row = lambda i: (i, 0)
    vec = pl.BlockSpec((1, D), lambda i: (0, 0))
    in_specs = [pl.BlockSpec((tm, D), row), pl.BlockSpec((tm, D), row), vec]
    for a in a_list:
        in_specs.append(pl.BlockSpec((2, tm, a.shape[2]), lambda i: (0, i, 0)) if a.ndim == 3 else pl.BlockSpec((tm, a.shape[1]), row))
    if na:
        in_specs.append(pl.BlockSpec(w.shape, (lambda i: (0, 0, 0)) if w.ndim == 3 else (lambda i: (0, 0))))
    return _call(body, name,
                 (jax.ShapeDtypeStruct((m, D), F32), jax.ShapeDtypeStruct((m, D), BF16),
                  jax.ShapeDtypeStruct((1, D), F32), jax.ShapeDtypeStruct((1, D), F32)),
                 (m // tm,), in_specs,
                 (pl.BlockSpec((tm, D), row), pl.BlockSpec((tm, D), row), vec, vec),
                 side=side)(dy, r, g.reshape(1, D), *a_list, *([w] if na else []))


def _ffn_act_bwd(drb, wdown, up3, w, wb, tm, side=None):
    _, m, dff = up3.shape
    ffs = dff // 2

    def body(dr_ref, wd_ref, x_ref, xh_ref, wu_ref, wg_ref, bu_ref, bg_ref, o_ref, win, dact):
        win[:, 0:FHALO, :] = xh_ref[...]
        win[:, FHALO:FHALO + tm, :] = x_ref[...]
        dact[...] = _dot_nt(dr_ref[...], wd_ref[...])
        for r0 in range(0, tm, RCH):
            for c0, c1 in _lane_groups(ffs):
                u = _conv3(win, 0, r0, c0, c1, wu_ref, bu_ref)
                g = _conv3(win, 1, r0, c0, c1, wg_ref, bg_ref)
                da = dact[r0:r0 + RCH, c0:c1]
                cdf = 0.5 * (1.0 + lax.erf(g * (1.0 / math.sqrt(2.0))))
                pdf = jnp.exp(-0.5 * g * g) * (1.0 / math.sqrt(2.0 * math.pi))
                o_ref[0, r0:r0 + RCH, c0:c1] = da * (g * cdf)
                o_ref[1, r0:r0 + RCH, c0:c1] = da * u * (cdf + g * pdf)

    specs = [pl.BlockSpec((tm, D), lambda c, i: (i, 0)), pl.BlockSpec((ffs, D), lambda c, i: (c, 0))] + _ffn_specs(tm, ffs, m)
    return _call(body, "ffn_act_bwd", jax.ShapeDtypeStruct((2, m, dff), F32), (2, m // tm), specs,
                 pl.BlockSpec((2, tm, ffs), lambda c, i: (0, i, c)),
                 scratch=[pltpu.VMEM((2, FHALO + tm, ffs), F32), pltpu.VMEM((tm, ffs), F32)],
                 side=side)(drb, wdown, up3, up3, w, w, wb, wb)


def _ffn_conv_bwd(dup3, up3, w, tm, nex, tp, side=None):
    _, m, dff = up3.shape
    ffs = dff // 2
    sub = tm // FHALO
    nt = m // tm

    def body(d_ref, dh_ref, x_ref, wu_ref, wg_ref, o_ref, dw_ref, db_ref, dwin, dwacc, dbacc):
        i = pl.program_id(1)
        dwin[:, 0:tm, :] = d_ref[...]
        dwin[:, tm:tm + FHALO, :] = jnp.where(i == nt - 1, 0.0, dh_ref[...])
        dwacc[...] = jnp.zeros_like(dwacc)
        dbacc[...] = jnp.zeros_like(dbacc)

        @pl.when(i == 0)
        def _():
            dw_ref[...] = jnp.zeros_like(dw_ref)
            db_ref[...] = jnp.zeros_like(db_ref)

        fold = lambda t: t[0:8, :] + t[8:16, :]
        for r0 in range(0, tm, RCH):
            rows = i * tm + r0 + lax.broadcasted_iota(jnp.int32, (RCH, 1), 0)
            pad = _pad_rows(rows, nex, tp)
            for p, w_ref in ((0, wu_ref), (1, wg_ref)):
                for c0, c1 in _lane_groups(ffs):
                    dblk = dwin[p, r0:r0 + RCH + FHALO, c0:c1]
                    d0 = dblk[:RCH]
                    d1 = pltpu.roll(dblk, RCH + FHALO - 1, axis=0)[:RCH]
                    d2 = pltpu.roll(dblk, RCH + FHALO - 2, axis=0)[:RCH]
                    dpre = w_ref[2:3, c0:c1] * d0 + w_ref[1:2, c0:c1] * d1 + w_ref[0:1, c0:c1] * d2
                    o_ref[p, r0:r0 + RCH, c0:c1] = jnp.where(pad, 0.0, dpre).astype(BF16)
                    x0 = x_ref[p, r0:r0 + RCH, c0:c1]
                    dwacc[p, 2, :, c0:c1] += fold(d0 * x0)
                    dwacc[p, 1, :, c0:c1] += fold(d1 * x0)
                    dwacc[p, 0, :, c0:c1] += fold(d2 * x0)
                    dbacc[p, :, c0:c1] += fold(d0)
        for p in range(2):
            for k in range(FTAPS):
                dw_ref[p, k:k + 1, :] += jnp.sum(dwacc[p, k], axis=0, keepdims=True)
            db_ref[p] += jnp.sum(dbacc[p], axis=0, keepdims=True)

    nxt = lambda i: jnp.minimum((i + 1) * sub, m // FHALO - 1)
    return _call(body, "ffn_conv_bwd",
                 (jax.ShapeDtypeStruct((2, m, dff), BF16), jax.ShapeDtypeStruct((2, FTAPS, dff), F32),
                  jax.ShapeDtypeStruct((2, 1, dff), F32)),
                 (2, nt),
                 [pl.BlockSpec((2, tm, ffs), lambda c, i: (0, i, c)), pl.BlockSpec((2, FHALO, ffs), lambda c, i: (0, nxt(i), c)),
                  pl.BlockSpec((2, tm, ffs), lambda c, i: (0, i, c)),
                  pl.BlockSpec((FTAPS, ffs), lambda c, i: (0, c)), pl.BlockSpec((FTAPS, ffs), lambda c, i: (0, c + 2))],
                 (pl.BlockSpec((2, tm, ffs), lambda c, i: (0, i, c)), pl.BlockSpec((2, FTAPS, ffs), lambda c, i: (0, 0, c)),
                  pl.BlockSpec((2, 1, ffs), lambda c, i: (0, 0, c))),
                 scratch=[pltpu.VMEM((2, tm + FHALO, ffs), F32),
                          pltpu.VMEM((2, FTAPS, 8, ffs), F32), pltpu.VMEM((2, 8, ffs), F32)], side=side)(dup3, dup3, up3, w, w)


def _mm_tn(name, a, b, tk, tn, b_cols=None, chip_out=False):
    m, k = a.shape
    n = b.shape[-1] * (2 if b.ndim == 3 else 1)

    def body(a_ref, b_ref, o_ref):
        o_ref[...] = _dot_tn(a_ref[...], b_ref[...])

    if b.ndim == 3:
        bspec = pl.BlockSpec((None, m, tn), lambda kk, j: (b_cols(j)[0], 0, b_cols(j)[1]))
    else:
        bspec = pl.BlockSpec((m, tn), lambda kk, j: (0, j))
    if chip_out:
        oshape, ospec = (n // tn, k, tn), pl.BlockSpec((None, tk, tn), lambda kk, j: (j, kk, 0))
    else:
        oshape, ospec = (k, n), pl.BlockSpec((tk, tn), lambda kk, j: (kk, j))
    return _call(body, name, jax.ShapeDtypeStruct(oshape, F32), (k // tk, n // tn),
                 [pl.BlockSpec((m, tk), lambda kk, j: (0, kk)), bspec], ospec)(a, b)


def _gate_bwd(drb, wout, ya, yc, z, tm):
    m = drb.shape[0]

    def body(dr_ref, w_ref, ya_ref, yc_ref, ga_ref, gc_ref, dya_ref, dyc_ref, dz_ref, cs_ref):
        i = pl.program_id(0)
        dmix = _dot_nt(dr_ref[...], w_ref[...])
        sa, sc = _sigmoid(ga_ref[...]), _sigmoid(gc_ref[...])
        dya_ref[...] = (dmix * sa).astype(BF16)
        dyc_ref[...] = (dmix * sc).astype(BF16)
        dga = dmix * ya_ref[...] * sa * (1.0 - sa)
        dgc = dmix * yc_ref[...] * sc * (1.0 - sc)
        dz_ref[:, 0:D] = dga.astype(BF16)
        dz_ref[:, D:2 * D] = dgc.astype(BF16)

        @pl.when(i == 0)
        def _():
            cs_ref[...] = jnp.zeros_like(cs_ref)

        cs_ref[:, 0:D] += jnp.sum(dga, axis=0, keepdims=True)
        cs_ref[:, D:2 * D] += jnp.sum(dgc, axis=0, keepdims=True)

    row = lambda i: (i, 0)
    return _call(body, "gate_bwd",
                 (jax.ShapeDtypeStruct((m, D), BF16), jax.ShapeDtypeStruct((m, D), BF16),
                  jax.ShapeDtypeStruct((m, 2 * D), BF16), jax.ShapeDtypeStruct((1, 2 * D), F32)),
                 (m // tm,),
                 [pl.BlockSpec((tm, D), row), pl.BlockSpec((D, D), lambda i: (0, 0)), pl.BlockSpec((tm, D), row),
                  pl.BlockSpec((tm, D), row), pl.BlockSpec((tm, D), lambda i: (i, 0)), pl.BlockSpec((tm, D), lambda i: (i, 1))],
                 (pl.BlockSpec((tm, D), row), pl.BlockSpec((tm, D), row), pl.BlockSpec((tm, 2 * D), row),
                  pl.BlockSpec((1, 2 * D), lambda i: (0, 0))))(drb, wout, ya, yc, z, z)


def _conv_bwd_a(dyc, wcp4, cc, g, b, tm):
    m = cc.shape[0]
    ns = wcp4.shape[2]

    def body(dy_ref, w_ref, cc_ref, g_ref, b_ref, dcc_ref, dg_ref, db_ref, dwb_ref):
        i = pl.program_id(0)
        dcs = jnp.zeros((tm, CW), F32)
        for j in range(NCHIP):
            dcs = dcs + _dot_nt(dy_ref[:, j * ns:(j + 1) * ns], w_ref[j])
        xhat, rstd = _ln_stats(cc_ref[...])
        cl = xhat * g_ref[...] + b_ref[...]
        sg = _sigmoid(cl)
        dcl = dcs * sg * (1.0 + cl * (1.0 - sg))
        dcc = _ln_bwd(dcl, xhat, rstd, g_ref[...])
        dcc_ref[...] = dcc

        @pl.when(i == 0)
        def _():
            dg_ref[...] = jnp.zeros_like(dg_ref)
            db_ref[...] = jnp.zeros_like(db_ref)
            dwb_ref[...] = jnp.zeros_like(dwb_ref)

        dg_ref[...] += jnp.sum(dcl * xhat, axis=0, keepdims=True)
        db_ref[...] += jnp.sum(dcl, axis=0, keepdims=True)
        dwb_ref[...] += jnp.sum(dcc, axis=0, keepdims=True)

    row = lambda i: (i, 0)
    vec = pl.BlockSpec((1, CW), lambda i: (0, 0))
    v = jax.ShapeDtypeStruct((1, CW), F32)
    return _call(body, "conv_bwd_a", (jax.ShapeDtypeStruct((m, CW), F32), v, v, v), (m // tm,),
                 [pl.BlockSpec((tm, D), row), pl.BlockSpec((NCHIP, CW, ns), lambda i: (0, 0, 0)), pl.BlockSpec((tm, CW), row), vec, vec],
                 (pl.BlockSpec((tm, CW), row), vec, vec, vec))(dyc, wcp4, cc, g.reshape(1, CW), b.reshape(1, CW))


def _conv_bwd_b(dcc, z, w, nex, tp, side=None):
    m = dcc.shape[0]
    tm = BLK
    sub = tm // CHALO
    nt = m // tm
    cvc, cgc = C_CONV // CW, C_CONV // CW + 1

    def body(d_ref, dh_ref, cv_ref, cg_ref, w_ref, dz_ref, dw_ref, cs_ref, dwin, dwacc):
        i = pl.program_id(0)
        rows = i * tm + lax.broadcasted_iota(jnp.int32, (tm, 1), 0)
        dwin[0:tm, :] = d_ref[...]
        dwin[tm:tm + CHALO, :] = jnp.where(i == nt - 1, 0.0, dh_ref[...])

        @pl.when(i == 0)
        def _():
            dwacc[...] = jnp.zeros_like(dwacc)
            cs_ref[...] = jnp.zeros_like(cs_ref)

        fold = lambda t: (t[0:8] + t[8:16]) + (t[16:24] + t[24:32])
        for sb in range(sub):
            lo = sb * CHALO
            pad = _pad_rows(rows[lo:lo + CHALO], nex, tp)
            for c0 in range(0, CW, CLANES):
                cs_ = slice(c0, c0 + CLANES)
                cv = cv_ref[lo:lo + CHALO, cs_]
                sg = _sigmoid(cg_ref[lo:lo + CHALO, cs_])
                cgin = jnp.where(pad, 0.0, cv * sg)
                blk = dwin[lo:lo + 2 * CHALO, cs_]
                acc = jnp.zeros((CHALO, CLANES), F32)
                for b in range(8):
                    rb = _rolled_up(blk, b)
                    for a in range(4):
                        s = 8 * a + b
                        if s <= CTAPS - 1:
                            k = CTAPS - 1 - s
                            sh = rb[8 * a:8 * a + CHALO]
                            acc = acc + w_ref[k:k + 1, cs_] * sh
                            dwacc[k, :, cs_] += fold(sh * cgin)
                dcg = jnp.where(pad, 0.0, acc)
                dcv = dcg * sg
                dgt = dcg * cv * sg * (1.0 - sg)
                dz_ref[lo:lo + CHALO, cs_] = dcv.astype(BF16)
                dz_ref[lo:lo + CHALO, CW + c0:CW + c0 + CLANES] = dgt.astype(BF16)
                cs_ref[:, cs_] += jnp.sum(dcv, axis=0, keepdims=True)
                cs_ref[:, CW + c0:CW + c0 + CLANES] += jnp.sum(dgt, axis=0, keepdims=True)

        @pl.when(i == nt - 1)
        def _():
            for k in range(CHALO):
                dw_ref[k:k + 1, :] = jnp.sum(dwacc[k], axis=0, keepdims=True)

    nxt = lambda i: jnp.minimum((i + 1) * sub, m // CHALO - 1)
    return _call(body, "conv_bwd_b",
                 (jax.ShapeDtypeStruct((m, 2 * CW), BF16), jax.ShapeDtypeStruct((CHALO, CW), F32),
                  jax.ShapeDtypeStruct((1, 2 * CW), F32)),
                 (nt,),
                 [pl.BlockSpec((tm, CW), lambda i: (i, 0)), pl.BlockSpec((CHALO, CW), lambda i: (nxt(i), 0)),
                  pl.BlockSpec((tm, CW), lambda i: (i, cvc)), pl.BlockSpec((tm, CW), lambda i: (i, cgc)),
                  pl.BlockSpec((CTAPS, CW), lambda i: (0, 0))],
                 (pl.BlockSpec((tm, 2 * CW), lambda i: (i, 0)), pl.BlockSpec((CHALO, CW), lambda i: (0, 0)),
                  pl.BlockSpec((1, 2 * CW), lambda i: (0, 0))),
                 scratch=[pltpu.VMEM((tm + CHALO, CW), F32), pltpu.VMEM((CHALO, 8, CW), F32)], side=side)(dcc, dcc, z, z, w)


def _attn_bwd(z, bias, sinks, dya, wap4, nex, nblk, side=None):
    m = z.shape[0]
    tp = nblk * BLK
    ns = wap4.shape[2]
    blk_of = lambda j: nblk - 1 - j

    def body(q_ref, cur_ref, prev_ref, meta_ref, bias_ref, sink_ref, dy_ref, w_ref,
             dz_ref, cs_ref, dsk_ref, dbias_ref, ks, vs, carry, macc, dqacc, dkv, okv):
        bb, j = pl.program_id(0), pl.program_id(1)
        n = nblk - 1 - j
        _fill_kv(ks, vs, prev_ref, cur_ref, meta_ref)
        q = q_ref[...]
        da = jnp.zeros((BLK, AW), F32)
        for jj in range(NCHIP):
            da = da + _dot_nt(dy_ref[:, jj * ns:(jj + 1) * ns], w_ref[jj])

        @pl.when(j == 0)
        def _():
            carry[...] = jnp.zeros_like(carry)
            macc[...] = jnp.zeros_like(macc)

        @pl.when((bb == 0) & (j == 0))
        def _():
            cs_ref[...] = jnp.zeros_like(cs_ref)
            dsk_ref[...] = jnp.zeros_like(dsk_ref)

        @pl.when((j == 0) | (n <= 1))
        def _():
            dbias_ref[...] = jnp.zeros_like(dbias_ref)

        lane = lax.broadcasted_iota(jnp.int32, (1, BLK), 1)
        dsk = jnp.zeros((1, BLK), F32)
        for hk in range(NKV):
            qg, pn, psink = _softmax_group(q, ks[hk], bias_ref, sink_ref, hk)
            dog = jnp.concatenate([da[:, (hk * GRP + g) * HD:(hk * GRP + g + 1) * HD] for g in range(GRP)], axis=0).astype(BF16)
            dp = _dot_nt(dog, vs[hk])
            dl = jnp.sum(pn * dp, axis=-1, keepdims=True)
            ds = pn * (dp - dl)
            dbias_ref[hk * GRP:(hk + 1) * GRP] += ds.reshape(GRP, BLK, NKEY)
            dsr = -psink * dl
            for g in range(GRP):
                dsk = dsk + jnp.where(lane == hk * GRP + g, jnp.sum(dsr[g * BLK:(g + 1) * BLK]), 0.0)
            dsb = (ds * (HD ** -0.5)).astype(BF16)
            dqg = _dot(dsb, ks[hk])
            for g in range(GRP):
                h = hk * GRP + g
                dqacc[:, h * HD:(h + 1) * HD] = dqg[g * BLK:(g + 1) * BLK, :]
            dkv[:, hk * HD:(hk + 1) * HD] = _dot_tn(dsb, qg)
            dkv[:, KVW + hk * HD:KVW + (hk + 1) * HD] = _dot_tn(pn.astype(BF16), dog)
        dsk_ref[...] += dsk
        macc[...] += dkv[2 * BLK:2 * BLK + N_META, :]
        okv[...] = dkv[BLK:2 * BLK, :] + carry[...]
        carry[...] = dkv[0:BLK, :]

        @pl.when(n == 0)
        def _():
            okv[PAD:BLK, :] += macc[...]

        dq = dqacc[...]
        ok = okv[...]
        dz_ref[:, 0:AW] = dq.astype(BF16)
        dz_ref[:, AW:AW + 2 * KVW] = ok.astype(BF16)
        cs_ref[:, 0:AW] += jnp.sum(dq, axis=0, keepdims=True)
        cs_ref[:, AW:AW + 2 * KVW] += jnp.sum(ok, axis=0, keepdims=True)

    wz = AW + 2 * KVW
    specs = _attn_specs(nblk, tp, blk_of) + [
        pl.BlockSpec((BLK, D), lambda bb, j: (bb * nblk + blk_of(j), 0)),
        pl.BlockSpec((NCHIP, AW, ns), lambda bb, j: (0, 0, 0))]
    return _call(body, "attn_bwd",
                 (jax.ShapeDtypeStruct((m, wz), BF16), jax.ShapeDtypeStruct((1, wz), F32), jax.ShapeDtypeStruct((1, BLK), F32),
                  jax.ShapeDtypeStruct((nex, 3, NQ, BLK, NKEY), F32)),
                 (nex, nblk), specs,
                 (pl.BlockSpec((BLK, wz), lambda bb, j: (bb * nblk + blk_of(j), 0)), pl.BlockSpec((1, wz), lambda bb, j: (0, 0)),
                  pl.BlockSpec((1, BLK), lambda bb, j: (0, 0)),
                  pl.BlockSpec((None, None, NQ, BLK, NKEY), lambda bb, j: (bb, jnp.minimum(blk_of(j), 2), 0, 0, 0))),
                 scratch=[pltpu.VMEM((NKV, NKEY, HD), BF16), pltpu.VMEM((NKV, NKEY, HD), BF16),
                          pltpu.VMEM((BLK, 2 * KVW), F32), pltpu.VMEM((N_META, 2 * KVW), F32),
                          pltpu.VMEM((BLK, AW), F32), pltpu.VMEM((NKEY, 2 * KVW), F32), pltpu.VMEM((BLK, 2 * KVW), F32)],
                 side=side)(z, z, z, z, bias, sinks.reshape(1, NQ), dya, wap4)


def _tile_rows(rows, cols, target_bytes=1 << 20):
    best = None
    for t in range(8, rows + 1, 8):
        if rows % t == 0 and t * cols * 4 <= target_bytes:
            best = t
    return best or rows


def _sum0(name, x):
    n, r, c = x.shape
    tr = _tile_rows(r, c * n)

    def body(x_ref, o_ref):
        acc = x_ref[0]
        for k in range(1, n):
            acc = acc + x_ref[k]
        o_ref[...] = acc

    return _call(body, name, jax.ShapeDtypeStruct((r, c), F32), (r // tr,),
                 [pl.BlockSpec((n, tr, c), lambda i: (0, i, 0))], pl.BlockSpec((tr, c), lambda i: (i, 0)))(x)


def _adamw(name, w, g, mom, vel):
    r, c = w.shape
    tr = _tile_rows(r, c)
    c1 = 1.0 / (1.0 - ADAM_B1 ** ADAM_STEP)
    c2 = 1.0 / (1.0 - ADAM_B2 ** ADAM_STEP)

    def body(w_ref, g_ref, m_ref, v_ref, d_ref, mo_ref, vo_ref):
        gg = g_ref[...]
        mn = ADAM_B1 * m_ref[...] + (1.0 - ADAM_B1) * gg
        vn = ADAM_B2 * v_ref[...] + (1.0 - ADAM_B2) * (gg * gg)
        mo_ref[...] = mn
        vo_ref[...] = vn
        d_ref[...] = -ADAM_LR * ((mn * c1) / (jnp.sqrt(vn * c2) + ADAM_EPS) + ADAM_WD * w_ref[...])

    spec = pl.BlockSpec((tr, c), lambda i: (i, 0))
    o = jax.ShapeDtypeStruct((r, c), F32)
    return _call(body, name, (o, o, o), (r // tr,), [spec] * 4, (spec, spec, spec))(w, g, mom, vel)


def _place():
    x, y, c = lax.axis_index("x"), lax.axis_index("y"), lax.axis_index("c")
    others = [(1 - x, y), (x, 1 - y), (1 - x, 1 - y)]
    return x, y, c, others


def _gather_job(items):
    nw = len(items)

    def views(s_ref, g_ref, layer, c):
        if layer is None:
            return s_ref.at[c], lambda chip, cc: g_ref.at[chip, cc]
        hr = s_ref.shape[1] // 2
        return s_ref.at[layer, pl.ds(c * hr, hr)], lambda chip, cc: g_ref.at[chip, pl.ds(cc * hr, hr)]

    def copies(s_refs, g_refs, send, recv):
        x, y, c, others = _place()
        chip = 2 * x + y
        firsts, arrive, passed, arrive2 = [], [], [], []
        for w, (_, layer) in enumerate(items):
            src, dst = views(s_refs[w], g_refs[w], layer, c)
            for k, (px, py) in enumerate(others):
                def rc(kk, s, d, to, w=w):
                    return pltpu.make_async_remote_copy(src_ref=s, dst_ref=d, send_sem=send.at[w, kk], recv_sem=recv.at[w, kk],
                                                        device_id=to, device_id_type=MESH)
                got, got2 = dst(2 * px + py, c), dst(2 * px + py, 1 - c)
                firsts.append(rc(k, src, dst(chip, c), (px, py, c)))
                arrive.append(rc(k, got, got, (x, y, c)))
                passed.append(rc(3 + k, got, got, (x, y, 1 - c)))
                arrive2.append(rc(3 + k, got2, got2, (x, y, c)))
        return firsts, arrive, passed, arrive2

    def start(s_refs, g_refs, send, recv):
        for cp in copies(s_refs, g_refs, send, recv)[0]:
            cp.start()

    def finish(s_refs, g_refs, send, recv):
        firsts, arrive, passed, arrive2 = copies(s_refs, g_refs, send, recv)
        for a, p in zip(arrive, passed):
            a.wait_recv()
            p.start()
        for a in arrive2:
            a.wait_recv()
        for cp in firsts + passed:
            cp.wait_send()

    outs = [jax.ShapeDtypeStruct((NCHIP,) + (s.shape if layer is None else s.shape[1:]), s.dtype) for s, layer in items]
    return _Job([s for s, _ in items], outs, (nw, 6), start, finish)


def _swap_job(grads):
    def copies(d_refs, a_refs, send, recv):
        x, y, c, _ = _place()
        cps = []
        for w in range(len(grads)):
            h = d_refs[w].shape[1] // 2
            cps.append(pltpu.make_async_remote_copy(
                src_ref=d_refs[w].at[:, pl.ds((1 - c) * h, h), :], dst_ref=a_refs[w], send_sem=send.at[w], recv_sem=recv.at[w],
                device_id=(x, y, 1 - c), device_id_type=MESH))
        return cps

    def start(*r):
        for cp in copies(*r):
            cp.start()

    def finish(*r):
        for cp in copies(*r):
            cp.wait()

    outs = [jax.ShapeDtypeStruct((NCHIP, g.shape[1] // 2, g.shape[2]), g.dtype) for g in grads]
    return _Job(list(grads), outs, (len(grads),), start, finish)


def _exchange_job(parts):
    def copies(q_refs, b_refs, send, recv):
        x, y, c, others = _place()
        cps = []
        for w in range(len(parts)):
            for k, (px, py) in enumerate(others):
                cps.append(pltpu.make_async_remote_copy(
                    src_ref=q_refs[w].at[2 * px + py], dst_ref=b_refs[w].at[k], send_sem=send.at[w, k], recv_sem=recv.at[w, k],
                    device_id=(px, py, c), device_id_type=MESH))
        return cps

    def start(*r):
        for cp in copies(*r):
            cp.start()

    def finish(*r):
        for cp in copies(*r):
            cp.wait()

    outs = [jax.ShapeDtypeStruct((3,) + p.shape[1:], p.dtype) for p in parts]
    return _Job(list(parts), outs, (len(parts), 3), start, finish)


def _run_job(name, job):
    n_in, n_out = len(job.ins), len(job.outs)

    def body(*refs):
        ins, outs = refs[:n_in], refs[n_in:n_in + n_out]
        send, recv = refs[n_in + n_out:]
        job.start(ins, outs, send, recv)
        job.finish(ins, outs, send, recv)

    return pl.pallas_call(
        body, name=name, out_shape=tuple(job.outs), in_specs=[_ANY] * n_in, out_specs=tuple([_ANY] * n_out),
        scratch_shapes=[pltpu.SemaphoreType.DMA(job.sems), pltpu.SemaphoreType.DMA(job.sems)])(*job.ins)


def _sibling_join(halves):
    nw = len(halves)

    def body(*refs):
        h_refs = refs[:2 * nw]
        f_refs = refs[2 * nw:3 * nw]
        send, recv = refs[3 * nw:]
        x, y, c, _ = _place()
        cps = []
        for w in range(nw):
            for l in range(2):
                src = h_refs[2 * w + l]
                h = src.shape[0]
                dst = f_refs[w].at[l, pl.ds(c * h, h), :]
                cp = pltpu.make_async_remote_copy(src_ref=src, dst_ref=dst, send_sem=send.at[w, l], recv_sem=recv.at[w, l],
                                                  device_id=(x, y, 1 - c), device_id_type=MESH)
                cp.start()
                cps.append(cp)
        for w in range(nw):
            for l in range(2):
                src = h_refs[2 * w + l]
                h = src.shape[0]
                other = f_refs[w].at[l, pl.ds((1 - c) * h, h), :]
                pltpu.make_async_remote_copy(src_ref=src, dst_ref=other, send_sem=send.at[w, l], recv_sem=recv.at[w, l],
                                             device_id=(x, y, c), device_id_type=MESH).wait_recv()
        for cp in cps:
            cp.wait_send()

    flat = [a for pair in halves for a in pair]
    outs = tuple(jax.ShapeDtypeStruct((2, 2 * pair[0].shape[0], pair[0].shape[1]), F32) for pair in halves)
    return pl.pallas_call(
        body, name="grad_sibling_join", out_shape=outs, in_specs=[_ANY] * (2 * nw), out_specs=tuple([_ANY] * nw),
        scratch_shapes=[pltpu.SemaphoreType.DMA((nw, 2)), pltpu.SemaphoreType.DMA((nw, 2))])(*flat)


def _allgather_small(v):
    r = v.shape[0]

    def body(x_ref, out_ref, send_sems, recv_sems, local_sem):
        x, y, c, chips = _place()
        me, sibling = (x, y, c), (x, y, 1 - c)

        def slab(px, py, pc):
            return out_ref.at[4 * px + 2 * py + pc]

        def copy(k, block, to, src=None):
            return pltpu.make_async_remote_copy(src_ref=slab(*block) if src is None else src, dst_ref=slab(*block),
                                                send_sem=send_sems.at[k], recv_sem=recv_sems.at[k],
                                                device_id=to, device_id_type=MESH)

        mine = pltpu.make_async_copy(x_ref, slab(*me), local_sem)
        mine.start()
        first = [copy(0, me, sibling, src=x_ref)]
        first += [copy(1 + j, me, (*chip, c), src=x_ref) for j, chip in enumerate(chips)]
        for cp in first:
            cp.start()
        passed = [copy(4 + j, (*chip, c), sibling) for j, chip in enumerate(chips)]
        for j, chip in enumerate(chips):
            copy(1 + j, (*chip, c), me).wait_recv()
            passed[j].start()
        copy(0, sibling, me).wait_recv()
        for j, chip in enumerate(chips):
            copy(4 + j, (*chip, 1 - c), me).wait_recv()
        for cp in first + passed:
            cp.wait_send()
        mine.wait()

    return pl.pallas_call(
        body, name="allgather_small", out_shape=jax.ShapeDtypeStruct((8, r, 128), F32),
        in_specs=[pl.BlockSpec(memory_space=pltpu.VMEM)], out_specs=pl.BlockSpec(memory_space=pltpu.VMEM),
        scratch_shapes=[pltpu.SemaphoreType.DMA((7,)), pltpu.SemaphoreType.DMA((7,)), pltpu.SemaphoreType.DMA],
    )(v)


def _add_half(name, d, a, c):
    _, h, cols = a.shape

    def body(c_ref, d_ref, a_ref, o_ref):
        o_ref[...] = (d_ref[...] + a_ref[...]).astype(BF16)

    return _call(body, name, jax.ShapeDtypeStruct(a.shape, BF16), (NCHIP,),
                 [pl.BlockSpec((None, h, cols), lambda p, cr: (p, cr[0], 0)), pl.BlockSpec((None, h, cols), lambda p, cr: (p, 0, 0))],
                 pl.BlockSpec((None, h, cols), lambda p, cr: (p, 0, 0)), prefetch=1)(c, d, a)


def _add_chips(name, d, a, b, where):
    _, h, cols = a.shape
    th = h // 4 if (h % 64 == 0) else h
    nt = h // th

    def body(w_ref, d_ref, a_ref, b_ref, o_ref):
        own = d_ref[...] + a_ref[...]
        o_ref[...] = ((own + b_ref[0].astype(F32)) + b_ref[1].astype(F32)) + b_ref[2].astype(F32)

    return _call(body, name, jax.ShapeDtypeStruct((h, cols), F32), (nt,),
                 [pl.BlockSpec((None, th, cols), lambda i, wr: (wr[0], wr[1] * nt + i, 0)),
                  pl.BlockSpec((None, th, cols), lambda i, wr: (wr[0], i, 0)),
                  pl.BlockSpec((3, th, cols), lambda i, wr: (0, i, 0))],
                 pl.BlockSpec((th, cols), lambda i, wr: (i, 0)), prefetch=1)(where, d, a, b)


def _pack(arrs):
    pieces = []
    for a in arrs:
        f = a.reshape(-1)
        n = -(-f.shape[0] // 1024) * 1024
        pieces.append(jnp.pad(f, (0, n - f.shape[0])).reshape(-1, 128))
    return jnp.concatenate(pieces, axis=0)


def _unpack(buf, shapes):
    out, r = [], 0
    for s in shapes:
        n = int(np.prod(s))
        rows = -(-n // 1024) * 8
        out.append(buf[r:r + rows].reshape(-1)[:n].reshape(s))
        r += rows
    return out


def kernel(x, meta_tokens, in_ln_g, in_ln_b, rel_bias, w_in, b_in, attn_sinks, w_attn_proj, conv_dw, conv_dw_b, conv_ln_g, conv_ln_b, w_conv_proj, w_out, ln1_g, ln1_b, ffn_w_up, ffn_dw, ffn_dw_b, ffn_w_down, ln2_g, ln2_b, loss_target, m_meta_tokens, m_in_ln_g, m_in_ln_b, m_rel_bias, m_w_in, m_b_in, m_attn_sinks, m_w_attn_proj, m_conv_dw, m_conv_dw_b, m_conv_ln_g, m_conv_ln_b, m_w_conv_proj, m_w_out, m_ln1_g, m_ln1_b, m_ffn_w_up, m_ffn_dw, m_ffn_dw_b, m_ffn_w_down, m_ln2_g, m_ln2_b, v_meta_tokens, v_in_ln_g, v_in_ln_b, v_rel_bias, v_w_in, v_b_in, v_attn_sinks, v_w_attn_proj, v_conv_dw, v_conv_dw_b, v_conv_ln_g, v_conv_ln_b, v_w_conv_proj, v_w_out, v_ln1_g, v_ln1_b, v_ffn_w_up, v_ffn_dw, v_ffn_dw_b, v_ffn_w_down, v_ln2_g, v_ln2_b):
    nex, seq, _ = x.shape
    nblk = seq // BLK + 1
    tp = nblk * BLK
    m = nex * tp
    tm = _row_tile(m)
    ffs = ffn_w_up.shape[2]
    dff = 2 * ffs
    cx, cy, cc = lax.axis_index("x"), lax.axis_index("y"), lax.axis_index("c")
    chip = (2 * cx + cy).astype(jnp.int32)
    core = cc.astype(jnp.int32)

    names = ("in", "ap", "cp", "out", "up", "down")
    big = dict(zip(names, [w_in, w_attn_proj, w_conv_proj, w_out, ffn_w_up, ffn_w_down]))
    sb = {k: v.astype(BF16) for k, v in big.items()}
    gathered = {}

    def land(items, outs):
        return [lax.dynamic_update_index_in_dim(g, s if layer is None else s[layer], chip, 0)
                for (s, layer), g in zip(items, outs)]

    first_items = [(sb["in"], 0), (meta_tokens.reshape(2, N_META // 2, -1), None), (conv_dw, None), (ffn_dw, None)]
    gathered[("in", 0)], g_meta, g_cdw, g_fdw = land(first_items, _run_job("gather_first", _gather_job(first_items)))
    meta_full = jnp.transpose(g_meta, (1, 2, 0, 3)).reshape(N_META, D)
    bias_tab = _bias_build(rel_bias)

    fwd_plan = {("in_proj", 0): [("ap", 0), ("cp", 0), ("out", 0)], ("attn_fwd", 0): [("up", 0)],
                ("conv_fwd", 0): [("down", 0)], ("mix_fwd", 0): [("in", 1)],
                ("out_proj_ln", 0): [("ap", 1), ("cp", 1), ("out", 1)], ("ffn_up", 0): [("up", 1)],
                ("ffn_act", 0): [("down", 1)]}

    def fwd(tag, l, fn, *args):
        keys = fwd_plan.get((tag, l))
        if not keys:
            return fn(*args)
        items = [(sb[k], kl) for k, kl in keys]
        res = fn(*args, side=_gather_job(items))
        for key, g in zip(keys, land(items, res[-len(keys):])):
            gathered[key] = g
        main = res[:-len(keys)]
        return main[0] if len(main) == 1 else main

    def layer_weights(l):
        win_old = jnp.transpose(gathered[("in", l)], (1, 0, 2)).reshape(D, IN_COLS)
        return dict(
            win=_to_new(win_old), bin=_to_new(b_in[l]).reshape(1, IN_COLS),
            cdw=jnp.transpose(g_cdw[:, l], (1, 0, 2)).reshape(CTAPS, CW),
            fdw=jnp.transpose(g_fdw[:, l], (1, 0, 2)).reshape(FTAPS, 2 * dff),
            fdwb=ffn_dw_b[l].reshape(1, 2 * dff))

    raw, h, hb = _embed_ln(x, meta_full, in_ln_g, in_ln_b, nblk)
    saved, lw = [], []
    for l in range(DEPTH):
        p = layer_weights(l)
        z = fwd("in_proj", l, functools.partial(_mm_bias, "in_proj"), hb, p["win"], p["bin"], IN_COLS // 3, tm)
        a = fwd("attn_fwd", l, _attn_fwd, z, bias_tab, attn_sinks[l], nex, nblk)
        ccv, cs = fwd("conv_fwd", l, _conv_fwd, z, p["cdw"], conv_dw_b[l], conv_ln_g[l], conv_ln_b[l], nex, tp)
        p["wap"], p["wcp"] = gathered[("ap", l)], gathered[("cp", l)]
        ya, yc, mixed = fwd("mix_fwd", l, _mix_fwd, a, cs, p["wap"], p["wcp"], z, tm)
        p["wout"] = gathered[("out", l)].reshape(D, D)
        r1, h1, h1b = fwd("out_proj_ln", l, functools.partial(_mm_res_ln, "out_proj_ln"), mixed, p["wout"], h, ln1_g[l], ln1_b[l], tm)
        p["wup"] = gathered[("up", l)]
        up3 = fwd("ffn_up", l, _ffn_up, h1b, p["wup"], tm, nex, tp)
        act = fwd("ffn_act", l, _ffn_act, up3, p["fdw"], p["fdwb"], tm)
        p["wdown"] = gathered[("down", l)].reshape(dff, D)
        r2, h2, h2b = _mm_res_ln("ffn_down_ln", act, p["wdown"], h1, ln2_g[l], ln2_b[l], tm)
        saved.append(dict(hb=hb, z=z, a=a, cc=ccv, cs=cs, ya=ya, yc=yc, mixed=mixed, r1=r1, h1b=h1b, up3=up3, act=act, r2=r2))
        lw.append(p)
        h, hb = h2, h2b

    dy, sq = _loss_grad(h, loss_target, nblk)
    loss = lax.psum(0.5 / D * jnp.sum(sq), ("x", "y", "c"))

    grads, swapped, pair_sums, reduced = {}, {}, {}, {}
    cvec, where = core.reshape(1), jnp.stack([chip, core])
    last = [(k, DEPTH - 1) for k in names]
    bwd_plan = {("ln2_bwd", 0): ("swap", last),
                ("ffn_act_bwd", 0): ("exch", [("up", 1), ("down", 1)]),
                ("ffn_conv_bwd", 0): ("exch", [("in", 1), ("ap", 1), ("cp", 1), ("out", 1)]),
                ("ln1_bwd", 0): ("swap", [("down", 0), ("up", 0)]),
                ("conv_bwd_b", 0): ("swap", [("out", 0), ("ap", 0), ("cp", 0)]),
                ("attn_bwd", 0): ("exch", [("down", 0), ("up", 0), ("out", 0), ("ap", 0), ("cp", 0)]),
                ("in_ln_bwd", 0): ("swap", [("in", 0)])}

    def after(kind, keys, outs):
        for key, o in zip(keys, outs):
            if kind == "swap":
                swapped[key] = o
                pair_sums[key] = _add_half("grad_add_sibling", grads[key], o, cvec)
            else:
                reduced[key] = _add_chips("grad_add_chips", grads[key], swapped[key], o, where)

    def bwd(tag, l, fn, *args):
        plan = bwd_plan.get((tag, l))
        if plan is None:
            return fn(*args)
        kind, keys = plan
        job = _swap_job([grads[k] for k in keys]) if kind == "swap" else _exchange_job([pair_sums[k] for k in keys])
        res = fn(*args, side=job)
        after(kind, keys, res[-len(keys):])
        main = res[:-len(keys)]
        return main[0] if len(main) == 1 else main

    small = {}
    prev_a, prev_w, prev_cols = (), None, None
    dprev = dy
    for l in reversed(range(DEPTH)):
        p, s = lw[l], saved[l]
        dr2, dr2b, dg2, db2 = bwd("ln2_bwd", l, functools.partial(_ln_bwd_call, "ln2_bwd"), dprev, s["r2"], ln2_g[l], tm,
                                  prev_a, prev_w, prev_cols)
        dup3 = bwd("ffn_act_bwd", l, _ffn_act_bwd, dr2b, p["wdown"], s["up3"], p["fdw"], p["fdwb"], tm)
        dpre3, dfdw, dfdwb = bwd("ffn_conv_bwd", l, _ffn_conv_bwd, dup3, s["up3"], p["fdw"], tm, nex, tp)
        grads[("down", l)] = _mm_tn("dw_down", s["act"], dr2b, ffs, D // 2).reshape(NCHIP, dff // NCHIP, D)
        grads[("up", l)] = _mm_tn("dw_up", s["h1b"], dpre3, D, ffs, b_cols=lambda j: (j // 2, j % 2), chip_out=True)
        dr1, dr1b, dg1, db1 = bwd("ln1_bwd", l, functools.partial(_ln_bwd_call, "ln1_bwd"), dr2, s["r1"], ln1_g[l], tm // 2,
                                  (dpre3,), p["wup"])
        dya, dyc, dzg, csg = _gate_bwd(dr1b, p["wout"], s["ya"], s["yc"], s["z"], tm)
        grads[("out", l)] = _mm_tn("dw_out", s["mixed"], dr1b, D, D // 2).reshape(NCHIP, D // NCHIP, D)
        grads[("ap", l)] = _mm_tn("dw_attn_proj", s["a"], dya, AW, D // NCHIP, chip_out=True)
        grads[("cp", l)] = _mm_tn("dw_conv_proj", s["cs"], dyc, CW, D // NCHIP, chip_out=True)
        dcc, dclg, dclb, dcwb = _conv_bwd_a(dyc, p["wcp"], s["cc"], conv_ln_g[l], conv_ln_b[l], tm)
        dzc, dcdw, csc = bwd("conv_bwd_b", l, _conv_bwd_b, dcc, s["z"], p["cdw"], nex, tp)
        dzq, csq, dsk, dbias = bwd("attn_bwd", l, _attn_bwd, s["z"], bias_tab, attn_sinks[l], dya, p["wap"], nex, nblk)
        gin = [_mm_tn("dw_in_gates", s["hb"], dzg, D, D // 2), _mm_tn("dw_in_conv", s["hb"], dzc, D, CW),
               _mm_tn("dw_in_qkv", s["hb"], dzq, D, 2 * KVW)]
        gin_old = _to_old(jnp.concatenate(gin, axis=1))
        grads[("in", l)] = jnp.transpose(gin_old.reshape(D, NCHIP, IN_COLS // NCHIP), (1, 0, 2))
        small[l] = dict(
            b_in=_to_old(jnp.concatenate([csg, csc, csq], axis=1)).reshape(IN_COLS), attn_sinks=dsk[0, :NQ],
            conv_dw=dcdw[:CTAPS], conv_dw_b=dcwb.reshape(CW), conv_ln_g=dclg.reshape(CW), conv_ln_b=dclb.reshape(CW),
            ln1_g=dg1.reshape(D), ln1_b=db1.reshape(D),
            ffn_dw=jnp.transpose(dfdw, (1, 0, 2)).reshape(FTAPS, 2 * dff), ffn_dw_b=jnp.transpose(dfdwb, (1, 0, 2)).reshape(2 * dff),
            ln2_g=dg2.reshape(D), ln2_b=db2.reshape(D), bias=_bias_grad(dbias))
        dprev = dr1
        prev_a, prev_w, prev_cols = (dzg, dzc, dzq), p["win"], [(C_GATES, C_CONV), (C_CONV, C_QKV), (C_QKV, IN_COLS)]
    draw, _, dg0, db0 = bwd("in_ln_bwd", 0, functools.partial(_ln_bwd_call, "in_ln_bwd"), dprev, raw, in_ln_g, tm,
                            prev_a, prev_w, prev_cols)
    draw3 = draw.reshape(nex, tp, D)
    grad_x = draw3[:, BLK:, :]
    dmeta = _sum0("meta_grad_sum", draw3[:, PAD:BLK, :])

    names_l = ["b_in", "attn_sinks", "conv_dw", "conv_dw_b", "conv_ln_g", "conv_ln_b", "ln1_g", "ln1_b", "ffn_dw", "ffn_dw_b", "ln2_g", "ln2_b"]
    part_list = [dmeta, dg0.reshape(D), db0.reshape(D), small[0]["bias"] + small[1]["bias"]]
    part_list += [jnp.stack([small[0][n], small[1][n]]) for n in names_l]
    shapes_small = [tuple(a.shape) for a in part_list]
    tot = _sum0("small_grad_sum", _allgather_small(_pack(part_list)))
    (g_meta_f, g_inlg, g_inlb, g_biasp, g_bin, g_sinks, g_cdw_f, g_cdwb, g_clg, g_clb, g_l1g, g_l1b, g_fdw_f, g_fdwb,
     g_l2g, g_l2b) = _unpack(tot, shapes_small)
    g_relb = g_biasp
    csh = D // NCHIP
    g_meta_s = lax.dynamic_slice_in_dim(g_meta_f, chip * csh, csh, axis=1)
    g_cdw_s = lax.dynamic_slice_in_dim(g_cdw_f, chip * (CW // NCHIP), CW // NCHIP, axis=2)
    g_fdw_s = lax.dynamic_slice_in_dim(g_fdw_f, chip * ffs, ffs, axis=2)

    tail = [("in", 0)]
    after("exch", tail, _run_job("grad_chip_exchange", _exchange_job([pair_sums[k] for k in tail])))
    joined = _sibling_join([[reduced[(k, l)] for l in range(DEPTH)] for k in names])
    full = []
    for k, f in zip(names, joined):
        hh = reduced[(k, 0)].shape[0]
        for l in range(DEPTH):
            f = lax.dynamic_update_slice(f, reduced[(k, l)][None], (l, core * hh, 0))
        full.append(f)

    moms = [m_w_in, m_w_attn_proj, m_w_conv_proj, m_w_out, m_ffn_w_up, m_ffn_w_down]
    vels = [v_w_in, v_w_attn_proj, v_w_conv_proj, v_w_out, v_ffn_w_up, v_ffn_w_down]
    big_out = []
    for w, g, mo, ve in zip(big.values(), full, moms, vels):
        sh = w.shape
        two = lambda t: t.reshape(sh[0] * sh[1], sh[2])
        d_, m_, v_ = _adamw("adamw_matrix", two(w), two(g), two(mo), two(ve))
        big_out.append((g.reshape(sh), d_.reshape(sh), m_.reshape(sh), v_.reshape(sh)))

    sm_w = [meta_tokens, in_ln_g, in_ln_b, rel_bias, b_in, attn_sinks, conv_dw, conv_dw_b, conv_ln_g, conv_ln_b, ln1_g, ln1_b,
            ffn_dw, ffn_dw_b, ln2_g, ln2_b]
    sm_m = [m_meta_tokens, m_in_ln_g, m_in_ln_b, m_rel_bias, m_b_in, m_attn_sinks, m_conv_dw, m_conv_dw_b, m_conv_ln_g, m_conv_ln_b,
            m_ln1_g, m_ln1_b, m_ffn_dw, m_ffn_dw_b, m_ln2_g, m_ln2_b]
    sm_v = [v_meta_tokens, v_in_ln_g, v_in_ln_b, v_rel_bias, v_b_in, v_attn_sinks, v_conv_dw, v_conv_dw_b, v_conv_ln_g, v_conv_ln_b,
            v_ln1_g, v_ln1_b, v_ffn_dw, v_ffn_dw_b, v_ln2_g, v_ln2_b]
    sm_g = [g_meta_s, g_inlg, g_inlb, g_relb, g_bin, g_sinks, g_cdw_s, g_cdwb, g_clg, g_clb, g_l1g, g_l1b, g_fdw_s, g_fdwb, g_l2g, g_l2b]
    sm_shapes = [tuple(a.shape) for a in sm_w]
    sd, smn, svn = _adamw("adamw_small", _pack(sm_w), _pack(sm_g), _pack(sm_m), _pack(sm_v))
    sd, smn, svn = _unpack(sd, sm_shapes), _unpack(smn, sm_shapes), _unpack(svn, sm_shapes)

    order = ["meta_tokens", "in_ln_g", "in_ln_b", "rel_bias", "w_in", "b_in", "attn_sinks", "w_attn_proj", "conv_dw", "conv_dw_b",
             "conv_ln_g", "conv_ln_b", "w_conv_proj", "w_out", "ln1_g", "ln1_b", "ffn_w_up", "ffn_dw", "ffn_dw_b", "ffn_w_down",
             "ln2_g", "ln2_b"]
    small_names = ["meta_tokens", "in_ln_g", "in_ln_b", "rel_bias", "b_in", "attn_sinks", "conv_dw", "conv_dw_b", "conv_ln_g",
                   "conv_ln_b", "ln1_g", "ln1_b", "ffn_dw", "ffn_dw_b", "ln2_g", "ln2_b"]
    big_names = ["w_in", "w_attn_proj", "w_conv_proj", "w_out", "ffn_w_up", "ffn_w_down"]
    res = {}
    for i, n in enumerate(small_names):
        res[n] = (sm_g[i], sd[i], smn[i], svn[i])
    for i, n in enumerate(big_names):
        res[n] = big_out[i]
    outs = [loss, grad_x]
    for k in range(4):
        outs += [res[n][k] for n in order]
    return tuple(outs)
```

```python
import functools
import math
from typing import Any, Callable, NamedTuple, Sequence

import numpy as np
import jax
import jax.numpy as jnp
from jax import lax
from jax.experimental import pallas as pl
from jax.experimental.pallas import tpu as pltpu

F32 = jnp.float32
BF16 = jnp.bfloat16
MESH = pl.DeviceIdType.MESH

D = 1024
N_META = 16
BLK = 128
PAD = BLK - N_META
HD = 64
NQ = 8
NKV = 2
GRP = NQ // NKV
AW = NQ * HD
KVW = NKV * HD
CW = D // 2
CTAPS = 31
FTAPS = 3
NBUCKET = 32
MAXDIST = 128
EPS = 1e-5
DEPTH = 2
ALPHA = (2.0 * DEPTH) ** 0.25
NCHIP = 4
NKEY = 3 * BLK
NEG = -1e30
CHALO = 32
FHALO = 8
IN_COLS = AW + 2 * KVW + 2 * CW + 2 * D
_OLD = dict(q=(0, AW), k=(AW, AW + KVW), v=(AW + KVW, AW + 2 * KVW), cv=(AW + 2 * KVW, AW + 2 * KVW + CW),
            cg=(AW + 2 * KVW + CW, AW + 2 * KVW + 2 * CW), ga=(AW + 2 * KVW + 2 * CW, AW + 2 * KVW + 2 * CW + D),
            gc=(AW + 2 * KVW + 2 * CW + D, IN_COLS))
_NEW_ORDER = ("ga", "gc", "cv", "cg", "q", "k", "v")
C_GATES, C_CONV, C_QKV = 0, 2 * D, 2 * D + 2 * CW

ADAM_LR, ADAM_B1, ADAM_B2, ADAM_EPS, ADAM_WD, ADAM_STEP = 0.001, 0.9, 0.999, 1e-08, 0.01, 10


def _to_new(a):
    return jnp.concatenate([a[..., _OLD[n][0]:_OLD[n][1]] for n in _NEW_ORDER], axis=-1)


def _to_old(a):
    offs, o = {}, 0
    for n in _NEW_ORDER:
        w = _OLD[n][1] - _OLD[n][0]
        offs[n] = (o, o + w)
        o += w
    return jnp.concatenate([a[..., offs[n][0]:offs[n][1]] for n in ("q", "k", "v", "cv", "cg", "ga", "gc")], axis=-1)


class _Job(NamedTuple):
    ins: Sequence[Any]
    outs: Sequence[Any]
    sems: tuple
    start: Callable
    finish: Callable


_ANY = pl.BlockSpec(memory_space=pl.ANY)


def _call(body, name, out_shape, grid, in_specs, out_specs, scratch=(), prefetch=0, side=None):
    params = pltpu.CompilerParams(dimension_semantics=("arbitrary",) * len(grid))
    if side is None:
        if prefetch:
            gs = pltpu.PrefetchScalarGridSpec(num_scalar_prefetch=prefetch, grid=grid, in_specs=in_specs,
                                              out_specs=out_specs, scratch_shapes=list(scratch))
            return pl.pallas_call(body, name=name, out_shape=out_shape, grid_spec=gs, compiler_params=params)
        return pl.pallas_call(body, name=name, out_shape=out_shape, grid=grid, in_specs=in_specs, out_specs=out_specs,
                              scratch_shapes=list(scratch), compiler_params=params)
    assert not prefetch
    single = not isinstance(out_shape, (tuple, list))
    main_shapes = (out_shape,) if single else tuple(out_shape)
    main_specs = (out_specs,) if single else tuple(out_specs)
    n_in, n_sin, n_out, n_sout, n_scr = len(in_specs), len(side.ins), len(main_shapes), len(side.outs), len(scratch)

    def wrapped(*refs):
        main_in, sin = refs[:n_in], refs[n_in:n_in + n_sin]
        o0 = n_in + n_sin
        main_out, sout = refs[o0:o0 + n_out], refs[o0 + n_out:o0 + n_out + n_sout]
        s0 = o0 + n_out + n_sout
        main_scr, (send, recv) = refs[s0:s0 + n_scr], refs[s0 + n_scr:]
        first = functools.reduce(lambda a, b: a & b, [pl.program_id(k) == 0 for k in range(len(grid))])
        last = functools.reduce(lambda a, b: a & b, [pl.program_id(k) == grid[k] - 1 for k in range(len(grid))])

        @pl.when(first)
        def _():
            side.start(sin, sout, send, recv)

        body(*main_in, *main_out, *main_scr)

        @pl.when(last)
        def _():
            side.finish(sin, sout, send, recv)

    call = pl.pallas_call(
        wrapped, name=name, out_shape=main_shapes + tuple(side.outs), grid=grid,
        in_specs=list(in_specs) + [_ANY] * n_sin, out_specs=main_specs + tuple([_ANY] * n_sout),
        scratch_shapes=list(scratch) + [pltpu.SemaphoreType.DMA(side.sems), pltpu.SemaphoreType.DMA(side.sems)],
        compiler_params=params)
    return lambda *args: call(*args, *side.ins)


def _row_tile(m):
    best = 32
    for t in range(32, 641, 32):
        if m % t == 0:
            best = t
    return best


def _pad_rows(rows, nex, tp):
    m = rows < PAD
    for b in range(1, nex):
        m = m | ((rows >= b * tp) & (rows < b * tp + PAD))
    return m


def _ln_stats(x):
    mu = jnp.mean(x, axis=-1, keepdims=True)
    xc = x - mu
    var = jnp.mean(xc * xc, axis=-1, keepdims=True)
    rstd = lax.rsqrt(var + EPS)
    return xc * rstd, rstd


def _ln_bwd(dy, xhat, rstd, g):
    dxh = dy * g
    m1 = jnp.mean(dxh, axis=-1, keepdims=True)
    m2 = jnp.mean(dxh * xhat, axis=-1, keepdims=True)
    return rstd * (dxh - m1 - xhat * m2)


def _dot(a, b):
    return jnp.dot(a, b, preferred_element_type=F32)


def _dot_nt(a, b):
    return lax.dot_general(a, b, (((1,), (1,)), ((), ())), preferred_element_type=F32)


def _dot_tn(a, b):
    return lax.dot_general(a, b, (((0,), (0,)), ((), ())), preferred_element_type=F32)


def _sigmoid(x):
    return 1.0 / (1.0 + jnp.exp(-x))


def _gelu_parts(g):
    cdf = 0.5 * (1.0 + lax.erf(g * (1.0 / math.sqrt(2.0))))
    pdf = jnp.exp(-0.5 * g * g) * (1.0 / math.sqrt(2.0 * math.pi))
    return g * cdf, cdf + g * pdf


def _bucket_np(d):
    n = np.maximum(d, 0)
    max_exact = NBUCKET // 2
    nf = np.maximum(n, 1).astype(np.float32)
    large = max_exact + (np.log(nf / np.float32(max_exact)) / np.float32(math.log(MAXDIST / max_exact))
                         * np.float32(NBUCKET - max_exact)).astype(np.int32)
    large = np.minimum(large, NBUCKET - 1)
    return np.where(n < max_exact, n, large).astype(np.int32)


def _bias_index():
    i = np.arange(BLK)[:, None]
    j = np.arange(2 * BLK)[None, :]
    d = BLK + i - j
    band_ok = (d >= 0) & (d < BLK)
    band = _bucket_np(d)
    idx = np.full((3, BLK, NKEY), -1, np.int32)
    m = np.arange(N_META)[None, :]
    d0 = (i - PAD) - m
    idx[0, :, 2 * BLK:2 * BLK + N_META] = np.where(d0 >= 0, _bucket_np(d0), -1)
    ok1 = band_ok & (j >= BLK)
    idx[1, :, :2 * BLK] = np.where(ok1, band, -1)
    idx[1, :, 2 * BLK:2 * BLK + N_META] = _bucket_np((N_META + i) - m)
    idx[2, :, :2 * BLK] = np.where(band_ok, band, -1)
    idx[2, :, 2 * BLK:2 * BLK + N_META] = NBUCKET - 1
    return idx


def _bias_build(rel_bias):
    idx = jnp.asarray(_bias_index())

    def body(idx_ref, rb_ref, o_ref):
        ix = idx_ref[...]
        for h in range(NQ):
            acc = jnp.full(ix.shape, NEG, F32)
            for b in range(NBUCKET):
                acc = jnp.where(ix == b, rb_ref[b, h], acc)
            o_ref[:, h, :, :] = acc

    return pl.pallas_call(
        body, name="bias_build", out_shape=jax.ShapeDtypeStruct((3, NQ, BLK, NKEY), F32),
        in_specs=[pl.BlockSpec(memory_space=pltpu.VMEM), pl.BlockSpec(memory_space=pltpu.SMEM)],
        out_specs=pl.BlockSpec(memory_space=pltpu.VMEM))(idx, rel_bias)


def _bias_grad(dbias):
    idx = jnp.asarray(_bias_index())

    def body(idx_ref, d_ref, o_ref):
        d = jnp.sum(d_ref[...], axis=0)
        for b in range(NBUCKET):
            acc = jnp.zeros((NQ, NKEY), F32)
            for case in range(3):
                hit = (idx_ref[case] == b)[None, :, :]
                acc = acc + jnp.sum(jnp.where(hit, d[case], 0.0), axis=1)
            o_ref[b] = jnp.sum(acc, axis=-1, keepdims=True)

    out = pl.pallas_call(
        body, name="bias_grad", out_shape=jax.ShapeDtypeStruct((NBUCKET, NQ, 1), F32),
        in_specs=[pl.BlockSpec(memory_space=pltpu.VMEM), pl.BlockSpec(memory_space=pltpu.VMEM)],
        out_specs=pl.BlockSpec(memory_space=pltpu.VMEM))(idx, dbias)
    return out.reshape(NBUCKET, NQ)


def _embed_ln(x, meta, g, b, nblk, side=None):
    nex, seq, _ = x.shape
    m = nex * nblk * BLK

    def body(x_ref, meta_ref, g_ref, b_ref, raw_ref, h_ref, hb_ref):
        j = pl.program_id(1)

        @pl.when(j == 0)
        def _():
            raw_ref[0:PAD, :] = jnp.zeros((PAD, D), F32)
            raw_ref[PAD:BLK, :] = meta_ref[...]

        @pl.when(j > 0)
        def _():
            raw_ref[...] = x_ref[...]

        xhat, _ = _ln_stats(raw_ref[...])
        y = xhat * g_ref[...] + b_ref[...]
        h_ref[...] = y
        hb_ref[...] = y.astype(BF16)

    row = lambda bb, j: (bb * nblk + j, 0)
    return _call(
        body, "embed_ln",
        (jax.ShapeDtypeStruct((m, D), F32), jax.ShapeDtypeStruct((m, D), F32), jax.ShapeDtypeStruct((m, D), BF16)),
        (nex, nblk),
        [pl.BlockSpec((None, BLK, D), lambda bb, j: (bb, jnp.maximum(j - 1, 0), 0)),
         pl.BlockSpec((N_META, D), lambda bb, j: (0, 0)),
         pl.BlockSpec((1, D), lambda bb, j: (0, 0)), pl.BlockSpec((1, D), lambda bb, j: (0, 0))],
        (pl.BlockSpec((BLK, D), row), pl.BlockSpec((BLK, D), row), pl.BlockSpec((BLK, D), row)), side=side,
    )(x, meta, g.reshape(1, D), b.reshape(1, D))


def _mm_bias(name, a, w, bias, tn, tm, side=None):
    m, k = a.shape
    n = w.shape[1]

    def body(a_ref, w_ref, b_ref, o_ref):
        o_ref[...] = _dot(a_ref[...], w_ref[...]) + b_ref[...]

    return _call(body, name, jax.ShapeDtypeStruct((m, n), F32), (n // tn, m // tm),
                 [pl.BlockSpec((tm, k), lambda j, i: (i, 0)), pl.BlockSpec((k, tn), lambda j, i: (0, j)),
                  pl.BlockSpec((1, tn), lambda j, i: (0, j))],
                 pl.BlockSpec((tm, tn), lambda j, i: (i, j)), side=side)(a, w, bias)


def _ffn_up_act(a, w4, cw, cb, tm, nex, tp, side=None):
    m, k = a.shape
    ffs = w4.shape[2]

    def body(a_ref, wu_ref, wg_ref, cu_ref, cg_ref, bu_ref, bg_ref, up_ref, act_ref, win):
        i = pl.program_id(1)

        @pl.when(i == 0)
        def _():
            win[:, 0:FHALO, :] = jnp.zeros((2, FHALO, ffs), F32)

        rows = i * tm + lax.broadcasted_iota(jnp.int32, (tm, 1), 0)
        pad = _pad_rows(rows, nex, tp)
        av = a_ref[...]
        for p, w_ref in ((0, wu_ref), (1, wg_ref)):
            x = jnp.where(pad, 0.0, _dot(av, w_ref[...]))
            win[p, FHALO:FHALO + tm, :] = x
            up_ref[p] = x
        for r0 in range(0, tm, RCH):
            for c0, c1 in _lane_groups(ffs):
                u = _conv3(win, 0, r0, c0, c1, cu_ref, bu_ref)
                g = _conv3(win, 1, r0, c0, c1, cg_ref, bg_ref)
                act_ref[r0:r0 + RCH, c0:c1] = (g * (0.5 * (1.0 + lax.erf(g * (1.0 / math.sqrt(2.0))))) * u).astype(BF16)
        win[:, 0:FHALO, :] = win[:, tm:tm + FHALO, :]

    return _call(body, "ffn_up_act", (jax.ShapeDtypeStruct((2, m, 2 * ffs), F32), jax.ShapeDtypeStruct((m, 2 * ffs), BF16)),
                 (2, m // tm),
                 [pl.BlockSpec((tm, k), lambda c, i: (i, 0)),
                  pl.BlockSpec((None, k, ffs), lambda c, i: (c, 0, 0)), pl.BlockSpec((None, k, ffs), lambda c, i: (c + 2, 0, 0)),
                  pl.BlockSpec((FTAPS, ffs), lambda c, i: (0, c)), pl.BlockSpec((FTAPS, ffs), lambda c, i: (0, c + 2)),
                  pl.BlockSpec((1, ffs), lambda c, i: (0, c)), pl.BlockSpec((1, ffs), lambda c, i: (0, c + 2))],
                 (pl.BlockSpec((2, tm, ffs), lambda c, i: (0, i, c)), pl.BlockSpec((tm, ffs), lambda c, i: (i, c))),
                 scratch=[pltpu.VMEM((2, FHALO + tm, ffs), F32)], side=side)(a, w4, w4, cw, cw, cb, cb)


def _fill_kv(ks, vs, e, prev_ref, cur_ref, meta_ref):
    for piece, lo, n in ((prev_ref, 0, BLK), (cur_ref, BLK, BLK), (meta_ref, 2 * BLK, N_META)):
        val = piece[e]
        for hk in range(NKV):
            ks[e, hk, lo:lo + n, :] = val[:, hk * HD:(hk + 1) * HD].astype(BF16)
            vs[e, hk, lo:lo + n, :] = val[:, KVW + hk * HD:KVW + (hk + 1) * HD].astype(BF16)
    for hk in range(NKV):
        ks[e, hk, 2 * BLK + N_META:NKEY, :] = jnp.zeros((BLK - N_META, HD), BF16)
        vs[e, hk, 2 * BLK + N_META:NKEY, :] = jnp.zeros((BLK - N_META, HD), BF16)


def _softmax_group(q, ks_hk, bias_ref, sink_ref, hk):
    qg = jnp.concatenate([q[:, (hk * GRP + g) * HD:(hk * GRP + g + 1) * HD] for g in range(GRP)], axis=0).astype(BF16)
    s = _dot_nt(qg, ks_hk) * (HD ** -0.5) + bias_ref[hk * GRP:(hk + 1) * GRP].reshape(GRP * BLK, NKEY)
    sink = jnp.concatenate([jnp.full((BLK, 1), sink_ref[0, hk * GRP + g], F32) for g in range(GRP)], axis=0)
    mx = jnp.maximum(jnp.max(s, axis=-1, keepdims=True), sink)
    p = jnp.exp(s - mx)
    es = jnp.exp(sink - mx)
    inv = 1.0 / (jnp.sum(p, axis=-1, keepdims=True) + es)
    return qg, p * inv, es * inv


def _attn_specs(nex, blk_of):
    qcol, kvcol = (C_QKV) // AW, (C_QKV + AW) // (2 * KVW)
    return [
        pl.BlockSpec((nex, BLK, AW), lambda j: (0, blk_of(j), qcol)),
        pl.BlockSpec((nex, BLK, 2 * KVW), lambda j: (0, blk_of(j), kvcol)),
        pl.BlockSpec((nex, BLK, 2 * KVW), lambda j: (0, jnp.maximum(blk_of(j) - 1, 0), kvcol)),
        pl.BlockSpec((nex, N_META, 2 * KVW), lambda j: (0, PAD // N_META, kvcol)),
        pl.BlockSpec((None, NQ, BLK, NKEY), lambda j: (jnp.minimum(blk_of(j), 2), 0, 0, 0)),
        pl.BlockSpec(memory_space=pltpu.SMEM),
    ]


def _attn_fwd(z, bias, sinks, nex, nblk, side=None):
    m = z.shape[0]
    z3 = z.reshape(nex, nblk * BLK, z.shape[1])

    def body(q_ref, cur_ref, prev_ref, meta_ref, bias_ref, sink_ref, o_ref, ks, vs, oacc):
        for e in range(nex):
            _fill_kv(ks, vs, e, prev_ref, cur_ref, meta_ref)
            q = q_ref[e]
            for hk in range(NKV):
                _, pn, _ = _softmax_group(q, ks[e, hk], bias_ref, sink_ref, hk)
                o = _dot(pn.astype(BF16), vs[e, hk])
                for g in range(GRP):
                    h = hk * GRP + g
                    oacc[e, :, h * HD:(h + 1) * HD] = o[g * BLK:(g + 1) * BLK, :]
            o_ref[e] = oacc[e].astype(BF16)

    res = _call(body, "attn_fwd", jax.ShapeDtypeStruct((nex, nblk * BLK, AW), BF16), (nblk,),
                _attn_specs(nex, lambda j: j),
                pl.BlockSpec((nex, BLK, AW), lambda j: (0, j, 0)),
                scratch=[pltpu.VMEM((nex, NKV, NKEY, HD), BF16), pltpu.VMEM((nex, NKV, NKEY, HD), BF16),
                         pltpu.VMEM((nex, BLK, AW), F32)], side=side)(z3, z3, z3, z3, bias, sinks.reshape(1, NQ))
    if side is None:
        return res.reshape(m, AW)
    return (res[0].reshape(m, AW),) + tuple(res[1:])


def _cgate(cv, cg, rows, nex, tp):
    return jnp.where(_pad_rows(rows, nex, tp), 0.0, cv * _sigmoid(cg))


CLANES = 256


def _rolled_up(blk, b):
    return blk if b == 0 else pltpu.roll(blk, blk.shape[0] - b, axis=0)


def _conv_fwd(z, w, wb, g, b, nex, tp, side=None):
    m = z.shape[0]
    tm = BLK
    sub = tm // CHALO
    cvc, cgc = C_CONV // CW, C_CONV // CW + 1

    def body(cv_ref, cg_ref, cvh_ref, cgh_ref, w_ref, wb_ref, g_ref, b_ref, cc_ref, cs_ref, win):
        i = pl.program_id(0)
        rows = i * tm + lax.broadcasted_iota(jnp.int32, (tm, 1), 0)
        hrows = i * tm - CHALO + lax.broadcasted_iota(jnp.int32, (CHALO, 1), 0)
        win[0:CHALO, :] = _cgate(cvh_ref[...], cgh_ref[...], hrows, nex, tp)
        win[CHALO:CHALO + tm, :] = _cgate(cv_ref[...], cg_ref[...], rows, nex, tp)
        for sb in range(sub):
            lo = sb * CHALO
            for c0 in range(0, CW, CLANES):
                blk = win[lo:lo + 2 * CHALO, c0:c0 + CLANES]
                acc = jnp.zeros((CHALO, CLANES), F32) + wb_ref[:, c0:c0 + CLANES]
                for b in range(8):
                    rb = _rolled_up(blk, b)
                    for a in range(5):
                        s = 8 * a + b
                        if 2 <= s <= CTAPS + 1:
                            acc = acc + w_ref[s - 2:s - 1, c0:c0 + CLANES] * rb[8 * a:8 * a + CHALO]
                cc_ref[lo:lo + CHALO, c0:c0 + CLANES] = acc
        xhat, _ = _ln_stats(cc_ref[...])
        cl = xhat * g_ref[...] + b_ref[...]
        cs_ref[...] = (cl * _sigmoid(cl)).astype(BF16)

    halo = lambda i: jnp.maximum(i * sub - 1, 0)
    vec = pl.BlockSpec((1, CW), lambda i: (0, 0))
    return _call(body, "conv_fwd", (jax.ShapeDtypeStruct((m, CW), F32), jax.ShapeDtypeStruct((m, CW), BF16)),
                 (m // tm,),
                 [pl.BlockSpec((tm, CW), lambda i: (i, cvc)), pl.BlockSpec((tm, CW), lambda i: (i, cgc)),
                  pl.BlockSpec((CHALO, CW), lambda i: (halo(i), cvc)), pl.BlockSpec((CHALO, CW), lambda i: (halo(i), cgc)),
                  pl.BlockSpec((CTAPS, CW), lambda i: (0, 0)), vec, vec, vec],
                 (pl.BlockSpec((tm, CW), lambda i: (i, 0)), pl.BlockSpec((tm, CW), lambda i: (i, 0))),
                 scratch=[pltpu.VMEM((CHALO + tm, CW), F32)], side=side)(z, z, z, z, w, wb.reshape(1, CW), g.reshape(1, CW), b.reshape(1, CW))


def _mix_fwd(a, cs, wap4, wcp4, z, tm, side=None):
    m = a.shape[0]
    ns = wap4.shape[2]

    def body(a_ref, cs_ref, wa_ref, wc_ref, ga_ref, gc_ref, ya_ref, yc_ref, mix_ref):
        av, cv = a_ref[...], cs_ref[...]
        for j in range(NCHIP):
            ya_ref[:, j * ns:(j + 1) * ns] = _dot(av, wa_ref[j])
            yc_ref[:, j * ns:(j + 1) * ns] = _dot(cv, wc_ref[j])
        mix_ref[...] = (_sigmoid(ga_ref[...]) * ya_ref[...] + _sigmoid(gc_ref[...]) * yc_ref[...]).astype(BF16)

    wspec = pl.BlockSpec((NCHIP, AW, ns), lambda i: (0, 0, 0))
    row = lambda i: (i, 0)
    return _call(body, "mix_fwd",
                 (jax.ShapeDtypeStruct((m, D), F32), jax.ShapeDtypeStruct((m, D), F32), jax.ShapeDtypeStruct((m, D), BF16)),
                 (m // tm,),
                 [pl.BlockSpec((tm, AW), row), pl.BlockSpec((tm, CW), row), wspec, wspec,
                  pl.BlockSpec((tm, D), lambda i: (i, 0)), pl.BlockSpec((tm, D), lambda i: (i, 1))],
                 (pl.BlockSpec((tm, D), row), pl.BlockSpec((tm, D), row), pl.BlockSpec((tm, D), row)), side=side)(a, cs, wap4, wcp4, z, z)


def _mm_res_ln(name, a, w, res, g, b, tm, side=None):
    m, k = a.shape

    def body(a_ref, w_ref, res_ref, g_ref, b_ref, r_ref, h_ref, hb_ref):
        r = ALPHA * res_ref[...] + _dot(a_ref[...], w_ref[...])
        r_ref[...] = r
        xhat, _ = _ln_stats(r)
        y = xhat * g_ref[...] + b_ref[...]
        h_ref[...] = y
        hb_ref[...] = y.astype(BF16)

    row = lambda i: (i, 0)
    vec = pl.BlockSpec((1, D), lambda i: (0, 0))
    return _call(body, name,
                 (jax.ShapeDtypeStruct((m, D), F32), jax.ShapeDtypeStruct((m, D), F32), jax.ShapeDtypeStruct((m, D), BF16)),
                 (m // tm,),
                 [pl.BlockSpec((tm, k), row), pl.BlockSpec((k, D), lambda i: (0, 0)), pl.BlockSpec((tm, D), row), vec, vec],
                 (pl.BlockSpec((tm, D), row), pl.BlockSpec((tm, D), row), pl.BlockSpec((tm, D), row)), side=side)(a, w, res, g.reshape(1, D), b.reshape(1, D))


RCH = 16


def _lane_groups(width, most=768):
    n = -(-width // most)
    step = -(-width // (128 * n)) * 128
    return [(c, min(c + step, width)) for c in range(0, width, step)]


def _conv3(win, p, r0, c0, c1, w_ref, b_ref):
    blk = win[p, r0:r0 + FHALO + RCH, c0:c1]
    x0, x1, x2 = blk[FHALO:], pltpu.roll(blk, 1, axis=0)[FHALO:], pltpu.roll(blk, 2, axis=0)[FHALO:]
    return b_ref[:, c0:c1] + w_ref[0:1, c0:c1] * x2 + w_ref[1:2, c0:c1] * x1 + w_ref[2:3, c0:c1] * x0


def _ffn_specs(tm, ffs, m):
    sub = tm // FHALO
    return [
        pl.BlockSpec((2, tm, ffs), lambda c, i: (0, i, c)),
        pl.BlockSpec((2, FHALO, ffs), lambda c, i: (0, jnp.maximum(i * sub - 1, 0), c)),
        pl.BlockSpec((FTAPS, ffs), lambda c, i: (0, c)), pl.BlockSpec((FTAPS, ffs), lambda c, i: (0, c + 2)),
        pl.BlockSpec((1, ffs), lambda c, i: (0, c)), pl.BlockSpec((1, ffs), lambda c, i: (0, c + 2)),
    ]


def _loss_grad(y, target, nblk):
    nex = target.shape[0]
    m = y.shape[0]

    def body(y_ref, t_ref, dy_ref, acc_ref):
        bb, j = pl.program_id(0), pl.program_id(1)

        @pl.when((bb == 0) & (j == 0))
        def _():
            acc_ref[...] = jnp.zeros_like(acc_ref)

        @pl.when(j == 0)
        def _():
            dy_ref[...] = jnp.zeros_like(dy_ref)

        @pl.when(j > 0)
        def _():
            e = y_ref[...] - t_ref[...]
            dy_ref[...] = e * (1.0 / D)
            acc_ref[...] += jnp.sum((e * e).reshape(BLK // 8, 8, D), axis=0)

    return _call(body, "loss_grad", (jax.ShapeDtypeStruct((m, D), F32), jax.ShapeDtypeStruct((8, D), F32)), (nex, nblk),
                 [pl.BlockSpec((BLK, D), lambda bb, j: (bb * nblk + j, 0)),
                  pl.BlockSpec((None, BLK, D), lambda bb, j: (bb, jnp.maximum(j - 1, 0), 0))],
                 (pl.BlockSpec((BLK, D), lambda bb, j: (bb * nblk + j, 0)), pl.BlockSpec((8, D), lambda bb, j: (0, 0))))(y, target)


def _ln_bwd_call(name, dy, r, g, tm, a_list=(), w=None, cols=None, side=None):
    m = dy.shape[0]
    na = len(a_list)

    def body(*refs):
        dy_ref, r_ref, g_ref = refs[0:3]
        a_refs = refs[3:3 + na]
        w_ref = refs[3 + na] if na else None
        dr_ref, drb_ref, dg_ref, db_ref = refs[-4:]
        i = pl.program_id(0)
        dh = dy_ref[...]
        if na:
            dh = ALPHA * dh
            if cols is None:
                ns = w.shape[2]
                for j in range(NCHIP):
                    dh = dh + _dot_nt(a_refs[0][j // 2, :, (j % 2) * ns:(j % 2 + 1) * ns], w_ref[j])
            else:
                for a_ref, (c0, c1) in zip(a_refs, cols):
                    dh = dh + _dot_nt(a_ref[...], w_ref[:, c0:c1])
        xhat, rstd = _ln_stats(r_ref[...])
        dr = _ln_bwd(dh, xhat, rstd, g_ref[...])
        dr_ref[...] = dr
        drb_ref[...] = dr.astype(BF16)

        @pl.when(i == 0)
        def _():
            dg_ref[...] = jnp.zeros_like(dg_ref)
            db_ref[...] = jnp.zeros_like(db_ref)

        dg_ref[...] += jnp.sum(dh * xhat, axis=0, keepdims=True)
        db_ref[...] += jnp.sum(dh, axis=0, keepdims=True)

    row = lambda i: (i, 0)
    vec = pl.BlockSpec((1, D), lambda i: (0, 0))
    in_specs = [pl.BlockSpec((tm, D), row), pl.BlockSpec((tm, D), row), vec]
    for a in a_list:
        in_specs.append(pl.BlockSpec((2, tm, a.shape[2]), lambda i: (0, i, 0)) if a.ndim == 3 else pl.BlockSpec((tm, a.shape[1]), row))
    if na:
        in_specs.append(pl.BlockSpec(w.shape, (lambda i: (0, 0, 0)) if w.ndim == 3 else (lambda i: (0, 0))))
    return _call(body, name,
                 (jax.ShapeDtypeStruct((m, D), F32), jax.ShapeDtypeStruct((m, D), BF16),
                  jax.ShapeDtypeStruct((1, D), F32), jax.ShapeDtypeStruct((1, D), F32)),
                 (m // tm,), in_specs,
                 (pl.BlockSpec((tm, D), row), pl.BlockSpec((tm, D), row), vec, vec),
                 side=side)(dy, r, g.reshape(1, D), *a_list, *([w] if na else []))


def _ffn_act_bwd(drb, wdown, up3, w, wb, tm, side=None):
    _, m, dff = up3.shape
    ffs = dff // 2

    def body(dr_ref, wd_ref, x_ref, xh_ref, wu_ref, wg_ref, bu_ref, bg_ref, o_ref, win, dact):
        win[:, 0:FHALO, :] = xh_ref[...]
        win[:, FHALO:FHALO + tm, :] = x_ref[...]
        dact[...] = _dot_nt(dr_ref[...], wd_ref[...])
        for r0 in range(0, tm, RCH):
            for c0, c1 in _lane_groups(ffs):
                u = _conv3(win, 0, r0, c0, c1, wu_ref, bu_ref)
                g = _conv3(win, 1, r0, c0, c1, wg_ref, bg_ref)
                da = dact[r0:r0 + RCH, c0:c1]
                cdf = 0.5 * (1.0 + lax.erf(g * (1.0 / math.sqrt(2.0))))
                pdf = jnp.exp(-0.5 * g * g) * (1.0 / math.sqrt(2.0 * math.pi))
                o_ref[0, r0:r0 + RCH, c0:c1] = da * (g * cdf)
                o_ref[1, r0:r0 + RCH, c0:c1] = da * u * (cdf + g * pdf)

    specs = [pl.BlockSpec((tm, D), lambda c, i: (i, 0)), pl.BlockSpec((ffs, D), lambda c, i: (c, 0))] + _ffn_specs(tm, ffs, m)
    return _call(body, "ffn_act_bwd", jax.ShapeDtypeStruct((2, m, dff), F32), (2, m // tm), specs,
                 pl.BlockSpec((2, tm, ffs), lambda c, i: (0, i, c)),
                 scratch=[pltpu.VMEM((2, FHALO + tm, ffs), F32), pltpu.VMEM((tm, ffs), F32)],
                 side=side)(drb, wdown, up3, up3, w, w, wb, wb)


def _ffn_conv_bwd(dup3, up3, w, tm, nex, tp, side=None):
    _, m, dff = up3.shape
    ffs = dff // 2
    sub = tm // FHALO
    nt = m // tm

    def body(d_ref, dh_ref, x_ref, wu_ref, wg_ref, o_ref, dw_ref, db_ref, dwin, dwacc, dbacc):
        i = pl.program_id(1)
        dwin[:, 0:tm, :] = d_ref[...]
        dwin[:, tm:tm + FHALO, :] = jnp.where(i == nt - 1, 0.0, dh_ref[...])
        dwacc[...] = jnp.zeros_like(dwacc)
        dbacc[...] = jnp.zeros_like(dbacc)

        @pl.when(i == 0)
        def _():
            dw_ref[...] = jnp.zeros_like(dw_ref)
            db_ref[...] = jnp.zeros_like(db_ref)

        fold = lambda t: t[0:8, :] + t[8:16, :]
        for r0 in range(0, tm, RCH):
            rows = i * tm + r0 + lax.broadcasted_iota(jnp.int32, (RCH, 1), 0)
            pad = _pad_rows(rows, nex, tp)
            for p, w_ref in ((0, wu_ref), (1, wg_ref)):
                for c0, c1 in _lane_groups(ffs):
                    dblk = dwin[p, r0:r0 + RCH + FHALO, c0:c1]
                    d0 = dblk[:RCH]
                    d1 = pltpu.roll(dblk, RCH + FHALO - 1, axis=0)[:RCH]
                    d2 = pltpu.roll(dblk, RCH + FHALO - 2, axis=0)[:RCH]
                    dpre = w_ref[2:3, c0:c1] * d0 + w_ref[1:2, c0:c1] * d1 + w_ref[0:1, c0:c1] * d2
                    o_ref[p, r0:r0 + RCH, c0:c1] = jnp.where(pad, 0.0, dpre).astype(BF16)
                    x0 = x_ref[p, r0:r0 + RCH, c0:c1]
                    dwacc[p, 2, :, c0:c1] += fold(d0 * x0)
                    dwacc[p, 1, :, c0:c1] += fold(d1 * x0)
                    dwacc[p, 0, :, c0:c1] += fold(d2 * x0)
                    dbacc[p, :, c0:c1] += fold(d0)
        for p in range(2):
            for k in range(FTAPS):
                dw_ref[p, k:k + 1, :] += jnp.sum(dwacc[p, k], axis=0, keepdims=True)
            db_ref[p] += jnp.sum(dbacc[p], axis=0, keepdims=True)

    nxt = lambda i: jnp.minimum((i + 1) * sub, m // FHALO - 1)
    return _call(body, "ffn_conv_bwd",
                 (jax.ShapeDtypeStruct((2, m, dff), BF16), jax.ShapeDtypeStruct((2, FTAPS, dff), F32),
                  jax.ShapeDtypeStruct((2, 1, dff), F32)),
                 (2, nt),
                 [pl.BlockSpec((2, tm, ffs), lambda c, i: (0, i, c)), pl.BlockSpec((2, FHALO, ffs), lambda c, i: (0, nxt(i), c)),
                  pl.BlockSpec((2, tm, ffs), lambda c, i: (0, i, c)),
                  pl.BlockSpec((FTAPS, ffs), lambda c, i: (0, c)), pl.BlockSpec((FTAPS, ffs), lambda c, i: (0, c + 2))],
                 (pl.BlockSpec((2, tm, ffs), lambda c, i: (0, i, c)), pl.BlockSpec((2, FTAPS, ffs), lambda c, i: (0, 0, c)),
                  pl.BlockSpec((2, 1, ffs), lambda c, i: (0, 0, c))),
                 scratch=[pltpu.VMEM((2, tm + FHALO, ffs), F32),
                          pltpu.VMEM((2, FTAPS, 8, ffs), F32), pltpu.VMEM((2, 8, ffs), F32)], side=side)(dup3, dup3, up3, w, w)


def _mm_tn(name, a, b, tk, tn, b_cols=None, chip_out=False):
    m, k = a.shape
    n = b.shape[-1] * (2 if b.ndim == 3 else 1)

    def body(a_ref, b_ref, o_ref):
        o_ref[...] = _dot_tn(a_ref[...], b_ref[...])

    if b.ndim == 3:
        bspec = pl.BlockSpec((None, m, tn), lambda kk, j: (b_cols(j)[0], 0, b_cols(j)[1]))
    else:
        bspec = pl.BlockSpec((m, tn), lambda kk, j: (0, j))
    if chip_out:
        oshape, ospec = (n // tn, k, tn), pl.BlockSpec((None, tk, tn), lambda kk, j: (j, kk, 0))
    else:
        oshape, ospec = (k, n), pl.BlockSpec((tk, tn), lambda kk, j: (kk, j))
    return _call(body, name, jax.ShapeDtypeStruct(oshape, F32), (k // tk, n // tn),
                 [pl.BlockSpec((m, tk), lambda kk, j: (0, kk)), bspec], ospec)(a, b)


def _gate_bwd(drb, wout, ya, yc, z, tm):
    m = drb.shape[0]

    def body(dr_ref, w_ref, ya_ref, yc_ref, ga_ref, gc_ref, dya_ref, dyc_ref, dz_ref, cs_ref):
        i = pl.program_id(0)
        dmix = _dot_nt(dr_ref[...], w_ref[...])
        sa, sc = _sigmoid(ga_ref[...]), _sigmoid(gc_ref[...])
        dya_ref[...] = (dmix * sa).astype(BF16)
        dyc_ref[...] = (dmix * sc).astype(BF16)
        dga = dmix * ya_ref[...] * sa * (1.0 - sa)
        dgc = dmix * yc_ref[...] * sc * (1.0 - sc)
        dz_ref[:, 0:D] = dga.astype(BF16)
        dz_ref[:, D:2 * D] = dgc.astype(BF16)

        @pl.when(i == 0)
        def _():
            cs_ref[...] = jnp.zeros_like(cs_ref)

        cs_ref[:, 0:D] += jnp.sum(dga, axis=0, keepdims=True)
        cs_ref[:, D:2 * D] += jnp.sum(dgc, axis=0, keepdims=True)

    row = lambda i: (i, 0)
    return _call(body, "gate_bwd",
                 (jax.ShapeDtypeStruct((m, D), BF16), jax.ShapeDtypeStruct((m, D), BF16),
                  jax.ShapeDtypeStruct((m, 2 * D), BF16), jax.ShapeDtypeStruct((1, 2 * D), F32)),
                 (m // tm,),
                 [pl.BlockSpec((tm, D), row), pl.BlockSpec((D, D), lambda i: (0, 0)), pl.BlockSpec((tm, D), row),
                  pl.BlockSpec((tm, D), row), pl.BlockSpec((tm, D), lambda i: (i, 0)), pl.BlockSpec((tm, D), lambda i: (i, 1))],
                 (pl.BlockSpec((tm, D), row), pl.BlockSpec((tm, D), row), pl.BlockSpec((tm, 2 * D), row),
                  pl.BlockSpec((1, 2 * D), lambda i: (0, 0))))(drb, wout, ya, yc, z, z)


def _conv_bwd_a(dyc, wcp4, cc, g, b, tm):
    m = cc.shape[0]
    ns = wcp4.shape[2]

    def body(dy_ref, w_ref, cc_ref, g_ref, b_ref, dcc_ref, dg_ref, db_ref, dwb_ref):
        i = pl.program_id(0)
        dcs = jnp.zeros((tm, CW), F32)
        for j in range(NCHIP):
            dcs = dcs + _dot_nt(dy_ref[:, j * ns:(j + 1) * ns], w_ref[j])
        xhat, rstd = _ln_stats(cc_ref[...])
        cl = xhat * g_ref[...] + b_ref[...]
        sg = _sigmoid(cl)
        dcl = dcs * sg * (1.0 + cl * (1.0 - sg))
        dcc = _ln_bwd(dcl, xhat, rstd, g_ref[...])
        dcc_ref[...] = dcc

        @pl.when(i == 0)
        def _():
            dg_ref[...] = jnp.zeros_like(dg_ref)
            db_ref[...] = jnp.zeros_like(db_ref)
            dwb_ref[...] = jnp.zeros_like(dwb_ref)

        dg_ref[...] += jnp.sum(dcl * xhat, axis=0, keepdims=True)
        db_ref[...] += jnp.sum(dcl, axis=0, keepdims=True)
        dwb_ref[...] += jnp.sum(dcc, axis=0, keepdims=True)

    row = lambda i: (i, 0)
    vec = pl.BlockSpec((1, CW), lambda i: (0, 0))
    v = jax.ShapeDtypeStruct((1, CW), F32)
    return _call(body, "conv_bwd_a", (jax.ShapeDtypeStruct((m, CW), F32), v, v, v), (m // tm,),
                 [pl.BlockSpec((tm, D), row), pl.BlockSpec((NCHIP, CW, ns), lambda i: (0, 0, 0)), pl.BlockSpec((tm, CW), row), vec, vec],
                 (pl.BlockSpec((tm, CW), row), vec, vec, vec))(dyc, wcp4, cc, g.reshape(1, CW), b.reshape(1, CW))


def _conv_bwd_b(dcc, z, w, nex, tp, side=None):
    m = dcc.shape[0]
    tm = BLK
    sub = tm // CHALO
    nt = m // tm
    cvc, cgc = C_CONV // CW, C_CONV // CW + 1

    def body(d_ref, dh_ref, cv_ref, cg_ref, w_ref, dz_ref, dw_ref, cs_ref, dwin, dwacc):
        i = pl.program_id(0)
        rows = i * tm + lax.broadcasted_iota(jnp.int32, (tm, 1), 0)
        dwin[0:tm, :] = d_ref[...]
        dwin[tm:tm + CHALO, :] = jnp.where(i == nt - 1, 0.0, dh_ref[...])

        @pl.when(i == 0)
        def _():
            dwacc[...] = jnp.zeros_like(dwacc)
            cs_ref[...] = jnp.zeros_like(cs_ref)

        fold = lambda t: (t[0:8] + t[8:16]) + (t[16:24] + t[24:32])
        for sb in range(sub):
            lo = sb * CHALO
            pad = _pad_rows(rows[lo:lo + CHALO], nex, tp)
            for c0 in range(0, CW, CLANES):
                cs_ = slice(c0, c0 + CLANES)
                cv = cv_ref[lo:lo + CHALO, cs_]
                sg = _sigmoid(cg_ref[lo:lo + CHALO, cs_])
                cgin = jnp.where(pad, 0.0, cv * sg)
                blk = dwin[lo:lo + 2 * CHALO, cs_]
                acc = jnp.zeros((CHALO, CLANES), F32)
                for b in range(8):
                    rb = _rolled_up(blk, b)
                    for a in range(4):
                        s = 8 * a + b
                        if s <= CTAPS - 1:
                            k = CTAPS - 1 - s
                            sh = rb[8 * a:8 * a + CHALO]
                            acc = acc + w_ref[k:k + 1, cs_] * sh
                            dwacc[k, :, cs_] += fold(sh * cgin)
                dcg = jnp.where(pad, 0.0, acc)
                dcv = dcg * sg
                dgt = dcg * cv * sg * (1.0 - sg)
                dz_ref[lo:lo + CHALO, cs_] = dcv.astype(BF16)
                dz_ref[lo:lo + CHALO, CW + c0:CW + c0 + CLANES] = dgt.astype(BF16)
                cs_ref[:, cs_] += jnp.sum(dcv, axis=0, keepdims=True)
                cs_ref[:, CW + c0:CW + c0 + CLANES] += jnp.sum(dgt, axis=0, keepdims=True)

        @pl.when(i == nt - 1)
        def _():
            for k in range(CHALO):
                dw_ref[k:k + 1, :] = jnp.sum(dwacc[k], axis=0, keepdims=True)

    nxt = lambda i: jnp.minimum((i + 1) * sub, m // CHALO - 1)
    return _call(body, "conv_bwd_b",
                 (jax.ShapeDtypeStruct((m, 2 * CW), BF16), jax.ShapeDtypeStruct((CHALO, CW), F32),
                  jax.ShapeDtypeStruct((1, 2 * CW), F32)),
                 (nt,),
                 [pl.BlockSpec((tm, CW), lambda i: (i, 0)), pl.BlockSpec((CHALO, CW), lambda i: (nxt(i), 0)),
                  pl.BlockSpec((tm, CW), lambda i: (i, cvc)), pl.BlockSpec((tm, CW), lambda i: (i, cgc)),
                  pl.BlockSpec((CTAPS, CW), lambda i: (0, 0))],
                 (pl.BlockSpec((tm, 2 * CW), lambda i: (i, 0)), pl.BlockSpec((CHALO, CW), lambda i: (0, 0)),
                  pl.BlockSpec((1, 2 * CW), lambda i: (0, 0))),
                 scratch=[pltpu.VMEM((tm + CHALO, CW), F32), pltpu.VMEM((CHALO, 8, CW), F32)], side=side)(dcc, dcc, z, z, w)


def _attn_bwd(z, bias, sinks, dya, wap4, nex, nblk, side=None):
    m = z.shape[0]
    tp = nblk * BLK
    ns = wap4.shape[2]
    blk_of = lambda j: nblk - 1 - j
    z3 = z.reshape(nex, tp, z.shape[1])
    dya3 = dya.reshape(nex, tp, D)

    def body(q_ref, cur_ref, prev_ref, meta_ref, bias_ref, sink_ref, dy_ref, w_ref,
             dz_ref, cs_ref, dsk_ref, dbias_ref, ks, vs, carry, macc, dqacc, dkv, okv):
        j = pl.program_id(0)
        n = nblk - 1 - j

        @pl.when(j == 0)
        def _():
            carry[...] = jnp.zeros_like(carry)
            macc[...] = jnp.zeros_like(macc)
            cs_ref[...] = jnp.zeros_like(cs_ref)
            dsk_ref[...] = jnp.zeros_like(dsk_ref)

        @pl.when((j == 0) | (n <= 1))
        def _():
            dbias_ref[...] = jnp.zeros_like(dbias_ref)

        lane = lax.broadcasted_iota(jnp.int32, (1, BLK), 1)
        dsk = jnp.zeros((1, BLK), F32)
        for e in range(nex):
            _fill_kv(ks, vs, e, prev_ref, cur_ref, meta_ref)
            q = q_ref[e]
            da = jnp.zeros((BLK, AW), F32)
            for jj in range(NCHIP):
                da = da + _dot_nt(dy_ref[e, :, jj * ns:(jj + 1) * ns], w_ref[jj])
            for hk in range(NKV):
                qg, pn, psink = _softmax_group(q, ks[e, hk], bias_ref, sink_ref, hk)
                dog = jnp.concatenate([da[:, (hk * GRP + g) * HD:(hk * GRP + g + 1) * HD] for g in range(GRP)], axis=0).astype(BF16)
                dp = _dot_nt(dog, vs[e, hk])
                dl = jnp.sum(pn * dp, axis=-1, keepdims=True)
                ds = pn * (dp - dl)
                dbias_ref[hk * GRP:(hk + 1) * GRP] += ds.reshape(GRP, BLK, NKEY)
                dsr = -psink * dl
                for g in range(GRP):
                    dsk = dsk + jnp.where(lane == hk * GRP + g, jnp.sum(dsr[g * BLK:(g + 1) * BLK]), 0.0)
                dsb = (ds * (HD ** -0.5)).astype(BF16)
                dqg = _dot(dsb, ks[e, hk])
                for g in range(GRP):
                    h = hk * GRP + g
                    dqacc[e, :, h * HD:(h + 1) * HD] = dqg[g * BLK:(g + 1) * BLK, :]
                dkv[e, :, hk * HD:(hk + 1) * HD] = _dot_tn(dsb, qg)
                dkv[e, :, KVW + hk * HD:KVW + (hk + 1) * HD] = _dot_tn(pn.astype(BF16), dog)
            macc[e] += dkv[e, 2 * BLK:2 * BLK + N_META, :]
            okv[e] = dkv[e, BLK:2 * BLK, :] + carry[e]
            carry[e] = dkv[e, 0:BLK, :]

            @pl.when(n == 0)
            def _():
                okv[e, PAD:BLK, :] += macc[e]

            dq = dqacc[e]
            ok = okv[e]
            dz_ref[e, :, 0:AW] = dq.astype(BF16)
            dz_ref[e, :, AW:AW + 2 * KVW] = ok.astype(BF16)
            cs_ref[:, 0:AW] += jnp.sum(dq, axis=0, keepdims=True)
            cs_ref[:, AW:AW + 2 * KVW] += jnp.sum(ok, axis=0, keepdims=True)
        dsk_ref[...] += dsk

    wz = AW + 2 * KVW
    specs = _attn_specs(nex, blk_of) + [
        pl.BlockSpec((nex, BLK, D), lambda j: (0, blk_of(j), 0)),
        pl.BlockSpec((NCHIP, AW, ns), lambda j: (0, 0, 0))]
    res = _call(body, "attn_bwd",
                (jax.ShapeDtypeStruct((nex, tp, wz), BF16), jax.ShapeDtypeStruct((1, wz), F32), jax.ShapeDtypeStruct((1, BLK), F32),
                 jax.ShapeDtypeStruct((1, 3, NQ, BLK, NKEY), F32)),
                (nblk,), specs,
                (pl.BlockSpec((nex, BLK, wz), lambda j: (0, blk_of(j), 0)), pl.BlockSpec((1, wz), lambda j: (0, 0)),
                 pl.BlockSpec((1, BLK), lambda j: (0, 0)),
                 pl.BlockSpec((None, None, NQ, BLK, NKEY), lambda j: (0, jnp.minimum(blk_of(j), 2), 0, 0, 0))),
                scratch=[pltpu.VMEM((nex, NKV, NKEY, HD), BF16), pltpu.VMEM((nex, NKV, NKEY, HD), BF16),
                         pltpu.VMEM((nex, BLK, 2 * KVW), F32), pltpu.VMEM((nex, N_META, 2 * KVW), F32),
                         pltpu.VMEM((nex, BLK, AW), F32), pltpu.VMEM((nex, NKEY, 2 * KVW), F32),
                         pltpu.VMEM((nex, BLK, 2 * KVW), F32)],
                side=side)(z3, z3, z3, z3, bias, sinks.reshape(1, NQ), dya3, wap4)
    return (res[0].reshape(m, wz),) + tuple(res[1:])


def _tile_rows(rows, cols, target_bytes=1 << 20):
    best = None
    for t in range(8, rows + 1, 8):
        if rows % t == 0 and t * cols * 4 <= target_bytes:
            best = t
    return best or rows


def _sum0(name, x):
    n, r, c = x.shape
    tr = _tile_rows(r, c * n)

    def body(x_ref, o_ref):
        acc = x_ref[0]
        for k in range(1, n):
            acc = acc + x_ref[k]
        o_ref[...] = acc

    return _call(body, name, jax.ShapeDtypeStruct((r, c), F32), (r // tr,),
                 [pl.BlockSpec((n, tr, c), lambda i: (0, i, 0))], pl.BlockSpec((tr, c), lambda i: (i, 0)))(x)


def _adamw(name, w, g, mom, vel):
    r, c = w.shape
    tr = _tile_rows(r, c)
    c1 = 1.0 / (1.0 - ADAM_B1 ** ADAM_STEP)
    c2 = 1.0 / (1.0 - ADAM_B2 ** ADAM_STEP)

    def body(w_ref, g_ref, m_ref, v_ref, d_ref, mo_ref, vo_ref):
        gg = g_ref[...]
        mn = ADAM_B1 * m_ref[...] + (1.0 - ADAM_B1) * gg
        vn = ADAM_B2 * v_ref[...] + (1.0 - ADAM_B2) * (gg * gg)
        mo_ref[...] = mn
        vo_ref[...] = vn
        d_ref[...] = -ADAM_LR * ((mn * c1) / (jnp.sqrt(vn * c2) + ADAM_EPS) + ADAM_WD * w_ref[...])

    spec = pl.BlockSpec((tr, c), lambda i: (i, 0))
    o = jax.ShapeDtypeStruct((r, c), F32)
    return _call(body, name, (o, o, o), (r // tr,), [spec] * 4, (spec, spec, spec))(w, g, mom, vel)


def _place():
    x, y, c = lax.axis_index("x"), lax.axis_index("y"), lax.axis_index("c")
    others = [(1 - x, y), (x, 1 - y), (1 - x, 1 - y)]
    return x, y, c, others


def _gather_job(items):
    nw = len(items)

    def views(s_ref, g_ref, layer, c):
        if layer is None:
            return s_ref.at[c], lambda chip, cc: g_ref.at[chip, cc]
        hr = s_ref.shape[1] // 2
        return s_ref.at[layer, pl.ds(c * hr, hr)], lambda chip, cc: g_ref.at[chip, pl.ds(cc * hr, hr)]

    def copies(s_refs, g_refs, send, recv):
        x, y, c, others = _place()
        chip = 2 * x + y
        firsts, arrive, passed, arrive2 = [], [], [], []
        for w, (_, layer) in enumerate(items):
            src, dst = views(s_refs[w], g_refs[w], layer, c)
            for k, (px, py) in enumerate(others):
                def rc(kk, s, d, to, w=w):
                    return pltpu.make_async_remote_copy(src_ref=s, dst_ref=d, send_sem=send.at[w, kk], recv_sem=recv.at[w, kk],
                                                        device_id=to, device_id_type=MESH)
                got, got2 = dst(2 * px + py, c), dst(2 * px + py, 1 - c)
                firsts.append(rc(k, src, dst(chip, c), (px, py, c)))
                arrive.append(rc(k, got, got, (x, y, c)))
                passed.append(rc(3 + k, got, got, (x, y, 1 - c)))
                arrive2.append(rc(3 + k, got2, got2, (x, y, c)))
        return firsts, arrive, passed, arrive2

    def start(s_refs, g_refs, send, recv):
        for cp in copies(s_refs, g_refs, send, recv)[0]:
            cp.start()

    def finish(s_refs, g_refs, send, recv):
        firsts, arrive, passed, arrive2 = copies(s_refs, g_refs, send, recv)
        for a, p in zip(arrive, passed):
            a.wait_recv()
            p.start()
        for a in arrive2:
            a.wait_recv()
        for cp in firsts + passed:
            cp.wait_send()

    outs = [jax.ShapeDtypeStruct((NCHIP,) + (s.shape if layer is None else s.shape[1:]), s.dtype) for s, layer in items]
    return _Job([s for s, _ in items], outs, (nw, 6), start, finish)


def _swap_job(grads):
    def copies(d_refs, a_refs, send, recv):
        x, y, c, _ = _place()
        cps = []
        for w in range(len(grads)):
            h = d_refs[w].shape[1] // 2
            cps.append(pltpu.make_async_remote_copy(
                src_ref=d_refs[w].at[:, pl.ds((1 - c) * h, h), :], dst_ref=a_refs[w], send_sem=send.at[w], recv_sem=recv.at[w],
                device_id=(x, y, 1 - c), device_id_type=MESH))
        return cps

    def start(*r):
        for cp in copies(*r):
            cp.start()

    def finish(*r):
        for cp in copies(*r):
            cp.wait()

    outs = [jax.ShapeDtypeStruct((NCHIP, g.shape[1] // 2, g.shape[2]), g.dtype) for g in grads]
    return _Job(list(grads), outs, (len(grads),), start, finish)


def _exchange_job(parts):
    def copies(q_refs, b_refs, send, recv):
        x, y, c, others = _place()
        cps = []
        for w in range(len(parts)):
            for k, (px, py) in enumerate(others):
                cps.append(pltpu.make_async_remote_copy(
                    src_ref=q_refs[w].at[2 * px + py], dst_ref=b_refs[w].at[k], send_sem=send.at[w, k], recv_sem=recv.at[w, k],
                    device_id=(px, py, c), device_id_type=MESH))
        return cps

    def start(*r):
        for cp in copies(*r):
            cp.start()

    def finish(*r):
        for cp in copies(*r):
            cp.wait()

    outs = [jax.ShapeDtypeStruct((3,) + p.shape[1:], p.dtype) for p in parts]
    return _Job(list(parts), outs, (len(parts), 3), start, finish)


def _run_job(name, job):
    n_in, n_out = len(job.ins), len(job.outs)

    def body(*refs):
        ins, outs = refs[:n_in], refs[n_in:n_in + n_out]
        send, recv = refs[n_in + n_out:]
        job.start(ins, outs, send, recv)
        job.finish(ins, outs, send, recv)

    return pl.pallas_call(
        body, name=name, out_shape=tuple(job.outs), in_specs=[_ANY] * n_in, out_specs=tuple([_ANY] * n_out),
        scratch_shapes=[pltpu.SemaphoreType.DMA(job.sems), pltpu.SemaphoreType.DMA(job.sems)])(*job.ins)


def _sibling_join(halves):
    nw = len(halves)

    def body(*refs):
        h_refs = refs[:2 * nw]
        f_refs = refs[2 * nw:3 * nw]
        send, recv = refs[3 * nw:]
        x, y, c, _ = _place()
        cps = []
        for w in range(nw):
            for l in range(2):
                src = h_refs[2 * w + l]
                h = src.shape[0]
                dst = f_refs[w].at[l, pl.ds(c * h, h), :]
                cp = pltpu.make_async_remote_copy(src_ref=src, dst_ref=dst, send_sem=send.at[w, l], recv_sem=recv.at[w, l],
                                                  device_id=(x, y, 1 - c), device_id_type=MESH)
                cp.start()
                cps.append(cp)
        for w in range(nw):
            for l in range(2):
                src = h_refs[2 * w + l]
                h = src.shape[0]
                other = f_refs[w].at[l, pl.ds((1 - c) * h, h), :]
                pltpu.make_async_remote_copy(src_ref=src, dst_ref=other, send_sem=send.at[w, l], recv_sem=recv.at[w, l],
                                             device_id=(x, y, c), device_id_type=MESH).wait_recv()
        for cp in cps:
            cp.wait_send()

    flat = [a for pair in halves for a in pair]
    outs = tuple(jax.ShapeDtypeStruct((2, 2 * pair[0].shape[0], pair[0].shape[1]), F32) for pair in halves)
    return pl.pallas_call(
        body, name="grad_sibling_join", out_shape=outs, in_specs=[_ANY] * (2 * nw), out_specs=tuple([_ANY] * nw),
        scratch_shapes=[pltpu.SemaphoreType.DMA((nw, 2)), pltpu.SemaphoreType.DMA((nw, 2))])(*flat)


def _allgather_small(v):
    r = v.shape[0]

    def body(x_ref, out_ref, send_sems, recv_sems, local_sem):
        x, y, c, chips = _place()
        me, sibling = (x, y, c), (x, y, 1 - c)

        def slab(px, py, pc):
            return out_ref.at[4 * px + 2 * py + pc]

        def copy(k, block, to, src=None):
            return pltpu.make_async_remote_copy(src_ref=slab(*block) if src is None else src, dst_ref=slab(*block),
                                                send_sem=send_sems.at[k], recv_sem=recv_sems.at[k],
                                                device_id=to, device_id_type=MESH)

        mine = pltpu.make_async_copy(x_ref, slab(*me), local_sem)
        mine.start()
        first = [copy(0, me, sibling, src=x_ref)]
        first += [copy(1 + j, me, (*chip, c), src=x_ref) for j, chip in enumerate(chips)]
        for cp in first:
            cp.start()
        passed = [copy(4 + j, (*chip, c), sibling) for j, chip in enumerate(chips)]
        for j, chip in enumerate(chips):
            copy(1 + j, (*chip, c), me).wait_recv()
            passed[j].start()
        copy(0, sibling, me).wait_recv()
        for j, chip in enumerate(chips):
            copy(4 + j, (*chip, 1 - c), me).wait_recv()
        for cp in first + passed:
            cp.wait_send()
        mine.wait()

    return pl.pallas_call(
        body, name="allgather_small", out_shape=jax.ShapeDtypeStruct((8, r, 128), F32),
        in_specs=[pl.BlockSpec(memory_space=pltpu.VMEM)], out_specs=pl.BlockSpec(memory_space=pltpu.VMEM),
        scratch_shapes=[pltpu.SemaphoreType.DMA((7,)), pltpu.SemaphoreType.DMA((7,)), pltpu.SemaphoreType.DMA],
    )(v)


def _add_half(name, d, a, c):
    _, h, cols = a.shape
    nt = 4 if h % 64 == 0 else (2 if h % 32 == 0 else 1)
    th = h // nt

    def body(c_ref, d_ref, a_ref, o_ref):
        o_ref[...] = (d_ref[...] + a_ref[...]).astype(BF16)

    return _call(body, name, jax.ShapeDtypeStruct(a.shape, BF16), (NCHIP, nt),
                 [pl.BlockSpec((None, th, cols), lambda p, i, cr: (p, cr[0] * nt + i, 0)),
                  pl.BlockSpec((None, th, cols), lambda p, i, cr: (p, i, 0))],
                 pl.BlockSpec((None, th, cols), lambda p, i, cr: (p, i, 0)), prefetch=1)(c, d, a)


def _add_chips(name, d, a, b, where):
    _, h, cols = a.shape
    th = h // 4 if (h % 64 == 0) else h
    nt = h // th

    def body(w_ref, d_ref, a_ref, b_ref, o_ref):
        own = d_ref[...] + a_ref[...]
        o_ref[...] = ((own + b_ref[0].astype(F32)) + b_ref[1].astype(F32)) + b_ref[2].astype(F32)

    return _call(body, name, jax.ShapeDtypeStruct((h, cols), F32), (nt,),
                 [pl.BlockSpec((None, th, cols), lambda i, wr: (wr[0], wr[1] * nt + i, 0)),
                  pl.BlockSpec((None, th, cols), lambda i, wr: (wr[0], i, 0)),
                  pl.BlockSpec((3, th, cols), lambda i, wr: (0, i, 0))],
                 pl.BlockSpec((th, cols), lambda i, wr: (i, 0)), prefetch=1)(where, d, a, b)


def _pack(arrs):
    pieces = []
    for a in arrs:
        f = a.reshape(-1)
        n = -(-f.shape[0] // 1024) * 1024
        pieces.append(jnp.pad(f, (0, n - f.shape[0])).reshape(-1, 128))
    return jnp.concatenate(pieces, axis=0)


def _unpack(buf, shapes):
    out, r = [], 0
    for s in shapes:
        n = int(np.prod(s))
        rows = -(-n // 1024) * 8
        out.append(buf[r:r + rows].reshape(-1)[:n].reshape(s))
        r += rows
    return out


def kernel(x, meta_tokens, in_ln_g, in_ln_b, rel_bias, w_in, b_in, attn_sinks, w_attn_proj, conv_dw, conv_dw_b, conv_ln_g, conv_ln_b, w_conv_proj, w_out, ln1_g, ln1_b, ffn_w_up, ffn_dw, ffn_dw_b, ffn_w_down, ln2_g, ln2_b, loss_target, m_meta_tokens, m_in_ln_g, m_in_ln_b, m_rel_bias, m_w_in, m_b_in, m_attn_sinks, m_w_attn_proj, m_conv_dw, m_conv_dw_b, m_conv_ln_g, m_conv_ln_b, m_w_conv_proj, m_w_out, m_ln1_g, m_ln1_b, m_ffn_w_up, m_ffn_dw, m_ffn_dw_b, m_ffn_w_down, m_ln2_g, m_ln2_b, v_meta_tokens, v_in_ln_g, v_in_ln_b, v_rel_bias, v_w_in, v_b_in, v_attn_sinks, v_w_attn_proj, v_conv_dw, v_conv_dw_b, v_conv_ln_g, v_conv_ln_b, v_w_conv_proj, v_w_out, v_ln1_g, v_ln1_b, v_ffn_w_up, v_ffn_dw, v_ffn_dw_b, v_ffn_w_down, v_ln2_g, v_ln2_b):
    nex, seq, _ = x.shape
    nblk = seq // BLK + 1
    tp = nblk * BLK
    m = nex * tp
    tm = _row_tile(m)
    ffs = ffn_w_up.shape[2]
    dff = 2 * ffs
    cx, cy, cc = lax.axis_index("x"), lax.axis_index("y"), lax.axis_index("c")
    chip = (2 * cx + cy).astype(jnp.int32)
    core = cc.astype(jnp.int32)

    names = ("in", "ap", "cp", "out", "up", "down")
    big = dict(zip(names, [w_in, w_attn_proj, w_conv_proj, w_out, ffn_w_up, ffn_w_down]))
    sb = {k: v.astype(BF16) for k, v in big.items()}
    gathered = {}

    def land(items, outs):
        return [lax.dynamic_update_index_in_dim(g, s if layer is None else s[layer], chip, 0)
                for (s, layer), g in zip(items, outs)]

    first_items = [(meta_tokens.reshape(2, N_META // 2, -1), None), (conv_dw, None), (ffn_dw, None)]
    g_meta, g_cdw, g_fdw = land(first_items, _run_job("gather_small", _gather_job(first_items)))
    meta_full = jnp.transpose(g_meta, (1, 2, 0, 3)).reshape(N_META, D)
    bias_tab = _bias_build(rel_bias)

    fwd_plan = {("embed_ln", 0): [("in", 0)],
                ("in_proj", 0): [("ap", 0), ("cp", 0), ("out", 0)], ("attn_fwd", 0): [("up", 0)],
                ("conv_fwd", 0): [("down", 0)], ("mix_fwd", 0): [("in", 1)],
                ("out_proj_ln", 0): [("ap", 1), ("cp", 1), ("out", 1)], ("ffn_up_act", 0): [("up", 1), ("down", 1)]}

    def fwd(tag, l, fn, *args):
        keys = fwd_plan.get((tag, l))
        if not keys:
            return fn(*args)
        items = [(sb[k], kl) for k, kl in keys]
        res = fn(*args, side=_gather_job(items))
        for key, g in zip(keys, land(items, res[-len(keys):])):
            gathered[key] = g
        main = res[:-len(keys)]
        return main[0] if len(main) == 1 else main

    def layer_weights(l):
        win_old = jnp.transpose(gathered[("in", l)], (1, 0, 2)).reshape(D, IN_COLS)
        return dict(
            win=_to_new(win_old), bin=_to_new(b_in[l]).reshape(1, IN_COLS),
            cdw=jnp.transpose(g_cdw[:, l], (1, 0, 2)).reshape(CTAPS, CW),
            fdw=jnp.transpose(g_fdw[:, l], (1, 0, 2)).reshape(FTAPS, 2 * dff),
            fdwb=ffn_dw_b[l].reshape(1, 2 * dff))

    raw, h, hb = fwd("embed_ln", 0, _embed_ln, x, meta_full, in_ln_g, in_ln_b, nblk)
    saved, lw = [], []
    for l in range(DEPTH):
        p = layer_weights(l)
        z = fwd("in_proj", l, functools.partial(_mm_bias, "in_proj"), hb, p["win"], p["bin"], IN_COLS // 3, tm)
        a = fwd("attn_fwd", l, _attn_fwd, z, bias_tab, attn_sinks[l], nex, nblk)
        ccv, cs = fwd("conv_fwd", l, _conv_fwd, z, p["cdw"], conv_dw_b[l], conv_ln_g[l], conv_ln_b[l], nex, tp)
        p["wap"], p["wcp"] = gathered[("ap", l)], gathered[("cp", l)]
        ya, yc, mixed = fwd("mix_fwd", l, _mix_fwd, a, cs, p["wap"], p["wcp"], z, tm)
        p["wout"] = gathered[("out", l)].reshape(D, D)
        r1, h1, h1b = fwd("out_proj_ln", l, functools.partial(_mm_res_ln, "out_proj_ln"), mixed, p["wout"], h, ln1_g[l], ln1_b[l], tm)
        p["wup"] = gathered[("up", l)]
        up3, act = fwd("ffn_up_act", l, _ffn_up_act, h1b, p["wup"], p["fdw"], p["fdwb"], tm, nex, tp)
        p["wdown"] = gathered[("down", l)].reshape(dff, D)
        r2, h2, h2b = _mm_res_ln("ffn_down_ln", act, p["wdown"], h1, ln2_g[l], ln2_b[l], tm)
        saved.append(dict(hb=hb, z=z, a=a, cc=ccv, cs=cs, ya=ya, yc=yc, mixed=mixed, r1=r1, h1b=h1b, up3=up3, act=act, r2=r2))
        lw.append(p)
        h, hb = h2, h2b

    dy, sq = _loss_grad(h, loss_target, nblk)
    loss = lax.psum(0.5 / D * jnp.sum(sq), ("x", "y", "c"))

    grads, swapped, pair_sums, reduced = {}, {}, {}, {}
    cvec, where = core.reshape(1), jnp.stack([chip, core])
    last = [(k, DEPTH - 1) for k in names]
    bwd_plan = {("ln2_bwd", 0): ("swap", last),
                ("ffn_act_bwd", 0): ("exch", [("up", 1), ("down", 1)]),
                ("ffn_conv_bwd", 0): ("exch", [("in", 1), ("ap", 1), ("cp", 1), ("out", 1)]),
                ("ln1_bwd", 0): ("swap", [("down", 0), ("up", 0)]),
                ("conv_bwd_b", 0): ("swap", [("out", 0), ("ap", 0), ("cp", 0)]),
                ("attn_bwd", 0): ("exch", [("down", 0), ("up", 0), ("out", 0), ("ap", 0), ("cp", 0)]),
                ("in_ln_bwd", 0): ("swap", [("in", 0)])}

    def after(kind, keys, outs):
        for key, o in zip(keys, outs):
            if kind == "swap":
                swapped[key] = o
                pair_sums[key] = _add_half("grad_add_sibling", grads[key], o, cvec)
            else:
                reduced[key] = _add_chips("grad_add_chips", grads[key], swapped[key], o, where)

    def bwd(tag, l, fn, *args):
        plan = bwd_plan.get((tag, l))
        if plan is None:
            return fn(*args)
        kind, keys = plan
        job = _swap_job([grads[k] for k in keys]) if kind == "swap" else _exchange_job([pair_sums[k] for k in keys])
        res = fn(*args, side=job)
        after(kind, keys, res[-len(keys):])
        main = res[:-len(keys)]
        return main[0] if len(main) == 1 else main

    small = {}
    prev_a, prev_w, prev_cols = (), None, None
    dprev = dy
    for l in reversed(range(DEPTH)):
        p, s = lw[l], saved[l]
        dr2, dr2b, dg2, db2 = bwd("ln2_bwd", l, functools.partial(_ln_bwd_call, "ln2_bwd"), dprev, s["r2"], ln2_g[l], tm,
                                  prev_a, prev_w, prev_cols)
        dup3 = bwd("ffn_act_bwd", l, _ffn_act_bwd, dr2b, p["wdown"], s["up3"], p["fdw"], p["fdwb"], tm)
        dpre3, dfdw, dfdwb = bwd("ffn_conv_bwd", l, _ffn_conv_bwd, dup3, s["up3"], p["fdw"], tm, nex, tp)
        grads[("down", l)] = _mm_tn("dw_down", s["act"], dr2b, ffs, D // 2).reshape(NCHIP, dff // NCHIP, D)
        grads[("up", l)] = _mm_tn("dw_up", s["h1b"], dpre3, D, ffs, b_cols=lambda j: (j // 2, j % 2), chip_out=True)
        dr1, dr1b, dg1, db1 = bwd("ln1_bwd", l, functools.partial(_ln_bwd_call, "ln1_bwd"), dr2, s["r1"], ln1_g[l], tm // 2,
                                  (dpre3,), p["wup"])
        dya, dyc, dzg, csg = _gate_bwd(dr1b, p["wout"], s["ya"], s["yc"], s["z"], tm)
        grads[("out", l)] = _mm_tn("dw_out", s["mixed"], dr1b, D, D // 2).reshape(NCHIP, D // NCHIP, D)
        grads[("ap", l)] = _mm_tn("dw_attn_proj", s["a"], dya, AW, D // NCHIP, chip_out=True)
        grads[("cp", l)] = _mm_tn("dw_conv_proj", s["cs"], dyc, CW, D // NCHIP, chip_out=True)
        dcc, dclg, dclb, dcwb = _conv_bwd_a(dyc, p["wcp"], s["cc"], conv_ln_g[l], conv_ln_b[l], tm)
        dzc, dcdw, csc = bwd("conv_bwd_b", l, _conv_bwd_b, dcc, s["z"], p["cdw"], nex, tp)
        dzq, csq, dsk, dbias = bwd("attn_bwd", l, _attn_bwd, s["z"], bias_tab, attn_sinks[l], dya, p["wap"], nex, nblk)
        gin = [_mm_tn("dw_in_gates", s["hb"], dzg, D, D // 2), _mm_tn("dw_in_conv", s["hb"], dzc, D, CW),
               _mm_tn("dw_in_qkv", s["hb"], dzq, D, 2 * KVW)]
        gin_old = _to_old(jnp.concatenate(gin, axis=1))
        grads[("in", l)] = jnp.transpose(gin_old.reshape(D, NCHIP, IN_COLS // NCHIP), (1, 0, 2))
        small[l] = dict(
            b_in=_to_old(jnp.concatenate([csg, csc, csq], axis=1)).reshape(IN_COLS), attn_sinks=dsk[0, :NQ],
            conv_dw=dcdw[:CTAPS], conv_dw_b=dcwb.reshape(CW), conv_ln_g=dclg.reshape(CW), conv_ln_b=dclb.reshape(CW),
            ln1_g=dg1.reshape(D), ln1_b=db1.reshape(D),
            ffn_dw=jnp.transpose(dfdw, (1, 0, 2)).reshape(FTAPS, 2 * dff), ffn_dw_b=jnp.transpose(dfdwb, (1, 0, 2)).reshape(2 * dff),
            ln2_g=dg2.reshape(D), ln2_b=db2.reshape(D), bias=dbias)
        dprev = dr1
        prev_a, prev_w, prev_cols = (dzg, dzc, dzq), p["win"], [(C_GATES, C_CONV), (C_CONV, C_QKV), (C_QKV, IN_COLS)]
    draw, _, dg0, db0 = bwd("in_ln_bwd", 0, functools.partial(_ln_bwd_call, "in_ln_bwd"), dprev, raw, in_ln_g, tm,
                            prev_a, prev_w, prev_cols)
    draw3 = draw.reshape(nex, tp, D)
    grad_x = draw3[:, BLK:, :]
    dmeta = _sum0("meta_grad_sum", draw3[:, PAD:BLK, :])

    names_l = ["b_in", "attn_sinks", "conv_dw", "conv_dw_b", "conv_ln_g", "conv_ln_b", "ln1_g", "ln1_b", "ffn_dw", "ffn_dw_b", "ln2_g", "ln2_b"]
    dbias_all = _bias_grad(jnp.concatenate([small[l]["bias"] for l in range(DEPTH)], axis=0))
    part_list = [dmeta, dg0.reshape(D), db0.reshape(D), dbias_all]
    part_list += [jnp.stack([small[0][n], small[1][n]]) for n in names_l]
    shapes_small = [tuple(a.shape) for a in part_list]
    tot = _sum0("small_grad_sum", _allgather_small(_pack(part_list)))
    (g_meta_f, g_inlg, g_inlb, g_biasp, g_bin, g_sinks, g_cdw_f, g_cdwb, g_clg, g_clb, g_l1g, g_l1b, g_fdw_f, g_fdwb,
     g_l2g, g_l2b) = _unpack(tot, shapes_small)
    g_relb = g_biasp
    csh = D // NCHIP
    g_meta_s = lax.dynamic_slice_in_dim(g_meta_f, chip * csh, csh, axis=1)
    g_cdw_s = lax.dynamic_slice_in_dim(g_cdw_f, chip * (CW // NCHIP), CW // NCHIP, axis=2)
    g_fdw_s = lax.dynamic_slice_in_dim(g_fdw_f, chip * ffs, ffs, axis=2)

    tail = [("in", 0)]
    after("exch", tail, _run_job("grad_chip_exchange", _exchange_job([pair_sums[k] for k in tail])))
    joined = _sibling_join([[reduced[(k, l)] for l in range(DEPTH)] for k in names])
    full = []
    for k, f in zip(names, joined):
        hh = reduced[(k, 0)].shape[0]
        for l in range(DEPTH):
            f = lax.dynamic_update_slice(f, reduced[(k, l)][None], (l, core * hh, 0))
        full.append(f)

    moms = [m_w_in, m_w_attn_proj, m_w_conv_proj, m_w_out, m_ffn_w_up, m_ffn_w_down]
    vels = [v_w_in, v_w_attn_proj, v_w_conv_proj, v_w_out, v_ffn_w_up, v_ffn_w_down]
    big_out = []
    for w, g, mo, ve in zip(big.values(), full, moms, vels):
        sh = w.shape
        two = lambda t: t.reshape(sh[0] * sh[1], sh[2])
        d_, m_, v_ = _adamw("adamw_matrix", two(w), two(g), two(mo), two(ve))
        big_out.append((g.reshape(sh), d_.reshape(sh), m_.reshape(sh), v_.reshape(sh)))

    sm_w = [meta_tokens, in_ln_g, in_ln_b, rel_bias, b_in, attn_sinks, conv_dw, conv_dw_b, conv_ln_g, conv_ln_b, ln1_g, ln1_b,
            ffn_dw, ffn_dw_b, ln2_g, ln2_b]
    sm_m = [m_meta_tokens, m_in_ln_g, m_in_ln_b, m_rel_bias, m_b_in, m_attn_sinks, m_conv_dw, m_conv_dw_b, m_conv_ln_g, m_conv_ln_b,
            m_ln1_g, m_ln1_b, m_ffn_dw, m_ffn_dw_b, m_ln2_g, m_ln2_b]
    sm_v = [v_meta_tokens, v_in_ln_g, v_in_ln_b, v_rel_bias, v_b_in, v_attn_sinks, v_conv_dw, v_conv_dw_b, v_conv_ln_g, v_conv_ln_b,
            v_ln1_g, v_ln1_b, v_ffn_dw, v_ffn_dw_b, v_ln2_g, v_ln2_b]
    sm_g = [g_meta_s, g_inlg, g_inlb, g_relb, g_bin, g_sinks, g_cdw_s, g_cdwb, g_clg, g_clb, g_l1g, g_l1b, g_fdw_s, g_fdwb, g_l2g, g_l2b]
    sm_shapes = [tuple(a.shape) for a in sm_w]
    sd, smn, svn = _adamw("adamw_small", _pack(sm_w), _pack(sm_g), _pack(sm_m), _pack(sm_v))
    sd, smn, svn = _unpack(sd, sm_shapes), _unpack(smn, sm_shapes), _unpack(svn, sm_shapes)

    order = ["meta_tokens", "in_ln_g", "in_ln_b", "rel_bias", "w_in", "b_in", "attn_sinks", "w_attn_proj", "conv_dw", "conv_dw_b",
             "conv_ln_g", "conv_ln_b", "w_conv_proj", "w_out", "ln1_g", "ln1_b", "ffn_w_up", "ffn_dw", "ffn_dw_b", "ffn_w_down",
             "ln2_g", "ln2_b"]
    small_names = ["meta_tokens", "in_ln_g", "in_ln_b", "rel_bias", "b_in", "attn_sinks", "conv_dw", "conv_dw_b", "conv_ln_g",
                   "conv_ln_b", "ln1_g", "ln1_b", "ffn_dw", "ffn_dw_b", "ln2_g", "ln2_b"]
    big_names = ["w_in", "w_attn_proj", "w_conv_proj", "w_out", "ffn_w_up", "ffn_w_down"]
    res = {}
    for i, n in enumerate(small_names):
        res[n] = (sm_g[i], sd[i], smn[i], svn[i])
    for i, n in enumerate(big_names):
        res[n] = big_out[i]
    outs = [loss, grad_x]
    for k in range(4):
        outs += [res[n][k] for n in order]
    return tuple(outs)
```

```python
import functools
import math
from typing import Any, Callable, NamedTuple, Sequence

import numpy as np
import jax
import jax.numpy as jnp
from jax import lax
from jax.experimental import pallas as pl
from jax.experimental.pallas import tpu as pltpu

F32 = jnp.float32
BF16 = jnp.bfloat16
MESH = pl.DeviceIdType.MESH

D = 1024
N_META = 16
BLK = 128
PAD = BLK - N_META
HD = 64
NQ = 8
NKV = 2
GRP = NQ // NKV
AW = NQ * HD
KVW = NKV * HD
CW = D // 2
CTAPS = 31
FTAPS = 3
NBUCKET = 32
MAXDIST = 128
EPS = 1e-5
DEPTH = 2
ALPHA = (2.0 * DEPTH) ** 0.25
NCHIP = 4
NKEY = 3 * BLK
NEG = -1e30
CHALO = 32
FHALO = 8
IN_COLS = AW + 2 * KVW + 2 * CW + 2 * D
_OLD = dict(q=(0, AW), k=(AW, AW + KVW), v=(AW + KVW, AW + 2 * KVW), cv=(AW + 2 * KVW, AW + 2 * KVW + CW),
            cg=(AW + 2 * KVW + CW, AW + 2 * KVW + 2 * CW), ga=(AW + 2 * KVW + 2 * CW, AW + 2 * KVW + 2 * CW + D),
            gc=(AW + 2 * KVW + 2 * CW + D, IN_COLS))
_NEW_ORDER = ("ga", "gc", "cv", "cg", "q", "k", "v")
C_GATES, C_CONV, C_QKV = 0, 2 * D, 2 * D + 2 * CW

ADAM_LR, ADAM_B1, ADAM_B2, ADAM_EPS, ADAM_WD, ADAM_STEP = 0.001, 0.9, 0.999, 1e-08, 0.01, 10


def _to_new(a):
    return jnp.concatenate([a[..., _OLD[n][0]:_OLD[n][1]] for n in _NEW_ORDER], axis=-1)


def _to_old(a):
    offs, o = {}, 0
    for n in _NEW_ORDER:
        w = _OLD[n][1] - _OLD[n][0]
        offs[n] = (o, o + w)
        o += w
    return jnp.concatenate([a[..., offs[n][0]:offs[n][1]] for n in ("q", "k", "v", "cv", "cg", "ga", "gc")], axis=-1)


class _Job(NamedTuple):
    ins: Sequence[Any]
    outs: Sequence[Any]
    sems: tuple
    start: Callable
    finish: Callable


_ANY = pl.BlockSpec(memory_space=pl.ANY)


def _call(body, name, out_shape, grid, in_specs, out_specs, scratch=(), prefetch=0, side=None):
    params = pltpu.CompilerParams(dimension_semantics=("arbitrary",) * len(grid))
    if side is None:
        if prefetch:
            gs = pltpu.PrefetchScalarGridSpec(num_scalar_prefetch=prefetch, grid=grid, in_specs=in_specs,
                                              out_specs=out_specs, scratch_shapes=list(scratch))
            return pl.pallas_call(body, name=name, out_shape=out_shape, grid_spec=gs, compiler_params=params)
        return pl.pallas_call(body, name=name, out_shape=out_shape, grid=grid, in_specs=in_specs, out_specs=out_specs,
                              scratch_shapes=list(scratch), compiler_params=params)
    assert not prefetch
    single = not isinstance(out_shape, (tuple, list))
    main_shapes = (out_shape,) if single else tuple(out_shape)
    main_specs = (out_specs,) if single else tuple(out_specs)
    n_in, n_sin, n_out, n_sout, n_scr = len(in_specs), len(side.ins), len(main_shapes), len(side.outs), len(scratch)

    def wrapped(*refs):
        main_in, sin = refs[:n_in], refs[n_in:n_in + n_sin]
        o0 = n_in + n_sin
        main_out, sout = refs[o0:o0 + n_out], refs[o0 + n_out:o0 + n_out + n_sout]
        s0 = o0 + n_out + n_sout
        main_scr, (send, recv) = refs[s0:s0 + n_scr], refs[s0 + n_scr:]
        first = functools.reduce(lambda a, b: a & b, [pl.program_id(k) == 0 for k in range(len(grid))])
        last = functools.reduce(lambda a, b: a & b, [pl.program_id(k) == grid[k] - 1 for k in range(len(grid))])

        @pl.when(first)
        def _():
            side.start(sin, sout, send, recv)

        body(*main_in, *main_out, *main_scr)

        @pl.when(last)
        def _():
            side.finish(sin, sout, send, recv)

    call = pl.pallas_call(
        wrapped, name=name, out_shape=main_shapes + tuple(side.outs), grid=grid,
        in_specs=list(in_specs) + [_ANY] * n_sin, out_specs=main_specs + tuple([_ANY] * n_sout),
        scratch_shapes=list(scratch) + [pltpu.SemaphoreType.DMA(side.sems), pltpu.SemaphoreType.DMA(side.sems)],
        compiler_params=params)
    return lambda *args: call(*args, *side.ins)


def _row_tile(m):
    best = 32
    for t in range(32, 641, 32):
        if m % t == 0:
            best = t
    return best


def _pad_rows(rows, nex, tp):
    m = rows < PAD
    for b in range(1, nex):
        m = m | ((rows >= b * tp) & (rows < b * tp + PAD))
    return m


def _ln_stats(x):
    mu = jnp.mean(x, axis=-1, keepdims=True)
    xc = x - mu
    var = jnp.mean(xc * xc, axis=-1, keepdims=True)
    rstd = lax.rsqrt(var + EPS)
    return xc * rstd, rstd


def _ln_bwd(dy, xhat, rstd, g):
    dxh = dy * g
    m1 = jnp.mean(dxh, axis=-1, keepdims=True)
    m2 = jnp.mean(dxh * xhat, axis=-1, keepdims=True)
    return rstd * (dxh - m1 - xhat * m2)


def _dot(a, b):
    return jnp.dot(a, b, preferred_element_type=F32)


def _dot_nt(a, b):
    return lax.dot_general(a, b, (((1,), (1,)), ((), ())), preferred_element_type=F32)


def _dot_tn(a, b):
    return lax.dot_general(a, b, (((0,), (0,)), ((), ())), preferred_element_type=F32)


def _sigmoid(x):
    return 1.0 / (1.0 + jnp.exp(-x))


def _bucket_np(d):
    n = np.maximum(d, 0)
    max_exact = NBUCKET // 2
    nf = np.maximum(n, 1).astype(np.float32)
    large = max_exact + (np.log(nf / np.float32(max_exact)) / np.float32(math.log(MAXDIST / max_exact))
                         * np.float32(NBUCKET - max_exact)).astype(np.int32)
    large = np.minimum(large, NBUCKET - 1)
    return np.where(n < max_exact, n, large).astype(np.int32)


def _bias_index():
    i = np.arange(BLK)[:, None]
    j = np.arange(2 * BLK)[None, :]
    d = BLK + i - j
    band_ok = (d >= 0) & (d < BLK)
    band = _bucket_np(d)
    idx = np.full((3, BLK, NKEY), -1, np.int32)
    m = np.arange(N_META)[None, :]
    d0 = (i - PAD) - m
    idx[0, :, 2 * BLK:2 * BLK + N_META] = np.where(d0 >= 0, _bucket_np(d0), -1)
    ok1 = band_ok & (j >= BLK)
    idx[1, :, :2 * BLK] = np.where(ok1, band, -1)
    idx[1, :, 2 * BLK:2 * BLK + N_META] = _bucket_np((N_META + i) - m)
    idx[2, :, :2 * BLK] = np.where(band_ok, band, -1)
    idx[2, :, 2 * BLK:2 * BLK + N_META] = NBUCKET - 1
    return idx


def _bias_build(rel_bias):
    idx = jnp.asarray(_bias_index())

    def body(idx_ref, rb_ref, o_ref):
        ix = idx_ref[...]
        for h in range(NQ):
            acc = jnp.full(ix.shape, NEG, F32)
            for b in range(NBUCKET):
                acc = jnp.where(ix == b, rb_ref[b, h], acc)
            o_ref[:, h, :, :] = acc

    return pl.pallas_call(
        body, name="bias_build", out_shape=jax.ShapeDtypeStruct((3, NQ, BLK, NKEY), F32),
        in_specs=[pl.BlockSpec(memory_space=pltpu.VMEM), pl.BlockSpec(memory_space=pltpu.SMEM)],
        out_specs=pl.BlockSpec(memory_space=pltpu.VMEM))(idx, rel_bias)


def _bias_grad(dbias):
    idx = jnp.asarray(_bias_index())

    def body(idx_ref, d_ref, o_ref):
        d = jnp.sum(d_ref[...], axis=0)
        for b in range(NBUCKET):
            acc = jnp.zeros((NQ, NKEY), F32)
            for case in range(3):
                hit = (idx_ref[case] == b)[None, :, :]
                acc = acc + jnp.sum(jnp.where(hit, d[case], 0.0), axis=1)
            o_ref[b] = jnp.sum(acc, axis=-1, keepdims=True)

    out = pl.pallas_call(
        body, name="bias_grad", out_shape=jax.ShapeDtypeStruct((NBUCKET, NQ, 1), F32),
        in_specs=[pl.BlockSpec(memory_space=pltpu.VMEM), pl.BlockSpec(memory_space=pltpu.VMEM)],
        out_specs=pl.BlockSpec(memory_space=pltpu.VMEM))(idx, dbias)
    return out.reshape(NBUCKET, NQ)


def _embed_ln(x, meta, g, b, nblk, side=None):
    nex, seq, _ = x.shape
    m = nex * nblk * BLK

    def body(x_ref, meta_ref, g_ref, b_ref, raw_ref, h_ref, hb_ref):
        j = pl.program_id(1)

        @pl.when(j == 0)
        def _():
            raw_ref[0:PAD, :] = jnp.zeros((PAD, D), F32)
            raw_ref[PAD:BLK, :] = meta_ref[...]

        @pl.when(j > 0)
        def _():
            raw_ref[...] = x_ref[...]

        xhat, _ = _ln_stats(raw_ref[...])
        y = xhat * g_ref[...] + b_ref[...]
        h_ref[...] = y
        hb_ref[...] = y.astype(BF16)

    row = lambda bb, j: (bb * nblk + j, 0)
    return _call(
        body, "embed_ln",
        (jax.ShapeDtypeStruct((m, D), F32), jax.ShapeDtypeStruct((m, D), F32), jax.ShapeDtypeStruct((m, D), BF16)),
        (nex, nblk),
        [pl.BlockSpec((None, BLK, D), lambda bb, j: (bb, jnp.maximum(j - 1, 0), 0)),
         pl.BlockSpec((N_META, D), lambda bb, j: (0, 0)),
         pl.BlockSpec((1, D), lambda bb, j: (0, 0)), pl.BlockSpec((1, D), lambda bb, j: (0, 0))],
        (pl.BlockSpec((BLK, D), row), pl.BlockSpec((BLK, D), row), pl.BlockSpec((BLK, D), row)), side=side,
    )(x, meta, g.reshape(1, D), b.reshape(1, D))


def _mm_bias(name, a, w, bias, tn, tm, side=None):
    m, k = a.shape
    n = w.shape[1]

    def body(a_ref, w_ref, b_ref, o_ref):
        o_ref[...] = _dot(a_ref[...], w_ref[...]) + b_ref[...]

    return _call(body, name, jax.ShapeDtypeStruct((m, n), F32), (n // tn, m // tm),
                 [pl.BlockSpec((tm, k), lambda j, i: (i, 0)), pl.BlockSpec((k, tn), lambda j, i: (0, j)),
                  pl.BlockSpec((1, tn), lambda j, i: (0, j))],
                 pl.BlockSpec((tm, tn), lambda j, i: (i, j)), side=side)(a, w, bias)


def _ffn_up_act(a, w4, cw, cb, tm, nex, tp, side=None):
    m, k = a.shape
    ffs = w4.shape[2]

    def body(a_ref, wu_ref, wg_ref, cu_ref, cg_ref, bu_ref, bg_ref, up_ref, ug_ref, act_ref, win):
        i = pl.program_id(1)

        @pl.when(i == 0)
        def _():
            win[:, 0:FHALO, :] = jnp.zeros((2, FHALO, ffs), F32)

        rows = i * tm + lax.broadcasted_iota(jnp.int32, (tm, 1), 0)
        pad = _pad_rows(rows, nex, tp)
        av = a_ref[...]
        for p, w_ref in ((0, wu_ref), (1, wg_ref)):
            x = jnp.where(pad, 0.0, _dot(av, w_ref[...]))
            win[p, FHALO:FHALO + tm, :] = x
            up_ref[p] = x.astype(BF16)
        for r0 in range(0, tm, RCH):
            for c0, c1 in _lane_groups(ffs):
                u = _conv3(win, 0, r0, c0, c1, cu_ref, bu_ref)
                g = _conv3(win, 1, r0, c0, c1, cg_ref, bg_ref)
                ug_ref[0, r0:r0 + RCH, c0:c1] = u.astype(BF16)
                ug_ref[1, r0:r0 + RCH, c0:c1] = g.astype(BF16)
                act_ref[r0:r0 + RCH, c0:c1] = (g * (0.5 * (1.0 + lax.erf(g * (1.0 / math.sqrt(2.0))))) * u).astype(BF16)
        win[:, 0:FHALO, :] = win[:, tm:tm + FHALO, :]

    wide = jax.ShapeDtypeStruct((2, m, 2 * ffs), BF16)
    return _call(body, "ffn_up_act", (wide, wide, jax.ShapeDtypeStruct((m, 2 * ffs), BF16)),
                 (2, m // tm),
                 [pl.BlockSpec((tm, k), lambda c, i: (i, 0)),
                  pl.BlockSpec((None, k, ffs), lambda c, i: (c, 0, 0)), pl.BlockSpec((None, k, ffs), lambda c, i: (c + 2, 0, 0)),
                  pl.BlockSpec((FTAPS, ffs), lambda c, i: (0, c)), pl.BlockSpec((FTAPS, ffs), lambda c, i: (0, c + 2)),
                  pl.BlockSpec((1, ffs), lambda c, i: (0, c)), pl.BlockSpec((1, ffs), lambda c, i: (0, c + 2))],
                 (pl.BlockSpec((2, tm, ffs), lambda c, i: (0, i, c)), pl.BlockSpec((2, tm, ffs), lambda c, i: (0, i, c)),
                  pl.BlockSpec((tm, ffs), lambda c, i: (i, c))),
                 scratch=[pltpu.VMEM((2, FHALO + tm, ffs), F32)], side=side)(a, w4, w4, cw, cw, cb, cb)


def _fill_kv(ks, vs, e, prev_ref, cur_ref, meta_ref):
    for piece, lo, n in ((prev_ref, 0, BLK), (cur_ref, BLK, BLK), (meta_ref, 2 * BLK, N_META)):
        val = piece[e]
        for hk in range(NKV):
            ks[e, hk, lo:lo + n, :] = val[:, hk * HD:(hk + 1) * HD].astype(BF16)
            vs[e, hk, lo:lo + n, :] = val[:, KVW + hk * HD:KVW + (hk + 1) * HD].astype(BF16)
    for hk in range(NKV):
        ks[e, hk, 2 * BLK + N_META:NKEY, :] = jnp.zeros((BLK - N_META, HD), BF16)
        vs[e, hk, 2 * BLK + N_META:NKEY, :] = jnp.zeros((BLK - N_META, HD), BF16)


def _softmax_group(q, ks_hk, bias_ref, sink_ref, hk):
    qg = jnp.concatenate([q[:, (hk * GRP + g) * HD:(hk * GRP + g + 1) * HD] for g in range(GRP)], axis=0).astype(BF16)
    s = _dot_nt(qg, ks_hk) * (HD ** -0.5) + bias_ref[hk * GRP:(hk + 1) * GRP].reshape(GRP * BLK, NKEY)
    sink = jnp.concatenate([jnp.full((BLK, 1), sink_ref[0, hk * GRP + g], F32) for g in range(GRP)], axis=0)
    mx = jnp.maximum(jnp.max(s, axis=-1, keepdims=True), sink)
    p = jnp.exp(s - mx)
    es = jnp.exp(sink - mx)
    inv = 1.0 / (jnp.sum(p, axis=-1, keepdims=True) + es)
    return qg, p * inv, es * inv


def _attn_specs(nex, blk_of):
    qcol, kvcol = (C_QKV) // AW, (C_QKV + AW) // (2 * KVW)
    return [
        pl.BlockSpec((nex, BLK, AW), lambda j: (0, blk_of(j), qcol)),
        pl.BlockSpec((nex, BLK, 2 * KVW), lambda j: (0, blk_of(j), kvcol)),
        pl.BlockSpec((nex, BLK, 2 * KVW), lambda j: (0, jnp.maximum(blk_of(j) - 1, 0), kvcol)),
        pl.BlockSpec((nex, N_META, 2 * KVW), lambda j: (0, PAD // N_META, kvcol)),
        pl.BlockSpec((None, NQ, BLK, NKEY), lambda j: (jnp.minimum(blk_of(j), 2), 0, 0, 0)),
        pl.BlockSpec(memory_space=pltpu.SMEM),
    ]


def _attn_fwd(z, bias, sinks, nex, nblk, side=None):
    m = z.shape[0]
    z3 = z.reshape(nex, nblk * BLK, z.shape[1])

    def body(q_ref, cur_ref, prev_ref, meta_ref, bias_ref, sink_ref, o_ref, ks, vs, oacc):
        for e in range(nex):
            _fill_kv(ks, vs, e, prev_ref, cur_ref, meta_ref)
            q = q_ref[e]
            for hk in range(NKV):
                _, pn, _ = _softmax_group(q, ks[e, hk], bias_ref, sink_ref, hk)
                o = _dot(pn.astype(BF16), vs[e, hk])
                for g in range(GRP):
                    h = hk * GRP + g
                    oacc[e, :, h * HD:(h + 1) * HD] = o[g * BLK:(g + 1) * BLK, :]
            o_ref[e] = oacc[e].astype(BF16)

    res = _call(body, "attn_fwd", jax.ShapeDtypeStruct((nex, nblk * BLK, AW), BF16), (nblk,),
                _attn_specs(nex, lambda j: j),
                pl.BlockSpec((nex, BLK, AW), lambda j: (0, j, 0)),
                scratch=[pltpu.VMEM((nex, NKV, NKEY, HD), BF16), pltpu.VMEM((nex, NKV, NKEY, HD), BF16),
                         pltpu.VMEM((nex, BLK, AW), F32)], side=side)(z3, z3, z3, z3, bias, sinks.reshape(1, NQ))
    if side is None:
        return res.reshape(m, AW)
    return (res[0].reshape(m, AW),) + tuple(res[1:])


def _cgate(cv, cg, rows, nex, tp):
    return jnp.where(_pad_rows(rows, nex, tp), 0.0, cv * _sigmoid(cg))


CLANES = 256


def _rolled_up(blk, b):
    return blk if b == 0 else pltpu.roll(blk, blk.shape[0] - b, axis=0)


def _conv_fwd(z, w, wb, g, b, nex, tp, side=None):
    m = z.shape[0]
    tm = BLK
    sub = tm // CHALO
    cvc, cgc = C_CONV // CW, C_CONV // CW + 1

    def body(cv_ref, cg_ref, cvh_ref, cgh_ref, w_ref, wb_ref, g_ref, b_ref, cc_ref, cs_ref, win):
        i = pl.program_id(0)
        rows = i * tm + lax.broadcasted_iota(jnp.int32, (tm, 1), 0)
        hrows = i * tm - CHALO + lax.broadcasted_iota(jnp.int32, (CHALO, 1), 0)
        win[0:CHALO, :] = _cgate(cvh_ref[...], cgh_ref[...], hrows, nex, tp)
        win[CHALO:CHALO + tm, :] = _cgate(cv_ref[...], cg_ref[...], rows, nex, tp)
        for sb in range(sub):
            lo = sb * CHALO
            for c0 in range(0, CW, CLANES):
                blk = win[lo:lo + 2 * CHALO, c0:c0 + CLANES]
                acc = jnp.zeros((CHALO, CLANES), F32) + wb_ref[:, c0:c0 + CLANES]
                for b in range(8):
                    rb = _rolled_up(blk, b)
                    for a in range(5):
                        s = 8 * a + b
                        if 2 <= s <= CTAPS + 1:
                            acc = acc + w_ref[s - 2:s - 1, c0:c0 + CLANES] * rb[8 * a:8 * a + CHALO]
                cc_ref[lo:lo + CHALO, c0:c0 + CLANES] = acc
        xhat, _ = _ln_stats(cc_ref[...])
        cl = xhat * g_ref[...] + b_ref[...]
        cs_ref[...] = (cl * _sigmoid(cl)).astype(BF16)

    halo = lambda i: jnp.maximum(i * sub - 1, 0)
    vec = pl.BlockSpec((1, CW), lambda i: (0, 0))
    return _call(body, "conv_fwd", (jax.ShapeDtypeStruct((m, CW), F32), jax.ShapeDtypeStruct((m, CW), BF16)),
                 (m // tm,),
                 [pl.BlockSpec((tm, CW), lambda i: (i, cvc)), pl.BlockSpec((tm, CW), lambda i: (i, cgc)),
                  pl.BlockSpec((CHALO, CW), lambda i: (halo(i), cvc)), pl.BlockSpec((CHALO, CW), lambda i: (halo(i), cgc)),
                  pl.BlockSpec((CTAPS, CW), lambda i: (0, 0)), vec, vec, vec],
                 (pl.BlockSpec((tm, CW), lambda i: (i, 0)), pl.BlockSpec((tm, CW), lambda i: (i, 0))),
                 scratch=[pltpu.VMEM((CHALO + tm, CW), F32)], side=side)(z, z, z, z, w, wb.reshape(1, CW), g.reshape(1, CW), b.reshape(1, CW))


def _mix_fwd(a, cs, wap4, wcp4, z, tm, side=None):
    m = a.shape[0]
    ns = wap4.shape[2]

    def body(a_ref, cs_ref, wa_ref, wc_ref, ga_ref, gc_ref, ya_ref, yc_ref, mix_ref):
        av, cv = a_ref[...], cs_ref[...]
        for j in range(NCHIP):
            ya_ref[:, j * ns:(j + 1) * ns] = _dot(av, wa_ref[j])
            yc_ref[:, j * ns:(j + 1) * ns] = _dot(cv, wc_ref[j])
        mix_ref[...] = (_sigmoid(ga_ref[...]) * ya_ref[...] + _sigmoid(gc_ref[...]) * yc_ref[...]).astype(BF16)

    wspec = pl.BlockSpec((NCHIP, AW, ns), lambda i: (0, 0, 0))
    row = lambda i: (i, 0)
    return _call(body, "mix_fwd",
                 (jax.ShapeDtypeStruct((m, D), F32), jax.ShapeDtypeStruct((m, D), F32), jax.ShapeDtypeStruct((m, D), BF16)),
                 (m // tm,),
                 [pl.BlockSpec((tm, AW), row), pl.BlockSpec((tm, CW), row), wspec, wspec,
                  pl.BlockSpec((tm, D), lambda i: (i, 0)), pl.BlockSpec((tm, D), lambda i: (i, 1))],
                 (pl.BlockSpec((tm, D), row), pl.BlockSpec((tm, D), row), pl.BlockSpec((tm, D), row)), side=side)(a, cs, wap4, wcp4, z, z)


def _mm_res_ln(name, a, w, res, g, b, tm, side=None):
    m, k = a.shape

    def body(a_ref, w_ref, res_ref, g_ref, b_ref, r_ref, h_ref, hb_ref):
        r = ALPHA * res_ref[...] + _dot(a_ref[...], w_ref[...])
        r_ref[...] = r
        xhat, _ = _ln_stats(r)
        y = xhat * g_ref[...] + b_ref[...]
        h_ref[...] = y
        hb_ref[...] = y.astype(BF16)

    row = lambda i: (i, 0)
    vec = pl.BlockSpec((1, D), lambda i: (0, 0))
    return _call(body, name,
                 (jax.ShapeDtypeStruct((m, D), F32), jax.ShapeDtypeStruct((m, D), F32), jax.ShapeDtypeStruct((m, D), BF16)),
                 (m // tm,),
                 [pl.BlockSpec((tm, k), row), pl.BlockSpec((k, D), lambda i: (0, 0)), pl.BlockSpec((tm, D), row), vec, vec],
                 (pl.BlockSpec((tm, D), row), pl.BlockSpec((tm, D), row), pl.BlockSpec((tm, D), row)), side=side)(a, w, res, g.reshape(1, D), b.reshape(1, D))


RCH = 16


def _lane_groups(width, most=768):
    n = -(-width // most)
    step = -(-width // (128 * n)) * 128
    return [(c, min(c + step, width)) for c in range(0, width, step)]


def _conv3(win, p, r0, c0, c1, w_ref, b_ref):
    blk = win[p, r0:r0 + FHALO + RCH, c0:c1]
    x0, x1, x2 = blk[FHALO:], pltpu.roll(blk, 1, axis=0)[FHALO:], pltpu.roll(blk, 2, axis=0)[FHALO:]
    return b_ref[:, c0:c1] + w_ref[0:1, c0:c1] * x2 + w_ref[1:2, c0:c1] * x1 + w_ref[2:3, c0:c1] * x0


def _loss_grad(y, target, nblk):
    nex = target.shape[0]
    m = y.shape[0]

    def body(y_ref, t_ref, dy_ref, acc_ref):
        bb, j = pl.program_id(0), pl.program_id(1)

        @pl.when((bb == 0) & (j == 0))
        def _():
            acc_ref[...] = jnp.zeros_like(acc_ref)

        @pl.when(j == 0)
        def _():
            dy_ref[...] = jnp.zeros_like(dy_ref)

        @pl.when(j > 0)
        def _():
            e = y_ref[...] - t_ref[...]
            dy_ref[...] = e * (1.0 / D)
            acc_ref[...] += jnp.sum((e * e).reshape(BLK // 8, 8, D), axis=0)

    return _call(body, "loss_grad", (jax.ShapeDtypeStruct((m, D), F32), jax.ShapeDtypeStruct((8, D), F32)), (nex, nblk),
                 [pl.BlockSpec((BLK, D), lambda bb, j: (bb * nblk + j, 0)),
                  pl.BlockSpec((None, BLK, D), lambda bb, j: (bb, jnp.maximum(j - 1, 0), 0))],
                 (pl.BlockSpec((BLK, D), lambda bb, j: (bb * nblk + j, 0)), pl.BlockSpec((8, D), lambda bb, j: (0, 0))))(y, target)


def _ln_bwd_call(name, dy, r, g, tm, a_list=(), w=None, cols=None, side=None):
    m = dy.shape[0]
    na = len(a_list)

    def body(*refs):
        dy_ref, r_ref, g_ref = refs[0:3]
        a_refs = refs[3:3 + na]
        w_ref = refs[3 + na] if na else None
        dr_ref, drb_ref, dg_ref, db_ref = refs[-4:]
        i = pl.program_id(0)
        dh = dy_ref[...]
        if na:
            dh = ALPHA * dh
            if cols is None:
                ns = w.shape[2]
                for j in range(NCHIP):
                    dh = dh + _dot_nt(a_refs[0][j // 2, :, (j % 2) * ns:(j % 2 + 1) * ns], w_ref[j])
            else:
                for a_ref, (c0, c1) in zip(a_refs, cols):
                    dh = dh + _dot_nt(a_ref[...], w_ref[:, c0:c1])
        xhat, rstd = _ln_stats(r_ref[...])
        dr = _ln_bwd(dh, xhat, rstd, g_ref[...])
        dr_ref[...] = dr
        drb_ref[...] = dr.astype(BF16)

        @pl.when(i == 0)
        def _():
            dg_ref[...] = jnp.zeros_like(dg_ref)
            db_ref[...] = jnp.zeros_like(db_ref)

        dg_ref[...] += jnp.sum(dh * xhat, axis=0, keepdims=True)
        db_ref[...] += jnp.sum(dh, axis=0, keepdims=True)

    row = lambda i: (i, 0)
    vec = pl.BlockSpec((1, D), lambda i: (0, 0))
    in_specs = [pl.BlockSpec((tm, D), row), pl.BlockSpec((tm, D), row), vec]
    for a in a_list:
        in_specs.append(pl.BlockSpec((2, tm, a.shape[2]), lambda i: (0, i, 0)) if a.ndim == 3 else pl.BlockSpec((tm, a.shape[1]), row))
    if na:
        in_specs.append(pl.BlockSpec(w.shape, (lambda i: (0, 0, 0)) if w.ndim == 3 else (lambda i: (0, 0))))
    return _call(body, name,
                 (jax.ShapeDtypeStruct((m, D), F32), jax.ShapeDtypeStruct((m, D), BF16),
                  jax.ShapeDtypeStruct((1, D), F32), jax.ShapeDtypeStruct((1, D), F32)),
                 (m // tm,), in_specs,
                 (pl.BlockSpec((tm, D), row), pl.BlockSpec((tm, D), row), vec, vec),
                 side=side)(dy, r, g.reshape(1, D), *a_list, *([w] if na else []))


def _ffn_act_bwd(drb, wdown, ug, tm, side=None):
    _, m, dff = ug.shape
    ffs = dff // 2

    def body(dr_ref, wd_ref, ug_ref, o_ref, dact):
        dact[...] = _dot_nt(dr_ref[...], wd_ref[...])
        for r0 in range(0, tm, RCH):
            for c0, c1 in _lane_groups(ffs):
                u = ug_ref[0, r0:r0 + RCH, c0:c1].astype(F32)
                g = ug_ref[1, r0:r0 + RCH, c0:c1].astype(F32)
                da = dact[r0:r0 + RCH, c0:c1]
                cdf = 0.5 * (1.0 + lax.erf(g * (1.0 / math.sqrt(2.0))))
                pdf = jnp.exp(-0.5 * g * g) * (1.0 / math.sqrt(2.0 * math.pi))
                o_ref[0, r0:r0 + RCH, c0:c1] = da * (g * cdf)
                o_ref[1, r0:r0 + RCH, c0:c1] = da * u * (cdf + g * pdf)

    return _call(body, "ffn_act_bwd", jax.ShapeDtypeStruct((2, m, dff), F32), (2, m // tm),
                 [pl.BlockSpec((tm, D), lambda c, i: (i, 0)), pl.BlockSpec((ffs, D), lambda c, i: (c, 0)),
                  pl.BlockSpec((2, tm, ffs), lambda c, i: (0, i, c))],
                 pl.BlockSpec((2, tm, ffs), lambda c, i: (0, i, c)),
                 scratch=[pltpu.VMEM((tm, ffs), F32)], side=side)(drb, wdown, ug)


def _ffn_conv_bwd(dup3, up3, w, tm, nex, tp, side=None):
    _, m, dff = up3.shape
    ffs = dff // 2
    sub = tm // FHALO
    nt = m // tm

    def body(d_ref, dh_ref, x_ref, wu_ref, wg_ref, o_ref, dw_ref, db_ref, dwin, dwacc, dbacc):
        i = pl.program_id(1)
        dwin[:, 0:tm, :] = d_ref[...]
        dwin[:, tm:tm + FHALO, :] = jnp.where(i == nt - 1, 0.0, dh_ref[...])
        dwacc[...] = jnp.zeros_like(dwacc)
        dbacc[...] = jnp.zeros_like(dbacc)

        @pl.when(i == 0)
        def _():
            dw_ref[...] = jnp.zeros_like(dw_ref)
            db_ref[...] = jnp.zeros_like(db_ref)

        fold = lambda t: t[0:8, :] + t[8:16, :]
        for r0 in range(0, tm, RCH):
            rows = i * tm + r0 + lax.broadcasted_iota(jnp.int32, (RCH, 1), 0)
            pad = _pad_rows(rows, nex, tp)
            for p, w_ref in ((0, wu_ref), (1, wg_ref)):
                for c0, c1 in _lane_groups(ffs):
                    dblk = dwin[p, r0:r0 + RCH + FHALO, c0:c1]
                    d0 = dblk[:RCH]
                    d1 = pltpu.roll(dblk, RCH + FHALO - 1, axis=0)[:RCH]
                    d2 = pltpu.roll(dblk, RCH + FHALO - 2, axis=0)[:RCH]
                    dpre = w_ref[2:3, c0:c1] * d0 + w_ref[1:2, c0:c1] * d1 + w_ref[0:1, c0:c1] * d2
                    o_ref[p, r0:r0 + RCH, c0:c1] = jnp.where(pad, 0.0, dpre).astype(BF16)
                    x0 = x_ref[p, r0:r0 + RCH, c0:c1].astype(F32)
                    dwacc[p, 2, :, c0:c1] += fold(d0 * x0)
                    dwacc[p, 1, :, c0:c1] += fold(d1 * x0)
                    dwacc[p, 0, :, c0:c1] += fold(d2 * x0)
                    dbacc[p, :, c0:c1] += fold(d0)
        for p in range(2):
            for k in range(FTAPS):
                dw_ref[p, k:k + 1, :] += jnp.sum(dwacc[p, k], axis=0, keepdims=True)
            db_ref[p] += jnp.sum(dbacc[p], axis=0, keepdims=True)

    nxt = lambda i: jnp.minimum((i + 1) * sub, m // FHALO - 1)
    return _call(body, "ffn_conv_bwd",
                 (jax.ShapeDtypeStruct((2, m, dff), BF16), jax.ShapeDtypeStruct((2, FTAPS, dff), F32),
                  jax.ShapeDtypeStruct((2, 1, dff), F32)),
                 (2, nt),
                 [pl.BlockSpec((2, tm, ffs), lambda c, i: (0, i, c)), pl.BlockSpec((2, FHALO, ffs), lambda c, i: (0, nxt(i), c)),
                  pl.BlockSpec((2, tm, ffs), lambda c, i: (0, i, c)),
                  pl.BlockSpec((FTAPS, ffs), lambda c, i: (0, c)), pl.BlockSpec((FTAPS, ffs), lambda c, i: (0, c + 2))],
                 (pl.BlockSpec((2, tm, ffs), lambda c, i: (0, i, c)), pl.BlockSpec((2, FTAPS, ffs), lambda c, i: (0, 0, c)),
                  pl.BlockSpec((2, 1, ffs), lambda c, i: (0, 0, c))),
                 scratch=[pltpu.VMEM((2, tm + FHALO, ffs), F32),
                          pltpu.VMEM((2, FTAPS, 8, ffs), F32), pltpu.VMEM((2, 8, ffs), F32)], side=side)(dup3, dup3, up3, w, w)


def _mm_tn(name, a, b, tk, tn, b_cols=None, chip_out=False):
    m, k = a.shape
    n = b.shape[-1] * (2 if b.ndim == 3 else 1)

    def body(a_ref, b_ref, o_ref):
        o_ref[...] = _dot_tn(a_ref[...], b_ref[...])

    if b.ndim == 3:
        bspec = pl.BlockSpec((None, m, tn), lambda kk, j: (b_cols(j)[0], 0, b_cols(j)[1]))
    else:
        bspec = pl.BlockSpec((m, tn), lambda kk, j: (0, j))
    if chip_out:
        oshape, ospec = (n // tn, k, tn), pl.BlockSpec((None, tk, tn), lambda kk, j: (j, kk, 0))
    else:
        oshape, ospec = (k, n), pl.BlockSpec((tk, tn), lambda kk, j: (kk, j))
    return _call(body, name, jax.ShapeDtypeStruct(oshape, F32), (k // tk, n // tn),
                 [pl.BlockSpec((m, tk), lambda kk, j: (0, kk)), bspec], ospec)(a, b)


def _gate_bwd(drb, wout, ya, yc, z, tm):
    m = drb.shape[0]

    def body(dr_ref, w_ref, ya_ref, yc_ref, ga_ref, gc_ref, dya_ref, dyc_ref, dz_ref, cs_ref):
        i = pl.program_id(0)
        dmix = _dot_nt(dr_ref[...], w_ref[...])
        sa, sc = _sigmoid(ga_ref[...]), _sigmoid(gc_ref[...])
        dya_ref[...] = (dmix * sa).astype(BF16)
        dyc_ref[...] = (dmix * sc).astype(BF16)
        dga = dmix * ya_ref[...] * sa * (1.0 - sa)
        dgc = dmix * yc_ref[...] * sc * (1.0 - sc)
        dz_ref[:, 0:D] = dga.astype(BF16)
        dz_ref[:, D:2 * D] = dgc.astype(BF16)

        @pl.when(i == 0)
        def _():
            cs_ref[...] = jnp.zeros_like(cs_ref)

        cs_ref[:, 0:D] += jnp.sum(dga, axis=0, keepdims=True)
        cs_ref[:, D:2 * D] += jnp.sum(dgc, axis=0, keepdims=True)

    row = lambda i: (i, 0)
    return _call(body, "gate_bwd",
                 (jax.ShapeDtypeStruct((m, D), BF16), jax.ShapeDtypeStruct((m, D), BF16),
                  jax.ShapeDtypeStruct((m, 2 * D), BF16), jax.ShapeDtypeStruct((1, 2 * D), F32)),
                 (m // tm,),
                 [pl.BlockSpec((tm, D), row), pl.BlockSpec((D, D), lambda i: (0, 0)), pl.BlockSpec((tm, D), row),
                  pl.BlockSpec((tm, D), row), pl.BlockSpec((tm, D), lambda i: (i, 0)), pl.BlockSpec((tm, D), lambda i: (i, 1))],
                 (pl.BlockSpec((tm, D), row), pl.BlockSpec((tm, D), row), pl.BlockSpec((tm, 2 * D), row),
                  pl.BlockSpec((1, 2 * D), lambda i: (0, 0))))(drb, wout, ya, yc, z, z)


def _conv_bwd_a(dyc, wcp4, cc, g, b, tm):
    m = cc.shape[0]
    ns = wcp4.shape[2]

    def body(dy_ref, w_ref, cc_ref, g_ref, b_ref, dcc_ref, dg_ref, db_ref, dwb_ref):
        i = pl.program_id(0)
        dcs = jnp.zeros((tm, CW), F32)
        for j in range(NCHIP):
            dcs = dcs + _dot_nt(dy_ref[:, j * ns:(j + 1) * ns], w_ref[j])
        xhat, rstd = _ln_stats(cc_ref[...])
        cl = xhat * g_ref[...] + b_ref[...]
        sg = _sigmoid(cl)
        dcl = dcs * sg * (1.0 + cl * (1.0 - sg))
        dcc = _ln_bwd(dcl, xhat, rstd, g_ref[...])
        dcc_ref[...] = dcc

        @pl.when(i == 0)
        def _():
            dg_ref[...] = jnp.zeros_like(dg_ref)
            db_ref[...] = jnp.zeros_like(db_ref)
            dwb_ref[...] = jnp.zeros_like(dwb_ref)

        dg_ref[...] += jnp.sum(dcl * xhat, axis=0, keepdims=True)
        db_ref[...] += jnp.sum(dcl, axis=0, keepdims=True)
        dwb_ref[...] += jnp.sum(dcc, axis=0, keepdims=True)

    row = lambda i: (i, 0)
    vec = pl.BlockSpec((1, CW), lambda i: (0, 0))
    v = jax.ShapeDtypeStruct((1, CW), F32)
    return _call(body, "conv_bwd_a", (jax.ShapeDtypeStruct((m, CW), F32), v, v, v), (m // tm,),
                 [pl.BlockSpec((tm, D), row), pl.BlockSpec((NCHIP, CW, ns), lambda i: (0, 0, 0)), pl.BlockSpec((tm, CW), row), vec, vec],
                 (pl.BlockSpec((tm, CW), row), vec, vec, vec))(dyc, wcp4, cc, g.reshape(1, CW), b.reshape(1, CW))


def _conv_bwd_b(dcc, z, w, nex, tp, side=None):
    m = dcc.shape[0]
    tm = BLK
    sub = tm // CHALO
    nt = m // tm
    cvc, cgc = C_CONV // CW, C_CONV // CW + 1

    def body(d_ref, dh_ref, cv_ref, cg_ref, w_ref, dz_ref, dw_ref, cs_ref, dwin, dwacc):
        i = pl.program_id(0)
        rows = i * tm + lax.broadcasted_iota(jnp.int32, (tm, 1), 0)
        dwin[0:tm, :] = d_ref[...]
        dwin[tm:tm + CHALO, :] = jnp.where(i == nt - 1, 0.0, dh_ref[...])

        @pl.when(i == 0)
        def _():
            dwacc[...] = jnp.zeros_like(dwacc)
            cs_ref[...] = jnp.zeros_like(cs_ref)

        fold = lambda t: (t[0:8] + t[8:16]) + (t[16:24] + t[24:32])
        for sb in range(sub):
            lo = sb * CHALO
            pad = _pad_rows(rows[lo:lo + CHALO], nex, tp)
            for c0 in range(0, CW, CLANES):
                cs_ = slice(c0, c0 + CLANES)
                cv = cv_ref[lo:lo + CHALO, cs_]
                sg = _sigmoid(cg_ref[lo:lo + CHALO, cs_])
                cgin = jnp.where(pad, 0.0, cv * sg)
                blk = dwin[lo:lo + 2 * CHALO, cs_]
                acc = jnp.zeros((CHALO, CLANES), F32)
                for b in range(8):
                    rb = _rolled_up(blk, b)
                    for a in range(4):
                        s = 8 * a + b
                        if s <= CTAPS - 1:
                            k = CTAPS - 1 - s
                            sh = rb[8 * a:8 * a + CHALO]
                            acc = acc + w_ref[k:k + 1, cs_] * sh
                            dwacc[k, :, cs_] += fold(sh * cgin)
                dcg = jnp.where(pad, 0.0, acc)
                dcv = dcg * sg
                dgt = dcg * cv * sg * (1.0 - sg)
                dz_ref[lo:lo + CHALO, cs_] = dcv.astype(BF16)
                dz_ref[lo:lo + CHALO, CW + c0:CW + c0 + CLANES] = dgt.astype(BF16)
                cs_ref[:, cs_] += jnp.sum(dcv, axis=0, keepdims=True)
                cs_ref[:, CW + c0:CW + c0 + CLANES] += jnp.sum(dgt, axis=0, keepdims=True)

        @pl.when(i == nt - 1)
        def _():
            for k in range(CHALO):
                dw_ref[k:k + 1, :] = jnp.sum(dwacc[k], axis=0, keepdims=True)

    nxt = lambda i: jnp.minimum((i + 1) * sub, m // CHALO - 1)
    return _call(body, "conv_bwd_b",
                 (jax.ShapeDtypeStruct((m, 2 * CW), BF16), jax.ShapeDtypeStruct((CHALO, CW), F32),
                  jax.ShapeDtypeStruct((1, 2 * CW), F32)),
                 (nt,),
                 [pl.BlockSpec((tm, CW), lambda i: (i, 0)), pl.BlockSpec((CHALO, CW), lambda i: (nxt(i), 0)),
                  pl.BlockSpec((tm, CW), lambda i: (i, cvc)), pl.BlockSpec((tm, CW), lambda i: (i, cgc)),
                  pl.BlockSpec((CTAPS, CW), lambda i: (0, 0))],
                 (pl.BlockSpec((tm, 2 * CW), lambda i: (i, 0)), pl.BlockSpec((CHALO, CW), lambda i: (0, 0)),
                  pl.BlockSpec((1, 2 * CW), lambda i: (0, 0))),
                 scratch=[pltpu.VMEM((tm + CHALO, CW), F32), pltpu.VMEM((CHALO, 8, CW), F32)], side=side)(dcc, dcc, z, z, w)


def _attn_bwd(z, bias, sinks, dya, wap4, nex, nblk, side=None):
    m = z.shape[0]
    tp = nblk * BLK
    ns = wap4.shape[2]
    blk_of = lambda j: nblk - 1 - j
    z3 = z.reshape(nex, tp, z.shape[1])
    dya3 = dya.reshape(nex, tp, D)

    def body(q_ref, cur_ref, prev_ref, meta_ref, bias_ref, sink_ref, dy_ref, w_ref,
             dz_ref, cs_ref, dsk_ref, dbias_ref, ks, vs, carry, macc, dqacc, dkv, okv):
        j = pl.program_id(0)
        n = nblk - 1 - j

        @pl.when(j == 0)
        def _():
            carry[...] = jnp.zeros_like(carry)
            macc[...] = jnp.zeros_like(macc)
            cs_ref[...] = jnp.zeros_like(cs_ref)
            dsk_ref[...] = jnp.zeros_like(dsk_ref)

        @pl.when((j == 0) | (n <= 1))
        def _():
            dbias_ref[...] = jnp.zeros_like(dbias_ref)

        lane = lax.broadcasted_iota(jnp.int32, (1, BLK), 1)
        dsk = jnp.zeros((1, BLK), F32)
        for e in range(nex):
            _fill_kv(ks, vs, e, prev_ref, cur_ref, meta_ref)
            q = q_ref[e]
            da = jnp.zeros((BLK, AW), F32)
            for jj in range(NCHIP):
                da = da + _dot_nt(dy_ref[e, :, jj * ns:(jj + 1) * ns], w_ref[jj])
            for hk in range(NKV):
                qg, pn, psink = _softmax_group(q, ks[e, hk], bias_ref, sink_ref, hk)
                dog = jnp.concatenate([da[:, (hk * GRP + g) * HD:(hk * GRP + g + 1) * HD] for g in range(GRP)], axis=0).astype(BF16)
                dp = _dot_nt(dog, vs[e, hk])
                dl = jnp.sum(pn * dp, axis=-1, keepdims=True)
                ds = pn * (dp - dl)
                dbias_ref[hk * GRP:(hk + 1) * GRP] += ds.reshape(GRP, BLK, NKEY)
                dsr = -psink * dl
                for g in range(GRP):
                    dsk = dsk + jnp.where(lane == hk * GRP + g, jnp.sum(dsr[g * BLK:(g + 1) * BLK]), 0.0)
                dsb = (ds * (HD ** -0.5)).astype(BF16)
                dqg = _dot(dsb, ks[e, hk])
                for g in range(GRP):
                    h = hk * GRP + g
                    dqacc[e, :, h * HD:(h + 1) * HD] = dqg[g * BLK:(g + 1) * BLK, :]
                dkv[e, :, hk * HD:(hk + 1) * HD] = _dot_tn(dsb, qg)
                dkv[e, :, KVW + hk * HD:KVW + (hk + 1) * HD] = _dot_tn(pn.astype(BF16), dog)
            macc[e] += dkv[e, 2 * BLK:2 * BLK + N_META, :]
            okv[e] = dkv[e, BLK:2 * BLK, :] + carry[e]
            carry[e] = dkv[e, 0:BLK, :]

            @pl.when(n == 0)
            def _():
                okv[e, PAD:BLK, :] += macc[e]

            dq = dqacc[e]
            ok = okv[e]
            dz_ref[e, :, 0:AW] = dq.astype(BF16)
            dz_ref[e, :, AW:AW + 2 * KVW] = ok.astype(BF16)
            cs_ref[:, 0:AW] += jnp.sum(dq, axis=0, keepdims=True)
            cs_ref[:, AW:AW + 2 * KVW] += jnp.sum(ok, axis=0, keepdims=True)
        dsk_ref[...] += dsk

    wz = AW + 2 * KVW
    specs = _attn_specs(nex, blk_of) + [
        pl.BlockSpec((nex, BLK, D), lambda j: (0, blk_of(j), 0)),
        pl.BlockSpec((NCHIP, AW, ns), lambda j: (0, 0, 0))]
    res = _call(body, "attn_bwd",
                (jax.ShapeDtypeStruct((nex, tp, wz), BF16), jax.ShapeDtypeStruct((1, wz), F32), jax.ShapeDtypeStruct((1, BLK), F32),
                 jax.ShapeDtypeStruct((1, 3, NQ, BLK, NKEY), F32)),
                (nblk,), specs,
                (pl.BlockSpec((nex, BLK, wz), lambda j: (0, blk_of(j), 0)), pl.BlockSpec((1, wz), lambda j: (0, 0)),
                 pl.BlockSpec((1, BLK), lambda j: (0, 0)),
                 pl.BlockSpec((None, None, NQ, BLK, NKEY), lambda j: (0, jnp.minimum(blk_of(j), 2), 0, 0, 0))),
                scratch=[pltpu.VMEM((nex, NKV, NKEY, HD), BF16), pltpu.VMEM((nex, NKV, NKEY, HD), BF16),
                         pltpu.VMEM((nex, BLK, 2 * KVW), F32), pltpu.VMEM((nex, N_META, 2 * KVW), F32),
                         pltpu.VMEM((nex, BLK, AW), F32), pltpu.VMEM((nex, NKEY, 2 * KVW), F32),
                         pltpu.VMEM((nex, BLK, 2 * KVW), F32)],
                side=side)(z3, z3, z3, z3, bias, sinks.reshape(1, NQ), dya3, wap4)
    return (res[0].reshape(m, wz),) + tuple(res[1:])


def _tile_rows(rows, cols, target_bytes=1 << 20):
    best = None
    for t in range(8, rows + 1, 8):
        if rows % t == 0 and t * cols * 4 <= target_bytes:
            best = t
    return best or rows


def _sum0(name, x):
    n, r, c = x.shape
    tr = _tile_rows(r, c * n)

    def body(x_ref, o_ref):
        acc = x_ref[0]
        for k in range(1, n):
            acc = acc + x_ref[k]
        o_ref[...] = acc

    return _call(body, name, jax.ShapeDtypeStruct((r, c), F32), (r // tr,),
                 [pl.BlockSpec((n, tr, c), lambda i: (0, i, 0))], pl.BlockSpec((tr, c), lambda i: (i, 0)))(x)


def _adamw(name, w, g, mom, vel):
    r, c = w.shape
    tr = _tile_rows(r, c)
    c1 = 1.0 / (1.0 - ADAM_B1 ** ADAM_STEP)
    c2 = 1.0 / (1.0 - ADAM_B2 ** ADAM_STEP)

    def body(w_ref, g_ref, m_ref, v_ref, d_ref, mo_ref, vo_ref):
        gg = g_ref[...]
        mn = ADAM_B1 * m_ref[...] + (1.0 - ADAM_B1) * gg
        vn = ADAM_B2 * v_ref[...] + (1.0 - ADAM_B2) * (gg * gg)
        mo_ref[...] = mn
        vo_ref[...] = vn
        d_ref[...] = -ADAM_LR * ((mn * c1) / (jnp.sqrt(vn * c2) + ADAM_EPS) + ADAM_WD * w_ref[...])

    spec = pl.BlockSpec((tr, c), lambda i: (i, 0))
    o = jax.ShapeDtypeStruct((r, c), F32)
    return _call(body, name, (o, o, o), (r // tr,), [spec] * 4, (spec, spec, spec))(w, g, mom, vel)


def _place():
    x, y, c = lax.axis_index("x"), lax.axis_index("y"), lax.axis_index("c")
    others = [(1 - x, y), (x, 1 - y), (1 - x, 1 - y)]
    return x, y, c, others


def _gather_job(items):
    nw = len(items)

    def views(s_ref, g_ref, layer, c):
        if layer is None:
            return s_ref.at[c], lambda chip, cc: g_ref.at[chip, cc]
        hr = s_ref.shape[1] // 2
        return s_ref.at[layer, pl.ds(c * hr, hr)], lambda chip, cc: g_ref.at[chip, pl.ds(cc * hr, hr)]

    def copies(s_refs, g_refs, send, recv):
        x, y, c, others = _place()
        chip = 2 * x + y
        firsts, arrive, passed, arrive2 = [], [], [], []
        for w, (_, layer) in enumerate(items):
            src, dst = views(s_refs[w], g_refs[w], layer, c)
            for k, (px, py) in enumerate(others):
                def rc(kk, s, d, to, w=w):
                    return pltpu.make_async_remote_copy(src_ref=s, dst_ref=d, send_sem=send.at[w, kk], recv_sem=recv.at[w, kk],
                                                        device_id=to, device_id_type=MESH)
                got, got2 = dst(2 * px + py, c), dst(2 * px + py, 1 - c)
                firsts.append(rc(k, src, dst(chip, c), (px, py, c)))
                arrive.append(rc(k, got, got, (x, y, c)))
                passed.append(rc(3 + k, got, got, (x, y, 1 - c)))
                arrive2.append(rc(3 + k, got2, got2, (x, y, c)))
        return firsts, arrive, passed, arrive2

    def start(s_refs, g_refs, send, recv):
        for cp in copies(s_refs, g_refs, send, recv)[0]:
            cp.start()

    def finish(s_refs, g_refs, send, recv):
        firsts, arrive, passed, arrive2 = copies(s_refs, g_refs, send, recv)
        for a, p in zip(arrive, passed):
            a.wait_recv()
            p.start()
        for a in arrive2:
            a.wait_recv()
        for cp in firsts + passed:
            cp.wait_send()

    outs = [jax.ShapeDtypeStruct((NCHIP,) + (s.shape if layer is None else s.shape[1:]), s.dtype) for s, layer in items]
    return _Job([s for s, _ in items], outs, (nw, 6), start, finish)


def _swap_job(grads):
    def copies(d_refs, a_refs, send, recv):
        x, y, c, _ = _place()
        cps = []
        for w in range(len(grads)):
            h = d_refs[w].shape[1] // 2
            cps.append(pltpu.make_async_remote_copy(
                src_ref=d_refs[w].at[:, pl.ds((1 - c) * h, h), :], dst_ref=a_refs[w], send_sem=send.at[w], recv_sem=recv.at[w],
                device_id=(x, y, 1 - c), device_id_type=MESH))
        return cps

    def start(*r):
        for cp in copies(*r):
            cp.start()

    def finish(*r):
        for cp in copies(*r):
            cp.wait()

    outs = [jax.ShapeDtypeStruct((NCHIP, g.shape[1] // 2, g.shape[2]), g.dtype) for g in grads]
    return _Job(list(grads), outs, (len(grads),), start, finish)


def _exchange_job(parts):
    def copies(q_refs, b_refs, send, recv):
        x, y, c, others = _place()
        cps = []
        for w in range(len(parts)):
            for k, (px, py) in enumerate(others):
                cps.append(pltpu.make_async_remote_copy(
                    src_ref=q_refs[w].at[2 * px + py], dst_ref=b_refs[w].at[k], send_sem=send.at[w, k], recv_sem=recv.at[w, k],
                    device_id=(px, py, c), device_id_type=MESH))
        return cps

    def start(*r):
        for cp in copies(*r):
            cp.start()

    def finish(*r):
        for cp in copies(*r):
            cp.wait()

    outs = [jax.ShapeDtypeStruct((3,) + p.shape[1:], p.dtype) for p in parts]
    return _Job(list(parts), outs, (len(parts), 3), start, finish)


def _run_job(name, job):
    n_in, n_out = len(job.ins), len(job.outs)

    def body(*refs):
        ins, outs = refs[:n_in], refs[n_in:n_in + n_out]
        send, recv = refs[n_in + n_out:]
        job.start(ins, outs, send, recv)
        job.finish(ins, outs, send, recv)

    return pl.pallas_call(
        body, name=name, out_shape=tuple(job.outs), in_specs=[_ANY] * n_in, out_specs=tuple([_ANY] * n_out),
        scratch_shapes=[pltpu.SemaphoreType.DMA(job.sems), pltpu.SemaphoreType.DMA(job.sems)])(*job.ins)


def _sibling_join(halves):
    nw = len(halves)

    def body(*refs):
        h_refs = refs[:2 * nw]
        f_refs = refs[2 * nw:3 * nw]
        send, recv = refs[3 * nw:]
        x, y, c, _ = _place()
        cps = []
        for w in range(nw):
            for l in range(2):
                src = h_refs[2 * w + l]
                h = src.shape[0]
                dst = f_refs[w].at[l, pl.ds(c * h, h), :]
                cp = pltpu.make_async_remote_copy(src_ref=src, dst_ref=dst, send_sem=send.at[w, l], recv_sem=recv.at[w, l],
                                                  device_id=(x, y, 1 - c), device_id_type=MESH)
                cp.start()
                cps.append(cp)
        for w in range(nw):
            for l in range(2):
                src = h_refs[2 * w + l]
                h = src.shape[0]
                other = f_refs[w].at[l, pl.ds((1 - c) * h, h), :]
                pltpu.make_async_remote_copy(src_ref=src, dst_ref=other, send_sem=send.at[w, l], recv_sem=recv.at[w, l],
                                             device_id=(x, y, c), device_id_type=MESH).wait_recv()
        for cp in cps:
            cp.wait_send()

    flat = [a for pair in halves for a in pair]
    outs = tuple(jax.ShapeDtypeStruct((2, 2 * pair[0].shape[0], pair[0].shape[1]), F32) for pair in halves)
    return pl.pallas_call(
        body, name="grad_sibling_join", out_shape=outs, in_specs=[_ANY] * (2 * nw), out_specs=tuple([_ANY] * nw),
        scratch_shapes=[pltpu.SemaphoreType.DMA((nw, 2)), pltpu.SemaphoreType.DMA((nw, 2))])(*flat)


def _allgather_small(v):
    r = v.shape[0]

    def body(x_ref, out_ref, send_sems, recv_sems, local_sem):
        x, y, c, chips = _place()
        me, sibling = (x, y, c), (x, y, 1 - c)

        def slab(px, py, pc):
            return out_ref.at[4 * px + 2 * py + pc]

        def copy(k, block, to, src=None):
            return pltpu.make_async_remote_copy(src_ref=slab(*block) if src is None else src, dst_ref=slab(*block),
                                                send_sem=send_sems.at[k], recv_sem=recv_sems.at[k],
                                                device_id=to, device_id_type=MESH)

        mine = pltpu.make_async_copy(x_ref, slab(*me), local_sem)
        mine.start()
        first = [copy(0, me, sibling, src=x_ref)]
        first += [copy(1 + j, me, (*chip, c), src=x_ref) for j, chip in enumerate(chips)]
        for cp in first:
            cp.start()
        passed = [copy(4 + j, (*chip, c), sibling) for j, chip in enumerate(chips)]
        for j, chip in enumerate(chips):
            copy(1 + j, (*chip, c), me).wait_recv()
            passed[j].start()
        copy(0, sibling, me).wait_recv()
        for j, chip in enumerate(chips):
            copy(4 + j, (*chip, 1 - c), me).wait_recv()
        for cp in first + passed:
            cp.wait_send()
        mine.wait()

    return pl.pallas_call(
        body, name="allgather_small", out_shape=jax.ShapeDtypeStruct((8, r, 128), F32),
        in_specs=[pl.BlockSpec(memory_space=pltpu.VMEM)], out_specs=pl.BlockSpec(memory_space=pltpu.VMEM),
        scratch_shapes=[pltpu.SemaphoreType.DMA((7,)), pltpu.SemaphoreType.DMA((7,)), pltpu.SemaphoreType.DMA],
    )(v)


def _add_half(name, d, a, c):
    _, h, cols = a.shape
    nt = 1
    th = h // nt

    def body(c_ref, d_ref, a_ref, o_ref):
        o_ref[...] = (d_ref[...] + a_ref[...]).astype(BF16)

    return _call(body, name, jax.ShapeDtypeStruct(a.shape, BF16), (NCHIP, nt),
                 [pl.BlockSpec((None, th, cols), lambda p, i, cr: (p, cr[0] * nt + i, 0)),
                  pl.BlockSpec((None, th, cols), lambda p, i, cr: (p, i, 0))],
                 pl.BlockSpec((None, th, cols), lambda p, i, cr: (p, i, 0)), prefetch=1)(c, d, a)


def _add_chips(name, d, a, b, where):
    _, h, cols = a.shape
    th = h // 4 if (h % 64 == 0) else h
    nt = h // th

    def body(w_ref, d_ref, a_ref, b_ref, o_ref):
        own = d_ref[...] + a_ref[...]
        o_ref[...] = ((own + b_ref[0].astype(F32)) + b_ref[1].astype(F32)) + b_ref[2].astype(F32)

    return _call(body, name, jax.ShapeDtypeStruct((h, cols), F32), (nt,),
                 [pl.BlockSpec((None, th, cols), lambda i, wr: (wr[0], wr[1] * nt + i, 0)),
                  pl.BlockSpec((None, th, cols), lambda i, wr: (wr[0], i, 0)),
                  pl.BlockSpec((3, th, cols), lambda i, wr: (0, i, 0))],
                 pl.BlockSpec((th, cols), lambda i, wr: (i, 0)), prefetch=1)(where, d, a, b)


def _pack(arrs):
    pieces = []
    for a in arrs:
        f = a.reshape(-1)
        n = -(-f.shape[0] // 1024) * 1024
        pieces.append(jnp.pad(f, (0, n - f.shape[0])).reshape(-1, 128))
    return jnp.concatenate(pieces, axis=0)


def _unpack(buf, shapes):
    out, r = [], 0
    for s in shapes:
        n = int(np.prod(s))
        rows = -(-n // 1024) * 8
        out.append(buf[r:r + rows].reshape(-1)[:n].reshape(s))
        r += rows
    return out


def kernel(x, meta_tokens, in_ln_g, in_ln_b, rel_bias, w_in, b_in, attn_sinks, w_attn_proj, conv_dw, conv_dw_b, conv_ln_g, conv_ln_b, w_conv_proj, w_out, ln1_g, ln1_b, ffn_w_up, ffn_dw, ffn_dw_b, ffn_w_down, ln2_g, ln2_b, loss_target, m_meta_tokens, m_in_ln_g, m_in_ln_b, m_rel_bias, m_w_in, m_b_in, m_attn_sinks, m_w_attn_proj, m_conv_dw, m_conv_dw_b, m_conv_ln_g, m_conv_ln_b, m_w_conv_proj, m_w_out, m_ln1_g, m_ln1_b, m_ffn_w_up, m_ffn_dw, m_ffn_dw_b, m_ffn_w_down, m_ln2_g, m_ln2_b, v_meta_tokens, v_in_ln_g, v_in_ln_b, v_rel_bias, v_w_in, v_b_in, v_attn_sinks, v_w_attn_proj, v_conv_dw, v_conv_dw_b, v_conv_ln_g, v_conv_ln_b, v_w_conv_proj, v_w_out, v_ln1_g, v_ln1_b, v_ffn_w_up, v_ffn_dw, v_ffn_dw_b, v_ffn_w_down, v_ln2_g, v_ln2_b):
    nex, seq, _ = x.shape
    nblk = seq // BLK + 1
    tp = nblk * BLK
    m = nex * tp
    tm = _row_tile(m)
    ffs = ffn_w_up.shape[2]
    dff = 2 * ffs
    cx, cy, cc = lax.axis_index("x"), lax.axis_index("y"), lax.axis_index("c")
    chip = (2 * cx + cy).astype(jnp.int32)
    core = cc.astype(jnp.int32)

    names = ("in", "ap", "cp", "out", "up", "down")
    big = dict(zip(names, [w_in, w_attn_proj, w_conv_proj, w_out, ffn_w_up, ffn_w_down]))
    sb = {k: v.astype(BF16) for k, v in big.items()}
    gathered = {}

    def land(items, outs):
        return [lax.dynamic_update_index_in_dim(g, s if layer is None else s[layer], chip, 0)
                for (s, layer), g in zip(items, outs)]

    first_items = [(meta_tokens.reshape(2, N_META // 2, -1), None), (conv_dw, None), (ffn_dw, None)]
    g_meta, g_cdw, g_fdw = land(first_items, _run_job("gather_small", _gather_job(first_items)))
    meta_full = jnp.transpose(g_meta, (1, 2, 0, 3)).reshape(N_META, D)
    bias_tab = _bias_build(rel_bias)

    fwd_plan = {("embed_ln", 0): [("in", 0)],
                ("in_proj", 0): [("ap", 0), ("cp", 0), ("out", 0)], ("attn_fwd", 0): [("up", 0)],
                ("conv_fwd", 0): [("down", 0)], ("mix_fwd", 0): [("in", 1)],
                ("out_proj_ln", 0): [("ap", 1), ("cp", 1), ("out", 1)], ("ffn_up_act", 0): [("up", 1), ("down", 1)]}

    def fwd(tag, l, fn, *args):
        keys = fwd_plan.get((tag, l))
        if not keys:
            return fn(*args)
        items = [(sb[k], kl) for k, kl in keys]
        res = fn(*args, side=_gather_job(items))
        for key, g in zip(keys, land(items, res[-len(keys):])):
            gathered[key] = g
        main = res[:-len(keys)]
        return main[0] if len(main) == 1 else main

    def layer_weights(l):
        win_old = jnp.transpose(gathered[("in", l)], (1, 0, 2)).reshape(D, IN_COLS)
        return dict(
            win=_to_new(win_old), bin=_to_new(b_in[l]).reshape(1, IN_COLS),
            cdw=jnp.transpose(g_cdw[:, l], (1, 0, 2)).reshape(CTAPS, CW),
            fdw=jnp.transpose(g_fdw[:, l], (1, 0, 2)).reshape(FTAPS, 2 * dff),
            fdwb=ffn_dw_b[l].reshape(1, 2 * dff))

    raw, h, hb = fwd("embed_ln", 0, _embed_ln, x, meta_full, in_ln_g, in_ln_b, nblk)
    saved, lw = [], []
    for l in range(DEPTH):
        p = layer_weights(l)
        z = fwd("in_proj", l, functools.partial(_mm_bias, "in_proj"), hb, p["win"], p["bin"], IN_COLS // 3, tm)
        a = fwd("attn_fwd", l, _attn_fwd, z, bias_tab, attn_sinks[l], nex, nblk)
        ccv, cs = fwd("conv_fwd", l, _conv_fwd, z, p["cdw"], conv_dw_b[l], conv_ln_g[l], conv_ln_b[l], nex, tp)
        p["wap"], p["wcp"] = gathered[("ap", l)], gathered[("cp", l)]
        ya, yc, mixed = fwd("mix_fwd", l, _mix_fwd, a, cs, p["wap"], p["wcp"], z, tm)
        p["wout"] = gathered[("out", l)].reshape(D, D)
        r1, h1, h1b = fwd("out_proj_ln", l, functools.partial(_mm_res_ln, "out_proj_ln"), mixed, p["wout"], h, ln1_g[l], ln1_b[l], tm)
        p["wup"] = gathered[("up", l)]
        up3, ug, act = fwd("ffn_up_act", l, _ffn_up_act, h1b, p["wup"], p["fdw"], p["fdwb"], tm, nex, tp)
        p["wdown"] = gathered[("down", l)].reshape(dff, D)
        r2, h2, h2b = _mm_res_ln("ffn_down_ln", act, p["wdown"], h1, ln2_g[l], ln2_b[l], tm)
        saved.append(dict(hb=hb, z=z, a=a, cc=ccv, cs=cs, ya=ya, yc=yc, mixed=mixed, r1=r1, h1b=h1b, up3=up3, ug=ug, act=act, r2=r2))
        lw.append(p)
        h, hb = h2, h2b

    dy, sq = _loss_grad(h, loss_target, nblk)
    loss = lax.psum(0.5 / D * jnp.sum(sq), ("x", "y", "c"))

    grads, swapped, pair_sums, reduced = {}, {}, {}, {}
    cvec, where = core.reshape(1), jnp.stack([chip, core])
    last = [(k, DEPTH - 1) for k in names]
    bwd_plan = {("ln2_bwd", 0): ("swap", last),
                ("ffn_act_bwd", 0): ("exch", [("up", 1), ("down", 1)]),
                ("ffn_conv_bwd", 0): ("exch", [("in", 1), ("ap", 1), ("cp", 1), ("out", 1)]),
                ("ln1_bwd", 0): ("swap", [("down", 0), ("up", 0)]),
                ("conv_bwd_b", 0): ("swap", [("out", 0), ("ap", 0), ("cp", 0)]),
                ("attn_bwd", 0): ("exch", [("down", 0), ("up", 0), ("out", 0), ("ap", 0), ("cp", 0)]),
                ("in_ln_bwd", 0): ("swap", [("in", 0)])}

    def after(kind, keys, outs):
        for key, o in zip(keys, outs):
            if kind == "swap":
                swapped[key] = o
                pair_sums[key] = _add_half("grad_add_sibling", grads[key], o, cvec)
            else:
                reduced[key] = _add_chips("grad_add_chips", grads[key], swapped[key], o, where)

    def bwd(tag, l, fn, *args):
        plan = bwd_plan.get((tag, l))
        if plan is None:
            return fn(*args)
        kind, keys = plan
        job = _swap_job([grads[k] for k in keys]) if kind == "swap" else _exchange_job([pair_sums[k] for k in keys])
        res = fn(*args, side=job)
        after(kind, keys, res[-len(keys):])
        main = res[:-len(keys)]
        return main[0] if len(main) == 1 else main

    small = {}
    prev_a, prev_w, prev_cols = (), None, None
    dprev = dy
    for l in reversed(range(DEPTH)):
        p, s = lw[l], saved[l]
        dr2, dr2b, dg2, db2 = bwd("ln2_bwd", l, functools.partial(_ln_bwd_call, "ln2_bwd"), dprev, s["r2"], ln2_g[l], tm,
                                  prev_a, prev_w, prev_cols)
        dup3 = bwd("ffn_act_bwd", l, _ffn_act_bwd, dr2b, p["wdown"], s["ug"], tm)
        dpre3, dfdw, dfdwb = bwd("ffn_conv_bwd", l, _ffn_conv_bwd, dup3, s["up3"], p["fdw"], tm, nex, tp)
        grads[("down", l)] = _mm_tn("dw_down", s["act"], dr2b, ffs, D // 2).reshape(NCHIP, dff // NCHIP, D)
        grads[("up", l)] = _mm_tn("dw_up", s["h1b"], dpre3, D, ffs, b_cols=lambda j: (j // 2, j % 2), chip_out=True)
        dr1, dr1b, dg1, db1 = bwd("ln1_bwd", l, functools.partial(_ln_bwd_call, "ln1_bwd"), dr2, s["r1"], ln1_g[l], tm // 2,
                                  (dpre3,), p["wup"])
        dya, dyc, dzg, csg = _gate_bwd(dr1b, p["wout"], s["ya"], s["yc"], s["z"], tm)
        grads[("out", l)] = _mm_tn("dw_out", s["mixed"], dr1b, D, D // 2).reshape(NCHIP, D // NCHIP, D)
        grads[("ap", l)] = _mm_tn("dw_attn_proj", s["a"], dya, AW, D // NCHIP, chip_out=True)
        grads[("cp", l)] = _mm_tn("dw_conv_proj", s["cs"], dyc, CW, D // NCHIP, chip_out=True)
        dcc, dclg, dclb, dcwb = _conv_bwd_a(dyc, p["wcp"], s["cc"], conv_ln_g[l], conv_ln_b[l], tm)
        dzc, dcdw, csc = bwd("conv_bwd_b", l, _conv_bwd_b, dcc, s["z"], p["cdw"], nex, tp)
        dzq, csq, dsk, dbias = bwd("attn_bwd", l, _attn_bwd, s["z"], bias_tab, attn_sinks[l], dya, p["wap"], nex, nblk)
        gin = [_mm_tn("dw_in_gates", s["hb"], dzg, D, D // 2), _mm_tn("dw_in_conv", s["hb"], dzc, D, CW),
               _mm_tn("dw_in_qkv", s["hb"], dzq, D, 2 * KVW)]
        gin_old = _to_old(jnp.concatenate(gin, axis=1))
        grads[("in", l)] = jnp.transpose(gin_old.reshape(D, NCHIP, IN_COLS // NCHIP), (1, 0, 2))
        small[l] = dict(
            b_in=_to_old(jnp.concatenate([csg, csc, csq], axis=1)).reshape(IN_COLS), attn_sinks=dsk[0, :NQ],
            conv_dw=dcdw[:CTAPS], conv_dw_b=dcwb.reshape(CW), conv_ln_g=dclg.reshape(CW), conv_ln_b=dclb.reshape(CW),
            ln1_g=dg1.reshape(D), ln1_b=db1.reshape(D),
            ffn_dw=jnp.transpose(dfdw, (1, 0, 2)).reshape(FTAPS, 2 * dff), ffn_dw_b=jnp.transpose(dfdwb, (1, 0, 2)).reshape(2 * dff),
            ln2_g=dg2.reshape(D), ln2_b=db2.reshape(D), bias=dbias)
        dprev = dr1
        prev_a, prev_w, prev_cols = (dzg, dzc, dzq), p["win"], [(C_GATES, C_CONV), (C_CONV, C_QKV), (C_QKV, IN_COLS)]
    draw, _, dg0, db0 = bwd("in_ln_bwd", 0, functools.partial(_ln_bwd_call, "in_ln_bwd"), dprev, raw, in_ln_g, tm,
                            prev_a, prev_w, prev_cols)
    draw3 = draw.reshape(nex, tp, D)
    grad_x = draw3[:, BLK:, :]
    dmeta = _sum0("meta_grad_sum", draw3[:, PAD:BLK, :])

    names_l = ["b_in", "attn_sinks", "conv_dw", "conv_dw_b", "conv_ln_g", "conv_ln_b", "ln1_g", "ln1_b", "ffn_dw", "ffn_dw_b", "ln2_g", "ln2_b"]
    dbias_all = _bias_grad(jnp.concatenate([small[l]["bias"] for l in range(DEPTH)], axis=0))
    part_list = [dmeta, dg0.reshape(D), db0.reshape(D), dbias_all]
    part_list += [jnp.stack([small[0][n], small[1][n]]) for n in names_l]
    shapes_small = [tuple(a.shape) for a in part_list]
    tot = _sum0("small_grad_sum", _allgather_small(_pack(part_list)))
    (g_meta_f, g_inlg, g_inlb, g_biasp, g_bin, g_sinks, g_cdw_f, g_cdwb, g_clg, g_clb, g_l1g, g_l1b, g_fdw_f, g_fdwb,
     g_l2g, g_l2b) = _unpack(tot, shapes_small)
    g_relb = g_biasp
    csh = D // NCHIP
    g_meta_s = lax.dynamic_slice_in_dim(g_meta_f, chip * csh, csh, axis=1)
    g_cdw_s = lax.dynamic_slice_in_dim(g_cdw_f, chip * (CW // NCHIP), CW // NCHIP, axis=2)
    g_fdw_s = lax.dynamic_slice_in_dim(g_fdw_f, chip * ffs, ffs, axis=2)

    tail = [("in", 0)]
    after("exch", tail, _run_job("grad_chip_exchange", _exchange_job([pair_sums[k] for k in tail])))
    joined = _sibling_join([[reduced[(k, l)] for l in range(DEPTH)] for k in names])
    full = []
    for k, f in zip(names, joined):
        hh = reduced[(k, 0)].shape[0]
        for l in range(DEPTH):
            f = lax.dynamic_update_slice(f, reduced[(k, l)][None], (l, core * hh, 0))
        full.append(f)

    moms = [m_w_in, m_w_attn_proj, m_w_conv_proj, m_w_out, m_ffn_w_up, m_ffn_w_down]
    vels = [v_w_in, v_w_attn_proj, v_w_conv_proj, v_w_out, v_ffn_w_up, v_ffn_w_down]
    big_out = []
    for w, g, mo, ve in zip(big.values(), full, moms, vels):
        sh = w.shape
        two = lambda t: t.reshape(sh[0] * sh[1], sh[2])
        d_, m_, v_ = _adamw("adamw_matrix", two(w), two(g), two(mo), two(ve))
        big_out.append((g.reshape(sh), d_.reshape(sh), m_.reshape(sh), v_.reshape(sh)))

    sm_w = [meta_tokens, in_ln_g, in_ln_b, rel_bias, b_in, attn_sinks, conv_dw, conv_dw_b, conv_ln_g, conv_ln_b, ln1_g, ln1_b,
            ffn_dw, ffn_dw_b, ln2_g, ln2_b]
    sm_m = [m_meta_tokens, m_in_ln_g, m_in_ln_b, m_rel_bias, m_b_in, m_attn_sinks, m_conv_dw, m_conv_dw_b, m_conv_ln_g, m_conv_ln_b,
            m_ln1_g, m_ln1_b, m_ffn_dw, m_ffn_dw_b, m_ln2_g, m_ln2_b]
    sm_v = [v_meta_tokens, v_in_ln_g, v_in_ln_b, v_rel_bias, v_b_in, v_attn_sinks, v_conv_dw, v_conv_dw_b, v_conv_ln_g, v_conv_ln_b,
            v_ln1_g, v_ln1_b, v_ffn_dw, v_ffn_dw_b, v_ln2_g, v_ln2_b]
    sm_g = [g_meta_s, g_inlg, g_inlb, g_relb, g_bin, g_sinks, g_cdw_s, g_cdwb, g_clg, g_clb, g_l1g, g_l1b, g_fdw_s, g_fdwb, g_l2g, g_l2b]
    sm_shapes = [tuple(a.shape) for a in sm_w]
    sd, smn, svn = _adamw("adamw_small", _pack(sm_w), _pack(sm_g), _pack(sm_m), _pack(sm_v))
    sd, smn, svn = _unpack(sd, sm_shapes), _unpack(smn, sm_shapes), _unpack(svn, sm_shapes)

    order = ["meta_tokens", "in_ln_g", "in_ln_b", "rel_bias", "w_in", "b_in", "attn_sinks", "w_attn_proj", "conv_dw", "conv_dw_b",
             "conv_ln_g", "conv_ln_b", "w_conv_proj", "w_out", "ln1_g", "ln1_b", "ffn_w_up", "ffn_dw", "ffn_dw_b", "ffn_w_down",
             "ln2_g", "ln2_b"]
    small_names = ["meta_tokens", "in_ln_g", "in_ln_b", "rel_bias", "b_in", "attn_sinks", "conv_dw", "conv_dw_b", "conv_ln_g",
                   "conv_ln_b", "ln1_g", "ln1_b", "ffn_dw", "ffn_dw_b", "ln2_g", "ln2_b"]
    big_names = ["w_in", "w_attn_proj", "w_conv_proj", "w_out", "ffn_w_up", "ffn_w_down"]
    res = {}
    for i, n in enumerate(small_names):
        res[n] = (sm_g[i], sd[i], smn[i], svn[i])
    for i, n in enumerate(big_names):
        res[n] = big_out[i]
    outs = [loss, grad_x]
    for k in range(4):
        outs += [res[n][k] for n in order]
    return tuple(outs)
```

```python
import functools
import math
from typing import Any, Callable, NamedTuple, Sequence

import numpy as np
import jax
import jax.numpy as jnp
from jax import lax
from jax.experimental import pallas as pl
from jax.experimental.pallas import tpu as pltpu

F32 = jnp.float32
BF16 = jnp.bfloat16
MESH = pl.DeviceIdType.MESH

D = 1024
N_META = 16
BLK = 128
PAD = BLK - N_META
HD = 64
NQ = 8
NKV = 2
GRP = NQ // NKV
AW = NQ * HD
KVW = NKV * HD
CW = D // 2
CTAPS = 31
FTAPS = 3
NBUCKET = 32
MAXDIST = 128
EPS = 1e-5
DEPTH = 2
ALPHA = (2.0 * DEPTH) ** 0.25
NCHIP = 4
NKEY = 3 * BLK
NEG = -1e30
CHALO = 32
FHALO = 8
IN_COLS = AW + 2 * KVW + 2 * CW + 2 * D
_OLD = dict(q=(0, AW), k=(AW, AW + KVW), v=(AW + KVW, AW + 2 * KVW), cv=(AW + 2 * KVW, AW + 2 * KVW + CW),
            cg=(AW + 2 * KVW + CW, AW + 2 * KVW + 2 * CW), ga=(AW + 2 * KVW + 2 * CW, AW + 2 * KVW + 2 * CW + D),
            gc=(AW + 2 * KVW + 2 * CW + D, IN_COLS))
_NEW_ORDER = ("ga", "gc", "cv", "cg", "q", "k", "v")
C_GATES, C_CONV, C_QKV = 0, 2 * D, 2 * D + 2 * CW

ADAM_LR, ADAM_B1, ADAM_B2, ADAM_EPS, ADAM_WD, ADAM_STEP = 0.001, 0.9, 0.999, 1e-08, 0.01, 10


def _to_new(a):
    return jnp.concatenate([a[..., _OLD[n][0]:_OLD[n][1]] for n in _NEW_ORDER], axis=-1)


def _to_old(a):
    offs, o = {}, 0
    for n in _NEW_ORDER:
        w = _OLD[n][1] - _OLD[n][0]
        offs[n] = (o, o + w)
        o += w
    return jnp.concatenate([a[..., offs[n][0]:offs[n][1]] for n in ("q", "k", "v", "cv", "cg", "ga", "gc")], axis=-1)


class _Job(NamedTuple):
    ins: Sequence[Any]
    outs: Sequence[Any]
    sems: tuple
    start: Callable
    finish: Callable


_ANY = pl.BlockSpec(memory_space=pl.ANY)


def _call(body, name, out_shape, grid, in_specs, out_specs, scratch=(), prefetch=0, side=None):
    params = pltpu.CompilerParams(dimension_semantics=("arbitrary",) * len(grid))
    if side is None:
        if prefetch:
            gs = pltpu.PrefetchScalarGridSpec(num_scalar_prefetch=prefetch, grid=grid, in_specs=in_specs,
                                              out_specs=out_specs, scratch_shapes=list(scratch))
            return pl.pallas_call(body, name=name, out_shape=out_shape, grid_spec=gs, compiler_params=params)
        return pl.pallas_call(body, name=name, out_shape=out_shape, grid=grid, in_specs=in_specs, out_specs=out_specs,
                              scratch_shapes=list(scratch), compiler_params=params)
    assert not prefetch
    single = not isinstance(out_shape, (tuple, list))
    main_shapes = (out_shape,) if single else tuple(out_shape)
    main_specs = (out_specs,) if single else tuple(out_specs)
    n_in, n_sin, n_out, n_sout, n_scr = len(in_specs), len(side.ins), len(main_shapes), len(side.outs), len(scratch)

    def wrapped(*refs):
        main_in, sin = refs[:n_in], refs[n_in:n_in + n_sin]
        o0 = n_in + n_sin
        main_out, sout = refs[o0:o0 + n_out], refs[o0 + n_out:o0 + n_out + n_sout]
        s0 = o0 + n_out + n_sout
        main_scr, (send, recv) = refs[s0:s0 + n_scr], refs[s0 + n_scr:]
        first = functools.reduce(lambda a, b: a & b, [pl.program_id(k) == 0 for k in range(len(grid))])
        last = functools.reduce(lambda a, b: a & b, [pl.program_id(k) == grid[k] - 1 for k in range(len(grid))])

        @pl.when(first)
        def _():
            side.start(sin, sout, send, recv)

        body(*main_in, *main_out, *main_scr)

        @pl.when(last)
        def _():
            side.finish(sin, sout, send, recv)

    call = pl.pallas_call(
        wrapped, name=name, out_shape=main_shapes + tuple(side.outs), grid=grid,
        in_specs=list(in_specs) + [_ANY] * n_sin, out_specs=main_specs + tuple([_ANY] * n_sout),
        scratch_shapes=list(scratch) + [pltpu.SemaphoreType.DMA(side.sems), pltpu.SemaphoreType.DMA(side.sems)],
        compiler_params=params)
    return lambda *args: call(*args, *side.ins)


def _row_tile(m):
    best = 32
    for t in range(32, 641, 32):
        if m % t == 0:
            best = t
    return best


def _pad_rows(rows, nex, tp):
    m = rows < PAD
    for b in range(1, nex):
        m = m | ((rows >= b * tp) & (rows < b * tp + PAD))
    return m


def _ln_stats(x):
    mu = jnp.mean(x, axis=-1, keepdims=True)
    xc = x - mu
    var = jnp.mean(xc * xc, axis=-1, keepdims=True)
    rstd = lax.rsqrt(var + EPS)
    return xc * rstd, rstd


def _ln_bwd(dy, xhat, rstd, g):
    dxh = dy * g
    m1 = jnp.mean(dxh, axis=-1, keepdims=True)
    m2 = jnp.mean(dxh * xhat, axis=-1, keepdims=True)
    return rstd * (dxh - m1 - xhat * m2)


def _dot(a, b):
    return jnp.dot(a, b, preferred_element_type=F32)


def _dot_nt(a, b):
    return lax.dot_general(a, b, (((1,), (1,)), ((), ())), preferred_element_type=F32)


def _dot_tn(a, b):
    return lax.dot_general(a, b, (((0,), (0,)), ((), ())), preferred_element_type=F32)


def _sigmoid(x):
    return 1.0 / (1.0 + jnp.exp(-x))


def _bucket_np(d):
    n = np.maximum(d, 0)
    max_exact = NBUCKET // 2
    nf = np.maximum(n, 1).astype(np.float32)
    large = max_exact + (np.log(nf / np.float32(max_exact)) / np.float32(math.log(MAXDIST / max_exact))
                         * np.float32(NBUCKET - max_exact)).astype(np.int32)
    large = np.minimum(large, NBUCKET - 1)
    return np.where(n < max_exact, n, large).astype(np.int32)


def _bias_index():
    i = np.arange(BLK)[:, None]
    j = np.arange(2 * BLK)[None, :]
    d = BLK + i - j
    band_ok = (d >= 0) & (d < BLK)
    band = _bucket_np(d)
    idx = np.full((3, BLK, NKEY), -1, np.int32)
    m = np.arange(N_META)[None, :]
    d0 = (i - PAD) - m
    idx[0, :, 2 * BLK:2 * BLK + N_META] = np.where(d0 >= 0, _bucket_np(d0), -1)
    ok1 = band_ok & (j >= BLK)
    idx[1, :, :2 * BLK] = np.where(ok1, band, -1)
    idx[1, :, 2 * BLK:2 * BLK + N_META] = _bucket_np((N_META + i) - m)
    idx[2, :, :2 * BLK] = np.where(band_ok, band, -1)
    idx[2, :, 2 * BLK:2 * BLK + N_META] = NBUCKET - 1
    return idx


def _bias_build(rel_bias):
    idx = jnp.asarray(_bias_index())

    def body(idx_ref, rb_ref, o_ref):
        ix = idx_ref[...]
        for h in range(NQ):
            acc = jnp.full(ix.shape, NEG, F32)
            for b in range(NBUCKET):
                acc = jnp.where(ix == b, rb_ref[b, h], acc)
            o_ref[:, h, :, :] = acc

    return pl.pallas_call(
        body, name="bias_build", out_shape=jax.ShapeDtypeStruct((3, NQ, BLK, NKEY), F32),
        in_specs=[pl.BlockSpec(memory_space=pltpu.VMEM), pl.BlockSpec(memory_space=pltpu.SMEM)],
        out_specs=pl.BlockSpec(memory_space=pltpu.VMEM))(idx, rel_bias)


def _bias_grad(dbias):
    idx = jnp.asarray(_bias_index())

    def body(idx_ref, d_ref, o_ref):
        d = jnp.sum(d_ref[...], axis=0)
        for b in range(NBUCKET):
            acc = jnp.zeros((NQ, NKEY), F32)
            for case in range(3):
                hit = (idx_ref[case] == b)[None, :, :]
                acc = acc + jnp.sum(jnp.where(hit, d[case], 0.0), axis=1)
            o_ref[b] = jnp.sum(acc, axis=-1, keepdims=True)

    out = pl.pallas_call(
        body, name="bias_grad", out_shape=jax.ShapeDtypeStruct((NBUCKET, NQ, 1), F32),
        in_specs=[pl.BlockSpec(memory_space=pltpu.VMEM), pl.BlockSpec(memory_space=pltpu.VMEM)],
        out_specs=pl.BlockSpec(memory_space=pltpu.VMEM))(idx, dbias)
    return out.reshape(NBUCKET, NQ)


def _embed_ln(x, meta, g, b, nblk, side=None):
    nex, seq, _ = x.shape
    m = nex * nblk * BLK

    def body(x_ref, meta_ref, g_ref, b_ref, raw_ref, h_ref, hb_ref):
        j = pl.program_id(1)

        @pl.when(j == 0)
        def _():
            raw_ref[0:PAD, :] = jnp.zeros((PAD, D), F32)
            raw_ref[PAD:BLK, :] = meta_ref[...]

        @pl.when(j > 0)
        def _():
            raw_ref[...] = x_ref[...]

        xhat, _ = _ln_stats(raw_ref[...])
        y = xhat * g_ref[...] + b_ref[...]
        h_ref[...] = y
        hb_ref[...] = y.astype(BF16)

    row = lambda bb, j: (bb * nblk + j, 0)
    return _call(
        body, "embed_ln",
        (jax.ShapeDtypeStruct((m, D), F32), jax.ShapeDtypeStruct((m, D), F32), jax.ShapeDtypeStruct((m, D), BF16)),
        (nex, nblk),
        [pl.BlockSpec((None, BLK, D), lambda bb, j: (bb, jnp.maximum(j - 1, 0), 0)),
         pl.BlockSpec((N_META, D), lambda bb, j: (0, 0)),
         pl.BlockSpec((1, D), lambda bb, j: (0, 0)), pl.BlockSpec((1, D), lambda bb, j: (0, 0))],
        (pl.BlockSpec((BLK, D), row), pl.BlockSpec((BLK, D), row), pl.BlockSpec((BLK, D), row)), side=side,
    )(x, meta, g.reshape(1, D), b.reshape(1, D))


def _mm_bias(name, a, w, bias, tn, tm, side=None):
    m, k = a.shape
    n = w.shape[1]

    def body(a_ref, w_ref, b_ref, o_ref):
        o_ref[...] = _dot(a_ref[...], w_ref[...]) + b_ref[...]

    return _call(body, name, jax.ShapeDtypeStruct((m, n), F32), (n // tn, m // tm),
                 [pl.BlockSpec((tm, k), lambda j, i: (i, 0)), pl.BlockSpec((k, tn), lambda j, i: (0, j)),
                  pl.BlockSpec((1, tn), lambda j, i: (0, j))],
                 pl.BlockSpec((tm, tn), lambda j, i: (i, j)), side=side)(a, w, bias)


def _ffn_up_act(a, w4, cw, cb, tm, nex, tp, side=None):
    m, k = a.shape
    ffs = w4.shape[2]

    def body(a_ref, wu_ref, wg_ref, cu_ref, cg_ref, bu_ref, bg_ref, up_ref, ug_ref, act_ref, win):
        i = pl.program_id(1)

        @pl.when(i == 0)
        def _():
            win[:, 0:FHALO, :] = jnp.zeros((2, FHALO, ffs), F32)

        rows = i * tm + lax.broadcasted_iota(jnp.int32, (tm, 1), 0)
        pad = _pad_rows(rows, nex, tp)
        av = a_ref[...]
        for p, w_ref in ((0, wu_ref), (1, wg_ref)):
            x = jnp.where(pad, 0.0, _dot(av, w_ref[...]))
            win[p, FHALO:FHALO + tm, :] = x
            up_ref[p] = x.astype(BF16)
        for r0 in range(0, tm, RCH):
            for c0, c1 in _lane_groups(ffs):
                u = _conv3(win, 0, r0, c0, c1, cu_ref, bu_ref)
                g = _conv3(win, 1, r0, c0, c1, cg_ref, bg_ref)
                ug_ref[0, r0:r0 + RCH, c0:c1] = u.astype(BF16)
                ug_ref[1, r0:r0 + RCH, c0:c1] = g.astype(BF16)
                act_ref[r0:r0 + RCH, c0:c1] = (g * (0.5 * (1.0 + lax.erf(g * (1.0 / math.sqrt(2.0))))) * u).astype(BF16)
        win[:, 0:FHALO, :] = win[:, tm:tm + FHALO, :]

    wide = jax.ShapeDtypeStruct((2, m, 2 * ffs), BF16)
    return _call(body, "ffn_up_act", (wide, wide, jax.ShapeDtypeStruct((m, 2 * ffs), BF16)),
                 (2, m // tm),
                 [pl.BlockSpec((tm, k), lambda c, i: (i, 0)),
                  pl.BlockSpec((None, k, ffs), lambda c, i: (c, 0, 0)), pl.BlockSpec((None, k, ffs), lambda c, i: (c + 2, 0, 0)),
                  pl.BlockSpec((FTAPS, ffs), lambda c, i: (0, c)), pl.BlockSpec((FTAPS, ffs), lambda c, i: (0, c + 2)),
                  pl.BlockSpec((1, ffs), lambda c, i: (0, c)), pl.BlockSpec((1, ffs), lambda c, i: (0, c + 2))],
                 (pl.BlockSpec((2, tm, ffs), lambda c, i: (0, i, c)), pl.BlockSpec((2, tm, ffs), lambda c, i: (0, i, c)),
                  pl.BlockSpec((tm, ffs), lambda c, i: (i, c))),
                 scratch=[pltpu.VMEM((2, FHALO + tm, ffs), F32)], side=side)(a, w4, w4, cw, cw, cb, cb)


def _fill_kv(ks, vs, e, prev_ref, cur_ref, meta_ref):
    for piece, lo, n in ((prev_ref, 0, BLK), (cur_ref, BLK, BLK), (meta_ref, 2 * BLK, N_META)):
        val = piece[e]
        for hk in range(NKV):
            ks[e, hk, lo:lo + n, :] = val[:, hk * HD:(hk + 1) * HD].astype(BF16)
            vs[e, hk, lo:lo + n, :] = val[:, KVW + hk * HD:KVW + (hk + 1) * HD].astype(BF16)
    for hk in range(NKV):
        ks[e, hk, 2 * BLK + N_META:NKEY, :] = jnp.zeros((BLK - N_META, HD), BF16)
        vs[e, hk, 2 * BLK + N_META:NKEY, :] = jnp.zeros((BLK - N_META, HD), BF16)


def _interleave(chains):
    live = list(chains)
    while live:
        for c in list(live):
            try:
                next(c)
            except StopIteration:
                live.remove(c)


def _head_softmax(q, ks_hk, bias_ref, sink_ref, h, out):
    qh = q[:, h * HD:(h + 1) * HD].astype(BF16)
    yield
    s = _dot_nt(qh, ks_hk) * (HD ** -0.5) + bias_ref[h]
    yield
    sink = sink_ref[0, h]
    mx = jnp.maximum(jnp.max(s, axis=-1, keepdims=True), sink)
    yield
    p = jnp.exp(s - mx)
    es = jnp.exp(sink - mx)
    yield
    inv = 1.0 / (jnp.sum(p, axis=-1, keepdims=True) + es)
    yield
    out["q"], out["p"], out["sink"] = qh, p * inv, es * inv
    yield


def _attn_specs(nex, blk_of):
    qcol, kvcol = (C_QKV) // AW, (C_QKV + AW) // (2 * KVW)
    return [
        pl.BlockSpec((nex, BLK, AW), lambda j: (0, blk_of(j), qcol)),
        pl.BlockSpec((nex, BLK, 2 * KVW), lambda j: (0, blk_of(j), kvcol)),
        pl.BlockSpec((nex, BLK, 2 * KVW), lambda j: (0, jnp.maximum(blk_of(j) - 1, 0), kvcol)),
        pl.BlockSpec((nex, N_META, 2 * KVW), lambda j: (0, PAD // N_META, kvcol)),
        pl.BlockSpec((None, NQ, BLK, NKEY), lambda j: (jnp.minimum(blk_of(j), 2), 0, 0, 0)),
        pl.BlockSpec(memory_space=pltpu.SMEM),
    ]


def _attn_fwd(z, bias, sinks, nex, nblk, side=None):
    m = z.shape[0]
    z3 = z.reshape(nex, nblk * BLK, z.shape[1])

    def head(e, h, q, ks, vs, bias_ref, sink_ref, oacc):
        out = {}
        yield from _head_softmax(q, ks[e, h // GRP], bias_ref, sink_ref, h, out)
        o = _dot(out["p"].astype(BF16), vs[e, h // GRP])
        yield
        oacc[e, :, h * HD:(h + 1) * HD] = o
        yield

    def body(q_ref, cur_ref, prev_ref, meta_ref, bias_ref, sink_ref, o_ref, ks, vs, oacc):
        for e in range(nex):
            _fill_kv(ks, vs, e, prev_ref, cur_ref, meta_ref)
            q = q_ref[e]
            _interleave([head(e, h, q, ks, vs, bias_ref, sink_ref, oacc) for h in range(NQ)])
            o_ref[e] = oacc[e].astype(BF16)

    res = _call(body, "attn_fwd", jax.ShapeDtypeStruct((nex, nblk * BLK, AW), BF16), (nblk,),
                _attn_specs(nex, lambda j: j),
                pl.BlockSpec((nex, BLK, AW), lambda j: (0, j, 0)),
                scratch=[pltpu.VMEM((nex, NKV, NKEY, HD), BF16), pltpu.VMEM((nex, NKV, NKEY, HD), BF16),
                         pltpu.VMEM((nex, BLK, AW), F32)], side=side)(z3, z3, z3, z3, bias, sinks.reshape(1, NQ))
    if side is None:
        return res.reshape(m, AW)
    return (res[0].reshape(m, AW),) + tuple(res[1:])


def _cgate(cv, cg, rows, nex, tp):
    return jnp.where(_pad_rows(rows, nex, tp), 0.0, cv * _sigmoid(cg))


CLANES = 256


def _rolled_up(blk, b):
    return blk if b == 0 else pltpu.roll(blk, blk.shape[0] - b, axis=0)


def _conv_fwd(z, w, wb, g, b, nex, tp, side=None):
    m = z.shape[0]
    tm = BLK
    sub = tm // CHALO
    cvc, cgc = C_CONV // CW, C_CONV // CW + 1

    def body(cv_ref, cg_ref, cvh_ref, cgh_ref, w_ref, wb_ref, g_ref, b_ref, cc_ref, cs_ref, win):
        i = pl.program_id(0)
        rows = i * tm + lax.broadcasted_iota(jnp.int32, (tm, 1), 0)
        hrows = i * tm - CHALO + lax.broadcasted_iota(jnp.int32, (CHALO, 1), 0)
        win[0:CHALO, :] = _cgate(cvh_ref[...], cgh_ref[...], hrows, nex, tp)
        win[CHALO:CHALO + tm, :] = _cgate(cv_ref[...], cg_ref[...], rows, nex, tp)
        for sb in range(sub):
            lo = sb * CHALO
            for c0 in range(0, CW, CLANES):
                blk = win[lo:lo + 2 * CHALO, c0:c0 + CLANES]
                acc = jnp.zeros((CHALO, CLANES), F32) + wb_ref[:, c0:c0 + CLANES]
                for b in range(8):
                    rb = _rolled_up(blk, b)
                    for a in range(5):
                        s = 8 * a + b
                        if 2 <= s <= CTAPS + 1:
                            acc = acc + w_ref[s - 2:s - 1, c0:c0 + CLANES] * rb[8 * a:8 * a + CHALO]
                cc_ref[lo:lo + CHALO, c0:c0 + CLANES] = acc
        xhat, _ = _ln_stats(cc_ref[...])
        cl = xhat * g_ref[...] + b_ref[...]
        cs_ref[...] = (cl * _sigmoid(cl)).astype(BF16)

    halo = lambda i: jnp.maximum(i * sub - 1, 0)
    vec = pl.BlockSpec((1, CW), lambda i: (0, 0))
    return _call(body, "conv_fwd", (jax.ShapeDtypeStruct((m, CW), F32), jax.ShapeDtypeStruct((m, CW), BF16)),
                 (m // tm,),
                 [pl.BlockSpec((tm, CW), lambda i: (i, cvc)), pl.BlockSpec((tm, CW), lambda i: (i, cgc)),
                  pl.BlockSpec((CHALO, CW), lambda i: (halo(i), cvc)), pl.BlockSpec((CHALO, CW), lambda i: (halo(i), cgc)),
                  pl.BlockSpec((CTAPS, CW), lambda i: (0, 0)), vec, vec, vec],
                 (pl.BlockSpec((tm, CW), lambda i: (i, 0)), pl.BlockSpec((tm, CW), lambda i: (i, 0))),
                 scratch=[pltpu.VMEM((CHALO + tm, CW), F32)], side=side)(z, z, z, z, w, wb.reshape(1, CW), g.reshape(1, CW), b.reshape(1, CW))


def _mix_fwd(a, cs, wap4, wcp4, z, tm, side=None):
    m = a.shape[0]
    ns = wap4.shape[2]

    def body(a_ref, cs_ref, wa_ref, wc_ref, ga_ref, gc_ref, ya_ref, yc_ref, mix_ref):
        av, cv = a_ref[...], cs_ref[...]
        for j in range(NCHIP):
            ya_ref[:, j * ns:(j + 1) * ns] = _dot(av, wa_ref[j])
            yc_ref[:, j * ns:(j + 1) * ns] = _dot(cv, wc_ref[j])
        mix_ref[...] = (_sigmoid(ga_ref[...]) * ya_ref[...] + _sigmoid(gc_ref[...]) * yc_ref[...]).astype(BF16)

    wspec = pl.BlockSpec((NCHIP, AW, ns), lambda i: (0, 0, 0))
    row = lambda i: (i, 0)
    return _call(body, "mix_fwd",
                 (jax.ShapeDtypeStruct((m, D), F32), jax.ShapeDtypeStruct((m, D), F32), jax.ShapeDtypeStruct((m, D), BF16)),
                 (m // tm,),
                 [pl.BlockSpec((tm, AW), row), pl.BlockSpec((tm, CW), row), wspec, wspec,
                  pl.BlockSpec((tm, D), lambda i: (i, 0)), pl.BlockSpec((tm, D), lambda i: (i, 1))],
                 (pl.BlockSpec((tm, D), row), pl.BlockSpec((tm, D), row), pl.BlockSpec((tm, D), row)), side=side)(a, cs, wap4, wcp4, z, z)


def _mm_res_ln(name, a, w, res, g, b, tm, side=None):
    m, k = a.shape

    def body(a_ref, w_ref, res_ref, g_ref, b_ref, r_ref, h_ref, hb_ref):
        r = ALPHA * res_ref[...] + _dot(a_ref[...], w_ref[...])
        r_ref[...] = r
        xhat, _ = _ln_stats(r)
        y = xhat * g_ref[...] + b_ref[...]
        h_ref[...] = y
        hb_ref[...] = y.astype(BF16)

    row = lambda i: (i, 0)
    vec = pl.BlockSpec((1, D), lambda i: (0, 0))
    return _call(body, name,
                 (jax.ShapeDtypeStruct((m, D), F32), jax.ShapeDtypeStruct((m, D), F32), jax.ShapeDtypeStruct((m, D), BF16)),
                 (m // tm,),
                 [pl.BlockSpec((tm, k), row), pl.BlockSpec((k, D), lambda i: (0, 0)), pl.BlockSpec((tm, D), row), vec, vec],
                 (pl.BlockSpec((tm, D), row), pl.BlockSpec((tm, D), row), pl.BlockSpec((tm, D), row)), side=side)(a, w, res, g.reshape(1, D), b.reshape(1, D))


RCH = 16


def _lane_groups(width, most=768):
    n = -(-width // most)
    step = -(-width // (128 * n)) * 128
    return [(c, min(c + step, width)) for c in range(0, width, step)]


def _conv3(win, p, r0, c0, c1, w_ref, b_ref):
    blk = win[p, r0:r0 + FHALO + RCH, c0:c1]
    x0, x1, x2 = blk[FHALO:], pltpu.roll(blk, 1, axis=0)[FHALO:], pltpu.roll(blk, 2, axis=0)[FHALO:]
    return b_ref[:, c0:c1] + w_ref[0:1, c0:c1] * x2 + w_ref[1:2, c0:c1] * x1 + w_ref[2:3, c0:c1] * x0


def _loss_grad(y, target, nblk):
    nex = target.shape[0]
    m = y.shape[0]

    def body(y_ref, t_ref, dy_ref, acc_ref):
        bb, j = pl.program_id(0), pl.program_id(1)

        @pl.when((bb == 0) & (j == 0))
        def _():
            acc_ref[...] = jnp.zeros_like(acc_ref)

        @pl.when(j == 0)
        def _():
            dy_ref[...] = jnp.zeros_like(dy_ref)

        @pl.when(j > 0)
        def _():
            e = y_ref[...] - t_ref[...]
            dy_ref[...] = e * (1.0 / D)
            acc_ref[...] += jnp.sum((e * e).reshape(BLK // 8, 8, D), axis=0)

    return _call(body, "loss_grad", (jax.ShapeDtypeStruct((m, D), F32), jax.ShapeDtypeStruct((8, D), F32)), (nex, nblk),
                 [pl.BlockSpec((BLK, D), lambda bb, j: (bb * nblk + j, 0)),
                  pl.BlockSpec((None, BLK, D), lambda bb, j: (bb, jnp.maximum(j - 1, 0), 0))],
                 (pl.BlockSpec((BLK, D), lambda bb, j: (bb * nblk + j, 0)), pl.BlockSpec((8, D), lambda bb, j: (0, 0))))(y, target)


def _ln_bwd_call(name, dy, r, g, tm, a_list=(), w=None, cols=None, side=None):
    m = dy.shape[0]
    na = len(a_list)

    def body(*refs):
        dy_ref, r_ref, g_ref = refs[0:3]
        a_refs = refs[3:3 + na]
        w_ref = refs[3 + na] if na else None
        dr_ref, drb_ref, dg_ref, db_ref = refs[-4:]
        i = pl.program_id(0)
        dh = dy_ref[...]
        if na:
            dh = ALPHA * dh
            if cols is None:
                ns = w.shape[2]
                for j in range(NCHIP):
                    dh = dh + _dot_nt(a_refs[0][j // 2, :, (j % 2) * ns:(j % 2 + 1) * ns], w_ref[j])
            else:
                for a_ref, (c0, c1) in zip(a_refs, cols):
                    dh = dh + _dot_nt(a_ref[...], w_ref[:, c0:c1])
        xhat, rstd = _ln_stats(r_ref[...])
        dr = _ln_bwd(dh, xhat, rstd, g_ref[...])
        dr_ref[...] = dr
        drb_ref[...] = dr.astype(BF16)

        @pl.when(i == 0)
        def _():
            dg_ref[...] = jnp.zeros_like(dg_ref)
            db_ref[...] = jnp.zeros_like(db_ref)

        dg_ref[...] += jnp.sum(dh * xhat, axis=0, keepdims=True)
        db_ref[...] += jnp.sum(dh, axis=0, keepdims=True)

    row = lambda i: (i, 0)
    vec = pl.BlockSpec((1, D), lambda i: (0, 0))
    in_specs = [pl.BlockSpec((tm, D), row), pl.BlockSpec((tm, D), row), vec]
    for a in a_list:
        in_specs.append(pl.BlockSpec((2, tm, a.shape[2]), lambda i: (0, i, 0)) if a.ndim == 3 else pl.BlockSpec((tm, a.shape[1]), row))
    if na:
        in_specs.append(pl.BlockSpec(w.shape, (lambda i: (0, 0, 0)) if w.ndim == 3 else (lambda i: (0, 0))))
    return _call(body, name,
                 (jax.ShapeDtypeStruct((m, D), F32), jax.ShapeDtypeStruct((m, D), BF16),
                  jax.ShapeDtypeStruct((1, D), F32), jax.ShapeDtypeStruct((1, D), F32)),
                 (m // tm,), in_specs,
                 (pl.BlockSpec((tm, D), row), pl.BlockSpec((tm, D), row), vec, vec),
                 side=side)(dy, r, g.reshape(1, D), *a_list, *([w] if na else []))


def _ffn_act_bwd(drb, wdown, ug, tm, side=None):
    _, m, dff = ug.shape
    ffs = dff // 2

    def body(dr_ref, wd_ref, ug_ref, o_ref, dact):
        dact[...] = _dot_nt(dr_ref[...], wd_ref[...])
        for r0 in range(0, tm, RCH):
            for c0, c1 in _lane_groups(ffs):
                u = ug_ref[0, r0:r0 + RCH, c0:c1].astype(F32)
                g = ug_ref[1, r0:r0 + RCH, c0:c1].astype(F32)
                da = dact[r0:r0 + RCH, c0:c1]
                cdf = 0.5 * (1.0 + lax.erf(g * (1.0 / math.sqrt(2.0))))
                pdf = jnp.exp(-0.5 * g * g) * (1.0 / math.sqrt(2.0 * math.pi))
                o_ref[0, r0:r0 + RCH, c0:c1] = da * (g * cdf)
                o_ref[1, r0:r0 + RCH, c0:c1] = da * u * (cdf + g * pdf)

    return _call(body, "ffn_act_bwd", jax.ShapeDtypeStruct((2, m, dff), F32), (2, m // tm),
                 [pl.BlockSpec((tm, D), lambda c, i: (i, 0)), pl.BlockSpec((ffs, D), lambda c, i: (c, 0)),
                  pl.BlockSpec((2, tm, ffs), lambda c, i: (0, i, c))],
                 pl.BlockSpec((2, tm, ffs), lambda c, i: (0, i, c)),
                 scratch=[pltpu.VMEM((tm, ffs), F32)], side=side)(drb, wdown, ug)


def _ffn_conv_bwd(dup3, up3, w, tm, nex, tp, side=None):
    _, m, dff = up3.shape
    ffs = dff // 2
    sub = tm // FHALO
    nt = m // tm

    def body(d_ref, dh_ref, x_ref, wu_ref, wg_ref, o_ref, dw_ref, db_ref, dwin, dwacc, dbacc):
        i = pl.program_id(1)
        dwin[:, 0:tm, :] = d_ref[...]
        dwin[:, tm:tm + FHALO, :] = jnp.where(i == nt - 1, 0.0, dh_ref[...])
        dwacc[...] = jnp.zeros_like(dwacc)
        dbacc[...] = jnp.zeros_like(dbacc)

        @pl.when(i == 0)
        def _():
            dw_ref[...] = jnp.zeros_like(dw_ref)
            db_ref[...] = jnp.zeros_like(db_ref)

        fold = lambda t: t[0:8, :] + t[8:16, :]
        for r0 in range(0, tm, RCH):
            rows = i * tm + r0 + lax.broadcasted_iota(jnp.int32, (RCH, 1), 0)
            pad = _pad_rows(rows, nex, tp)
            for p, w_ref in ((0, wu_ref), (1, wg_ref)):
                for c0, c1 in _lane_groups(ffs):
                    dblk = dwin[p, r0:r0 + RCH + FHALO, c0:c1]
                    d0 = dblk[:RCH]
                    d1 = pltpu.roll(dblk, RCH + FHALO - 1, axis=0)[:RCH]
                    d2 = pltpu.roll(dblk, RCH + FHALO - 2, axis=0)[:RCH]
                    dpre = w_ref[2:3, c0:c1] * d0 + w_ref[1:2, c0:c1] * d1 + w_ref[0:1, c0:c1] * d2
                    o_ref[p, r0:r0 + RCH, c0:c1] = jnp.where(pad, 0.0, dpre).astype(BF16)
                    x0 = x_ref[p, r0:r0 + RCH, c0:c1].astype(F32)
                    dwacc[p, 2, :, c0:c1] += fold(d0 * x0)
                    dwacc[p, 1, :, c0:c1] += fold(d1 * x0)
                    dwacc[p, 0, :, c0:c1] += fold(d2 * x0)
                    dbacc[p, :, c0:c1] += fold(d0)
        for p in range(2):
            for k in range(FTAPS):
                dw_ref[p, k:k + 1, :] += jnp.sum(dwacc[p, k], axis=0, keepdims=True)
            db_ref[p] += jnp.sum(dbacc[p], axis=0, keepdims=True)

    nxt = lambda i: jnp.minimum((i + 1) * sub, m // FHALO - 1)
    return _call(body, "ffn_conv_bwd",
                 (jax.ShapeDtypeStruct((2, m, dff), BF16), jax.ShapeDtypeStruct((2, FTAPS, dff), F32),
                  jax.ShapeDtypeStruct((2, 1, dff), F32)),
                 (2, nt),
                 [pl.BlockSpec((2, tm, ffs), lambda c, i: (0, i, c)), pl.BlockSpec((2, FHALO, ffs), lambda c, i: (0, nxt(i), c)),
                  pl.BlockSpec((2, tm, ffs), lambda c, i: (0, i, c)),
                  pl.BlockSpec((FTAPS, ffs), lambda c, i: (0, c)), pl.BlockSpec((FTAPS, ffs), lambda c, i: (0, c + 2))],
                 (pl.BlockSpec((2, tm, ffs), lambda c, i: (0, i, c)), pl.BlockSpec((2, FTAPS, ffs), lambda c, i: (0, 0, c)),
                  pl.BlockSpec((2, 1, ffs), lambda c, i: (0, 0, c))),
                 scratch=[pltpu.VMEM((2, tm + FHALO, ffs), F32),
                          pltpu.VMEM((2, FTAPS, 8, ffs), F32), pltpu.VMEM((2, 8, ffs), F32)], side=side)(dup3, dup3, up3, w, w)


def _mm_tn(name, a, b, tk, tn, b_cols=None, chip_out=False):
    m, k = a.shape
    n = b.shape[-1] * (2 if b.ndim == 3 else 1)

    def body(a_ref, b_ref, o_ref):
        o_ref[...] = _dot_tn(a_ref[...], b_ref[...])

    if b.ndim == 3:
        bspec = pl.BlockSpec((None, m, tn), lambda kk, j: (b_cols(j)[0], 0, b_cols(j)[1]))
    else:
        bspec = pl.BlockSpec((m, tn), lambda kk, j: (0, j))
    if chip_out:
        oshape, ospec = (n // tn, k, tn), pl.BlockSpec((None, tk, tn), lambda kk, j: (j, kk, 0))
    else:
        oshape, ospec = (k, n), pl.BlockSpec((tk, tn), lambda kk, j: (kk, j))
    return _call(body, name, jax.ShapeDtypeStruct(oshape, F32), (k // tk, n // tn),
                 [pl.BlockSpec((m, tk), lambda kk, j: (0, kk)), bspec], ospec)(a, b)


def _gate_bwd(drb, wout, ya, yc, z, tm):
    m = drb.shape[0]

    def body(dr_ref, w_ref, ya_ref, yc_ref, ga_ref, gc_ref, dya_ref, dyc_ref, dz_ref, cs_ref):
        i = pl.program_id(0)
        dmix = _dot_nt(dr_ref[...], w_ref[...])
        sa, sc = _sigmoid(ga_ref[...]), _sigmoid(gc_ref[...])
        dya_ref[...] = (dmix * sa).astype(BF16)
        dyc_ref[...] = (dmix * sc).astype(BF16)
        dga = dmix * ya_ref[...] * sa * (1.0 - sa)
        dgc = dmix * yc_ref[...] * sc * (1.0 - sc)
        dz_ref[:, 0:D] = dga.astype(BF16)
        dz_ref[:, D:2 * D] = dgc.astype(BF16)

        @pl.when(i == 0)
        def _():
            cs_ref[...] = jnp.zeros_like(cs_ref)

        cs_ref[:, 0:D] += jnp.sum(dga, axis=0, keepdims=True)
        cs_ref[:, D:2 * D] += jnp.sum(dgc, axis=0, keepdims=True)

    row = lambda i: (i, 0)
    return _call(body, "gate_bwd",
                 (jax.ShapeDtypeStruct((m, D), BF16), jax.ShapeDtypeStruct((m, D), BF16),
                  jax.ShapeDtypeStruct((m, 2 * D), BF16), jax.ShapeDtypeStruct((1, 2 * D), F32)),
                 (m // tm,),
                 [pl.BlockSpec((tm, D), row), pl.BlockSpec((D, D), lambda i: (0, 0)), pl.BlockSpec((tm, D), row),
                  pl.BlockSpec((tm, D), row), pl.BlockSpec((tm, D), lambda i: (i, 0)), pl.BlockSpec((tm, D), lambda i: (i, 1))],
                 (pl.BlockSpec((tm, D), row), pl.BlockSpec((tm, D), row), pl.BlockSpec((tm, 2 * D), row),
                  pl.BlockSpec((1, 2 * D), lambda i: (0, 0))))(drb, wout, ya, yc, z, z)


def _conv_bwd_a(dyc, wcp4, cc, g, b, tm):
    m = cc.shape[0]
    ns = wcp4.shape[2]

    def body(dy_ref, w_ref, cc_ref, g_ref, b_ref, dcc_ref, dg_ref, db_ref, dwb_ref):
        i = pl.program_id(0)
        dcs = jnp.zeros((tm, CW), F32)
        for j in range(NCHIP):
            dcs = dcs + _dot_nt(dy_ref[:, j * ns:(j + 1) * ns], w_ref[j])
        xhat, rstd = _ln_stats(cc_ref[...])
        cl = xhat * g_ref[...] + b_ref[...]
        sg = _sigmoid(cl)
        dcl = dcs * sg * (1.0 + cl * (1.0 - sg))
        dcc = _ln_bwd(dcl, xhat, rstd, g_ref[...])
        dcc_ref[...] = dcc

        @pl.when(i == 0)
        def _():
            dg_ref[...] = jnp.zeros_like(dg_ref)
            db_ref[...] = jnp.zeros_like(db_ref)
            dwb_ref[...] = jnp.zeros_like(dwb_ref)

        dg_ref[...] += jnp.sum(dcl * xhat, axis=0, keepdims=True)
        db_ref[...] += jnp.sum(dcl, axis=0, keepdims=True)
        dwb_ref[...] += jnp.sum(dcc, axis=0, keepdims=True)

    row = lambda i: (i, 0)
    vec = pl.BlockSpec((1, CW), lambda i: (0, 0))
    v = jax.ShapeDtypeStruct((1, CW), F32)
    return _call(body, "conv_bwd_a", (jax.ShapeDtypeStruct((m, CW), F32), v, v, v), (m // tm,),
                 [pl.BlockSpec((tm, D), row), pl.BlockSpec((NCHIP, CW, ns), lambda i: (0, 0, 0)), pl.BlockSpec((tm, CW), row), vec, vec],
                 (pl.BlockSpec((tm, CW), row), vec, vec, vec))(dyc, wcp4, cc, g.reshape(1, CW), b.reshape(1, CW))


def _conv_bwd_b(dcc, z, w, nex, tp, side=None):
    m = dcc.shape[0]
    tm = BLK
    sub = tm // CHALO
    nt = m // tm
    cvc, cgc = C_CONV // CW, C_CONV // CW + 1

    def body(d_ref, dh_ref, cv_ref, cg_ref, w_ref, dz_ref, dw_ref, cs_ref, dwin, dwacc):
        i = pl.program_id(0)
        rows = i * tm + lax.broadcasted_iota(jnp.int32, (tm, 1), 0)
        dwin[0:tm, :] = d_ref[...]
        dwin[tm:tm + CHALO, :] = jnp.where(i == nt - 1, 0.0, dh_ref[...])

        @pl.when(i == 0)
        def _():
            dwacc[...] = jnp.zeros_like(dwacc)
            cs_ref[...] = jnp.zeros_like(cs_ref)

        fold = lambda t: (t[0:8] + t[8:16]) + (t[16:24] + t[24:32])
        for sb in range(sub):
            lo = sb * CHALO
            pad = _pad_rows(rows[lo:lo + CHALO], nex, tp)
            for c0 in range(0, CW, CLANES):
                cs_ = slice(c0, c0 + CLANES)
                cv = cv_ref[lo:lo + CHALO, cs_]
                sg = _sigmoid(cg_ref[lo:lo + CHALO, cs_])
                cgin = jnp.where(pad, 0.0, cv * sg)
                blk = dwin[lo:lo + 2 * CHALO, cs_]
                acc = jnp.zeros((CHALO, CLANES), F32)
                for b in range(8):
                    rb = _rolled_up(blk, b)
                    for a in range(4):
                        s = 8 * a + b
                        if s <= CTAPS - 1:
                            k = CTAPS - 1 - s
                            sh = rb[8 * a:8 * a + CHALO]
                            acc = acc + w_ref[k:k + 1, cs_] * sh
                            dwacc[k, :, cs_] += fold(sh * cgin)
                dcg = jnp.where(pad, 0.0, acc)
                dcv = dcg * sg
                dgt = dcg * cv * sg * (1.0 - sg)
                dz_ref[lo:lo + CHALO, cs_] = dcv.astype(BF16)
                dz_ref[lo:lo + CHALO, CW + c0:CW + c0 + CLANES] = dgt.astype(BF16)
                cs_ref[:, cs_] += jnp.sum(dcv, axis=0, keepdims=True)
                cs_ref[:, CW + c0:CW + c0 + CLANES] += jnp.sum(dgt, axis=0, keepdims=True)

        @pl.when(i == nt - 1)
        def _():
            for k in range(CHALO):
                dw_ref[k:k + 1, :] = jnp.sum(dwacc[k], axis=0, keepdims=True)

    nxt = lambda i: jnp.minimum((i + 1) * sub, m // CHALO - 1)
    return _call(body, "conv_bwd_b",
                 (jax.ShapeDtypeStruct((m, 2 * CW), BF16), jax.ShapeDtypeStruct((CHALO, CW), F32),
                  jax.ShapeDtypeStruct((1, 2 * CW), F32)),
                 (nt,),
                 [pl.BlockSpec((tm, CW), lambda i: (i, 0)), pl.BlockSpec((CHALO, CW), lambda i: (nxt(i), 0)),
                  pl.BlockSpec((tm, CW), lambda i: (i, cvc)), pl.BlockSpec((tm, CW), lambda i: (i, cgc)),
                  pl.BlockSpec((CTAPS, CW), lambda i: (0, 0))],
                 (pl.BlockSpec((tm, 2 * CW), lambda i: (i, 0)), pl.BlockSpec((CHALO, CW), lambda i: (0, 0)),
                  pl.BlockSpec((1, 2 * CW), lambda i: (0, 0))),
                 scratch=[pltpu.VMEM((tm + CHALO, CW), F32), pltpu.VMEM((CHALO, 8, CW), F32)], side=side)(dcc, dcc, z, z, w)


def _attn_bwd(z, bias, sinks, dya, wap4, nex, nblk, side=None):
    m = z.shape[0]
    tp = nblk * BLK
    ns = wap4.shape[2]
    blk_of = lambda j: nblk - 1 - j
    z3 = z.reshape(nex, tp, z.shape[1])
    dya3 = dya.reshape(nex, tp, D)

    def body(q_ref, cur_ref, prev_ref, meta_ref, bias_ref, sink_ref, dy_ref, w_ref,
             dz_ref, cs_ref, dsk_ref, dbias_ref, ks, vs, carry, macc, dqacc, dkv, okv):
        j = pl.program_id(0)
        n = nblk - 1 - j

        @pl.when(j == 0)
        def _():
            carry[...] = jnp.zeros_like(carry)
            macc[...] = jnp.zeros_like(macc)
            cs_ref[...] = jnp.zeros_like(cs_ref)
            dsk_ref[...] = jnp.zeros_like(dsk_ref)

        @pl.when((j == 0) | (n <= 1))
        def _():
            dbias_ref[...] = jnp.zeros_like(dbias_ref)

        lane = lax.broadcasted_iota(jnp.int32, (1, BLK), 1)
        dsk = jnp.zeros((1, BLK), F32)

        def head(e, h, q, da, out):
            hk = h // GRP
            yield from _head_softmax(q, ks[e, hk], bias_ref, sink_ref, h, out)
            pn = out["p"]
            doh = da[:, h * HD:(h + 1) * HD].astype(BF16)
            yield
            dp = _dot_nt(doh, vs[e, hk])
            yield
            dl = jnp.sum(pn * dp, axis=-1, keepdims=True)
            yield
            ds = pn * (dp - dl)
            yield
            dbias_ref[h] += ds
            out["dsink"] = jnp.sum(-out["sink"] * dl)
            yield
            dsb = (ds * (HD ** -0.5)).astype(BF16)
            yield
            dqacc[e, :, h * HD:(h + 1) * HD] = _dot(dsb, ks[e, hk])
            out["ds"], out["pb"], out["do"] = dsb, pn.astype(BF16), doh
            yield

        def kv_head(e, hk, outs):
            rows = lambda key: jnp.concatenate([outs[hk * GRP + g][key] for g in range(GRP)], axis=0)
            dk = _dot_tn(rows("ds"), rows("q"))
            yield
            dv = _dot_tn(rows("pb"), rows("do"))
            yield
            dkv[e, :, hk * HD:(hk + 1) * HD] = dk
            dkv[e, :, KVW + hk * HD:KVW + (hk + 1) * HD] = dv
            yield

        for e in range(nex):
            _fill_kv(ks, vs, e, prev_ref, cur_ref, meta_ref)
            q = q_ref[e]
            da = jnp.zeros((BLK, AW), F32)
            for jj in range(NCHIP):
                da = da + _dot_nt(dy_ref[e, :, jj * ns:(jj + 1) * ns], w_ref[jj])
            outs = [{} for _ in range(NQ)]
            _interleave([head(e, h, q, da, outs[h]) for h in range(NQ)])
            _interleave([kv_head(e, hk, outs) for hk in range(NKV)])
            for h in range(NQ):
                dsk = dsk + jnp.where(lane == h, outs[h]["dsink"], 0.0)
            macc[e] += dkv[e, 2 * BLK:2 * BLK + N_META, :]
            okv[e] = dkv[e, BLK:2 * BLK, :] + carry[e]
            carry[e] = dkv[e, 0:BLK, :]

            @pl.when(n == 0)
            def _():
                okv[e, PAD:BLK, :] += macc[e]

            dq = dqacc[e]
            ok = okv[e]
            dz_ref[e, :, 0:AW] = dq.astype(BF16)
            dz_ref[e, :, AW:AW + 2 * KVW] = ok.astype(BF16)
            cs_ref[:, 0:AW] += jnp.sum(dq, axis=0, keepdims=True)
            cs_ref[:, AW:AW + 2 * KVW] += jnp.sum(ok, axis=0, keepdims=True)
        dsk_ref[...] += dsk

    wz = AW + 2 * KVW
    specs = _attn_specs(nex, blk_of) + [
        pl.BlockSpec((nex, BLK, D), lambda j: (0, blk_of(j), 0)),
        pl.BlockSpec((NCHIP, AW, ns), lambda j: (0, 0, 0))]
    res = _call(body, "attn_bwd",
                (jax.ShapeDtypeStruct((nex, tp, wz), BF16), jax.ShapeDtypeStruct((1, wz), F32), jax.ShapeDtypeStruct((1, BLK), F32),
                 jax.ShapeDtypeStruct((1, 3, NQ, BLK, NKEY), F32)),
                (nblk,), specs,
                (pl.BlockSpec((nex, BLK, wz), lambda j: (0, blk_of(j), 0)), pl.BlockSpec((1, wz), lambda j: (0, 0)),
                 pl.BlockSpec((1, BLK), lambda j: (0, 0)),
                 pl.BlockSpec((None, None, NQ, BLK, NKEY), lambda j: (0, jnp.minimum(blk_of(j), 2), 0, 0, 0))),
                scratch=[pltpu.VMEM((nex, NKV, NKEY, HD), BF16), pltpu.VMEM((nex, NKV, NKEY, HD), BF16),
                         pltpu.VMEM((nex, BLK, 2 * KVW), F32), pltpu.VMEM((nex, N_META, 2 * KVW), F32),
                         pltpu.VMEM((nex, BLK, AW), F32), pltpu.VMEM((nex, NKEY, 2 * KVW), F32),
                         pltpu.VMEM((nex, BLK, 2 * KVW), F32)],
                side=side)(z3, z3, z3, z3, bias, sinks.reshape(1, NQ), dya3, wap4)
    return (res[0].reshape(m, wz),) + tuple(res[1:])


def _tile_rows(rows, cols, target_bytes=1 << 20):
    best = None
    for t in range(8, rows + 1, 8):
        if rows % t == 0 and t * cols * 4 <= target_bytes:
            best = t
    return best or rows


def _sum0(name, x):
    n, r, c = x.shape
    tr = _tile_rows(r, c * n)

    def body(x_ref, o_ref):
        acc = x_ref[0]
        for k in range(1, n):
            acc = acc + x_ref[k]
        o_ref[...] = acc

    return _call(body, name, jax.ShapeDtypeStruct((r, c), F32), (r // tr,),
                 [pl.BlockSpec((n, tr, c), lambda i: (0, i, 0))], pl.BlockSpec((tr, c), lambda i: (i, 0)))(x)


def _adamw(name, w, g, mom, vel):
    r, c = w.shape
    tr = _tile_rows(r, c)
    c1 = 1.0 / (1.0 - ADAM_B1 ** ADAM_STEP)
    c2 = 1.0 / (1.0 - ADAM_B2 ** ADAM_STEP)

    def body(w_ref, g_ref, m_ref, v_ref, d_ref, mo_ref, vo_ref):
        gg = g_ref[...]
        mn = ADAM_B1 * m_ref[...] + (1.0 - ADAM_B1) * gg
        vn = ADAM_B2 * v_ref[...] + (1.0 - ADAM_B2) * (gg * gg)
        mo_ref[...] = mn
        vo_ref[...] = vn
        d_ref[...] = -ADAM_LR * ((mn * c1) / (jnp.sqrt(vn * c2) + ADAM_EPS) + ADAM_WD * w_ref[...])

    spec = pl.BlockSpec((tr, c), lambda i: (i, 0))
    o = jax.ShapeDtypeStruct((r, c), F32)
    return _call(body, name, (o, o, o), (r // tr,), [spec] * 4, (spec, spec, spec))(w, g, mom, vel)


def _place():
    x, y, c = lax.axis_index("x"), lax.axis_index("y"), lax.axis_index("c")
    others = [(1 - x, y), (x, 1 - y), (1 - x, 1 - y)]
    return x, y, c, others


def _gather_job(items):
    nw = len(items)

    def views(s_ref, g_ref, layer, c):
        if layer is None:
            return s_ref.at[c], lambda chip, cc: g_ref.at[chip, cc]
        hr = s_ref.shape[1] // 2
        return s_ref.at[layer, pl.ds(c * hr, hr)], lambda chip, cc: g_ref.at[chip, pl.ds(cc * hr, hr)]

    def copies(s_refs, g_refs, send, recv):
        x, y, c, others = _place()
        chip = 2 * x + y
        firsts, arrive, passed, arrive2 = [], [], [], []
        for w, (_, layer) in enumerate(items):
            src, dst = views(s_refs[w], g_refs[w], layer, c)
            for k, (px, py) in enumerate(others):
                def rc(kk, s, d, to, w=w):
                    return pltpu.make_async_remote_copy(src_ref=s, dst_ref=d, send_sem=send.at[w, kk], recv_sem=recv.at[w, kk],
                                                        device_id=to, device_id_type=MESH)
                got, got2 = dst(2 * px + py, c), dst(2 * px + py, 1 - c)
                firsts.append(rc(k, src, dst(chip, c), (px, py, c)))
                arrive.append(rc(k, got, got, (x, y, c)))
                passed.append(rc(3 + k, got, got, (x, y, 1 - c)))
                arrive2.append(rc(3 + k, got2, got2, (x, y, c)))
        return firsts, arrive, passed, arrive2

    def start(s_refs, g_refs, send, recv):
        for cp in copies(s_refs, g_refs, send, recv)[0]:
            cp.start()

    def finish(s_refs, g_refs, send, recv):
        firsts, arrive, passed, arrive2 = copies(s_refs, g_refs, send, recv)
        for a, p in zip(arrive, passed):
            a.wait_recv()
            p.start()
        for a in arrive2:
            a.wait_recv()
        for cp in firsts + passed:
            cp.wait_send()

    outs = [jax.ShapeDtypeStruct((NCHIP,) + (s.shape if layer is None else s.shape[1:]), s.dtype) for s, layer in items]
    return _Job([s for s, _ in items], outs, (nw, 6), start, finish)


def _swap_job(grads):
    def copies(d_refs, a_refs, send, recv):
        x, y, c, _ = _place()
        cps = []
        for w in range(len(grads)):
            h = d_refs[w].shape[1] // 2
            cps.append(pltpu.make_async_remote_copy(
                src_ref=d_refs[w].at[:, pl.ds((1 - c) * h, h), :], dst_ref=a_refs[w], send_sem=send.at[w], recv_sem=recv.at[w],
                device_id=(x, y, 1 - c), device_id_type=MESH))
        return cps

    def start(*r):
        for cp in copies(*r):
            cp.start()

    def finish(*r):
        for cp in copies(*r):
            cp.wait()

    outs = [jax.ShapeDtypeStruct((NCHIP, g.shape[1] // 2, g.shape[2]), g.dtype) for g in grads]
    return _Job(list(grads), outs, (len(grads),), start, finish)


def _exchange_job(parts):
    def copies(q_refs, b_refs, send, recv):
        x, y, c, others = _place()
        cps = []
        for w in range(len(parts)):
            for k, (px, py) in enumerate(others):
                cps.append(pltpu.make_async_remote_copy(
                    src_ref=q_refs[w].at[2 * px + py], dst_ref=b_refs[w].at[k], send_sem=send.at[w, k], recv_sem=recv.at[w, k],
                    device_id=(px, py, c), device_id_type=MESH))
        return cps

    def start(*r):
        for cp in copies(*r):
            cp.start()

    def finish(*r):
        for cp in copies(*r):
            cp.wait()

    outs = [jax.ShapeDtypeStruct((3,) + p.shape[1:], p.dtype) for p in parts]
    return _Job(list(parts), outs, (len(parts), 3), start, finish)


def _run_job(name, job):
    n_in, n_out = len(job.ins), len(job.outs)

    def body(*refs):
        ins, outs = refs[:n_in], refs[n_in:n_in + n_out]
        send, recv = refs[n_in + n_out:]
        job.start(ins, outs, send, recv)
        job.finish(ins, outs, send, recv)

    return pl.pallas_call(
        body, name=name, out_shape=tuple(job.outs), in_specs=[_ANY] * n_in, out_specs=tuple([_ANY] * n_out),
        scratch_shapes=[pltpu.SemaphoreType.DMA(job.sems), pltpu.SemaphoreType.DMA(job.sems)])(*job.ins)


def _sibling_join(halves):
    nw = len(halves)

    def body(*refs):
        h_refs = refs[:2 * nw]
        f_refs = refs[2 * nw:3 * nw]
        send, recv = refs[3 * nw:]
        x, y, c, _ = _place()
        cps = []
        for w in range(nw):
            for l in range(2):
                src = h_refs[2 * w + l]
                h = src.shape[0]
                dst = f_refs[w].at[l, pl.ds(c * h, h), :]
                cp = pltpu.make_async_remote_copy(src_ref=src, dst_ref=dst, send_sem=send.at[w, l], recv_sem=recv.at[w, l],
                                                  device_id=(x, y, 1 - c), device_id_type=MESH)
                cp.start()
                cps.append(cp)
        for w in range(nw):
            for l in range(2):
                src = h_refs[2 * w + l]
                h = src.shape[0]
                other = f_refs[w].at[l, pl.ds((1 - c) * h, h), :]
                pltpu.make_async_remote_copy(src_ref=src, dst_ref=other, send_sem=send.at[w, l], recv_sem=recv.at[w, l],
                                             device_id=(x, y, c), device_id_type=MESH).wait_recv()
        for cp in cps:
            cp.wait_send()

    flat = [a for pair in halves for a in pair]
    outs = tuple(jax.ShapeDtypeStruct((2, 2 * pair[0].shape[0], pair[0].shape[1]), F32) for pair in halves)
    return pl.pallas_call(
        body, name="grad_sibling_join", out_shape=outs, in_specs=[_ANY] * (2 * nw), out_specs=tuple([_ANY] * nw),
        scratch_shapes=[pltpu.SemaphoreType.DMA((nw, 2)), pltpu.SemaphoreType.DMA((nw, 2))])(*flat)


def _allgather_small(v):
    r = v.shape[0]

    def body(x_ref, out_ref, send_sems, recv_sems, local_sem):
        x, y, c, chips = _place()
        me, sibling = (x, y, c), (x, y, 1 - c)

        def slab(px, py, pc):
            return out_ref.at[4 * px + 2 * py + pc]

        def copy(k, block, to, src=None):
            return pltpu.make_async_remote_copy(src_ref=slab(*block) if src is None else src, dst_ref=slab(*block),
                                                send_sem=send_sems.at[k], recv_sem=recv_sems.at[k],
                                                device_id=to, device_id_type=MESH)

        mine = pltpu.make_async_copy(x_ref, slab(*me), local_sem)
        mine.start()
        first = [copy(0, me, sibling, src=x_ref)]
        first += [copy(1 + j, me, (*chip, c), src=x_ref) for j, chip in enumerate(chips)]
        for cp in first:
            cp.start()
        passed = [copy(4 + j, (*chip, c), sibling) for j, chip in enumerate(chips)]
        for j, chip in enumerate(chips):
            copy(1 + j, (*chip, c), me).wait_recv()
            passed[j].start()
        copy(0, sibling, me).wait_recv()
        for j, chip in enumerate(chips):
            copy(4 + j, (*chip, 1 - c), me).wait_recv()
        for cp in first + passed:
            cp.wait_send()
        mine.wait()

    return pl.pallas_call(
        body, name="allgather_small", out_shape=jax.ShapeDtypeStruct((8, r, 128), F32),
        in_specs=[pl.BlockSpec(memory_space=pltpu.VMEM)], out_specs=pl.BlockSpec(memory_space=pltpu.VMEM),
        scratch_shapes=[pltpu.SemaphoreType.DMA((7,)), pltpu.SemaphoreType.DMA((7,)), pltpu.SemaphoreType.DMA],
    )(v)


def _add_half(name, d, a, c):
    _, h, cols = a.shape
    nt = 1
    th = h // nt

    def body(c_ref, d_ref, a_ref, o_ref):
        o_ref[...] = (d_ref[...] + a_ref[...]).astype(BF16)

    return _call(body, name, jax.ShapeDtypeStruct(a.shape, BF16), (NCHIP, nt),
                 [pl.BlockSpec((None, th, cols), lambda p, i, cr: (p, cr[0] * nt + i, 0)),
                  pl.BlockSpec((None, th, cols), lambda p, i, cr: (p, i, 0))],
                 pl.BlockSpec((None, th, cols), lambda p, i, cr: (p, i, 0)), prefetch=1)(c, d, a)


def _add_chips(name, d, a, b, where):
    _, h, cols = a.shape
    th = h // 4 if (h % 64 == 0) else h
    nt = h // th

    def body(w_ref, d_ref, a_ref, b_ref, o_ref):
        own = d_ref[...] + a_ref[...]
        o_ref[...] = ((own + b_ref[0].astype(F32)) + b_ref[1].astype(F32)) + b_ref[2].astype(F32)

    return _call(body, name, jax.ShapeDtypeStruct((h, cols), F32), (nt,),
                 [pl.BlockSpec((None, th, cols), lambda i, wr: (wr[0], wr[1] * nt + i, 0)),
                  pl.BlockSpec((None, th, cols), lambda i, wr: (wr[0], i, 0)),
                  pl.BlockSpec((3, th, cols), lambda i, wr: (0, i, 0))],
                 pl.BlockSpec((th, cols), lambda i, wr: (i, 0)), prefetch=1)(where, d, a, b)


def _pack(arrs):
    pieces = []
    for a in arrs:
        f = a.reshape(-1)
        n = -(-f.shape[0] // 1024) * 1024
        pieces.append(jnp.pad(f, (0, n - f.shape[0])).reshape(-1, 128))
    return jnp.concatenate(pieces, axis=0)


def _unpack(buf, shapes):
    out, r = [], 0
    for s in shapes:
        n = int(np.prod(s))
        rows = -(-n // 1024) * 8
        out.append(buf[r:r + rows].reshape(-1)[:n].reshape(s))
        r += rows
    return out


def kernel(x, meta_tokens, in_ln_g, in_ln_b, rel_bias, w_in, b_in, attn_sinks, w_attn_proj, conv_dw, conv_dw_b, conv_ln_g, conv_ln_b, w_conv_proj, w_out, ln1_g, ln1_b, ffn_w_up, ffn_dw, ffn_dw_b, ffn_w_down, ln2_g, ln2_b, loss_target, m_meta_tokens, m_in_ln_g, m_in_ln_b, m_rel_bias, m_w_in, m_b_in, m_attn_sinks, m_w_attn_proj, m_conv_dw, m_conv_dw_b, m_conv_ln_g, m_conv_ln_b, m_w_conv_proj, m_w_out, m_ln1_g, m_ln1_b, m_ffn_w_up, m_ffn_dw, m_ffn_dw_b, m_ffn_w_down, m_ln2_g, m_ln2_b, v_meta_tokens, v_in_ln_g, v_in_ln_b, v_rel_bias, v_w_in, v_b_in, v_attn_sinks, v_w_attn_proj, v_conv_dw, v_conv_dw_b, v_conv_ln_g, v_conv_ln_b, v_w_conv_proj, v_w_out, v_ln1_g, v_ln1_b, v_ffn_w_up, v_ffn_dw, v_ffn_dw_b, v_ffn_w_down, v_ln2_g, v_ln2_b):
    nex, seq, _ = x.shape
    nblk = seq // BLK + 1
    tp = nblk * BLK
    m = nex * tp
    tm = _row_tile(m)
    ffs = ffn_w_up.shape[2]
    dff = 2 * ffs
    cx, cy, cc = lax.axis_index("x"), lax.axis_index("y"), lax.axis_index("c")
    chip = (2 * cx + cy).astype(jnp.int32)
    core = cc.astype(jnp.int32)

    names = ("in", "ap", "cp", "out", "up", "down")
    big = dict(zip(names, [w_in, w_attn_proj, w_conv_proj, w_out, ffn_w_up, ffn_w_down]))
    sb = {k: v.astype(BF16) for k, v in big.items()}
    gathered = {}

    def land(items, outs):
        return [lax.dynamic_update_index_in_dim(g, s if layer is None else s[layer], chip, 0)
                for (s, layer), g in zip(items, outs)]

    first_items = [(meta_tokens.reshape(2, N_META // 2, -1), None), (conv_dw, None), (ffn_dw, None)]
    g_meta, g_cdw, g_fdw = land(first_items, _run_job("gather_small", _gather_job(first_items)))
    meta_full = jnp.transpose(g_meta, (1, 2, 0, 3)).reshape(N_META, D)
    bias_tab = _bias_build(rel_bias)

    fwd_plan = {("embed_ln", 0): [("in", 0)],
                ("in_proj", 0): [("ap", 0), ("cp", 0), ("out", 0)], ("attn_fwd", 0): [("up", 0)],
                ("conv_fwd", 0): [("down", 0)], ("mix_fwd", 0): [("in", 1)],
                ("out_proj_ln", 0): [("ap", 1), ("cp", 1), ("out", 1)], ("ffn_up_act", 0): [("up", 1), ("down", 1)]}

    def fwd(tag, l, fn, *args):
        keys = fwd_plan.get((tag, l))
        if not keys:
            return fn(*args)
        items = [(sb[k], kl) for k, kl in keys]
        res = fn(*args, side=_gather_job(items))
        for key, g in zip(keys, land(items, res[-len(keys):])):
            gathered[key] = g
        main = res[:-len(keys)]
        return main[0] if len(main) == 1 else main

    def layer_weights(l):
        win_old = jnp.transpose(gathered[("in", l)], (1, 0, 2)).reshape(D, IN_COLS)
        return dict(
            win=_to_new(win_old), bin=_to_new(b_in[l]).reshape(1, IN_COLS),
            cdw=jnp.transpose(g_cdw[:, l], (1, 0, 2)).reshape(CTAPS, CW),
            fdw=jnp.transpose(g_fdw[:, l], (1, 0, 2)).reshape(FTAPS, 2 * dff),
            fdwb=ffn_dw_b[l].reshape(1, 2 * dff))

    raw, h, hb = fwd("embed_ln", 0, _embed_ln, x, meta_full, in_ln_g, in_ln_b, nblk)
    saved, lw = [], []
    for l in range(DEPTH):
        p = layer_weights(l)
        z = fwd("in_proj", l, functools.partial(_mm_bias, "in_proj"), hb, p["win"], p["bin"], IN_COLS // 3, tm)
        a = fwd("attn_fwd", l, _attn_fwd, z, bias_tab, attn_sinks[l], nex, nblk)
        ccv, cs = fwd("conv_fwd", l, _conv_fwd, z, p["cdw"], conv_dw_b[l], conv_ln_g[l], conv_ln_b[l], nex, tp)
        p["wap"], p["wcp"] = gathered[("ap", l)], gathered[("cp", l)]
        ya, yc, mixed = fwd("mix_fwd", l, _mix_fwd, a, cs, p["wap"], p["wcp"], z, tm)
        p["wout"] = gathered[("out", l)].reshape(D, D)
        r1, h1, h1b = fwd("out_proj_ln", l, functools.partial(_mm_res_ln, "out_proj_ln"), mixed, p["wout"], h, ln1_g[l], ln1_b[l], tm)
        p["wup"] = gathered[("up", l)]
        up3, ug, act = fwd("ffn_up_act", l, _ffn_up_act, h1b, p["wup"], p["fdw"], p["fdwb"], tm, nex, tp)
        p["wdown"] = gathered[("down", l)].reshape(dff, D)
        r2, h2, h2b = _mm_res_ln("ffn_down_ln", act, p["wdown"], h1, ln2_g[l], ln2_b[l], tm)
        saved.append(dict(hb=hb, z=z, a=a, cc=ccv, cs=cs, ya=ya, yc=yc, mixed=mixed, r1=r1, h1b=h1b, up3=up3, ug=ug, act=act, r2=r2))
        lw.append(p)
        h, hb = h2, h2b

    dy, sq = _loss_grad(h, loss_target, nblk)
    loss = lax.psum(0.5 / D * jnp.sum(sq), ("x", "y", "c"))

    grads, swapped, pair_sums, reduced = {}, {}, {}, {}
    cvec, where = core.reshape(1), jnp.stack([chip, core])
    last = [(k, DEPTH - 1) for k in names]
    bwd_plan = {("ln2_bwd", 0): ("swap", last),
                ("ffn_act_bwd", 0): ("exch", [("up", 1), ("down", 1)]),
                ("ffn_conv_bwd", 0): ("exch", [("in", 1), ("ap", 1), ("cp", 1), ("out", 1)]),
                ("ln1_bwd", 0): ("swap", [("down", 0), ("up", 0)]),
                ("conv_bwd_b", 0): ("swap", [("out", 0), ("ap", 0), ("cp", 0)]),
                ("attn_bwd", 0): ("exch", [("down", 0), ("up", 0), ("out", 0), ("ap", 0), ("cp", 0)]),
                ("in_ln_bwd", 0): ("swap", [("in", 0)])}

    def after(kind, keys, outs):
        for key, o in zip(keys, outs):
            if kind == "swap":
                swapped[key] = o
                pair_sums[key] = _add_half("grad_add_sibling", grads[key], o, cvec)
            else:
                reduced[key] = _add_chips("grad_add_chips", grads[key], swapped[key], o, where)

    def bwd(tag, l, fn, *args):
        plan = bwd_plan.get((tag, l))
        if plan is None:
            return fn(*args)
        kind, keys = plan
        job = _swap_job([grads[k] for k in keys]) if kind == "swap" else _exchange_job([pair_sums[k] for k in keys])
        res = fn(*args, side=job)
        after(kind, keys, res[-len(keys):])
        main = res[:-len(keys)]
        return main[0] if len(main) == 1 else main

    small = {}
    prev_a, prev_w, prev_cols = (), None, None
    dprev = dy
    for l in reversed(range(DEPTH)):
        p, s = lw[l], saved[l]
        dr2, dr2b, dg2, db2 = bwd("ln2_bwd", l, functools.partial(_ln_bwd_call, "ln2_bwd"), dprev, s["r2"], ln2_g[l], tm,
                                  prev_a, prev_w, prev_cols)
        dup3 = bwd("ffn_act_bwd", l, _ffn_act_bwd, dr2b, p["wdown"], s["ug"], tm)
        dpre3, dfdw, dfdwb = bwd("ffn_conv_bwd", l, _ffn_conv_bwd, dup3, s["up3"], p["fdw"], tm, nex, tp)
        grads[("down", l)] = _mm_tn("dw_down", s["act"], dr2b, ffs, D // 2).reshape(NCHIP, dff // NCHIP, D)
        grads[("up", l)] = _mm_tn("dw_up", s["h1b"], dpre3, D, ffs, b_cols=lambda j: (j // 2, j % 2), chip_out=True)
        dr1, dr1b, dg1, db1 = bwd("ln1_bwd", l, functools.partial(_ln_bwd_call, "ln1_bwd"), dr2, s["r1"], ln1_g[l], tm // 2,
                                  (dpre3,), p["wup"])
        dya, dyc, dzg, csg = _gate_bwd(dr1b, p["wout"], s["ya"], s["yc"], s["z"], tm)
        grads[("out", l)] = _mm_tn("dw_out", s["mixed"], dr1b, D, D // 2).reshape(NCHIP, D // NCHIP, D)
        grads[("ap", l)] = _mm_tn("dw_attn_proj", s["a"], dya, AW, D // NCHIP, chip_out=True)
        grads[("cp", l)] = _mm_tn("dw_conv_proj", s["cs"], dyc, CW, D // NCHIP, chip_out=True)
        dcc, dclg, dclb, dcwb = _conv_bwd_a(dyc, p["wcp"], s["cc"], conv_ln_g[l], conv_ln_b[l], tm)
        dzc, dcdw, csc = bwd("conv_bwd_b", l, _conv_bwd_b, dcc, s["z"], p["cdw"], nex, tp)
        dzq, csq, dsk, dbias = bwd("attn_bwd", l, _attn_bwd, s["z"], bias_tab, attn_sinks[l], dya, p["wap"], nex, nblk)
        gin = [_mm_tn("dw_in_gates", s["hb"], dzg, D, D // 2), _mm_tn("dw_in_conv", s["hb"], dzc, D, CW),
               _mm_tn("dw_in_qkv", s["hb"], dzq, D, 2 * KVW)]
        gin_old = _to_old(jnp.concatenate(gin, axis=1))
        grads[("in", l)] = jnp.transpose(gin_old.reshape(D, NCHIP, IN_COLS // NCHIP), (1, 0, 2))
        small[l] = dict(
            b_in=_to_old(jnp.concatenate([csg, csc, csq], axis=1)).reshape(IN_COLS), attn_sinks=dsk[0, :NQ],
            conv_dw=dcdw[:CTAPS], conv_dw_b=dcwb.reshape(CW), conv_ln_g=dclg.reshape(CW), conv_ln_b=dclb.reshape(CW),
            ln1_g=dg1.reshape(D), ln1_b=db1.reshape(D),
            ffn_dw=jnp.transpose(dfdw, (1, 0, 2)).reshape(FTAPS, 2 * dff), ffn_dw_b=jnp.transpose(dfdwb, (1, 0, 2)).reshape(2 * dff),
            ln2_g=dg2.reshape(D), ln2_b=db2.reshape(D), bias=dbias)
        dprev = dr1
        prev_a, prev_w, prev_cols = (dzg, dzc, dzq), p["win"], [(C_GATES, C_CONV), (C_CONV, C_QKV), (C_QKV, IN_COLS)]
    draw, _, dg0, db0 = bwd("in_ln_bwd", 0, functools.partial(_ln_bwd_call, "in_ln_bwd"), dprev, raw, in_ln_g, tm,
                            prev_a, prev_w, prev_cols)
    draw3 = draw.reshape(nex, tp, D)
    grad_x = draw3[:, BLK:, :]
    dmeta = _sum0("meta_grad_sum", draw3[:, PAD:BLK, :])

    names_l = ["b_in", "attn_sinks", "conv_dw", "conv_dw_b", "conv_ln_g", "conv_ln_b", "ln1_g", "ln1_b", "ffn_dw", "ffn_dw_b", "ln2_g", "ln2_b"]
    dbias_all = _bias_grad(jnp.concatenate([small[l]["bias"] for l in range(DEPTH)], axis=0))
    part_list = [dmeta, dg0.reshape(D), db0.reshape(D), dbias_all]
    part_list += [jnp.stack([small[0][n], small[1][n]]) for n in names_l]
    shapes_small = [tuple(a.shape) for a in part_list]
    tot = _sum0("small_grad_sum", _allgather_small(_pack(part_list)))
    (g_meta_f, g_inlg, g_inlb, g_biasp, g_bin, g_sinks, g_cdw_f, g_cdwb, g_clg, g_clb, g_l1g, g_l1b, g_fdw_f, g_fdwb,
     g_l2g, g_l2b) = _unpack(tot, shapes_small)
    g_relb = g_biasp
    csh = D // NCHIP
    g_meta_s = lax.dynamic_slice_in_dim(g_meta_f, chip * csh, csh, axis=1)
    g_cdw_s = lax.dynamic_slice_in_dim(g_cdw_f, chip * (CW // NCHIP), CW // NCHIP, axis=2)
    g_fdw_s = lax.dynamic_slice_in_dim(g_fdw_f, chip * ffs, ffs, axis=2)

    tail = [("in", 0)]
    after("exch", tail, _run_job("grad_chip_exchange", _exchange_job([pair_sums[k] for k in tail])))
    joined = _sibling_join([[reduced[(k, l)] for l in range(DEPTH)] for k in names])
    full = []
    for k, f in zip(names, joined):
        hh = reduced[(k, 0)].shape[0]
        for l in range(DEPTH):
            f = lax.dynamic_update_slice(f, reduced[(k, l)][None], (l, core * hh, 0))
        full.append(f)

    moms = [m_w_in, m_w_attn_proj, m_w_conv_proj, m_w_out, m_ffn_w_up, m_ffn_w_down]
    vels = [v_w_in, v_w_attn_proj, v_w_conv_proj, v_w_out, v_ffn_w_up, v_ffn_w_down]
    big_out = []
    for w, g, mo, ve in zip(big.values(), full, moms, vels):
        sh = w.shape
        two = lambda t: t.reshape(sh[0] * sh[1], sh[2])
        d_, m_, v_ = _adamw("adamw_matrix", two(w), two(g), two(mo), two(ve))
        big_out.append((g.reshape(sh), d_.reshape(sh), m_.reshape(sh), v_.reshape(sh)))

    sm_w = [meta_tokens, in_ln_g, in_ln_b, rel_bias, b_in, attn_sinks, conv_dw, conv_dw_b, conv_ln_g, conv_ln_b, ln1_g, ln1_b,
            ffn_dw, ffn_dw_b, ln2_g, ln2_b]
    sm_m = [m_meta_tokens, m_in_ln_g, m_in_ln_b, m_rel_bias, m_b_in, m_attn_sinks, m_conv_dw, m_conv_dw_b, m_conv_ln_g, m_conv_ln_b,
            m_ln1_g, m_ln1_b, m_ffn_dw, m_ffn_dw_b, m_ln2_g, m_ln2_b]
    sm_v = [v_meta_tokens, v_in_ln_g, v_in_ln_b, v_rel_bias, v_b_in, v_attn_sinks, v_conv_dw, v_conv_dw_b, v_conv_ln_g, v_conv_ln_b,
            v_ln1_g, v_ln1_b, v_ffn_dw, v_ffn_dw_b, v_ln2_g, v_ln2_b]
    sm_g = [g_meta_s, g_inlg, g_inlb, g_relb, g_bin, g_sinks, g_cdw_s, g_cdwb, g_clg, g_clb, g_l1g, g_l1b, g_fdw_s, g_fdwb, g_l2g, g_l2b]
    sm_shapes = [tuple(a.shape) for a in sm_w]
    sd, smn, svn = _adamw("adamw_small", _pack(sm_w), _pack(sm_g), _pack(sm_m), _pack(sm_v))
    sd, smn, svn = _unpack(sd, sm_shapes), _unpack(smn, sm_shapes), _unpack(svn, sm_shapes)

    order = ["meta_tokens", "in_ln_g", "in_ln_b", "rel_bias", "w_in", "b_in", "attn_sinks", "w_attn_proj", "conv_dw", "conv_dw_b",
             "conv_ln_g", "conv_ln_b", "w_conv_proj", "w_out", "ln1_g", "ln1_b", "ffn_w_up", "ffn_dw", "ffn_dw_b", "ffn_w_down",
             "ln2_g", "ln2_b"]
    small_names = ["meta_tokens", "in_ln_g", "in_ln_b", "rel_bias", "b_in", "attn_sinks", "conv_dw", "conv_dw_b", "conv_ln_g",
                   "conv_ln_b", "ln1_g", "ln1_b", "ffn_dw", "ffn_dw_b", "ln2_g", "ln2_b"]
    big_names = ["w_in", "w_attn_proj", "w_conv_proj", "w_out", "ffn_w_up", "ffn_w_down"]
    res = {}
    for i, n in enumerate(small_names):
        res[n] = (sm_g[i], sd[i], smn[i], svn[i])
    for i, n in enumerate(big_names):
        res[n] = big_out[i]
    outs = [loss, grad_x]
    for k in range(4):
        outs += [res[n][k] for n in order]
    return tuple(outs)
```

```python
import functools
import math
from typing import Any, Callable, NamedTuple, Sequence

import numpy as np
import jax
import jax.numpy as jnp
from jax import lax
from jax.experimental import pallas as pl
from jax.experimental.pallas import tpu as pltpu

F32 = jnp.float32
BF16 = jnp.bfloat16
MESH = pl.DeviceIdType.MESH

D = 1024
N_META = 16
BLK = 128
PAD = BLK - N_META
HD = 64
NQ = 8
NKV = 2
GRP = NQ // NKV
AW = NQ * HD
KVW = NKV * HD
CW = D // 2
CTAPS = 31
FTAPS = 3
NBUCKET = 32
MAXDIST = 128
EPS = 1e-5
DEPTH = 2
ALPHA = (2.0 * DEPTH) ** 0.25
NCHIP = 4
NKEY = 3 * BLK
NEG = -1e30
CHALO = 32
FHALO = 8
IN_COLS = AW + 2 * KVW + 2 * CW + 2 * D
_OLD = dict(q=(0, AW), k=(AW, AW + KVW), v=(AW + KVW, AW + 2 * KVW), cv=(AW + 2 * KVW, AW + 2 * KVW + CW),
            cg=(AW + 2 * KVW + CW, AW + 2 * KVW + 2 * CW), ga=(AW + 2 * KVW + 2 * CW, AW + 2 * KVW + 2 * CW + D),
            gc=(AW + 2 * KVW + 2 * CW + D, IN_COLS))
_NEW_ORDER = ("ga", "gc", "cv", "cg", "q", "k", "v")
C_GATES, C_CONV, C_QKV = 0, 2 * D, 2 * D + 2 * CW

ADAM_LR, ADAM_B1, ADAM_B2, ADAM_EPS, ADAM_WD, ADAM_STEP = 0.001, 0.9, 0.999, 1e-08, 0.01, 10


def _to_new(a):
    return jnp.concatenate([a[..., _OLD[n][0]:_OLD[n][1]] for n in _NEW_ORDER], axis=-1)


def _to_old(a):
    offs, o = {}, 0
    for n in _NEW_ORDER:
        w = _OLD[n][1] - _OLD[n][0]
        offs[n] = (o, o + w)
        o += w
    return jnp.concatenate([a[..., offs[n][0]:offs[n][1]] for n in ("q", "k", "v", "cv", "cg", "ga", "gc")], axis=-1)


class _Job(NamedTuple):
    ins: Sequence[Any]
    outs: Sequence[Any]
    sems: tuple
    start: Callable
    finish: Callable


_ANY = pl.BlockSpec(memory_space=pl.ANY)


def _call(body, name, out_shape, grid, in_specs, out_specs, scratch=(), prefetch=0, side=None):
    params = pltpu.CompilerParams(dimension_semantics=("arbitrary",) * len(grid))
    if side is None:
        if prefetch:
            gs = pltpu.PrefetchScalarGridSpec(num_scalar_prefetch=prefetch, grid=grid, in_specs=in_specs,
                                              out_specs=out_specs, scratch_shapes=list(scratch))
            return pl.pallas_call(body, name=name, out_shape=out_shape, grid_spec=gs, compiler_params=params)
        return pl.pallas_call(body, name=name, out_shape=out_shape, grid=grid, in_specs=in_specs, out_specs=out_specs,
                              scratch_shapes=list(scratch), compiler_params=params)
    assert not prefetch
    single = not isinstance(out_shape, (tuple, list))
    main_shapes = (out_shape,) if single else tuple(out_shape)
    main_specs = (out_specs,) if single else tuple(out_specs)
    n_in, n_sin, n_out, n_sout, n_scr = len(in_specs), len(side.ins), len(main_shapes), len(side.outs), len(scratch)

    def wrapped(*refs):
        main_in, sin = refs[:n_in], refs[n_in:n_in + n_sin]
        o0 = n_in + n_sin
        main_out, sout = refs[o0:o0 + n_out], refs[o0 + n_out:o0 + n_out + n_sout]
        s0 = o0 + n_out + n_sout
        main_scr, (send, recv) = refs[s0:s0 + n_scr], refs[s0 + n_scr:]
        first = functools.reduce(lambda a, b: a & b, [pl.program_id(k) == 0 for k in range(len(grid))])
        last = functools.reduce(lambda a, b: a & b, [pl.program_id(k) == grid[k] - 1 for k in range(len(grid))])

        @pl.when(first)
        def _():
            side.start(sin, sout, send, recv)

        body(*main_in, *main_out, *main_scr)

        @pl.when(last)
        def _():
            side.finish(sin, sout, send, recv)

    call = pl.pallas_call(
        wrapped, name=name, out_shape=main_shapes + tuple(side.outs), grid=grid,
        in_specs=list(in_specs) + [_ANY] * n_sin, out_specs=main_specs + tuple([_ANY] * n_sout),
        scratch_shapes=list(scratch) + [pltpu.SemaphoreType.DMA(side.sems), pltpu.SemaphoreType.DMA(side.sems)],
        compiler_params=params)
    return lambda *args: call(*args, *side.ins)


def _row_tile(m):
    best = 32
    for t in range(32, 641, 32):
        if m % t == 0:
            best = t
    return best


def _pad_rows(rows, nex, tp):
    m = rows < PAD
    for b in range(1, nex):
        m = m | ((rows >= b * tp) & (rows < b * tp + PAD))
    return m


def _ln_stats(x):
    mu = jnp.mean(x, axis=-1, keepdims=True)
    xc = x - mu
    var = jnp.mean(xc * xc, axis=-1, keepdims=True)
    rstd = lax.rsqrt(var + EPS)
    return xc * rstd, rstd


def _ln_bwd(dy, xhat, rstd, g):
    dxh = dy * g
    m1 = jnp.mean(dxh, axis=-1, keepdims=True)
    m2 = jnp.mean(dxh * xhat, axis=-1, keepdims=True)
    return rstd * (dxh - m1 - xhat * m2)


def _dot(a, b):
    return jnp.dot(a, b, preferred_element_type=F32)


def _dot_nt(a, b):
    return lax.dot_general(a, b, (((1,), (1,)), ((), ())), preferred_element_type=F32)


def _dot_tn(a, b):
    return lax.dot_general(a, b, (((0,), (0,)), ((), ())), preferred_element_type=F32)


def _sigmoid(x):
    return 1.0 / (1.0 + jnp.exp(-x))


def _bucket_np(d):
    n = np.maximum(d, 0)
    max_exact = NBUCKET // 2
    nf = np.maximum(n, 1).astype(np.float32)
    large = max_exact + (np.log(nf / np.float32(max_exact)) / np.float32(math.log(MAXDIST / max_exact))
                         * np.float32(NBUCKET - max_exact)).astype(np.int32)
    large = np.minimum(large, NBUCKET - 1)
    return np.where(n < max_exact, n, large).astype(np.int32)


def _bias_index():
    i = np.arange(BLK)[:, None]
    j = np.arange(2 * BLK)[None, :]
    d = BLK + i - j
    band_ok = (d >= 0) & (d < BLK)
    band = _bucket_np(d)
    idx = np.full((3, BLK, NKEY), -1, np.int32)
    m = np.arange(N_META)[None, :]
    d0 = (i - PAD) - m
    idx[0, :, 2 * BLK:2 * BLK + N_META] = np.where(d0 >= 0, _bucket_np(d0), -1)
    ok1 = band_ok & (j >= BLK)
    idx[1, :, :2 * BLK] = np.where(ok1, band, -1)
    idx[1, :, 2 * BLK:2 * BLK + N_META] = _bucket_np((N_META + i) - m)
    idx[2, :, :2 * BLK] = np.where(band_ok, band, -1)
    idx[2, :, 2 * BLK:2 * BLK + N_META] = NBUCKET - 1
    return idx


def _bias_build(rel_bias):
    idx = jnp.asarray(_bias_index())

    def body(idx_ref, rb_ref, o_ref):
        ix = idx_ref[...]
        for h in range(NQ):
            acc = jnp.full(ix.shape, NEG, F32)
            for b in range(NBUCKET):
                acc = jnp.where(ix == b, rb_ref[b, h], acc)
            o_ref[:, h, :, :] = acc

    return pl.pallas_call(
        body, name="bias_build", out_shape=jax.ShapeDtypeStruct((3, NQ, BLK, NKEY), F32),
        in_specs=[pl.BlockSpec(memory_space=pltpu.VMEM), pl.BlockSpec(memory_space=pltpu.SMEM)],
        out_specs=pl.BlockSpec(memory_space=pltpu.VMEM))(idx, rel_bias)


def _bias_grad(dbias):
    idx = jnp.asarray(_bias_index())

    def body(idx_ref, d_ref, o_ref):
        d = jnp.sum(d_ref[...], axis=0)
        for b in range(NBUCKET):
            acc = jnp.zeros((NQ, NKEY), F32)
            for case in range(3):
                hit = (idx_ref[case] == b)[None, :, :]
                acc = acc + jnp.sum(jnp.where(hit, d[case], 0.0), axis=1)
            o_ref[b] = jnp.sum(acc, axis=-1, keepdims=True)

    out = pl.pallas_call(
        body, name="bias_grad", out_shape=jax.ShapeDtypeStruct((NBUCKET, NQ, 1), F32),
        in_specs=[pl.BlockSpec(memory_space=pltpu.VMEM), pl.BlockSpec(memory_space=pltpu.VMEM)],
        out_specs=pl.BlockSpec(memory_space=pltpu.VMEM))(idx, dbias)
    return out.reshape(NBUCKET, NQ)


def _embed_ln(x, meta, g, b, nblk, side=None):
    nex, seq, _ = x.shape
    m = nex * nblk * BLK

    def body(x_ref, meta_ref, g_ref, b_ref, raw_ref, h_ref, hb_ref):
        j = pl.program_id(1)

        @pl.when(j == 0)
        def _():
            raw_ref[0:PAD, :] = jnp.zeros((PAD, D), F32)
            raw_ref[PAD:BLK, :] = meta_ref[...]

        @pl.when(j > 0)
        def _():
            raw_ref[...] = x_ref[...]

        xhat, _ = _ln_stats(raw_ref[...])
        y = xhat * g_ref[...] + b_ref[...]
        h_ref[...] = y
        hb_ref[...] = y.astype(BF16)

    row = lambda bb, j: (bb * nblk + j, 0)
    return _call(
        body, "embed_ln",
        (jax.ShapeDtypeStruct((m, D), F32), jax.ShapeDtypeStruct((m, D), F32), jax.ShapeDtypeStruct((m, D), BF16)),
        (nex, nblk),
        [pl.BlockSpec((None, BLK, D), lambda bb, j: (bb, jnp.maximum(j - 1, 0), 0)),
         pl.BlockSpec((N_META, D), lambda bb, j: (0, 0)),
         pl.BlockSpec((1, D), lambda bb, j: (0, 0)), pl.BlockSpec((1, D), lambda bb, j: (0, 0))],
        (pl.BlockSpec((BLK, D), row), pl.BlockSpec((BLK, D), row), pl.BlockSpec((BLK, D), row)), side=side,
    )(x, meta, g.reshape(1, D), b.reshape(1, D))


def _mm_bias(name, a, w, bias, tn, tm, side=None):
    m, k = a.shape
    n = w.shape[1]

    def body(a_ref, w_ref, b_ref, o_ref):
        o_ref[...] = _dot(a_ref[...], w_ref[...]) + b_ref[...]

    return _call(body, name, jax.ShapeDtypeStruct((m, n), F32), (n // tn, m // tm),
                 [pl.BlockSpec((tm, k), lambda j, i: (i, 0)), pl.BlockSpec((k, tn), lambda j, i: (0, j)),
                  pl.BlockSpec((1, tn), lambda j, i: (0, j))],
                 pl.BlockSpec((tm, tn), lambda j, i: (i, j)), side=side)(a, w, bias)


def _ffn_up_act(a, w4, cw, cb, tm, nex, tp, side=None):
    m, k = a.shape
    ffs = w4.shape[2]

    def body(a_ref, wu_ref, wg_ref, cu_ref, cg_ref, bu_ref, bg_ref, up_ref, ug_ref, act_ref, win):
        i = pl.program_id(1)

        @pl.when(i == 0)
        def _():
            win[:, 0:FHALO, :] = jnp.zeros((2, FHALO, ffs), F32)

        rows = i * tm + lax.broadcasted_iota(jnp.int32, (tm, 1), 0)
        pad = _pad_rows(rows, nex, tp)
        av = a_ref[...]
        for p, w_ref in ((0, wu_ref), (1, wg_ref)):
            x = jnp.where(pad, 0.0, _dot(av, w_ref[...]))
            win[p, FHALO:FHALO + tm, :] = x
            up_ref[p] = x.astype(BF16)
        for r0 in range(0, tm, RCH):
            for c0, c1 in _lane_groups(ffs):
                u = _conv3(win, 0, r0, c0, c1, cu_ref, bu_ref)
                g = _conv3(win, 1, r0, c0, c1, cg_ref, bg_ref)
                ug_ref[0, r0:r0 + RCH, c0:c1] = u.astype(BF16)
                ug_ref[1, r0:r0 + RCH, c0:c1] = g.astype(BF16)
                act_ref[r0:r0 + RCH, c0:c1] = (g * (0.5 * (1.0 + lax.erf(g * (1.0 / math.sqrt(2.0))))) * u).astype(BF16)
        win[:, 0:FHALO, :] = win[:, tm:tm + FHALO, :]

    wide = jax.ShapeDtypeStruct((2, m, 2 * ffs), BF16)
    return _call(body, "ffn_up_act", (wide, wide, jax.ShapeDtypeStruct((m, 2 * ffs), BF16)),
                 (2, m // tm),
                 [pl.BlockSpec((tm, k), lambda c, i: (i, 0)),
                  pl.BlockSpec((None, k, ffs), lambda c, i: (c, 0, 0)), pl.BlockSpec((None, k, ffs), lambda c, i: (c + 2, 0, 0)),
                  pl.BlockSpec((FTAPS, ffs), lambda c, i: (0, c)), pl.BlockSpec((FTAPS, ffs), lambda c, i: (0, c + 2)),
                  pl.BlockSpec((1, ffs), lambda c, i: (0, c)), pl.BlockSpec((1, ffs), lambda c, i: (0, c + 2))],
                 (pl.BlockSpec((2, tm, ffs), lambda c, i: (0, i, c)), pl.BlockSpec((2, tm, ffs), lambda c, i: (0, i, c)),
                  pl.BlockSpec((tm, ffs), lambda c, i: (i, c))),
                 scratch=[pltpu.VMEM((2, FHALO + tm, ffs), F32)], side=side)(a, w4, w4, cw, cw, cb, cb)


def _fill_kv(ks, vs, e, prev_ref, cur_ref, meta_ref):
    for piece, lo, n in ((prev_ref, 0, BLK), (cur_ref, BLK, BLK), (meta_ref, 2 * BLK, N_META)):
        val = piece[e]
        for hk in range(NKV):
            ks[e, hk, lo:lo + n, :] = val[:, hk * HD:(hk + 1) * HD].astype(BF16)
            vs[e, hk, lo:lo + n, :] = val[:, KVW + hk * HD:KVW + (hk + 1) * HD].astype(BF16)
    for hk in range(NKV):
        ks[e, hk, 2 * BLK + N_META:NKEY, :] = jnp.zeros((BLK - N_META, HD), BF16)
        vs[e, hk, 2 * BLK + N_META:NKEY, :] = jnp.zeros((BLK - N_META, HD), BF16)


def _interleave(chains):
    live = list(chains)
    while live:
        for c in list(live):
            try:
                next(c)
            except StopIteration:
                live.remove(c)


def _head_softmax(q, ks_hk, bias_ref, sink_ref, h, out):
    qh = q[:, h * HD:(h + 1) * HD].astype(BF16)
    yield
    s = _dot_nt(qh, ks_hk) * (HD ** -0.5) + bias_ref[h]
    yield
    sink = sink_ref[0, h]
    mx = jnp.maximum(jnp.max(s, axis=-1, keepdims=True), sink)
    yield
    p = jnp.exp(s - mx)
    es = jnp.exp(sink - mx)
    yield
    inv = 1.0 / (jnp.sum(p, axis=-1, keepdims=True) + es)
    yield
    out["q"], out["p"], out["sink"] = qh, p * inv, es * inv
    yield


def _attn_specs(nex, blk_of):
    qcol, kvcol = (C_QKV) // AW, (C_QKV + AW) // (2 * KVW)
    return [
        pl.BlockSpec((nex, BLK, AW), lambda j: (0, blk_of(j), qcol)),
        pl.BlockSpec((nex, BLK, 2 * KVW), lambda j: (0, blk_of(j), kvcol)),
        pl.BlockSpec((nex, BLK, 2 * KVW), lambda j: (0, jnp.maximum(blk_of(j) - 1, 0), kvcol)),
        pl.BlockSpec((nex, N_META, 2 * KVW), lambda j: (0, PAD // N_META, kvcol)),
        pl.BlockSpec((None, NQ, BLK, NKEY), lambda j: (jnp.minimum(blk_of(j), 2), 0, 0, 0)),
        pl.BlockSpec(memory_space=pltpu.SMEM),
    ]


def _attn_fwd(z, bias, sinks, nex, nblk, side=None):
    m = z.shape[0]
    z3 = z.reshape(nex, nblk * BLK, z.shape[1])

    def head(e, h, q, ks, vs, bias_ref, sink_ref, oacc):
        out = {}
        yield from _head_softmax(q, ks[e, h // GRP], bias_ref, sink_ref, h, out)
        o = _dot(out["p"].astype(BF16), vs[e, h // GRP])
        yield
        oacc[e, :, h * HD:(h + 1) * HD] = o
        yield

    def body(q_ref, cur_ref, prev_ref, meta_ref, bias_ref, sink_ref, o_ref, ks, vs, oacc):
        for e in range(nex):
            _fill_kv(ks, vs, e, prev_ref, cur_ref, meta_ref)
            q = q_ref[e]
            _interleave([head(e, h, q, ks, vs, bias_ref, sink_ref, oacc) for h in range(NQ)])
            o_ref[e] = oacc[e].astype(BF16)

    res = _call(body, "attn_fwd", jax.ShapeDtypeStruct((nex, nblk * BLK, AW), BF16), (nblk,),
                _attn_specs(nex, lambda j: j),
                pl.BlockSpec((nex, BLK, AW), lambda j: (0, j, 0)),
                scratch=[pltpu.VMEM((nex, NKV, NKEY, HD), BF16), pltpu.VMEM((nex, NKV, NKEY, HD), BF16),
                         pltpu.VMEM((nex, BLK, AW), F32)], side=side)(z3, z3, z3, z3, bias, sinks.reshape(1, NQ))
    if side is None:
        return res.reshape(m, AW)
    return (res[0].reshape(m, AW),) + tuple(res[1:])


def _cgate(cv, cg, rows, nex, tp):
    return jnp.where(_pad_rows(rows, nex, tp), 0.0, cv * _sigmoid(cg))


CLANES = 256


def _rolled_up(blk, b):
    return blk if b == 0 else pltpu.roll(blk, blk.shape[0] - b, axis=0)


def _conv_fwd(z, w, wb, g, b, nex, tp, side=None):
    m = z.shape[0]
    tm = BLK
    sub = tm // CHALO
    cvc, cgc = C_CONV // CW, C_CONV // CW + 1

    def body(cv_ref, cg_ref, cvh_ref, cgh_ref, w_ref, wb_ref, g_ref, b_ref, cc_ref, cs_ref, win):
        i = pl.program_id(0)
        rows = i * tm + lax.broadcasted_iota(jnp.int32, (tm, 1), 0)
        hrows = i * tm - CHALO + lax.broadcasted_iota(jnp.int32, (CHALO, 1), 0)
        win[0:CHALO, :] = _cgate(cvh_ref[...], cgh_ref[...], hrows, nex, tp)
        win[CHALO:CHALO + tm, :] = _cgate(cv_ref[...], cg_ref[...], rows, nex, tp)
        for sb in range(sub):
            lo = sb * CHALO
            for c0 in range(0, CW, CLANES):
                blk = win[lo:lo + 2 * CHALO, c0:c0 + CLANES]
                acc = jnp.zeros((CHALO, CLANES), F32) + wb_ref[:, c0:c0 + CLANES]
                for b in range(8):
                    rb = _rolled_up(blk, b)
                    for a in range(5):
                        s = 8 * a + b
                        if 2 <= s <= CTAPS + 1:
                            acc = acc + w_ref[s - 2:s - 1, c0:c0 + CLANES] * rb[8 * a:8 * a + CHALO]
                cc_ref[lo:lo + CHALO, c0:c0 + CLANES] = acc
        xhat, _ = _ln_stats(cc_ref[...])
        cl = xhat * g_ref[...] + b_ref[...]
        cs_ref[...] = (cl * _sigmoid(cl)).astype(BF16)

    halo = lambda i: jnp.maximum(i * sub - 1, 0)
    vec = pl.BlockSpec((1, CW), lambda i: (0, 0))
    return _call(body, "conv_fwd", (jax.ShapeDtypeStruct((m, CW), F32), jax.ShapeDtypeStruct((m, CW), BF16)),
                 (m // tm,),
                 [pl.BlockSpec((tm, CW), lambda i: (i, cvc)), pl.BlockSpec((tm, CW), lambda i: (i, cgc)),
                  pl.BlockSpec((CHALO, CW), lambda i: (halo(i), cvc)), pl.BlockSpec((CHALO, CW), lambda i: (halo(i), cgc)),
                  pl.BlockSpec((CTAPS, CW), lambda i: (0, 0)), vec, vec, vec],
                 (pl.BlockSpec((tm, CW), lambda i: (i, 0)), pl.BlockSpec((tm, CW), lambda i: (i, 0))),
                 scratch=[pltpu.VMEM((CHALO + tm, CW), F32)], side=side)(z, z, z, z, w, wb.reshape(1, CW), g.reshape(1, CW), b.reshape(1, CW))


def _mix_fwd(a, cs, wap4, wcp4, z, tm, side=None):
    m = a.shape[0]
    ns = wap4.shape[2]

    def body(a_ref, cs_ref, wa_ref, wc_ref, ga_ref, gc_ref, ya_ref, yc_ref, mix_ref):
        av, cv = a_ref[...], cs_ref[...]
        for j in range(NCHIP):
            cols = slice(j * ns, (j + 1) * ns)
            ya, yc = _dot(av, wa_ref[j]), _dot(cv, wc_ref[j])
            ya_ref[:, cols] = ya.astype(BF16)
            yc_ref[:, cols] = yc.astype(BF16)
            mix_ref[:, cols] = (_sigmoid(ga_ref[:, cols]) * ya + _sigmoid(gc_ref[:, cols]) * yc).astype(BF16)

    wspec = pl.BlockSpec((NCHIP, AW, ns), lambda i: (0, 0, 0))
    row = lambda i: (i, 0)
    return _call(body, "mix_fwd",
                 (jax.ShapeDtypeStruct((m, D), BF16), jax.ShapeDtypeStruct((m, D), BF16), jax.ShapeDtypeStruct((m, D), BF16)),
                 (m // tm,),
                 [pl.BlockSpec((tm, AW), row), pl.BlockSpec((tm, CW), row), wspec, wspec,
                  pl.BlockSpec((tm, D), lambda i: (i, 0)), pl.BlockSpec((tm, D), lambda i: (i, 1))],
                 (pl.BlockSpec((tm, D), row), pl.BlockSpec((tm, D), row), pl.BlockSpec((tm, D), row)), side=side)(a, cs, wap4, wcp4, z, z)


def _mm_res_ln(name, a, w, res, g, b, tm, side=None):
    m, k = a.shape

    def body(a_ref, w_ref, res_ref, g_ref, b_ref, r_ref, h_ref, hb_ref):
        r = ALPHA * res_ref[...] + _dot(a_ref[...], w_ref[...])
        r_ref[...] = r
        xhat, _ = _ln_stats(r)
        y = xhat * g_ref[...] + b_ref[...]
        h_ref[...] = y
        hb_ref[...] = y.astype(BF16)

    row = lambda i: (i, 0)
    vec = pl.BlockSpec((1, D), lambda i: (0, 0))
    return _call(body, name,
                 (jax.ShapeDtypeStruct((m, D), F32), jax.ShapeDtypeStruct((m, D), F32), jax.ShapeDtypeStruct((m, D), BF16)),
                 (m // tm,),
                 [pl.BlockSpec((tm, k), row), pl.BlockSpec((k, D), lambda i: (0, 0)), pl.BlockSpec((tm, D), row), vec, vec],
                 (pl.BlockSpec((tm, D), row), pl.BlockSpec((tm, D), row), pl.BlockSpec((tm, D), row)), side=side)(a, w, res, g.reshape(1, D), b.reshape(1, D))


RCH = 16


def _lane_groups(width, most=768):
    n = -(-width // most)
    step = -(-width // (128 * n)) * 128
    return [(c, min(c + step, width)) for c in range(0, width, step)]


def _conv3(win, p, r0, c0, c1, w_ref, b_ref):
    blk = win[p, r0:r0 + FHALO + RCH, c0:c1]
    x0, x1, x2 = blk[FHALO:], pltpu.roll(blk, 1, axis=0)[FHALO:], pltpu.roll(blk, 2, axis=0)[FHALO:]
    return b_ref[:, c0:c1] + w_ref[0:1, c0:c1] * x2 + w_ref[1:2, c0:c1] * x1 + w_ref[2:3, c0:c1] * x0


def _loss_grad(y, target, nblk):
    nex = target.shape[0]
    m = y.shape[0]

    def body(y_ref, t_ref, dy_ref, acc_ref):
        bb, j = pl.program_id(0), pl.program_id(1)

        @pl.when((bb == 0) & (j == 0))
        def _():
            acc_ref[...] = jnp.zeros_like(acc_ref)

        @pl.when(j == 0)
        def _():
            dy_ref[...] = jnp.zeros_like(dy_ref)

        @pl.when(j > 0)
        def _():
            e = y_ref[...] - t_ref[...]
            dy_ref[...] = e * (1.0 / D)
            acc_ref[...] += jnp.sum((e * e).reshape(BLK // 8, 8, D), axis=0)

    return _call(body, "loss_grad", (jax.ShapeDtypeStruct((m, D), F32), jax.ShapeDtypeStruct((8, D), F32)), (nex, nblk),
                 [pl.BlockSpec((BLK, D), lambda bb, j: (bb * nblk + j, 0)),
                  pl.BlockSpec((None, BLK, D), lambda bb, j: (bb, jnp.maximum(j - 1, 0), 0))],
                 (pl.BlockSpec((BLK, D), lambda bb, j: (bb * nblk + j, 0)), pl.BlockSpec((8, D), lambda bb, j: (0, 0))))(y, target)


def _ln_bwd_call(name, dy, r, g, tm, a_list=(), w=None, cols=None, side=None):
    m = dy.shape[0]
    na = len(a_list)

    def body(*refs):
        dy_ref, r_ref, g_ref = refs[0:3]
        a_refs = refs[3:3 + na]
        w_ref = refs[3 + na] if na else None
        dr_ref, drb_ref, dg_ref, db_ref = refs[-4:]
        i = pl.program_id(0)
        dh = dy_ref[...]
        if na:
            dh = ALPHA * dh
            if cols is None:
                ns = w.shape[2]
                for j in range(NCHIP):
                    dh = dh + _dot_nt(a_refs[0][j // 2, :, (j % 2) * ns:(j % 2 + 1) * ns], w_ref[j])
            else:
                for a_ref, (c0, c1) in zip(a_refs, cols):
                    dh = dh + _dot_nt(a_ref[...], w_ref[:, c0:c1])
        xhat, rstd = _ln_stats(r_ref[...])
        dr = _ln_bwd(dh, xhat, rstd, g_ref[...])
        dr_ref[...] = dr
        drb_ref[...] = dr.astype(BF16)

        @pl.when(i == 0)
        def _():
            dg_ref[...] = jnp.zeros_like(dg_ref)
            db_ref[...] = jnp.zeros_like(db_ref)

        dg_ref[...] += jnp.sum(dh * xhat, axis=0, keepdims=True)
        db_ref[...] += jnp.sum(dh, axis=0, keepdims=True)

    row = lambda i: (i, 0)
    vec = pl.BlockSpec((1, D), lambda i: (0, 0))
    in_specs = [pl.BlockSpec((tm, D), row), pl.BlockSpec((tm, D), row), vec]
    for a in a_list:
        in_specs.append(pl.BlockSpec((2, tm, a.shape[2]), lambda i: (0, i, 0)) if a.ndim == 3 else pl.BlockSpec((tm, a.shape[1]), row))
    if na:
        in_specs.append(pl.BlockSpec(w.shape, (lambda i: (0, 0, 0)) if w.ndim == 3 else (lambda i: (0, 0))))
    return _call(body, name,
                 (jax.ShapeDtypeStruct((m, D), F32), jax.ShapeDtypeStruct((m, D), BF16),
                  jax.ShapeDtypeStruct((1, D), F32), jax.ShapeDtypeStruct((1, D), F32)),
                 (m // tm,), in_specs,
                 (pl.BlockSpec((tm, D), row), pl.BlockSpec((tm, D), row), vec, vec),
                 side=side)(dy, r, g.reshape(1, D), *a_list, *([w] if na else []))


def _ffn_bwd(drb, wdown, ug, up3, w, tm, nex, tp, side=None):
    _, m, dff = ug.shape
    ffs = dff // 2
    nt = m // tm

    def body(dr_ref, wd_ref, ug_ref, x_ref, wu_ref, wg_ref, o_ref, dw_ref, db_ref, dact, carry, dwacc, dbacc):
        i = pl.program_id(1)
        tile = nt - 1 - i
        dact[...] = _dot_nt(dr_ref[...], wd_ref[...])
        dwacc[...] = jnp.zeros_like(dwacc)
        dbacc[...] = jnp.zeros_like(dbacc)

        @pl.when(i == 0)
        def _():
            carry[...] = jnp.zeros_like(carry)
            dw_ref[...] = jnp.zeros_like(dw_ref)
            db_ref[...] = jnp.zeros_like(db_ref)

        fold = lambda t: t[0:8, :] + t[8:16, :]
        for r0 in reversed(range(0, tm, RCH)):
            rows = tile * tm + r0 + lax.broadcasted_iota(jnp.int32, (RCH, 1), 0)
            pad = _pad_rows(rows, nex, tp)
            for c0, c1 in _lane_groups(ffs, 384):
                u = ug_ref[0, r0:r0 + RCH, c0:c1].astype(F32)
                g = ug_ref[1, r0:r0 + RCH, c0:c1].astype(F32)
                da = dact[r0:r0 + RCH, c0:c1]
                cdf = 0.5 * (1.0 + lax.erf(g * (1.0 / math.sqrt(2.0))))
                pdf = jnp.exp(-0.5 * g * g) * (1.0 / math.sqrt(2.0 * math.pi))
                for p, w_ref, d0 in ((0, wu_ref, da * (g * cdf)), (1, wg_ref, da * u * (cdf + g * pdf))):
                    dblk = jnp.concatenate([d0, carry[p, :, c0:c1]], axis=0)
                    d1 = pltpu.roll(dblk, RCH + FHALO - 1, axis=0)[:RCH]
                    d2 = pltpu.roll(dblk, RCH + FHALO - 2, axis=0)[:RCH]
                    carry[p, :, c0:c1] = d0[0:FHALO]
                    dpre = w_ref[2:3, c0:c1] * d0 + w_ref[1:2, c0:c1] * d1 + w_ref[0:1, c0:c1] * d2
                    o_ref[p, r0:r0 + RCH, c0:c1] = jnp.where(pad, 0.0, dpre).astype(BF16)
                    x0 = x_ref[p, r0:r0 + RCH, c0:c1].astype(F32)
                    dwacc[p, 2, :, c0:c1] += fold(d0 * x0)
                    dwacc[p, 1, :, c0:c1] += fold(d1 * x0)
                    dwacc[p, 0, :, c0:c1] += fold(d2 * x0)
                    dbacc[p, :, c0:c1] += fold(d0)
        for p in range(2):
            for k in range(FTAPS):
                dw_ref[p, k:k + 1, :] += jnp.sum(dwacc[p, k], axis=0, keepdims=True)
            db_ref[p] += jnp.sum(dbacc[p], axis=0, keepdims=True)

    wide = pl.BlockSpec((2, tm, ffs), lambda c, i: (0, nt - 1 - i, c))
    return _call(body, "ffn_bwd",
                 (jax.ShapeDtypeStruct((2, m, dff), BF16), jax.ShapeDtypeStruct((2, FTAPS, dff), F32),
                  jax.ShapeDtypeStruct((2, 1, dff), F32)),
                 (2, nt),
                 [pl.BlockSpec((tm, D), lambda c, i: (nt - 1 - i, 0)), pl.BlockSpec((ffs, D), lambda c, i: (c, 0)), wide, wide,
                  pl.BlockSpec((FTAPS, ffs), lambda c, i: (0, c)), pl.BlockSpec((FTAPS, ffs), lambda c, i: (0, c + 2))],
                 (wide, pl.BlockSpec((2, FTAPS, ffs), lambda c, i: (0, 0, c)), pl.BlockSpec((2, 1, ffs), lambda c, i: (0, 0, c))),
                 scratch=[pltpu.VMEM((tm, ffs), F32), pltpu.VMEM((2, FHALO, ffs), F32),
                          pltpu.VMEM((2, FTAPS, 8, ffs), F32), pltpu.VMEM((2, 8, ffs), F32)], side=side)(drb, wdown, ug, up3, w, w)


def _mm_tn(name, a, b, tk, tn, b_cols=None, chip_out=False, side=None):
    m, k = a.shape
    n = b.shape[-1] * (2 if b.ndim == 3 else 1)

    def body(a_ref, b_ref, o_ref):
        o_ref[...] = _dot_tn(a_ref[...], b_ref[...])

    if b.ndim == 3:
        bspec = pl.BlockSpec((None, m, tn), lambda kk, j: (b_cols(j)[0], 0, b_cols(j)[1]))
    else:
        bspec = pl.BlockSpec((m, tn), lambda kk, j: (0, j))
    if chip_out:
        oshape, ospec = (n // tn, k, tn), pl.BlockSpec((None, tk, tn), lambda kk, j: (j, kk, 0))
    else:
        oshape, ospec = (k, n), pl.BlockSpec((tk, tn), lambda kk, j: (kk, j))
    return _call(body, name, jax.ShapeDtypeStruct(oshape, F32), (k // tk, n // tn),
                 [pl.BlockSpec((m, tk), lambda kk, j: (0, kk)), bspec], ospec, side=side)(a, b)


def _gate_bwd(drb, wout, ya, yc, z, tm):
    m = drb.shape[0]

    def body(dr_ref, w_ref, ya_ref, yc_ref, ga_ref, gc_ref, dya_ref, dyc_ref, dz_ref, cs_ref):
        i = pl.program_id(0)
        dmix = _dot_nt(dr_ref[...], w_ref[...])
        sa, sc = _sigmoid(ga_ref[...]), _sigmoid(gc_ref[...])
        dya_ref[...] = (dmix * sa).astype(BF16)
        dyc_ref[...] = (dmix * sc).astype(BF16)
        dga = dmix * ya_ref[...].astype(F32) * sa * (1.0 - sa)
        dgc = dmix * yc_ref[...].astype(F32) * sc * (1.0 - sc)
        dz_ref[:, 0:D] = dga.astype(BF16)
        dz_ref[:, D:2 * D] = dgc.astype(BF16)

        @pl.when(i == 0)
        def _():
            cs_ref[...] = jnp.zeros_like(cs_ref)

        cs_ref[:, 0:D] += jnp.sum(dga, axis=0, keepdims=True)
        cs_ref[:, D:2 * D] += jnp.sum(dgc, axis=0, keepdims=True)

    row = lambda i: (i, 0)
    return _call(body, "gate_bwd",
                 (jax.ShapeDtypeStruct((m, D), BF16), jax.ShapeDtypeStruct((m, D), BF16),
                  jax.ShapeDtypeStruct((m, 2 * D), BF16), jax.ShapeDtypeStruct((1, 2 * D), F32)),
                 (m // tm,),
                 [pl.BlockSpec((tm, D), row), pl.BlockSpec((D, D), lambda i: (0, 0)), pl.BlockSpec((tm, D), row),
                  pl.BlockSpec((tm, D), row), pl.BlockSpec((tm, D), lambda i: (i, 0)), pl.BlockSpec((tm, D), lambda i: (i, 1))],
                 (pl.BlockSpec((tm, D), row), pl.BlockSpec((tm, D), row), pl.BlockSpec((tm, 2 * D), row),
                  pl.BlockSpec((1, 2 * D), lambda i: (0, 0))))(drb, wout, ya, yc, z, z)


def _conv_bwd_a(dyc, wcp4, cc, g, b, tm):
    m = cc.shape[0]
    ns = wcp4.shape[2]

    def body(dy_ref, w_ref, cc_ref, g_ref, b_ref, dcc_ref, dg_ref, db_ref, dwb_ref):
        i = pl.program_id(0)
        dcs = jnp.zeros((tm, CW), F32)
        for j in range(NCHIP):
            dcs = dcs + _dot_nt(dy_ref[:, j * ns:(j + 1) * ns], w_ref[j])
        xhat, rstd = _ln_stats(cc_ref[...])
        cl = xhat * g_ref[...] + b_ref[...]
        sg = _sigmoid(cl)
        dcl = dcs * sg * (1.0 + cl * (1.0 - sg))
        dcc = _ln_bwd(dcl, xhat, rstd, g_ref[...])
        dcc_ref[...] = dcc

        @pl.when(i == 0)
        def _():
            dg_ref[...] = jnp.zeros_like(dg_ref)
            db_ref[...] = jnp.zeros_like(db_ref)
            dwb_ref[...] = jnp.zeros_like(dwb_ref)

        dg_ref[...] += jnp.sum(dcl * xhat, axis=0, keepdims=True)
        db_ref[...] += jnp.sum(dcl, axis=0, keepdims=True)
        dwb_ref[...] += jnp.sum(dcc, axis=0, keepdims=True)

    row = lambda i: (i, 0)
    vec = pl.BlockSpec((1, CW), lambda i: (0, 0))
    v = jax.ShapeDtypeStruct((1, CW), F32)
    return _call(body, "conv_bwd_a", (jax.ShapeDtypeStruct((m, CW), F32), v, v, v), (m // tm,),
                 [pl.BlockSpec((tm, D), row), pl.BlockSpec((NCHIP, CW, ns), lambda i: (0, 0, 0)), pl.BlockSpec((tm, CW), row), vec, vec],
                 (pl.BlockSpec((tm, CW), row), vec, vec, vec))(dyc, wcp4, cc, g.reshape(1, CW), b.reshape(1, CW))


def _conv_bwd_b(dcc, z, w, nex, tp, side=None):
    m = dcc.shape[0]
    tm = BLK
    sub = tm // CHALO
    nt = m // tm
    cvc, cgc = C_CONV // CW, C_CONV // CW + 1

    def body(d_ref, dh_ref, cv_ref, cg_ref, w_ref, dz_ref, dw_ref, cs_ref, dwin, dwacc):
        i = pl.program_id(0)
        rows = i * tm + lax.broadcasted_iota(jnp.int32, (tm, 1), 0)
        dwin[0:tm, :] = d_ref[...]
        dwin[tm:tm + CHALO, :] = jnp.where(i == nt - 1, 0.0, dh_ref[...])

        @pl.when(i == 0)
        def _():
            dwacc[...] = jnp.zeros_like(dwacc)
            cs_ref[...] = jnp.zeros_like(cs_ref)

        fold = lambda t: (t[0:8] + t[8:16]) + (t[16:24] + t[24:32])
        for sb in range(sub):
            lo = sb * CHALO
            pad = _pad_rows(rows[lo:lo + CHALO], nex, tp)
            for c0 in range(0, CW, CLANES):
                cs_ = slice(c0, c0 + CLANES)
                cv = cv_ref[lo:lo + CHALO, cs_]
                sg = _sigmoid(cg_ref[lo:lo + CHALO, cs_])
                cgin = jnp.where(pad, 0.0, cv * sg)
                blk = dwin[lo:lo + 2 * CHALO, cs_]
                acc = jnp.zeros((CHALO, CLANES), F32)
                for b in range(8):
                    rb = _rolled_up(blk, b)
                    for a in range(4):
                        s = 8 * a + b
                        if s <= CTAPS - 1:
                            k = CTAPS - 1 - s
                            sh = rb[8 * a:8 * a + CHALO]
                            acc = acc + w_ref[k:k + 1, cs_] * sh
                            dwacc[k, :, cs_] += fold(sh * cgin)
                dcg = jnp.where(pad, 0.0, acc)
                dcv = dcg * sg
                dgt = dcg * cv * sg * (1.0 - sg)
                dz_ref[lo:lo + CHALO, cs_] = dcv.astype(BF16)
                dz_ref[lo:lo + CHALO, CW + c0:CW + c0 + CLANES] = dgt.astype(BF16)
                cs_ref[:, cs_] += jnp.sum(dcv, axis=0, keepdims=True)
                cs_ref[:, CW + c0:CW + c0 + CLANES] += jnp.sum(dgt, axis=0, keepdims=True)

        @pl.when(i == nt - 1)
        def _():
            for k in range(CHALO):
                dw_ref[k:k + 1, :] = jnp.sum(dwacc[k], axis=0, keepdims=True)

    nxt = lambda i: jnp.minimum((i + 1) * sub, m // CHALO - 1)
    return _call(body, "conv_bwd_b",
                 (jax.ShapeDtypeStruct((m, 2 * CW), BF16), jax.ShapeDtypeStruct((CHALO, CW), F32),
                  jax.ShapeDtypeStruct((1, 2 * CW), F32)),
                 (nt,),
                 [pl.BlockSpec((tm, CW), lambda i: (i, 0)), pl.BlockSpec((CHALO, CW), lambda i: (nxt(i), 0)),
                  pl.BlockSpec((tm, CW), lambda i: (i, cvc)), pl.BlockSpec((tm, CW), lambda i: (i, cgc)),
                  pl.BlockSpec((CTAPS, CW), lambda i: (0, 0))],
                 (pl.BlockSpec((tm, 2 * CW), lambda i: (i, 0)), pl.BlockSpec((CHALO, CW), lambda i: (0, 0)),
                  pl.BlockSpec((1, 2 * CW), lambda i: (0, 0))),
                 scratch=[pltpu.VMEM((tm + CHALO, CW), F32), pltpu.VMEM((CHALO, 8, CW), F32)], side=side)(dcc, dcc, z, z, w)


def _attn_bwd(z, bias, sinks, dya, wap4, nex, nblk, side=None):
    m = z.shape[0]
    tp = nblk * BLK
    ns = wap4.shape[2]
    blk_of = lambda j: nblk - 1 - j
    z3 = z.reshape(nex, tp, z.shape[1])
    dya3 = dya.reshape(nex, tp, D)

    def body(q_ref, cur_ref, prev_ref, meta_ref, bias_ref, sink_ref, dy_ref, w_ref,
             dz_ref, cs_ref, dsk_ref, dbias_ref, ks, vs, carry, macc, dqacc, dkv, okv):
        j = pl.program_id(0)
        n = nblk - 1 - j

        @pl.when(j == 0)
        def _():
            carry[...] = jnp.zeros_like(carry)
            macc[...] = jnp.zeros_like(macc)
            cs_ref[...] = jnp.zeros_like(cs_ref)
            dsk_ref[...] = jnp.zeros_like(dsk_ref)

        @pl.when((j == 0) | (n <= 1))
        def _():
            dbias_ref[...] = jnp.zeros_like(dbias_ref)

        lane = lax.broadcasted_iota(jnp.int32, (1, BLK), 1)
        dsk = jnp.zeros((1, BLK), F32)

        def head(e, h, q, da, out):
            hk = h // GRP
            yield from _head_softmax(q, ks[e, hk], bias_ref, sink_ref, h, out)
            pn = out["p"]
            doh = da[:, h * HD:(h + 1) * HD].astype(BF16)
            yield
            dp = _dot_nt(doh, vs[e, hk])
            yield
            dl = jnp.sum(pn * dp, axis=-1, keepdims=True)
            yield
            ds = pn * (dp - dl)
            yield
            dbias_ref[h] += ds
            out["dsink"] = jnp.sum(-out["sink"] * dl)
            yield
            dsb = (ds * (HD ** -0.5)).astype(BF16)
            yield
            dqacc[e, :, h * HD:(h + 1) * HD] = _dot(dsb, ks[e, hk])
            out["ds"], out["pb"], out["do"] = dsb, pn.astype(BF16), doh
            yield

        def kv_head(e, hk, outs):
            rows = lambda key: jnp.concatenate([outs[hk * GRP + g][key] for g in range(GRP)], axis=0)
            dk = _dot_tn(rows("ds"), rows("q"))
            yield
            dv = _dot_tn(rows("pb"), rows("do"))
            yield
            dkv[e, :, hk * HD:(hk + 1) * HD] = dk
            dkv[e, :, KVW + hk * HD:KVW + (hk + 1) * HD] = dv
            yield

        for e in range(nex):
            _fill_kv(ks, vs, e, prev_ref, cur_ref, meta_ref)
            q = q_ref[e]
            da = jnp.zeros((BLK, AW), F32)
            for jj in range(NCHIP):
                da = da + _dot_nt(dy_ref[e, :, jj * ns:(jj + 1) * ns], w_ref[jj])
            outs = [{} for _ in range(NQ)]
            _interleave([head(e, h, q, da, outs[h]) for h in range(NQ)])
            _interleave([kv_head(e, hk, outs) for hk in range(NKV)])
            for h in range(NQ):
                dsk = dsk + jnp.where(lane == h, outs[h]["dsink"], 0.0)
            macc[e] += dkv[e, 2 * BLK:2 * BLK + N_META, :]
            okv[e] = dkv[e, BLK:2 * BLK, :] + carry[e]
            carry[e] = dkv[e, 0:BLK, :]

            @pl.when(n == 0)
            def _():
                okv[e, PAD:BLK, :] += macc[e]

            dq = dqacc[e]
            ok = okv[e]
            dz_ref[e, :, 0:AW] = dq.astype(BF16)
            dz_ref[e, :, AW:AW + 2 * KVW] = ok.astype(BF16)
            cs_ref[:, 0:AW] += jnp.sum(dq, axis=0, keepdims=True)
            cs_ref[:, AW:AW + 2 * KVW] += jnp.sum(ok, axis=0, keepdims=True)
        dsk_ref[...] += dsk

    wz = AW + 2 * KVW
    specs = _attn_specs(nex, blk_of) + [
        pl.BlockSpec((nex, BLK, D), lambda j: (0, blk_of(j), 0)),
        pl.BlockSpec((NCHIP, AW, ns), lambda j: (0, 0, 0))]
    res = _call(body, "attn_bwd",
                (jax.ShapeDtypeStruct((nex, tp, wz), BF16), jax.ShapeDtypeStruct((1, wz), F32), jax.ShapeDtypeStruct((1, BLK), F32),
                 jax.ShapeDtypeStruct((1, 3, NQ, BLK, NKEY), F32)),
                (nblk,), specs,
                (pl.BlockSpec((nex, BLK, wz), lambda j: (0, blk_of(j), 0)), pl.BlockSpec((1, wz), lambda j: (0, 0)),
                 pl.BlockSpec((1, BLK), lambda j: (0, 0)),
                 pl.BlockSpec((None, None, NQ, BLK, NKEY), lambda j: (0, jnp.minimum(blk_of(j), 2), 0, 0, 0))),
                scratch=[pltpu.VMEM((nex, NKV, NKEY, HD), BF16), pltpu.VMEM((nex, NKV, NKEY, HD), BF16),
                         pltpu.VMEM((nex, BLK, 2 * KVW), F32), pltpu.VMEM((nex, N_META, 2 * KVW), F32),
                         pltpu.VMEM((nex, BLK, AW), F32), pltpu.VMEM((nex, NKEY, 2 * KVW), F32),
                         pltpu.VMEM((nex, BLK, 2 * KVW), F32)],
                side=side)(z3, z3, z3, z3, bias, sinks.reshape(1, NQ), dya3, wap4)
    return (res[0].reshape(m, wz),) + tuple(res[1:])


def _tile_rows(rows, cols, target_bytes=1 << 20):
    best = None
    for t in range(8, rows + 1, 8):
        if rows % t == 0 and t * cols * 4 <= target_bytes:
            best = t
    return best or rows


def _sum0(name, x):
    n, r, c = x.shape
    tr = _tile_rows(r, c * n)

    def body(x_ref, o_ref):
        acc = x_ref[0]
        for k in range(1, n):
            acc = acc + x_ref[k]
        o_ref[...] = acc

    return _call(body, name, jax.ShapeDtypeStruct((r, c), F32), (r // tr,),
                 [pl.BlockSpec((n, tr, c), lambda i: (0, i, 0))], pl.BlockSpec((tr, c), lambda i: (i, 0)))(x)


def _adamw(name, w, g, mom, vel):
    r, c = w.shape
    tr = _tile_rows(r, c)
    c1 = 1.0 / (1.0 - ADAM_B1 ** ADAM_STEP)
    c2 = 1.0 / (1.0 - ADAM_B2 ** ADAM_STEP)

    def body(w_ref, g_ref, m_ref, v_ref, d_ref, mo_ref, vo_ref):
        gg = g_ref[...]
        mn = ADAM_B1 * m_ref[...] + (1.0 - ADAM_B1) * gg
        vn = ADAM_B2 * v_ref[...] + (1.0 - ADAM_B2) * (gg * gg)
        mo_ref[...] = mn
        vo_ref[...] = vn
        d_ref[...] = -ADAM_LR * ((mn * c1) / (jnp.sqrt(vn * c2) + ADAM_EPS) + ADAM_WD * w_ref[...])

    spec = pl.BlockSpec((tr, c), lambda i: (i, 0))
    o = jax.ShapeDtypeStruct((r, c), F32)
    return _call(body, name, (o, o, o), (r // tr,), [spec] * 4, (spec, spec, spec))(w, g, mom, vel)


def _place():
    x, y, c = lax.axis_index("x"), lax.axis_index("y"), lax.axis_index("c")
    others = [(1 - x, y), (x, 1 - y), (1 - x, 1 - y)]
    return x, y, c, others


def _gather_job(items):
    nw = len(items)

    def views(s_ref, g_ref, layer, c):
        if layer is None:
            return s_ref.at[c], lambda chip, cc: g_ref.at[chip, cc]
        hr = s_ref.shape[1] // 2
        return s_ref.at[layer, pl.ds(c * hr, hr)], lambda chip, cc: g_ref.at[chip, pl.ds(cc * hr, hr)]

    def copies(s_refs, g_refs, send, recv):
        x, y, c, others = _place()
        chip = 2 * x + y
        firsts, arrive, passed, arrive2 = [], [], [], []
        for w, (_, layer) in enumerate(items):
            src, dst = views(s_refs[w], g_refs[w], layer, c)
            for k, (px, py) in enumerate(others):
                def rc(kk, s, d, to, w=w):
                    return pltpu.make_async_remote_copy(src_ref=s, dst_ref=d, send_sem=send.at[w, kk], recv_sem=recv.at[w, kk],
                                                        device_id=to, device_id_type=MESH)
                got, got2 = dst(2 * px + py, c), dst(2 * px + py, 1 - c)
                firsts.append(rc(k, src, dst(chip, c), (px, py, c)))
                arrive.append(rc(k, got, got, (x, y, c)))
                passed.append(rc(3 + k, got, got, (x, y, 1 - c)))
                arrive2.append(rc(3 + k, got2, got2, (x, y, c)))
        return firsts, arrive, passed, arrive2

    def start(s_refs, g_refs, send, recv):
        for cp in copies(s_refs, g_refs, send, recv)[0]:
            cp.start()

    def finish(s_refs, g_refs, send, recv):
        firsts, arrive, passed, arrive2 = copies(s_refs, g_refs, send, recv)
        for a, p in zip(arrive, passed):
            a.wait_recv()
            p.start()
        for a in arrive2:
            a.wait_recv()
        for cp in firsts + passed:
            cp.wait_send()

    outs = [jax.ShapeDtypeStruct((NCHIP,) + (s.shape if layer is None else s.shape[1:]), s.dtype) for s, layer in items]
    return _Job([s for s, _ in items], outs, (nw, 6), start, finish)


def _swap_job(grads):
    def copies(d_refs, a_refs, send, recv):
        x, y, c, _ = _place()
        cps = []
        for w in range(len(grads)):
            h = d_refs[w].shape[1] // 2
            cps.append(pltpu.make_async_remote_copy(
                src_ref=d_refs[w].at[:, pl.ds((1 - c) * h, h), :], dst_ref=a_refs[w], send_sem=send.at[w], recv_sem=recv.at[w],
                device_id=(x, y, 1 - c), device_id_type=MESH))
        return cps

    def start(*r):
        for cp in copies(*r):
            cp.start()

    def finish(*r):
        for cp in copies(*r):
            cp.wait()

    outs = [jax.ShapeDtypeStruct((NCHIP, g.shape[1] // 2, g.shape[2]), g.dtype) for g in grads]
    return _Job(list(grads), outs, (len(grads),), start, finish)


def _exchange_job(parts):
    def copies(q_refs, b_refs, send, recv):
        x, y, c, others = _place()
        cps = []
        for w in range(len(parts)):
            for k, (px, py) in enumerate(others):
                cps.append(pltpu.make_async_remote_copy(
                    src_ref=q_refs[w].at[2 * px + py], dst_ref=b_refs[w].at[k], send_sem=send.at[w, k], recv_sem=recv.at[w, k],
                    device_id=(px, py, c), device_id_type=MESH))
        return cps

    def start(*r):
        for cp in copies(*r):
            cp.start()

    def finish(*r):
        for cp in copies(*r):
            cp.wait()

    outs = [jax.ShapeDtypeStruct((3,) + p.shape[1:], p.dtype) for p in parts]
    return _Job(list(parts), outs, (len(parts), 3), start, finish)


def _run_job(name, job):
    n_in, n_out = len(job.ins), len(job.outs)

    def body(*refs):
        ins, outs = refs[:n_in], refs[n_in:n_in + n_out]
        send, recv = refs[n_in + n_out:]
        job.start(ins, outs, send, recv)
        job.finish(ins, outs, send, recv)

    return pl.pallas_call(
        body, name=name, out_shape=tuple(job.outs), in_specs=[_ANY] * n_in, out_specs=tuple([_ANY] * n_out),
        scratch_shapes=[pltpu.SemaphoreType.DMA(job.sems), pltpu.SemaphoreType.DMA(job.sems)])(*job.ins)


def _sibling_join(halves):
    nw = len(halves)

    def body(*refs):
        h_refs = refs[:2 * nw]
        f_refs = refs[2 * nw:3 * nw]
        send, recv = refs[3 * nw:]
        x, y, c, _ = _place()
        cps = []
        for w in range(nw):
            for l in range(2):
                src = h_refs[2 * w + l]
                h = src.shape[0]
                dst = f_refs[w].at[l, pl.ds(c * h, h), :]
                cp = pltpu.make_async_remote_copy(src_ref=src, dst_ref=dst, send_sem=send.at[w, l], recv_sem=recv.at[w, l],
                                                  device_id=(x, y, 1 - c), device_id_type=MESH)
                cp.start()
                cps.append(cp)
        for w in range(nw):
            for l in range(2):
                src = h_refs[2 * w + l]
                h = src.shape[0]
                other = f_refs[w].at[l, pl.ds((1 - c) * h, h), :]
                pltpu.make_async_remote_copy(src_ref=src, dst_ref=other, send_sem=send.at[w, l], recv_sem=recv.at[w, l],
                                             device_id=(x, y, c), device_id_type=MESH).wait_recv()
        for cp in cps:
            cp.wait_send()

    flat = [a for pair in halves for a in pair]
    outs = tuple(jax.ShapeDtypeStruct((2, 2 * pair[0].shape[0], pair[0].shape[1]), F32) for pair in halves)
    return pl.pallas_call(
        body, name="grad_sibling_join", out_shape=outs, in_specs=[_ANY] * (2 * nw), out_specs=tuple([_ANY] * nw),
        scratch_shapes=[pltpu.SemaphoreType.DMA((nw, 2)), pltpu.SemaphoreType.DMA((nw, 2))])(*flat)


def _allgather_small(v):
    r = v.shape[0]

    def body(x_ref, out_ref, send_sems, recv_sems, local_sem):
        x, y, c, chips = _place()
        me, sibling = (x, y, c), (x, y, 1 - c)

        def slab(px, py, pc):
            return out_ref.at[4 * px + 2 * py + pc]

        def copy(k, block, to, src=None):
            return pltpu.make_async_remote_copy(src_ref=slab(*block) if src is None else src, dst_ref=slab(*block),
                                                send_sem=send_sems.at[k], recv_sem=recv_sems.at[k],
                                                device_id=to, device_id_type=MESH)

        mine = pltpu.make_async_copy(x_ref, slab(*me), local_sem)
        mine.start()
        first = [copy(0, me, sibling, src=x_ref)]
        first += [copy(1 + j, me, (*chip, c), src=x_ref) for j, chip in enumerate(chips)]
        for cp in first:
            cp.start()
        passed = [copy(4 + j, (*chip, c), sibling) for j, chip in enumerate(chips)]
        for j, chip in enumerate(chips):
            copy(1 + j, (*chip, c), me).wait_recv()
            passed[j].start()
        copy(0, sibling, me).wait_recv()
        for j, chip in enumerate(chips):
            copy(4 + j, (*chip, 1 - c), me).wait_recv()
        for cp in first + passed:
            cp.wait_send()
        mine.wait()

    return pl.pallas_call(
        body, name="allgather_small", out_shape=jax.ShapeDtypeStruct((8, r, 128), F32),
        in_specs=[pl.BlockSpec(memory_space=pltpu.VMEM)], out_specs=pl.BlockSpec(memory_space=pltpu.VMEM),
        scratch_shapes=[pltpu.SemaphoreType.DMA((7,)), pltpu.SemaphoreType.DMA((7,)), pltpu.SemaphoreType.DMA],
    )(v)


def _add_half(name, d, a, c):
    _, h, cols = a.shape
    nt = 1
    th = h // nt

    def body(c_ref, d_ref, a_ref, o_ref):
        o_ref[...] = (d_ref[...] + a_ref[...]).astype(BF16)

    return _call(body, name, jax.ShapeDtypeStruct(a.shape, BF16), (NCHIP, nt),
                 [pl.BlockSpec((None, th, cols), lambda p, i, cr: (p, cr[0] * nt + i, 0)),
                  pl.BlockSpec((None, th, cols), lambda p, i, cr: (p, i, 0))],
                 pl.BlockSpec((None, th, cols), lambda p, i, cr: (p, i, 0)), prefetch=1)(c, d, a)


def _add_chips(name, d, a, b, where):
    _, h, cols = a.shape
    th = h // 4 if (h % 64 == 0) else h
    nt = h // th

    def body(w_ref, d_ref, a_ref, b_ref, o_ref):
        own = d_ref[...] + a_ref[...]
        o_ref[...] = ((own + b_ref[0].astype(F32)) + b_ref[1].astype(F32)) + b_ref[2].astype(F32)

    return _call(body, name, jax.ShapeDtypeStruct((h, cols), F32), (nt,),
                 [pl.BlockSpec((None, th, cols), lambda i, wr: (wr[0], wr[1] * nt + i, 0)),
                  pl.BlockSpec((None, th, cols), lambda i, wr: (wr[0], i, 0)),
                  pl.BlockSpec((3, th, cols), lambda i, wr: (0, i, 0))],
                 pl.BlockSpec((th, cols), lambda i, wr: (i, 0)), prefetch=1)(where, d, a, b)


def _pack(arrs):
    pieces = []
    for a in arrs:
        f = a.reshape(-1)
        n = -(-f.shape[0] // 1024) * 1024
        pieces.append(jnp.pad(f, (0, n - f.shape[0])).reshape(-1, 128))
    return jnp.concatenate(pieces, axis=0)


def _unpack(buf, shapes):
    out, r = [], 0
    for s in shapes:
        n = int(np.prod(s))
        rows = -(-n // 1024) * 8
        out.append(buf[r:r + rows].reshape(-1)[:n].reshape(s))
        r += rows
    return out


def kernel(x, meta_tokens, in_ln_g, in_ln_b, rel_bias, w_in, b_in, attn_sinks, w_attn_proj, conv_dw, conv_dw_b, conv_ln_g, conv_ln_b, w_conv_proj, w_out, ln1_g, ln1_b, ffn_w_up, ffn_dw, ffn_dw_b, ffn_w_down, ln2_g, ln2_b, loss_target, m_meta_tokens, m_in_ln_g, m_in_ln_b, m_rel_bias, m_w_in, m_b_in, m_attn_sinks, m_w_attn_proj, m_conv_dw, m_conv_dw_b, m_conv_ln_g, m_conv_ln_b, m_w_conv_proj, m_w_out, m_ln1_g, m_ln1_b, m_ffn_w_up, m_ffn_dw, m_ffn_dw_b, m_ffn_w_down, m_ln2_g, m_ln2_b, v_meta_tokens, v_in_ln_g, v_in_ln_b, v_rel_bias, v_w_in, v_b_in, v_attn_sinks, v_w_attn_proj, v_conv_dw, v_conv_dw_b, v_conv_ln_g, v_conv_ln_b, v_w_conv_proj, v_w_out, v_ln1_g, v_ln1_b, v_ffn_w_up, v_ffn_dw, v_ffn_dw_b, v_ffn_w_down, v_ln2_g, v_ln2_b):
    nex, seq, _ = x.shape
    nblk = seq // BLK + 1
    tp = nblk * BLK
    m = nex * tp
    tm = _row_tile(m)
    ffs = ffn_w_up.shape[2]
    dff = 2 * ffs
    cx, cy, cc = lax.axis_index("x"), lax.axis_index("y"), lax.axis_index("c")
    chip = (2 * cx + cy).astype(jnp.int32)
    core = cc.astype(jnp.int32)

    names = ("in", "ap", "cp", "out", "up", "down")
    big = dict(zip(names, [w_in, w_attn_proj, w_conv_proj, w_out, ffn_w_up, ffn_w_down]))
    sb = {k: v.astype(BF16) for k, v in big.items()}
    gathered = {}

    def land(items, outs):
        return [lax.dynamic_update_index_in_dim(g, s if layer is None else s[layer], chip, 0)
                for (s, layer), g in zip(items, outs)]

    first_items = [(meta_tokens.reshape(2, N_META // 2, -1), None), (conv_dw, None), (ffn_dw, None)]
    g_meta, g_cdw, g_fdw = land(first_items, _run_job("gather_small", _gather_job(first_items)))
    meta_full = jnp.transpose(g_meta, (1, 2, 0, 3)).reshape(N_META, D)
    bias_tab = _bias_build(rel_bias)

    fwd_plan = {("embed_ln", 0): [("in", 0)],
                ("in_proj", 0): [("ap", 0), ("cp", 0), ("out", 0)], ("attn_fwd", 0): [("up", 0)],
                ("conv_fwd", 0): [("down", 0)], ("mix_fwd", 0): [("in", 1)],
                ("out_proj_ln", 0): [("ap", 1), ("cp", 1), ("out", 1)], ("ffn_up_act", 0): [("up", 1), ("down", 1)]}

    def fwd(tag, l, fn, *args):
        keys = fwd_plan.get((tag, l))
        if not keys:
            return fn(*args)
        items = [(sb[k], kl) for k, kl in keys]
        res = fn(*args, side=_gather_job(items))
        for key, g in zip(keys, land(items, res[-len(keys):])):
            gathered[key] = g
        main = res[:-len(keys)]
        return main[0] if len(main) == 1 else main

    def layer_weights(l):
        win_old = jnp.transpose(gathered[("in", l)], (1, 0, 2)).reshape(D, IN_COLS)
        return dict(
            win=_to_new(win_old), bin=_to_new(b_in[l]).reshape(1, IN_COLS),
            cdw=jnp.transpose(g_cdw[:, l], (1, 0, 2)).reshape(CTAPS, CW),
            fdw=jnp.transpose(g_fdw[:, l], (1, 0, 2)).reshape(FTAPS, 2 * dff),
            fdwb=ffn_dw_b[l].reshape(1, 2 * dff))

    raw, h, hb = fwd("embed_ln", 0, _embed_ln, x, meta_full, in_ln_g, in_ln_b, nblk)
    saved, lw = [], []
    for l in range(DEPTH):
        p = layer_weights(l)
        z = fwd("in_proj", l, functools.partial(_mm_bias, "in_proj"), hb, p["win"], p["bin"], IN_COLS // 3, tm)
        a = fwd("attn_fwd", l, _attn_fwd, z, bias_tab, attn_sinks[l], nex, nblk)
        ccv, cs = fwd("conv_fwd", l, _conv_fwd, z, p["cdw"], conv_dw_b[l], conv_ln_g[l], conv_ln_b[l], nex, tp)
        p["wap"], p["wcp"] = gathered[("ap", l)], gathered[("cp", l)]
        ya, yc, mixed = fwd("mix_fwd", l, _mix_fwd, a, cs, p["wap"], p["wcp"], z, tm)
        p["wout"] = gathered[("out", l)].reshape(D, D)
        r1, h1, h1b = fwd("out_proj_ln", l, functools.partial(_mm_res_ln, "out_proj_ln"), mixed, p["wout"], h, ln1_g[l], ln1_b[l], tm)
        p["wup"] = gathered[("up", l)]
        up3, ug, act = fwd("ffn_up_act", l, _ffn_up_act, h1b, p["wup"], p["fdw"], p["fdwb"], tm, nex, tp)
        p["wdown"] = gathered[("down", l)].reshape(dff, D)
        r2, h2, h2b = _mm_res_ln("ffn_down_ln", act, p["wdown"], h1, ln2_g[l], ln2_b[l], tm)
        saved.append(dict(hb=hb, z=z, a=a, cc=ccv, cs=cs, ya=ya, yc=yc, mixed=mixed, r1=r1, h1b=h1b, up3=up3, ug=ug, act=act, r2=r2))
        lw.append(p)
        h, hb = h2, h2b

    dy, sq = _loss_grad(h, loss_target, nblk)
    loss = lax.psum(0.5 / D * jnp.sum(sq), ("x", "y", "c"))

    grads, swapped, pair_sums, reduced = {}, {}, {}, {}
    cvec, where = core.reshape(1), jnp.stack([chip, core])
    last = [(k, DEPTH - 1) for k in names]
    bwd_plan = {("ln2_bwd", 0): ("swap", last),
                ("ffn_bwd", 0): ("exch", [("up", 1), ("down", 1)]),
                ("dw_up", 0): ("exch", [("in", 1), ("ap", 1), ("cp", 1), ("out", 1)]),
                ("ln1_bwd", 0): ("swap", [("down", 0), ("up", 0)]),
                ("conv_bwd_b", 0): ("swap", [("out", 0), ("ap", 0), ("cp", 0)]),
                ("attn_bwd", 0): ("exch", [("down", 0), ("up", 0), ("out", 0), ("ap", 0), ("cp", 0)]),
                ("in_ln_bwd", 0): ("swap", [("in", 0)])}

    def after(kind, keys, outs):
        for key, o in zip(keys, outs):
            if kind == "swap":
                swapped[key] = o
                pair_sums[key] = _add_half("grad_add_sibling", grads[key], o, cvec)
            else:
                reduced[key] = _add_chips("grad_add_chips", grads[key], swapped[key], o, where)

    def bwd(tag, l, fn, *args):
        plan = bwd_plan.get((tag, l))
        if plan is None:
            return fn(*args)
        kind, keys = plan
        job = _swap_job([grads[k] for k in keys]) if kind == "swap" else _exchange_job([pair_sums[k] for k in keys])
        res = fn(*args, side=job)
        after(kind, keys, res[-len(keys):])
        main = res[:-len(keys)]
        return main[0] if len(main) == 1 else main

    small = {}
    prev_a, prev_w, prev_cols = (), None, None
    dprev = dy
    for l in reversed(range(DEPTH)):
        p, s = lw[l], saved[l]
        dr2, dr2b, dg2, db2 = bwd("ln2_bwd", l, functools.partial(_ln_bwd_call, "ln2_bwd"), dprev, s["r2"], ln2_g[l], tm,
                                  prev_a, prev_w, prev_cols)
        dpre3, dfdw, dfdwb = bwd("ffn_bwd", l, _ffn_bwd, dr2b, p["wdown"], s["ug"], s["up3"], p["fdw"], tm, nex, tp)
        grads[("down", l)] = _mm_tn("dw_down", s["act"], dr2b, ffs, D // 2).reshape(NCHIP, dff // NCHIP, D)
        grads[("up", l)] = bwd("dw_up", l, functools.partial(_mm_tn, "dw_up"), s["h1b"], dpre3, D, ffs,
                               lambda j: (j // 2, j % 2), True)
        dr1, dr1b, dg1, db1 = bwd("ln1_bwd", l, functools.partial(_ln_bwd_call, "ln1_bwd"), dr2, s["r1"], ln1_g[l], tm // 2,
                                  (dpre3,), p["wup"])
        dya, dyc, dzg, csg = _gate_bwd(dr1b, p["wout"], s["ya"], s["yc"], s["z"], tm)
        grads[("out", l)] = _mm_tn("dw_out", s["mixed"], dr1b, D, D // 2).reshape(NCHIP, D // NCHIP, D)
        grads[("ap", l)] = _mm_tn("dw_attn_proj", s["a"], dya, AW, D // NCHIP, chip_out=True)
        grads[("cp", l)] = _mm_tn("dw_conv_proj", s["cs"], dyc, CW, D // NCHIP, chip_out=True)
        dcc, dclg, dclb, dcwb = _conv_bwd_a(dyc, p["wcp"], s["cc"], conv_ln_g[l], conv_ln_b[l], tm)
        dzc, dcdw, csc = bwd("conv_bwd_b", l, _conv_bwd_b, dcc, s["z"], p["cdw"], nex, tp)
        dzq, csq, dsk, dbias = bwd("attn_bwd", l, _attn_bwd, s["z"], bias_tab, attn_sinks[l], dya, p["wap"], nex, nblk)
        gin = [_mm_tn("dw_in_gates", s["hb"], dzg, D, D // 2), _mm_tn("dw_in_conv", s["hb"], dzc, D, CW),
               _mm_tn("dw_in_qkv", s["hb"], dzq, D, 2 * KVW)]
        gin_old = _to_old(jnp.concatenate(gin, axis=1))
        grads[("in", l)] = jnp.transpose(gin_old.reshape(D, NCHIP, IN_COLS // NCHIP), (1, 0, 2))
        small[l] = dict(
            b_in=_to_old(jnp.concatenate([csg, csc, csq], axis=1)).reshape(IN_COLS), attn_sinks=dsk[0, :NQ],
            conv_dw=dcdw[:CTAPS], conv_dw_b=dcwb.reshape(CW), conv_ln_g=dclg.reshape(CW), conv_ln_b=dclb.reshape(CW),
            ln1_g=dg1.reshape(D), ln1_b=db1.reshape(D),
            ffn_dw=jnp.transpose(dfdw, (1, 0, 2)).reshape(FTAPS, 2 * dff), ffn_dw_b=jnp.transpose(dfdwb, (1, 0, 2)).reshape(2 * dff),
            ln2_g=dg2.reshape(D), ln2_b=db2.reshape(D), bias=dbias)
        dprev = dr1
        prev_a, prev_w, prev_cols = (dzg, dzc, dzq), p["win"], [(C_GATES, C_CONV), (C_CONV, C_QKV), (C_QKV, IN_COLS)]
    draw, _, dg0, db0 = bwd("in_ln_bwd", 0, functools.partial(_ln_bwd_call, "in_ln_bwd"), dprev, raw, in_ln_g, tm,
                            prev_a, prev_w, prev_cols)
    draw3 = draw.reshape(nex, tp, D)
    grad_x = draw3[:, BLK:, :]
    dmeta = _sum0("meta_grad_sum", draw3[:, PAD:BLK, :])

    names_l = ["b_in", "attn_sinks", "conv_dw", "conv_dw_b", "conv_ln_g", "conv_ln_b", "ln1_g", "ln1_b", "ffn_dw", "ffn_dw_b", "ln2_g", "ln2_b"]
    dbias_all = _bias_grad(jnp.concatenate([small[l]["bias"] for l in range(DEPTH)], axis=0))
    part_list = [dmeta, dg0.reshape(D), db0.reshape(D), dbias_all]
    part_list += [jnp.stack([small[0][n], small[1][n]]) for n in names_l]
    shapes_small = [tuple(a.shape) for a in part_list]
    tot = _sum0("small_grad_sum", _allgather_small(_pack(part_list)))
    (g_meta_f, g_inlg, g_inlb, g_biasp, g_bin, g_sinks, g_cdw_f, g_cdwb, g_clg, g_clb, g_l1g, g_l1b, g_fdw_f, g_fdwb,
     g_l2g, g_l2b) = _unpack(tot, shapes_small)
    g_relb = g_biasp
    csh = D // NCHIP
    g_meta_s = lax.dynamic_slice_in_dim(g_meta_f, chip * csh, csh, axis=1)
    g_cdw_s = lax.dynamic_slice_in_dim(g_cdw_f, chip * (CW // NCHIP), CW // NCHIP, axis=2)
    g_fdw_s = lax.dynamic_slice_in_dim(g_fdw_f, chip * ffs, ffs, axis=2)

    tail = [("in", 0)]
    after("exch", tail, _run_job("grad_chip_exchange", _exchange_job([pair_sums[k] for k in tail])))
    joined = _sibling_join([[reduced[(k, l)] for l in range(DEPTH)] for k in names])
    full = []
    for k, f in zip(names, joined):
        hh = reduced[(k, 0)].shape[0]
        for l in range(DEPTH):
            f = lax.dynamic_update_slice(f, reduced[(k, l)][None], (l, core * hh, 0))
        full.append(f)

    moms = [m_w_in, m_w_attn_proj, m_w_conv_proj, m_w_out, m_ffn_w_up, m_ffn_w_down]
    vels = [v_w_in, v_w_attn_proj, v_w_conv_proj, v_w_out, v_ffn_w_up, v_ffn_w_down]
    big_out = []
    for w, g, mo, ve in zip(big.values(), full, moms, vels):
        sh = w.shape
        two = lambda t: t.reshape(sh[0] * sh[1], sh[2])
        d_, m_, v_ = _adamw("adamw_matrix", two(w), two(g), two(mo), two(ve))
        big_out.append((g.reshape(sh), d_.reshape(sh), m_.reshape(sh), v_.reshape(sh)))

    sm_w = [meta_tokens, in_ln_g, in_ln_b, rel_bias, b_in, attn_sinks, conv_dw, conv_dw_b, conv_ln_g, conv_ln_b, ln1_g, ln1_b,
            ffn_dw, ffn_dw_b, ln2_g, ln2_b]
    sm_m = [m_meta_tokens, m_in_ln_g, m_in_ln_b, m_rel_bias, m_b_in, m_attn_sinks, m_conv_dw, m_conv_dw_b, m_conv_ln_g, m_conv_ln_b,
            m_ln1_g, m_ln1_b, m_ffn_dw, m_ffn_dw_b, m_ln2_g, m_ln2_b]
    sm_v = [v_meta_tokens, v_in_ln_g, v_in_ln_b, v_rel_bias, v_b_in, v_attn_sinks, v_conv_dw, v_conv_dw_b, v_conv_ln_g, v_conv_ln_b,
            v_ln1_g, v_ln1_b, v_ffn_dw, v_ffn_dw_b, v_ln2_g, v_ln2_b]
    sm_g = [g_meta_s, g_inlg, g_inlb, g_relb, g_bin, g_sinks, g_cdw_s, g_cdwb, g_clg, g_clb, g_l1g, g_l1b, g_fdw_s, g_fdwb, g_l2g, g_l2b]
    sm_shapes = [tuple(a.shape) for a in sm_w]
    sd, smn, svn = _adamw("adamw_small", _pack(sm_w), _pack(sm_g), _pack(sm_m), _pack(sm_v))
    sd, smn, svn = _unpack(sd, sm_shapes), _unpack(smn, sm_shapes), _unpack(svn, sm_shapes)

    order = ["meta_tokens", "in_ln_g", "in_ln_b", "rel_bias", "w_in", "b_in", "attn_sinks", "w_attn_proj", "conv_dw", "conv_dw_b",
             "conv_ln_g", "conv_ln_b", "w_conv_proj", "w_out", "ln1_g", "ln1_b", "ffn_w_up", "ffn_dw", "ffn_dw_b", "ffn_w_down",
             "ln2_g", "ln2_b"]
    small_names = ["meta_tokens", "in_ln_g", "in_ln_b", "rel_bias", "b_in", "attn_sinks", "conv_dw", "conv_dw_b", "conv_ln_g",
                   "conv_ln_b", "ln1_g", "ln1_b", "ffn_dw", "ffn_dw_b", "ln2_g", "ln2_b"]
    big_names = ["w_in", "w_attn_proj", "w_conv_proj", "w_out", "ffn_w_up", "ffn_w_down"]
    res = {}
    for i, n in enumerate(small_names):
        res[n] = (sm_g[i], sd[i], smn[i], svn[i])
    for i, n in enumerate(big_names):
        res[n] = big_out[i]
    outs = [loss, grad_x]
    for k in range(4):
        outs += [res[n][k] for n in order]
    return tuple(outs)
```

```python
import functools
import math
from typing import Any, Callable, NamedTuple, Sequence

import numpy as np
import jax
import jax.numpy as jnp
from jax import lax
from jax.experimental import pallas as pl
from jax.experimental.pallas import tpu as pltpu

F32 = jnp.float32
BF16 = jnp.bfloat16
MESH = pl.DeviceIdType.MESH

D = 1024
N_META = 16
BLK = 128
PAD = BLK - N_META
HD = 64
NQ = 8
NKV = 2
GRP = NQ // NKV
AW = NQ * HD
KVW = NKV * HD
CW = D // 2
CTAPS = 31
FTAPS = 3
NBUCKET = 32
MAXDIST = 128
EPS = 1e-5
DEPTH = 2
ALPHA = (2.0 * DEPTH) ** 0.25
NCHIP = 4
NKEY = 3 * BLK
NEG = -1e30
CHALO = 32
FHALO = 8
IN_COLS = AW + 2 * KVW + 2 * CW + 2 * D
_OLD = dict(q=(0, AW), k=(AW, AW + KVW), v=(AW + KVW, AW + 2 * KVW), cv=(AW + 2 * KVW, AW + 2 * KVW + CW),
            cg=(AW + 2 * KVW + CW, AW + 2 * KVW + 2 * CW), ga=(AW + 2 * KVW + 2 * CW, AW + 2 * KVW + 2 * CW + D),
            gc=(AW + 2 * KVW + 2 * CW + D, IN_COLS))
_NEW_ORDER = ("ga", "gc", "cv", "cg", "q", "k", "v")
C_GATES, C_CONV, C_QKV = 0, 2 * D, 2 * D + 2 * CW

ADAM_LR, ADAM_B1, ADAM_B2, ADAM_EPS, ADAM_WD, ADAM_STEP = 0.001, 0.9, 0.999, 1e-08, 0.01, 10


def _to_new(a):
    return jnp.concatenate([a[..., _OLD[n][0]:_OLD[n][1]] for n in _NEW_ORDER], axis=-1)


def _to_old(a):
    offs, o = {}, 0
    for n in _NEW_ORDER:
        w = _OLD[n][1] - _OLD[n][0]
        offs[n] = (o, o + w)
        o += w
    return jnp.concatenate([a[..., offs[n][0]:offs[n][1]] for n in ("q", "k", "v", "cv", "cg", "ga", "gc")], axis=-1)


def _new_starts():
    starts, o = {}, 0
    for n in _NEW_ORDER:
        starts[n] = o
        o += _OLD[n][1] - _OLD[n][0]
    return starts


def _w_in_from_chips(g4):
    cs = IN_COLS // NCHIP
    pieces = []
    for n in _NEW_ORDER:
        lo, hi = _OLD[n]
        while lo < hi:
            chip = lo // cs
            end = min(hi, (chip + 1) * cs)
            pieces.append(g4[chip][:, lo - chip * cs:end - chip * cs])
            lo = end
    return jnp.concatenate(pieces, axis=1)


def _w_in_to_chips(parts):
    cs = IN_COLS // NCHIP
    starts = _new_starts()
    bounds, o = [], 0
    for p in parts:
        bounds.append((o, o + p.shape[1], p))
        o += p.shape[1]

    def new_cols(a, b):
        out = []
        for s, e, p in bounds:
            lo, hi = max(a, s), min(b, e)
            if lo < hi:
                out.append(p[:, lo - s:hi - s])
        return out

    slabs = []
    for chip in range(NCHIP):
        pieces = []
        for n in ("q", "k", "v", "cv", "cg", "ga", "gc"):
            lo, hi = max(_OLD[n][0], chip * cs), min(_OLD[n][1], (chip + 1) * cs)
            if lo < hi:
                pieces += new_cols(starts[n] + lo - _OLD[n][0], starts[n] + hi - _OLD[n][0])
        slabs.append(jnp.concatenate(pieces, axis=1))
    return jnp.stack(slabs)


class _Job(NamedTuple):
    ins: Sequence[Any]
    outs: Sequence[Any]
    sems: tuple
    start: Callable
    finish: Callable


_ANY = pl.BlockSpec(memory_space=pl.ANY)


def _call(body, name, out_shape, grid, in_specs, out_specs, scratch=(), prefetch=0, side=None):
    params = pltpu.CompilerParams(dimension_semantics=("arbitrary",) * len(grid))
    if side is None:
        if prefetch:
            gs = pltpu.PrefetchScalarGridSpec(num_scalar_prefetch=prefetch, grid=grid, in_specs=in_specs,
                                              out_specs=out_specs, scratch_shapes=list(scratch))
            return pl.pallas_call(body, name=name, out_shape=out_shape, grid_spec=gs, compiler_params=params)
        return pl.pallas_call(body, name=name, out_shape=out_shape, grid=grid, in_specs=in_specs, out_specs=out_specs,
                              scratch_shapes=list(scratch), compiler_params=params)
    assert not prefetch
    single = not isinstance(out_shape, (tuple, list))
    main_shapes = (out_shape,) if single else tuple(out_shape)
    main_specs = (out_specs,) if single else tuple(out_specs)
    n_in, n_sin, n_out, n_sout, n_scr = len(in_specs), len(side.ins), len(main_shapes), len(side.outs), len(scratch)

    def wrapped(*refs):
        main_in, sin = refs[:n_in], refs[n_in:n_in + n_sin]
        o0 = n_in + n_sin
        main_out, sout = refs[o0:o0 + n_out], refs[o0 + n_out:o0 + n_out + n_sout]
        s0 = o0 + n_out + n_sout
        main_scr, (send, recv) = refs[s0:s0 + n_scr], refs[s0 + n_scr:]
        first = functools.reduce(lambda a, b: a & b, [pl.program_id(k) == 0 for k in range(len(grid))])
        last = functools.reduce(lambda a, b: a & b, [pl.program_id(k) == grid[k] - 1 for k in range(len(grid))])

        @pl.when(first)
        def _():
            side.start(sin, sout, send, recv)

        body(*main_in, *main_out, *main_scr)

        @pl.when(last)
        def _():
            side.finish(sin, sout, send, recv)

    call = pl.pallas_call(
        wrapped, name=name, out_shape=main_shapes + tuple(side.outs), grid=grid,
        in_specs=list(in_specs) + [_ANY] * n_sin, out_specs=main_specs + tuple([_ANY] * n_sout),
        scratch_shapes=list(scratch) + [pltpu.SemaphoreType.DMA(side.sems), pltpu.SemaphoreType.DMA(side.sems)],
        compiler_params=params)
    return lambda *args: call(*args, *side.ins)


def _row_tile(m):
    best = 32
    for t in range(32, 641, 32):
        if m % t == 0:
            best = t
    return best


def _pad_rows(rows, nex, tp):
    m = rows < PAD
    for b in range(1, nex):
        m = m | ((rows >= b * tp) & (rows < b * tp + PAD))
    return m


def _ln_stats(x):
    mu = jnp.mean(x, axis=-1, keepdims=True)
    xc = x - mu
    var = jnp.mean(xc * xc, axis=-1, keepdims=True)
    rstd = lax.rsqrt(var + EPS)
    return xc * rstd, rstd


def _ln_bwd(dy, xhat, rstd, g):
    dxh = dy * g
    m1 = jnp.mean(dxh, axis=-1, keepdims=True)
    m2 = jnp.mean(dxh * xhat, axis=-1, keepdims=True)
    return rstd * (dxh - m1 - xhat * m2)


def _dot(a, b):
    return jnp.dot(a, b, preferred_element_type=F32)


def _dot_nt(a, b):
    return lax.dot_general(a, b, (((1,), (1,)), ((), ())), preferred_element_type=F32)


def _dot_tn(a, b):
    return lax.dot_general(a, b, (((0,), (0,)), ((), ())), preferred_element_type=F32)


def _sigmoid(x):
    return 1.0 / (1.0 + jnp.exp(-x))


def _bucket_np(d):
    n = np.maximum(d, 0)
    max_exact = NBUCKET // 2
    nf = np.maximum(n, 1).astype(np.float32)
    large = max_exact + (np.log(nf / np.float32(max_exact)) / np.float32(math.log(MAXDIST / max_exact))
                         * np.float32(NBUCKET - max_exact)).astype(np.int32)
    large = np.minimum(large, NBUCKET - 1)
    return np.where(n < max_exact, n, large).astype(np.int32)


def _bias_index():
    i = np.arange(BLK)[:, None]
    j = np.arange(2 * BLK)[None, :]
    d = BLK + i - j
    band_ok = (d >= 0) & (d < BLK)
    band = _bucket_np(d)
    idx = np.full((3, BLK, NKEY), -1, np.int32)
    m = np.arange(N_META)[None, :]
    d0 = (i - PAD) - m
    idx[0, :, 2 * BLK:2 * BLK + N_META] = np.where(d0 >= 0, _bucket_np(d0), -1)
    ok1 = band_ok & (j >= BLK)
    idx[1, :, :2 * BLK] = np.where(ok1, band, -1)
    idx[1, :, 2 * BLK:2 * BLK + N_META] = _bucket_np((N_META + i) - m)
    idx[2, :, :2 * BLK] = np.where(band_ok, band, -1)
    idx[2, :, 2 * BLK:2 * BLK + N_META] = NBUCKET - 1
    return idx


def _bias_build(rel_bias):
    idx = jnp.asarray(_bias_index())

    def body(idx_ref, rb_ref, o_ref):
        ix = idx_ref[...]
        for h in range(NQ):
            acc = jnp.full(ix.shape, NEG, F32)
            for b in range(NBUCKET):
                acc = jnp.where(ix == b, rb_ref[b, h], acc)
            o_ref[:, h, :, :] = acc

    return pl.pallas_call(
        body, name="bias_build", out_shape=jax.ShapeDtypeStruct((3, NQ, BLK, NKEY), F32),
        in_specs=[pl.BlockSpec(memory_space=pltpu.VMEM), pl.BlockSpec(memory_space=pltpu.SMEM)],
        out_specs=pl.BlockSpec(memory_space=pltpu.VMEM))(idx, rel_bias)


def _bias_grad(dbias):
    idx = jnp.asarray(_bias_index())

    def body(idx_ref, d_ref, o_ref):
        d = jnp.sum(d_ref[...], axis=0)
        for b in range(NBUCKET):
            acc = jnp.zeros((NQ, NKEY), F32)
            for case in range(3):
                hit = (idx_ref[case] == b)[None, :, :]
                acc = acc + jnp.sum(jnp.where(hit, d[case], 0.0), axis=1)
            o_ref[b] = jnp.sum(acc, axis=-1, keepdims=True)

    out = pl.pallas_call(
        body, name="bias_grad", out_shape=jax.ShapeDtypeStruct((NBUCKET, NQ, 1), F32),
        in_specs=[pl.BlockSpec(memory_space=pltpu.VMEM), pl.BlockSpec(memory_space=pltpu.VMEM)],
        out_specs=pl.BlockSpec(memory_space=pltpu.VMEM))(idx, dbias)
    return out.reshape(NBUCKET, NQ)


def _embed_ln(x, meta, g, b, nblk, side=None):
    nex, seq, _ = x.shape
    m = nex * nblk * BLK

    def body(x_ref, meta_ref, g_ref, b_ref, raw_ref, h_ref, hb_ref):
        j = pl.program_id(1)

        @pl.when(j == 0)
        def _():
            raw_ref[0:PAD, :] = jnp.zeros((PAD, D), F32)
            raw_ref[PAD:BLK, :] = meta_ref[...]

        @pl.when(j > 0)
        def _():
            raw_ref[...] = x_ref[...]

        xhat, _ = _ln_stats(raw_ref[...])
        y = xhat * g_ref[...] + b_ref[...]
        h_ref[...] = y
        hb_ref[...] = y.astype(BF16)

    row = lambda bb, j: (bb * nblk + j, 0)
    return _call(
        body, "embed_ln",
        (jax.ShapeDtypeStruct((m, D), F32), jax.ShapeDtypeStruct((m, D), F32), jax.ShapeDtypeStruct((m, D), BF16)),
        (nex, nblk),
        [pl.BlockSpec((None, BLK, D), lambda bb, j: (bb, jnp.maximum(j - 1, 0), 0)),
         pl.BlockSpec((N_META, D), lambda bb, j: (0, 0)),
         pl.BlockSpec((1, D), lambda bb, j: (0, 0)), pl.BlockSpec((1, D), lambda bb, j: (0, 0))],
        (pl.BlockSpec((BLK, D), row), pl.BlockSpec((BLK, D), row), pl.BlockSpec((BLK, D), row)), side=side,
    )(x, meta, g.reshape(1, D), b.reshape(1, D))


def _mm_bias(name, a, w, bias, tn, tm, side=None):
    m, k = a.shape
    n = w.shape[1]

    def body(a_ref, w_ref, b_ref, o_ref):
        o_ref[...] = _dot(a_ref[...], w_ref[...]) + b_ref[...]

    return _call(body, name, jax.ShapeDtypeStruct((m, n), F32), (n // tn, m // tm),
                 [pl.BlockSpec((tm, k), lambda j, i: (i, 0)), pl.BlockSpec((k, tn), lambda j, i: (0, j)),
                  pl.BlockSpec((1, tn), lambda j, i: (0, j))],
                 pl.BlockSpec((tm, tn), lambda j, i: (i, j)), side=side)(a, w, bias)


def _ffn_up_act(a, w4, cw, cb, tm, nex, tp, side=None):
    m, k = a.shape
    ffs = w4.shape[2]

    def body(a_ref, wu_ref, wg_ref, cu_ref, cg_ref, bu_ref, bg_ref, up_ref, ug_ref, act_ref, win):
        i = pl.program_id(1)

        @pl.when(i == 0)
        def _():
            win[:, 0:FHALO, :] = jnp.zeros((2, FHALO, ffs), F32)

        rows = i * tm + lax.broadcasted_iota(jnp.int32, (tm, 1), 0)
        pad = _pad_rows(rows, nex, tp)
        av = a_ref[...]
        for p, w_ref in ((0, wu_ref), (1, wg_ref)):
            x = jnp.where(pad, 0.0, _dot(av, w_ref[...]))
            win[p, FHALO:FHALO + tm, :] = x
            up_ref[p] = x.astype(BF16)
        for r0 in range(0, tm, RCH):
            for c0, c1 in _lane_groups(ffs):
                u = _conv3(win, 0, r0, c0, c1, cu_ref, bu_ref)
                g = _conv3(win, 1, r0, c0, c1, cg_ref, bg_ref)
                ug_ref[0, r0:r0 + RCH, c0:c1] = u.astype(BF16)
                ug_ref[1, r0:r0 + RCH, c0:c1] = g.astype(BF16)
                act_ref[r0:r0 + RCH, c0:c1] = (g * (0.5 * (1.0 + lax.erf(g * (1.0 / math.sqrt(2.0))))) * u).astype(BF16)
        win[:, 0:FHALO, :] = win[:, tm:tm + FHALO, :]

    wide = jax.ShapeDtypeStruct((2, m, 2 * ffs), BF16)
    return _call(body, "ffn_up_act", (wide, wide, jax.ShapeDtypeStruct((m, 2 * ffs), BF16)),
                 (2, m // tm),
                 [pl.BlockSpec((tm, k), lambda c, i: (i, 0)),
                  pl.BlockSpec((None, k, ffs), lambda c, i: (c, 0, 0)), pl.BlockSpec((None, k, ffs), lambda c, i: (c + 2, 0, 0)),
                  pl.BlockSpec((FTAPS, ffs), lambda c, i: (0, c)), pl.BlockSpec((FTAPS, ffs), lambda c, i: (0, c + 2)),
                  pl.BlockSpec((1, ffs), lambda c, i: (0, c)), pl.BlockSpec((1, ffs), lambda c, i: (0, c + 2))],
                 (pl.BlockSpec((2, tm, ffs), lambda c, i: (0, i, c)), pl.BlockSpec((2, tm, ffs), lambda c, i: (0, i, c)),
                  pl.BlockSpec((tm, ffs), lambda c, i: (i, c))),
                 scratch=[pltpu.VMEM((2, FHALO + tm, ffs), F32)], side=side)(a, w4, w4, cw, cw, cb, cb)


def _fill_kv(ks, vs, e, prev_ref, cur_ref, meta_ref):
    for piece, lo, n in ((prev_ref, 0, BLK), (cur_ref, BLK, BLK), (meta_ref, 2 * BLK, N_META)):
        val = piece[e]
        for hk in range(NKV):
            ks[e, hk, lo:lo + n, :] = val[:, hk * HD:(hk + 1) * HD].astype(BF16)
            vs[e, hk, lo:lo + n, :] = val[:, KVW + hk * HD:KVW + (hk + 1) * HD].astype(BF16)
    for hk in range(NKV):
        ks[e, hk, 2 * BLK + N_META:NKEY, :] = jnp.zeros((BLK - N_META, HD), BF16)
        vs[e, hk, 2 * BLK + N_META:NKEY, :] = jnp.zeros((BLK - N_META, HD), BF16)


def _interleave(chains):
    live = list(chains)
    while live:
        for c in list(live):
            try:
                next(c)
            except StopIteration:
                live.remove(c)


def _head_softmax(q, ks_hk, bias_ref, sink_ref, h, out):
    qh = q[:, h * HD:(h + 1) * HD].astype(BF16)
    yield
    s = _dot_nt(qh, ks_hk) * (HD ** -0.5) + bias_ref[h]
    yield
    sink = sink_ref[0, h]
    mx = jnp.maximum(jnp.max(s, axis=-1, keepdims=True), sink)
    yield
    p = jnp.exp(s - mx)
    es = jnp.exp(sink - mx)
    yield
    inv = 1.0 / (jnp.sum(p, axis=-1, keepdims=True) + es)
    yield
    out["q"], out["p"], out["sink"] = qh, p * inv, es * inv
    yield


def _attn_specs(nex, blk_of):
    qcol, kvcol = (C_QKV) // AW, (C_QKV + AW) // (2 * KVW)
    return [
        pl.BlockSpec((nex, BLK, AW), lambda j: (0, blk_of(j), qcol)),
        pl.BlockSpec((nex, BLK, 2 * KVW), lambda j: (0, blk_of(j), kvcol)),
        pl.BlockSpec((nex, BLK, 2 * KVW), lambda j: (0, jnp.maximum(blk_of(j) - 1, 0), kvcol)),
        pl.BlockSpec((nex, N_META, 2 * KVW), lambda j: (0, PAD // N_META, kvcol)),
        pl.BlockSpec((None, NQ, BLK, NKEY), lambda j: (jnp.minimum(blk_of(j), 2), 0, 0, 0)),
        pl.BlockSpec(memory_space=pltpu.SMEM),
    ]


def _attn_fwd(z, bias, sinks, nex, nblk, side=None):
    m = z.shape[0]
    z3 = z.reshape(nex, nblk * BLK, z.shape[1])

    def head(e, h, q, ks, vs, bias_ref, sink_ref, oacc):
        out = {}
        yield from _head_softmax(q, ks[e, h // GRP], bias_ref, sink_ref, h, out)
        o = _dot(out["p"].astype(BF16), vs[e, h // GRP])
        yield
        oacc[e, :, h * HD:(h + 1) * HD] = o
        yield

    def body(q_ref, cur_ref, prev_ref, meta_ref, bias_ref, sink_ref, o_ref, ks, vs, oacc):
        for e in range(nex):
            _fill_kv(ks, vs, e, prev_ref, cur_ref, meta_ref)
        _interleave([head(e, h, q_ref[e], ks, vs, bias_ref, sink_ref, oacc) for e in range(nex) for h in range(NQ)])
        for e in range(nex):
            o_ref[e] = oacc[e].astype(BF16)

    res = _call(body, "attn_fwd", jax.ShapeDtypeStruct((nex, nblk * BLK, AW), BF16), (nblk,),
                _attn_specs(nex, lambda j: j),
                pl.BlockSpec((nex, BLK, AW), lambda j: (0, j, 0)),
                scratch=[pltpu.VMEM((nex, NKV, NKEY, HD), BF16), pltpu.VMEM((nex, NKV, NKEY, HD), BF16),
                         pltpu.VMEM((nex, BLK, AW), F32)], side=side)(z3, z3, z3, z3, bias, sinks.reshape(1, NQ))
    if side is None:
        return res.reshape(m, AW)
    return (res[0].reshape(m, AW),) + tuple(res[1:])


def _cgate(cv, cg, rows, nex, tp):
    return jnp.where(_pad_rows(rows, nex, tp), 0.0, cv * _sigmoid(cg))


CLANES = 256


def _rolled_up(blk, b):
    return blk if b == 0 else pltpu.roll(blk, blk.shape[0] - b, axis=0)


def _conv_fwd(z, w, wb, g, b, nex, tp, side=None):
    m = z.shape[0]
    tm = BLK
    sub = tm // CHALO
    cvc, cgc = C_CONV // CW, C_CONV // CW + 1

    def body(cv_ref, cg_ref, cvh_ref, cgh_ref, w_ref, wb_ref, g_ref, b_ref, cc_ref, cs_ref, win):
        i = pl.program_id(0)
        rows = i * tm + lax.broadcasted_iota(jnp.int32, (tm, 1), 0)
        hrows = i * tm - CHALO + lax.broadcasted_iota(jnp.int32, (CHALO, 1), 0)
        win[0:CHALO, :] = _cgate(cvh_ref[...], cgh_ref[...], hrows, nex, tp)
        win[CHALO:CHALO + tm, :] = _cgate(cv_ref[...], cg_ref[...], rows, nex, tp)
        for sb in range(sub):
            lo = sb * CHALO
            for c0 in range(0, CW, CLANES):
                blk = win[lo:lo + 2 * CHALO, c0:c0 + CLANES]
                acc = jnp.zeros((CHALO, CLANES), F32) + wb_ref[:, c0:c0 + CLANES]
                for b in range(8):
                    rb = _rolled_up(blk, b)
                    for a in range(5):
                        s = 8 * a + b
                        if 2 <= s <= CTAPS + 1:
                            acc = acc + w_ref[s - 2:s - 1, c0:c0 + CLANES] * rb[8 * a:8 * a + CHALO]
                cc_ref[lo:lo + CHALO, c0:c0 + CLANES] = acc
        xhat, _ = _ln_stats(cc_ref[...])
        cl = xhat * g_ref[...] + b_ref[...]
        cs_ref[...] = (cl * _sigmoid(cl)).astype(BF16)

    halo = lambda i: jnp.maximum(i * sub - 1, 0)
    vec = pl.BlockSpec((1, CW), lambda i: (0, 0))
    return _call(body, "conv_fwd", (jax.ShapeDtypeStruct((m, CW), F32), jax.ShapeDtypeStruct((m, CW), BF16)),
                 (m // tm,),
                 [pl.BlockSpec((tm, CW), lambda i: (i, cvc)), pl.BlockSpec((tm, CW), lambda i: (i, cgc)),
                  pl.BlockSpec((CHALO, CW), lambda i: (halo(i), cvc)), pl.BlockSpec((CHALO, CW), lambda i: (halo(i), cgc)),
                  pl.BlockSpec((CTAPS, CW), lambda i: (0, 0)), vec, vec, vec],
                 (pl.BlockSpec((tm, CW), lambda i: (i, 0)), pl.BlockSpec((tm, CW), lambda i: (i, 0))),
                 scratch=[pltpu.VMEM((CHALO + tm, CW), F32)], side=side)(z, z, z, z, w, wb.reshape(1, CW), g.reshape(1, CW), b.reshape(1, CW))


def _mix_fwd(a, cs, wap4, wcp4, z, tm, side=None):
    m = a.shape[0]
    ns = wap4.shape[2]

    def body(a_ref, cs_ref, wa_ref, wc_ref, ga_ref, gc_ref, ya_ref, yc_ref, mix_ref):
        av, cv = a_ref[...], cs_ref[...]
        for j in range(NCHIP):
            cols = slice(j * ns, (j + 1) * ns)
            ya, yc = _dot(av, wa_ref[j]), _dot(cv, wc_ref[j])
            ya_ref[:, cols] = ya.astype(BF16)
            yc_ref[:, cols] = yc.astype(BF16)
            mix_ref[:, cols] = (_sigmoid(ga_ref[:, cols]) * ya + _sigmoid(gc_ref[:, cols]) * yc).astype(BF16)

    wspec = pl.BlockSpec((NCHIP, AW, ns), lambda i: (0, 0, 0))
    row = lambda i: (i, 0)
    return _call(body, "mix_fwd",
                 (jax.ShapeDtypeStruct((m, D), BF16), jax.ShapeDtypeStruct((m, D), BF16), jax.ShapeDtypeStruct((m, D), BF16)),
                 (m // tm,),
                 [pl.BlockSpec((tm, AW), row), pl.BlockSpec((tm, CW), row), wspec, wspec,
                  pl.BlockSpec((tm, D), lambda i: (i, 0)), pl.BlockSpec((tm, D), lambda i: (i, 1))],
                 (pl.BlockSpec((tm, D), row), pl.BlockSpec((tm, D), row), pl.BlockSpec((tm, D), row)), side=side)(a, cs, wap4, wcp4, z, z)


def _mm_res_ln(name, a, w, res, g, b, tm, side=None):
    m, k = a.shape

    def body(a_ref, w_ref, res_ref, g_ref, b_ref, r_ref, h_ref, hb_ref):
        r = ALPHA * res_ref[...] + _dot(a_ref[...], w_ref[...])
        r_ref[...] = r
        xhat, _ = _ln_stats(r)
        y = xhat * g_ref[...] + b_ref[...]
        h_ref[...] = y
        hb_ref[...] = y.astype(BF16)

    row = lambda i: (i, 0)
    vec = pl.BlockSpec((1, D), lambda i: (0, 0))
    return _call(body, name,
                 (jax.ShapeDtypeStruct((m, D), F32), jax.ShapeDtypeStruct((m, D), F32), jax.ShapeDtypeStruct((m, D), BF16)),
                 (m // tm,),
                 [pl.BlockSpec((tm, k), row), pl.BlockSpec((k, D), lambda i: (0, 0)), pl.BlockSpec((tm, D), row), vec, vec],
                 (pl.BlockSpec((tm, D), row), pl.BlockSpec((tm, D), row), pl.BlockSpec((tm, D), row)), side=side)(a, w, res, g.reshape(1, D), b.reshape(1, D))


RCH = 16


def _lane_groups(width, most=768):
    n = -(-width // most)
    step = -(-width // (128 * n)) * 128
    return [(c, min(c + step, width)) for c in range(0, width, step)]


def _conv3(win, p, r0, c0, c1, w_ref, b_ref):
    blk = win[p, r0:r0 + FHALO + RCH, c0:c1]
    x0, x1, x2 = blk[FHALO:], pltpu.roll(blk, 1, axis=0)[FHALO:], pltpu.roll(blk, 2, axis=0)[FHALO:]
    return b_ref[:, c0:c1] + w_ref[0:1, c0:c1] * x2 + w_ref[1:2, c0:c1] * x1 + w_ref[2:3, c0:c1] * x0


def _loss_grad(y, target, nblk):
    nex = target.shape[0]
    m = y.shape[0]

    def body(y_ref, t_ref, dy_ref, acc_ref):
        bb, j = pl.program_id(0), pl.program_id(1)

        @pl.when((bb == 0) & (j == 0))
        def _():
            acc_ref[...] = jnp.zeros_like(acc_ref)

        @pl.when(j == 0)
        def _():
            dy_ref[...] = jnp.zeros_like(dy_ref)

        @pl.when(j > 0)
        def _():
            e = y_ref[...] - t_ref[...]
            dy_ref[...] = e * (1.0 / D)
            acc_ref[...] += jnp.sum((e * e).reshape(BLK // 8, 8, D), axis=0)

    return _call(body, "loss_grad", (jax.ShapeDtypeStruct((m, D), F32), jax.ShapeDtypeStruct((8, D), F32)), (nex, nblk),
                 [pl.BlockSpec((BLK, D), lambda bb, j: (bb * nblk + j, 0)),
                  pl.BlockSpec((None, BLK, D), lambda bb, j: (bb, jnp.maximum(j - 1, 0), 0))],
                 (pl.BlockSpec((BLK, D), lambda bb, j: (bb * nblk + j, 0)), pl.BlockSpec((8, D), lambda bb, j: (0, 0))))(y, target)


def _ln_bwd_call(name, dy, r, g, tm, a_list=(), w=None, cols=None, side=None):
    m = dy.shape[0]
    na = len(a_list)

    def body(*refs):
        dy_ref, r_ref, g_ref = refs[0:3]
        a_refs = refs[3:3 + na]
        w_ref = refs[3 + na] if na else None
        dr_ref, drb_ref, dg_ref, db_ref = refs[-4:]
        i = pl.program_id(0)
        dh = dy_ref[...]
        if na:
            dh = ALPHA * dh
            if cols is None:
                ns = w.shape[2]
                for j in range(NCHIP):
                    dh = dh + _dot_nt(a_refs[0][j // 2, :, (j % 2) * ns:(j % 2 + 1) * ns], w_ref[j])
            else:
                for a_ref, (c0, c1) in zip(a_refs, cols):
                    dh = dh + _dot_nt(a_ref[...], w_ref[:, c0:c1])
        xhat, rstd = _ln_stats(r_ref[...])
        dr = _ln_bwd(dh, xhat, rstd, g_ref[...])
        dr_ref[...] = dr
        drb_ref[...] = dr.astype(BF16)

        @pl.when(i == 0)
        def _():
            dg_ref[...] = jnp.zeros_like(dg_ref)
            db_ref[...] = jnp.zeros_like(db_ref)

        dg_ref[...] += jnp.sum(dh * xhat, axis=0, keepdims=True)
        db_ref[...] += jnp.sum(dh, axis=0, keepdims=True)

    row = lambda i: (i, 0)
    vec = pl.BlockSpec((1, D), lambda i: (0, 0))
    in_specs = [pl.BlockSpec((tm, D), row), pl.BlockSpec((tm, D), row), vec]
    for a in a_list:
        in_specs.append(pl.BlockSpec((2, tm, a.shape[2]), lambda i: (0, i, 0)) if a.ndim == 3 else pl.BlockSpec((tm, a.shape[1]), row))
    if na:
        in_specs.append(pl.BlockSpec(w.shape, (lambda i: (0, 0, 0)) if w.ndim == 3 else (lambda i: (0, 0))))
    return _call(body, name,
                 (jax.ShapeDtypeStruct((m, D), F32), jax.ShapeDtypeStruct((m, D), BF16),
                  jax.ShapeDtypeStruct((1, D), F32), jax.ShapeDtypeStruct((1, D), F32)),
                 (m // tm,), in_specs,
                 (pl.BlockSpec((tm, D), row), pl.BlockSpec((tm, D), row), vec, vec),
                 side=side)(dy, r, g.reshape(1, D), *a_list, *([w] if na else []))


def _ffn_bwd(drb, wdown, ug, up3, w, tm, nex, tp, side=None):
    _, m, dff = ug.shape
    ffs = dff // 2
    nt = m // tm

    def body(dr_ref, wd_ref, ug_ref, x_ref, wu_ref, wg_ref, o_ref, dw_ref, db_ref, dact, carry, dwacc, dbacc):
        i = pl.program_id(1)
        tile = nt - 1 - i
        dact[...] = _dot_nt(dr_ref[...], wd_ref[...])
        dwacc[...] = jnp.zeros_like(dwacc)
        dbacc[...] = jnp.zeros_like(dbacc)

        @pl.when(i == 0)
        def _():
            carry[...] = jnp.zeros_like(carry)
            dw_ref[...] = jnp.zeros_like(dw_ref)
            db_ref[...] = jnp.zeros_like(db_ref)

        fold = lambda t: t[0:8, :] + t[8:16, :]
        for r0 in reversed(range(0, tm, RCH)):
            rows = tile * tm + r0 + lax.broadcasted_iota(jnp.int32, (RCH, 1), 0)
            pad = _pad_rows(rows, nex, tp)
            for c0, c1 in _lane_groups(ffs, 384):
                u = ug_ref[0, r0:r0 + RCH, c0:c1].astype(F32)
                g = ug_ref[1, r0:r0 + RCH, c0:c1].astype(F32)
                da = dact[r0:r0 + RCH, c0:c1]
                cdf = 0.5 * (1.0 + lax.erf(g * (1.0 / math.sqrt(2.0))))
                pdf = jnp.exp(-0.5 * g * g) * (1.0 / math.sqrt(2.0 * math.pi))
                for p, w_ref, d0 in ((0, wu_ref, da * (g * cdf)), (1, wg_ref, da * u * (cdf + g * pdf))):
                    dblk = jnp.concatenate([d0, carry[p, :, c0:c1]], axis=0)
                    d1 = pltpu.roll(dblk, RCH + FHALO - 1, axis=0)[:RCH]
                    d2 = pltpu.roll(dblk, RCH + FHALO - 2, axis=0)[:RCH]
                    carry[p, :, c0:c1] = d0[0:FHALO]
                    dpre = w_ref[2:3, c0:c1] * d0 + w_ref[1:2, c0:c1] * d1 + w_ref[0:1, c0:c1] * d2
                    o_ref[p, r0:r0 + RCH, c0:c1] = jnp.where(pad, 0.0, dpre).astype(BF16)
                    x0 = x_ref[p, r0:r0 + RCH, c0:c1].astype(F32)
                    dwacc[p, 2, :, c0:c1] += fold(d0 * x0)
                    dwacc[p, 1, :, c0:c1] += fold(d1 * x0)
                    dwacc[p, 0, :, c0:c1] += fold(d2 * x0)
                    dbacc[p, :, c0:c1] += fold(d0)
        for p in range(2):
            for k in range(FTAPS):
                dw_ref[p, k:k + 1, :] += jnp.sum(dwacc[p, k], axis=0, keepdims=True)
            db_ref[p] += jnp.sum(dbacc[p], axis=0, keepdims=True)

    wide = pl.BlockSpec((2, tm, ffs), lambda c, i: (0, nt - 1 - i, c))
    return _call(body, "ffn_bwd",
                 (jax.ShapeDtypeStruct((2, m, dff), BF16), jax.ShapeDtypeStruct((2, FTAPS, dff), F32),
                  jax.ShapeDtypeStruct((2, 1, dff), F32)),
                 (2, nt),
                 [pl.BlockSpec((tm, D), lambda c, i: (nt - 1 - i, 0)), pl.BlockSpec((ffs, D), lambda c, i: (c, 0)), wide, wide,
                  pl.BlockSpec((FTAPS, ffs), lambda c, i: (0, c)), pl.BlockSpec((FTAPS, ffs), lambda c, i: (0, c + 2))],
                 (wide, pl.BlockSpec((2, FTAPS, ffs), lambda c, i: (0, 0, c)), pl.BlockSpec((2, 1, ffs), lambda c, i: (0, 0, c))),
                 scratch=[pltpu.VMEM((tm, ffs), F32), pltpu.VMEM((2, FHALO, ffs), F32),
                          pltpu.VMEM((2, FTAPS, 8, ffs), F32), pltpu.VMEM((2, 8, ffs), F32)], side=side)(drb, wdown, ug, up3, w, w)


def _mm_tn(name, a, b, tk, tn, b_cols=None, chip_out=False, side=None):
    m, k = a.shape
    n = b.shape[-1] * (2 if b.ndim == 3 else 1)

    def body(a_ref, b_ref, o_ref):
        o_ref[...] = _dot_tn(a_ref[...], b_ref[...])

    if b.ndim == 3:
        bspec = pl.BlockSpec((None, m, tn), lambda kk, j: (b_cols(j)[0], 0, b_cols(j)[1]))
    else:
        bspec = pl.BlockSpec((m, tn), lambda kk, j: (0, j))
    if chip_out:
        oshape, ospec = (n // tn, k, tn), pl.BlockSpec((None, tk, tn), lambda kk, j: (j, kk, 0))
    else:
        oshape, ospec = (k, n), pl.BlockSpec((tk, tn), lambda kk, j: (kk, j))
    return _call(body, name, jax.ShapeDtypeStruct(oshape, F32), (k // tk, n // tn),
                 [pl.BlockSpec((m, tk), lambda kk, j: (0, kk)), bspec], ospec, side=side)(a, b)


def _gate_bwd(drb, wout, ya, yc, z, tm):
    m = drb.shape[0]

    def body(dr_ref, w_ref, ya_ref, yc_ref, ga_ref, gc_ref, dya_ref, dyc_ref, dz_ref, cs_ref):
        i = pl.program_id(0)
        dmix = _dot_nt(dr_ref[...], w_ref[...])
        sa, sc = _sigmoid(ga_ref[...]), _sigmoid(gc_ref[...])
        dya_ref[...] = (dmix * sa).astype(BF16)
        dyc_ref[...] = (dmix * sc).astype(BF16)
        dga = dmix * ya_ref[...].astype(F32) * sa * (1.0 - sa)
        dgc = dmix * yc_ref[...].astype(F32) * sc * (1.0 - sc)
        dz_ref[:, 0:D] = dga.astype(BF16)
        dz_ref[:, D:2 * D] = dgc.astype(BF16)

        @pl.when(i == 0)
        def _():
            cs_ref[...] = jnp.zeros_like(cs_ref)

        cs_ref[:, 0:D] += jnp.sum(dga, axis=0, keepdims=True)
        cs_ref[:, D:2 * D] += jnp.sum(dgc, axis=0, keepdims=True)

    row = lambda i: (i, 0)
    return _call(body, "gate_bwd",
                 (jax.ShapeDtypeStruct((m, D), BF16), jax.ShapeDtypeStruct((m, D), BF16),
                  jax.ShapeDtypeStruct((m, 2 * D), BF16), jax.ShapeDtypeStruct((1, 2 * D), F32)),
                 (m // tm,),
                 [pl.BlockSpec((tm, D), row), pl.BlockSpec((D, D), lambda i: (0, 0)), pl.BlockSpec((tm, D), row),
                  pl.BlockSpec((tm, D), row), pl.BlockSpec((tm, D), lambda i: (i, 0)), pl.BlockSpec((tm, D), lambda i: (i, 1))],
                 (pl.BlockSpec((tm, D), row), pl.BlockSpec((tm, D), row), pl.BlockSpec((tm, 2 * D), row),
                  pl.BlockSpec((1, 2 * D), lambda i: (0, 0))))(drb, wout, ya, yc, z, z)


def _conv_bwd_a(dyc, wcp4, cc, g, b, tm):
    m = cc.shape[0]
    ns = wcp4.shape[2]

    def body(dy_ref, w_ref, cc_ref, g_ref, b_ref, dcc_ref, dg_ref, db_ref, dwb_ref):
        i = pl.program_id(0)
        dcs = jnp.zeros((tm, CW), F32)
        for j in range(NCHIP):
            dcs = dcs + _dot_nt(dy_ref[:, j * ns:(j + 1) * ns], w_ref[j])
        xhat, rstd = _ln_stats(cc_ref[...])
        cl = xhat * g_ref[...] + b_ref[...]
        sg = _sigmoid(cl)
        dcl = dcs * sg * (1.0 + cl * (1.0 - sg))
        dcc = _ln_bwd(dcl, xhat, rstd, g_ref[...])
        dcc_ref[...] = dcc

        @pl.when(i == 0)
        def _():
            dg_ref[...] = jnp.zeros_like(dg_ref)
            db_ref[...] = jnp.zeros_like(db_ref)
            dwb_ref[...] = jnp.zeros_like(dwb_ref)

        dg_ref[...] += jnp.sum(dcl * xhat, axis=0, keepdims=True)
        db_ref[...] += jnp.sum(dcl, axis=0, keepdims=True)
        dwb_ref[...] += jnp.sum(dcc, axis=0, keepdims=True)

    row = lambda i: (i, 0)
    vec = pl.BlockSpec((1, CW), lambda i: (0, 0))
    v = jax.ShapeDtypeStruct((1, CW), F32)
    return _call(body, "conv_bwd_a", (jax.ShapeDtypeStruct((m, CW), F32), v, v, v), (m // tm,),
                 [pl.BlockSpec((tm, D), row), pl.BlockSpec((NCHIP, CW, ns), lambda i: (0, 0, 0)), pl.BlockSpec((tm, CW), row), vec, vec],
                 (pl.BlockSpec((tm, CW), row), vec, vec, vec))(dyc, wcp4, cc, g.reshape(1, CW), b.reshape(1, CW))


def _conv_bwd_b(dcc, z, w, nex, tp, side=None):
    m = dcc.shape[0]
    tm = BLK
    sub = tm // CHALO
    nt = m // tm
    cvc, cgc = C_CONV // CW, C_CONV // CW + 1

    def body(d_ref, dh_ref, cv_ref, cg_ref, w_ref, dz_ref, dw_ref, cs_ref, dwin, dwacc):
        i = pl.program_id(0)
        rows = i * tm + lax.broadcasted_iota(jnp.int32, (tm, 1), 0)
        dwin[0:tm, :] = d_ref[...]
        dwin[tm:tm + CHALO, :] = jnp.where(i == nt - 1, 0.0, dh_ref[...])

        @pl.when(i == 0)
        def _():
            dwacc[...] = jnp.zeros_like(dwacc)
            cs_ref[...] = jnp.zeros_like(cs_ref)

        fold = lambda t: (t[0:8] + t[8:16]) + (t[16:24] + t[24:32])
        for sb in range(sub):
            lo = sb * CHALO
            pad = _pad_rows(rows[lo:lo + CHALO], nex, tp)
            for c0 in range(0, CW, CLANES):
                cs_ = slice(c0, c0 + CLANES)
                cv = cv_ref[lo:lo + CHALO, cs_]
                sg = _sigmoid(cg_ref[lo:lo + CHALO, cs_])
                cgin = jnp.where(pad, 0.0, cv * sg)
                blk = dwin[lo:lo + 2 * CHALO, cs_]
                acc = jnp.zeros((CHALO, CLANES), F32)
                for b in range(8):
                    rb = _rolled_up(blk, b)
                    for a in range(4):
                        s = 8 * a + b
                        if s <= CTAPS - 1:
                            k = CTAPS - 1 - s
                            sh = rb[8 * a:8 * a + CHALO]
                            acc = acc + w_ref[k:k + 1, cs_] * sh
                            dwacc[k, :, cs_] += fold(sh * cgin)
                dcg = jnp.where(pad, 0.0, acc)
                dcv = dcg * sg
                dgt = dcg * cv * sg * (1.0 - sg)
                dz_ref[lo:lo + CHALO, cs_] = dcv.astype(BF16)
                dz_ref[lo:lo + CHALO, CW + c0:CW + c0 + CLANES] = dgt.astype(BF16)
                cs_ref[:, cs_] += jnp.sum(dcv, axis=0, keepdims=True)
                cs_ref[:, CW + c0:CW + c0 + CLANES] += jnp.sum(dgt, axis=0, keepdims=True)

        @pl.when(i == nt - 1)
        def _():
            for k in range(CHALO):
                dw_ref[k:k + 1, :] = jnp.sum(dwacc[k], axis=0, keepdims=True)

    nxt = lambda i: jnp.minimum((i + 1) * sub, m // CHALO - 1)
    return _call(body, "conv_bwd_b",
                 (jax.ShapeDtypeStruct((m, 2 * CW), BF16), jax.ShapeDtypeStruct((CHALO, CW), F32),
                  jax.ShapeDtypeStruct((1, 2 * CW), F32)),
                 (nt,),
                 [pl.BlockSpec((tm, CW), lambda i: (i, 0)), pl.BlockSpec((CHALO, CW), lambda i: (nxt(i), 0)),
                  pl.BlockSpec((tm, CW), lambda i: (i, cvc)), pl.BlockSpec((tm, CW), lambda i: (i, cgc)),
                  pl.BlockSpec((CTAPS, CW), lambda i: (0, 0))],
                 (pl.BlockSpec((tm, 2 * CW), lambda i: (i, 0)), pl.BlockSpec((CHALO, CW), lambda i: (0, 0)),
                  pl.BlockSpec((1, 2 * CW), lambda i: (0, 0))),
                 scratch=[pltpu.VMEM((tm + CHALO, CW), F32), pltpu.VMEM((CHALO, 8, CW), F32)], side=side)(dcc, dcc, z, z, w)


def _attn_bwd(z, bias, sinks, dya, wap4, nex, nblk, side=None):
    m = z.shape[0]
    tp = nblk * BLK
    ns = wap4.shape[2]
    blk_of = lambda j: nblk - 1 - j
    z3 = z.reshape(nex, tp, z.shape[1])
    dya3 = dya.reshape(nex, tp, D)

    def body(q_ref, cur_ref, prev_ref, meta_ref, bias_ref, sink_ref, dy_ref, w_ref,
             dz_ref, cs_ref, dsk_ref, dbias_ref, ks, vs, carry, macc, dqacc, dkv, okv):
        j = pl.program_id(0)
        n = nblk - 1 - j

        @pl.when(j == 0)
        def _():
            carry[...] = jnp.zeros_like(carry)
            macc[...] = jnp.zeros_like(macc)
            cs_ref[...] = jnp.zeros_like(cs_ref)
            dsk_ref[...] = jnp.zeros_like(dsk_ref)

        @pl.when((j == 0) | (n <= 1))
        def _():
            dbias_ref[...] = jnp.zeros_like(dbias_ref)

        lane = lax.broadcasted_iota(jnp.int32, (1, BLK), 1)
        dsk = jnp.zeros((1, BLK), F32)

        def head(e, h, q, da, out):
            hk = h // GRP
            yield from _head_softmax(q, ks[e, hk], bias_ref, sink_ref, h, out)
            pn = out["p"]
            doh = da[:, h * HD:(h + 1) * HD].astype(BF16)
            yield
            dp = _dot_nt(doh, vs[e, hk])
            yield
            dl = jnp.sum(pn * dp, axis=-1, keepdims=True)
            yield
            ds = pn * (dp - dl)
            yield
            dbias_ref[h] += ds
            out["dsink"] = jnp.sum(-out["sink"] * dl)
            yield
            dsb = (ds * (HD ** -0.5)).astype(BF16)
            yield
            dqacc[e, :, h * HD:(h + 1) * HD] = _dot(dsb, ks[e, hk])
            out["ds"], out["pb"], out["do"] = dsb, pn.astype(BF16), doh
            yield

        def kv_head(e, hk, outs):
            rows = lambda key: jnp.concatenate([outs[hk * GRP + g][key] for g in range(GRP)], axis=0)
            dk = _dot_tn(rows("ds"), rows("q"))
            yield
            dv = _dot_tn(rows("pb"), rows("do"))
            yield
            dkv[e, :, hk * HD:(hk + 1) * HD] = dk
            dkv[e, :, KVW + hk * HD:KVW + (hk + 1) * HD] = dv
            yield

        das, outs = [], [[{} for _ in range(NQ)] for _ in range(nex)]
        for e in range(nex):
            _fill_kv(ks, vs, e, prev_ref, cur_ref, meta_ref)
            da = jnp.zeros((BLK, AW), F32)
            for jj in range(NCHIP):
                da = da + _dot_nt(dy_ref[e, :, jj * ns:(jj + 1) * ns], w_ref[jj])
            das.append(da)
        _interleave([head(e, h, q_ref[e], das[e], outs[e][h]) for e in range(nex) for h in range(NQ)])
        _interleave([kv_head(e, hk, outs[e]) for e in range(nex) for hk in range(NKV)])
        for e in range(nex):
            for h in range(NQ):
                dsk = dsk + jnp.where(lane == h, outs[e][h]["dsink"], 0.0)
            macc[e] += dkv[e, 2 * BLK:2 * BLK + N_META, :]
            okv[e] = dkv[e, BLK:2 * BLK, :] + carry[e]
            carry[e] = dkv[e, 0:BLK, :]

            @pl.when(n == 0)
            def _():
                okv[e, PAD:BLK, :] += macc[e]

            dq = dqacc[e]
            ok = okv[e]
            dz_ref[e, :, 0:AW] = dq.astype(BF16)
            dz_ref[e, :, AW:AW + 2 * KVW] = ok.astype(BF16)
            cs_ref[:, 0:AW] += jnp.sum(dq, axis=0, keepdims=True)
            cs_ref[:, AW:AW + 2 * KVW] += jnp.sum(ok, axis=0, keepdims=True)
        dsk_ref[...] += dsk

    wz = AW + 2 * KVW
    specs = _attn_specs(nex, blk_of) + [
        pl.BlockSpec((nex, BLK, D), lambda j: (0, blk_of(j), 0)),
        pl.BlockSpec((NCHIP, AW, ns), lambda j: (0, 0, 0))]
    res = _call(body, "attn_bwd",
                (jax.ShapeDtypeStruct((nex, tp, wz), BF16), jax.ShapeDtypeStruct((1, wz), F32), jax.ShapeDtypeStruct((1, BLK), F32),
                 jax.ShapeDtypeStruct((1, 3, NQ, BLK, NKEY), F32)),
                (nblk,), specs,
                (pl.BlockSpec((nex, BLK, wz), lambda j: (0, blk_of(j), 0)), pl.BlockSpec((1, wz), lambda j: (0, 0)),
                 pl.BlockSpec((1, BLK), lambda j: (0, 0)),
                 pl.BlockSpec((None, None, NQ, BLK, NKEY), lambda j: (0, jnp.minimum(blk_of(j), 2), 0, 0, 0))),
                scratch=[pltpu.VMEM((nex, NKV, NKEY, HD), BF16), pltpu.VMEM((nex, NKV, NKEY, HD), BF16),
                         pltpu.VMEM((nex, BLK, 2 * KVW), F32), pltpu.VMEM((nex, N_META, 2 * KVW), F32),
                         pltpu.VMEM((nex, BLK, AW), F32), pltpu.VMEM((nex, NKEY, 2 * KVW), F32),
                         pltpu.VMEM((nex, BLK, 2 * KVW), F32)],
                side=side)(z3, z3, z3, z3, bias, sinks.reshape(1, NQ), dya3, wap4)
    return (res[0].reshape(m, wz),) + tuple(res[1:])


def _tile_rows(rows, cols, target_bytes=1 << 20):
    best = None
    for t in range(8, rows + 1, 8):
        if rows % t == 0 and t * cols * 4 <= target_bytes:
            best = t
    return best or rows


def _sum0(name, x):
    n, r, c = x.shape
    tr = _tile_rows(r, c * n)

    def body(x_ref, o_ref):
        acc = x_ref[0]
        for k in range(1, n):
            acc = acc + x_ref[k]
        o_ref[...] = acc

    return _call(body, name, jax.ShapeDtypeStruct((r, c), F32), (r // tr,),
                 [pl.BlockSpec((n, tr, c), lambda i: (0, i, 0))], pl.BlockSpec((tr, c), lambda i: (i, 0)))(x)


def _adamw(name, w, g, mom, vel):
    r, c = w.shape
    tr = _tile_rows(r, c)
    c1 = 1.0 / (1.0 - ADAM_B1 ** ADAM_STEP)
    c2 = 1.0 / (1.0 - ADAM_B2 ** ADAM_STEP)

    def body(w_ref, g_ref, m_ref, v_ref, d_ref, mo_ref, vo_ref):
        gg = g_ref[...]
        mn = ADAM_B1 * m_ref[...] + (1.0 - ADAM_B1) * gg
        vn = ADAM_B2 * v_ref[...] + (1.0 - ADAM_B2) * (gg * gg)
        mo_ref[...] = mn
        vo_ref[...] = vn
        d_ref[...] = -ADAM_LR * ((mn * c1) / (jnp.sqrt(vn * c2) + ADAM_EPS) + ADAM_WD * w_ref[...])

    spec = pl.BlockSpec((tr, c), lambda i: (i, 0))
    o = jax.ShapeDtypeStruct((r, c), F32)
    return _call(body, name, (o, o, o), (r // tr,), [spec] * 4, (spec, spec, spec))(w, g, mom, vel)


def _place():
    x, y, c = lax.axis_index("x"), lax.axis_index("y"), lax.axis_index("c")
    others = [(1 - x, y), (x, 1 - y), (1 - x, 1 - y)]
    return x, y, c, others


def _gather_job(items):
    nw = len(items)

    def views(s_ref, g_ref, layer, c):
        if layer is None:
            return s_ref.at[c], lambda chip, cc: g_ref.at[chip, cc]
        hr = s_ref.shape[1] // 2
        return s_ref.at[layer, pl.ds(c * hr, hr)], lambda chip, cc: g_ref.at[chip, pl.ds(cc * hr, hr)]

    def copies(s_refs, g_refs, send, recv):
        x, y, c, others = _place()
        chip = 2 * x + y
        firsts, arrive, passed, arrive2 = [], [], [], []
        for w, (_, layer) in enumerate(items):
            src, dst = views(s_refs[w], g_refs[w], layer, c)
            for k, (px, py) in enumerate(others):
                def rc(kk, s, d, to, w=w):
                    return pltpu.make_async_remote_copy(src_ref=s, dst_ref=d, send_sem=send.at[w, kk], recv_sem=recv.at[w, kk],
                                                        device_id=to, device_id_type=MESH)
                got, got2 = dst(2 * px + py, c), dst(2 * px + py, 1 - c)
                firsts.append(rc(k, src, dst(chip, c), (px, py, c)))
                arrive.append(rc(k, got, got, (x, y, c)))
                passed.append(rc(3 + k, got, got, (x, y, 1 - c)))
                arrive2.append(rc(3 + k, got2, got2, (x, y, c)))
        return firsts, arrive, passed, arrive2

    def start(s_refs, g_refs, send, recv):
        for cp in copies(s_refs, g_refs, send, recv)[0]:
            cp.start()

    def finish(s_refs, g_refs, send, recv):
        firsts, arrive, passed, arrive2 = copies(s_refs, g_refs, send, recv)
        for a, p in zip(arrive, passed):
            a.wait_recv()
            p.start()
        for a in arrive2:
            a.wait_recv()
        for cp in firsts + passed:
            cp.wait_send()

    outs = [jax.ShapeDtypeStruct((NCHIP,) + (s.shape if layer is None else s.shape[1:]), s.dtype) for s, layer in items]
    return _Job([s for s, _ in items], outs, (nw, 6), start, finish)


def _swap_job(grads):
    def copies(d_refs, a_refs, send, recv):
        x, y, c, _ = _place()
        cps = []
        for w in range(len(grads)):
            h = d_refs[w].shape[1] // 2
            cps.append(pltpu.make_async_remote_copy(
                src_ref=d_refs[w].at[:, pl.ds((1 - c) * h, h), :], dst_ref=a_refs[w], send_sem=send.at[w], recv_sem=recv.at[w],
                device_id=(x, y, 1 - c), device_id_type=MESH))
        return cps

    def start(*r):
        for cp in copies(*r):
            cp.start()

    def finish(*r):
        for cp in copies(*r):
            cp.wait()

    outs = [jax.ShapeDtypeStruct((NCHIP, g.shape[1] // 2, g.shape[2]), g.dtype) for g in grads]
    return _Job(list(grads), outs, (len(grads),), start, finish)


def _exchange_job(parts):
    def copies(q_refs, b_refs, send, recv):
        x, y, c, others = _place()
        cps = []
        for w in range(len(parts)):
            for k, (px, py) in enumerate(others):
                cps.append(pltpu.make_async_remote_copy(
                    src_ref=q_refs[w].at[2 * px + py], dst_ref=b_refs[w].at[k], send_sem=send.at[w, k], recv_sem=recv.at[w, k],
                    device_id=(px, py, c), device_id_type=MESH))
        return cps

    def start(*r):
        for cp in copies(*r):
            cp.start()

    def finish(*r):
        for cp in copies(*r):
            cp.wait()

    outs = [jax.ShapeDtypeStruct((3,) + p.shape[1:], p.dtype) for p in parts]
    return _Job(list(parts), outs, (len(parts), 3), start, finish)


def _run_job(name, job):
    n_in, n_out = len(job.ins), len(job.outs)

    def body(*refs):
        ins, outs = refs[:n_in], refs[n_in:n_in + n_out]
        send, recv = refs[n_in + n_out:]
        job.start(ins, outs, send, recv)
        job.finish(ins, outs, send, recv)

    return pl.pallas_call(
        body, name=name, out_shape=tuple(job.outs), in_specs=[_ANY] * n_in, out_specs=tuple([_ANY] * n_out),
        scratch_shapes=[pltpu.SemaphoreType.DMA(job.sems), pltpu.SemaphoreType.DMA(job.sems)])(*job.ins)


def _sibling_join(halves):
    nw = len(halves)

    def body(*refs):
        h_refs = refs[:2 * nw]
        f_refs = refs[2 * nw:3 * nw]
        send, recv = refs[3 * nw:]
        x, y, c, _ = _place()
        cps = []
        for w in range(nw):
            for l in range(2):
                src = h_refs[2 * w + l]
                h = src.shape[0]
                dst = f_refs[w].at[l, pl.ds(c * h, h), :]
                cp = pltpu.make_async_remote_copy(src_ref=src, dst_ref=dst, send_sem=send.at[w, l], recv_sem=recv.at[w, l],
                                                  device_id=(x, y, 1 - c), device_id_type=MESH)
                cp.start()
                cps.append(cp)
        for w in range(nw):
            for l in range(2):
                src = h_refs[2 * w + l]
                h = src.shape[0]
                other = f_refs[w].at[l, pl.ds((1 - c) * h, h), :]
                pltpu.make_async_remote_copy(src_ref=src, dst_ref=other, send_sem=send.at[w, l], recv_sem=recv.at[w, l],
                                             device_id=(x, y, c), device_id_type=MESH).wait_recv()
        for cp in cps:
            cp.wait_send()

    flat = [a for pair in halves for a in pair]
    outs = tuple(jax.ShapeDtypeStruct((2, 2 * pair[0].shape[0], pair[0].shape[1]), F32) for pair in halves)
    return pl.pallas_call(
        body, name="grad_sibling_join", out_shape=outs, in_specs=[_ANY] * (2 * nw), out_specs=tuple([_ANY] * nw),
        scratch_shapes=[pltpu.SemaphoreType.DMA((nw, 2)), pltpu.SemaphoreType.DMA((nw, 2))])(*flat)


def _allgather_small(v):
    r = v.shape[0]

    def body(x_ref, out_ref, send_sems, recv_sems, local_sem):
        x, y, c, chips = _place()
        me, sibling = (x, y, c), (x, y, 1 - c)

        def slab(px, py, pc):
            return out_ref.at[4 * px + 2 * py + pc]

        def copy(k, block, to, src=None):
            return pltpu.make_async_remote_copy(src_ref=slab(*block) if src is None else src, dst_ref=slab(*block),
                                                send_sem=send_sems.at[k], recv_sem=recv_sems.at[k],
                                                device_id=to, device_id_type=MESH)

        mine = pltpu.make_async_copy(x_ref, slab(*me), local_sem)
        mine.start()
        first = [copy(0, me, sibling, src=x_ref)]
        first += [copy(1 + j, me, (*chip, c), src=x_ref) for j, chip in enumerate(chips)]
        for cp in first:
            cp.start()
        passed = [copy(4 + j, (*chip, c), sibling) for j, chip in enumerate(chips)]
        for j, chip in enumerate(chips):
            copy(1 + j, (*chip, c), me).wait_recv()
            passed[j].start()
        copy(0, sibling, me).wait_recv()
        for j, chip in enumerate(chips):
            copy(4 + j, (*chip, 1 - c), me).wait_recv()
        for cp in first + passed:
            cp.wait_send()
        mine.wait()

    return pl.pallas_call(
        body, name="allgather_small", out_shape=jax.ShapeDtypeStruct((8, r, 128), F32),
        in_specs=[pl.BlockSpec(memory_space=pltpu.VMEM)], out_specs=pl.BlockSpec(memory_space=pltpu.VMEM),
        scratch_shapes=[pltpu.SemaphoreType.DMA((7,)), pltpu.SemaphoreType.DMA((7,)), pltpu.SemaphoreType.DMA],
    )(v)


def _add_half(name, d, a, c):
    _, h, cols = a.shape
    nt = 1
    th = h // nt

    def body(c_ref, d_ref, a_ref, o_ref):
        o_ref[...] = (d_ref[...] + a_ref[...]).astype(BF16)

    return _call(body, name, jax.ShapeDtypeStruct(a.shape, BF16), (NCHIP, nt),
                 [pl.BlockSpec((None, th, cols), lambda p, i, cr: (p, cr[0] * nt + i, 0)),
                  pl.BlockSpec((None, th, cols), lambda p, i, cr: (p, i, 0))],
                 pl.BlockSpec((None, th, cols), lambda p, i, cr: (p, i, 0)), prefetch=1)(c, d, a)


def _add_chips(name, d, a, b, where):
    _, h, cols = a.shape
    th = h // 4 if (h % 64 == 0) else h
    nt = h // th

    def body(w_ref, d_ref, a_ref, b_ref, o_ref):
        own = d_ref[...] + a_ref[...]
        o_ref[...] = ((own + b_ref[0].astype(F32)) + b_ref[1].astype(F32)) + b_ref[2].astype(F32)

    return _call(body, name, jax.ShapeDtypeStruct((h, cols), F32), (nt,),
                 [pl.BlockSpec((None, th, cols), lambda i, wr: (wr[0], wr[1] * nt + i, 0)),
                  pl.BlockSpec((None, th, cols), lambda i, wr: (wr[0], i, 0)),
                  pl.BlockSpec((3, th, cols), lambda i, wr: (0, i, 0))],
                 pl.BlockSpec((th, cols), lambda i, wr: (i, 0)), prefetch=1)(where, d, a, b)


def _pack(arrs):
    pieces = []
    for a in arrs:
        f = a.reshape(-1)
        n = -(-f.shape[0] // 1024) * 1024
        pieces.append(jnp.pad(f, (0, n - f.shape[0])).reshape(-1, 128))
    return jnp.concatenate(pieces, axis=0)


def _unpack(buf, shapes):
    out, r = [], 0
    for s in shapes:
        n = int(np.prod(s))
        rows = -(-n // 1024) * 8
        out.append(buf[r:r + rows].reshape(-1)[:n].reshape(s))
        r += rows
    return out


def kernel(x, meta_tokens, in_ln_g, in_ln_b, rel_bias, w_in, b_in, attn_sinks, w_attn_proj, conv_dw, conv_dw_b, conv_ln_g, conv_ln_b, w_conv_proj, w_out, ln1_g, ln1_b, ffn_w_up, ffn_dw, ffn_dw_b, ffn_w_down, ln2_g, ln2_b, loss_target, m_meta_tokens, m_in_ln_g, m_in_ln_b, m_rel_bias, m_w_in, m_b_in, m_attn_sinks, m_w_attn_proj, m_conv_dw, m_conv_dw_b, m_conv_ln_g, m_conv_ln_b, m_w_conv_proj, m_w_out, m_ln1_g, m_ln1_b, m_ffn_w_up, m_ffn_dw, m_ffn_dw_b, m_ffn_w_down, m_ln2_g, m_ln2_b, v_meta_tokens, v_in_ln_g, v_in_ln_b, v_rel_bias, v_w_in, v_b_in, v_attn_sinks, v_w_attn_proj, v_conv_dw, v_conv_dw_b, v_conv_ln_g, v_conv_ln_b, v_w_conv_proj, v_w_out, v_ln1_g, v_ln1_b, v_ffn_w_up, v_ffn_dw, v_ffn_dw_b, v_ffn_w_down, v_ln2_g, v_ln2_b):
    nex, seq, _ = x.shape
    nblk = seq // BLK + 1
    tp = nblk * BLK
    m = nex * tp
    tm = _row_tile(m)
    ffs = ffn_w_up.shape[2]
    dff = 2 * ffs
    cx, cy, cc = lax.axis_index("x"), lax.axis_index("y"), lax.axis_index("c")
    chip = (2 * cx + cy).astype(jnp.int32)
    core = cc.astype(jnp.int32)

    names = ("in", "ap", "cp", "out", "up", "down")
    big = dict(zip(names, [w_in, w_attn_proj, w_conv_proj, w_out, ffn_w_up, ffn_w_down]))
    sb = {k: v.astype(BF16) for k, v in big.items()}
    gathered = {}

    def land(items, outs):
        return [lax.dynamic_update_index_in_dim(g, s if layer is None else s[layer], chip, 0)
                for (s, layer), g in zip(items, outs)]

    first_items = [(meta_tokens.reshape(2, N_META // 2, -1), None), (conv_dw, None), (ffn_dw, None)]
    g_meta, g_cdw, g_fdw = land(first_items, _run_job("gather_small", _gather_job(first_items)))
    meta_full = jnp.transpose(g_meta, (1, 2, 0, 3)).reshape(N_META, D)
    bias_tab = _bias_build(rel_bias)

    fwd_plan = {("embed_ln", 0): [("in", 0)],
                ("in_proj", 0): [("ap", 0), ("cp", 0), ("out", 0)], ("attn_fwd", 0): [("up", 0)],
                ("conv_fwd", 0): [("down", 0)], ("mix_fwd", 0): [("in", 1)],
                ("out_proj_ln", 0): [("ap", 1), ("cp", 1), ("out", 1)], ("ffn_up_act", 0): [("up", 1), ("down", 1)]}

    def fwd(tag, l, fn, *args):
        keys = fwd_plan.get((tag, l))
        if not keys:
            return fn(*args)
        items = [(sb[k], kl) for k, kl in keys]
        res = fn(*args, side=_gather_job(items))
        for key, g in zip(keys, land(items, res[-len(keys):])):
            gathered[key] = g
        main = res[:-len(keys)]
        return main[0] if len(main) == 1 else main

    def layer_weights(l):
        return dict(
            win=_w_in_from_chips(gathered[("in", l)]), bin=_to_new(b_in[l]).reshape(1, IN_COLS),
            cdw=jnp.transpose(g_cdw[:, l], (1, 0, 2)).reshape(CTAPS, CW),
            fdw=jnp.transpose(g_fdw[:, l], (1, 0, 2)).reshape(FTAPS, 2 * dff),
            fdwb=ffn_dw_b[l].reshape(1, 2 * dff))

    raw, h, hb = fwd("embed_ln", 0, _embed_ln, x, meta_full, in_ln_g, in_ln_b, nblk)
    saved, lw = [], []
    for l in range(DEPTH):
        p = layer_weights(l)
        z = fwd("in_proj", l, functools.partial(_mm_bias, "in_proj"), hb, p["win"], p["bin"], IN_COLS // 3, tm)
        a = fwd("attn_fwd", l, _attn_fwd, z, bias_tab, attn_sinks[l], nex, nblk)
        ccv, cs = fwd("conv_fwd", l, _conv_fwd, z, p["cdw"], conv_dw_b[l], conv_ln_g[l], conv_ln_b[l], nex, tp)
        p["wap"], p["wcp"] = gathered[("ap", l)], gathered[("cp", l)]
        ya, yc, mixed = fwd("mix_fwd", l, _mix_fwd, a, cs, p["wap"], p["wcp"], z, tm)
        p["wout"] = gathered[("out", l)].reshape(D, D)
        r1, h1, h1b = fwd("out_proj_ln", l, functools.partial(_mm_res_ln, "out_proj_ln"), mixed, p["wout"], h, ln1_g[l], ln1_b[l], tm)
        p["wup"] = gathered[("up", l)]
        up3, ug, act = fwd("ffn_up_act", l, _ffn_up_act, h1b, p["wup"], p["fdw"], p["fdwb"], tm, nex, tp)
        p["wdown"] = gathered[("down", l)].reshape(dff, D)
        r2, h2, h2b = _mm_res_ln("ffn_down_ln", act, p["wdown"], h1, ln2_g[l], ln2_b[l], tm)
        saved.append(dict(hb=hb, z=z, a=a, cc=ccv, cs=cs, ya=ya, yc=yc, mixed=mixed, r1=r1, h1b=h1b, up3=up3, ug=ug, act=act, r2=r2))
        lw.append(p)
        h, hb = h2, h2b

    dy, sq = _loss_grad(h, loss_target, nblk)
    loss = lax.psum(0.5 / D * jnp.sum(sq), ("x", "y", "c"))

    grads, swapped, pair_sums, reduced = {}, {}, {}, {}
    cvec, where = core.reshape(1), jnp.stack([chip, core])
    last = [(k, DEPTH - 1) for k in names]
    bwd_plan = {("ln2_bwd", 0): ("swap", last),
                ("ffn_bwd", 0): ("exch", [("up", 1), ("down", 1)]),
                ("dw_up", 0): ("exch", [("in", 1), ("ap", 1), ("cp", 1), ("out", 1)]),
                ("ln1_bwd", 0): ("swap", [("down", 0), ("up", 0)]),
                ("conv_bwd_b", 0): ("swap", [("out", 0), ("ap", 0), ("cp", 0)]),
                ("attn_bwd", 0): ("exch", [("down", 0), ("up", 0), ("out", 0), ("ap", 0), ("cp", 0)]),
                ("in_ln_bwd", 0): ("swap", [("in", 0)])}

    def after(kind, keys, outs):
        for key, o in zip(keys, outs):
            if kind == "swap":
                swapped[key] = o
                pair_sums[key] = _add_half("grad_add_sibling", grads[key], o, cvec)
            else:
                reduced[key] = _add_chips("grad_add_chips", grads[key], swapped[key], o, where)

    def bwd(tag, l, fn, *args):
        plan = bwd_plan.get((tag, l))
        if plan is None:
            return fn(*args)
        kind, keys = plan
        job = _swap_job([grads[k] for k in keys]) if kind == "swap" else _exchange_job([pair_sums[k] for k in keys])
        res = fn(*args, side=job)
        after(kind, keys, res[-len(keys):])
        main = res[:-len(keys)]
        return main[0] if len(main) == 1 else main

    small = {}
    prev_a, prev_w, prev_cols = (), None, None
    dprev = dy
    for l in reversed(range(DEPTH)):
        p, s = lw[l], saved[l]
        dr2, dr2b, dg2, db2 = bwd("ln2_bwd", l, functools.partial(_ln_bwd_call, "ln2_bwd"), dprev, s["r2"], ln2_g[l], tm,
                                  prev_a, prev_w, prev_cols)
        dpre3, dfdw, dfdwb = bwd("ffn_bwd", l, _ffn_bwd, dr2b, p["wdown"], s["ug"], s["up3"], p["fdw"], tm, nex, tp)
        grads[("down", l)] = _mm_tn("dw_down", s["act"], dr2b, ffs, D // 2).reshape(NCHIP, dff // NCHIP, D)
        grads[("up", l)] = bwd("dw_up", l, functools.partial(_mm_tn, "dw_up"), s["h1b"], dpre3, D, ffs,
                               lambda j: (j // 2, j % 2), True)
        dr1, dr1b, dg1, db1 = bwd("ln1_bwd", l, functools.partial(_ln_bwd_call, "ln1_bwd"), dr2, s["r1"], ln1_g[l], tm // 2,
                                  (dpre3,), p["wup"])
        dya, dyc, dzg, csg = _gate_bwd(dr1b, p["wout"], s["ya"], s["yc"], s["z"], tm)
        grads[("out", l)] = _mm_tn("dw_out", s["mixed"], dr1b, D, D // 2).reshape(NCHIP, D // NCHIP, D)
        grads[("ap", l)] = _mm_tn("dw_attn_proj", s["a"], dya, AW, D // NCHIP, chip_out=True)
        grads[("cp", l)] = _mm_tn("dw_conv_proj", s["cs"], dyc, CW, D // NCHIP, chip_out=True)
        dcc, dclg, dclb, dcwb = _conv_bwd_a(dyc, p["wcp"], s["cc"], conv_ln_g[l], conv_ln_b[l], tm)
        dzc, dcdw, csc = bwd("conv_bwd_b", l, _conv_bwd_b, dcc, s["z"], p["cdw"], nex, tp)
        dzq, csq, dsk, dbias = bwd("attn_bwd", l, _attn_bwd, s["z"], bias_tab, attn_sinks[l], dya, p["wap"], nex, nblk)
        gin = [_mm_tn("dw_in_gates", s["hb"], dzg, D, D // 2), _mm_tn("dw_in_conv", s["hb"], dzc, D, CW),
               _mm_tn("dw_in_qkv", s["hb"], dzq, D, 2 * KVW)]
        grads[("in", l)] = _w_in_to_chips(gin)
        small[l] = dict(
            b_in=_to_old(jnp.concatenate([csg, csc, csq], axis=1)).reshape(IN_COLS), attn_sinks=dsk[0, :NQ],
            conv_dw=dcdw[:CTAPS], conv_dw_b=dcwb.reshape(CW), conv_ln_g=dclg.reshape(CW), conv_ln_b=dclb.reshape(CW),
            ln1_g=dg1.reshape(D), ln1_b=db1.reshape(D),
            ffn_dw=jnp.transpose(dfdw, (1, 0, 2)).reshape(FTAPS, 2 * dff), ffn_dw_b=jnp.transpose(dfdwb, (1, 0, 2)).reshape(2 * dff),
            ln2_g=dg2.reshape(D), ln2_b=db2.reshape(D), bias=dbias)
        dprev = dr1
        prev_a, prev_w, prev_cols = (dzg, dzc, dzq), p["win"], [(C_GATES, C_CONV), (C_CONV, C_QKV), (C_QKV, IN_COLS)]
    draw, _, dg0, db0 = bwd("in_ln_bwd", 0, functools.partial(_ln_bwd_call, "in_ln_bwd"), dprev, raw, in_ln_g, tm,
                            prev_a, prev_w, prev_cols)
    draw3 = draw.reshape(nex, tp, D)
    grad_x = draw3[:, BLK:, :]
    dmeta = _sum0("meta_grad_sum", draw3[:, PAD:BLK, :])

    names_l = ["b_in", "attn_sinks", "conv_dw", "conv_dw_b", "conv_ln_g", "conv_ln_b", "ln1_g", "ln1_b", "ffn_dw", "ffn_dw_b", "ln2_g", "ln2_b"]
    dbias_all = _bias_grad(jnp.concatenate([small[l]["bias"] for l in range(DEPTH)], axis=0))
    part_list = [dmeta, dg0.reshape(D), db0.reshape(D), dbias_all]
    part_list += [jnp.stack([small[0][n], small[1][n]]) for n in names_l]
    shapes_small = [tuple(a.shape) for a in part_list]
    tot = _sum0("small_grad_sum", _allgather_small(_pack(part_list)))
    (g_meta_f, g_inlg, g_inlb, g_biasp, g_bin, g_sinks, g_cdw_f, g_cdwb, g_clg, g_clb, g_l1g, g_l1b, g_fdw_f, g_fdwb,
     g_l2g, g_l2b) = _unpack(tot, shapes_small)
    g_relb = g_biasp
    csh = D // NCHIP
    g_meta_s = lax.dynamic_slice_in_dim(g_meta_f, chip * csh, csh, axis=1)
    g_cdw_s = lax.dynamic_slice_in_dim(g_cdw_f, chip * (CW // NCHIP), CW // NCHIP, axis=2)
    g_fdw_s = lax.dynamic_slice_in_dim(g_fdw_f, chip * ffs, ffs, axis=2)

    tail = [("in", 0)]
    after("exch", tail, _run_job("grad_chip_exchange", _exchange_job([pair_sums[k] for k in tail])))
    joined = _sibling_join([[reduced[(k, l)] for l in range(DEPTH)] for k in names])
    full = []
    for k, f in zip(names, joined):
        hh = reduced[(k, 0)].shape[0]
        for l in range(DEPTH):
            f = lax.dynamic_update_slice(f, reduced[(k, l)][None], (l, core * hh, 0))
        full.append(f)

    moms = [m_w_in, m_w_attn_proj, m_w_conv_proj, m_w_out, m_ffn_w_up, m_ffn_w_down]
    vels = [v_w_in, v_w_attn_proj, v_w_conv_proj, v_w_out, v_ffn_w_up, v_ffn_w_down]
    big_out = []
    for w, g, mo, ve in zip(big.values(), full, moms, vels):
        sh = w.shape
        two = lambda t: t.reshape(sh[0] * sh[1], sh[2])
        d_, m_, v_ = _adamw("adamw_matrix", two(w), two(g), two(mo), two(ve))
        big_out.append((g.reshape(sh), d_.reshape(sh), m_.reshape(sh), v_.reshape(sh)))

    sm_w = [meta_tokens, in_ln_g, in_ln_b, rel_bias, b_in, attn_sinks, conv_dw, conv_dw_b, conv_ln_g, conv_ln_b, ln1_g, ln1_b,
            ffn_dw, ffn_dw_b, ln2_g, ln2_b]
    sm_m = [m_meta_tokens, m_in_ln_g, m_in_ln_b, m_rel_bias, m_b_in, m_attn_sinks, m_conv_dw, m_conv_dw_b, m_conv_ln_g, m_conv_ln_b,
            m_ln1_g, m_ln1_b, m_ffn_dw, m_ffn_dw_b, m_ln2_g, m_ln2_b]
    sm_v = [v_meta_tokens, v_in_ln_g, v_in_ln_b, v_rel_bias, v_b_in, v_attn_sinks, v_conv_dw, v_conv_dw_b, v_conv_ln_g, v_conv_ln_b,
            v_ln1_g, v_ln1_b, v_ffn_dw, v_ffn_dw_b, v_ln2_g, v_ln2_b]
    sm_g = [g_meta_s, g_inlg, g_inlb, g_relb, g_bin, g_sinks, g_cdw_s, g_cdwb, g_clg, g_clb, g_l1g, g_l1b, g_fdw_s, g_fdwb, g_l2g, g_l2b]
    sm_shapes = [tuple(a.shape) for a in sm_w]
    sd, smn, svn = _adamw("adamw_small", _pack(sm_w), _pack(sm_g), _pack(sm_m), _pack(sm_v))
    sd, smn, svn = _unpack(sd, sm_shapes), _unpack(smn, sm_shapes), _unpack(svn, sm_shapes)

    order = ["meta_tokens", "in_ln_g", "in_ln_b", "rel_bias", "w_in", "b_in", "attn_sinks", "w_attn_proj", "conv_dw", "conv_dw_b",
             "conv_ln_g", "conv_ln_b", "w_conv_proj", "w_out", "ln1_g", "ln1_b", "ffn_w_up", "ffn_dw", "ffn_dw_b", "ffn_w_down",
             "ln2_g", "ln2_b"]
    small_names = ["meta_tokens", "in_ln_g", "in_ln_b", "rel_bias", "b_in", "attn_sinks", "conv_dw", "conv_dw_b", "conv_ln_g",
                   "conv_ln_b", "ln1_g", "ln1_b", "ffn_dw", "ffn_dw_b", "ln2_g", "ln2_b"]
    big_names = ["w_in", "w_attn_proj", "w_conv_proj", "w_out", "ffn_w_up", "ffn_w_down"]
    res = {}
    for i, n in enumerate(small_names):
        res[n] = (sm_g[i], sd[i], smn[i], svn[i])
    for i, n in enumerate(big_names):
        res[n] = big_out[i]
    outs = [loss, grad_x]
    for k in range(4):
        outs += [res[n][k] for n in order]
    return tuple(outs)
```

```python
import functools
import math
from typing import Any, Callable, NamedTuple, Sequence

import numpy as np
import jax
import jax.numpy as jnp
from jax import lax
from jax.experimental import pallas as pl
from jax.experimental.pallas import tpu as pltpu

F32 = jnp.float32
BF16 = jnp.bfloat16
MESH = pl.DeviceIdType.MESH

D = 1024
N_META = 16
BLK = 128
PAD = BLK - N_META
HD = 64
NQ = 8
NKV = 2
GRP = NQ // NKV
AW = NQ * HD
KVW = NKV * HD
CW = D // 2
CTAPS = 31
FTAPS = 3
NBUCKET = 32
MAXDIST = 128
EPS = 1e-5
DEPTH = 2
ALPHA = (2.0 * DEPTH) ** 0.25
NCHIP = 4
NKEY = 3 * BLK
NEG = -1e30
CHALO = 32
FHALO = 8
IN_COLS = AW + 2 * KVW + 2 * CW + 2 * D
_OLD = dict(q=(0, AW), k=(AW, AW + KVW), v=(AW + KVW, AW + 2 * KVW), cv=(AW + 2 * KVW, AW + 2 * KVW + CW),
            cg=(AW + 2 * KVW + CW, AW + 2 * KVW + 2 * CW), ga=(AW + 2 * KVW + 2 * CW, AW + 2 * KVW + 2 * CW + D),
            gc=(AW + 2 * KVW + 2 * CW + D, IN_COLS))
_NEW_ORDER = ("ga", "gc", "cv", "cg", "q", "k", "v")
C_GATES, C_CONV, C_QKV = 0, 2 * D, 2 * D + 2 * CW

ADAM_LR, ADAM_B1, ADAM_B2, ADAM_EPS, ADAM_WD, ADAM_STEP = 0.001, 0.9, 0.999, 1e-08, 0.01, 10


def _to_new(a):
    return jnp.concatenate([a[..., _OLD[n][0]:_OLD[n][1]] for n in _NEW_ORDER], axis=-1)


def _to_old(a):
    offs, o = {}, 0
    for n in _NEW_ORDER:
        w = _OLD[n][1] - _OLD[n][0]
        offs[n] = (o, o + w)
        o += w
    return jnp.concatenate([a[..., offs[n][0]:offs[n][1]] for n in ("q", "k", "v", "cv", "cg", "ga", "gc")], axis=-1)


def _new_starts():
    starts, o = {}, 0
    for n in _NEW_ORDER:
        starts[n] = o
        o += _OLD[n][1] - _OLD[n][0]
    return starts


def _w_in_from_chips(g4):
    cs = IN_COLS // NCHIP
    pieces = []
    for n in _NEW_ORDER:
        lo, hi = _OLD[n]
        while lo < hi:
            chip = lo // cs
            end = min(hi, (chip + 1) * cs)
            pieces.append(g4[chip][:, lo - chip * cs:end - chip * cs])
            lo = end
    return jnp.concatenate(pieces, axis=1)


def _w_in_to_chips(parts):
    cs = IN_COLS // NCHIP
    starts = _new_starts()
    bounds, o = [], 0
    for p in parts:
        bounds.append((o, o + p.shape[1], p))
        o += p.shape[1]

    def new_cols(a, b):
        out = []
        for s, e, p in bounds:
            lo, hi = max(a, s), min(b, e)
            if lo < hi:
                out.append(p[:, lo - s:hi - s])
        return out

    slabs = []
    for chip in range(NCHIP):
        pieces = []
        for n in ("q", "k", "v", "cv", "cg", "ga", "gc"):
            lo, hi = max(_OLD[n][0], chip * cs), min(_OLD[n][1], (chip + 1) * cs)
            if lo < hi:
                pieces += new_cols(starts[n] + lo - _OLD[n][0], starts[n] + hi - _OLD[n][0])
        slabs.append(jnp.concatenate(pieces, axis=1))
    return jnp.stack(slabs)


class _Job(NamedTuple):
    ins: Sequence[Any]
    outs: Sequence[Any]
    sems: tuple
    start: Callable
    finish: Callable


_ANY = pl.BlockSpec(memory_space=pl.ANY)


def _call(body, name, out_shape, grid, in_specs, out_specs, scratch=(), prefetch=0, side=None):
    params = pltpu.CompilerParams(dimension_semantics=("arbitrary",) * len(grid))
    if side is None:
        if prefetch:
            gs = pltpu.PrefetchScalarGridSpec(num_scalar_prefetch=prefetch, grid=grid, in_specs=in_specs,
                                              out_specs=out_specs, scratch_shapes=list(scratch))
            return pl.pallas_call(body, name=name, out_shape=out_shape, grid_spec=gs, compiler_params=params)
        return pl.pallas_call(body, name=name, out_shape=out_shape, grid=grid, in_specs=in_specs, out_specs=out_specs,
                              scratch_shapes=list(scratch), compiler_params=params)
    assert not prefetch
    single = not isinstance(out_shape, (tuple, list))
    main_shapes = (out_shape,) if single else tuple(out_shape)
    main_specs = (out_specs,) if single else tuple(out_specs)
    n_in, n_sin, n_out, n_sout, n_scr = len(in_specs), len(side.ins), len(main_shapes), len(side.outs), len(scratch)

    def wrapped(*refs):
        main_in, sin = refs[:n_in], refs[n_in:n_in + n_sin]
        o0 = n_in + n_sin
        main_out, sout = refs[o0:o0 + n_out], refs[o0 + n_out:o0 + n_out + n_sout]
        s0 = o0 + n_out + n_sout
        main_scr, (send, recv) = refs[s0:s0 + n_scr], refs[s0 + n_scr:]
        first = functools.reduce(lambda a, b: a & b, [pl.program_id(k) == 0 for k in range(len(grid))])
        last = functools.reduce(lambda a, b: a & b, [pl.program_id(k) == grid[k] - 1 for k in range(len(grid))])

        @pl.when(first)
        def _():
            side.start(sin, sout, send, recv)

        body(*main_in, *main_out, *main_scr)

        @pl.when(last)
        def _():
            side.finish(sin, sout, send, recv)

    call = pl.pallas_call(
        wrapped, name=name, out_shape=main_shapes + tuple(side.outs), grid=grid,
        in_specs=list(in_specs) + [_ANY] * n_sin, out_specs=main_specs + tuple([_ANY] * n_sout),
        scratch_shapes=list(scratch) + [pltpu.SemaphoreType.DMA(side.sems), pltpu.SemaphoreType.DMA(side.sems)],
        compiler_params=params)
    return lambda *args: call(*args, *side.ins)


def _row_tile(m):
    best = 32
    for t in range(32, 641, 32):
        if m % t == 0:
            best = t
    return best


def _pad_rows(rows, nex, tp):
    m = rows < PAD
    for b in range(1, nex):
        m = m | ((rows >= b * tp) & (rows < b * tp + PAD))
    return m


def _ln_stats(x):
    mu = jnp.mean(x, axis=-1, keepdims=True)
    xc = x - mu
    var = jnp.mean(xc * xc, axis=-1, keepdims=True)
    rstd = lax.rsqrt(var + EPS)
    return xc * rstd, rstd


def _ln_bwd(dy, xhat, rstd, g):
    dxh = dy * g
    m1 = jnp.mean(dxh, axis=-1, keepdims=True)
    m2 = jnp.mean(dxh * xhat, axis=-1, keepdims=True)
    return rstd * (dxh - m1 - xhat * m2)


def _dot(a, b):
    return jnp.dot(a, b, preferred_element_type=F32)


def _dot_nt(a, b):
    return lax.dot_general(a, b, (((1,), (1,)), ((), ())), preferred_element_type=F32)


def _dot_tn(a, b):
    return lax.dot_general(a, b, (((0,), (0,)), ((), ())), preferred_element_type=F32)


def _sigmoid(x):
    return 1.0 / (1.0 + jnp.exp(-x))


def _bucket_np(d):
    n = np.maximum(d, 0)
    max_exact = NBUCKET // 2
    nf = np.maximum(n, 1).astype(np.float32)
    large = max_exact + (np.log(nf / np.float32(max_exact)) / np.float32(math.log(MAXDIST / max_exact))
                         * np.float32(NBUCKET - max_exact)).astype(np.int32)
    large = np.minimum(large, NBUCKET - 1)
    return np.where(n < max_exact, n, large).astype(np.int32)


def _bias_index():
    i = np.arange(BLK)[:, None]
    j = np.arange(2 * BLK)[None, :]
    d = BLK + i - j
    band_ok = (d >= 0) & (d < BLK)
    band = _bucket_np(d)
    idx = np.full((3, BLK, NKEY), -1, np.int32)
    m = np.arange(N_META)[None, :]
    d0 = (i - PAD) - m
    idx[0, :, 2 * BLK:2 * BLK + N_META] = np.where(d0 >= 0, _bucket_np(d0), -1)
    ok1 = band_ok & (j >= BLK)
    idx[1, :, :2 * BLK] = np.where(ok1, band, -1)
    idx[1, :, 2 * BLK:2 * BLK + N_META] = _bucket_np((N_META + i) - m)
    idx[2, :, :2 * BLK] = np.where(band_ok, band, -1)
    idx[2, :, 2 * BLK:2 * BLK + N_META] = NBUCKET - 1
    return idx


def _bias_build(rel_bias):
    idx = jnp.asarray(_bias_index())

    def body(idx_ref, rb_ref, o_ref):
        ix = idx_ref[...]
        for h in range(NQ):
            acc = jnp.full(ix.shape, NEG, F32)
            for b in range(NBUCKET):
                acc = jnp.where(ix == b, rb_ref[b, h], acc)
            o_ref[:, h, :, :] = acc

    return pl.pallas_call(
        body, name="bias_build", out_shape=jax.ShapeDtypeStruct((3, NQ, BLK, NKEY), F32),
        in_specs=[pl.BlockSpec(memory_space=pltpu.VMEM), pl.BlockSpec(memory_space=pltpu.SMEM)],
        out_specs=pl.BlockSpec(memory_space=pltpu.VMEM))(idx, rel_bias)


def _bias_grad(dbias):
    idx = jnp.asarray(_bias_index())

    def body(idx_ref, d_ref, o_ref):
        d = jnp.sum(d_ref[...], axis=0)
        for b in range(NBUCKET):
            acc = jnp.zeros((NQ, NKEY), F32)
            for case in range(3):
                hit = (idx_ref[case] == b)[None, :, :]
                acc = acc + jnp.sum(jnp.where(hit, d[case], 0.0), axis=1)
            o_ref[b] = jnp.sum(acc, axis=-1, keepdims=True)

    out = pl.pallas_call(
        body, name="bias_grad", out_shape=jax.ShapeDtypeStruct((NBUCKET, NQ, 1), F32),
        in_specs=[pl.BlockSpec(memory_space=pltpu.VMEM), pl.BlockSpec(memory_space=pltpu.VMEM)],
        out_specs=pl.BlockSpec(memory_space=pltpu.VMEM))(idx, dbias)
    return out.reshape(NBUCKET, NQ)


def _embed_ln(x, meta, g, b, nblk, side=None):
    nex, seq, _ = x.shape
    m = nex * nblk * BLK

    def body(x_ref, meta_ref, g_ref, b_ref, raw_ref, h_ref, hb_ref):
        j = pl.program_id(1)

        @pl.when(j == 0)
        def _():
            raw_ref[0:PAD, :] = jnp.zeros((PAD, D), F32)
            raw_ref[PAD:BLK, :] = meta_ref[...]

        @pl.when(j > 0)
        def _():
            raw_ref[...] = x_ref[...]

        xhat, _ = _ln_stats(raw_ref[...])
        y = xhat * g_ref[...] + b_ref[...]
        h_ref[...] = y
        hb_ref[...] = y.astype(BF16)

    row = lambda bb, j: (bb * nblk + j, 0)
    return _call(
        body, "embed_ln",
        (jax.ShapeDtypeStruct((m, D), F32), jax.ShapeDtypeStruct((m, D), F32), jax.ShapeDtypeStruct((m, D), BF16)),
        (nex, nblk),
        [pl.BlockSpec((None, BLK, D), lambda bb, j: (bb, jnp.maximum(j - 1, 0), 0)),
         pl.BlockSpec((N_META, D), lambda bb, j: (0, 0)),
         pl.BlockSpec((1, D), lambda bb, j: (0, 0)), pl.BlockSpec((1, D), lambda bb, j: (0, 0))],
        (pl.BlockSpec((BLK, D), row), pl.BlockSpec((BLK, D), row), pl.BlockSpec((BLK, D), row)), side=side,
    )(x, meta, g.reshape(1, D), b.reshape(1, D))


def _mm_bias(name, a, w, bias, tn, tm, side=None):
    m, k = a.shape
    n = w.shape[1]

    def body(a_ref, w_ref, b_ref, o_ref):
        o_ref[...] = _dot(a_ref[...], w_ref[...]) + b_ref[...]

    return _call(body, name, jax.ShapeDtypeStruct((m, n), F32), (n // tn, m // tm),
                 [pl.BlockSpec((tm, k), lambda j, i: (i, 0)), pl.BlockSpec((k, tn), lambda j, i: (0, j)),
                  pl.BlockSpec((1, tn), lambda j, i: (0, j))],
                 pl.BlockSpec((tm, tn), lambda j, i: (i, j)), side=side)(a, w, bias)


def _ffn_up_act(a, w4, cw, cb, tm, nex, tp, side=None):
    m, k = a.shape
    ffs = w4.shape[2]

    def body(a_ref, wu_ref, wg_ref, cu_ref, cg_ref, bu_ref, bg_ref, up_ref, ug_ref, act_ref, win):
        i = pl.program_id(1)

        @pl.when(i == 0)
        def _():
            win[:, 0:FHALO, :] = jnp.zeros((2, FHALO, ffs), F32)

        rows = i * tm + lax.broadcasted_iota(jnp.int32, (tm, 1), 0)
        pad = _pad_rows(rows, nex, tp)
        av = a_ref[...]
        for p, w_ref in ((0, wu_ref), (1, wg_ref)):
            x = jnp.where(pad, 0.0, _dot(av, w_ref[...]))
            win[p, FHALO:FHALO + tm, :] = x
            up_ref[p] = x.astype(BF16)
        for r0 in range(0, tm, RCH):
            for c0, c1 in _lane_groups(ffs):
                u = _conv3(win, 0, r0, c0, c1, cu_ref, bu_ref)
                g = _conv3(win, 1, r0, c0, c1, cg_ref, bg_ref)
                ug_ref[0, r0:r0 + RCH, c0:c1] = u.astype(BF16)
                ug_ref[1, r0:r0 + RCH, c0:c1] = g.astype(BF16)
                act_ref[r0:r0 + RCH, c0:c1] = (g * (0.5 * (1.0 + lax.erf(g * (1.0 / math.sqrt(2.0))))) * u).astype(BF16)
        win[:, 0:FHALO, :] = win[:, tm:tm + FHALO, :]

    wide = jax.ShapeDtypeStruct((2, m, 2 * ffs), BF16)
    return _call(body, "ffn_up_act", (wide, wide, jax.ShapeDtypeStruct((m, 2 * ffs), BF16)),
                 (2, m // tm),
                 [pl.BlockSpec((tm, k), lambda c, i: (i, 0)),
                  pl.BlockSpec((None, k, ffs), lambda c, i: (c, 0, 0)), pl.BlockSpec((None, k, ffs), lambda c, i: (c + 2, 0, 0)),
                  pl.BlockSpec((FTAPS, ffs), lambda c, i: (0, c)), pl.BlockSpec((FTAPS, ffs), lambda c, i: (0, c + 2)),
                  pl.BlockSpec((1, ffs), lambda c, i: (0, c)), pl.BlockSpec((1, ffs), lambda c, i: (0, c + 2))],
                 (pl.BlockSpec((2, tm, ffs), lambda c, i: (0, i, c)), pl.BlockSpec((2, tm, ffs), lambda c, i: (0, i, c)),
                  pl.BlockSpec((tm, ffs), lambda c, i: (i, c))),
                 scratch=[pltpu.VMEM((2, FHALO + tm, ffs), F32)], side=side)(a, w4, w4, cw, cw, cb, cb)


def _fill_kv(ks, vs, e, prev_ref, cur_ref, meta_ref):
    for piece, lo, n in ((prev_ref, 0, BLK), (cur_ref, BLK, BLK), (meta_ref, 2 * BLK, N_META)):
        val = piece[e]
        for hk in range(NKV):
            ks[e, hk, lo:lo + n, :] = val[:, hk * HD:(hk + 1) * HD].astype(BF16)
            vs[e, hk, lo:lo + n, :] = val[:, KVW + hk * HD:KVW + (hk + 1) * HD].astype(BF16)
    for hk in range(NKV):
        ks[e, hk, 2 * BLK + N_META:NKEY, :] = jnp.zeros((BLK - N_META, HD), BF16)
        vs[e, hk, 2 * BLK + N_META:NKEY, :] = jnp.zeros((BLK - N_META, HD), BF16)


def _interleave(chains):
    live = list(chains)
    while live:
        for c in list(live):
            try:
                next(c)
            except StopIteration:
                live.remove(c)


def _head_softmax(q, ks_hk, bias_ref, sink_ref, h, out):
    qh = q[:, h * HD:(h + 1) * HD].astype(BF16)
    yield
    s = _dot_nt(qh, ks_hk) * (HD ** -0.5) + bias_ref[h]
    yield
    sink = sink_ref[0, h]
    mx = jnp.maximum(jnp.max(s, axis=-1, keepdims=True), sink)
    yield
    p = jnp.exp(s - mx)
    es = jnp.exp(sink - mx)
    yield
    inv = 1.0 / (jnp.sum(p, axis=-1, keepdims=True) + es)
    yield
    out["q"], out["p"], out["sink"] = qh, p * inv, es * inv
    yield


def _attn_specs(nex, blk_of):
    qcol, kvcol = (C_QKV) // AW, (C_QKV + AW) // (2 * KVW)
    return [
        pl.BlockSpec((nex, BLK, AW), lambda j: (0, blk_of(j), qcol)),
        pl.BlockSpec((nex, BLK, 2 * KVW), lambda j: (0, blk_of(j), kvcol)),
        pl.BlockSpec((nex, BLK, 2 * KVW), lambda j: (0, jnp.maximum(blk_of(j) - 1, 0), kvcol)),
        pl.BlockSpec((nex, N_META, 2 * KVW), lambda j: (0, PAD // N_META, kvcol)),
        pl.BlockSpec((None, NQ, BLK, NKEY), lambda j: (jnp.minimum(blk_of(j), 2), 0, 0, 0)),
        pl.BlockSpec(memory_space=pltpu.SMEM),
    ]


def _attn_fwd(z, bias, sinks, nex, nblk, side=None):
    m = z.shape[0]
    z3 = z.reshape(nex, nblk * BLK, z.shape[1])

    def head(e, h, q, ks, vs, bias_ref, sink_ref, oacc):
        out = {}
        yield from _head_softmax(q, ks[e, h // GRP], bias_ref, sink_ref, h, out)
        o = _dot(out["p"].astype(BF16), vs[e, h // GRP])
        yield
        oacc[e, :, h * HD:(h + 1) * HD] = o
        yield

    def body(q_ref, cur_ref, prev_ref, meta_ref, bias_ref, sink_ref, o_ref, ks, vs, oacc):
        for e in range(nex):
            _fill_kv(ks, vs, e, prev_ref, cur_ref, meta_ref)
        _interleave([head(e, h, q_ref[e], ks, vs, bias_ref, sink_ref, oacc) for e in range(nex) for h in range(NQ)])
        for e in range(nex):
            o_ref[e] = oacc[e].astype(BF16)

    res = _call(body, "attn_fwd", jax.ShapeDtypeStruct((nex, nblk * BLK, AW), BF16), (nblk,),
                _attn_specs(nex, lambda j: j),
                pl.BlockSpec((nex, BLK, AW), lambda j: (0, j, 0)),
                scratch=[pltpu.VMEM((nex, NKV, NKEY, HD), BF16), pltpu.VMEM((nex, NKV, NKEY, HD), BF16),
                         pltpu.VMEM((nex, BLK, AW), F32)], side=side)(z3, z3, z3, z3, bias, sinks.reshape(1, NQ))
    if side is None:
        return res.reshape(m, AW)
    return (res[0].reshape(m, AW),) + tuple(res[1:])


def _cgate(cv, cg, rows, nex, tp):
    return jnp.where(_pad_rows(rows, nex, tp), 0.0, cv * _sigmoid(cg))


CLANES = 256


def _rolled_up(blk, b):
    return blk if b == 0 else pltpu.roll(blk, blk.shape[0] - b, axis=0)


def _conv_fwd(z, w, wb, g, b, nex, tp, side=None):
    m = z.shape[0]
    tm = BLK
    sub = tm // CHALO
    cvc, cgc = C_CONV // CW, C_CONV // CW + 1

    def body(cv_ref, cg_ref, cvh_ref, cgh_ref, w_ref, wb_ref, g_ref, b_ref, cc_ref, cs_ref, win):
        i = pl.program_id(0)
        rows = i * tm + lax.broadcasted_iota(jnp.int32, (tm, 1), 0)
        hrows = i * tm - CHALO + lax.broadcasted_iota(jnp.int32, (CHALO, 1), 0)
        win[0:CHALO, :] = _cgate(cvh_ref[...], cgh_ref[...], hrows, nex, tp)
        win[CHALO:CHALO + tm, :] = _cgate(cv_ref[...], cg_ref[...], rows, nex, tp)
        for sb in range(sub):
            lo = sb * CHALO
            for c0 in range(0, CW, CLANES):
                blk = win[lo:lo + 2 * CHALO, c0:c0 + CLANES]
                acc = jnp.zeros((CHALO, CLANES), F32) + wb_ref[:, c0:c0 + CLANES]
                for b in range(8):
                    rb = _rolled_up(blk, b)
                    for a in range(5):
                        s = 8 * a + b
                        if 2 <= s <= CTAPS + 1:
                            acc = acc + w_ref[s - 2:s - 1, c0:c0 + CLANES] * rb[8 * a:8 * a + CHALO]
                cc_ref[lo:lo + CHALO, c0:c0 + CLANES] = acc
        xhat, _ = _ln_stats(cc_ref[...])
        cl = xhat * g_ref[...] + b_ref[...]
        cs_ref[...] = (cl * _sigmoid(cl)).astype(BF16)

    halo = lambda i: jnp.maximum(i * sub - 1, 0)
    vec = pl.BlockSpec((1, CW), lambda i: (0, 0))
    return _call(body, "conv_fwd", (jax.ShapeDtypeStruct((m, CW), F32), jax.ShapeDtypeStruct((m, CW), BF16)),
                 (m // tm,),
                 [pl.BlockSpec((tm, CW), lambda i: (i, cvc)), pl.BlockSpec((tm, CW), lambda i: (i, cgc)),
                  pl.BlockSpec((CHALO, CW), lambda i: (halo(i), cvc)), pl.BlockSpec((CHALO, CW), lambda i: (halo(i), cgc)),
                  pl.BlockSpec((CTAPS, CW), lambda i: (0, 0)), vec, vec, vec],
                 (pl.BlockSpec((tm, CW), lambda i: (i, 0)), pl.BlockSpec((tm, CW), lambda i: (i, 0))),
                 scratch=[pltpu.VMEM((CHALO + tm, CW), F32)], side=side)(z, z, z, z, w, wb.reshape(1, CW), g.reshape(1, CW), b.reshape(1, CW))


def _mix_fwd(a, cs, wap4, wcp4, z, tm, side=None):
    m = a.shape[0]
    ns = wap4.shape[2]

    def body(a_ref, cs_ref, wa_ref, wc_ref, ga_ref, gc_ref, ya_ref, yc_ref, mix_ref):
        av, cv = a_ref[...], cs_ref[...]
        for j in range(NCHIP):
            cols = slice(j * ns, (j + 1) * ns)
            ya, yc = _dot(av, wa_ref[j]), _dot(cv, wc_ref[j])
            ya_ref[:, cols] = ya.astype(BF16)
            yc_ref[:, cols] = yc.astype(BF16)
            mix_ref[:, cols] = (_sigmoid(ga_ref[:, cols]) * ya + _sigmoid(gc_ref[:, cols]) * yc).astype(BF16)

    wspec = pl.BlockSpec((NCHIP, AW, ns), lambda i: (0, 0, 0))
    row = lambda i: (i, 0)
    return _call(body, "mix_fwd",
                 (jax.ShapeDtypeStruct((m, D), BF16), jax.ShapeDtypeStruct((m, D), BF16), jax.ShapeDtypeStruct((m, D), BF16)),
                 (m // tm,),
                 [pl.BlockSpec((tm, AW), row), pl.BlockSpec((tm, CW), row), wspec, wspec,
                  pl.BlockSpec((tm, D), lambda i: (i, 0)), pl.BlockSpec((tm, D), lambda i: (i, 1))],
                 (pl.BlockSpec((tm, D), row), pl.BlockSpec((tm, D), row), pl.BlockSpec((tm, D), row)), side=side)(a, cs, wap4, wcp4, z, z)


def _mm_res_ln(name, a, w, res, g, b, tm, side=None):
    m, k = a.shape

    def body(a_ref, w_ref, res_ref, g_ref, b_ref, r_ref, h_ref, hb_ref):
        r = ALPHA * res_ref[...] + _dot(a_ref[...], w_ref[...])
        r_ref[...] = r
        xhat, _ = _ln_stats(r)
        y = xhat * g_ref[...] + b_ref[...]
        h_ref[...] = y
        hb_ref[...] = y.astype(BF16)

    row = lambda i: (i, 0)
    vec = pl.BlockSpec((1, D), lambda i: (0, 0))
    return _call(body, name,
                 (jax.ShapeDtypeStruct((m, D), F32), jax.ShapeDtypeStruct((m, D), F32), jax.ShapeDtypeStruct((m, D), BF16)),
                 (m // tm,),
                 [pl.BlockSpec((tm, k), row), pl.BlockSpec((k, D), lambda i: (0, 0)), pl.BlockSpec((tm, D), row), vec, vec],
                 (pl.BlockSpec((tm, D), row), pl.BlockSpec((tm, D), row), pl.BlockSpec((tm, D), row)), side=side)(a, w, res, g.reshape(1, D), b.reshape(1, D))


RCH = 16


def _lane_groups(width, most=768):
    n = -(-width // most)
    step = -(-width // (128 * n)) * 128
    return [(c, min(c + step, width)) for c in range(0, width, step)]


def _conv3(win, p, r0, c0, c1, w_ref, b_ref):
    blk = win[p, r0:r0 + FHALO + RCH, c0:c1]
    x0, x1, x2 = blk[FHALO:], pltpu.roll(blk, 1, axis=0)[FHALO:], pltpu.roll(blk, 2, axis=0)[FHALO:]
    return b_ref[:, c0:c1] + w_ref[0:1, c0:c1] * x2 + w_ref[1:2, c0:c1] * x1 + w_ref[2:3, c0:c1] * x0


def _loss_grad(y, target, nblk):
    nex = target.shape[0]
    m = y.shape[0]

    def body(y_ref, t_ref, dy_ref, acc_ref):
        bb, j = pl.program_id(0), pl.program_id(1)

        @pl.when((bb == 0) & (j == 0))
        def _():
            acc_ref[...] = jnp.zeros_like(acc_ref)

        @pl.when(j == 0)
        def _():
            dy_ref[...] = jnp.zeros_like(dy_ref)

        @pl.when(j > 0)
        def _():
            e = y_ref[...] - t_ref[...]
            dy_ref[...] = e * (1.0 / D)
            acc_ref[...] += jnp.sum((e * e).reshape(BLK // 8, 8, D), axis=0)

    return _call(body, "loss_grad", (jax.ShapeDtypeStruct((m, D), F32), jax.ShapeDtypeStruct((8, D), F32)), (nex, nblk),
                 [pl.BlockSpec((BLK, D), lambda bb, j: (bb * nblk + j, 0)),
                  pl.BlockSpec((None, BLK, D), lambda bb, j: (bb, jnp.maximum(j - 1, 0), 0))],
                 (pl.BlockSpec((BLK, D), lambda bb, j: (bb * nblk + j, 0)), pl.BlockSpec((8, D), lambda bb, j: (0, 0))))(y, target)


def _ln_bwd_call(name, dy, r, g, tm, a_list=(), w=None, cols=None, side=None):
    m = dy.shape[0]
    na = len(a_list)

    def body(*refs):
        dy_ref, r_ref, g_ref = refs[0:3]
        a_refs = refs[3:3 + na]
        w_ref = refs[3 + na] if na else None
        dr_ref, drb_ref, dg_ref, db_ref = refs[-4:]
        i = pl.program_id(0)
        dh = dy_ref[...]
        if na:
            dh = ALPHA * dh
            if cols is None:
                ns = w.shape[2]
                for j in range(NCHIP):
                    dh = dh + _dot_nt(a_refs[0][j // 2, :, (j % 2) * ns:(j % 2 + 1) * ns], w_ref[j])
            else:
                for a_ref, (c0, c1) in zip(a_refs, cols):
                    dh = dh + _dot_nt(a_ref[...], w_ref[:, c0:c1])
        xhat, rstd = _ln_stats(r_ref[...])
        dr = _ln_bwd(dh, xhat, rstd, g_ref[...])
        dr_ref[...] = dr
        drb_ref[...] = dr.astype(BF16)

        @pl.when(i == 0)
        def _():
            dg_ref[...] = jnp.zeros_like(dg_ref)
            db_ref[...] = jnp.zeros_like(db_ref)

        dg_ref[...] += jnp.sum(dh * xhat, axis=0, keepdims=True)
        db_ref[...] += jnp.sum(dh, axis=0, keepdims=True)

    row = lambda i: (i, 0)
    vec = pl.BlockSpec((1, D), lambda i: (0, 0))
    in_specs = [pl.BlockSpec((tm, D), row), pl.BlockSpec((tm, D), row), vec]
    for a in a_list:
        in_specs.append(pl.BlockSpec((2, tm, a.shape[2]), lambda i: (0, i, 0)) if a.ndim == 3 else pl.BlockSpec((tm, a.shape[1]), row))
    if na:
        in_specs.append(pl.BlockSpec(w.shape, (lambda i: (0, 0, 0)) if w.ndim == 3 else (lambda i: (0, 0))))
    return _call(body, name,
                 (jax.ShapeDtypeStruct((m, D), F32), jax.ShapeDtypeStruct((m, D), BF16),
                  jax.ShapeDtypeStruct((1, D), F32), jax.ShapeDtypeStruct((1, D), F32)),
                 (m // tm,), in_specs,
                 (pl.BlockSpec((tm, D), row), pl.BlockSpec((tm, D), row), vec, vec),
                 side=side)(dy, r, g.reshape(1, D), *a_list, *([w] if na else []))


def _ffn_bwd(drb, wdown, ug, up3, w, tm, nex, tp, side=None):
    _, m, dff = ug.shape
    ffs = dff // 2
    nt = m // tm

    def body(dr_ref, wd_ref, ug_ref, x_ref, wu_ref, wg_ref, o_ref, dw_ref, db_ref, dact, carry, dwacc, dbacc):
        i = pl.program_id(1)
        tile = nt - 1 - i
        dact[...] = _dot_nt(dr_ref[...], wd_ref[...])
        dwacc[...] = jnp.zeros_like(dwacc)
        dbacc[...] = jnp.zeros_like(dbacc)

        @pl.when(i == 0)
        def _():
            carry[...] = jnp.zeros_like(carry)
            dw_ref[...] = jnp.zeros_like(dw_ref)
            db_ref[...] = jnp.zeros_like(db_ref)

        fold = lambda t: t[0:8, :] + t[8:16, :]
        for r0 in reversed(range(0, tm, RCH)):
            rows = tile * tm + r0 + lax.broadcasted_iota(jnp.int32, (RCH, 1), 0)
            pad = _pad_rows(rows, nex, tp)
            for c0, c1 in _lane_groups(ffs, 384):
                u = ug_ref[0, r0:r0 + RCH, c0:c1].astype(F32)
                g = ug_ref[1, r0:r0 + RCH, c0:c1].astype(F32)
                da = dact[r0:r0 + RCH, c0:c1]
                cdf = 0.5 * (1.0 + lax.erf(g * (1.0 / math.sqrt(2.0))))
                pdf = jnp.exp(-0.5 * g * g) * (1.0 / math.sqrt(2.0 * math.pi))
                for p, w_ref, d0 in ((0, wu_ref, da * (g * cdf)), (1, wg_ref, da * u * (cdf + g * pdf))):
                    dblk = jnp.concatenate([d0, carry[p, :, c0:c1]], axis=0)
                    d1 = pltpu.roll(dblk, RCH + FHALO - 1, axis=0)[:RCH]
                    d2 = pltpu.roll(dblk, RCH + FHALO - 2, axis=0)[:RCH]
                    carry[p, :, c0:c1] = d0[0:FHALO]
                    dpre = w_ref[2:3, c0:c1] * d0 + w_ref[1:2, c0:c1] * d1 + w_ref[0:1, c0:c1] * d2
                    o_ref[p, r0:r0 + RCH, c0:c1] = jnp.where(pad, 0.0, dpre).astype(BF16)
                    x0 = x_ref[p, r0:r0 + RCH, c0:c1].astype(F32)
                    dwacc[p, 2, :, c0:c1] += fold(d0 * x0)
                    dwacc[p, 1, :, c0:c1] += fold(d1 * x0)
                    dwacc[p, 0, :, c0:c1] += fold(d2 * x0)
                    dbacc[p, :, c0:c1] += fold(d0)
        for p in range(2):
            for k in range(FTAPS):
                dw_ref[p, k:k + 1, :] += jnp.sum(dwacc[p, k], axis=0, keepdims=True)
            db_ref[p] += jnp.sum(dbacc[p], axis=0, keepdims=True)

    wide = pl.BlockSpec((2, tm, ffs), lambda c, i: (0, nt - 1 - i, c))
    return _call(body, "ffn_bwd",
                 (jax.ShapeDtypeStruct((2, m, dff), BF16), jax.ShapeDtypeStruct((2, FTAPS, dff), F32),
                  jax.ShapeDtypeStruct((2, 1, dff), F32)),
                 (2, nt),
                 [pl.BlockSpec((tm, D), lambda c, i: (nt - 1 - i, 0)), pl.BlockSpec((ffs, D), lambda c, i: (c, 0)), wide, wide,
                  pl.BlockSpec((FTAPS, ffs), lambda c, i: (0, c)), pl.BlockSpec((FTAPS, ffs), lambda c, i: (0, c + 2))],
                 (wide, pl.BlockSpec((2, FTAPS, ffs), lambda c, i: (0, 0, c)), pl.BlockSpec((2, 1, ffs), lambda c, i: (0, 0, c))),
                 scratch=[pltpu.VMEM((tm, ffs), F32), pltpu.VMEM((2, FHALO, ffs), F32),
                          pltpu.VMEM((2, FTAPS, 8, ffs), F32), pltpu.VMEM((2, 8, ffs), F32)], side=side)(drb, wdown, ug, up3, w, w)


def _mm_tn(name, a, b, tk, tn, b_cols=None, chip_out=False, side=None):
    m, k = a.shape
    n = b.shape[-1] * (2 if b.ndim == 3 else 1)

    def body(a_ref, b_ref, o_ref):
        o_ref[...] = _dot_tn(a_ref[...], b_ref[...])

    if b.ndim == 3:
        bspec = pl.BlockSpec((None, m, tn), lambda kk, j: (b_cols(j)[0], 0, b_cols(j)[1]))
    else:
        bspec = pl.BlockSpec((m, tn), lambda kk, j: (0, j))
    if chip_out:
        oshape, ospec = (n // tn, k, tn), pl.BlockSpec((None, tk, tn), lambda kk, j: (j, kk, 0))
    else:
        oshape, ospec = (k, n), pl.BlockSpec((tk, tn), lambda kk, j: (kk, j))
    return _call(body, name, jax.ShapeDtypeStruct(oshape, F32), (k // tk, n // tn),
                 [pl.BlockSpec((m, tk), lambda kk, j: (0, kk)), bspec], ospec, side=side)(a, b)


def _gate_bwd(drb, wout, ya, yc, z, tm):
    m = drb.shape[0]

    def body(dr_ref, w_ref, ya_ref, yc_ref, ga_ref, gc_ref, dya_ref, dyc_ref, dz_ref, cs_ref):
        i = pl.program_id(0)
        dmix = _dot_nt(dr_ref[...], w_ref[...])
        sa, sc = _sigmoid(ga_ref[...]), _sigmoid(gc_ref[...])
        dya_ref[...] = (dmix * sa).astype(BF16)
        dyc_ref[...] = (dmix * sc).astype(BF16)
        dga = dmix * ya_ref[...].astype(F32) * sa * (1.0 - sa)
        dgc = dmix * yc_ref[...].astype(F32) * sc * (1.0 - sc)
        dz_ref[:, 0:D] = dga.astype(BF16)
        dz_ref[:, D:2 * D] = dgc.astype(BF16)

        @pl.when(i == 0)
        def _():
            cs_ref[...] = jnp.zeros_like(cs_ref)

        cs_ref[:, 0:D] += jnp.sum(dga, axis=0, keepdims=True)
        cs_ref[:, D:2 * D] += jnp.sum(dgc, axis=0, keepdims=True)

    row = lambda i: (i, 0)
    return _call(body, "gate_bwd",
                 (jax.ShapeDtypeStruct((m, D), BF16), jax.ShapeDtypeStruct((m, D), BF16),
                  jax.ShapeDtypeStruct((m, 2 * D), BF16), jax.ShapeDtypeStruct((1, 2 * D), F32)),
                 (m // tm,),
                 [pl.BlockSpec((tm, D), row), pl.BlockSpec((D, D), lambda i: (0, 0)), pl.BlockSpec((tm, D), row),
                  pl.BlockSpec((tm, D), row), pl.BlockSpec((tm, D), lambda i: (i, 0)), pl.BlockSpec((tm, D), lambda i: (i, 1))],
                 (pl.BlockSpec((tm, D), row), pl.BlockSpec((tm, D), row), pl.BlockSpec((tm, 2 * D), row),
                  pl.BlockSpec((1, 2 * D), lambda i: (0, 0))))(drb, wout, ya, yc, z, z)


def _conv_bwd_a(dyc, wcp4, cc, g, b, tm):
    m = cc.shape[0]
    ns = wcp4.shape[2]

    def body(dy_ref, w_ref, cc_ref, g_ref, b_ref, dcc_ref, dg_ref, db_ref, dwb_ref):
        i = pl.program_id(0)
        dcs = jnp.zeros((tm, CW), F32)
        for j in range(NCHIP):
            dcs = dcs + _dot_nt(dy_ref[:, j * ns:(j + 1) * ns], w_ref[j])
        xhat, rstd = _ln_stats(cc_ref[...])
        cl = xhat * g_ref[...] + b_ref[...]
        sg = _sigmoid(cl)
        dcl = dcs * sg * (1.0 + cl * (1.0 - sg))
        dcc = _ln_bwd(dcl, xhat, rstd, g_ref[...])
        dcc_ref[...] = dcc

        @pl.when(i == 0)
        def _():
            dg_ref[...] = jnp.zeros_like(dg_ref)
            db_ref[...] = jnp.zeros_like(db_ref)
            dwb_ref[...] = jnp.zeros_like(dwb_ref)

        dg_ref[...] += jnp.sum(dcl * xhat, axis=0, keepdims=True)
        db_ref[...] += jnp.sum(dcl, axis=0, keepdims=True)
        dwb_ref[...] += jnp.sum(dcc, axis=0, keepdims=True)

    row = lambda i: (i, 0)
    vec = pl.BlockSpec((1, CW), lambda i: (0, 0))
    v = jax.ShapeDtypeStruct((1, CW), F32)
    return _call(body, "conv_bwd_a", (jax.ShapeDtypeStruct((m, CW), F32), v, v, v), (m // tm,),
                 [pl.BlockSpec((tm, D), row), pl.BlockSpec((NCHIP, CW, ns), lambda i: (0, 0, 0)), pl.BlockSpec((tm, CW), row), vec, vec],
                 (pl.BlockSpec((tm, CW), row), vec, vec, vec))(dyc, wcp4, cc, g.reshape(1, CW), b.reshape(1, CW))


def _conv_bwd_b(dcc, z, w, nex, tp, side=None):
    m = dcc.shape[0]
    tm = BLK
    sub = tm // CHALO
    nt = m // tm
    cvc, cgc = C_CONV // CW, C_CONV // CW + 1

    def body(d_ref, dh_ref, cv_ref, cg_ref, w_ref, dz_ref, dw_ref, cs_ref, dwin, dwacc):
        i = pl.program_id(0)
        rows = i * tm + lax.broadcasted_iota(jnp.int32, (tm, 1), 0)
        dwin[0:tm, :] = d_ref[...]
        dwin[tm:tm + CHALO, :] = jnp.where(i == nt - 1, 0.0, dh_ref[...])

        @pl.when(i == 0)
        def _():
            dwacc[...] = jnp.zeros_like(dwacc)
            cs_ref[...] = jnp.zeros_like(cs_ref)

        fold = lambda t: (t[0:8] + t[8:16]) + (t[16:24] + t[24:32])
        for sb in range(sub):
            lo = sb * CHALO
            pad = _pad_rows(rows[lo:lo + CHALO], nex, tp)
            for c0 in range(0, CW, CLANES):
                cs_ = slice(c0, c0 + CLANES)
                cv = cv_ref[lo:lo + CHALO, cs_]
                sg = _sigmoid(cg_ref[lo:lo + CHALO, cs_])
                cgin = jnp.where(pad, 0.0, cv * sg)
                blk = dwin[lo:lo + 2 * CHALO, cs_]
                acc = jnp.zeros((CHALO, CLANES), F32)
                for b in range(8):
                    rb = _rolled_up(blk, b)
                    for a in range(4):
                        s = 8 * a + b
                        if s <= CTAPS - 1:
                            k = CTAPS - 1 - s
                            sh = rb[8 * a:8 * a + CHALO]
                            acc = acc + w_ref[k:k + 1, cs_] * sh
                            dwacc[k, :, cs_] += fold(sh * cgin)
                dcg = jnp.where(pad, 0.0, acc)
                dcv = dcg * sg
                dgt = dcg * cv * sg * (1.0 - sg)
                dz_ref[lo:lo + CHALO, cs_] = dcv.astype(BF16)
                dz_ref[lo:lo + CHALO, CW + c0:CW + c0 + CLANES] = dgt.astype(BF16)
                cs_ref[:, cs_] += jnp.sum(dcv, axis=0, keepdims=True)
                cs_ref[:, CW + c0:CW + c0 + CLANES] += jnp.sum(dgt, axis=0, keepdims=True)

        @pl.when(i == nt - 1)
        def _():
            for k in range(CHALO):
                dw_ref[k:k + 1, :] = jnp.sum(dwacc[k], axis=0, keepdims=True)

    nxt = lambda i: jnp.minimum((i + 1) * sub, m // CHALO - 1)
    return _call(body, "conv_bwd_b",
                 (jax.ShapeDtypeStruct((m, 2 * CW), BF16), jax.ShapeDtypeStruct((CHALO, CW), F32),
                  jax.ShapeDtypeStruct((1, 2 * CW), F32)),
                 (nt,),
                 [pl.BlockSpec((tm, CW), lambda i: (i, 0)), pl.BlockSpec((CHALO, CW), lambda i: (nxt(i), 0)),
                  pl.BlockSpec((tm, CW), lambda i: (i, cvc)), pl.BlockSpec((tm, CW), lambda i: (i, cgc)),
                  pl.BlockSpec((CTAPS, CW), lambda i: (0, 0))],
                 (pl.BlockSpec((tm, 2 * CW), lambda i: (i, 0)), pl.BlockSpec((CHALO, CW), lambda i: (0, 0)),
                  pl.BlockSpec((1, 2 * CW), lambda i: (0, 0))),
                 scratch=[pltpu.VMEM((tm + CHALO, CW), F32), pltpu.VMEM((CHALO, 8, CW), F32)], side=side)(dcc, dcc, z, z, w)


def _attn_bwd(z, bias, sinks, dya, wap4, nex, nblk, side=None):
    m = z.shape[0]
    tp = nblk * BLK
    ns = wap4.shape[2]
    blk_of = lambda j: nblk - 1 - j
    z3 = z.reshape(nex, tp, z.shape[1])
    dya3 = dya.reshape(nex, tp, D)

    def body(q_ref, cur_ref, prev_ref, meta_ref, bias_ref, sink_ref, dy_ref, w_ref,
             dz_ref, cs_ref, dsk_ref, dbias_ref, ks, vs, carry, macc, dqacc, dkv, okv):
        j = pl.program_id(0)
        n = nblk - 1 - j

        @pl.when(j == 0)
        def _():
            carry[...] = jnp.zeros_like(carry)
            macc[...] = jnp.zeros_like(macc)
            cs_ref[...] = jnp.zeros_like(cs_ref)
            dsk_ref[...] = jnp.zeros_like(dsk_ref)

        @pl.when((j == 0) | (n <= 1))
        def _():
            dbias_ref[...] = jnp.zeros_like(dbias_ref)

        lane = lax.broadcasted_iota(jnp.int32, (1, BLK), 1)
        dsk = jnp.zeros((1, BLK), F32)

        def head(e, h, q, da, out):
            hk = h // GRP
            yield from _head_softmax(q, ks[e, hk], bias_ref, sink_ref, h, out)
            pn = out["p"]
            doh = da[:, h * HD:(h + 1) * HD].astype(BF16)
            yield
            dp = _dot_nt(doh, vs[e, hk])
            yield
            dl = jnp.sum(pn * dp, axis=-1, keepdims=True)
            yield
            ds = pn * (dp - dl)
            yield
            dbias_ref[h] += ds
            out["dsink"] = jnp.sum(-out["sink"] * dl)
            yield
            dsb = (ds * (HD ** -0.5)).astype(BF16)
            yield
            dqacc[e, :, h * HD:(h + 1) * HD] = _dot(dsb, ks[e, hk])
            out["ds"], out["pb"], out["do"] = dsb, pn.astype(BF16), doh
            yield

        def kv_head(e, hk, outs):
            rows = lambda key: jnp.concatenate([outs[hk * GRP + g][key] for g in range(GRP)], axis=0)
            dk = _dot_tn(rows("ds"), rows("q"))
            yield
            dv = _dot_tn(rows("pb"), rows("do"))
            yield
            dkv[e, :, hk * HD:(hk + 1) * HD] = dk
            dkv[e, :, KVW + hk * HD:KVW + (hk + 1) * HD] = dv
            yield

        das, outs = [], [[{} for _ in range(NQ)] for _ in range(nex)]
        for e in range(nex):
            _fill_kv(ks, vs, e, prev_ref, cur_ref, meta_ref)
            da = jnp.zeros((BLK, AW), F32)
            for jj in range(NCHIP):
                da = da + _dot_nt(dy_ref[e, :, jj * ns:(jj + 1) * ns], w_ref[jj])
            das.append(da)
        _interleave([head(e, h, q_ref[e], das[e], outs[e][h]) for e in range(nex) for h in range(NQ)])
        _interleave([kv_head(e, hk, outs[e]) for e in range(nex) for hk in range(NKV)])
        for e in range(nex):
            for h in range(NQ):
                dsk = dsk + jnp.where(lane == h, outs[e][h]["dsink"], 0.0)
            macc[e] += dkv[e, 2 * BLK:2 * BLK + N_META, :]
            okv[e] = dkv[e, BLK:2 * BLK, :] + carry[e]
            carry[e] = dkv[e, 0:BLK, :]

            @pl.when(n == 0)
            def _():
                okv[e, PAD:BLK, :] += macc[e]

            dq = dqacc[e]
            ok = okv[e]
            dz_ref[e, :, 0:AW] = dq.astype(BF16)
            dz_ref[e, :, AW:AW + 2 * KVW] = ok.astype(BF16)
            cs_ref[:, 0:AW] += jnp.sum(dq, axis=0, keepdims=True)
            cs_ref[:, AW:AW + 2 * KVW] += jnp.sum(ok, axis=0, keepdims=True)
        dsk_ref[...] += dsk

    wz = AW + 2 * KVW
    specs = _attn_specs(nex, blk_of) + [
        pl.BlockSpec((nex, BLK, D), lambda j: (0, blk_of(j), 0)),
        pl.BlockSpec((NCHIP, AW, ns), lambda j: (0, 0, 0))]
    res = _call(body, "attn_bwd",
                (jax.ShapeDtypeStruct((nex, tp, wz), BF16), jax.ShapeDtypeStruct((1, wz), F32), jax.ShapeDtypeStruct((1, BLK), F32),
                 jax.ShapeDtypeStruct((1, 3, NQ, BLK, NKEY), F32)),
                (nblk,), specs,
                (pl.BlockSpec((nex, BLK, wz), lambda j: (0, blk_of(j), 0)), pl.BlockSpec((1, wz), lambda j: (0, 0)),
                 pl.BlockSpec((1, BLK), lambda j: (0, 0)),
                 pl.BlockSpec((None, None, NQ, BLK, NKEY), lambda j: (0, jnp.minimum(blk_of(j), 2), 0, 0, 0))),
                scratch=[pltpu.VMEM((nex, NKV, NKEY, HD), BF16), pltpu.VMEM((nex, NKV, NKEY, HD), BF16),
                         pltpu.VMEM((nex, BLK, 2 * KVW), F32), pltpu.VMEM((nex, N_META, 2 * KVW), F32),
                         pltpu.VMEM((nex, BLK, AW), F32), pltpu.VMEM((nex, NKEY, 2 * KVW), F32),
                         pltpu.VMEM((nex, BLK, 2 * KVW), F32)],
                side=side)(z3, z3, z3, z3, bias, sinks.reshape(1, NQ), dya3, wap4)
    return (res[0].reshape(m, wz),) + tuple(res[1:])


def _tile_rows(rows, cols, target_bytes=1 << 20):
    best = None
    for t in range(8, rows + 1, 8):
        if rows % t == 0 and t * cols * 4 <= target_bytes:
            best = t
    return best or rows


def _sum0(name, x):
    n, r, c = x.shape
    tr = _tile_rows(r, c * n)

    def body(x_ref, o_ref):
        acc = x_ref[0]
        for k in range(1, n):
            acc = acc + x_ref[k]
        o_ref[...] = acc

    return _call(body, name, jax.ShapeDtypeStruct((r, c), F32), (r // tr,),
                 [pl.BlockSpec((n, tr, c), lambda i: (0, i, 0))], pl.BlockSpec((tr, c), lambda i: (i, 0)))(x)


def _adamw(name, w, g, mom, vel):
    r, c = w.shape
    tr = _tile_rows(r, c)
    c1 = 1.0 / (1.0 - ADAM_B1 ** ADAM_STEP)
    c2 = 1.0 / (1.0 - ADAM_B2 ** ADAM_STEP)

    def body(w_ref, g_ref, m_ref, v_ref, d_ref, mo_ref, vo_ref):
        gg = g_ref[...]
        mn = ADAM_B1 * m_ref[...] + (1.0 - ADAM_B1) * gg
        vn = ADAM_B2 * v_ref[...] + (1.0 - ADAM_B2) * (gg * gg)
        mo_ref[...] = mn
        vo_ref[...] = vn
        d_ref[...] = -ADAM_LR * ((mn * c1) / (jnp.sqrt(vn * c2) + ADAM_EPS) + ADAM_WD * w_ref[...])

    spec = pl.BlockSpec((tr, c), lambda i: (i, 0))
    o = jax.ShapeDtypeStruct((r, c), F32)
    return _call(body, name, (o, o, o), (r // tr,), [spec] * 4, (spec, spec, spec))(w, g, mom, vel)


def _adamw_matrix(w, joined, own, mom, vel, core):
    _, r, c = w.shape
    h = r // 2
    th = _tile_rows(h, c)
    nt = h // th
    c1 = 1.0 / (1.0 - ADAM_B1 ** ADAM_STEP)
    c2 = 1.0 / (1.0 - ADAM_B2 ** ADAM_STEP)

    def body(core_ref, w_ref, j_ref, o0_ref, o1_ref, m_ref, v_ref, g_ref, d_ref, mo_ref, vo_ref):
        layer, half = pl.program_id(0), pl.program_id(1)
        mine = jnp.where(layer == 0, o0_ref[...], o1_ref[...])
        gg = jnp.where(half == core_ref[0], mine, j_ref[...])
        g_ref[...] = gg
        mn = ADAM_B1 * m_ref[...] + (1.0 - ADAM_B1) * gg
        vn = ADAM_B2 * v_ref[...] + (1.0 - ADAM_B2) * (gg * gg)
        mo_ref[...] = mn
        vo_ref[...] = vn
        d_ref[...] = -ADAM_LR * ((mn * c1) / (jnp.sqrt(vn * c2) + ADAM_EPS) + ADAM_WD * w_ref[...])

    full = pl.BlockSpec((None, th, c), lambda l, hh, i, cr: (l, hh * nt + i, 0))
    other = pl.BlockSpec((None, th, c), lambda l, hh, i, cr: (l, (1 - cr[0]) * nt + i, 0))
    part = pl.BlockSpec((th, c), lambda l, hh, i, cr: (i, 0))
    o = jax.ShapeDtypeStruct(w.shape, F32)
    return _call(body, "adamw_matrix", (o, o, o, o), (2, 2, nt), [full, other, part, part, full, full],
                 (full, full, full, full), prefetch=1)(core, w, joined, own[0], own[1], mom, vel)


def _place():
    x, y, c = lax.axis_index("x"), lax.axis_index("y"), lax.axis_index("c")
    others = [(1 - x, y), (x, 1 - y), (1 - x, 1 - y)]
    return x, y, c, others


def _gather_job(items):
    nw = len(items)

    def views(s_ref, g_ref, layer, c):
        if layer is None:
            return s_ref, s_ref.at[c], lambda chip, cc: g_ref.at[chip, cc]
        hr = s_ref.shape[1] // 2
        return s_ref.at[layer], s_ref.at[layer, pl.ds(c * hr, hr)], lambda chip, cc: g_ref.at[chip, pl.ds(cc * hr, hr)]

    def copies(s_refs, g_refs, send, recv):
        x, y, c, others = _place()
        chip = 2 * x + y
        firsts, arrive, passed, arrive2 = [], [], [], []
        for w, (_, layer) in enumerate(items):
            whole, src, dst = views(s_refs[w], g_refs[w], layer, c)

            def rc(kk, s, d, to, w=w):
                return pltpu.make_async_remote_copy(src_ref=s, dst_ref=d, send_sem=send.at[w, kk], recv_sem=recv.at[w, kk],
                                                    device_id=to, device_id_type=MESH)
            firsts.append(rc(6, whole, g_refs[w].at[chip], (x, y, 1 - c)))
            arrive2.append(rc(6, whole, g_refs[w].at[chip], (x, y, c)))
            for k, (px, py) in enumerate(others):
                got, got2 = dst(2 * px + py, c), dst(2 * px + py, 1 - c)
                firsts.append(rc(k, src, dst(chip, c), (px, py, c)))
                arrive.append(rc(k, got, got, (x, y, c)))
                passed.append(rc(3 + k, got, got, (x, y, 1 - c)))
                arrive2.append(rc(3 + k, got2, got2, (x, y, c)))
        return firsts, arrive, passed, arrive2

    def start(s_refs, g_refs, send, recv):
        for cp in copies(s_refs, g_refs, send, recv)[0]:
            cp.start()

    def finish(s_refs, g_refs, send, recv):
        firsts, arrive, passed, arrive2 = copies(s_refs, g_refs, send, recv)
        for a, p in zip(arrive, passed):
            a.wait_recv()
            p.start()
        for a in arrive2:
            a.wait_recv()
        for cp in firsts + passed:
            cp.wait_send()

    outs = [jax.ShapeDtypeStruct((NCHIP,) + (s.shape if layer is None else s.shape[1:]), s.dtype) for s, layer in items]
    return _Job([s for s, _ in items], outs, (nw, 7), start, finish)


def _swap_job(grads):
    def copies(d_refs, a_refs, send, recv):
        x, y, c, _ = _place()
        cps = []
        for w in range(len(grads)):
            h = d_refs[w].shape[1] // 2
            cps.append(pltpu.make_async_remote_copy(
                src_ref=d_refs[w].at[:, pl.ds((1 - c) * h, h), :], dst_ref=a_refs[w], send_sem=send.at[w], recv_sem=recv.at[w],
                device_id=(x, y, 1 - c), device_id_type=MESH))
        return cps

    def start(*r):
        for cp in copies(*r):
            cp.start()

    def finish(*r):
        for cp in copies(*r):
            cp.wait()

    outs = [jax.ShapeDtypeStruct((NCHIP, g.shape[1] // 2, g.shape[2]), g.dtype) for g in grads]
    return _Job(list(grads), outs, (len(grads),), start, finish)


def _exchange_job(parts):
    def copies(q_refs, b_refs, send, recv):
        x, y, c, others = _place()
        cps = []
        for w in range(len(parts)):
            for k, (px, py) in enumerate(others):
                cps.append(pltpu.make_async_remote_copy(
                    src_ref=q_refs[w].at[2 * px + py], dst_ref=b_refs[w].at[k], send_sem=send.at[w, k], recv_sem=recv.at[w, k],
                    device_id=(px, py, c), device_id_type=MESH))
        return cps

    def start(*r):
        for cp in copies(*r):
            cp.start()

    def finish(*r):
        for cp in copies(*r):
            cp.wait()

    outs = [jax.ShapeDtypeStruct((3,) + p.shape[1:], p.dtype) for p in parts]
    return _Job(list(parts), outs, (len(parts), 3), start, finish)


def _run_job(name, job):
    n_in, n_out = len(job.ins), len(job.outs)

    def body(*refs):
        ins, outs = refs[:n_in], refs[n_in:n_in + n_out]
        send, recv = refs[n_in + n_out:]
        job.start(ins, outs, send, recv)
        job.finish(ins, outs, send, recv)

    return pl.pallas_call(
        body, name=name, out_shape=tuple(job.outs), in_specs=[_ANY] * n_in, out_specs=tuple([_ANY] * n_out),
        scratch_shapes=[pltpu.SemaphoreType.DMA(job.sems), pltpu.SemaphoreType.DMA(job.sems)])(*job.ins)


def _sibling_join(halves):
    nw = len(halves)

    def body(*refs):
        h_refs = refs[:2 * nw]
        f_refs = refs[2 * nw:3 * nw]
        send, recv = refs[3 * nw:]
        x, y, c, _ = _place()
        cps = []
        for w in range(nw):
            for l in range(2):
                src = h_refs[2 * w + l]
                h = src.shape[0]
                dst = f_refs[w].at[l, pl.ds(c * h, h), :]
                cp = pltpu.make_async_remote_copy(src_ref=src, dst_ref=dst, send_sem=send.at[w, l], recv_sem=recv.at[w, l],
                                                  device_id=(x, y, 1 - c), device_id_type=MESH)
                cp.start()
                cps.append(cp)
        for w in range(nw):
            for l in range(2):
                src = h_refs[2 * w + l]
                h = src.shape[0]
                other = f_refs[w].at[l, pl.ds((1 - c) * h, h), :]
                pltpu.make_async_remote_copy(src_ref=src, dst_ref=other, send_sem=send.at[w, l], recv_sem=recv.at[w, l],
                                             device_id=(x, y, c), device_id_type=MESH).wait_recv()
        for cp in cps:
            cp.wait_send()

    flat = [a for pair in halves for a in pair]
    outs = tuple(jax.ShapeDtypeStruct((2, 2 * pair[0].shape[0], pair[0].shape[1]), F32) for pair in halves)
    return pl.pallas_call(
        body, name="grad_sibling_join", out_shape=outs, in_specs=[_ANY] * (2 * nw), out_specs=tuple([_ANY] * nw),
        scratch_shapes=[pltpu.SemaphoreType.DMA((nw, 2)), pltpu.SemaphoreType.DMA((nw, 2))])(*flat)


def _allgather_small(v):
    r = v.shape[0]

    def body(x_ref, out_ref, send_sems, recv_sems, local_sem):
        x, y, c, chips = _place()
        me, sibling = (x, y, c), (x, y, 1 - c)

        def slab(px, py, pc):
            return out_ref.at[4 * px + 2 * py + pc]

        def copy(k, block, to, src=None):
            return pltpu.make_async_remote_copy(src_ref=slab(*block) if src is None else src, dst_ref=slab(*block),
                                                send_sem=send_sems.at[k], recv_sem=recv_sems.at[k],
                                                device_id=to, device_id_type=MESH)

        mine = pltpu.make_async_copy(x_ref, slab(*me), local_sem)
        mine.start()
        first = [copy(0, me, sibling, src=x_ref)]
        first += [copy(1 + j, me, (*chip, c), src=x_ref) for j, chip in enumerate(chips)]
        for cp in first:
            cp.start()
        passed = [copy(4 + j, (*chip, c), sibling) for j, chip in enumerate(chips)]
        for j, chip in enumerate(chips):
            copy(1 + j, (*chip, c), me).wait_recv()
            passed[j].start()
        copy(0, sibling, me).wait_recv()
        for j, chip in enumerate(chips):
            copy(4 + j, (*chip, 1 - c), me).wait_recv()
        for cp in first + passed:
            cp.wait_send()
        mine.wait()

    return pl.pallas_call(
        body, name="allgather_small", out_shape=jax.ShapeDtypeStruct((8, r, 128), F32),
        in_specs=[pl.BlockSpec(memory_space=pltpu.VMEM)], out_specs=pl.BlockSpec(memory_space=pltpu.VMEM),
        scratch_shapes=[pltpu.SemaphoreType.DMA((7,)), pltpu.SemaphoreType.DMA((7,)), pltpu.SemaphoreType.DMA],
    )(v)


def _add_half(name, d, a, c):
    _, h, cols = a.shape
    nt = 1
    th = h // nt

    def body(c_ref, d_ref, a_ref, o_ref):
        o_ref[...] = (d_ref[...] + a_ref[...]).astype(BF16)

    return _call(body, name, jax.ShapeDtypeStruct(a.shape, BF16), (NCHIP, nt),
                 [pl.BlockSpec((None, th, cols), lambda p, i, cr: (p, cr[0] * nt + i, 0)),
                  pl.BlockSpec((None, th, cols), lambda p, i, cr: (p, i, 0))],
                 pl.BlockSpec((None, th, cols), lambda p, i, cr: (p, i, 0)), prefetch=1)(c, d, a)


def _add_chips(name, d, a, b, where):
    _, h, cols = a.shape
    th = h // 4 if (h % 64 == 0) else h
    nt = h // th

    def body(w_ref, d_ref, a_ref, b_ref, o_ref):
        own = d_ref[...] + a_ref[...]
        o_ref[...] = ((own + b_ref[0].astype(F32)) + b_ref[1].astype(F32)) + b_ref[2].astype(F32)

    return _call(body, name, jax.ShapeDtypeStruct((h, cols), F32), (nt,),
                 [pl.BlockSpec((None, th, cols), lambda i, wr: (wr[0], wr[1] * nt + i, 0)),
                  pl.BlockSpec((None, th, cols), lambda i, wr: (wr[0], i, 0)),
                  pl.BlockSpec((3, th, cols), lambda i, wr: (0, i, 0))],
                 pl.BlockSpec((th, cols), lambda i, wr: (i, 0)), prefetch=1)(where, d, a, b)


def _pack(arrs):
    pieces = []
    for a in arrs:
        f = a.reshape(-1)
        n = -(-f.shape[0] // 1024) * 1024
        pieces.append(jnp.pad(f, (0, n - f.shape[0])).reshape(-1, 128))
    return jnp.concatenate(pieces, axis=0)


def _unpack(buf, shapes):
    out, r = [], 0
    for s in shapes:
        n = int(np.prod(s))
        rows = -(-n // 1024) * 8
        out.append(buf[r:r + rows].reshape(-1)[:n].reshape(s))
        r += rows
    return out


def kernel(x, meta_tokens, in_ln_g, in_ln_b, rel_bias, w_in, b_in, attn_sinks, w_attn_proj, conv_dw, conv_dw_b, conv_ln_g, conv_ln_b, w_conv_proj, w_out, ln1_g, ln1_b, ffn_w_up, ffn_dw, ffn_dw_b, ffn_w_down, ln2_g, ln2_b, loss_target, m_meta_tokens, m_in_ln_g, m_in_ln_b, m_rel_bias, m_w_in, m_b_in, m_attn_sinks, m_w_attn_proj, m_conv_dw, m_conv_dw_b, m_conv_ln_g, m_conv_ln_b, m_w_conv_proj, m_w_out, m_ln1_g, m_ln1_b, m_ffn_w_up, m_ffn_dw, m_ffn_dw_b, m_ffn_w_down, m_ln2_g, m_ln2_b, v_meta_tokens, v_in_ln_g, v_in_ln_b, v_rel_bias, v_w_in, v_b_in, v_attn_sinks, v_w_attn_proj, v_conv_dw, v_conv_dw_b, v_conv_ln_g, v_conv_ln_b, v_w_conv_proj, v_w_out, v_ln1_g, v_ln1_b, v_ffn_w_up, v_ffn_dw, v_ffn_dw_b, v_ffn_w_down, v_ln2_g, v_ln2_b):
    nex, seq, _ = x.shape
    nblk = seq // BLK + 1
    tp = nblk * BLK
    m = nex * tp
    tm = _row_tile(m)
    ffs = ffn_w_up.shape[2]
    dff = 2 * ffs
    cx, cy, cc = lax.axis_index("x"), lax.axis_index("y"), lax.axis_index("c")
    chip = (2 * cx + cy).astype(jnp.int32)
    core = cc.astype(jnp.int32)

    names = ("in", "ap", "cp", "out", "up", "down")
    big = dict(zip(names, [w_in, w_attn_proj, w_conv_proj, w_out, ffn_w_up, ffn_w_down]))
    sb = {k: v.astype(BF16) for k, v in big.items()}
    gathered = {}

    first_items = [(meta_tokens.reshape(2, N_META // 2, -1), None), (conv_dw, None), (ffn_dw, None)]
    g_meta, g_cdw, g_fdw = _run_job("gather_small", _gather_job(first_items))
    meta_full = jnp.transpose(g_meta, (1, 2, 0, 3)).reshape(N_META, D)
    bias_tab = _bias_build(rel_bias)

    fwd_plan = {("embed_ln", 0): [("in", 0)],
                ("in_proj", 0): [("ap", 0), ("cp", 0), ("out", 0)], ("attn_fwd", 0): [("up", 0)],
                ("conv_fwd", 0): [("down", 0)], ("mix_fwd", 0): [("in", 1)],
                ("out_proj_ln", 0): [("ap", 1), ("cp", 1), ("out", 1)], ("ffn_up_act", 0): [("up", 1), ("down", 1)]}

    def fwd(tag, l, fn, *args):
        keys = fwd_plan.get((tag, l))
        if not keys:
            return fn(*args)
        items = [(sb[k], kl) for k, kl in keys]
        res = fn(*args, side=_gather_job(items))
        for key, g in zip(keys, res[-len(keys):]):
            gathered[key] = g
        main = res[:-len(keys)]
        return main[0] if len(main) == 1 else main

    def layer_weights(l):
        return dict(
            win=_w_in_from_chips(gathered[("in", l)]), bin=_to_new(b_in[l]).reshape(1, IN_COLS),
            cdw=jnp.transpose(g_cdw[:, l], (1, 0, 2)).reshape(CTAPS, CW),
            fdw=jnp.transpose(g_fdw[:, l], (1, 0, 2)).reshape(FTAPS, 2 * dff),
            fdwb=ffn_dw_b[l].reshape(1, 2 * dff))

    raw, h, hb = fwd("embed_ln", 0, _embed_ln, x, meta_full, in_ln_g, in_ln_b, nblk)
    saved, lw = [], []
    for l in range(DEPTH):
        p = layer_weights(l)
        z = fwd("in_proj", l, functools.partial(_mm_bias, "in_proj"), hb, p["win"], p["bin"], IN_COLS // 3, tm)
        a = fwd("attn_fwd", l, _attn_fwd, z, bias_tab, attn_sinks[l], nex, nblk)
        ccv, cs = fwd("conv_fwd", l, _conv_fwd, z, p["cdw"], conv_dw_b[l], conv_ln_g[l], conv_ln_b[l], nex, tp)
        p["wap"], p["wcp"] = gathered[("ap", l)], gathered[("cp", l)]
        ya, yc, mixed = fwd("mix_fwd", l, _mix_fwd, a, cs, p["wap"], p["wcp"], z, tm)
        p["wout"] = gathered[("out", l)].reshape(D, D)
        r1, h1, h1b = fwd("out_proj_ln", l, functools.partial(_mm_res_ln, "out_proj_ln"), mixed, p["wout"], h, ln1_g[l], ln1_b[l], tm)
        p["wup"] = gathered[("up", l)]
        up3, ug, act = fwd("ffn_up_act", l, _ffn_up_act, h1b, p["wup"], p["fdw"], p["fdwb"], tm, nex, tp)
        p["wdown"] = gathered[("down", l)].reshape(dff, D)
        r2, h2, h2b = _mm_res_ln("ffn_down_ln", act, p["wdown"], h1, ln2_g[l], ln2_b[l], tm)
        saved.append(dict(hb=hb, z=z, a=a, cc=ccv, cs=cs, ya=ya, yc=yc, mixed=mixed, r1=r1, h1b=h1b, up3=up3, ug=ug, act=act, r2=r2))
        lw.append(p)
        h, hb = h2, h2b

    dy, sq = _loss_grad(h, loss_target, nblk)
    loss = lax.psum(0.5 / D * jnp.sum(sq), ("x", "y", "c"))

    grads, swapped, pair_sums, reduced = {}, {}, {}, {}
    cvec, where = core.reshape(1), jnp.stack([chip, core])
    last = [(k, DEPTH - 1) for k in names]
    bwd_plan = {("ln2_bwd", 0): ("swap", last),
                ("ffn_bwd", 0): ("exch", [("up", 1), ("down", 1)]),
                ("dw_up", 0): ("exch", [("in", 1), ("ap", 1), ("cp", 1), ("out", 1)]),
                ("ln1_bwd", 0): ("swap", [("down", 0), ("up", 0)]),
                ("conv_bwd_b", 0): ("swap", [("out", 0), ("ap", 0), ("cp", 0)]),
                ("attn_bwd", 0): ("exch", [("down", 0), ("up", 0), ("out", 0), ("ap", 0), ("cp", 0)]),
                ("in_ln_bwd", 0): ("swap", [("in", 0)])}

    def after(kind, keys, outs):
        for key, o in zip(keys, outs):
            if kind == "swap":
                swapped[key] = o
                pair_sums[key] = _add_half("grad_add_sibling", grads[key], o, cvec)
            else:
                reduced[key] = _add_chips("grad_add_chips", grads[key], swapped[key], o, where)

    def bwd(tag, l, fn, *args):
        plan = bwd_plan.get((tag, l))
        if plan is None:
            return fn(*args)
        kind, keys = plan
        job = _swap_job([grads[k] for k in keys]) if kind == "swap" else _exchange_job([pair_sums[k] for k in keys])
        res = fn(*args, side=job)
        after(kind, keys, res[-len(keys):])
        main = res[:-len(keys)]
        return main[0] if len(main) == 1 else main

    small = {}
    prev_a, prev_w, prev_cols = (), None, None
    dprev = dy
    for l in reversed(range(DEPTH)):
        p, s = lw[l], saved[l]
        dr2, dr2b, dg2, db2 = bwd("ln2_bwd", l, functools.partial(_ln_bwd_call, "ln2_bwd"), dprev, s["r2"], ln2_g[l], tm,
                                  prev_a, prev_w, prev_cols)
        dpre3, dfdw, dfdwb = bwd("ffn_bwd", l, _ffn_bwd, dr2b, p["wdown"], s["ug"], s["up3"], p["fdw"], tm, nex, tp)
        grads[("down", l)] = _mm_tn("dw_down", s["act"], dr2b, ffs, D // 2).reshape(NCHIP, dff // NCHIP, D)
        grads[("up", l)] = bwd("dw_up", l, functools.partial(_mm_tn, "dw_up"), s["h1b"], dpre3, D, ffs,
                               lambda j: (j // 2, j % 2), True)
        dr1, dr1b, dg1, db1 = bwd("ln1_bwd", l, functools.partial(_ln_bwd_call, "ln1_bwd"), dr2, s["r1"], ln1_g[l], tm // 2,
                                  (dpre3,), p["wup"])
        dya, dyc, dzg, csg = _gate_bwd(dr1b, p["wout"], s["ya"], s["yc"], s["z"], tm)
        grads[("out", l)] = _mm_tn("dw_out", s["mixed"], dr1b, D, D // 2).reshape(NCHIP, D // NCHIP, D)
        grads[("ap", l)] = _mm_tn("dw_attn_proj", s["a"], dya, AW, D // NCHIP, chip_out=True)
        grads[("cp", l)] = _mm_tn("dw_conv_proj", s["cs"], dyc, CW, D // NCHIP, chip_out=True)
        dcc, dclg, dclb, dcwb = _conv_bwd_a(dyc, p["wcp"], s["cc"], conv_ln_g[l], conv_ln_b[l], tm)
        dzc, dcdw, csc = bwd("conv_bwd_b", l, _conv_bwd_b, dcc, s["z"], p["cdw"], nex, tp)
        dzq, csq, dsk, dbias = bwd("attn_bwd", l, _attn_bwd, s["z"], bias_tab, attn_sinks[l], dya, p["wap"], nex, nblk)
        gin = [_mm_tn("dw_in_gates", s["hb"], dzg, D, D // 2), _mm_tn("dw_in_conv", s["hb"], dzc, D, CW),
               _mm_tn("dw_in_qkv", s["hb"], dzq, D, 2 * KVW)]
        grads[("in", l)] = _w_in_to_chips(gin)
        small[l] = dict(
            b_in=_to_old(jnp.concatenate([csg, csc, csq], axis=1)).reshape(IN_COLS), attn_sinks=dsk[0, :NQ],
            conv_dw=dcdw[:CTAPS], conv_dw_b=dcwb.reshape(CW), conv_ln_g=dclg.reshape(CW), conv_ln_b=dclb.reshape(CW),
            ln1_g=dg1.reshape(D), ln1_b=db1.reshape(D),
            ffn_dw=jnp.transpose(dfdw, (1, 0, 2)).reshape(FTAPS, 2 * dff), ffn_dw_b=jnp.transpose(dfdwb, (1, 0, 2)).reshape(2 * dff),
            ln2_g=dg2.reshape(D), ln2_b=db2.reshape(D), bias=dbias)
        dprev = dr1
        prev_a, prev_w, prev_cols = (dzg, dzc, dzq), p["win"], [(C_GATES, C_CONV), (C_CONV, C_QKV), (C_QKV, IN_COLS)]
    draw, _, dg0, db0 = bwd("in_ln_bwd", 0, functools.partial(_ln_bwd_call, "in_ln_bwd"), dprev, raw, in_ln_g, tm,
                            prev_a, prev_w, prev_cols)
    draw3 = draw.reshape(nex, tp, D)
    grad_x = draw3[:, BLK:, :]
    dmeta = _sum0("meta_grad_sum", draw3[:, PAD:BLK, :])

    names_l = ["b_in", "attn_sinks", "conv_dw", "conv_dw_b", "conv_ln_g", "conv_ln_b", "ln1_g", "ln1_b", "ffn_dw", "ffn_dw_b", "ln2_g", "ln2_b"]
    dbias_all = _bias_grad(jnp.concatenate([small[l]["bias"] for l in range(DEPTH)], axis=0))
    part_list = [dmeta, dg0.reshape(D), db0.reshape(D), dbias_all]
    part_list += [jnp.stack([small[0][n], small[1][n]]) for n in names_l]
    shapes_small = [tuple(a.shape) for a in part_list]
    tot = _sum0("small_grad_sum", _allgather_small(_pack(part_list)))
    (g_meta_f, g_inlg, g_inlb, g_biasp, g_bin, g_sinks, g_cdw_f, g_cdwb, g_clg, g_clb, g_l1g, g_l1b, g_fdw_f, g_fdwb,
     g_l2g, g_l2b) = _unpack(tot, shapes_small)
    g_relb = g_biasp
    csh = D // NCHIP
    g_meta_s = lax.dynamic_slice_in_dim(g_meta_f, chip * csh, csh, axis=1)
    g_cdw_s = lax.dynamic_slice_in_dim(g_cdw_f, chip * (CW // NCHIP), CW // NCHIP, axis=2)
    g_fdw_s = lax.dynamic_slice_in_dim(g_fdw_f, chip * ffs, ffs, axis=2)

    tail = [("in", 0)]
    after("exch", tail, _run_job("grad_chip_exchange", _exchange_job([pair_sums[k] for k in tail])))
    joined = _sibling_join([[reduced[(k, l)] for l in range(DEPTH)] for k in names])

    moms = [m_w_in, m_w_attn_proj, m_w_conv_proj, m_w_out, m_ffn_w_up, m_ffn_w_down]
    vels = [v_w_in, v_w_attn_proj, v_w_conv_proj, v_w_out, v_ffn_w_up, v_ffn_w_down]
    big_out = [_adamw_matrix(big[k], f, [reduced[(k, l)] for l in range(DEPTH)], mo, ve, cvec)
               for k, f, mo, ve in zip(names, joined, moms, vels)]

    sm_w = [meta_tokens, in_ln_g, in_ln_b, rel_bias, b_in, attn_sinks, conv_dw, conv_dw_b, conv_ln_g, conv_ln_b, ln1_g, ln1_b,
            ffn_dw, ffn_dw_b, ln2_g, ln2_b]
    sm_m = [m_meta_tokens, m_in_ln_g, m_in_ln_b, m_rel_bias, m_b_in, m_attn_sinks, m_conv_dw, m_conv_dw_b, m_conv_ln_g, m_conv_ln_b,
            m_ln1_g, m_ln1_b, m_ffn_dw, m_ffn_dw_b, m_ln2_g, m_ln2_b]
    sm_v = [v_meta_tokens, v_in_ln_g, v_in_ln_b, v_rel_bias, v_b_in, v_attn_sinks, v_conv_dw, v_conv_dw_b, v_conv_ln_g, v_conv_ln_b,
            v_ln1_g, v_ln1_b, v_ffn_dw, v_ffn_dw_b, v_ln2_g, v_ln2_b]
    sm_g = [g_meta_s, g_inlg, g_inlb, g_relb, g_bin, g_sinks, g_cdw_s, g_cdwb, g_clg, g_clb, g_l1g, g_l1b, g_fdw_s, g_fdwb, g_l2g, g_l2b]
    sm_shapes = [tuple(a.shape) for a in sm_w]
    sd, smn, svn = _adamw("adamw_small", _pack(sm_w), _pack(sm_g), _pack(sm_m), _pack(sm_v))
    sd, smn, svn = _unpack(sd, sm_shapes), _unpack(smn, sm_shapes), _unpack(svn, sm_shapes)

    order = ["meta_tokens", "in_ln_g", "in_ln_b", "rel_bias", "w_in", "b_in", "attn_sinks", "w_attn_proj", "conv_dw", "conv_dw_b",
             "conv_ln_g", "conv_ln_b", "w_conv_proj", "w_out", "ln1_g", "ln1_b", "ffn_w_up", "ffn_dw", "ffn_dw_b", "ffn_w_down",
             "ln2_g", "ln2_b"]
    small_names = ["meta_tokens", "in_ln_g", "in_ln_b", "rel_bias", "b_in", "attn_sinks", "conv_dw", "conv_dw_b", "conv_ln_g",
                   "conv_ln_b", "ln1_g", "ln1_b", "ffn_dw", "ffn_dw_b", "ln2_g", "ln2_b"]
    big_names = ["w_in", "w_attn_proj", "w_conv_proj", "w_out", "ffn_w_up", "ffn_w_down"]
    res = {}
    for i, n in enumerate(small_names):
        res[n] = (sm_g[i], sd[i], smn[i], svn[i])
    for i, n in enumerate(big_names):
        res[n] = big_out[i]
    outs = [loss, grad_x]
    for k in range(4):
        outs += [res[n][k] for n in order]
    return tuple(outs)
```

```python
import functools
import math
from typing import Any, Callable, NamedTuple, Sequence

import numpy as np
import jax
import jax.numpy as jnp
from jax import lax
from jax.experimental import pallas as pl
from jax.experimental.pallas import tpu as pltpu

F32 = jnp.float32
BF16 = jnp.bfloat16
MESH = pl.DeviceIdType.MESH

D = 1024
N_META = 16
BLK = 128
PAD = BLK - N_META
HD = 64
NQ = 8
NKV = 2
GRP = NQ // NKV
AW = NQ * HD
KVW = NKV * HD
CW = D // 2
CTAPS = 31
FTAPS = 3
NBUCKET = 32
MAXDIST = 128
EPS = 1e-5
DEPTH = 2
ALPHA = (2.0 * DEPTH) ** 0.25
NCHIP = 4
NKEY = 3 * BLK
NEG = -1e30
CHALO = 32
FHALO = 8
IN_COLS = AW + 2 * KVW + 2 * CW + 2 * D
_OLD = dict(q=(0, AW), k=(AW, AW + KVW), v=(AW + KVW, AW + 2 * KVW), cv=(AW + 2 * KVW, AW + 2 * KVW + CW),
            cg=(AW + 2 * KVW + CW, AW + 2 * KVW + 2 * CW), ga=(AW + 2 * KVW + 2 * CW, AW + 2 * KVW + 2 * CW + D),
            gc=(AW + 2 * KVW + 2 * CW + D, IN_COLS))
_NEW_ORDER = ("ga", "gc", "cv", "cg", "q", "k", "v")
C_GATES, C_CONV, C_QKV = 0, 2 * D, 2 * D + 2 * CW

ADAM_LR, ADAM_B1, ADAM_B2, ADAM_EPS, ADAM_WD, ADAM_STEP = 0.001, 0.9, 0.999, 1e-08, 0.01, 10


def _to_new(a):
    return jnp.concatenate([a[..., _OLD[n][0]:_OLD[n][1]] for n in _NEW_ORDER], axis=-1)


def _to_old(a):
    offs, o = {}, 0
    for n in _NEW_ORDER:
        w = _OLD[n][1] - _OLD[n][0]
        offs[n] = (o, o + w)
        o += w
    return jnp.concatenate([a[..., offs[n][0]:offs[n][1]] for n in ("q", "k", "v", "cv", "cg", "ga", "gc")], axis=-1)


def _new_starts():
    starts, o = {}, 0
    for n in _NEW_ORDER:
        starts[n] = o
        o += _OLD[n][1] - _OLD[n][0]
    return starts


def _w_in_from_chips(g4):
    cs = IN_COLS // NCHIP
    pieces = []
    for n in _NEW_ORDER:
        lo, hi = _OLD[n]
        while lo < hi:
            chip = lo // cs
            end = min(hi, (chip + 1) * cs)
            pieces.append(g4[chip][:, lo - chip * cs:end - chip * cs])
            lo = end
    return jnp.concatenate(pieces, axis=1)


def _w_in_to_chips(parts):
    cs = IN_COLS // NCHIP
    starts = _new_starts()
    bounds, o = [], 0
    for p in parts:
        bounds.append((o, o + p.shape[1], p))
        o += p.shape[1]

    def new_cols(a, b):
        out = []
        for s, e, p in bounds:
            lo, hi = max(a, s), min(b, e)
            if lo < hi:
                out.append(p[:, lo - s:hi - s])
        return out

    slabs = []
    for chip in range(NCHIP):
        pieces = []
        for n in ("q", "k", "v", "cv", "cg", "ga", "gc"):
            lo, hi = max(_OLD[n][0], chip * cs), min(_OLD[n][1], (chip + 1) * cs)
            if lo < hi:
                pieces += new_cols(starts[n] + lo - _OLD[n][0], starts[n] + hi - _OLD[n][0])
        slabs.append(jnp.concatenate(pieces, axis=1))
    return jnp.stack(slabs)


class _Job(NamedTuple):
    ins: Sequence[Any]
    outs: Sequence[Any]
    sems: tuple
    start: Callable
    finish: Callable


_ANY = pl.BlockSpec(memory_space=pl.ANY)


def _call(body, name, out_shape, grid, in_specs, out_specs, scratch=(), prefetch=0, side=None):
    params = pltpu.CompilerParams(dimension_semantics=("arbitrary",) * len(grid))
    if side is None:
        if prefetch:
            gs = pltpu.PrefetchScalarGridSpec(num_scalar_prefetch=prefetch, grid=grid, in_specs=in_specs,
                                              out_specs=out_specs, scratch_shapes=list(scratch))
            return pl.pallas_call(body, name=name, out_shape=out_shape, grid_spec=gs, compiler_params=params)
        return pl.pallas_call(body, name=name, out_shape=out_shape, grid=grid, in_specs=in_specs, out_specs=out_specs,
                              scratch_shapes=list(scratch), compiler_params=params)
    assert not prefetch
    single = not isinstance(out_shape, (tuple, list))
    main_shapes = (out_shape,) if single else tuple(out_shape)
    main_specs = (out_specs,) if single else tuple(out_specs)
    n_in, n_sin, n_out, n_sout, n_scr = len(in_specs), len(side.ins), len(main_shapes), len(side.outs), len(scratch)

    def wrapped(*refs):
        main_in, sin = refs[:n_in], refs[n_in:n_in + n_sin]
        o0 = n_in + n_sin
        main_out, sout = refs[o0:o0 + n_out], refs[o0 + n_out:o0 + n_out + n_sout]
        s0 = o0 + n_out + n_sout
        main_scr, (send, recv) = refs[s0:s0 + n_scr], refs[s0 + n_scr:]
        first = functools.reduce(lambda a, b: a & b, [pl.program_id(k) == 0 for k in range(len(grid))])
        last = functools.reduce(lambda a, b: a & b, [pl.program_id(k) == grid[k] - 1 for k in range(len(grid))])

        @pl.when(first)
        def _():
            side.start(sin, sout, send, recv)

        body(*main_in, *main_out, *main_scr)

        @pl.when(last)
        def _():
            side.finish(sin, sout, send, recv)

    call = pl.pallas_call(
        wrapped, name=name, out_shape=main_shapes + tuple(side.outs), grid=grid,
        in_specs=list(in_specs) + [_ANY] * n_sin, out_specs=main_specs + tuple([_ANY] * n_sout),
        scratch_shapes=list(scratch) + [pltpu.SemaphoreType.DMA(side.sems), pltpu.SemaphoreType.DMA(side.sems)],
        compiler_params=params)
    return lambda *args: call(*args, *side.ins)


def _row_tile(m):
    best = 32
    for t in range(32, 641, 32):
        if m % t == 0:
            best = t
    return best


def _pad_rows(rows, nex, tp):
    m = rows < PAD
    for b in range(1, nex):
        m = m | ((rows >= b * tp) & (rows < b * tp + PAD))
    return m


def _ln_stats(x):
    mu = jnp.mean(x, axis=-1, keepdims=True)
    xc = x - mu
    var = jnp.mean(xc * xc, axis=-1, keepdims=True)
    rstd = lax.rsqrt(var + EPS)
    return xc * rstd, rstd


def _ln_bwd(dy, xhat, rstd, g):
    dxh = dy * g
    m1 = jnp.mean(dxh, axis=-1, keepdims=True)
    m2 = jnp.mean(dxh * xhat, axis=-1, keepdims=True)
    return rstd * (dxh - m1 - xhat * m2)


def _dot(a, b):
    return jnp.dot(a, b, preferred_element_type=F32)


def _dot_nt(a, b):
    return lax.dot_general(a, b, (((1,), (1,)), ((), ())), preferred_element_type=F32)


def _dot_tn(a, b):
    return lax.dot_general(a, b, (((0,), (0,)), ((), ())), preferred_element_type=F32)


def _sigmoid(x):
    return 1.0 / (1.0 + jnp.exp(-x))


def _bucket_np(d):
    n = np.maximum(d, 0)
    max_exact = NBUCKET // 2
    nf = np.maximum(n, 1).astype(np.float32)
    large = max_exact + (np.log(nf / np.float32(max_exact)) / np.float32(math.log(MAXDIST / max_exact))
                         * np.float32(NBUCKET - max_exact)).astype(np.int32)
    large = np.minimum(large, NBUCKET - 1)
    return np.where(n < max_exact, n, large).astype(np.int32)


def _bias_index():
    i = np.arange(BLK)[:, None]
    j = np.arange(2 * BLK)[None, :]
    d = BLK + i - j
    band_ok = (d >= 0) & (d < BLK)
    band = _bucket_np(d)
    idx = np.full((3, BLK, NKEY), -1, np.int32)
    m = np.arange(N_META)[None, :]
    d0 = (i - PAD) - m
    idx[0, :, 2 * BLK:2 * BLK + N_META] = np.where(d0 >= 0, _bucket_np(d0), -1)
    ok1 = band_ok & (j >= BLK)
    idx[1, :, :2 * BLK] = np.where(ok1, band, -1)
    idx[1, :, 2 * BLK:2 * BLK + N_META] = _bucket_np((N_META + i) - m)
    idx[2, :, :2 * BLK] = np.where(band_ok, band, -1)
    idx[2, :, 2 * BLK:2 * BLK + N_META] = NBUCKET - 1
    return idx


def _bias_build(rel_bias):
    idx = jnp.asarray(_bias_index())

    def body(idx_ref, rb_ref, o_ref):
        ix = idx_ref[...]
        for h in range(NQ):
            acc = jnp.full(ix.shape, NEG, F32)
            for b in range(NBUCKET):
                acc = jnp.where(ix == b, rb_ref[b, h], acc)
            o_ref[:, h, :, :] = acc

    return pl.pallas_call(
        body, name="bias_build", out_shape=jax.ShapeDtypeStruct((3, NQ, BLK, NKEY), F32),
        in_specs=[pl.BlockSpec(memory_space=pltpu.VMEM), pl.BlockSpec(memory_space=pltpu.SMEM)],
        out_specs=pl.BlockSpec(memory_space=pltpu.VMEM))(idx, rel_bias)


def _bias_grad(dbias):
    idx = jnp.asarray(_bias_index())

    def body(idx_ref, d_ref, o_ref):
        d = jnp.sum(d_ref[...], axis=0)
        for b in range(NBUCKET):
            acc = jnp.zeros((NQ, NKEY), F32)
            for case in range(3):
                hit = (idx_ref[case] == b)[None, :, :]
                acc = acc + jnp.sum(jnp.where(hit, d[case], 0.0), axis=1)
            o_ref[b] = jnp.sum(acc, axis=-1, keepdims=True)

    out = pl.pallas_call(
        body, name="bias_grad", out_shape=jax.ShapeDtypeStruct((NBUCKET, NQ, 1), F32),
        in_specs=[pl.BlockSpec(memory_space=pltpu.VMEM), pl.BlockSpec(memory_space=pltpu.VMEM)],
        out_specs=pl.BlockSpec(memory_space=pltpu.VMEM))(idx, dbias)
    return out.reshape(NBUCKET, NQ)


def _embed_ln(x, meta, g, b, nblk, side=None):
    nex, seq, _ = x.shape
    m = nex * nblk * BLK

    def body(x_ref, meta_ref, g_ref, b_ref, raw_ref, h_ref, hb_ref):
        j = pl.program_id(1)

        @pl.when(j == 0)
        def _():
            raw_ref[0:PAD, :] = jnp.zeros((PAD, D), F32)
            raw_ref[PAD:BLK, :] = meta_ref[...]

        @pl.when(j > 0)
        def _():
            raw_ref[...] = x_ref[...]

        xhat, _ = _ln_stats(raw_ref[...])
        y = xhat * g_ref[...] + b_ref[...]
        h_ref[...] = y
        hb_ref[...] = y.astype(BF16)

    row = lambda bb, j: (bb * nblk + j, 0)
    return _call(
        body, "embed_ln",
        (jax.ShapeDtypeStruct((m, D), F32), jax.ShapeDtypeStruct((m, D), F32), jax.ShapeDtypeStruct((m, D), BF16)),
        (nex, nblk),
        [pl.BlockSpec((None, BLK, D), lambda bb, j: (bb, jnp.maximum(j - 1, 0), 0)),
         pl.BlockSpec((N_META, D), lambda bb, j: (0, 0)),
         pl.BlockSpec((1, D), lambda bb, j: (0, 0)), pl.BlockSpec((1, D), lambda bb, j: (0, 0))],
        (pl.BlockSpec((BLK, D), row), pl.BlockSpec((BLK, D), row), pl.BlockSpec((BLK, D), row)), side=side,
    )(x, meta, g.reshape(1, D), b.reshape(1, D))


def _mm_bias(name, a, w, bias, tn, tm, side=None):
    m, k = a.shape
    n = w.shape[1]

    def body(a_ref, w_ref, b_ref, o_ref):
        o_ref[...] = _dot(a_ref[...], w_ref[...]) + b_ref[...]

    return _call(body, name, jax.ShapeDtypeStruct((m, n), F32), (n // tn, m // tm),
                 [pl.BlockSpec((tm, k), lambda j, i: (i, 0)), pl.BlockSpec((k, tn), lambda j, i: (0, j)),
                  pl.BlockSpec((1, tn), lambda j, i: (0, j))],
                 pl.BlockSpec((tm, tn), lambda j, i: (i, j)), side=side)(a, w, bias)


def _ffn_up_act(a, w4, cw, cb, tm, nex, tp, side=None):
    m, k = a.shape
    ffs = w4.shape[2]

    def body(a_ref, wu_ref, wg_ref, cu_ref, cg_ref, bu_ref, bg_ref, up_ref, ug_ref, act_ref, win):
        i = pl.program_id(1)

        @pl.when(i == 0)
        def _():
            win[:, 0:FHALO, :] = jnp.zeros((2, FHALO, ffs), F32)

        rows = i * tm + lax.broadcasted_iota(jnp.int32, (tm, 1), 0)
        pad = _pad_rows(rows, nex, tp)
        av = a_ref[...]
        for p, w_ref in ((0, wu_ref), (1, wg_ref)):
            x = jnp.where(pad, 0.0, _dot(av, w_ref[...]))
            win[p, FHALO:FHALO + tm, :] = x
            up_ref[p] = x.astype(BF16)
        for r0 in range(0, tm, RCH):
            for c0, c1 in _lane_groups(ffs):
                u = _conv3(win, 0, r0, c0, c1, cu_ref, bu_ref)
                g = _conv3(win, 1, r0, c0, c1, cg_ref, bg_ref)
                ug_ref[0, r0:r0 + RCH, c0:c1] = u.astype(BF16)
                ug_ref[1, r0:r0 + RCH, c0:c1] = g.astype(BF16)
                act_ref[r0:r0 + RCH, c0:c1] = (g * (0.5 * (1.0 + lax.erf(g * (1.0 / math.sqrt(2.0))))) * u).astype(BF16)
        win[:, 0:FHALO, :] = win[:, tm:tm + FHALO, :]

    wide = jax.ShapeDtypeStruct((2, m, 2 * ffs), BF16)
    return _call(body, "ffn_up_act", (wide, wide, jax.ShapeDtypeStruct((m, 2 * ffs), BF16)),
                 (2, m // tm),
                 [pl.BlockSpec((tm, k), lambda c, i: (i, 0)),
                  pl.BlockSpec((None, k, ffs), lambda c, i: (c, 0, 0)), pl.BlockSpec((None, k, ffs), lambda c, i: (c + 2, 0, 0)),
                  pl.BlockSpec((FTAPS, ffs), lambda c, i: (0, c)), pl.BlockSpec((FTAPS, ffs), lambda c, i: (0, c + 2)),
                  pl.BlockSpec((1, ffs), lambda c, i: (0, c)), pl.BlockSpec((1, ffs), lambda c, i: (0, c + 2))],
                 (pl.BlockSpec((2, tm, ffs), lambda c, i: (0, i, c)), pl.BlockSpec((2, tm, ffs), lambda c, i: (0, i, c)),
                  pl.BlockSpec((tm, ffs), lambda c, i: (i, c))),
                 scratch=[pltpu.VMEM((2, FHALO + tm, ffs), F32)], side=side)(a, w4, w4, cw, cw, cb, cb)


def _fill_kv(ks, vs, e, prev_ref, cur_ref, meta_ref):
    for piece, lo, n in ((prev_ref, 0, BLK), (cur_ref, BLK, BLK), (meta_ref, 2 * BLK, N_META)):
        val = piece[e]
        for hk in range(NKV):
            ks[e, hk, lo:lo + n, :] = val[:, hk * HD:(hk + 1) * HD].astype(BF16)
            vs[e, hk, lo:lo + n, :] = val[:, KVW + hk * HD:KVW + (hk + 1) * HD].astype(BF16)
    for hk in range(NKV):
        ks[e, hk, 2 * BLK + N_META:NKEY, :] = jnp.zeros((BLK - N_META, HD), BF16)
        vs[e, hk, 2 * BLK + N_META:NKEY, :] = jnp.zeros((BLK - N_META, HD), BF16)


def _interleave(chains):
    live = list(chains)
    while live:
        for c in list(live):
            try:
                next(c)
            except StopIteration:
                live.remove(c)


def _head_softmax(q, ks_hk, bias_ref, sink_ref, h, out):
    qh = q[:, h * HD:(h + 1) * HD].astype(BF16)
    yield
    s = _dot_nt(qh, ks_hk) * (HD ** -0.5) + bias_ref[h]
    yield
    sink = sink_ref[0, h]
    mx = jnp.maximum(jnp.max(s, axis=-1, keepdims=True), sink)
    yield
    p = jnp.exp(s - mx)
    es = jnp.exp(sink - mx)
    yield
    inv = 1.0 / (jnp.sum(p, axis=-1, keepdims=True) + es)
    yield
    out["q"], out["p"], out["sink"] = qh, p * inv, es * inv
    yield


def _attn_specs(nex, blk_of):
    qcol, kvcol = (C_QKV) // AW, (C_QKV + AW) // (2 * KVW)
    return [
        pl.BlockSpec((nex, BLK, AW), lambda j: (0, blk_of(j), qcol)),
        pl.BlockSpec((nex, BLK, 2 * KVW), lambda j: (0, blk_of(j), kvcol)),
        pl.BlockSpec((nex, BLK, 2 * KVW), lambda j: (0, jnp.maximum(blk_of(j) - 1, 0), kvcol)),
        pl.BlockSpec((nex, N_META, 2 * KVW), lambda j: (0, PAD // N_META, kvcol)),
        pl.BlockSpec((None, NQ, BLK, NKEY), lambda j: (jnp.minimum(blk_of(j), 2), 0, 0, 0)),
        pl.BlockSpec(memory_space=pltpu.SMEM),
    ]


def _attn_fwd(z, bias, sinks, nex, nblk, side=None):
    m = z.shape[0]
    z3 = z.reshape(nex, nblk * BLK, z.shape[1])

    def head(e, h, q, ks, vs, bias_ref, sink_ref, oacc):
        out = {}
        yield from _head_softmax(q, ks[e, h // GRP], bias_ref, sink_ref, h, out)
        o = _dot(out["p"].astype(BF16), vs[e, h // GRP])
        yield
        oacc[e, :, h * HD:(h + 1) * HD] = o
        yield

    def body(q_ref, cur_ref, prev_ref, meta_ref, bias_ref, sink_ref, o_ref, ks, vs, oacc):
        for e in range(nex):
            _fill_kv(ks, vs, e, prev_ref, cur_ref, meta_ref)
        _interleave([head(e, h, q_ref[e], ks, vs, bias_ref, sink_ref, oacc) for e in range(nex) for h in range(NQ)])
        for e in range(nex):
            o_ref[e] = oacc[e].astype(BF16)

    res = _call(body, "attn_fwd", jax.ShapeDtypeStruct((nex, nblk * BLK, AW), BF16), (nblk,),
                _attn_specs(nex, lambda j: j),
                pl.BlockSpec((nex, BLK, AW), lambda j: (0, j, 0)),
                scratch=[pltpu.VMEM((nex, NKV, NKEY, HD), BF16), pltpu.VMEM((nex, NKV, NKEY, HD), BF16),
                         pltpu.VMEM((nex, BLK, AW), F32)], side=side)(z3, z3, z3, z3, bias, sinks.reshape(1, NQ))
    if side is None:
        return res.reshape(m, AW)
    return (res[0].reshape(m, AW),) + tuple(res[1:])


def _cgate(cv, cg, rows, nex, tp):
    return jnp.where(_pad_rows(rows, nex, tp), 0.0, cv * _sigmoid(cg))


CLANES = 256


def _rolled_up(blk, b):
    return blk if b == 0 else pltpu.roll(blk, blk.shape[0] - b, axis=0)


def _conv_fwd(z, w, wb, g, b, nex, tp, side=None):
    m = z.shape[0]
    tm = BLK
    sub = tm // CHALO
    cvc, cgc = C_CONV // CW, C_CONV // CW + 1

    def body(cv_ref, cg_ref, cvh_ref, cgh_ref, w_ref, wb_ref, g_ref, b_ref, cc_ref, cs_ref, win):
        i = pl.program_id(0)
        rows = i * tm + lax.broadcasted_iota(jnp.int32, (tm, 1), 0)
        hrows = i * tm - CHALO + lax.broadcasted_iota(jnp.int32, (CHALO, 1), 0)
        win[0:CHALO, :] = _cgate(cvh_ref[...], cgh_ref[...], hrows, nex, tp)
        win[CHALO:CHALO + tm, :] = _cgate(cv_ref[...], cg_ref[...], rows, nex, tp)
        for sb in range(sub):
            lo = sb * CHALO
            for c0 in range(0, CW, CLANES):
                blk = win[lo:lo + 2 * CHALO, c0:c0 + CLANES]
                acc = jnp.zeros((CHALO, CLANES), F32) + wb_ref[:, c0:c0 + CLANES]
                for b in range(8):
                    rb = _rolled_up(blk, b)
                    for a in range(5):
                        s = 8 * a + b
                        if 2 <= s <= CTAPS + 1:
                            acc = acc + w_ref[s - 2:s - 1, c0:c0 + CLANES] * rb[8 * a:8 * a + CHALO]
                cc_ref[lo:lo + CHALO, c0:c0 + CLANES] = acc
        xhat, _ = _ln_stats(cc_ref[...])
        cl = xhat * g_ref[...] + b_ref[...]
        cs_ref[...] = (cl * _sigmoid(cl)).astype(BF16)

    halo = lambda i: jnp.maximum(i * sub - 1, 0)
    vec = pl.BlockSpec((1, CW), lambda i: (0, 0))
    return _call(body, "conv_fwd", (jax.ShapeDtypeStruct((m, CW), F32), jax.ShapeDtypeStruct((m, CW), BF16)),
                 (m // tm,),
                 [pl.BlockSpec((tm, CW), lambda i: (i, cvc)), pl.BlockSpec((tm, CW), lambda i: (i, cgc)),
                  pl.BlockSpec((CHALO, CW), lambda i: (halo(i), cvc)), pl.BlockSpec((CHALO, CW), lambda i: (halo(i), cgc)),
                  pl.BlockSpec((CTAPS, CW), lambda i: (0, 0)), vec, vec, vec],
                 (pl.BlockSpec((tm, CW), lambda i: (i, 0)), pl.BlockSpec((tm, CW), lambda i: (i, 0))),
                 scratch=[pltpu.VMEM((CHALO + tm, CW), F32)], side=side)(z, z, z, z, w, wb.reshape(1, CW), g.reshape(1, CW), b.reshape(1, CW))


def _mix_fwd(a, cs, wap4, wcp4, z, tm, side=None):
    m = a.shape[0]
    ns = wap4.shape[2]

    def body(a_ref, cs_ref, wa_ref, wc_ref, ga_ref, gc_ref, ya_ref, yc_ref, mix_ref):
        av, cv = a_ref[...], cs_ref[...]
        for j in range(NCHIP):
            cols = slice(j * ns, (j + 1) * ns)
            ya, yc = _dot(av, wa_ref[j]), _dot(cv, wc_ref[j])
            ya_ref[:, cols] = ya.astype(BF16)
            yc_ref[:, cols] = yc.astype(BF16)
            mix_ref[:, cols] = (_sigmoid(ga_ref[:, cols]) * ya + _sigmoid(gc_ref[:, cols]) * yc).astype(BF16)

    wspec = pl.BlockSpec((NCHIP, AW, ns), lambda i: (0, 0, 0))
    row = lambda i: (i, 0)
    return _call(body, "mix_fwd",
                 (jax.ShapeDtypeStruct((m, D), BF16), jax.ShapeDtypeStruct((m, D), BF16), jax.ShapeDtypeStruct((m, D), BF16)),
                 (m // tm,),
                 [pl.BlockSpec((tm, AW), row), pl.BlockSpec((tm, CW), row), wspec, wspec,
                  pl.BlockSpec((tm, D), lambda i: (i, 0)), pl.BlockSpec((tm, D), lambda i: (i, 1))],
                 (pl.BlockSpec((tm, D), row), pl.BlockSpec((tm, D), row), pl.BlockSpec((tm, D), row)), side=side)(a, cs, wap4, wcp4, z, z)


def _mm_res_ln(name, a, w, res, g, b, tm, side=None):
    m, k = a.shape

    def body(a_ref, w_ref, res_ref, g_ref, b_ref, r_ref, h_ref, hb_ref):
        r = ALPHA * res_ref[...] + _dot(a_ref[...], w_ref[...])
        r_ref[...] = r
        xhat, _ = _ln_stats(r)
        y = xhat * g_ref[...] + b_ref[...]
        h_ref[...] = y
        hb_ref[...] = y.astype(BF16)

    row = lambda i: (i, 0)
    vec = pl.BlockSpec((1, D), lambda i: (0, 0))
    return _call(body, name,
                 (jax.ShapeDtypeStruct((m, D), F32), jax.ShapeDtypeStruct((m, D), F32), jax.ShapeDtypeStruct((m, D), BF16)),
                 (m // tm,),
                 [pl.BlockSpec((tm, k), row), pl.BlockSpec((k, D), lambda i: (0, 0)), pl.BlockSpec((tm, D), row), vec, vec],
                 (pl.BlockSpec((tm, D), row), pl.BlockSpec((tm, D), row), pl.BlockSpec((tm, D), row)), side=side)(a, w, res, g.reshape(1, D), b.reshape(1, D))


RCH = 16


def _lane_groups(width, most=768):
    n = -(-width // most)
    step = -(-width // (128 * n)) * 128
    return [(c, min(c + step, width)) for c in range(0, width, step)]


def _conv3(win, p, r0, c0, c1, w_ref, b_ref):
    blk = win[p, r0:r0 + FHALO + RCH, c0:c1]
    x0, x1, x2 = blk[FHALO:], pltpu.roll(blk, 1, axis=0)[FHALO:], pltpu.roll(blk, 2, axis=0)[FHALO:]
    return b_ref[:, c0:c1] + w_ref[0:1, c0:c1] * x2 + w_ref[1:2, c0:c1] * x1 + w_ref[2:3, c0:c1] * x0


def _loss_grad(y, target, nblk):
    nex = target.shape[0]
    m = y.shape[0]

    def body(y_ref, t_ref, dy_ref, acc_ref):
        bb, j = pl.program_id(0), pl.program_id(1)

        @pl.when((bb == 0) & (j == 0))
        def _():
            acc_ref[...] = jnp.zeros_like(acc_ref)

        @pl.when(j == 0)
        def _():
            dy_ref[...] = jnp.zeros_like(dy_ref)

        @pl.when(j > 0)
        def _():
            e = y_ref[...] - t_ref[...]
            dy_ref[...] = e * (1.0 / D)
            acc_ref[...] += jnp.sum((e * e).reshape(BLK // 8, 8, D), axis=0)

    return _call(body, "loss_grad", (jax.ShapeDtypeStruct((m, D), F32), jax.ShapeDtypeStruct((8, D), F32)), (nex, nblk),
                 [pl.BlockSpec((BLK, D), lambda bb, j: (bb * nblk + j, 0)),
                  pl.BlockSpec((None, BLK, D), lambda bb, j: (bb, jnp.maximum(j - 1, 0), 0))],
                 (pl.BlockSpec((BLK, D), lambda bb, j: (bb * nblk + j, 0)), pl.BlockSpec((8, D), lambda bb, j: (0, 0))))(y, target)


def _ln_bwd_call(name, dy, r, g, tm, a_list=(), w=None, cols=None, side=None):
    m = dy.shape[0]
    na = len(a_list)

    def body(*refs):
        dy_ref, r_ref, g_ref = refs[0:3]
        a_refs = refs[3:3 + na]
        w_ref = refs[3 + na] if na else None
        dr_ref, drb_ref, dg_ref, db_ref = refs[-4:]
        i = pl.program_id(0)
        dh = dy_ref[...]
        if na:
            dh = ALPHA * dh
            if cols is None:
                ns = w.shape[2]
                for j in range(NCHIP):
                    dh = dh + _dot_nt(a_refs[0][j // 2, :, (j % 2) * ns:(j % 2 + 1) * ns], w_ref[j])
            else:
                for a_ref, (c0, c1) in zip(a_refs, cols):
                    dh = dh + _dot_nt(a_ref[...], w_ref[:, c0:c1])
        xhat, rstd = _ln_stats(r_ref[...])
        dr = _ln_bwd(dh, xhat, rstd, g_ref[...])
        dr_ref[...] = dr
        drb_ref[...] = dr.astype(BF16)

        @pl.when(i == 0)
        def _():
            dg_ref[...] = jnp.zeros_like(dg_ref)
            db_ref[...] = jnp.zeros_like(db_ref)

        dg_ref[...] += jnp.sum(dh * xhat, axis=0, keepdims=True)
        db_ref[...] += jnp.sum(dh, axis=0, keepdims=True)

    row = lambda i: (i, 0)
    vec = pl.BlockSpec((1, D), lambda i: (0, 0))
    in_specs = [pl.BlockSpec((tm, D), row), pl.BlockSpec((tm, D), row), vec]
    for a in a_list:
        in_specs.append(pl.BlockSpec((2, tm, a.shape[2]), lambda i: (0, i, 0)) if a.ndim == 3 else pl.BlockSpec((tm, a.shape[1]), row))
    if na:
        in_specs.append(pl.BlockSpec(w.shape, (lambda i: (0, 0, 0)) if w.ndim == 3 else (lambda i: (0, 0))))
    return _call(body, name,
                 (jax.ShapeDtypeStruct((m, D), F32), jax.ShapeDtypeStruct((m, D), BF16),
                  jax.ShapeDtypeStruct((1, D), F32), jax.ShapeDtypeStruct((1, D), F32)),
                 (m // tm,), in_specs,
                 (pl.BlockSpec((tm, D), row), pl.BlockSpec((tm, D), row), vec, vec),
                 side=side)(dy, r, g.reshape(1, D), *a_list, *([w] if na else []))


def _ffn_bwd(drb, wdown, ug, up3, w, tm, nex, tp, side=None):
    _, m, dff = ug.shape
    ffs = dff // 2
    nt = m // tm

    def body(dr_ref, wd_ref, ug_ref, x_ref, wu_ref, wg_ref, o_ref, dw_ref, db_ref, dact, carry, dwacc, dbacc):
        i = pl.program_id(1)
        tile = nt - 1 - i
        dact[...] = _dot_nt(dr_ref[...], wd_ref[...])
        dwacc[...] = jnp.zeros_like(dwacc)
        dbacc[...] = jnp.zeros_like(dbacc)

        @pl.when(i == 0)
        def _():
            carry[...] = jnp.zeros_like(carry)
            dw_ref[...] = jnp.zeros_like(dw_ref)
            db_ref[...] = jnp.zeros_like(db_ref)

        fold = lambda t: t[0:8, :] + t[8:16, :]
        for r0 in reversed(range(0, tm, RCH)):
            rows = tile * tm + r0 + lax.broadcasted_iota(jnp.int32, (RCH, 1), 0)
            pad = _pad_rows(rows, nex, tp)
            for c0, c1 in _lane_groups(ffs, 384):
                u = ug_ref[0, r0:r0 + RCH, c0:c1].astype(F32)
                g = ug_ref[1, r0:r0 + RCH, c0:c1].astype(F32)
                da = dact[r0:r0 + RCH, c0:c1]
                cdf = 0.5 * (1.0 + lax.erf(g * (1.0 / math.sqrt(2.0))))
                pdf = jnp.exp(-0.5 * g * g) * (1.0 / math.sqrt(2.0 * math.pi))
                for p, w_ref, d0 in ((0, wu_ref, da * (g * cdf)), (1, wg_ref, da * u * (cdf + g * pdf))):
                    dblk = jnp.concatenate([d0, carry[p, :, c0:c1]], axis=0)
                    d1 = pltpu.roll(dblk, RCH + FHALO - 1, axis=0)[:RCH]
                    d2 = pltpu.roll(dblk, RCH + FHALO - 2, axis=0)[:RCH]
                    carry[p, :, c0:c1] = d0[0:FHALO]
                    dpre = w_ref[2:3, c0:c1] * d0 + w_ref[1:2, c0:c1] * d1 + w_ref[0:1, c0:c1] * d2
                    o_ref[p, r0:r0 + RCH, c0:c1] = jnp.where(pad, 0.0, dpre).astype(BF16)
                    x0 = x_ref[p, r0:r0 + RCH, c0:c1].astype(F32)
                    dwacc[p, 2, :, c0:c1] += fold(d0 * x0)
                    dwacc[p, 1, :, c0:c1] += fold(d1 * x0)
                    dwacc[p, 0, :, c0:c1] += fold(d2 * x0)
                    dbacc[p, :, c0:c1] += fold(d0)
        for p in range(2):
            for k in range(FTAPS):
                dw_ref[p, k:k + 1, :] += jnp.sum(dwacc[p, k], axis=0, keepdims=True)
            db_ref[p] += jnp.sum(dbacc[p], axis=0, keepdims=True)

    wide = pl.BlockSpec((2, tm, ffs), lambda c, i: (0, nt - 1 - i, c))
    return _call(body, "ffn_bwd",
                 (jax.ShapeDtypeStruct((2, m, dff), BF16), jax.ShapeDtypeStruct((2, FTAPS, dff), F32),
                  jax.ShapeDtypeStruct((2, 1, dff), F32)),
                 (2, nt),
                 [pl.BlockSpec((tm, D), lambda c, i: (nt - 1 - i, 0)), pl.BlockSpec((ffs, D), lambda c, i: (c, 0)), wide, wide,
                  pl.BlockSpec((FTAPS, ffs), lambda c, i: (0, c)), pl.BlockSpec((FTAPS, ffs), lambda c, i: (0, c + 2))],
                 (wide, pl.BlockSpec((2, FTAPS, ffs), lambda c, i: (0, 0, c)), pl.BlockSpec((2, 1, ffs), lambda c, i: (0, 0, c))),
                 scratch=[pltpu.VMEM((tm, ffs), F32), pltpu.VMEM((2, FHALO, ffs), F32),
                          pltpu.VMEM((2, FTAPS, 8, ffs), F32), pltpu.VMEM((2, 8, ffs), F32)], side=side)(drb, wdown, ug, up3, w, w)


def _mm_tn(name, a, b, tk, tn, b_cols=None, chip_out=False, side=None):
    m, k = a.shape
    n = b.shape[-1] * (2 if b.ndim == 3 else 1)

    def body(a_ref, b_ref, o_ref):
        o_ref[...] = _dot_tn(a_ref[...], b_ref[...])

    if b.ndim == 3:
        bspec = pl.BlockSpec((None, m, tn), lambda kk, j: (b_cols(j)[0], 0, b_cols(j)[1]))
    else:
        bspec = pl.BlockSpec((m, tn), lambda kk, j: (0, j))
    if chip_out:
        oshape, ospec = (n // tn, k, tn), pl.BlockSpec((None, tk, tn), lambda kk, j: (j, kk, 0))
    else:
        oshape, ospec = (k, n), pl.BlockSpec((tk, tn), lambda kk, j: (kk, j))
    return _call(body, name, jax.ShapeDtypeStruct(oshape, F32), (k // tk, n // tn),
                 [pl.BlockSpec((m, tk), lambda kk, j: (0, kk)), bspec], ospec, side=side)(a, b)


def _gate_bwd(drb, wout, ya, yc, z, tm):
    m = drb.shape[0]

    def body(dr_ref, w_ref, ya_ref, yc_ref, ga_ref, gc_ref, dya_ref, dyc_ref, dz_ref, cs_ref):
        i = pl.program_id(0)
        dmix = _dot_nt(dr_ref[...], w_ref[...])
        sa, sc = _sigmoid(ga_ref[...]), _sigmoid(gc_ref[...])
        dya_ref[...] = (dmix * sa).astype(BF16)
        dyc_ref[...] = (dmix * sc).astype(BF16)
        dga = dmix * ya_ref[...].astype(F32) * sa * (1.0 - sa)
        dgc = dmix * yc_ref[...].astype(F32) * sc * (1.0 - sc)
        dz_ref[:, 0:D] = dga.astype(BF16)
        dz_ref[:, D:2 * D] = dgc.astype(BF16)

        @pl.when(i == 0)
        def _():
            cs_ref[...] = jnp.zeros_like(cs_ref)

        cs_ref[:, 0:D] += jnp.sum(dga, axis=0, keepdims=True)
        cs_ref[:, D:2 * D] += jnp.sum(dgc, axis=0, keepdims=True)

    row = lambda i: (i, 0)
    return _call(body, "gate_bwd",
                 (jax.ShapeDtypeStruct((m, D), BF16), jax.ShapeDtypeStruct((m, D), BF16),
                  jax.ShapeDtypeStruct((m, 2 * D), BF16), jax.ShapeDtypeStruct((1, 2 * D), F32)),
                 (m // tm,),
                 [pl.BlockSpec((tm, D), row), pl.BlockSpec((D, D), lambda i: (0, 0)), pl.BlockSpec((tm, D), row),
                  pl.BlockSpec((tm, D), row), pl.BlockSpec((tm, D), lambda i: (i, 0)), pl.BlockSpec((tm, D), lambda i: (i, 1))],
                 (pl.BlockSpec((tm, D), row), pl.BlockSpec((tm, D), row), pl.BlockSpec((tm, 2 * D), row),
                  pl.BlockSpec((1, 2 * D), lambda i: (0, 0))))(drb, wout, ya, yc, z, z)


def _conv_bwd_a(dyc, wcp4, cc, g, b, tm):
    m = cc.shape[0]
    ns = wcp4.shape[2]

    def body(dy_ref, w_ref, cc_ref, g_ref, b_ref, dcc_ref, dg_ref, db_ref, dwb_ref):
        i = pl.program_id(0)
        dcs = jnp.zeros((tm, CW), F32)
        for j in range(NCHIP):
            dcs = dcs + _dot_nt(dy_ref[:, j * ns:(j + 1) * ns], w_ref[j])
        xhat, rstd = _ln_stats(cc_ref[...])
        cl = xhat * g_ref[...] + b_ref[...]
        sg = _sigmoid(cl)
        dcl = dcs * sg * (1.0 + cl * (1.0 - sg))
        dcc = _ln_bwd(dcl, xhat, rstd, g_ref[...])
        dcc_ref[...] = dcc

        @pl.when(i == 0)
        def _():
            dg_ref[...] = jnp.zeros_like(dg_ref)
            db_ref[...] = jnp.zeros_like(db_ref)
            dwb_ref[...] = jnp.zeros_like(dwb_ref)

        dg_ref[...] += jnp.sum(dcl * xhat, axis=0, keepdims=True)
        db_ref[...] += jnp.sum(dcl, axis=0, keepdims=True)
        dwb_ref[...] += jnp.sum(dcc, axis=0, keepdims=True)

    row = lambda i: (i, 0)
    vec = pl.BlockSpec((1, CW), lambda i: (0, 0))
    v = jax.ShapeDtypeStruct((1, CW), F32)
    return _call(body, "conv_bwd_a", (jax.ShapeDtypeStruct((m, CW), F32), v, v, v), (m // tm,),
                 [pl.BlockSpec((tm, D), row), pl.BlockSpec((NCHIP, CW, ns), lambda i: (0, 0, 0)), pl.BlockSpec((tm, CW), row), vec, vec],
                 (pl.BlockSpec((tm, CW), row), vec, vec, vec))(dyc, wcp4, cc, g.reshape(1, CW), b.reshape(1, CW))


def _conv_bwd_b(dcc, z, w, nex, tp, side=None):
    m = dcc.shape[0]
    tm = BLK
    sub = tm // CHALO
    nt = m // tm
    cvc, cgc = C_CONV // CW, C_CONV // CW + 1

    def body(d_ref, dh_ref, cv_ref, cg_ref, w_ref, dz_ref, dw_ref, cs_ref, dwin, dwacc):
        i = pl.program_id(0)
        rows = i * tm + lax.broadcasted_iota(jnp.int32, (tm, 1), 0)
        dwin[0:tm, :] = d_ref[...]
        dwin[tm:tm + CHALO, :] = jnp.where(i == nt - 1, 0.0, dh_ref[...])

        @pl.when(i == 0)
        def _():
            dwacc[...] = jnp.zeros_like(dwacc)
            cs_ref[...] = jnp.zeros_like(cs_ref)

        fold = lambda t: (t[0:8] + t[8:16]) + (t[16:24] + t[24:32])
        for sb in range(sub):
            lo = sb * CHALO
            pad = _pad_rows(rows[lo:lo + CHALO], nex, tp)
            for c0 in range(0, CW, CLANES):
                cs_ = slice(c0, c0 + CLANES)
                cv = cv_ref[lo:lo + CHALO, cs_]
                sg = _sigmoid(cg_ref[lo:lo + CHALO, cs_])
                cgin = jnp.where(pad, 0.0, cv * sg)
                blk = dwin[lo:lo + 2 * CHALO, cs_]
                acc = jnp.zeros((CHALO, CLANES), F32)
                for b in range(8):
                    rb = _rolled_up(blk, b)
                    for a in range(4):
                        s = 8 * a + b
                        if s <= CTAPS - 1:
                            k = CTAPS - 1 - s
                            sh = rb[8 * a:8 * a + CHALO]
                            acc = acc + w_ref[k:k + 1, cs_] * sh
                            dwacc[k, :, cs_] += fold(sh * cgin)
                dcg = jnp.where(pad, 0.0, acc)
                dcv = dcg * sg
                dgt = dcg * cv * sg * (1.0 - sg)
                dz_ref[lo:lo + CHALO, cs_] = dcv.astype(BF16)
                dz_ref[lo:lo + CHALO, CW + c0:CW + c0 + CLANES] = dgt.astype(BF16)
                cs_ref[:, cs_] += jnp.sum(dcv, axis=0, keepdims=True)
                cs_ref[:, CW + c0:CW + c0 + CLANES] += jnp.sum(dgt, axis=0, keepdims=True)

        @pl.when(i == nt - 1)
        def _():
            for k in range(CHALO):
                dw_ref[k:k + 1, :] = jnp.sum(dwacc[k], axis=0, keepdims=True)

    nxt = lambda i: jnp.minimum((i + 1) * sub, m // CHALO - 1)
    return _call(body, "conv_bwd_b",
                 (jax.ShapeDtypeStruct((m, 2 * CW), BF16), jax.ShapeDtypeStruct((CHALO, CW), F32),
                  jax.ShapeDtypeStruct((1, 2 * CW), F32)),
                 (nt,),
                 [pl.BlockSpec((tm, CW), lambda i: (i, 0)), pl.BlockSpec((CHALO, CW), lambda i: (nxt(i), 0)),
                  pl.BlockSpec((tm, CW), lambda i: (i, cvc)), pl.BlockSpec((tm, CW), lambda i: (i, cgc)),
                  pl.BlockSpec((CTAPS, CW), lambda i: (0, 0))],
                 (pl.BlockSpec((tm, 2 * CW), lambda i: (i, 0)), pl.BlockSpec((CHALO, CW), lambda i: (0, 0)),
                  pl.BlockSpec((1, 2 * CW), lambda i: (0, 0))),
                 scratch=[pltpu.VMEM((tm + CHALO, CW), F32), pltpu.VMEM((CHALO, 8, CW), F32)], side=side)(dcc, dcc, z, z, w)


def _attn_bwd(z, bias, sinks, dya, wap4, nex, nblk, side=None):
    m = z.shape[0]
    tp = nblk * BLK
    ns = wap4.shape[2]
    blk_of = lambda j: nblk - 1 - j
    z3 = z.reshape(nex, tp, z.shape[1])
    dya3 = dya.reshape(nex, tp, D)

    def body(q_ref, cur_ref, prev_ref, meta_ref, bias_ref, sink_ref, dy_ref, w_ref,
             dz_ref, cs_ref, dsk_ref, dbias_ref, ks, vs, carry, macc, dqacc, dkv, okv):
        j = pl.program_id(0)
        n = nblk - 1 - j

        @pl.when(j == 0)
        def _():
            carry[...] = jnp.zeros_like(carry)
            macc[...] = jnp.zeros_like(macc)
            cs_ref[...] = jnp.zeros_like(cs_ref)
            dsk_ref[...] = jnp.zeros_like(dsk_ref)

        @pl.when((j == 0) | (n <= 1))
        def _():
            dbias_ref[...] = jnp.zeros_like(dbias_ref)

        lane = lax.broadcasted_iota(jnp.int32, (1, BLK), 1)
        dsk = jnp.zeros((1, BLK), F32)

        def head(e, h, q, da, out):
            hk = h // GRP
            yield from _head_softmax(q, ks[e, hk], bias_ref, sink_ref, h, out)
            pn = out["p"]
            doh = da[:, h * HD:(h + 1) * HD].astype(BF16)
            yield
            dp = _dot_nt(doh, vs[e, hk])
            yield
            dl = jnp.sum(pn * dp, axis=-1, keepdims=True)
            yield
            ds = pn * (dp - dl)
            yield
            dbias_ref[h] += ds
            out["dsink"] = jnp.sum(-out["sink"] * dl)
            yield
            dsb = (ds * (HD ** -0.5)).astype(BF16)
            yield
            dqacc[e, :, h * HD:(h + 1) * HD] = _dot(dsb, ks[e, hk])
            out["ds"], out["pb"], out["do"] = dsb, pn.astype(BF16), doh
            yield

        def kv_head(e, hk, outs):
            rows = lambda key: jnp.concatenate([outs[hk * GRP + g][key] for g in range(GRP)], axis=0)
            dk = _dot_tn(rows("ds"), rows("q"))
            yield
            dv = _dot_tn(rows("pb"), rows("do"))
            yield
            dkv[e, :, hk * HD:(hk + 1) * HD] = dk
            dkv[e, :, KVW + hk * HD:KVW + (hk + 1) * HD] = dv
            yield

        das, outs = [], [[{} for _ in range(NQ)] for _ in range(nex)]
        for e in range(nex):
            _fill_kv(ks, vs, e, prev_ref, cur_ref, meta_ref)
            da = jnp.zeros((BLK, AW), F32)
            for jj in range(NCHIP):
                da = da + _dot_nt(dy_ref[e, :, jj * ns:(jj + 1) * ns], w_ref[jj])
            das.append(da)
        _interleave([head(e, h, q_ref[e], das[e], outs[e][h]) for e in range(nex) for h in range(NQ)])
        _interleave([kv_head(e, hk, outs[e]) for e in range(nex) for hk in range(NKV)])
        for e in range(nex):
            for h in range(NQ):
                dsk = dsk + jnp.where(lane == h, outs[e][h]["dsink"], 0.0)
            macc[e] += dkv[e, 2 * BLK:2 * BLK + N_META, :]
            okv[e] = dkv[e, BLK:2 * BLK, :] + carry[e]
            carry[e] = dkv[e, 0:BLK, :]

            @pl.when(n == 0)
            def _():
                okv[e, PAD:BLK, :] += macc[e]

            dq = dqacc[e]
            ok = okv[e]
            dz_ref[e, :, 0:AW] = dq.astype(BF16)
            dz_ref[e, :, AW:AW + 2 * KVW] = ok.astype(BF16)
            cs_ref[:, 0:AW] += jnp.sum(dq, axis=0, keepdims=True)
            cs_ref[:, AW:AW + 2 * KVW] += jnp.sum(ok, axis=0, keepdims=True)
        dsk_ref[...] += dsk

    wz = AW + 2 * KVW
    specs = _attn_specs(nex, blk_of) + [
        pl.BlockSpec((nex, BLK, D), lambda j: (0, blk_of(j), 0)),
        pl.BlockSpec((NCHIP, AW, ns), lambda j: (0, 0, 0))]
    res = _call(body, "attn_bwd",
                (jax.ShapeDtypeStruct((nex, tp, wz), BF16), jax.ShapeDtypeStruct((1, wz), F32), jax.ShapeDtypeStruct((1, BLK), F32),
                 jax.ShapeDtypeStruct((1, 3, NQ, BLK, NKEY), F32)),
                (nblk,), specs,
                (pl.BlockSpec((nex, BLK, wz), lambda j: (0, blk_of(j), 0)), pl.BlockSpec((1, wz), lambda j: (0, 0)),
                 pl.BlockSpec((1, BLK), lambda j: (0, 0)),
                 pl.BlockSpec((None, None, NQ, BLK, NKEY), lambda j: (0, jnp.minimum(blk_of(j), 2), 0, 0, 0))),
                scratch=[pltpu.VMEM((nex, NKV, NKEY, HD), BF16), pltpu.VMEM((nex, NKV, NKEY, HD), BF16),
                         pltpu.VMEM((nex, BLK, 2 * KVW), F32), pltpu.VMEM((nex, N_META, 2 * KVW), F32),
                         pltpu.VMEM((nex, BLK, AW), F32), pltpu.VMEM((nex, NKEY, 2 * KVW), F32),
                         pltpu.VMEM((nex, BLK, 2 * KVW), F32)],
                side=side)(z3, z3, z3, z3, bias, sinks.reshape(1, NQ), dya3, wap4)
    return (res[0].reshape(m, wz),) + tuple(res[1:])


def _tile_rows(rows, cols, target_bytes=1 << 20):
    best = None
    for t in range(8, rows + 1, 8):
        if rows % t == 0 and t * cols * 4 <= target_bytes:
            best = t
    return best or rows


def _sum0(name, x):
    n, r, c = x.shape
    tr = _tile_rows(r, c * n)

    def body(x_ref, o_ref):
        acc = x_ref[0]
        for k in range(1, n):
            acc = acc + x_ref[k]
        o_ref[...] = acc

    return _call(body, name, jax.ShapeDtypeStruct((r, c), F32), (r // tr,),
                 [pl.BlockSpec((n, tr, c), lambda i: (0, i, 0))], pl.BlockSpec((tr, c), lambda i: (i, 0)))(x)


def _adamw(name, w, g, mom, vel):
    r, c = w.shape
    tr = _tile_rows(r, c)
    c1 = 1.0 / (1.0 - ADAM_B1 ** ADAM_STEP)
    c2 = 1.0 / (1.0 - ADAM_B2 ** ADAM_STEP)

    def body(w_ref, g_ref, m_ref, v_ref, d_ref, mo_ref, vo_ref):
        gg = g_ref[...]
        mn = ADAM_B1 * m_ref[...] + (1.0 - ADAM_B1) * gg
        vn = ADAM_B2 * v_ref[...] + (1.0 - ADAM_B2) * (gg * gg)
        mo_ref[...] = mn
        vo_ref[...] = vn
        d_ref[...] = -ADAM_LR * ((mn * c1) / (jnp.sqrt(vn * c2) + ADAM_EPS) + ADAM_WD * w_ref[...])

    spec = pl.BlockSpec((tr, c), lambda i: (i, 0))
    o = jax.ShapeDtypeStruct((r, c), F32)
    return _call(body, name, (o, o, o), (r // tr,), [spec] * 4, (spec, spec, spec))(w, g, mom, vel)


def _adamw_matrix(w, joined, own, mom, vel, core):
    _, r, c = w.shape
    h = r // 2
    th = _tile_rows(h, c, 2 << 20)
    nt = h // th
    c1 = 1.0 / (1.0 - ADAM_B1 ** ADAM_STEP)
    c2 = 1.0 / (1.0 - ADAM_B2 ** ADAM_STEP)

    def body(core_ref, w_ref, j_ref, o0_ref, o1_ref, m_ref, v_ref, g_ref, d_ref, mo_ref, vo_ref):
        layer, half = pl.program_id(0), pl.program_id(1)
        mine = jnp.where(layer == 0, o0_ref[...], o1_ref[...])
        gg = jnp.where(half == core_ref[0], mine, j_ref[...])
        g_ref[...] = gg
        mn = ADAM_B1 * m_ref[...] + (1.0 - ADAM_B1) * gg
        vn = ADAM_B2 * v_ref[...] + (1.0 - ADAM_B2) * (gg * gg)
        mo_ref[...] = mn
        vo_ref[...] = vn
        d_ref[...] = -ADAM_LR * ((mn * c1) / (jnp.sqrt(vn * c2) + ADAM_EPS) + ADAM_WD * w_ref[...])

    full = pl.BlockSpec((None, th, c), lambda l, hh, i, cr: (l, hh * nt + i, 0))
    other = pl.BlockSpec((None, th, c), lambda l, hh, i, cr: (l, (1 - cr[0]) * nt + i, 0))
    part = pl.BlockSpec((th, c), lambda l, hh, i, cr: (i, 0))
    o = jax.ShapeDtypeStruct(w.shape, F32)
    return _call(body, "adamw_matrix", (o, o, o, o), (2, 2, nt), [full, other, part, part, full, full],
                 (full, full, full, full), prefetch=1)(core, w, joined, own[0], own[1], mom, vel)


def _place():
    x, y, c = lax.axis_index("x"), lax.axis_index("y"), lax.axis_index("c")
    others = [(1 - x, y), (x, 1 - y), (1 - x, 1 - y)]
    return x, y, c, others


def _gather_job(items):
    nw = len(items)

    def views(s_ref, g_ref, layer, c):
        if layer is None:
            return s_ref, s_ref.at[c], lambda chip, cc: g_ref.at[chip, cc]
        hr = s_ref.shape[1] // 2
        return s_ref.at[layer], s_ref.at[layer, pl.ds(c * hr, hr)], lambda chip, cc: g_ref.at[chip, pl.ds(cc * hr, hr)]

    def copies(s_refs, g_refs, send, recv):
        x, y, c, others = _place()
        chip = 2 * x + y
        firsts, arrive, passed, arrive2 = [], [], [], []
        for w, (_, layer) in enumerate(items):
            whole, src, dst = views(s_refs[w], g_refs[w], layer, c)

            def rc(kk, s, d, to, w=w):
                return pltpu.make_async_remote_copy(src_ref=s, dst_ref=d, send_sem=send.at[w, kk], recv_sem=recv.at[w, kk],
                                                    device_id=to, device_id_type=MESH)
            firsts.append(rc(6, whole, g_refs[w].at[chip], (x, y, 1 - c)))
            arrive2.append(rc(6, whole, g_refs[w].at[chip], (x, y, c)))
            for k, (px, py) in enumerate(others):
                got, got2 = dst(2 * px + py, c), dst(2 * px + py, 1 - c)
                firsts.append(rc(k, src, dst(chip, c), (px, py, c)))
                arrive.append(rc(k, got, got, (x, y, c)))
                passed.append(rc(3 + k, got, got, (x, y, 1 - c)))
                arrive2.append(rc(3 + k, got2, got2, (x, y, c)))
        return firsts, arrive, passed, arrive2

    def start(s_refs, g_refs, send, recv):
        for cp in copies(s_refs, g_refs, send, recv)[0]:
            cp.start()

    def finish(s_refs, g_refs, send, recv):
        firsts, arrive, passed, arrive2 = copies(s_refs, g_refs, send, recv)
        for a, p in zip(arrive, passed):
            a.wait_recv()
            p.start()
        for a in arrive2:
            a.wait_recv()
        for cp in firsts + passed:
            cp.wait_send()

    outs = [jax.ShapeDtypeStruct((NCHIP,) + (s.shape if layer is None else s.shape[1:]), s.dtype) for s, layer in items]
    return _Job([s for s, _ in items], outs, (nw, 7), start, finish)


def _swap_job(grads):
    def copies(d_refs, a_refs, send, recv):
        x, y, c, _ = _place()
        cps = []
        for w in range(len(grads)):
            h = d_refs[w].shape[1] // 2
            cps.append(pltpu.make_async_remote_copy(
                src_ref=d_refs[w].at[:, pl.ds((1 - c) * h, h), :], dst_ref=a_refs[w], send_sem=send.at[w], recv_sem=recv.at[w],
                device_id=(x, y, 1 - c), device_id_type=MESH))
        return cps

    def start(*r):
        for cp in copies(*r):
            cp.start()

    def finish(*r):
        for cp in copies(*r):
            cp.wait()

    outs = [jax.ShapeDtypeStruct((NCHIP, g.shape[1] // 2, g.shape[2]), g.dtype) for g in grads]
    return _Job(list(grads), outs, (len(grads),), start, finish)


def _exchange_job(parts):
    def copies(q_refs, b_refs, send, recv):
        x, y, c, others = _place()
        cps = []
        for w in range(len(parts)):
            for k, (px, py) in enumerate(others):
                cps.append(pltpu.make_async_remote_copy(
                    src_ref=q_refs[w].at[2 * px + py], dst_ref=b_refs[w].at[k], send_sem=send.at[w, k], recv_sem=recv.at[w, k],
                    device_id=(px, py, c), device_id_type=MESH))
        return cps

    def start(*r):
        for cp in copies(*r):
            cp.start()

    def finish(*r):
        for cp in copies(*r):
            cp.wait()

    outs = [jax.ShapeDtypeStruct((3,) + p.shape[1:], p.dtype) for p in parts]
    return _Job(list(parts), outs, (len(parts), 3), start, finish)


def _run_job(name, job):
    n_in, n_out = len(job.ins), len(job.outs)

    def body(*refs):
        ins, outs = refs[:n_in], refs[n_in:n_in + n_out]
        send, recv = refs[n_in + n_out:]
        job.start(ins, outs, send, recv)
        job.finish(ins, outs, send, recv)

    return pl.pallas_call(
        body, name=name, out_shape=tuple(job.outs), in_specs=[_ANY] * n_in, out_specs=tuple([_ANY] * n_out),
        scratch_shapes=[pltpu.SemaphoreType.DMA(job.sems), pltpu.SemaphoreType.DMA(job.sems)])(*job.ins)


def _sibling_join(halves):
    nw = len(halves)

    def body(*refs):
        h_refs = refs[:2 * nw]
        f_refs = refs[2 * nw:3 * nw]
        send, recv = refs[3 * nw:]
        x, y, c, _ = _place()
        cps = []
        for w in range(nw):
            for l in range(2):
                src = h_refs[2 * w + l]
                h = src.shape[0]
                dst = f_refs[w].at[l, pl.ds(c * h, h), :]
                cp = pltpu.make_async_remote_copy(src_ref=src, dst_ref=dst, send_sem=send.at[w, l], recv_sem=recv.at[w, l],
                                                  device_id=(x, y, 1 - c), device_id_type=MESH)
                cp.start()
                cps.append(cp)
        for w in range(nw):
            for l in range(2):
                src = h_refs[2 * w + l]
                h = src.shape[0]
                other = f_refs[w].at[l, pl.ds((1 - c) * h, h), :]
                pltpu.make_async_remote_copy(src_ref=src, dst_ref=other, send_sem=send.at[w, l], recv_sem=recv.at[w, l],
                                             device_id=(x, y, c), device_id_type=MESH).wait_recv()
        for cp in cps:
            cp.wait_send()

    flat = [a for pair in halves for a in pair]
    outs = tuple(jax.ShapeDtypeStruct((2, 2 * pair[0].shape[0], pair[0].shape[1]), F32) for pair in halves)
    return pl.pallas_call(
        body, name="grad_sibling_join", out_shape=outs, in_specs=[_ANY] * (2 * nw), out_specs=tuple([_ANY] * nw),
        scratch_shapes=[pltpu.SemaphoreType.DMA((nw, 2)), pltpu.SemaphoreType.DMA((nw, 2))])(*flat)


def _allgather_small(v):
    r = v.shape[0]

    def body(x_ref, out_ref, send_sems, recv_sems, local_sem):
        x, y, c, chips = _place()
        me, sibling = (x, y, c), (x, y, 1 - c)

        def slab(px, py, pc):
            return out_ref.at[4 * px + 2 * py + pc]

        def copy(k, block, to, src=None):
            return pltpu.make_async_remote_copy(src_ref=slab(*block) if src is None else src, dst_ref=slab(*block),
                                                send_sem=send_sems.at[k], recv_sem=recv_sems.at[k],
                                                device_id=to, device_id_type=MESH)

        mine = pltpu.make_async_copy(x_ref, slab(*me), local_sem)
        mine.start()
        first = [copy(0, me, sibling, src=x_ref)]
        first += [copy(1 + j, me, (*chip, c), src=x_ref) for j, chip in enumerate(chips)]
        for cp in first:
            cp.start()
        passed = [copy(4 + j, (*chip, c), sibling) for j, chip in enumerate(chips)]
        for j, chip in enumerate(chips):
            copy(1 + j, (*chip, c), me).wait_recv()
            passed[j].start()
        copy(0, sibling, me).wait_recv()
        for j, chip in enumerate(chips):
            copy(4 + j, (*chip, 1 - c), me).wait_recv()
        for cp in first + passed:
            cp.wait_send()
        mine.wait()

    return pl.pallas_call(
        body, name="allgather_small", out_shape=jax.ShapeDtypeStruct((8, r, 128), F32),
        in_specs=[pl.BlockSpec(memory_space=pltpu.VMEM)], out_specs=pl.BlockSpec(memory_space=pltpu.VMEM),
        scratch_shapes=[pltpu.SemaphoreType.DMA((7,)), pltpu.SemaphoreType.DMA((7,)), pltpu.SemaphoreType.DMA],
    )(v)


def _add_half(name, d, a, c):
    _, h, cols = a.shape
    nt = 1
    th = h // nt

    def body(c_ref, d_ref, a_ref, o_ref):
        o_ref[...] = (d_ref[...] + a_ref[...]).astype(BF16)

    return _call(body, name, jax.ShapeDtypeStruct(a.shape, BF16), (NCHIP, nt),
                 [pl.BlockSpec((None, th, cols), lambda p, i, cr: (p, cr[0] * nt + i, 0)),
                  pl.BlockSpec((None, th, cols), lambda p, i, cr: (p, i, 0))],
                 pl.BlockSpec((None, th, cols), lambda p, i, cr: (p, i, 0)), prefetch=1)(c, d, a)


def _add_chips(name, d, a, b, where):
    _, h, cols = a.shape
    th = h // 4 if (h % 64 == 0) else h
    nt = h // th

    def body(w_ref, d_ref, a_ref, b_ref, o_ref):
        own = d_ref[...] + a_ref[...]
        o_ref[...] = ((own + b_ref[0].astype(F32)) + b_ref[1].astype(F32)) + b_ref[2].astype(F32)

    return _call(body, name, jax.ShapeDtypeStruct((h, cols), F32), (nt,),
                 [pl.BlockSpec((None, th, cols), lambda i, wr: (wr[0], wr[1] * nt + i, 0)),
                  pl.BlockSpec((None, th, cols), lambda i, wr: (wr[0], i, 0)),
                  pl.BlockSpec((3, th, cols), lambda i, wr: (0, i, 0))],
                 pl.BlockSpec((th, cols), lambda i, wr: (i, 0)), prefetch=1)(where, d, a, b)


def _pack(arrs):
    pieces = []
    for a in arrs:
        f = a.reshape(-1)
        n = -(-f.shape[0] // 1024) * 1024
        pieces.append(jnp.pad(f, (0, n - f.shape[0])).reshape(-1, 128))
    return jnp.concatenate(pieces, axis=0)


def _unpack(buf, shapes):
    out, r = [], 0
    for s in shapes:
        n = int(np.prod(s))
        rows = -(-n // 1024) * 8
        out.append(buf[r:r + rows].reshape(-1)[:n].reshape(s))
        r += rows
    return out


def kernel(x, meta_tokens, in_ln_g, in_ln_b, rel_bias, w_in, b_in, attn_sinks, w_attn_proj, conv_dw, conv_dw_b, conv_ln_g, conv_ln_b, w_conv_proj, w_out, ln1_g, ln1_b, ffn_w_up, ffn_dw, ffn_dw_b, ffn_w_down, ln2_g, ln2_b, loss_target, m_meta_tokens, m_in_ln_g, m_in_ln_b, m_rel_bias, m_w_in, m_b_in, m_attn_sinks, m_w_attn_proj, m_conv_dw, m_conv_dw_b, m_conv_ln_g, m_conv_ln_b, m_w_conv_proj, m_w_out, m_ln1_g, m_ln1_b, m_ffn_w_up, m_ffn_dw, m_ffn_dw_b, m_ffn_w_down, m_ln2_g, m_ln2_b, v_meta_tokens, v_in_ln_g, v_in_ln_b, v_rel_bias, v_w_in, v_b_in, v_attn_sinks, v_w_attn_proj, v_conv_dw, v_conv_dw_b, v_conv_ln_g, v_conv_ln_b, v_w_conv_proj, v_w_out, v_ln1_g, v_ln1_b, v_ffn_w_up, v_ffn_dw, v_ffn_dw_b, v_ffn_w_down, v_ln2_g, v_ln2_b):
    nex, seq, _ = x.shape
    nblk = seq // BLK + 1
    tp = nblk * BLK
    m = nex * tp
    tm = _row_tile(m)
    ffs = ffn_w_up.shape[2]
    dff = 2 * ffs
    cx, cy, cc = lax.axis_index("x"), lax.axis_index("y"), lax.axis_index("c")
    chip = (2 * cx + cy).astype(jnp.int32)
    core = cc.astype(jnp.int32)

    names = ("in", "ap", "cp", "out", "up", "down")
    big = dict(zip(names, [w_in, w_attn_proj, w_conv_proj, w_out, ffn_w_up, ffn_w_down]))
    sb = {k: v.astype(BF16) for k, v in big.items()}
    gathered = {}

    first_items = [(meta_tokens.reshape(2, N_META // 2, -1), None), (conv_dw, None), (ffn_dw, None)]
    g_meta, g_cdw, g_fdw = _run_job("gather_small", _gather_job(first_items))
    meta_full = jnp.transpose(g_meta, (1, 2, 0, 3)).reshape(N_META, D)
    bias_tab = _bias_build(rel_bias)

    fwd_plan = {("embed_ln", 0): [("in", 0)],
                ("in_proj", 0): [("ap", 0), ("cp", 0), ("out", 0)], ("attn_fwd", 0): [("up", 0)],
                ("conv_fwd", 0): [("down", 0)], ("mix_fwd", 0): [("in", 1)],
                ("out_proj_ln", 0): [("ap", 1), ("cp", 1), ("out", 1)], ("ffn_up_act", 0): [("up", 1), ("down", 1)]}

    def fwd(tag, l, fn, *args):
        keys = fwd_plan.get((tag, l))
        if not keys:
            return fn(*args)
        items = [(sb[k], kl) for k, kl in keys]
        res = fn(*args, side=_gather_job(items))
        for key, g in zip(keys, res[-len(keys):]):
            gathered[key] = g
        main = res[:-len(keys)]
        return main[0] if len(main) == 1 else main

    def layer_weights(l):
        return dict(
            win=_w_in_from_chips(gathered[("in", l)]), bin=_to_new(b_in[l]).reshape(1, IN_COLS),
            cdw=jnp.transpose(g_cdw[:, l], (1, 0, 2)).reshape(CTAPS, CW),
            fdw=jnp.transpose(g_fdw[:, l], (1, 0, 2)).reshape(FTAPS, 2 * dff),
            fdwb=ffn_dw_b[l].reshape(1, 2 * dff))

    raw, h, hb = fwd("embed_ln", 0, _embed_ln, x, meta_full, in_ln_g, in_ln_b, nblk)
    saved, lw = [], []
    for l in range(DEPTH):
        p = layer_weights(l)
        z = fwd("in_proj", l, functools.partial(_mm_bias, "in_proj"), hb, p["win"], p["bin"], IN_COLS // 3, tm)
        a = fwd("attn_fwd", l, _attn_fwd, z, bias_tab, attn_sinks[l], nex, nblk)
        ccv, cs = fwd("conv_fwd", l, _conv_fwd, z, p["cdw"], conv_dw_b[l], conv_ln_g[l], conv_ln_b[l], nex, tp)
        p["wap"], p["wcp"] = gathered[("ap", l)], gathered[("cp", l)]
        ya, yc, mixed = fwd("mix_fwd", l, _mix_fwd, a, cs, p["wap"], p["wcp"], z, tm)
        p["wout"] = gathered[("out", l)].reshape(D, D)
        r1, h1, h1b = fwd("out_proj_ln", l, functools.partial(_mm_res_ln, "out_proj_ln"), mixed, p["wout"], h, ln1_g[l], ln1_b[l], tm)
        p["wup"] = gathered[("up", l)]
        up3, ug, act = fwd("ffn_up_act", l, _ffn_up_act, h1b, p["wup"], p["fdw"], p["fdwb"], tm, nex, tp)
        p["wdown"] = gathered[("down", l)].reshape(dff, D)
        r2, h2, h2b = _mm_res_ln("ffn_down_ln", act, p["wdown"], h1, ln2_g[l], ln2_b[l], tm)
        saved.append(dict(hb=hb, z=z, a=a, cc=ccv, cs=cs, ya=ya, yc=yc, mixed=mixed, r1=r1, h1b=h1b, up3=up3, ug=ug, act=act, r2=r2))
        lw.append(p)
        h, hb = h2, h2b

    dy, sq = _loss_grad(h, loss_target, nblk)

    grads, swapped, pair_sums, reduced = {}, {}, {}, {}
    cvec, where = core.reshape(1), jnp.stack([chip, core])
    last = [(k, DEPTH - 1) for k in names]
    bwd_plan = {("ln2_bwd", 0): ("swap", last),
                ("ffn_bwd", 0): ("exch", [("up", 1), ("down", 1)]),
                ("dw_down", 0): ("exch", [("ap", 1), ("cp", 1), ("out", 1)]),
                ("dw_up", 0): ("exch", [("in", 1)]),
                ("ln1_bwd", 0): ("swap", [("down", 0), ("up", 0)]),
                ("conv_bwd_b", 0): ("swap", [("out", 0), ("ap", 0), ("cp", 0)]),
                ("attn_bwd", 0): ("exch", [("down", 0), ("up", 0)]),
                ("dw_in_gates", 0): ("exch", [("out", 0), ("ap", 0), ("cp", 0)]),
                ("in_ln_bwd", 0): ("swap", [("in", 0)])}

    def after(kind, keys, outs):
        for key, o in zip(keys, outs):
            if kind == "swap":
                swapped[key] = o
                pair_sums[key] = _add_half("grad_add_sibling", grads[key], o, cvec)
            else:
                reduced[key] = _add_chips("grad_add_chips", grads[key], swapped[key], o, where)

    def bwd(tag, l, fn, *args):
        plan = bwd_plan.get((tag, l))
        if plan is None:
            return fn(*args)
        kind, keys = plan
        job = _swap_job([grads[k] for k in keys]) if kind == "swap" else _exchange_job([pair_sums[k] for k in keys])
        res = fn(*args, side=job)
        after(kind, keys, res[-len(keys):])
        main = res[:-len(keys)]
        return main[0] if len(main) == 1 else main

    small = {}
    prev_a, prev_w, prev_cols = (), None, None
    dprev = dy
    for l in reversed(range(DEPTH)):
        p, s = lw[l], saved[l]
        dr2, dr2b, dg2, db2 = bwd("ln2_bwd", l, functools.partial(_ln_bwd_call, "ln2_bwd"), dprev, s["r2"], ln2_g[l], tm,
                                  prev_a, prev_w, prev_cols)
        dpre3, dfdw, dfdwb = bwd("ffn_bwd", l, _ffn_bwd, dr2b, p["wdown"], s["ug"], s["up3"], p["fdw"], tm, nex, tp)
        grads[("down", l)] = bwd("dw_down", l, functools.partial(_mm_tn, "dw_down"), s["act"], dr2b, ffs,
                                 D // 2).reshape(NCHIP, dff // NCHIP, D)
        grads[("up", l)] = bwd("dw_up", l, functools.partial(_mm_tn, "dw_up"), s["h1b"], dpre3, D, ffs,
                               lambda j: (j // 2, j % 2), True)
        dr1, dr1b, dg1, db1 = bwd("ln1_bwd", l, functools.partial(_ln_bwd_call, "ln1_bwd"), dr2, s["r1"], ln1_g[l], tm // 2,
                                  (dpre3,), p["wup"])
        dya, dyc, dzg, csg = _gate_bwd(dr1b, p["wout"], s["ya"], s["yc"], s["z"], tm)
        grads[("out", l)] = _mm_tn("dw_out", s["mixed"], dr1b, D, D // 2).reshape(NCHIP, D // NCHIP, D)
        grads[("ap", l)] = _mm_tn("dw_attn_proj", s["a"], dya, AW, D // NCHIP, chip_out=True)
        grads[("cp", l)] = _mm_tn("dw_conv_proj", s["cs"], dyc, CW, D // NCHIP, chip_out=True)
        dcc, dclg, dclb, dcwb = _conv_bwd_a(dyc, p["wcp"], s["cc"], conv_ln_g[l], conv_ln_b[l], tm)
        dzc, dcdw, csc = bwd("conv_bwd_b", l, _conv_bwd_b, dcc, s["z"], p["cdw"], nex, tp)
        dzq, csq, dsk, dbias = bwd("attn_bwd", l, _attn_bwd, s["z"], bias_tab, attn_sinks[l], dya, p["wap"], nex, nblk)
        gin = [bwd("dw_in_gates", l, functools.partial(_mm_tn, "dw_in_gates"), s["hb"], dzg, D, D // 2),
               _mm_tn("dw_in_conv", s["hb"], dzc, D, CW),
               _mm_tn("dw_in_qkv", s["hb"], dzq, D, 2 * KVW)]
        grads[("in", l)] = _w_in_to_chips(gin)
        small[l] = dict(
            b_in=_to_old(jnp.concatenate([csg, csc, csq], axis=1)).reshape(IN_COLS), attn_sinks=dsk[0, :NQ],
            conv_dw=dcdw[:CTAPS], conv_dw_b=dcwb.reshape(CW), conv_ln_g=dclg.reshape(CW), conv_ln_b=dclb.reshape(CW),
            ln1_g=dg1.reshape(D), ln1_b=db1.reshape(D),
            ffn_dw=jnp.transpose(dfdw, (1, 0, 2)).reshape(FTAPS, 2 * dff), ffn_dw_b=jnp.transpose(dfdwb, (1, 0, 2)).reshape(2 * dff),
            ln2_g=dg2.reshape(D), ln2_b=db2.reshape(D), bias=dbias)
        dprev = dr1
        prev_a, prev_w, prev_cols = (dzg, dzc, dzq), p["win"], [(C_GATES, C_CONV), (C_CONV, C_QKV), (C_QKV, IN_COLS)]
    draw, _, dg0, db0 = bwd("in_ln_bwd", 0, functools.partial(_ln_bwd_call, "in_ln_bwd"), dprev, raw, in_ln_g, tm,
                            prev_a, prev_w, prev_cols)
    draw3 = draw.reshape(nex, tp, D)
    grad_x = draw3[:, BLK:, :]
    dmeta = _sum0("meta_grad_sum", draw3[:, PAD:BLK, :])

    names_l = ["b_in", "attn_sinks", "conv_dw", "conv_dw_b", "conv_ln_g", "conv_ln_b", "ln1_g", "ln1_b", "ffn_dw", "ffn_dw_b", "ln2_g", "ln2_b"]
    dbias_all = _bias_grad(jnp.concatenate([small[l]["bias"] for l in range(DEPTH)], axis=0))
    part_list = [sq, dmeta, dg0.reshape(D), db0.reshape(D), dbias_all]
    part_list += [jnp.stack([small[0][n], small[1][n]]) for n in names_l]
    shapes_small = [tuple(a.shape) for a in part_list]
    tot = _sum0("small_grad_sum", _allgather_small(_pack(part_list)))
    (sq_all, g_meta_f, g_inlg, g_inlb, g_biasp, g_bin, g_sinks, g_cdw_f, g_cdwb, g_clg, g_clb, g_l1g, g_l1b, g_fdw_f, g_fdwb,
     g_l2g, g_l2b) = _unpack(tot, shapes_small)
    loss = 0.5 / D * jnp.sum(sq_all)
    g_relb = g_biasp
    csh = D // NCHIP
    g_meta_s = lax.dynamic_slice_in_dim(g_meta_f, chip * csh, csh, axis=1)
    g_cdw_s = lax.dynamic_slice_in_dim(g_cdw_f, chip * (CW // NCHIP), CW // NCHIP, axis=2)
    g_fdw_s = lax.dynamic_slice_in_dim(g_fdw_f, chip * ffs, ffs, axis=2)

    tail = [("in", 0)]
    after("exch", tail, _run_job("grad_chip_exchange", _exchange_job([pair_sums[k] for k in tail])))
    joined = _sibling_join([[reduced[(k, l)] for l in range(DEPTH)] for k in names])

    moms = [m_w_in, m_w_attn_proj, m_w_conv_proj, m_w_out, m_ffn_w_up, m_ffn_w_down]
    vels = [v_w_in, v_w_attn_proj, v_w_conv_proj, v_w_out, v_ffn_w_up, v_ffn_w_down]
    big_out = [_adamw_matrix(big[k], f, [reduced[(k, l)] for l in range(DEPTH)], mo, ve, cvec)
               for k, f, mo, ve in zip(names, joined, moms, vels)]

    sm_w = [meta_tokens, in_ln_g, in_ln_b, rel_bias, b_in, attn_sinks, conv_dw, conv_dw_b, conv_ln_g, conv_ln_b, ln1_g, ln1_b,
            ffn_dw, ffn_dw_b, ln2_g, ln2_b]
    sm_m = [m_meta_tokens, m_in_ln_g, m_in_ln_b, m_rel_bias, m_b_in, m_attn_sinks, m_conv_dw, m_conv_dw_b, m_conv_ln_g, m_conv_ln_b,
            m_ln1_g, m_ln1_b, m_ffn_dw, m_ffn_dw_b, m_ln2_g, m_ln2_b]
    sm_v = [v_meta_tokens, v_in_ln_g, v_in_ln_b, v_rel_bias, v_b_in, v_attn_sinks, v_conv_dw, v_conv_dw_b, v_conv_ln_g, v_conv_ln_b,
            v_ln1_g, v_ln1_b, v_ffn_dw, v_ffn_dw_b, v_ln2_g, v_ln2_b]
    sm_g = [g_meta_s, g_inlg, g_inlb, g_relb, g_bin, g_sinks, g_cdw_s, g_cdwb, g_clg, g_clb, g_l1g, g_l1b, g_fdw_s, g_fdwb, g_l2g, g_l2b]
    sm_shapes = [tuple(a.shape) for a in sm_w]
    sd, smn, svn = _adamw("adamw_small", _pack(sm_w), _pack(sm_g), _pack(sm_m), _pack(sm_v))
    sd, smn, svn = _unpack(sd, sm_shapes), _unpack(smn, sm_shapes), _unpack(svn, sm_shapes)

    order = ["meta_tokens", "in_ln_g", "in_ln_b", "rel_bias", "w_in", "b_in", "attn_sinks", "w_attn_proj", "conv_dw", "conv_dw_b",
             "conv_ln_g", "conv_ln_b", "w_conv_proj", "w_out", "ln1_g", "ln1_b", "ffn_w_up", "ffn_dw", "ffn_dw_b", "ffn_w_down",
             "ln2_g", "ln2_b"]
    small_names = ["meta_tokens", "in_ln_g", "in_ln_b", "rel_bias", "b_in", "attn_sinks", "conv_dw", "conv_dw_b", "conv_ln_g",
                   "conv_ln_b", "ln1_g", "ln1_b", "ffn_dw", "ffn_dw_b", "ln2_g", "ln2_b"]
    big_names = ["w_in", "w_attn_proj", "w_conv_proj", "w_out", "ffn_w_up", "ffn_w_down"]
    res = {}
    for i, n in enumerate(small_names):
        res[n] = (sm_g[i], sd[i], smn[i], svn[i])
    for i, n in enumerate(big_names):
        res[n] = big_out[i]
    outs = [loss, grad_x]
    for k in range(4):
        outs += [res[n][k] for n in order]
    return tuple(outs)
```

```python
import functools
import math
from typing import Any, Callable, NamedTuple, Sequence

import numpy as np
import jax
import jax.numpy as jnp
from jax import lax
from jax.experimental import pallas as pl
from jax.experimental.pallas import tpu as pltpu

F32 = jnp.float32
BF16 = jnp.bfloat16
MESH = pl.DeviceIdType.MESH

D = 1024
N_META = 16
BLK = 128
PAD = BLK - N_META
HD = 64
NQ = 8
NKV = 2
GRP = NQ // NKV
AW = NQ * HD
KVW = NKV * HD
CW = D // 2
CTAPS = 31
FTAPS = 3
NBUCKET = 32
MAXDIST = 128
EPS = 1e-5
DEPTH = 2
ALPHA = (2.0 * DEPTH) ** 0.25
NCHIP = 4
NKEY = 3 * BLK
NEG = -1e30
CHALO = 32
FHALO = 8
IN_COLS = AW + 2 * KVW + 2 * CW + 2 * D
_OLD = dict(q=(0, AW), k=(AW, AW + KVW), v=(AW + KVW, AW + 2 * KVW), cv=(AW + 2 * KVW, AW + 2 * KVW + CW),
            cg=(AW + 2 * KVW + CW, AW + 2 * KVW + 2 * CW), ga=(AW + 2 * KVW + 2 * CW, AW + 2 * KVW + 2 * CW + D),
            gc=(AW + 2 * KVW + 2 * CW + D, IN_COLS))
_NEW_ORDER = ("ga", "gc", "cv", "cg", "q", "k", "v")
C_GATES, C_CONV, C_QKV = 0, 2 * D, 2 * D + 2 * CW

ADAM_LR, ADAM_B1, ADAM_B2, ADAM_EPS, ADAM_WD, ADAM_STEP = 0.001, 0.9, 0.999, 1e-08, 0.01, 10


def _to_new(a):
    return jnp.concatenate([a[..., _OLD[n][0]:_OLD[n][1]] for n in _NEW_ORDER], axis=-1)


def _to_old(a):
    offs, o = {}, 0
    for n in _NEW_ORDER:
        w = _OLD[n][1] - _OLD[n][0]
        offs[n] = (o, o + w)
        o += w
    return jnp.concatenate([a[..., offs[n][0]:offs[n][1]] for n in ("q", "k", "v", "cv", "cg", "ga", "gc")], axis=-1)


def _new_starts():
    starts, o = {}, 0
    for n in _NEW_ORDER:
        starts[n] = o
        o += _OLD[n][1] - _OLD[n][0]
    return starts


def _w_in_from_chips(g4):
    cs = IN_COLS // NCHIP
    pieces = []
    for n in _NEW_ORDER:
        lo, hi = _OLD[n]
        while lo < hi:
            chip = lo // cs
            end = min(hi, (chip + 1) * cs)
            pieces.append(g4[chip][:, lo - chip * cs:end - chip * cs])
            lo = end
    return jnp.concatenate(pieces, axis=1)


def _w_in_to_chips(parts):
    cs = IN_COLS // NCHIP
    starts = _new_starts()
    bounds, o = [], 0
    for p in parts:
        bounds.append((o, o + p.shape[1], p))
        o += p.shape[1]

    def new_cols(a, b):
        out = []
        for s, e, p in bounds:
            lo, hi = max(a, s), min(b, e)
            if lo < hi:
                out.append(p[:, lo - s:hi - s])
        return out

    slabs = []
    for chip in range(NCHIP):
        pieces = []
        for n in ("q", "k", "v", "cv", "cg", "ga", "gc"):
            lo, hi = max(_OLD[n][0], chip * cs), min(_OLD[n][1], (chip + 1) * cs)
            if lo < hi:
                pieces += new_cols(starts[n] + lo - _OLD[n][0], starts[n] + hi - _OLD[n][0])
        slabs.append(jnp.concatenate(pieces, axis=1))
    return jnp.stack(slabs)


class _Job(NamedTuple):
    ins: Sequence[Any]
    outs: Sequence[Any]
    sems: tuple
    start: Callable
    finish: Callable


_ANY = pl.BlockSpec(memory_space=pl.ANY)


def _call(body, name, out_shape, grid, in_specs, out_specs, scratch=(), prefetch=0, side=None):
    params = pltpu.CompilerParams(dimension_semantics=("arbitrary",) * len(grid))
    if side is None:
        if prefetch:
            gs = pltpu.PrefetchScalarGridSpec(num_scalar_prefetch=prefetch, grid=grid, in_specs=in_specs,
                                              out_specs=out_specs, scratch_shapes=list(scratch))
            return pl.pallas_call(body, name=name, out_shape=out_shape, grid_spec=gs, compiler_params=params)
        return pl.pallas_call(body, name=name, out_shape=out_shape, grid=grid, in_specs=in_specs, out_specs=out_specs,
                              scratch_shapes=list(scratch), compiler_params=params)
    assert not prefetch
    single = not isinstance(out_shape, (tuple, list))
    main_shapes = (out_shape,) if single else tuple(out_shape)
    main_specs = (out_specs,) if single else tuple(out_specs)
    n_in, n_sin, n_out, n_sout, n_scr = len(in_specs), len(side.ins), len(main_shapes), len(side.outs), len(scratch)

    def wrapped(*refs):
        main_in, sin = refs[:n_in], refs[n_in:n_in + n_sin]
        o0 = n_in + n_sin
        main_out, sout = refs[o0:o0 + n_out], refs[o0 + n_out:o0 + n_out + n_sout]
        s0 = o0 + n_out + n_sout
        main_scr, (send, recv) = refs[s0:s0 + n_scr], refs[s0 + n_scr:]
        first = functools.reduce(lambda a, b: a & b, [pl.program_id(k) == 0 for k in range(len(grid))])
        last = functools.reduce(lambda a, b: a & b, [pl.program_id(k) == grid[k] - 1 for k in range(len(grid))])

        @pl.when(first)
        def _():
            side.start(sin, sout, send, recv)

        body(*main_in, *main_out, *main_scr)

        @pl.when(last)
        def _():
            side.finish(sin, sout, send, recv)

    call = pl.pallas_call(
        wrapped, name=name, out_shape=main_shapes + tuple(side.outs), grid=grid,
        in_specs=list(in_specs) + [_ANY] * n_sin, out_specs=main_specs + tuple([_ANY] * n_sout),
        scratch_shapes=list(scratch) + [pltpu.SemaphoreType.DMA(side.sems), pltpu.SemaphoreType.DMA(side.sems)],
        compiler_params=params)
    return lambda *args: call(*args, *side.ins)


def _row_tile(m):
    best = 32
    for t in range(32, 641, 32):
        if m % t == 0:
            best = t
    return best


def _pad_rows(rows, nex, tp):
    m = rows < PAD
    for b in range(1, nex):
        m = m | ((rows >= b * tp) & (rows < b * tp + PAD))
    return m


def _ln_stats(x):
    mu = jnp.mean(x, axis=-1, keepdims=True)
    xc = x - mu
    var = jnp.mean(xc * xc, axis=-1, keepdims=True)
    rstd = lax.rsqrt(var + EPS)
    return xc * rstd, rstd


def _ln_bwd(dy, xhat, rstd, g):
    dxh = dy * g
    m1 = jnp.mean(dxh, axis=-1, keepdims=True)
    m2 = jnp.mean(dxh * xhat, axis=-1, keepdims=True)
    return rstd * (dxh - m1 - xhat * m2)


def _dot(a, b):
    return jnp.dot(a, b, preferred_element_type=F32)


def _dot_nt(a, b):
    return lax.dot_general(a, b, (((1,), (1,)), ((), ())), preferred_element_type=F32)


def _dot_tn(a, b):
    return lax.dot_general(a, b, (((0,), (0,)), ((), ())), preferred_element_type=F32)


def _sigmoid(x):
    return 1.0 / (1.0 + jnp.exp(-x))


def _bucket_np(d):
    n = np.maximum(d, 0)
    max_exact = NBUCKET // 2
    nf = np.maximum(n, 1).astype(np.float32)
    large = max_exact + (np.log(nf / np.float32(max_exact)) / np.float32(math.log(MAXDIST / max_exact))
                         * np.float32(NBUCKET - max_exact)).astype(np.int32)
    large = np.minimum(large, NBUCKET - 1)
    return np.where(n < max_exact, n, large).astype(np.int32)


def _bias_index():
    i = np.arange(BLK)[:, None]
    j = np.arange(2 * BLK)[None, :]
    d = BLK + i - j
    band_ok = (d >= 0) & (d < BLK)
    band = _bucket_np(d)
    idx = np.full((3, BLK, NKEY), -1, np.int32)
    m = np.arange(N_META)[None, :]
    d0 = (i - PAD) - m
    idx[0, :, 2 * BLK:2 * BLK + N_META] = np.where(d0 >= 0, _bucket_np(d0), -1)
    ok1 = band_ok & (j >= BLK)
    idx[1, :, :2 * BLK] = np.where(ok1, band, -1)
    idx[1, :, 2 * BLK:2 * BLK + N_META] = _bucket_np((N_META + i) - m)
    idx[2, :, :2 * BLK] = np.where(band_ok, band, -1)
    idx[2, :, 2 * BLK:2 * BLK + N_META] = NBUCKET - 1
    return idx


def _bias_build(rel_bias):
    idx = jnp.asarray(_bias_index())

    def body(idx_ref, rb_ref, o_ref):
        ix = idx_ref[...]
        for h in range(NQ):
            acc = jnp.full(ix.shape, NEG, F32)
            for b in range(NBUCKET):
                acc = jnp.where(ix == b, rb_ref[b, h], acc)
            o_ref[:, h, :, :] = acc

    return pl.pallas_call(
        body, name="bias_build", out_shape=jax.ShapeDtypeStruct((3, NQ, BLK, NKEY), F32),
        in_specs=[pl.BlockSpec(memory_space=pltpu.VMEM), pl.BlockSpec(memory_space=pltpu.SMEM)],
        out_specs=pl.BlockSpec(memory_space=pltpu.VMEM))(idx, rel_bias)


def _bias_grad(dbias):
    idx = jnp.asarray(_bias_index())

    def body(idx_ref, d_ref, o_ref):
        d = jnp.sum(d_ref[...], axis=0)
        for b in range(NBUCKET):
            acc = jnp.zeros((NQ, NKEY), F32)
            for case in range(3):
                hit = (idx_ref[case] == b)[None, :, :]
                acc = acc + jnp.sum(jnp.where(hit, d[case], 0.0), axis=1)
            o_ref[b] = jnp.sum(acc, axis=-1, keepdims=True)

    out = pl.pallas_call(
        body, name="bias_grad", out_shape=jax.ShapeDtypeStruct((NBUCKET, NQ, 1), F32),
        in_specs=[pl.BlockSpec(memory_space=pltpu.VMEM), pl.BlockSpec(memory_space=pltpu.VMEM)],
        out_specs=pl.BlockSpec(memory_space=pltpu.VMEM))(idx, dbias)
    return out.reshape(NBUCKET, NQ)


def _embed_ln(x, meta, g, b, nblk, side=None):
    nex, seq, _ = x.shape
    m = nex * nblk * BLK

    def body(x_ref, meta_ref, g_ref, b_ref, raw_ref, h_ref, hb_ref):
        j = pl.program_id(1)

        @pl.when(j == 0)
        def _():
            raw_ref[0:PAD, :] = jnp.zeros((PAD, D), F32)
            raw_ref[PAD:BLK, :] = meta_ref[...]

        @pl.when(j > 0)
        def _():
            raw_ref[...] = x_ref[...]

        xhat, _ = _ln_stats(raw_ref[...])
        y = xhat * g_ref[...] + b_ref[...]
        h_ref[...] = y
        hb_ref[...] = y.astype(BF16)

    row = lambda bb, j: (bb * nblk + j, 0)
    return _call(
        body, "embed_ln",
        (jax.ShapeDtypeStruct((m, D), F32), jax.ShapeDtypeStruct((m, D), F32), jax.ShapeDtypeStruct((m, D), BF16)),
        (nex, nblk),
        [pl.BlockSpec((None, BLK, D), lambda bb, j: (bb, jnp.maximum(j - 1, 0), 0)),
         pl.BlockSpec((N_META, D), lambda bb, j: (0, 0)),
         pl.BlockSpec((1, D), lambda bb, j: (0, 0)), pl.BlockSpec((1, D), lambda bb, j: (0, 0))],
        (pl.BlockSpec((BLK, D), row), pl.BlockSpec((BLK, D), row), pl.BlockSpec((BLK, D), row)), side=side,
    )(x, meta, g.reshape(1, D), b.reshape(1, D))


def _mm_bias(name, a, w, bias, tn, tm, side=None):
    m, k = a.shape
    n = w.shape[1]

    def body(a_ref, w_ref, b_ref, o_ref):
        o_ref[...] = _dot(a_ref[...], w_ref[...]) + b_ref[...]

    return _call(body, name, jax.ShapeDtypeStruct((m, n), F32), (n // tn, m // tm),
                 [pl.BlockSpec((tm, k), lambda j, i: (i, 0)), pl.BlockSpec((k, tn), lambda j, i: (0, j)),
                  pl.BlockSpec((1, tn), lambda j, i: (0, j))],
                 pl.BlockSpec((tm, tn), lambda j, i: (i, j)), side=side)(a, w, bias)


def _ffn_up_act(a, w4, cw, cb, tm, nex, tp, side=None):
    m, k = a.shape
    ffs = w4.shape[2]

    def body(a_ref, wu_ref, wg_ref, cu_ref, cg_ref, bu_ref, bg_ref, up_ref, ug_ref, act_ref, win):
        i = pl.program_id(1)

        @pl.when(i == 0)
        def _():
            win[:, 0:FHALO, :] = jnp.zeros((2, FHALO, ffs), F32)

        rows = i * tm + lax.broadcasted_iota(jnp.int32, (tm, 1), 0)
        pad = _pad_rows(rows, nex, tp)
        av = a_ref[...]
        for p, w_ref in ((0, wu_ref), (1, wg_ref)):
            x = jnp.where(pad, 0.0, _dot(av, w_ref[...]))
            win[p, FHALO:FHALO + tm, :] = x
            up_ref[p] = x.astype(BF16)
        for r0 in range(0, tm, RCH):
            for c0, c1 in _lane_groups(ffs):
                u = _conv3(win, 0, r0, c0, c1, cu_ref, bu_ref)
                g = _conv3(win, 1, r0, c0, c1, cg_ref, bg_ref)
                ug_ref[0, r0:r0 + RCH, c0:c1] = u.astype(BF16)
                ug_ref[1, r0:r0 + RCH, c0:c1] = g.astype(BF16)
                act_ref[r0:r0 + RCH, c0:c1] = (g * (0.5 * (1.0 + lax.erf(g * (1.0 / math.sqrt(2.0))))) * u).astype(BF16)
        win[:, 0:FHALO, :] = win[:, tm:tm + FHALO, :]

    wide = jax.ShapeDtypeStruct((2, m, 2 * ffs), BF16)
    return _call(body, "ffn_up_act", (wide, wide, jax.ShapeDtypeStruct((m, 2 * ffs), BF16)),
                 (2, m // tm),
                 [pl.BlockSpec((tm, k), lambda c, i: (i, 0)),
                  pl.BlockSpec((None, k, ffs), lambda c, i: (c, 0, 0)), pl.BlockSpec((None, k, ffs), lambda c, i: (c + 2, 0, 0)),
                  pl.BlockSpec((FTAPS, ffs), lambda c, i: (0, c)), pl.BlockSpec((FTAPS, ffs), lambda c, i: (0, c + 2)),
                  pl.BlockSpec((1, ffs), lambda c, i: (0, c)), pl.BlockSpec((1, ffs), lambda c, i: (0, c + 2))],
                 (pl.BlockSpec((2, tm, ffs), lambda c, i: (0, i, c)), pl.BlockSpec((2, tm, ffs), lambda c, i: (0, i, c)),
                  pl.BlockSpec((tm, ffs), lambda c, i: (i, c))),
                 scratch=[pltpu.VMEM((2, FHALO + tm, ffs), F32)], side=side)(a, w4, w4, cw, cw, cb, cb)


def _fill_kv(ks, vs, e, prev_ref, cur_ref, meta_ref):
    for piece, lo, n in ((prev_ref, 0, BLK), (cur_ref, BLK, BLK), (meta_ref, 2 * BLK, N_META)):
        val = piece[e]
        for hk in range(NKV):
            ks[e, hk, lo:lo + n, :] = val[:, hk * HD:(hk + 1) * HD].astype(BF16)
            vs[e, hk, lo:lo + n, :] = val[:, KVW + hk * HD:KVW + (hk + 1) * HD].astype(BF16)
    for hk in range(NKV):
        ks[e, hk, 2 * BLK + N_META:NKEY, :] = jnp.zeros((BLK - N_META, HD), BF16)
        vs[e, hk, 2 * BLK + N_META:NKEY, :] = jnp.zeros((BLK - N_META, HD), BF16)


def _interleave(chains):
    live = list(chains)
    while live:
        for c in list(live):
            try:
                next(c)
            except StopIteration:
                live.remove(c)


def _head_softmax(q, ks_hk, bias_ref, sink_ref, h, out):
    qh = q[:, h * HD:(h + 1) * HD].astype(BF16)
    yield
    s = _dot_nt(qh, ks_hk) * (HD ** -0.5) + bias_ref[h]
    yield
    sink = sink_ref[0, h]
    mx = jnp.maximum(jnp.max(s, axis=-1, keepdims=True), sink)
    yield
    p = jnp.exp(s - mx)
    es = jnp.exp(sink - mx)
    yield
    inv = 1.0 / (jnp.sum(p, axis=-1, keepdims=True) + es)
    yield
    out["q"], out["p"], out["sink"] = qh, p * inv, es * inv
    yield


def _attn_specs(nex, blk_of):
    qcol, kvcol = (C_QKV) // AW, (C_QKV + AW) // (2 * KVW)
    return [
        pl.BlockSpec((nex, BLK, AW), lambda j: (0, blk_of(j), qcol)),
        pl.BlockSpec((nex, BLK, 2 * KVW), lambda j: (0, blk_of(j), kvcol)),
        pl.BlockSpec((nex, BLK, 2 * KVW), lambda j: (0, jnp.maximum(blk_of(j) - 1, 0), kvcol)),
        pl.BlockSpec((nex, N_META, 2 * KVW), lambda j: (0, PAD // N_META, kvcol)),
        pl.BlockSpec((None, NQ, BLK, NKEY), lambda j: (jnp.minimum(blk_of(j), 2), 0, 0, 0)),
        pl.BlockSpec(memory_space=pltpu.SMEM),
    ]


def _attn_fwd(z, bias, sinks, nex, nblk, side=None):
    m = z.shape[0]
    z3 = z.reshape(nex, nblk * BLK, z.shape[1])

    def head(e, h, q, ks, vs, bias_ref, sink_ref, oacc):
        out = {}
        yield from _head_softmax(q, ks[e, h // GRP], bias_ref, sink_ref, h, out)
        o = _dot(out["p"].astype(BF16), vs[e, h // GRP])
        yield
        oacc[e, :, h * HD:(h + 1) * HD] = o
        yield

    def body(q_ref, cur_ref, prev_ref, meta_ref, bias_ref, sink_ref, o_ref, ks, vs, oacc):
        for e in range(nex):
            _fill_kv(ks, vs, e, prev_ref, cur_ref, meta_ref)
        _interleave([head(e, h, q_ref[e], ks, vs, bias_ref, sink_ref, oacc) for e in range(nex) for h in range(NQ)])
        for e in range(nex):
            o_ref[e] = oacc[e].astype(BF16)

    res = _call(body, "attn_fwd", jax.ShapeDtypeStruct((nex, nblk * BLK, AW), BF16), (nblk,),
                _attn_specs(nex, lambda j: j),
                pl.BlockSpec((nex, BLK, AW), lambda j: (0, j, 0)),
                scratch=[pltpu.VMEM((nex, NKV, NKEY, HD), BF16), pltpu.VMEM((nex, NKV, NKEY, HD), BF16),
                         pltpu.VMEM((nex, BLK, AW), F32)], side=side)(z3, z3, z3, z3, bias, sinks.reshape(1, NQ))
    if side is None:
        return res.reshape(m, AW)
    return (res[0].reshape(m, AW),) + tuple(res[1:])


def _cgate(cv, cg, rows, nex, tp):
    return jnp.where(_pad_rows(rows, nex, tp), 0.0, cv * _sigmoid(cg))


CLANES = 256


def _rolled_up(blk, b):
    return blk if b == 0 else pltpu.roll(blk, blk.shape[0] - b, axis=0)


def _conv_fwd(z, w, wb, g, b, nex, tp, side=None):
    m = z.shape[0]
    tm = BLK
    sub = tm // CHALO
    cvc, cgc = C_CONV // CW, C_CONV // CW + 1

    def body(cv_ref, cg_ref, cvh_ref, cgh_ref, w_ref, wb_ref, g_ref, b_ref, cc_ref, cs_ref, win):
        i = pl.program_id(0)
        rows = i * tm + lax.broadcasted_iota(jnp.int32, (tm, 1), 0)
        hrows = i * tm - CHALO + lax.broadcasted_iota(jnp.int32, (CHALO, 1), 0)
        win[0:CHALO, :] = _cgate(cvh_ref[...], cgh_ref[...], hrows, nex, tp)
        win[CHALO:CHALO + tm, :] = _cgate(cv_ref[...], cg_ref[...], rows, nex, tp)
        for sb in range(sub):
            lo = sb * CHALO
            for c0 in range(0, CW, CLANES):
                blk = win[lo:lo + 2 * CHALO, c0:c0 + CLANES]
                acc = jnp.zeros((CHALO, CLANES), F32) + wb_ref[:, c0:c0 + CLANES]
                for b in range(8):
                    rb = _rolled_up(blk, b)
                    for a in range(5):
                        s = 8 * a + b
                        if 2 <= s <= CTAPS + 1:
                            acc = acc + w_ref[s - 2:s - 1, c0:c0 + CLANES] * rb[8 * a:8 * a + CHALO]
                cc_ref[lo:lo + CHALO, c0:c0 + CLANES] = acc
        xhat, _ = _ln_stats(cc_ref[...])
        cl = xhat * g_ref[...] + b_ref[...]
        cs_ref[...] = (cl * _sigmoid(cl)).astype(BF16)

    halo = lambda i: jnp.maximum(i * sub - 1, 0)
    vec = pl.BlockSpec((1, CW), lambda i: (0, 0))
    return _call(body, "conv_fwd", (jax.ShapeDtypeStruct((m, CW), F32), jax.ShapeDtypeStruct((m, CW), BF16)),
                 (m // tm,),
                 [pl.BlockSpec((tm, CW), lambda i: (i, cvc)), pl.BlockSpec((tm, CW), lambda i: (i, cgc)),
                  pl.BlockSpec((CHALO, CW), lambda i: (halo(i), cvc)), pl.BlockSpec((CHALO, CW), lambda i: (halo(i), cgc)),
                  pl.BlockSpec((CTAPS, CW), lambda i: (0, 0)), vec, vec, vec],
                 (pl.BlockSpec((tm, CW), lambda i: (i, 0)), pl.BlockSpec((tm, CW), lambda i: (i, 0))),
                 scratch=[pltpu.VMEM((CHALO + tm, CW), F32)], side=side)(z, z, z, z, w, wb.reshape(1, CW), g.reshape(1, CW), b.reshape(1, CW))


def _mix_fwd(a, cs, wap4, wcp4, z, tm, side=None):
    m = a.shape[0]
    ns = wap4.shape[2]

    def body(a_ref, cs_ref, wa_ref, wc_ref, ga_ref, gc_ref, ya_ref, yc_ref, mix_ref):
        av, cv = a_ref[...], cs_ref[...]
        for j in range(NCHIP):
            cols = slice(j * ns, (j + 1) * ns)
            ya, yc = _dot(av, wa_ref[j]), _dot(cv, wc_ref[j])
            ya_ref[:, cols] = ya.astype(BF16)
            yc_ref[:, cols] = yc.astype(BF16)
            mix_ref[:, cols] = (_sigmoid(ga_ref[:, cols]) * ya + _sigmoid(gc_ref[:, cols]) * yc).astype(BF16)

    wspec = pl.BlockSpec((NCHIP, AW, ns), lambda i: (0, 0, 0))
    row = lambda i: (i, 0)
    return _call(body, "mix_fwd",
                 (jax.ShapeDtypeStruct((m, D), BF16), jax.ShapeDtypeStruct((m, D), BF16), jax.ShapeDtypeStruct((m, D), BF16)),
                 (m // tm,),
                 [pl.BlockSpec((tm, AW), row), pl.BlockSpec((tm, CW), row), wspec, wspec,
                  pl.BlockSpec((tm, D), lambda i: (i, 0)), pl.BlockSpec((tm, D), lambda i: (i, 1))],
                 (pl.BlockSpec((tm, D), row), pl.BlockSpec((tm, D), row), pl.BlockSpec((tm, D), row)), side=side)(a, cs, wap4, wcp4, z, z)


def _mm_res_ln(name, a, w, res, g, b, tm, side=None):
    m, k = a.shape
    nband = 2 if tm % 32 == 0 else 1

    def rows(lo, hi, a_ref, w_ref, res_ref, g_ref, b_ref, r_ref, h_ref, hb_ref):
        acc = _dot(a_ref[lo:hi, :], w_ref[...])
        yield
        r = ALPHA * res_ref[lo:hi, :] + acc
        r_ref[lo:hi, :] = r
        yield
        xhat, _ = _ln_stats(r)
        yield
        y = xhat * g_ref[...] + b_ref[...]
        h_ref[lo:hi, :] = y
        hb_ref[lo:hi, :] = y.astype(BF16)
        yield

    def body(*refs):
        _interleave([rows(lo, lo + tm // nband, *refs) for lo in range(0, tm, tm // nband)])

    row = lambda i: (i, 0)
    vec = pl.BlockSpec((1, D), lambda i: (0, 0))
    return _call(body, name,
                 (jax.ShapeDtypeStruct((m, D), F32), jax.ShapeDtypeStruct((m, D), F32), jax.ShapeDtypeStruct((m, D), BF16)),
                 (m // tm,),
                 [pl.BlockSpec((tm, k), row), pl.BlockSpec((k, D), lambda i: (0, 0)), pl.BlockSpec((tm, D), row), vec, vec],
                 (pl.BlockSpec((tm, D), row), pl.BlockSpec((tm, D), row), pl.BlockSpec((tm, D), row)), side=side)(a, w, res, g.reshape(1, D), b.reshape(1, D))


RCH = 16


def _lane_groups(width, most=768):
    n = -(-width // most)
    step = -(-width // (128 * n)) * 128
    return [(c, min(c + step, width)) for c in range(0, width, step)]


def _conv3(win, p, r0, c0, c1, w_ref, b_ref):
    blk = win[p, r0:r0 + FHALO + RCH, c0:c1]
    x0, x1, x2 = blk[FHALO:], pltpu.roll(blk, 1, axis=0)[FHALO:], pltpu.roll(blk, 2, axis=0)[FHALO:]
    return b_ref[:, c0:c1] + w_ref[0:1, c0:c1] * x2 + w_ref[1:2, c0:c1] * x1 + w_ref[2:3, c0:c1] * x0


def _loss_grad(y, target, nblk):
    nex = target.shape[0]
    m = y.shape[0]

    def body(y_ref, t_ref, dy_ref, acc_ref):
        bb, j = pl.program_id(0), pl.program_id(1)

        @pl.when((bb == 0) & (j == 0))
        def _():
            acc_ref[...] = jnp.zeros_like(acc_ref)

        @pl.when(j == 0)
        def _():
            dy_ref[...] = jnp.zeros_like(dy_ref)

        @pl.when(j > 0)
        def _():
            e = y_ref[...] - t_ref[...]
            dy_ref[...] = e * (1.0 / D)
            acc_ref[...] += jnp.sum((e * e).reshape(BLK // 8, 8, D), axis=0)

    return _call(body, "loss_grad", (jax.ShapeDtypeStruct((m, D), F32), jax.ShapeDtypeStruct((8, D), F32)), (nex, nblk),
                 [pl.BlockSpec((BLK, D), lambda bb, j: (bb * nblk + j, 0)),
                  pl.BlockSpec((None, BLK, D), lambda bb, j: (bb, jnp.maximum(j - 1, 0), 0))],
                 (pl.BlockSpec((BLK, D), lambda bb, j: (bb * nblk + j, 0)), pl.BlockSpec((8, D), lambda bb, j: (0, 0))))(y, target)


def _ln_bwd_call(name, dy, r, g, tm, a_list=(), w=None, cols=None, side=None):
    m = dy.shape[0]
    na = len(a_list)

    nband = 2 if tm % 32 == 0 else 1

    def body(*refs):
        dy_ref, r_ref, g_ref = refs[0:3]
        a_refs = refs[3:3 + na]
        w_ref = refs[3 + na] if na else None
        dr_ref, drb_ref, dg_ref, db_ref = refs[-4:]
        i = pl.program_id(0)
        sums = []

        def band(lo, hi):
            dh = dy_ref[lo:hi, :]
            if na:
                dh = ALPHA * dh
                if cols is None:
                    ns = w.shape[2]
                    for j in range(NCHIP):
                        dh = dh + _dot_nt(a_refs[0][j // 2, lo:hi, (j % 2) * ns:(j % 2 + 1) * ns], w_ref[j])
                        yield
                else:
                    for a_ref, (c0, c1) in zip(a_refs, cols):
                        dh = dh + _dot_nt(a_ref[lo:hi, :], w_ref[:, c0:c1])
                        yield
            xhat, rstd = _ln_stats(r_ref[lo:hi, :])
            yield
            dr = _ln_bwd(dh, xhat, rstd, g_ref[...])
            yield
            dr_ref[lo:hi, :] = dr
            drb_ref[lo:hi, :] = dr.astype(BF16)
            sums.append((jnp.sum(dh * xhat, axis=0, keepdims=True), jnp.sum(dh, axis=0, keepdims=True)))
            yield

        _interleave([band(lo, lo + tm // nband) for lo in range(0, tm, tm // nband)])

        @pl.when(i == 0)
        def _():
            dg_ref[...] = jnp.zeros_like(dg_ref)
            db_ref[...] = jnp.zeros_like(db_ref)

        for sg, sb_ in sums:
            dg_ref[...] += sg
            db_ref[...] += sb_

    row = lambda i: (i, 0)
    vec = pl.BlockSpec((1, D), lambda i: (0, 0))
    in_specs = [pl.BlockSpec((tm, D), row), pl.BlockSpec((tm, D), row), vec]
    for a in a_list:
        in_specs.append(pl.BlockSpec((2, tm, a.shape[2]), lambda i: (0, i, 0)) if a.ndim == 3 else pl.BlockSpec((tm, a.shape[1]), row))
    if na:
        in_specs.append(pl.BlockSpec(w.shape, (lambda i: (0, 0, 0)) if w.ndim == 3 else (lambda i: (0, 0))))
    return _call(body, name,
                 (jax.ShapeDtypeStruct((m, D), F32), jax.ShapeDtypeStruct((m, D), BF16),
                  jax.ShapeDtypeStruct((1, D), F32), jax.ShapeDtypeStruct((1, D), F32)),
                 (m // tm,), in_specs,
                 (pl.BlockSpec((tm, D), row), pl.BlockSpec((tm, D), row), vec, vec),
                 side=side)(dy, r, g.reshape(1, D), *a_list, *([w] if na else []))


def _ffn_bwd(drb, wdown, ug, up3, w, tm, nex, tp, side=None):
    _, m, dff = ug.shape
    ffs = dff // 2
    nt = m // tm

    def body(dr_ref, wd_ref, ug_ref, x_ref, wu_ref, wg_ref, o_ref, dw_ref, db_ref, dact, carry, dwacc, dbacc):
        i = pl.program_id(1)
        tile = nt - 1 - i
        dact[...] = _dot_nt(dr_ref[...], wd_ref[...])
        dwacc[...] = jnp.zeros_like(dwacc)
        dbacc[...] = jnp.zeros_like(dbacc)

        @pl.when(i == 0)
        def _():
            carry[...] = jnp.zeros_like(carry)
            dw_ref[...] = jnp.zeros_like(dw_ref)
            db_ref[...] = jnp.zeros_like(db_ref)

        fold = lambda t: t[0:8, :] + t[8:16, :]
        for r0 in reversed(range(0, tm, RCH)):
            rows = tile * tm + r0 + lax.broadcasted_iota(jnp.int32, (RCH, 1), 0)
            pad = _pad_rows(rows, nex, tp)
            for c0, c1 in _lane_groups(ffs, 384):
                u = ug_ref[0, r0:r0 + RCH, c0:c1].astype(F32)
                g = ug_ref[1, r0:r0 + RCH, c0:c1].astype(F32)
                da = dact[r0:r0 + RCH, c0:c1]
                cdf = 0.5 * (1.0 + lax.erf(g * (1.0 / math.sqrt(2.0))))
                pdf = jnp.exp(-0.5 * g * g) * (1.0 / math.sqrt(2.0 * math.pi))
                for p, w_ref, d0 in ((0, wu_ref, da * (g * cdf)), (1, wg_ref, da * u * (cdf + g * pdf))):
                    dblk = jnp.concatenate([d0, carry[p, :, c0:c1]], axis=0)
                    d1 = pltpu.roll(dblk, RCH + FHALO - 1, axis=0)[:RCH]
                    d2 = pltpu.roll(dblk, RCH + FHALO - 2, axis=0)[:RCH]
                    carry[p, :, c0:c1] = d0[0:FHALO]
                    dpre = w_ref[2:3, c0:c1] * d0 + w_ref[1:2, c0:c1] * d1 + w_ref[0:1, c0:c1] * d2
                    o_ref[p, r0:r0 + RCH, c0:c1] = jnp.where(pad, 0.0, dpre).astype(BF16)
                    x0 = x_ref[p, r0:r0 + RCH, c0:c1].astype(F32)
                    dwacc[p, 2, :, c0:c1] += fold(d0 * x0)
                    dwacc[p, 1, :, c0:c1] += fold(d1 * x0)
                    dwacc[p, 0, :, c0:c1] += fold(d2 * x0)
                    dbacc[p, :, c0:c1] += fold(d0)
        for p in range(2):
            for k in range(FTAPS):
                dw_ref[p, k:k + 1, :] += jnp.sum(dwacc[p, k], axis=0, keepdims=True)
            db_ref[p] += jnp.sum(dbacc[p], axis=0, keepdims=True)

    wide = pl.BlockSpec((2, tm, ffs), lambda c, i: (0, nt - 1 - i, c))
    return _call(body, "ffn_bwd",
                 (jax.ShapeDtypeStruct((2, m, dff), BF16), jax.ShapeDtypeStruct((2, FTAPS, dff), F32),
                  jax.ShapeDtypeStruct((2, 1, dff), F32)),
                 (2, nt),
                 [pl.BlockSpec((tm, D), lambda c, i: (nt - 1 - i, 0)), pl.BlockSpec((ffs, D), lambda c, i: (c, 0)), wide, wide,
                  pl.BlockSpec((FTAPS, ffs), lambda c, i: (0, c)), pl.BlockSpec((FTAPS, ffs), lambda c, i: (0, c + 2))],
                 (wide, pl.BlockSpec((2, FTAPS, ffs), lambda c, i: (0, 0, c)), pl.BlockSpec((2, 1, ffs), lambda c, i: (0, 0, c))),
                 scratch=[pltpu.VMEM((tm, ffs), F32), pltpu.VMEM((2, FHALO, ffs), F32),
                          pltpu.VMEM((2, FTAPS, 8, ffs), F32), pltpu.VMEM((2, 8, ffs), F32)], side=side)(drb, wdown, ug, up3, w, w)


def _mm_tn(name, a, b, tk, tn, b_cols=None, chip_out=False, side=None):
    m, k = a.shape
    n = b.shape[-1] * (2 if b.ndim == 3 else 1)

    def body(a_ref, b_ref, o_ref):
        o_ref[...] = _dot_tn(a_ref[...], b_ref[...])

    if b.ndim == 3:
        bspec = pl.BlockSpec((None, m, tn), lambda kk, j: (b_cols(j)[0], 0, b_cols(j)[1]))
    else:
        bspec = pl.BlockSpec((m, tn), lambda kk, j: (0, j))
    if chip_out:
        oshape, ospec = (n // tn, k, tn), pl.BlockSpec((None, tk, tn), lambda kk, j: (j, kk, 0))
    else:
        oshape, ospec = (k, n), pl.BlockSpec((tk, tn), lambda kk, j: (kk, j))
    return _call(body, name, jax.ShapeDtypeStruct(oshape, F32), (k // tk, n // tn),
                 [pl.BlockSpec((m, tk), lambda kk, j: (0, kk)), bspec], ospec, side=side)(a, b)


def _gate_bwd(drb, wout, ya, yc, z, tm):
    m = drb.shape[0]

    def body(dr_ref, w_ref, ya_ref, yc_ref, ga_ref, gc_ref, dya_ref, dyc_ref, dz_ref, cs_ref):
        i = pl.program_id(0)
        dmix = _dot_nt(dr_ref[...], w_ref[...])
        sa, sc = _sigmoid(ga_ref[...]), _sigmoid(gc_ref[...])
        dya_ref[...] = (dmix * sa).astype(BF16)
        dyc_ref[...] = (dmix * sc).astype(BF16)
        dga = dmix * ya_ref[...].astype(F32) * sa * (1.0 - sa)
        dgc = dmix * yc_ref[...].astype(F32) * sc * (1.0 - sc)
        dz_ref[:, 0:D] = dga.astype(BF16)
        dz_ref[:, D:2 * D] = dgc.astype(BF16)

        @pl.when(i == 0)
        def _():
            cs_ref[...] = jnp.zeros_like(cs_ref)

        cs_ref[:, 0:D] += jnp.sum(dga, axis=0, keepdims=True)
        cs_ref[:, D:2 * D] += jnp.sum(dgc, axis=0, keepdims=True)

    row = lambda i: (i, 0)
    return _call(body, "gate_bwd",
                 (jax.ShapeDtypeStruct((m, D), BF16), jax.ShapeDtypeStruct((m, D), BF16),
                  jax.ShapeDtypeStruct((m, 2 * D), BF16), jax.ShapeDtypeStruct((1, 2 * D), F32)),
                 (m // tm,),
                 [pl.BlockSpec((tm, D), row), pl.BlockSpec((D, D), lambda i: (0, 0)), pl.BlockSpec((tm, D), row),
                  pl.BlockSpec((tm, D), row), pl.BlockSpec((tm, D), lambda i: (i, 0)), pl.BlockSpec((tm, D), lambda i: (i, 1))],
                 (pl.BlockSpec((tm, D), row), pl.BlockSpec((tm, D), row), pl.BlockSpec((tm, 2 * D), row),
                  pl.BlockSpec((1, 2 * D), lambda i: (0, 0))))(drb, wout, ya, yc, z, z)


def _conv_bwd_a(dyc, wcp4, cc, g, b, tm):
    m = cc.shape[0]
    ns = wcp4.shape[2]

    def body(dy_ref, w_ref, cc_ref, g_ref, b_ref, dcc_ref, dg_ref, db_ref, dwb_ref):
        i = pl.program_id(0)
        dcs = jnp.zeros((tm, CW), F32)
        for j in range(NCHIP):
            dcs = dcs + _dot_nt(dy_ref[:, j * ns:(j + 1) * ns], w_ref[j])
        xhat, rstd = _ln_stats(cc_ref[...])
        cl = xhat * g_ref[...] + b_ref[...]
        sg = _sigmoid(cl)
        dcl = dcs * sg * (1.0 + cl * (1.0 - sg))
        dcc = _ln_bwd(dcl, xhat, rstd, g_ref[...])
        dcc_ref[...] = dcc

        @pl.when(i == 0)
        def _():
            dg_ref[...] = jnp.zeros_like(dg_ref)
            db_ref[...] = jnp.zeros_like(db_ref)
            dwb_ref[...] = jnp.zeros_like(dwb_ref)

        dg_ref[...] += jnp.sum(dcl * xhat, axis=0, keepdims=True)
        db_ref[...] += jnp.sum(dcl, axis=0, keepdims=True)
        dwb_ref[...] += jnp.sum(dcc, axis=0, keepdims=True)

    row = lambda i: (i, 0)
    vec = pl.BlockSpec((1, CW), lambda i: (0, 0))
    v = jax.ShapeDtypeStruct((1, CW), F32)
    return _call(body, "conv_bwd_a", (jax.ShapeDtypeStruct((m, CW), F32), v, v, v), (m // tm,),
                 [pl.BlockSpec((tm, D), row), pl.BlockSpec((NCHIP, CW, ns), lambda i: (0, 0, 0)), pl.BlockSpec((tm, CW), row), vec, vec],
                 (pl.BlockSpec((tm, CW), row), vec, vec, vec))(dyc, wcp4, cc, g.reshape(1, CW), b.reshape(1, CW))


def _conv_bwd_b(dcc, z, w, nex, tp, side=None):
    m = dcc.shape[0]
    tm = BLK
    sub = tm // CHALO
    nt = m // tm
    cvc, cgc = C_CONV // CW, C_CONV // CW + 1

    def body(d_ref, dh_ref, cv_ref, cg_ref, w_ref, dz_ref, dw_ref, cs_ref, dwin, dwacc):
        i = pl.program_id(0)
        rows = i * tm + lax.broadcasted_iota(jnp.int32, (tm, 1), 0)
        dwin[0:tm, :] = d_ref[...]
        dwin[tm:tm + CHALO, :] = jnp.where(i == nt - 1, 0.0, dh_ref[...])

        @pl.when(i == 0)
        def _():
            dwacc[...] = jnp.zeros_like(dwacc)
            cs_ref[...] = jnp.zeros_like(cs_ref)

        fold = lambda t: (t[0:8] + t[8:16]) + (t[16:24] + t[24:32])
        for sb in range(sub):
            lo = sb * CHALO
            pad = _pad_rows(rows[lo:lo + CHALO], nex, tp)
            for c0 in range(0, CW, CLANES):
                cs_ = slice(c0, c0 + CLANES)
                cv = cv_ref[lo:lo + CHALO, cs_]
                sg = _sigmoid(cg_ref[lo:lo + CHALO, cs_])
                cgin = jnp.where(pad, 0.0, cv * sg)
                blk = dwin[lo:lo + 2 * CHALO, cs_]
                acc = jnp.zeros((CHALO, CLANES), F32)
                for b in range(8):
                    rb = _rolled_up(blk, b)
                    for a in range(4):
                        s = 8 * a + b
                        if s <= CTAPS - 1:
                            k = CTAPS - 1 - s
                            sh = rb[8 * a:8 * a + CHALO]
                            acc = acc + w_ref[k:k + 1, cs_] * sh
                            dwacc[k, :, cs_] += fold(sh * cgin)
                dcg = jnp.where(pad, 0.0, acc)
                dcv = dcg * sg
                dgt = dcg * cv * sg * (1.0 - sg)
                dz_ref[lo:lo + CHALO, cs_] = dcv.astype(BF16)
                dz_ref[lo:lo + CHALO, CW + c0:CW + c0 + CLANES] = dgt.astype(BF16)
                cs_ref[:, cs_] += jnp.sum(dcv, axis=0, keepdims=True)
                cs_ref[:, CW + c0:CW + c0 + CLANES] += jnp.sum(dgt, axis=0, keepdims=True)

        @pl.when(i == nt - 1)
        def _():
            for k in range(CHALO):
                dw_ref[k:k + 1, :] = jnp.sum(dwacc[k], axis=0, keepdims=True)

    nxt = lambda i: jnp.minimum((i + 1) * sub, m // CHALO - 1)
    return _call(body, "conv_bwd_b",
                 (jax.ShapeDtypeStruct((m, 2 * CW), BF16), jax.ShapeDtypeStruct((CHALO, CW), F32),
                  jax.ShapeDtypeStruct((1, 2 * CW), F32)),
                 (nt,),
                 [pl.BlockSpec((tm, CW), lambda i: (i, 0)), pl.BlockSpec((CHALO, CW), lambda i: (nxt(i), 0)),
                  pl.BlockSpec((tm, CW), lambda i: (i, cvc)), pl.BlockSpec((tm, CW), lambda i: (i, cgc)),
                  pl.BlockSpec((CTAPS, CW), lambda i: (0, 0))],
                 (pl.BlockSpec((tm, 2 * CW), lambda i: (i, 0)), pl.BlockSpec((CHALO, CW), lambda i: (0, 0)),
                  pl.BlockSpec((1, 2 * CW), lambda i: (0, 0))),
                 scratch=[pltpu.VMEM((tm + CHALO, CW), F32), pltpu.VMEM((CHALO, 8, CW), F32)], side=side)(dcc, dcc, z, z, w)


def _attn_bwd(z, bias, sinks, dya, wap4, nex, nblk, side=None):
    m = z.shape[0]
    tp = nblk * BLK
    ns = wap4.shape[2]
    blk_of = lambda j: nblk - 1 - j
    z3 = z.reshape(nex, tp, z.shape[1])
    dya3 = dya.reshape(nex, tp, D)

    def body(q_ref, cur_ref, prev_ref, meta_ref, bias_ref, sink_ref, dy_ref, w_ref,
             dz_ref, cs_ref, dsk_ref, dbias_ref, ks, vs, carry, macc, dqacc, dkv, okv):
        j = pl.program_id(0)
        n = nblk - 1 - j

        @pl.when(j == 0)
        def _():
            carry[...] = jnp.zeros_like(carry)
            macc[...] = jnp.zeros_like(macc)
            cs_ref[...] = jnp.zeros_like(cs_ref)
            dsk_ref[...] = jnp.zeros_like(dsk_ref)

        @pl.when((j == 0) | (n <= 1))
        def _():
            dbias_ref[...] = jnp.zeros_like(dbias_ref)

        lane = lax.broadcasted_iota(jnp.int32, (1, BLK), 1)
        dsk = jnp.zeros((1, BLK), F32)

        def head(e, h, q, da, out):
            hk = h // GRP
            yield from _head_softmax(q, ks[e, hk], bias_ref, sink_ref, h, out)
            pn = out["p"]
            doh = da[:, h * HD:(h + 1) * HD].astype(BF16)
            yield
            dp = _dot_nt(doh, vs[e, hk])
            yield
            dl = jnp.sum(pn * dp, axis=-1, keepdims=True)
            yield
            ds = pn * (dp - dl)
            yield
            dbias_ref[h] += ds
            out["dsink"] = jnp.sum(-out["sink"] * dl)
            yield
            dsb = (ds * (HD ** -0.5)).astype(BF16)
            yield
            dqacc[e, :, h * HD:(h + 1) * HD] = _dot(dsb, ks[e, hk])
            out["ds"], out["pb"], out["do"] = dsb, pn.astype(BF16), doh
            yield

        def kv_head(e, hk, outs):
            rows = lambda key: jnp.concatenate([outs[hk * GRP + g][key] for g in range(GRP)], axis=0)
            dk = _dot_tn(rows("ds"), rows("q"))
            yield
            dv = _dot_tn(rows("pb"), rows("do"))
            yield
            dkv[e, :, hk * HD:(hk + 1) * HD] = dk
            dkv[e, :, KVW + hk * HD:KVW + (hk + 1) * HD] = dv
            yield

        das, outs = [], [[{} for _ in range(NQ)] for _ in range(nex)]
        for e in range(nex):
            _fill_kv(ks, vs, e, prev_ref, cur_ref, meta_ref)
            da = jnp.zeros((BLK, AW), F32)
            for jj in range(NCHIP):
                da = da + _dot_nt(dy_ref[e, :, jj * ns:(jj + 1) * ns], w_ref[jj])
            das.append(da)
        _interleave([head(e, h, q_ref[e], das[e], outs[e][h]) for e in range(nex) for h in range(NQ)])
        _interleave([kv_head(e, hk, outs[e]) for e in range(nex) for hk in range(NKV)])
        for e in range(nex):
            for h in range(NQ):
                dsk = dsk + jnp.where(lane == h, outs[e][h]["dsink"], 0.0)
            macc[e] += dkv[e, 2 * BLK:2 * BLK + N_META, :]
            okv[e] = dkv[e, BLK:2 * BLK, :] + carry[e]
            carry[e] = dkv[e, 0:BLK, :]

            @pl.when(n == 0)
            def _():
                okv[e, PAD:BLK, :] += macc[e]

            dq = dqacc[e]
            ok = okv[e]
            dz_ref[e, :, 0:AW] = dq.astype(BF16)
            dz_ref[e, :, AW:AW + 2 * KVW] = ok.astype(BF16)
            cs_ref[:, 0:AW] += jnp.sum(dq, axis=0, keepdims=True)
            cs_ref[:, AW:AW + 2 * KVW] += jnp.sum(ok, axis=0, keepdims=True)
        dsk_ref[...] += dsk

    wz = AW + 2 * KVW
    specs = _attn_specs(nex, blk_of) + [
        pl.BlockSpec((nex, BLK, D), lambda j: (0, blk_of(j), 0)),
        pl.BlockSpec((NCHIP, AW, ns), lambda j: (0, 0, 0))]
    res = _call(body, "attn_bwd",
                (jax.ShapeDtypeStruct((nex, tp, wz), BF16), jax.ShapeDtypeStruct((1, wz), F32), jax.ShapeDtypeStruct((1, BLK), F32),
                 jax.ShapeDtypeStruct((1, 3, NQ, BLK, NKEY), F32)),
                (nblk,), specs,
                (pl.BlockSpec((nex, BLK, wz), lambda j: (0, blk_of(j), 0)), pl.BlockSpec((1, wz), lambda j: (0, 0)),
                 pl.BlockSpec((1, BLK), lambda j: (0, 0)),
                 pl.BlockSpec((None, None, NQ, BLK, NKEY), lambda j: (0, jnp.minimum(blk_of(j), 2), 0, 0, 0))),
                scratch=[pltpu.VMEM((nex, NKV, NKEY, HD), BF16), pltpu.VMEM((nex, NKV, NKEY, HD), BF16),
                         pltpu.VMEM((nex, BLK, 2 * KVW), F32), pltpu.VMEM((nex, N_META, 2 * KVW), F32),
                         pltpu.VMEM((nex, BLK, AW), F32), pltpu.VMEM((nex, NKEY, 2 * KVW), F32),
                         pltpu.VMEM((nex, BLK, 2 * KVW), F32)],
                side=side)(z3, z3, z3, z3, bias, sinks.reshape(1, NQ), dya3, wap4)
    return (res[0].reshape(m, wz),) + tuple(res[1:])


def _tile_rows(rows, cols, target_bytes=1 << 20):
    best = None
    for t in range(8, rows + 1, 8):
        if rows % t == 0 and t * cols * 4 <= target_bytes:
            best = t
    return best or rows


def _sum0(name, x):
    n, r, c = x.shape
    tr = _tile_rows(r, c * n)

    def body(x_ref, o_ref):
        acc = x_ref[0]
        for k in range(1, n):
            acc = acc + x_ref[k]
        o_ref[...] = acc

    return _call(body, name, jax.ShapeDtypeStruct((r, c), F32), (r // tr,),
                 [pl.BlockSpec((n, tr, c), lambda i: (0, i, 0))], pl.BlockSpec((tr, c), lambda i: (i, 0)))(x)


def _adamw(name, w, g, mom, vel):
    r, c = w.shape
    tr = _tile_rows(r, c)
    c1 = 1.0 / (1.0 - ADAM_B1 ** ADAM_STEP)
    c2 = 1.0 / (1.0 - ADAM_B2 ** ADAM_STEP)

    def body(w_ref, g_ref, m_ref, v_ref, d_ref, mo_ref, vo_ref):
        gg = g_ref[...]
        mn = ADAM_B1 * m_ref[...] + (1.0 - ADAM_B1) * gg
        vn = ADAM_B2 * v_ref[...] + (1.0 - ADAM_B2) * (gg * gg)
        mo_ref[...] = mn
        vo_ref[...] = vn
        d_ref[...] = -ADAM_LR * ((mn * c1) / (jnp.sqrt(vn * c2) + ADAM_EPS) + ADAM_WD * w_ref[...])

    spec = pl.BlockSpec((tr, c), lambda i: (i, 0))
    o = jax.ShapeDtypeStruct((r, c), F32)
    return _call(body, name, (o, o, o), (r // tr,), [spec] * 4, (spec, spec, spec))(w, g, mom, vel)


def _adamw_matrix(w, joined, own, mom, vel, core):
    _, r, c = w.shape
    h = r // 2
    th = _tile_rows(h, c, 2 << 20)
    nt = h // th
    c1 = 1.0 / (1.0 - ADAM_B1 ** ADAM_STEP)
    c2 = 1.0 / (1.0 - ADAM_B2 ** ADAM_STEP)

    def body(core_ref, w_ref, j_ref, o0_ref, o1_ref, m_ref, v_ref, g_ref, d_ref, mo_ref, vo_ref):
        layer, half = pl.program_id(0), pl.program_id(1)
        mine = jnp.where(layer == 0, o0_ref[...], o1_ref[...])
        gg = jnp.where(half == core_ref[0], mine, j_ref[...])
        g_ref[...] = gg
        mn = ADAM_B1 * m_ref[...] + (1.0 - ADAM_B1) * gg
        vn = ADAM_B2 * v_ref[...] + (1.0 - ADAM_B2) * (gg * gg)
        mo_ref[...] = mn
        vo_ref[...] = vn
        d_ref[...] = -ADAM_LR * ((mn * c1) / (jnp.sqrt(vn * c2) + ADAM_EPS) + ADAM_WD * w_ref[...])

    full = pl.BlockSpec((None, th, c), lambda l, hh, i, cr: (l, hh * nt + i, 0))
    other = pl.BlockSpec((None, th, c), lambda l, hh, i, cr: (l, (1 - cr[0]) * nt + i, 0))
    part = pl.BlockSpec((th, c), lambda l, hh, i, cr: (i, 0))
    o = jax.ShapeDtypeStruct(w.shape, F32)
    return _call(body, "adamw_matrix", (o, o, o, o), (2, 2, nt), [full, other, part, part, full, full],
                 (full, full, full, full), prefetch=1)(core, w, joined, own[0], own[1], mom, vel)


def _place():
    x, y, c = lax.axis_index("x"), lax.axis_index("y"), lax.axis_index("c")
    others = [(1 - x, y), (x, 1 - y), (1 - x, 1 - y)]
    return x, y, c, others


def _gather_job(items):
    nw = len(items)

    def views(s_ref, g_ref, layer, c):
        if layer is None:
            return s_ref, s_ref.at[c], lambda chip, cc: g_ref.at[chip, cc]
        hr = s_ref.shape[1] // 2
        return s_ref.at[layer], s_ref.at[layer, pl.ds(c * hr, hr)], lambda chip, cc: g_ref.at[chip, pl.ds(cc * hr, hr)]

    def copies(s_refs, g_refs, send, recv):
        x, y, c, others = _place()
        chip = 2 * x + y
        firsts, arrive, passed, arrive2 = [], [], [], []
        for w, (_, layer) in enumerate(items):
            whole, src, dst = views(s_refs[w], g_refs[w], layer, c)

            def rc(kk, s, d, to, w=w):
                return pltpu.make_async_remote_copy(src_ref=s, dst_ref=d, send_sem=send.at[w, kk], recv_sem=recv.at[w, kk],
                                                    device_id=to, device_id_type=MESH)
            firsts.append(rc(6, whole, g_refs[w].at[chip], (x, y, 1 - c)))
            arrive2.append(rc(6, whole, g_refs[w].at[chip], (x, y, c)))
            for k, (px, py) in enumerate(others):
                got, got2 = dst(2 * px + py, c), dst(2 * px + py, 1 - c)
                firsts.append(rc(k, src, dst(chip, c), (px, py, c)))
                arrive.append(rc(k, got, got, (x, y, c)))
                passed.append(rc(3 + k, got, got, (x, y, 1 - c)))
                arrive2.append(rc(3 + k, got2, got2, (x, y, c)))
        return firsts, arrive, passed, arrive2

    def start(s_refs, g_refs, send, recv):
        for cp in copies(s_refs, g_refs, send, recv)[0]:
            cp.start()

    def finish(s_refs, g_refs, send, recv):
        firsts, arrive, passed, arrive2 = copies(s_refs, g_refs, send, recv)
        for a, p in zip(arrive, passed):
            a.wait_recv()
            p.start()
        for a in arrive2:
            a.wait_recv()
        for cp in firsts + passed:
            cp.wait_send()

    outs = [jax.ShapeDtypeStruct((NCHIP,) + (s.shape if layer is None else s.shape[1:]), s.dtype) for s, layer in items]
    return _Job([s for s, _ in items], outs, (nw, 7), start, finish)


def _swap_job(grads):
    def copies(d_refs, a_refs, send, recv):
        x, y, c, _ = _place()
        cps = []
        for w in range(len(grads)):
            h = d_refs[w].shape[1] // 2
            cps.append(pltpu.make_async_remote_copy(
                src_ref=d_refs[w].at[:, pl.ds((1 - c) * h, h), :], dst_ref=a_refs[w], send_sem=send.at[w], recv_sem=recv.at[w],
                device_id=(x, y, 1 - c), device_id_type=MESH))
        return cps

    def start(*r):
        for cp in copies(*r):
            cp.start()

    def finish(*r):
        for cp in copies(*r):
            cp.wait()

    outs = [jax.ShapeDtypeStruct((NCHIP, g.shape[1] // 2, g.shape[2]), g.dtype) for g in grads]
    return _Job(list(grads), outs, (len(grads),), start, finish)


def _exchange_job(parts):
    def copies(q_refs, b_refs, send, recv):
        x, y, c, others = _place()
        cps = []
        for w in range(len(parts)):
            for k, (px, py) in enumerate(others):
                cps.append(pltpu.make_async_remote_copy(
                    src_ref=q_refs[w].at[2 * px + py], dst_ref=b_refs[w].at[k], send_sem=send.at[w, k], recv_sem=recv.at[w, k],
                    device_id=(px, py, c), device_id_type=MESH))
        return cps

    def start(*r):
        for cp in copies(*r):
            cp.start()

    def finish(*r):
        for cp in copies(*r):
            cp.wait()

    outs = [jax.ShapeDtypeStruct((3,) + p.shape[1:], p.dtype) for p in parts]
    return _Job(list(parts), outs, (len(parts), 3), start, finish)


def _run_job(name, job):
    n_in, n_out = len(job.ins), len(job.outs)

    def body(*refs):
        ins, outs = refs[:n_in], refs[n_in:n_in + n_out]
        send, recv = refs[n_in + n_out:]
        job.start(ins, outs, send, recv)
        job.finish(ins, outs, send, recv)

    return pl.pallas_call(
        body, name=name, out_shape=tuple(job.outs), in_specs=[_ANY] * n_in, out_specs=tuple([_ANY] * n_out),
        scratch_shapes=[pltpu.SemaphoreType.DMA(job.sems), pltpu.SemaphoreType.DMA(job.sems)])(*job.ins)


def _sibling_join(halves):
    nw = len(halves)

    def body(*refs):
        h_refs = refs[:2 * nw]
        f_refs = refs[2 * nw:3 * nw]
        send, recv = refs[3 * nw:]
        x, y, c, _ = _place()
        cps = []
        for w in range(nw):
            for l in range(2):
                src = h_refs[2 * w + l]
                h = src.shape[0]
                dst = f_refs[w].at[l, pl.ds(c * h, h), :]
                cp = pltpu.make_async_remote_copy(src_ref=src, dst_ref=dst, send_sem=send.at[w, l], recv_sem=recv.at[w, l],
                                                  device_id=(x, y, 1 - c), device_id_type=MESH)
                cp.start()
                cps.append(cp)
        for w in range(nw):
            for l in range(2):
                src = h_refs[2 * w + l]
                h = src.shape[0]
                other = f_refs[w].at[l, pl.ds((1 - c) * h, h), :]
                pltpu.make_async_remote_copy(src_ref=src, dst_ref=other, send_sem=send.at[w, l], recv_sem=recv.at[w, l],
                                             device_id=(x, y, c), device_id_type=MESH).wait_recv()
        for cp in cps:
            cp.wait_send()

    flat = [a for pair in halves for a in pair]
    outs = tuple(jax.ShapeDtypeStruct((2, 2 * pair[0].shape[0], pair[0].shape[1]), F32) for pair in halves)
    return pl.pallas_call(
        body, name="grad_sibling_join", out_shape=outs, in_specs=[_ANY] * (2 * nw), out_specs=tuple([_ANY] * nw),
        scratch_shapes=[pltpu.SemaphoreType.DMA((nw, 2)), pltpu.SemaphoreType.DMA((nw, 2))])(*flat)


def _allgather_small(v):
    r = v.shape[0]

    def body(x_ref, out_ref, send_sems, recv_sems, local_sem):
        x, y, c, chips = _place()
        me, sibling = (x, y, c), (x, y, 1 - c)

        def slab(px, py, pc):
            return out_ref.at[4 * px + 2 * py + pc]

        def copy(k, block, to, src=None):
            return pltpu.make_async_remote_copy(src_ref=slab(*block) if src is None else src, dst_ref=slab(*block),
                                                send_sem=send_sems.at[k], recv_sem=recv_sems.at[k],
                                                device_id=to, device_id_type=MESH)

        mine = pltpu.make_async_copy(x_ref, slab(*me), local_sem)
        mine.start()
        first = [copy(0, me, sibling, src=x_ref)]
        first += [copy(1 + j, me, (*chip, c), src=x_ref) for j, chip in enumerate(chips)]
        for cp in first:
            cp.start()
        passed = [copy(4 + j, (*chip, c), sibling) for j, chip in enumerate(chips)]
        for j, chip in enumerate(chips):
            copy(1 + j, (*chip, c), me).wait_recv()
            passed[j].start()
        copy(0, sibling, me).wait_recv()
        for j, chip in enumerate(chips):
            copy(4 + j, (*chip, 1 - c), me).wait_recv()
        for cp in first + passed:
            cp.wait_send()
        mine.wait()

    return pl.pallas_call(
        body, name="allgather_small", out_shape=jax.ShapeDtypeStruct((8, r, 128), F32),
        in_specs=[pl.BlockSpec(memory_space=pltpu.VMEM)], out_specs=pl.BlockSpec(memory_space=pltpu.VMEM),
        scratch_shapes=[pltpu.SemaphoreType.DMA((7,)), pltpu.SemaphoreType.DMA((7,)), pltpu.SemaphoreType.DMA],
    )(v)


def _add_half(name, d, a, c):
    _, h, cols = a.shape
    nt = 1
    th = h // nt

    def body(c_ref, d_ref, a_ref, o_ref):
        o_ref[...] = (d_ref[...] + a_ref[...]).astype(BF16)

    return _call(body, name, jax.ShapeDtypeStruct(a.shape, BF16), (NCHIP, nt),
                 [pl.BlockSpec((None, th, cols), lambda p, i, cr: (p, cr[0] * nt + i, 0)),
                  pl.BlockSpec((None, th, cols), lambda p, i, cr: (p, i, 0))],
                 pl.BlockSpec((None, th, cols), lambda p, i, cr: (p, i, 0)), prefetch=1)(c, d, a)


def _add_chips(name, d, a, b, where):
    _, h, cols = a.shape
    th = h // 4 if (h % 64 == 0) else h
    nt = h // th

    def body(w_ref, d_ref, a_ref, b_ref, o_ref):
        own = d_ref[...] + a_ref[...]
        o_ref[...] = ((own + b_ref[0].astype(F32)) + b_ref[1].astype(F32)) + b_ref[2].astype(F32)

    return _call(body, name, jax.ShapeDtypeStruct((h, cols), F32), (nt,),
                 [pl.BlockSpec((None, th, cols), lambda i, wr: (wr[0], wr[1] * nt + i, 0)),
                  pl.BlockSpec((None, th, cols), lambda i, wr: (wr[0], i, 0)),
                  pl.BlockSpec((3, th, cols), lambda i, wr: (0, i, 0))],
                 pl.BlockSpec((th, cols), lambda i, wr: (i, 0)), prefetch=1)(where, d, a, b)


def _pack(arrs):
    pieces = []
    for a in arrs:
        f = a.reshape(-1)
        n = -(-f.shape[0] // 1024) * 1024
        pieces.append(jnp.pad(f, (0, n - f.shape[0])).reshape(-1, 128))
    return jnp.concatenate(pieces, axis=0)


def _unpack(buf, shapes):
    out, r = [], 0
    for s in shapes:
        n = int(np.prod(s))
        rows = -(-n // 1024) * 8
        out.append(buf[r:r + rows].reshape(-1)[:n].reshape(s))
        r += rows
    return out


def kernel(x, meta_tokens, in_ln_g, in_ln_b, rel_bias, w_in, b_in, attn_sinks, w_attn_proj, conv_dw, conv_dw_b, conv_ln_g, conv_ln_b, w_conv_proj, w_out, ln1_g, ln1_b, ffn_w_up, ffn_dw, ffn_dw_b, ffn_w_down, ln2_g, ln2_b, loss_target, m_meta_tokens, m_in_ln_g, m_in_ln_b, m_rel_bias, m_w_in, m_b_in, m_attn_sinks, m_w_attn_proj, m_conv_dw, m_conv_dw_b, m_conv_ln_g, m_conv_ln_b, m_w_conv_proj, m_w_out, m_ln1_g, m_ln1_b, m_ffn_w_up, m_ffn_dw, m_ffn_dw_b, m_ffn_w_down, m_ln2_g, m_ln2_b, v_meta_tokens, v_in_ln_g, v_in_ln_b, v_rel_bias, v_w_in, v_b_in, v_attn_sinks, v_w_attn_proj, v_conv_dw, v_conv_dw_b, v_conv_ln_g, v_conv_ln_b, v_w_conv_proj, v_w_out, v_ln1_g, v_ln1_b, v_ffn_w_up, v_ffn_dw, v_ffn_dw_b, v_ffn_w_down, v_ln2_g, v_ln2_b):
    nex, seq, _ = x.shape
    nblk = seq // BLK + 1
    tp = nblk * BLK
    m = nex * tp
    tm = _row_tile(m)
    ffs = ffn_w_up.shape[2]
    dff = 2 * ffs
    cx, cy, cc = lax.axis_index("x"), lax.axis_index("y"), lax.axis_index("c")
    chip = (2 * cx + cy).astype(jnp.int32)
    core = cc.astype(jnp.int32)

    names = ("in", "ap", "cp", "out", "up", "down")
    big = dict(zip(names, [w_in, w_attn_proj, w_conv_proj, w_out, ffn_w_up, ffn_w_down]))
    sb = {k: v.astype(BF16) for k, v in big.items()}
    gathered = {}

    first_items = [(meta_tokens.reshape(2, N_META // 2, -1), None), (conv_dw, None), (ffn_dw, None)]
    g_meta, g_cdw, g_fdw = _run_job("gather_small", _gather_job(first_items))
    meta_full = jnp.transpose(g_meta, (1, 2, 0, 3)).reshape(N_META, D)
    bias_tab = _bias_build(rel_bias)

    fwd_plan = {("embed_ln", 0): [("in", 0)],
                ("in_proj", 0): [("ap", 0), ("cp", 0), ("out", 0)], ("attn_fwd", 0): [("up", 0)],
                ("conv_fwd", 0): [("down", 0)], ("mix_fwd", 0): [("in", 1)],
                ("out_proj_ln", 0): [("ap", 1), ("cp", 1), ("out", 1)], ("ffn_up_act", 0): [("up", 1), ("down", 1)]}

    def fwd(tag, l, fn, *args):
        keys = fwd_plan.get((tag, l))
        if not keys:
            return fn(*args)
        items = [(sb[k], kl) for k, kl in keys]
        res = fn(*args, side=_gather_job(items))
        for key, g in zip(keys, res[-len(keys):]):
            gathered[key] = g
        main = res[:-len(keys)]
        return main[0] if len(main) == 1 else main

    def layer_weights(l):
        return dict(
            win=_w_in_from_chips(gathered[("in", l)]), bin=_to_new(b_in[l]).reshape(1, IN_COLS),
            cdw=jnp.transpose(g_cdw[:, l], (1, 0, 2)).reshape(CTAPS, CW),
            fdw=jnp.transpose(g_fdw[:, l], (1, 0, 2)).reshape(FTAPS, 2 * dff),
            fdwb=ffn_dw_b[l].reshape(1, 2 * dff))

    raw, h, hb = fwd("embed_ln", 0, _embed_ln, x, meta_full, in_ln_g, in_ln_b, nblk)
    saved, lw = [], []
    for l in range(DEPTH):
        p = layer_weights(l)
        z = fwd("in_proj", l, functools.partial(_mm_bias, "in_proj"), hb, p["win"], p["bin"], IN_COLS // 3, tm)
        a = fwd("attn_fwd", l, _attn_fwd, z, bias_tab, attn_sinks[l], nex, nblk)
        ccv, cs = fwd("conv_fwd", l, _conv_fwd, z, p["cdw"], conv_dw_b[l], conv_ln_g[l], conv_ln_b[l], nex, tp)
        p["wap"], p["wcp"] = gathered[("ap", l)], gathered[("cp", l)]
        ya, yc, mixed = fwd("mix_fwd", l, _mix_fwd, a, cs, p["wap"], p["wcp"], z, tm)
        p["wout"] = gathered[("out", l)].reshape(D, D)
        r1, h1, h1b = fwd("out_proj_ln", l, functools.partial(_mm_res_ln, "out_proj_ln"), mixed, p["wout"], h, ln1_g[l], ln1_b[l], tm)
        p["wup"] = gathered[("up", l)]
        up3, ug, act = fwd("ffn_up_act", l, _ffn_up_act, h1b, p["wup"], p["fdw"], p["fdwb"], tm, nex, tp)
        p["wdown"] = gathered[("down", l)].reshape(dff, D)
        r2, h2, h2b = _mm_res_ln("ffn_down_ln", act, p["wdown"], h1, ln2_g[l], ln2_b[l], tm)
        saved.append(dict(hb=hb, z=z, a=a, cc=ccv, cs=cs, ya=ya, yc=yc, mixed=mixed, r1=r1, h1b=h1b, up3=up3, ug=ug, act=act, r2=r2))
        lw.append(p)
        h, hb = h2, h2b

    dy, sq = _loss_grad(h, loss_target, nblk)

    grads, swapped, pair_sums, reduced = {}, {}, {}, {}
    cvec, where = core.reshape(1), jnp.stack([chip, core])
    last = [(k, DEPTH - 1) for k in names]
    bwd_plan = {("ln2_bwd", 0): ("swap", last),
                ("ffn_bwd", 0): ("exch", [("up", 1), ("down", 1)]),
                ("dw_down", 0): ("exch", [("ap", 1), ("cp", 1), ("out", 1)]),
                ("dw_up", 0): ("exch", [("in", 1)]),
                ("ln1_bwd", 0): ("swap", [("down", 0), ("up", 0)]),
                ("conv_bwd_b", 0): ("swap", [("out", 0), ("ap", 0), ("cp", 0)]),
                ("attn_bwd", 0): ("exch", [("down", 0), ("up", 0)]),
                ("dw_in_gates", 0): ("exch", [("out", 0), ("ap", 0), ("cp", 0)]),
                ("in_ln_bwd", 0): ("swap", [("in", 0)])}

    def after(kind, keys, outs):
        for key, o in zip(keys, outs):
            if kind == "swap":
                swapped[key] = o
                pair_sums[key] = _add_half("grad_add_sibling", grads[key], o, cvec)
            else:
                reduced[key] = _add_chips("grad_add_chips", grads[key], swapped[key], o, where)

    def bwd(tag, l, fn, *args):
        plan = bwd_plan.get((tag, l))
        if plan is None:
            return fn(*args)
        kind, keys = plan
        job = _swap_job([grads[k] for k in keys]) if kind == "swap" else _exchange_job([pair_sums[k] for k in keys])
        res = fn(*args, side=job)
        after(kind, keys, res[-len(keys):])
        main = res[:-len(keys)]
        return main[0] if len(main) == 1 else main

    small = {}
    prev_a, prev_w, prev_cols = (), None, None
    dprev = dy
    for l in reversed(range(DEPTH)):
        p, s = lw[l], saved[l]
        dr2, dr2b, dg2, db2 = bwd("ln2_bwd", l, functools.partial(_ln_bwd_call, "ln2_bwd"), dprev, s["r2"], ln2_g[l], tm,
                                  prev_a, prev_w, prev_cols)
        dpre3, dfdw, dfdwb = bwd("ffn_bwd", l, _ffn_bwd, dr2b, p["wdown"], s["ug"], s["up3"], p["fdw"], tm, nex, tp)
        grads[("down", l)] = bwd("dw_down", l, functools.partial(_mm_tn, "dw_down"), s["act"], dr2b, ffs,
                                 D // 2).reshape(NCHIP, dff // NCHIP, D)
        grads[("up", l)] = bwd("dw_up", l, functools.partial(_mm_tn, "dw_up"), s["h1b"], dpre3, D, ffs,
                               lambda j: (j // 2, j % 2), True)
        dr1, dr1b, dg1, db1 = bwd("ln1_bwd", l, functools.partial(_ln_bwd_call, "ln1_bwd"), dr2, s["r1"], ln1_g[l], tm // 2,
                                  (dpre3,), p["wup"])
        dya, dyc, dzg, csg = _gate_bwd(dr1b, p["wout"], s["ya"], s["yc"], s["z"], tm)
        grads[("out", l)] = _mm_tn("dw_out", s["mixed"], dr1b, D, D // 2).reshape(NCHIP, D // NCHIP, D)
        grads[("ap", l)] = _mm_tn("dw_attn_proj", s["a"], dya, AW, D // NCHIP, chip_out=True)
        grads[("cp", l)] = _mm_tn("dw_conv_proj", s["cs"], dyc, CW, D // NCHIP, chip_out=True)
        dcc, dclg, dclb, dcwb = _conv_bwd_a(dyc, p["wcp"], s["cc"], conv_ln_g[l], conv_ln_b[l], tm)
        dzc, dcdw, csc = bwd("conv_bwd_b", l, _conv_bwd_b, dcc, s["z"], p["cdw"], nex, tp)
        dzq, csq, dsk, dbias = bwd("attn_bwd", l, _attn_bwd, s["z"], bias_tab, attn_sinks[l], dya, p["wap"], nex, nblk)
        gin = [bwd("dw_in_gates", l, functools.partial(_mm_tn, "dw_in_gates"), s["hb"], dzg, D, D // 2),
               _mm_tn("dw_in_conv", s["hb"], dzc, D, CW),
               _mm_tn("dw_in_qkv", s["hb"], dzq, D, 2 * KVW)]
        grads[("in", l)] = _w_in_to_chips(gin)
        small[l] = dict(
            b_in=_to_old(jnp.concatenate([csg, csc, csq], axis=1)).reshape(IN_COLS), attn_sinks=dsk[0, :NQ],
            conv_dw=dcdw[:CTAPS], conv_dw_b=dcwb.reshape(CW), conv_ln_g=dclg.reshape(CW), conv_ln_b=dclb.reshape(CW),
            ln1_g=dg1.reshape(D), ln1_b=db1.reshape(D),
            ffn_dw=jnp.transpose(dfdw, (1, 0, 2)).reshape(FTAPS, 2 * dff), ffn_dw_b=jnp.transpose(dfdwb, (1, 0, 2)).reshape(2 * dff),
            ln2_g=dg2.reshape(D), ln2_b=db2.reshape(D), bias=dbias)
        dprev = dr1
        prev_a, prev_w, prev_cols = (dzg, dzc, dzq), p["win"], [(C_GATES, C_CONV), (C_CONV, C_QKV), (C_QKV, IN_COLS)]
    draw, _, dg0, db0 = bwd("in_ln_bwd", 0, functools.partial(_ln_bwd_call, "in_ln_bwd"), dprev, raw, in_ln_g, tm,
                            prev_a, prev_w, prev_cols)
    draw3 = draw.reshape(nex, tp, D)
    grad_x = draw3[:, BLK:, :]
    dmeta = _sum0("meta_grad_sum", draw3[:, PAD:BLK, :])

    names_l = ["b_in", "attn_sinks", "conv_dw", "conv_dw_b", "conv_ln_g", "conv_ln_b", "ln1_g", "ln1_b", "ffn_dw", "ffn_dw_b", "ln2_g", "ln2_b"]
    dbias_all = _bias_grad(jnp.concatenate([small[l]["bias"] for l in range(DEPTH)], axis=0))
    part_list = [sq, dmeta, dg0.reshape(D), db0.reshape(D), dbias_all]
    part_list += [jnp.stack([small[0][n], small[1][n]]) for n in names_l]
    shapes_small = [tuple(a.shape) for a in part_list]
    tot = _sum0("small_grad_sum", _allgather_small(_pack(part_list)))
    (sq_all, g_meta_f, g_inlg, g_inlb, g_biasp, g_bin, g_sinks, g_cdw_f, g_cdwb, g_clg, g_clb, g_l1g, g_l1b, g_fdw_f, g_fdwb,
     g_l2g, g_l2b) = _unpack(tot, shapes_small)
    loss = 0.5 / D * jnp.sum(sq_all)
    g_relb = g_biasp
    csh = D // NCHIP
    g_meta_s = lax.dynamic_slice_in_dim(g_meta_f, chip * csh, csh, axis=1)
    g_cdw_s = lax.dynamic_slice_in_dim(g_cdw_f, chip * (CW // NCHIP), CW // NCHIP, axis=2)
    g_fdw_s = lax.dynamic_slice_in_dim(g_fdw_f, chip * ffs, ffs, axis=2)

    tail = [("in", 0)]
    after("exch", tail, _run_job("grad_chip_exchange", _exchange_job([pair_sums[k] for k in tail])))
    joined = _sibling_join([[reduced[(k, l)] for l in range(DEPTH)] for k in names])

    moms = [m_w_in, m_w_attn_proj, m_w_conv_proj, m_w_out, m_ffn_w_up, m_ffn_w_down]
    vels = [v_w_in, v_w_attn_proj, v_w_conv_proj, v_w_out, v_ffn_w_up, v_ffn_w_down]
    big_out = [_adamw_matrix(big[k], f, [reduced[(k, l)] for l in range(DEPTH)], mo, ve, cvec)
               for k, f, mo, ve in zip(names, joined, moms, vels)]

    sm_w = [meta_tokens, in_ln_g, in_ln_b, rel_bias, b_in, attn_sinks, conv_dw, conv_dw_b, conv_ln_g, conv_ln_b, ln1_g, ln1_b,
            ffn_dw, ffn_dw_b, ln2_g, ln2_b]
    sm_m = [m_meta_tokens, m_in_ln_g, m_in_ln_b, m_rel_bias, m_b_in, m_attn_sinks, m_conv_dw, m_conv_dw_b, m_conv_ln_g, m_conv_ln_b,
            m_ln1_g, m_ln1_b, m_ffn_dw, m_ffn_dw_b, m_ln2_g, m_ln2_b]
    sm_v = [v_meta_tokens, v_in_ln_g, v_in_ln_b, v_rel_bias, v_b_in, v_attn_sinks, v_conv_dw, v_conv_dw_b, v_conv_ln_g, v_conv_ln_b,
            v_ln1_g, v_ln1_b, v_ffn_dw, v_ffn_dw_b, v_ln2_g, v_ln2_b]
    sm_g = [g_meta_s, g_inlg, g_inlb, g_relb, g_bin, g_sinks, g_cdw_s, g_cdwb, g_clg, g_clb, g_l1g, g_l1b, g_fdw_s, g_fdwb, g_l2g, g_l2b]
    sm_shapes = [tuple(a.shape) for a in sm_w]
    sd, smn, svn = _adamw("adamw_small", _pack(sm_w), _pack(sm_g), _pack(sm_m), _pack(sm_v))
    sd, smn, svn = _unpack(sd, sm_shapes), _unpack(smn, sm_shapes), _unpack(svn, sm_shapes)

    order = ["meta_tokens", "in_ln_g", "in_ln_b", "rel_bias", "w_in", "b_in", "attn_sinks", "w_attn_proj", "conv_dw", "conv_dw_b",
             "conv_ln_g", "conv_ln_b", "w_conv_proj", "w_out", "ln1_g", "ln1_b", "ffn_w_up", "ffn_dw", "ffn_dw_b", "ffn_w_down",
             "ln2_g", "ln2_b"]
    small_names = ["meta_tokens", "in_ln_g", "in_ln_b", "rel_bias", "b_in", "attn_sinks", "conv_dw", "conv_dw_b", "conv_ln_g",
                   "conv_ln_b", "ln1_g", "ln1_b", "ffn_dw", "ffn_dw_b", "ln2_g", "ln2_b"]
    big_names = ["w_in", "w_attn_proj", "w_conv_proj", "w_out", "ffn_w_up", "ffn_w_down"]
    res = {}
    for i, n in enumerate(small_names):
        res[n] = (sm_g[i], sd[i], smn[i], svn[i])
    for i, n in enumerate(big_names):
        res[n] = big_out[i]
    outs = [loss, grad_x]
    for k in range(4):
        outs += [res[n][k] for n in order]
    return tuple(outs)
```

```python
import functools
import math
from typing import Any, Callable, NamedTuple, Sequence

import numpy as np
import jax
import jax.numpy as jnp
from jax import lax
from jax.experimental import pallas as pl
from jax.experimental.pallas import tpu as pltpu

F32 = jnp.float32
BF16 = jnp.bfloat16
MESH = pl.DeviceIdType.MESH

D = 1024
N_META = 16
BLK = 128
PAD = BLK - N_META
HD = 64
NQ = 8
NKV = 2
GRP = NQ // NKV
AW = NQ * HD
KVW = NKV * HD
CW = D // 2
CTAPS = 31
FTAPS = 3
NBUCKET = 32
MAXDIST = 128
EPS = 1e-5
DEPTH = 2
ALPHA = (2.0 * DEPTH) ** 0.25
NCHIP = 4
NKEY = 3 * BLK
NEG = -1e30
CHALO = 32
FHALO = 8
IN_COLS = AW + 2 * KVW + 2 * CW + 2 * D
_OLD = dict(q=(0, AW), k=(AW, AW + KVW), v=(AW + KVW, AW + 2 * KVW), cv=(AW + 2 * KVW, AW + 2 * KVW + CW),
            cg=(AW + 2 * KVW + CW, AW + 2 * KVW + 2 * CW), ga=(AW + 2 * KVW + 2 * CW, AW + 2 * KVW + 2 * CW + D),
            gc=(AW + 2 * KVW + 2 * CW + D, IN_COLS))
_NEW_ORDER = ("ga", "gc", "cv", "cg", "q", "k", "v")
C_GATES, C_CONV, C_QKV = 0, 2 * D, 2 * D + 2 * CW

ADAM_LR, ADAM_B1, ADAM_B2, ADAM_EPS, ADAM_WD, ADAM_STEP = 0.001, 0.9, 0.999, 1e-08, 0.01, 10


def _to_new(a):
    return jnp.concatenate([a[..., _OLD[n][0]:_OLD[n][1]] for n in _NEW_ORDER], axis=-1)


def _to_old(a):
    offs, o = {}, 0
    for n in _NEW_ORDER:
        w = _OLD[n][1] - _OLD[n][0]
        offs[n] = (o, o + w)
        o += w
    return jnp.concatenate([a[..., offs[n][0]:offs[n][1]] for n in ("q", "k", "v", "cv", "cg", "ga", "gc")], axis=-1)


def _new_starts():
    starts, o = {}, 0
    for n in _NEW_ORDER:
        starts[n] = o
        o += _OLD[n][1] - _OLD[n][0]
    return starts


def _w_in_from_chips(g4):
    cs = IN_COLS // NCHIP
    pieces = []
    for n in _NEW_ORDER:
        lo, hi = _OLD[n]
        while lo < hi:
            chip = lo // cs
            end = min(hi, (chip + 1) * cs)
            pieces.append(g4[chip][:, lo - chip * cs:end - chip * cs])
            lo = end
    return jnp.concatenate(pieces, axis=1)


def _w_in_to_chips(parts):
    cs = IN_COLS // NCHIP
    starts = _new_starts()
    bounds, o = [], 0
    for p in parts:
        bounds.append((o, o + p.shape[1], p))
        o += p.shape[1]

    def new_cols(a, b):
        out = []
        for s, e, p in bounds:
            lo, hi = max(a, s), min(b, e)
            if lo < hi:
                out.append(p[:, lo - s:hi - s])
        return out

    slabs = []
    for chip in range(NCHIP):
        pieces = []
        for n in ("q", "k", "v", "cv", "cg", "ga", "gc"):
            lo, hi = max(_OLD[n][0], chip * cs), min(_OLD[n][1], (chip + 1) * cs)
            if lo < hi:
                pieces += new_cols(starts[n] + lo - _OLD[n][0], starts[n] + hi - _OLD[n][0])
        slabs.append(jnp.concatenate(pieces, axis=1))
    return jnp.stack(slabs)


class _Job(NamedTuple):
    ins: Sequence[Any]
    outs: Sequence[Any]
    sems: tuple
    start: Callable
    finish: Callable


_ANY = pl.BlockSpec(memory_space=pl.ANY)


def _call(body, name, out_shape, grid, in_specs, out_specs, scratch=(), prefetch=0, side=None):
    params = pltpu.CompilerParams(dimension_semantics=("arbitrary",) * len(grid))
    if side is None:
        if prefetch:
            gs = pltpu.PrefetchScalarGridSpec(num_scalar_prefetch=prefetch, grid=grid, in_specs=in_specs,
                                              out_specs=out_specs, scratch_shapes=list(scratch))
            return pl.pallas_call(body, name=name, out_shape=out_shape, grid_spec=gs, compiler_params=params)
        return pl.pallas_call(body, name=name, out_shape=out_shape, grid=grid, in_specs=in_specs, out_specs=out_specs,
                              scratch_shapes=list(scratch), compiler_params=params)
    assert not prefetch
    single = not isinstance(out_shape, (tuple, list))
    main_shapes = (out_shape,) if single else tuple(out_shape)
    main_specs = (out_specs,) if single else tuple(out_specs)
    n_in, n_sin, n_out, n_sout, n_scr = len(in_specs), len(side.ins), len(main_shapes), len(side.outs), len(scratch)

    def wrapped(*refs):
        main_in, sin = refs[:n_in], refs[n_in:n_in + n_sin]
        o0 = n_in + n_sin
        main_out, sout = refs[o0:o0 + n_out], refs[o0 + n_out:o0 + n_out + n_sout]
        s0 = o0 + n_out + n_sout
        main_scr, (send, recv) = refs[s0:s0 + n_scr], refs[s0 + n_scr:]
        first = functools.reduce(lambda a, b: a & b, [pl.program_id(k) == 0 for k in range(len(grid))])
        last = functools.reduce(lambda a, b: a & b, [pl.program_id(k) == grid[k] - 1 for k in range(len(grid))])

        @pl.when(first)
        def _():
            side.start(sin, sout, send, recv)

        body(*main_in, *main_out, *main_scr)

        @pl.when(last)
        def _():
            side.finish(sin, sout, send, recv)

    call = pl.pallas_call(
        wrapped, name=name, out_shape=main_shapes + tuple(side.outs), grid=grid,
        in_specs=list(in_specs) + [_ANY] * n_sin, out_specs=main_specs + tuple([_ANY] * n_sout),
        scratch_shapes=list(scratch) + [pltpu.SemaphoreType.DMA(side.sems), pltpu.SemaphoreType.DMA(side.sems)],
        compiler_params=params)
    return lambda *args: call(*args, *side.ins)


def _row_tile(m):
    best = 32
    for t in range(32, 641, 32):
        if m % t == 0:
            best = t
    return best


def _pad_rows(rows, nex, tp):
    m = rows < PAD
    for b in range(1, nex):
        m = m | ((rows >= b * tp) & (rows < b * tp + PAD))
    return m


def _ln_stats(x):
    mu = jnp.mean(x, axis=-1, keepdims=True)
    xc = x - mu
    var = jnp.mean(xc * xc, axis=-1, keepdims=True)
    rstd = lax.rsqrt(var + EPS)
    return xc * rstd, rstd


def _ln_bwd(dy, xhat, rstd, g):
    dxh = dy * g
    m1 = jnp.mean(dxh, axis=-1, keepdims=True)
    m2 = jnp.mean(dxh * xhat, axis=-1, keepdims=True)
    return rstd * (dxh - m1 - xhat * m2)


def _dot(a, b):
    return jnp.dot(a, b, preferred_element_type=F32)


def _dot_nt(a, b):
    return lax.dot_general(a, b, (((1,), (1,)), ((), ())), preferred_element_type=F32)


def _dot_tn(a, b):
    return lax.dot_general(a, b, (((0,), (0,)), ((), ())), preferred_element_type=F32)


def _sigmoid(x):
    return 1.0 / (1.0 + jnp.exp(-x))


def _bucket_np(d):
    n = np.maximum(d, 0)
    max_exact = NBUCKET // 2
    nf = np.maximum(n, 1).astype(np.float32)
    large = max_exact + (np.log(nf / np.float32(max_exact)) / np.float32(math.log(MAXDIST / max_exact))
                         * np.float32(NBUCKET - max_exact)).astype(np.int32)
    large = np.minimum(large, NBUCKET - 1)
    return np.where(n < max_exact, n, large).astype(np.int32)


def _bias_index():
    i = np.arange(BLK)[:, None]
    j = np.arange(2 * BLK)[None, :]
    d = BLK + i - j
    band_ok = (d >= 0) & (d < BLK)
    band = _bucket_np(d)
    idx = np.full((3, BLK, NKEY), -1, np.int32)
    m = np.arange(N_META)[None, :]
    d0 = (i - PAD) - m
    idx[0, :, 2 * BLK:2 * BLK + N_META] = np.where(d0 >= 0, _bucket_np(d0), -1)
    ok1 = band_ok & (j >= BLK)
    idx[1, :, :2 * BLK] = np.where(ok1, band, -1)
    idx[1, :, 2 * BLK:2 * BLK + N_META] = _bucket_np((N_META + i) - m)
    idx[2, :, :2 * BLK] = np.where(band_ok, band, -1)
    idx[2, :, 2 * BLK:2 * BLK + N_META] = NBUCKET - 1
    return idx


def _bias_build(rel_bias):
    idx = jnp.asarray(_bias_index())

    def body(idx_ref, rb_ref, o_ref):
        ix = idx_ref[...]
        for h in range(NQ):
            acc = jnp.full(ix.shape, NEG, F32)
            for b in range(NBUCKET):
                acc = jnp.where(ix == b, rb_ref[b, h], acc)
            o_ref[:, h, :, :] = acc

    return pl.pallas_call(
        body, name="bias_build", out_shape=jax.ShapeDtypeStruct((3, NQ, BLK, NKEY), F32),
        in_specs=[pl.BlockSpec(memory_space=pltpu.VMEM), pl.BlockSpec(memory_space=pltpu.SMEM)],
        out_specs=pl.BlockSpec(memory_space=pltpu.VMEM))(idx, rel_bias)


def _bias_grad(dbias):
    idx = jnp.asarray(_bias_index())

    def body(idx_ref, d_ref, o_ref):
        d = jnp.sum(d_ref[...], axis=0)
        for b in range(NBUCKET):
            acc = jnp.zeros((NQ, NKEY), F32)
            for case in range(3):
                hit = (idx_ref[case] == b)[None, :, :]
                acc = acc + jnp.sum(jnp.where(hit, d[case], 0.0), axis=1)
            o_ref[b] = jnp.sum(acc, axis=-1, keepdims=True)

    out = pl.pallas_call(
        body, name="bias_grad", out_shape=jax.ShapeDtypeStruct((NBUCKET, NQ, 1), F32),
        in_specs=[pl.BlockSpec(memory_space=pltpu.VMEM), pl.BlockSpec(memory_space=pltpu.VMEM)],
        out_specs=pl.BlockSpec(memory_space=pltpu.VMEM))(idx, dbias)
    return out.reshape(NBUCKET, NQ)


def _embed_ln(x, meta, g, b, nblk, side=None):
    nex, seq, _ = x.shape
    m = nex * nblk * BLK

    def body(x_ref, meta_ref, g_ref, b_ref, raw_ref, h_ref, hb_ref):
        j = pl.program_id(1)

        @pl.when(j == 0)
        def _():
            raw_ref[0:PAD, :] = jnp.zeros((PAD, D), F32)
            raw_ref[PAD:BLK, :] = meta_ref[...]

        @pl.when(j > 0)
        def _():
            raw_ref[...] = x_ref[...]

        xhat, _ = _ln_stats(raw_ref[...])
        y = xhat * g_ref[...] + b_ref[...]
        h_ref[...] = y
        hb_ref[...] = y.astype(BF16)

    row = lambda bb, j: (bb * nblk + j, 0)
    return _call(
        body, "embed_ln",
        (jax.ShapeDtypeStruct((m, D), F32), jax.ShapeDtypeStruct((m, D), F32), jax.ShapeDtypeStruct((m, D), BF16)),
        (nex, nblk),
        [pl.BlockSpec((None, BLK, D), lambda bb, j: (bb, jnp.maximum(j - 1, 0), 0)),
         pl.BlockSpec((N_META, D), lambda bb, j: (0, 0)),
         pl.BlockSpec((1, D), lambda bb, j: (0, 0)), pl.BlockSpec((1, D), lambda bb, j: (0, 0))],
        (pl.BlockSpec((BLK, D), row), pl.BlockSpec((BLK, D), row), pl.BlockSpec((BLK, D), row)), side=side,
    )(x, meta, g.reshape(1, D), b.reshape(1, D))


def _mm_bias(name, a, w, bias, tn, tm, side=None):
    m, k = a.shape
    n = w.shape[1]

    def body(a_ref, w_ref, b_ref, o_ref):
        o_ref[...] = _dot(a_ref[...], w_ref[...]) + b_ref[...]

    return _call(body, name, jax.ShapeDtypeStruct((m, n), F32), (n // tn, m // tm),
                 [pl.BlockSpec((tm, k), lambda j, i: (i, 0)), pl.BlockSpec((k, tn), lambda j, i: (0, j)),
                  pl.BlockSpec((1, tn), lambda j, i: (0, j))],
                 pl.BlockSpec((tm, tn), lambda j, i: (i, j)), side=side)(a, w, bias)


def _ffn_up_act(a, w4, cw, cb, tm, nex, tp, side=None):
    m, k = a.shape
    ffs = w4.shape[2]

    def body(a_ref, wu_ref, wg_ref, cu_ref, cg_ref, bu_ref, bg_ref, up_ref, ug_ref, act_ref, win):
        i = pl.program_id(1)

        @pl.when(i == 0)
        def _():
            win[:, 0:FHALO, :] = jnp.zeros((2, FHALO, ffs), F32)

        rows = i * tm + lax.broadcasted_iota(jnp.int32, (tm, 1), 0)
        pad = _pad_rows(rows, nex, tp)
        av = a_ref[...]
        for p, w_ref in ((0, wu_ref), (1, wg_ref)):
            x = jnp.where(pad, 0.0, _dot(av, w_ref[...]))
            win[p, FHALO:FHALO + tm, :] = x
            up_ref[p] = x.astype(BF16)
        for r0 in range(0, tm, RCH):
            for c0, c1 in _lane_groups(ffs):
                u = _conv3(win, 0, r0, c0, c1, cu_ref, bu_ref)
                g = _conv3(win, 1, r0, c0, c1, cg_ref, bg_ref)
                ug_ref[0, r0:r0 + RCH, c0:c1] = u.astype(BF16)
                ug_ref[1, r0:r0 + RCH, c0:c1] = g.astype(BF16)
                act_ref[r0:r0 + RCH, c0:c1] = (g * (0.5 * (1.0 + lax.erf(g * (1.0 / math.sqrt(2.0))))) * u).astype(BF16)
        win[:, 0:FHALO, :] = win[:, tm:tm + FHALO, :]

    wide = jax.ShapeDtypeStruct((2, m, 2 * ffs), BF16)
    return _call(body, "ffn_up_act", (wide, wide, jax.ShapeDtypeStruct((m, 2 * ffs), BF16)),
                 (2, m // tm),
                 [pl.BlockSpec((tm, k), lambda c, i: (i, 0)),
                  pl.BlockSpec((None, k, ffs), lambda c, i: (c, 0, 0)), pl.BlockSpec((None, k, ffs), lambda c, i: (c + 2, 0, 0)),
                  pl.BlockSpec((FTAPS, ffs), lambda c, i: (0, c)), pl.BlockSpec((FTAPS, ffs), lambda c, i: (0, c + 2)),
                  pl.BlockSpec((1, ffs), lambda c, i: (0, c)), pl.BlockSpec((1, ffs), lambda c, i: (0, c + 2))],
                 (pl.BlockSpec((2, tm, ffs), lambda c, i: (0, i, c)), pl.BlockSpec((2, tm, ffs), lambda c, i: (0, i, c)),
                  pl.BlockSpec((tm, ffs), lambda c, i: (i, c))),
                 scratch=[pltpu.VMEM((2, FHALO + tm, ffs), F32)], side=side)(a, w4, w4, cw, cw, cb, cb)


def _fill_kv(ks, vs, e, prev_ref, cur_ref, meta_ref):
    for piece, lo, n in ((prev_ref, 0, BLK), (cur_ref, BLK, BLK), (meta_ref, 2 * BLK, N_META)):
        val = piece[e]
        for hk in range(NKV):
            ks[e, hk, lo:lo + n, :] = val[:, hk * HD:(hk + 1) * HD].astype(BF16)
            vs[e, hk, lo:lo + n, :] = val[:, KVW + hk * HD:KVW + (hk + 1) * HD].astype(BF16)
    for hk in range(NKV):
        ks[e, hk, 2 * BLK + N_META:NKEY, :] = jnp.zeros((BLK - N_META, HD), BF16)
        vs[e, hk, 2 * BLK + N_META:NKEY, :] = jnp.zeros((BLK - N_META, HD), BF16)


def _interleave(chains):
    live = list(chains)
    while live:
        for c in list(live):
            try:
                next(c)
            except StopIteration:
                live.remove(c)


def _head_softmax(q, ks_hk, bias_ref, sink_ref, h, out):
    qh = q[:, h * HD:(h + 1) * HD].astype(BF16)
    yield
    s = _dot_nt(qh, ks_hk) * (HD ** -0.5) + bias_ref[h]
    yield
    sink = sink_ref[0, h]
    mx = jnp.maximum(jnp.max(s, axis=-1, keepdims=True), sink)
    yield
    p = jnp.exp(s - mx)
    es = jnp.exp(sink - mx)
    yield
    inv = 1.0 / (jnp.sum(p, axis=-1, keepdims=True) + es)
    yield
    out["q"], out["p"], out["sink"] = qh, p * inv, es * inv
    yield


def _attn_specs(nex, blk_of):
    qcol, kvcol = (C_QKV) // AW, (C_QKV + AW) // (2 * KVW)
    return [
        pl.BlockSpec((nex, BLK, AW), lambda j: (0, blk_of(j), qcol)),
        pl.BlockSpec((nex, BLK, 2 * KVW), lambda j: (0, blk_of(j), kvcol)),
        pl.BlockSpec((nex, BLK, 2 * KVW), lambda j: (0, jnp.maximum(blk_of(j) - 1, 0), kvcol)),
        pl.BlockSpec((nex, N_META, 2 * KVW), lambda j: (0, PAD // N_META, kvcol)),
        pl.BlockSpec((None, NQ, BLK, NKEY), lambda j: (jnp.minimum(blk_of(j), 2), 0, 0, 0)),
        pl.BlockSpec(memory_space=pltpu.SMEM),
    ]


def _attn_fwd(z, bias, sinks, nex, nblk, side=None):
    m = z.shape[0]
    z3 = z.reshape(nex, nblk * BLK, z.shape[1])

    def head(e, h, q, ks, vs, bias_ref, sink_ref, oacc):
        out = {}
        yield from _head_softmax(q, ks[e, h // GRP], bias_ref, sink_ref, h, out)
        o = _dot(out["p"].astype(BF16), vs[e, h // GRP])
        yield
        oacc[e, :, h * HD:(h + 1) * HD] = o
        yield

    def body(q_ref, cur_ref, prev_ref, meta_ref, bias_ref, sink_ref, o_ref, ks, vs, oacc):
        for e in range(nex):
            _fill_kv(ks, vs, e, prev_ref, cur_ref, meta_ref)
        _interleave([head(e, h, q_ref[e], ks, vs, bias_ref, sink_ref, oacc) for e in range(nex) for h in range(NQ)])
        for e in range(nex):
            o_ref[e] = oacc[e].astype(BF16)

    res = _call(body, "attn_fwd", jax.ShapeDtypeStruct((nex, nblk * BLK, AW), BF16), (nblk,),
                _attn_specs(nex, lambda j: j),
                pl.BlockSpec((nex, BLK, AW), lambda j: (0, j, 0)),
                scratch=[pltpu.VMEM((nex, NKV, NKEY, HD), BF16), pltpu.VMEM((nex, NKV, NKEY, HD), BF16),
                         pltpu.VMEM((nex, BLK, AW), F32)], side=side)(z3, z3, z3, z3, bias, sinks.reshape(1, NQ))
    if side is None:
        return res.reshape(m, AW)
    return (res[0].reshape(m, AW),) + tuple(res[1:])


def _cgate(cv, cg, rows, nex, tp):
    return jnp.where(_pad_rows(rows, nex, tp), 0.0, cv * _sigmoid(cg))


CLANES = 256


def _rolled_up(blk, b):
    return blk if b == 0 else pltpu.roll(blk, blk.shape[0] - b, axis=0)


def _conv_fwd(z, w, wb, g, b, nex, tp, side=None):
    m = z.shape[0]
    tm = BLK
    sub = tm // CHALO
    cvc, cgc = C_CONV // CW, C_CONV // CW + 1

    def body(cv_ref, cg_ref, cvh_ref, cgh_ref, w_ref, wb_ref, g_ref, b_ref, cc_ref, cs_ref, win):
        i = pl.program_id(0)
        rows = i * tm + lax.broadcasted_iota(jnp.int32, (tm, 1), 0)
        hrows = i * tm - CHALO + lax.broadcasted_iota(jnp.int32, (CHALO, 1), 0)
        win[0:CHALO, :] = _cgate(cvh_ref[...], cgh_ref[...], hrows, nex, tp)
        win[CHALO:CHALO + tm, :] = _cgate(cv_ref[...], cg_ref[...], rows, nex, tp)
        for sb in range(sub):
            lo = sb * CHALO
            for c0 in range(0, CW, CLANES):
                blk = win[lo:lo + 2 * CHALO, c0:c0 + CLANES]
                acc = jnp.zeros((CHALO, CLANES), F32) + wb_ref[:, c0:c0 + CLANES]
                for b in range(8):
                    rb = _rolled_up(blk, b)
                    for a in range(5):
                        s = 8 * a + b
                        if 2 <= s <= CTAPS + 1:
                            acc = acc + w_ref[s - 2:s - 1, c0:c0 + CLANES] * rb[8 * a:8 * a + CHALO]
                cc_ref[lo:lo + CHALO, c0:c0 + CLANES] = acc
        xhat, _ = _ln_stats(cc_ref[...])
        cl = xhat * g_ref[...] + b_ref[...]
        cs_ref[...] = (cl * _sigmoid(cl)).astype(BF16)

    halo = lambda i: jnp.maximum(i * sub - 1, 0)
    vec = pl.BlockSpec((1, CW), lambda i: (0, 0))
    return _call(body, "conv_fwd", (jax.ShapeDtypeStruct((m, CW), F32), jax.ShapeDtypeStruct((m, CW), BF16)),
                 (m // tm,),
                 [pl.BlockSpec((tm, CW), lambda i: (i, cvc)), pl.BlockSpec((tm, CW), lambda i: (i, cgc)),
                  pl.BlockSpec((CHALO, CW), lambda i: (halo(i), cvc)), pl.BlockSpec((CHALO, CW), lambda i: (halo(i), cgc)),
                  pl.BlockSpec((CTAPS, CW), lambda i: (0, 0)), vec, vec, vec],
                 (pl.BlockSpec((tm, CW), lambda i: (i, 0)), pl.BlockSpec((tm, CW), lambda i: (i, 0))),
                 scratch=[pltpu.VMEM((CHALO + tm, CW), F32)], side=side)(z, z, z, z, w, wb.reshape(1, CW), g.reshape(1, CW), b.reshape(1, CW))


def _mix_fwd(a, cs, wap4, wcp4, z, tm, side=None):
    m = a.shape[0]
    ns = wap4.shape[2]

    def body(a_ref, cs_ref, wa_ref, wc_ref, ga_ref, gc_ref, ya_ref, yc_ref, mix_ref):
        av, cv = a_ref[...], cs_ref[...]
        for j in range(NCHIP):
            cols = slice(j * ns, (j + 1) * ns)
            ya, yc = _dot(av, wa_ref[j]), _dot(cv, wc_ref[j])
            ya_ref[:, cols] = ya.astype(BF16)
            yc_ref[:, cols] = yc.astype(BF16)
            mix_ref[:, cols] = (_sigmoid(ga_ref[:, cols]) * ya + _sigmoid(gc_ref[:, cols]) * yc).astype(BF16)

    wspec = pl.BlockSpec((NCHIP, AW, ns), lambda i: (0, 0, 0))
    row = lambda i: (i, 0)
    return _call(body, "mix_fwd",
                 (jax.ShapeDtypeStruct((m, D), BF16), jax.ShapeDtypeStruct((m, D), BF16), jax.ShapeDtypeStruct((m, D), BF16)),
                 (m // tm,),
                 [pl.BlockSpec((tm, AW), row), pl.BlockSpec((tm, CW), row), wspec, wspec,
                  pl.BlockSpec((tm, D), lambda i: (i, 0)), pl.BlockSpec((tm, D), lambda i: (i, 1))],
                 (pl.BlockSpec((tm, D), row), pl.BlockSpec((tm, D), row), pl.BlockSpec((tm, D), row)), side=side)(a, cs, wap4, wcp4, z, z)


def _mm_res_ln(name, a, w, res, g, b, tm, side=None):
    m, k = a.shape
    nband = 2 if tm % 32 == 0 else 1

    def rows(lo, hi, a_ref, w_ref, res_ref, g_ref, b_ref, r_ref, h_ref, hb_ref):
        acc = _dot(a_ref[lo:hi, :], w_ref[...])
        yield
        r = ALPHA * res_ref[lo:hi, :] + acc
        r_ref[lo:hi, :] = r
        yield
        xhat, _ = _ln_stats(r)
        yield
        y = xhat * g_ref[...] + b_ref[...]
        h_ref[lo:hi, :] = y
        hb_ref[lo:hi, :] = y.astype(BF16)
        yield

    def body(*refs):
        _interleave([rows(lo, lo + tm // nband, *refs) for lo in range(0, tm, tm // nband)])

    row = lambda i: (i, 0)
    vec = pl.BlockSpec((1, D), lambda i: (0, 0))
    return _call(body, name,
                 (jax.ShapeDtypeStruct((m, D), F32), jax.ShapeDtypeStruct((m, D), F32), jax.ShapeDtypeStruct((m, D), BF16)),
                 (m // tm,),
                 [pl.BlockSpec((tm, k), row), pl.BlockSpec((k, D), lambda i: (0, 0)), pl.BlockSpec((tm, D), row), vec, vec],
                 (pl.BlockSpec((tm, D), row), pl.BlockSpec((tm, D), row), pl.BlockSpec((tm, D), row)), side=side)(a, w, res, g.reshape(1, D), b.reshape(1, D))


RCH = 16


def _lane_groups(width, most=768):
    n = -(-width // most)
    step = -(-width // (128 * n)) * 128
    return [(c, min(c + step, width)) for c in range(0, width, step)]


def _conv3(win, p, r0, c0, c1, w_ref, b_ref):
    blk = win[p, r0:r0 + FHALO + RCH, c0:c1]
    x0, x1, x2 = blk[FHALO:], pltpu.roll(blk, 1, axis=0)[FHALO:], pltpu.roll(blk, 2, axis=0)[FHALO:]
    return b_ref[:, c0:c1] + w_ref[0:1, c0:c1] * x2 + w_ref[1:2, c0:c1] * x1 + w_ref[2:3, c0:c1] * x0


def _loss_grad(y, target, nblk):
    nex = target.shape[0]
    m = y.shape[0]

    def body(y_ref, t_ref, dy_ref, acc_ref):
        bb, j = pl.program_id(0), pl.program_id(1)

        @pl.when((bb == 0) & (j == 0))
        def _():
            acc_ref[...] = jnp.zeros_like(acc_ref)

        @pl.when(j == 0)
        def _():
            dy_ref[...] = jnp.zeros_like(dy_ref)

        @pl.when(j > 0)
        def _():
            e = y_ref[...] - t_ref[...]
            dy_ref[...] = e * (1.0 / D)
            acc_ref[...] += jnp.sum((e * e).reshape(BLK // 8, 8, D), axis=0)

    return _call(body, "loss_grad", (jax.ShapeDtypeStruct((m, D), F32), jax.ShapeDtypeStruct((8, D), F32)), (nex, nblk),
                 [pl.BlockSpec((BLK, D), lambda bb, j: (bb * nblk + j, 0)),
                  pl.BlockSpec((None, BLK, D), lambda bb, j: (bb, jnp.maximum(j - 1, 0), 0))],
                 (pl.BlockSpec((BLK, D), lambda bb, j: (bb * nblk + j, 0)), pl.BlockSpec((8, D), lambda bb, j: (0, 0))))(y, target)


def _ln_bwd_call(name, dy, r, g, tm, a_list=(), w=None, cols=None, side=None):
    m = dy.shape[0]
    na = len(a_list)

    nband = 2 if tm % 32 == 0 else 1

    def body(*refs):
        dy_ref, r_ref, g_ref = refs[0:3]
        a_refs = refs[3:3 + na]
        w_ref = refs[3 + na] if na else None
        dr_ref, drb_ref, dg_ref, db_ref = refs[-4:]
        i = pl.program_id(0)
        sums = []

        def band(lo, hi):
            dh = dy_ref[lo:hi, :]
            if na:
                dh = ALPHA * dh
                if cols is None:
                    ns = w.shape[2]
                    for j in range(NCHIP):
                        dh = dh + _dot_nt(a_refs[0][j // 2, lo:hi, (j % 2) * ns:(j % 2 + 1) * ns], w_ref[j])
                        yield
                else:
                    for a_ref, (c0, c1) in zip(a_refs, cols):
                        dh = dh + _dot_nt(a_ref[lo:hi, :], w_ref[:, c0:c1])
                        yield
            xhat, rstd = _ln_stats(r_ref[lo:hi, :])
            yield
            dr = _ln_bwd(dh, xhat, rstd, g_ref[...])
            yield
            dr_ref[lo:hi, :] = dr
            drb_ref[lo:hi, :] = dr.astype(BF16)
            sums.append((jnp.sum(dh * xhat, axis=0, keepdims=True), jnp.sum(dh, axis=0, keepdims=True)))
            yield

        _interleave([band(lo, lo + tm // nband) for lo in range(0, tm, tm // nband)])

        @pl.when(i == 0)
        def _():
            dg_ref[...] = jnp.zeros_like(dg_ref)
            db_ref[...] = jnp.zeros_like(db_ref)

        for sg, sb_ in sums:
            dg_ref[...] += sg
            db_ref[...] += sb_

    row = lambda i: (i, 0)
    vec = pl.BlockSpec((1, D), lambda i: (0, 0))
    in_specs = [pl.BlockSpec((tm, D), row), pl.BlockSpec((tm, D), row), vec]
    for a in a_list:
        in_specs.append(pl.BlockSpec((2, tm, a.shape[2]), lambda i: (0, i, 0)) if a.ndim == 3 else pl.BlockSpec((tm, a.shape[1]), row))
    if na:
        in_specs.append(pl.BlockSpec(w.shape, (lambda i: (0, 0, 0)) if w.ndim == 3 else (lambda i: (0, 0))))
    return _call(body, name,
                 (jax.ShapeDtypeStruct((m, D), F32), jax.ShapeDtypeStruct((m, D), BF16),
                  jax.ShapeDtypeStruct((1, D), F32), jax.ShapeDtypeStruct((1, D), F32)),
                 (m // tm,), in_specs,
                 (pl.BlockSpec((tm, D), row), pl.BlockSpec((tm, D), row), vec, vec),
                 side=side)(dy, r, g.reshape(1, D), *a_list, *([w] if na else []))


def _ffn_bwd(drb, wdown, ug, up3, w, tm, nex, tp, side=None):
    _, m, dff = ug.shape
    ffs = dff // 2
    nt = m // tm

    def body(dr_ref, wd_ref, ug_ref, x_ref, wu_ref, wg_ref, o_ref, dw_ref, db_ref, dact, carry, dwacc, dbacc):
        i = pl.program_id(1)
        tile = nt - 1 - i
        dact[...] = _dot_nt(dr_ref[...], wd_ref[...])
        dwacc[...] = jnp.zeros_like(dwacc)
        dbacc[...] = jnp.zeros_like(dbacc)

        @pl.when(i == 0)
        def _():
            carry[...] = jnp.zeros_like(carry)
            dw_ref[...] = jnp.zeros_like(dw_ref)
            db_ref[...] = jnp.zeros_like(db_ref)

        fold = lambda t: t[0:8, :] + t[8:16, :]
        for r0 in reversed(range(0, tm, RCH)):
            rows = tile * tm + r0 + lax.broadcasted_iota(jnp.int32, (RCH, 1), 0)
            pad = _pad_rows(rows, nex, tp)
            for c0, c1 in _lane_groups(ffs, 384):
                u = ug_ref[0, r0:r0 + RCH, c0:c1].astype(F32)
                g = ug_ref[1, r0:r0 + RCH, c0:c1].astype(F32)
                da = dact[r0:r0 + RCH, c0:c1]
                cdf = 0.5 * (1.0 + lax.erf(g * (1.0 / math.sqrt(2.0))))
                pdf = jnp.exp(-0.5 * g * g) * (1.0 / math.sqrt(2.0 * math.pi))
                for p, w_ref, d0 in ((0, wu_ref, da * (g * cdf)), (1, wg_ref, da * u * (cdf + g * pdf))):
                    dblk = jnp.concatenate([d0, carry[p, :, c0:c1]], axis=0)
                    d1 = pltpu.roll(dblk, RCH + FHALO - 1, axis=0)[:RCH]
                    d2 = pltpu.roll(dblk, RCH + FHALO - 2, axis=0)[:RCH]
                    carry[p, :, c0:c1] = d0[0:FHALO]
                    dpre = w_ref[2:3, c0:c1] * d0 + w_ref[1:2, c0:c1] * d1 + w_ref[0:1, c0:c1] * d2
                    o_ref[p, r0:r0 + RCH, c0:c1] = jnp.where(pad, 0.0, dpre).astype(BF16)
                    x0 = x_ref[p, r0:r0 + RCH, c0:c1].astype(F32)
                    dwacc[p, 2, :, c0:c1] += fold(d0 * x0)
                    dwacc[p, 1, :, c0:c1] += fold(d1 * x0)
                    dwacc[p, 0, :, c0:c1] += fold(d2 * x0)
                    dbacc[p, :, c0:c1] += fold(d0)
        for p in range(2):
            for k in range(FTAPS):
                dw_ref[p, k:k + 1, :] += jnp.sum(dwacc[p, k], axis=0, keepdims=True)
            db_ref[p] += jnp.sum(dbacc[p], axis=0, keepdims=True)

    wide = pl.BlockSpec((2, tm, ffs), lambda c, i: (0, nt - 1 - i, c))
    return _call(body, "ffn_bwd",
                 (jax.ShapeDtypeStruct((2, m, dff), BF16), jax.ShapeDtypeStruct((2, FTAPS, dff), F32),
                  jax.ShapeDtypeStruct((2, 1, dff), F32)),
                 (2, nt),
                 [pl.BlockSpec((tm, D), lambda c, i: (nt - 1 - i, 0)), pl.BlockSpec((ffs, D), lambda c, i: (c, 0)), wide, wide,
                  pl.BlockSpec((FTAPS, ffs), lambda c, i: (0, c)), pl.BlockSpec((FTAPS, ffs), lambda c, i: (0, c + 2))],
                 (wide, pl.BlockSpec((2, FTAPS, ffs), lambda c, i: (0, 0, c)), pl.BlockSpec((2, 1, ffs), lambda c, i: (0, 0, c))),
                 scratch=[pltpu.VMEM((tm, ffs), F32), pltpu.VMEM((2, FHALO, ffs), F32),
                          pltpu.VMEM((2, FTAPS, 8, ffs), F32), pltpu.VMEM((2, 8, ffs), F32)], side=side)(drb, wdown, ug, up3, w, w)


def _mm_tn(name, a, b, tk, tn, b_cols=None, chip_out=False, side=None):
    m, k = a.shape
    n = b.shape[-1] * (2 if b.ndim == 3 else 1)

    def body(a_ref, b_ref, o_ref):
        o_ref[...] = _dot_tn(a_ref[...], b_ref[...])

    if b.ndim == 3:
        bspec = pl.BlockSpec((None, m, tn), lambda kk, j: (b_cols(j)[0], 0, b_cols(j)[1]))
    else:
        bspec = pl.BlockSpec((m, tn), lambda kk, j: (0, j))
    if chip_out:
        oshape, ospec = (n // tn, k, tn), pl.BlockSpec((None, tk, tn), lambda kk, j: (j, kk, 0))
    else:
        oshape, ospec = (k, n), pl.BlockSpec((tk, tn), lambda kk, j: (kk, j))
    return _call(body, name, jax.ShapeDtypeStruct(oshape, F32), (k // tk, n // tn),
                 [pl.BlockSpec((m, tk), lambda kk, j: (0, kk)), bspec], ospec, side=side)(a, b)


def _gate_bwd(drb, wout, ya, yc, z, tm):
    m = drb.shape[0]

    def body(dr_ref, w_ref, ya_ref, yc_ref, ga_ref, gc_ref, dya_ref, dyc_ref, dz_ref, cs_ref):
        i = pl.program_id(0)
        dmix = _dot_nt(dr_ref[...], w_ref[...])
        sa, sc = _sigmoid(ga_ref[...]), _sigmoid(gc_ref[...])
        dya_ref[...] = (dmix * sa).astype(BF16)
        dyc_ref[...] = (dmix * sc).astype(BF16)
        dga = dmix * ya_ref[...].astype(F32) * sa * (1.0 - sa)
        dgc = dmix * yc_ref[...].astype(F32) * sc * (1.0 - sc)
        dz_ref[:, 0:D] = dga.astype(BF16)
        dz_ref[:, D:2 * D] = dgc.astype(BF16)

        @pl.when(i == 0)
        def _():
            cs_ref[...] = jnp.zeros_like(cs_ref)

        cs_ref[:, 0:D] += jnp.sum(dga, axis=0, keepdims=True)
        cs_ref[:, D:2 * D] += jnp.sum(dgc, axis=0, keepdims=True)

    row = lambda i: (i, 0)
    return _call(body, "gate_bwd",
                 (jax.ShapeDtypeStruct((m, D), BF16), jax.ShapeDtypeStruct((m, D), BF16),
                  jax.ShapeDtypeStruct((m, 2 * D), BF16), jax.ShapeDtypeStruct((1, 2 * D), F32)),
                 (m // tm,),
                 [pl.BlockSpec((tm, D), row), pl.BlockSpec((D, D), lambda i: (0, 0)), pl.BlockSpec((tm, D), row),
                  pl.BlockSpec((tm, D), row), pl.BlockSpec((tm, D), lambda i: (i, 0)), pl.BlockSpec((tm, D), lambda i: (i, 1))],
                 (pl.BlockSpec((tm, D), row), pl.BlockSpec((tm, D), row), pl.BlockSpec((tm, 2 * D), row),
                  pl.BlockSpec((1, 2 * D), lambda i: (0, 0))))(drb, wout, ya, yc, z, z)


def _conv_bwd_a(dyc, wcp4, cc, g, b, tm):
    m = cc.shape[0]
    ns = wcp4.shape[2]

    def body(dy_ref, w_ref, cc_ref, g_ref, b_ref, dcc_ref, dg_ref, db_ref, dwb_ref):
        i = pl.program_id(0)
        dcs = jnp.zeros((tm, CW), F32)
        for j in range(NCHIP):
            dcs = dcs + _dot_nt(dy_ref[:, j * ns:(j + 1) * ns], w_ref[j])
        xhat, rstd = _ln_stats(cc_ref[...])
        cl = xhat * g_ref[...] + b_ref[...]
        sg = _sigmoid(cl)
        dcl = dcs * sg * (1.0 + cl * (1.0 - sg))
        dcc = _ln_bwd(dcl, xhat, rstd, g_ref[...])
        dcc_ref[...] = dcc

        @pl.when(i == 0)
        def _():
            dg_ref[...] = jnp.zeros_like(dg_ref)
            db_ref[...] = jnp.zeros_like(db_ref)
            dwb_ref[...] = jnp.zeros_like(dwb_ref)

        dg_ref[...] += jnp.sum(dcl * xhat, axis=0, keepdims=True)
        db_ref[...] += jnp.sum(dcl, axis=0, keepdims=True)
        dwb_ref[...] += jnp.sum(dcc, axis=0, keepdims=True)

    row = lambda i: (i, 0)
    vec = pl.BlockSpec((1, CW), lambda i: (0, 0))
    v = jax.ShapeDtypeStruct((1, CW), F32)
    return _call(body, "conv_bwd_a", (jax.ShapeDtypeStruct((m, CW), F32), v, v, v), (m // tm,),
                 [pl.BlockSpec((tm, D), row), pl.BlockSpec((NCHIP, CW, ns), lambda i: (0, 0, 0)), pl.BlockSpec((tm, CW), row), vec, vec],
                 (pl.BlockSpec((tm, CW), row), vec, vec, vec))(dyc, wcp4, cc, g.reshape(1, CW), b.reshape(1, CW))


def _conv_bwd_b(dcc, z, w, nex, tp, side=None):
    m = dcc.shape[0]
    tm = BLK
    sub = tm // CHALO
    nt = m // tm
    cvc, cgc = C_CONV // CW, C_CONV // CW + 1

    def body(d_ref, dh_ref, cv_ref, cg_ref, w_ref, dz_ref, dw_ref, cs_ref, dwin, dwacc):
        i = pl.program_id(0)
        rows = i * tm + lax.broadcasted_iota(jnp.int32, (tm, 1), 0)
        dwin[0:tm, :] = d_ref[...]
        dwin[tm:tm + CHALO, :] = jnp.where(i == nt - 1, 0.0, dh_ref[...])

        @pl.when(i == 0)
        def _():
            dwacc[...] = jnp.zeros_like(dwacc)
            cs_ref[...] = jnp.zeros_like(cs_ref)

        fold = lambda t: (t[0:8] + t[8:16]) + (t[16:24] + t[24:32])
        for sb in range(sub):
            lo = sb * CHALO
            pad = _pad_rows(rows[lo:lo + CHALO], nex, tp)
            for c0 in range(0, CW, CLANES):
                cs_ = slice(c0, c0 + CLANES)
                cv = cv_ref[lo:lo + CHALO, cs_]
                sg = _sigmoid(cg_ref[lo:lo + CHALO, cs_])
                cgin = jnp.where(pad, 0.0, cv * sg)
                blk = dwin[lo:lo + 2 * CHALO, cs_]
                acc = jnp.zeros((CHALO, CLANES), F32)
                for b in range(8):
                    rb = _rolled_up(blk, b)
                    for a in range(4):
                        s = 8 * a + b
                        if s <= CTAPS - 1:
                            k = CTAPS - 1 - s
                            sh = rb[8 * a:8 * a + CHALO]
                            acc = acc + w_ref[k:k + 1, cs_] * sh
                            dwacc[k, :, cs_] += fold(sh * cgin)
                dcg = jnp.where(pad, 0.0, acc)
                dcv = dcg * sg
                dgt = dcg * cv * sg * (1.0 - sg)
                dz_ref[lo:lo + CHALO, cs_] = dcv.astype(BF16)
                dz_ref[lo:lo + CHALO, CW + c0:CW + c0 + CLANES] = dgt.astype(BF16)
                cs_ref[:, cs_] += jnp.sum(dcv, axis=0, keepdims=True)
                cs_ref[:, CW + c0:CW + c0 + CLANES] += jnp.sum(dgt, axis=0, keepdims=True)

        @pl.when(i == nt - 1)
        def _():
            for k in range(CHALO):
                dw_ref[k:k + 1, :] = jnp.sum(dwacc[k], axis=0, keepdims=True)

    nxt = lambda i: jnp.minimum((i + 1) * sub, m // CHALO - 1)
    return _call(body, "conv_bwd_b",
                 (jax.ShapeDtypeStruct((m, 2 * CW), BF16), jax.ShapeDtypeStruct((CHALO, CW), F32),
                  jax.ShapeDtypeStruct((1, 2 * CW), F32)),
                 (nt,),
                 [pl.BlockSpec((tm, CW), lambda i: (i, 0)), pl.BlockSpec((CHALO, CW), lambda i: (nxt(i), 0)),
                  pl.BlockSpec((tm, CW), lambda i: (i, cvc)), pl.BlockSpec((tm, CW), lambda i: (i, cgc)),
                  pl.BlockSpec((CTAPS, CW), lambda i: (0, 0))],
                 (pl.BlockSpec((tm, 2 * CW), lambda i: (i, 0)), pl.BlockSpec((CHALO, CW), lambda i: (0, 0)),
                  pl.BlockSpec((1, 2 * CW), lambda i: (0, 0))),
                 scratch=[pltpu.VMEM((tm + CHALO, CW), F32), pltpu.VMEM((CHALO, 8, CW), F32)], side=side)(dcc, dcc, z, z, w)


def _attn_bwd(z, bias, sinks, dya, wap4, nex, nblk, side=None):
    m = z.shape[0]
    tp = nblk * BLK
    ns = wap4.shape[2]
    blk_of = lambda j: nblk - 1 - j
    z3 = z.reshape(nex, tp, z.shape[1])
    dya3 = dya.reshape(nex, tp, D)

    def body(q_ref, cur_ref, prev_ref, meta_ref, bias_ref, sink_ref, dy_ref, w_ref,
             dz_ref, cs_ref, dsk_ref, dbias_ref, ks, vs, carry, macc, dqacc, dkv, okv):
        j = pl.program_id(0)
        n = nblk - 1 - j

        @pl.when(j == 0)
        def _():
            carry[...] = jnp.zeros_like(carry)
            macc[...] = jnp.zeros_like(macc)
            cs_ref[...] = jnp.zeros_like(cs_ref)
            dsk_ref[...] = jnp.zeros_like(dsk_ref)

        @pl.when((j == 0) | (n <= 1))
        def _():
            dbias_ref[...] = jnp.zeros_like(dbias_ref)

        lane = lax.broadcasted_iota(jnp.int32, (1, BLK), 1)
        dsk = jnp.zeros((1, BLK), F32)

        def head(e, h, q, da, out):
            hk = h // GRP
            yield from _head_softmax(q, ks[e, hk], bias_ref, sink_ref, h, out)
            pn = out["p"]
            doh = da[:, h * HD:(h + 1) * HD].astype(BF16)
            yield
            dp = _dot_nt(doh, vs[e, hk])
            yield
            dl = jnp.sum(pn * dp, axis=-1, keepdims=True)
            yield
            ds = pn * (dp - dl)
            yield
            dbias_ref[h] += ds
            out["dsink"] = jnp.sum(-out["sink"] * dl)
            yield
            dsb = (ds * (HD ** -0.5)).astype(BF16)
            yield
            dqacc[e, :, h * HD:(h + 1) * HD] = _dot(dsb, ks[e, hk])
            out["ds"], out["pb"], out["do"] = dsb, pn.astype(BF16), doh
            yield

        def kv_head(e, hk, outs):
            rows = lambda key: jnp.concatenate([outs[hk * GRP + g][key] for g in range(GRP)], axis=0)
            dk = _dot_tn(rows("ds"), rows("q"))
            yield
            dv = _dot_tn(rows("pb"), rows("do"))
            yield
            dkv[e, :, hk * HD:(hk + 1) * HD] = dk
            dkv[e, :, KVW + hk * HD:KVW + (hk + 1) * HD] = dv
            yield

        das, outs = [], [[{} for _ in range(NQ)] for _ in range(nex)]
        for e in range(nex):
            _fill_kv(ks, vs, e, prev_ref, cur_ref, meta_ref)
            da = jnp.zeros((BLK, AW), F32)
            for jj in range(NCHIP):
                da = da + _dot_nt(dy_ref[e, :, jj * ns:(jj + 1) * ns], w_ref[jj])
            das.append(da)
        _interleave([head(e, h, q_ref[e], das[e], outs[e][h]) for e in range(nex) for h in range(NQ)])
        _interleave([kv_head(e, hk, outs[e]) for e in range(nex) for hk in range(NKV)])
        for e in range(nex):
            for h in range(NQ):
                dsk = dsk + jnp.where(lane == h, outs[e][h]["dsink"], 0.0)
            macc[e] += dkv[e, 2 * BLK:2 * BLK + N_META, :]
            okv[e] = dkv[e, BLK:2 * BLK, :] + carry[e]
            carry[e] = dkv[e, 0:BLK, :]

            @pl.when(n == 0)
            def _():
                okv[e, PAD:BLK, :] += macc[e]

            dq = dqacc[e]
            ok = okv[e]
            dz_ref[e, :, 0:AW] = dq.astype(BF16)
            dz_ref[e, :, AW:AW + 2 * KVW] = ok.astype(BF16)
            cs_ref[:, 0:AW] += jnp.sum(dq, axis=0, keepdims=True)
            cs_ref[:, AW:AW + 2 * KVW] += jnp.sum(ok, axis=0, keepdims=True)
        dsk_ref[...] += dsk

    wz = AW + 2 * KVW
    specs = _attn_specs(nex, blk_of) + [
        pl.BlockSpec((nex, BLK, D), lambda j: (0, blk_of(j), 0)),
        pl.BlockSpec((NCHIP, AW, ns), lambda j: (0, 0, 0))]
    res = _call(body, "attn_bwd",
                (jax.ShapeDtypeStruct((nex, tp, wz), BF16), jax.ShapeDtypeStruct((1, wz), F32), jax.ShapeDtypeStruct((1, BLK), F32),
                 jax.ShapeDtypeStruct((1, 3, NQ, BLK, NKEY), F32)),
                (nblk,), specs,
                (pl.BlockSpec((nex, BLK, wz), lambda j: (0, blk_of(j), 0)), pl.BlockSpec((1, wz), lambda j: (0, 0)),
                 pl.BlockSpec((1, BLK), lambda j: (0, 0)),
                 pl.BlockSpec((None, None, NQ, BLK, NKEY), lambda j: (0, jnp.minimum(blk_of(j), 2), 0, 0, 0))),
                scratch=[pltpu.VMEM((nex, NKV, NKEY, HD), BF16), pltpu.VMEM((nex, NKV, NKEY, HD), BF16),
                         pltpu.VMEM((nex, BLK, 2 * KVW), F32), pltpu.VMEM((nex, N_META, 2 * KVW), F32),
                         pltpu.VMEM((nex, BLK, AW), F32), pltpu.VMEM((nex, NKEY, 2 * KVW), F32),
                         pltpu.VMEM((nex, BLK, 2 * KVW), F32)],
                side=side)(z3, z3, z3, z3, bias, sinks.reshape(1, NQ), dya3, wap4)
    return (res[0].reshape(m, wz),) + tuple(res[1:])


def _tile_rows(rows, cols, target_bytes=1 << 20):
    best = None
    for t in range(8, rows + 1, 8):
        if rows % t == 0 and t * cols * 4 <= target_bytes:
            best = t
    return best or rows


def _sum0(name, x):
    n, r, c = x.shape
    tr = r if n * r * c * 4 <= (8 << 20) else _tile_rows(r, c * n)

    def body(x_ref, o_ref):
        acc = x_ref[0]
        for k in range(1, n):
            acc = acc + x_ref[k]
        o_ref[...] = acc

    return _call(body, name, jax.ShapeDtypeStruct((r, c), F32), (r // tr,),
                 [pl.BlockSpec((n, tr, c), lambda i: (0, i, 0))], pl.BlockSpec((tr, c), lambda i: (i, 0)))(x)


def _adamw(name, w, g, mom, vel):
    r, c = w.shape
    tr = _tile_rows(r, c)
    c1 = 1.0 / (1.0 - ADAM_B1 ** ADAM_STEP)
    c2 = 1.0 / (1.0 - ADAM_B2 ** ADAM_STEP)

    def body(w_ref, g_ref, m_ref, v_ref, d_ref, mo_ref, vo_ref):
        gg = g_ref[...]
        mn = ADAM_B1 * m_ref[...] + (1.0 - ADAM_B1) * gg
        vn = ADAM_B2 * v_ref[...] + (1.0 - ADAM_B2) * (gg * gg)
        mo_ref[...] = mn
        vo_ref[...] = vn
        d_ref[...] = -ADAM_LR * ((mn * c1) / (jnp.sqrt(vn * c2) + ADAM_EPS) + ADAM_WD * w_ref[...])

    spec = pl.BlockSpec((tr, c), lambda i: (i, 0))
    o = jax.ShapeDtypeStruct((r, c), F32)
    return _call(body, name, (o, o, o), (r // tr,), [spec] * 4, (spec, spec, spec))(w, g, mom, vel)


def _adamw_matrix(w, joined, own, mom, vel, core):
    _, r, c = w.shape
    h = r // 2
    th = _tile_rows(h, c, 2 << 20)
    nt = h // th
    c1 = 1.0 / (1.0 - ADAM_B1 ** ADAM_STEP)
    c2 = 1.0 / (1.0 - ADAM_B2 ** ADAM_STEP)

    def body(core_ref, w_ref, j_ref, o0_ref, o1_ref, m_ref, v_ref, g_ref, d_ref, mo_ref, vo_ref):
        layer, half = pl.program_id(0), pl.program_id(1)
        mine = jnp.where(layer == 0, o0_ref[...], o1_ref[...])
        gg = jnp.where(half == core_ref[0], mine, j_ref[...])
        g_ref[...] = gg
        mn = ADAM_B1 * m_ref[...] + (1.0 - ADAM_B1) * gg
        vn = ADAM_B2 * v_ref[...] + (1.0 - ADAM_B2) * (gg * gg)
        mo_ref[...] = mn
        vo_ref[...] = vn
        d_ref[...] = -ADAM_LR * ((mn * c1) / (jnp.sqrt(vn * c2) + ADAM_EPS) + ADAM_WD * w_ref[...])

    full = pl.BlockSpec((None, th, c), lambda l, hh, i, cr: (l, hh * nt + i, 0))
    other = pl.BlockSpec((None, th, c), lambda l, hh, i, cr: (l, (1 - cr[0]) * nt + i, 0))
    part = pl.BlockSpec((th, c), lambda l, hh, i, cr: (i, 0))
    o = jax.ShapeDtypeStruct(w.shape, F32)
    return _call(body, "adamw_matrix", (o, o, o, o), (2, 2, nt), [full, other, part, part, full, full],
                 (full, full, full, full), prefetch=1)(core, w, joined, own[0], own[1], mom, vel)


def _place():
    x, y, c = lax.axis_index("x"), lax.axis_index("y"), lax.axis_index("c")
    others = [(1 - x, y), (x, 1 - y), (1 - x, 1 - y)]
    return x, y, c, others


def _gather_job(items):
    nw = len(items)

    def views(s_ref, g_ref, layer, c):
        if layer is None:
            return s_ref, s_ref.at[c], lambda chip, cc: g_ref.at[chip, cc]
        hr = s_ref.shape[1] // 2
        return s_ref.at[layer], s_ref.at[layer, pl.ds(c * hr, hr)], lambda chip, cc: g_ref.at[chip, pl.ds(cc * hr, hr)]

    def copies(s_refs, g_refs, send, recv):
        x, y, c, others = _place()
        chip = 2 * x + y
        firsts, arrive, passed, arrive2 = [], [], [], []
        for w, (_, layer) in enumerate(items):
            whole, src, dst = views(s_refs[w], g_refs[w], layer, c)

            def rc(kk, s, d, to, w=w):
                return pltpu.make_async_remote_copy(src_ref=s, dst_ref=d, send_sem=send.at[w, kk], recv_sem=recv.at[w, kk],
                                                    device_id=to, device_id_type=MESH)
            firsts.append(rc(6, whole, g_refs[w].at[chip], (x, y, 1 - c)))
            arrive2.append(rc(6, whole, g_refs[w].at[chip], (x, y, c)))
            for k, (px, py) in enumerate(others):
                got, got2 = dst(2 * px + py, c), dst(2 * px + py, 1 - c)
                firsts.append(rc(k, src, dst(chip, c), (px, py, c)))
                arrive.append(rc(k, got, got, (x, y, c)))
                passed.append(rc(3 + k, got, got, (x, y, 1 - c)))
                arrive2.append(rc(3 + k, got2, got2, (x, y, c)))
        return firsts, arrive, passed, arrive2

    def start(s_refs, g_refs, send, recv):
        for cp in copies(s_refs, g_refs, send, recv)[0]:
            cp.start()

    def finish(s_refs, g_refs, send, recv):
        firsts, arrive, passed, arrive2 = copies(s_refs, g_refs, send, recv)
        for a, p in zip(arrive, passed):
            a.wait_recv()
            p.start()
        for a in arrive2:
            a.wait_recv()
        for cp in firsts + passed:
            cp.wait_send()

    outs = [jax.ShapeDtypeStruct((NCHIP,) + (s.shape if layer is None else s.shape[1:]), s.dtype) for s, layer in items]
    return _Job([s for s, _ in items], outs, (nw, 7), start, finish)


def _swap_job(grads):
    def copies(d_refs, a_refs, send, recv):
        x, y, c, _ = _place()
        cps = []
        for w in range(len(grads)):
            h = d_refs[w].shape[1] // 2
            cps.append(pltpu.make_async_remote_copy(
                src_ref=d_refs[w].at[:, pl.ds((1 - c) * h, h), :], dst_ref=a_refs[w], send_sem=send.at[w], recv_sem=recv.at[w],
                device_id=(x, y, 1 - c), device_id_type=MESH))
        return cps

    def start(*r):
        for cp in copies(*r):
            cp.start()

    def finish(*r):
        for cp in copies(*r):
            cp.wait()

    outs = [jax.ShapeDtypeStruct((NCHIP, g.shape[1] // 2, g.shape[2]), g.dtype) for g in grads]
    return _Job(list(grads), outs, (len(grads),), start, finish)


def _exchange_job(parts):
    def copies(q_refs, b_refs, send, recv):
        x, y, c, others = _place()
        cps = []
        for w in range(len(parts)):
            for k, (px, py) in enumerate(others):
                cps.append(pltpu.make_async_remote_copy(
                    src_ref=q_refs[w].at[2 * px + py], dst_ref=b_refs[w].at[k], send_sem=send.at[w, k], recv_sem=recv.at[w, k],
                    device_id=(px, py, c), device_id_type=MESH))
        return cps

    def start(*r):
        for cp in copies(*r):
            cp.start()

    def finish(*r):
        for cp in copies(*r):
            cp.wait()

    outs = [jax.ShapeDtypeStruct((3,) + p.shape[1:], p.dtype) for p in parts]
    return _Job(list(parts), outs, (len(parts), 3), start, finish)


def _run_job(name, job):
    n_in, n_out = len(job.ins), len(job.outs)

    def body(*refs):
        ins, outs = refs[:n_in], refs[n_in:n_in + n_out]
        send, recv = refs[n_in + n_out:]
        job.start(ins, outs, send, recv)
        job.finish(ins, outs, send, recv)

    return pl.pallas_call(
        body, name=name, out_shape=tuple(job.outs), in_specs=[_ANY] * n_in, out_specs=tuple([_ANY] * n_out),
        scratch_shapes=[pltpu.SemaphoreType.DMA(job.sems), pltpu.SemaphoreType.DMA(job.sems)])(*job.ins)


def _sibling_join(halves):
    nw = len(halves)

    def body(*refs):
        h_refs = refs[:2 * nw]
        f_refs = refs[2 * nw:3 * nw]
        send, recv = refs[3 * nw:]
        x, y, c, _ = _place()
        cps = []
        for w in range(nw):
            for l in range(2):
                src = h_refs[2 * w + l]
                h = src.shape[0]
                dst = f_refs[w].at[l, pl.ds(c * h, h), :]
                cp = pltpu.make_async_remote_copy(src_ref=src, dst_ref=dst, send_sem=send.at[w, l], recv_sem=recv.at[w, l],
                                                  device_id=(x, y, 1 - c), device_id_type=MESH)
                cp.start()
                cps.append(cp)
        for w in range(nw):
            for l in range(2):
                src = h_refs[2 * w + l]
                h = src.shape[0]
                other = f_refs[w].at[l, pl.ds((1 - c) * h, h), :]
                pltpu.make_async_remote_copy(src_ref=src, dst_ref=other, send_sem=send.at[w, l], recv_sem=recv.at[w, l],
                                             device_id=(x, y, c), device_id_type=MESH).wait_recv()
        for cp in cps:
            cp.wait_send()

    flat = [a for pair in halves for a in pair]
    outs = tuple(jax.ShapeDtypeStruct((2, 2 * pair[0].shape[0], pair[0].shape[1]), F32) for pair in halves)
    return pl.pallas_call(
        body, name="grad_sibling_join", out_shape=outs, in_specs=[_ANY] * (2 * nw), out_specs=tuple([_ANY] * nw),
        scratch_shapes=[pltpu.SemaphoreType.DMA((nw, 2)), pltpu.SemaphoreType.DMA((nw, 2))])(*flat)


def _allgather_small(v):
    r = v.shape[0]

    def body(x_ref, out_ref, send_sems, recv_sems, local_sem):
        x, y, c, chips = _place()
        me, sibling = (x, y, c), (x, y, 1 - c)

        def slab(px, py, pc):
            return out_ref.at[4 * px + 2 * py + pc]

        def copy(k, block, to, src=None):
            return pltpu.make_async_remote_copy(src_ref=slab(*block) if src is None else src, dst_ref=slab(*block),
                                                send_sem=send_sems.at[k], recv_sem=recv_sems.at[k],
                                                device_id=to, device_id_type=MESH)

        mine = pltpu.make_async_copy(x_ref, slab(*me), local_sem)
        mine.start()
        first = [copy(0, me, sibling, src=x_ref)]
        first += [copy(1 + j, me, (*chip, c), src=x_ref) for j, chip in enumerate(chips)]
        for cp in first:
            cp.start()
        passed = [copy(4 + j, (*chip, c), sibling) for j, chip in enumerate(chips)]
        for j, chip in enumerate(chips):
            copy(1 + j, (*chip, c), me).wait_recv()
            passed[j].start()
        copy(0, sibling, me).wait_recv()
        for j, chip in enumerate(chips):
            copy(4 + j, (*chip, 1 - c), me).wait_recv()
        for cp in first + passed:
            cp.wait_send()
        mine.wait()

    return pl.pallas_call(
        body, name="allgather_small", out_shape=jax.ShapeDtypeStruct((8, r, 128), F32),
        in_specs=[pl.BlockSpec(memory_space=pltpu.VMEM)], out_specs=pl.BlockSpec(memory_space=pltpu.VMEM),
        scratch_shapes=[pltpu.SemaphoreType.DMA((7,)), pltpu.SemaphoreType.DMA((7,)), pltpu.SemaphoreType.DMA],
    )(v)


def _add_half(name, d, a, c):
    _, h, cols = a.shape
    nt = 1
    th = h // nt

    def body(c_ref, d_ref, a_ref, o_ref):
        o_ref[...] = (d_ref[...] + a_ref[...]).astype(BF16)

    return _call(body, name, jax.ShapeDtypeStruct(a.shape, BF16), (NCHIP, nt),
                 [pl.BlockSpec((None, th, cols), lambda p, i, cr: (p, cr[0] * nt + i, 0)),
                  pl.BlockSpec((None, th, cols), lambda p, i, cr: (p, i, 0))],
                 pl.BlockSpec((None, th, cols), lambda p, i, cr: (p, i, 0)), prefetch=1)(c, d, a)


def _add_chips(name, d, a, b, where):
    _, h, cols = a.shape
    th = h // 4 if (h % 64 == 0) else h
    nt = h // th

    def body(w_ref, d_ref, a_ref, b_ref, o_ref):
        own = d_ref[...] + a_ref[...]
        o_ref[...] = ((own + b_ref[0].astype(F32)) + b_ref[1].astype(F32)) + b_ref[2].astype(F32)

    return _call(body, name, jax.ShapeDtypeStruct((h, cols), F32), (nt,),
                 [pl.BlockSpec((None, th, cols), lambda i, wr: (wr[0], wr[1] * nt + i, 0)),
                  pl.BlockSpec((None, th, cols), lambda i, wr: (wr[0], i, 0)),
                  pl.BlockSpec((3, th, cols), lambda i, wr: (0, i, 0))],
                 pl.BlockSpec((th, cols), lambda i, wr: (i, 0)), prefetch=1)(where, d, a, b)


def _pack(arrs):
    pieces = []
    for a in arrs:
        f = a.reshape(-1)
        n = -(-f.shape[0] // 1024) * 1024
        pieces.append(jnp.pad(f, (0, n - f.shape[0])).reshape(-1, 128))
    return jnp.concatenate(pieces, axis=0)


def _unpack(buf, shapes):
    out, r = [], 0
    for s in shapes:
        n = int(np.prod(s))
        rows = -(-n // 1024) * 8
        out.append(buf[r:r + rows].reshape(-1)[:n].reshape(s))
        r += rows
    return out


def kernel(x, meta_tokens, in_ln_g, in_ln_b, rel_bias, w_in, b_in, attn_sinks, w_attn_proj, conv_dw, conv_dw_b, conv_ln_g, conv_ln_b, w_conv_proj, w_out, ln1_g, ln1_b, ffn_w_up, ffn_dw, ffn_dw_b, ffn_w_down, ln2_g, ln2_b, loss_target, m_meta_tokens, m_in_ln_g, m_in_ln_b, m_rel_bias, m_w_in, m_b_in, m_attn_sinks, m_w_attn_proj, m_conv_dw, m_conv_dw_b, m_conv_ln_g, m_conv_ln_b, m_w_conv_proj, m_w_out, m_ln1_g, m_ln1_b, m_ffn_w_up, m_ffn_dw, m_ffn_dw_b, m_ffn_w_down, m_ln2_g, m_ln2_b, v_meta_tokens, v_in_ln_g, v_in_ln_b, v_rel_bias, v_w_in, v_b_in, v_attn_sinks, v_w_attn_proj, v_conv_dw, v_conv_dw_b, v_conv_ln_g, v_conv_ln_b, v_w_conv_proj, v_w_out, v_ln1_g, v_ln1_b, v_ffn_w_up, v_ffn_dw, v_ffn_dw_b, v_ffn_w_down, v_ln2_g, v_ln2_b):
    nex, seq, _ = x.shape
    nblk = seq // BLK + 1
    tp = nblk * BLK
    m = nex * tp
    tm = _row_tile(m)
    ffs = ffn_w_up.shape[2]
    dff = 2 * ffs
    cx, cy, cc = lax.axis_index("x"), lax.axis_index("y"), lax.axis_index("c")
    chip = (2 * cx + cy).astype(jnp.int32)
    core = cc.astype(jnp.int32)

    names = ("in", "ap", "cp", "out", "up", "down")
    big = dict(zip(names, [w_in, w_attn_proj, w_conv_proj, w_out, ffn_w_up, ffn_w_down]))
    sb = {k: v.astype(BF16) for k, v in big.items()}
    gathered = {}

    first_items = [(meta_tokens.reshape(2, N_META // 2, -1), None), (conv_dw, None), (ffn_dw, None)]
    g_meta, g_cdw, g_fdw = _run_job("gather_small", _gather_job(first_items))
    meta_full = jnp.transpose(g_meta, (1, 2, 0, 3)).reshape(N_META, D)
    bias_tab = _bias_build(rel_bias)

    fwd_plan = {("embed_ln", 0): [("in", 0)],
                ("in_proj", 0): [("ap", 0), ("cp", 0), ("out", 0)], ("attn_fwd", 0): [("up", 0)],
                ("conv_fwd", 0): [("down", 0)], ("mix_fwd", 0): [("in", 1)],
                ("out_proj_ln", 0): [("ap", 1), ("cp", 1), ("out", 1)], ("ffn_up_act", 0): [("up", 1), ("down", 1)]}

    def fwd(tag, l, fn, *args):
        keys = fwd_plan.get((tag, l))
        if not keys:
            return fn(*args)
        items = [(sb[k], kl) for k, kl in keys]
        res = fn(*args, side=_gather_job(items))
        for key, g in zip(keys, res[-len(keys):]):
            gathered[key] = g
        main = res[:-len(keys)]
        return main[0] if len(main) == 1 else main

    def layer_weights(l):
        return dict(
            win=_w_in_from_chips(gathered[("in", l)]), bin=_to_new(b_in[l]).reshape(1, IN_COLS),
            cdw=jnp.transpose(g_cdw[:, l], (1, 0, 2)).reshape(CTAPS, CW),
            fdw=jnp.transpose(g_fdw[:, l], (1, 0, 2)).reshape(FTAPS, 2 * dff),
            fdwb=ffn_dw_b[l].reshape(1, 2 * dff))

    raw, h, hb = fwd("embed_ln", 0, _embed_ln, x, meta_full, in_ln_g, in_ln_b, nblk)
    saved, lw = [], []
    for l in range(DEPTH):
        p = layer_weights(l)
        z = fwd("in_proj", l, functools.partial(_mm_bias, "in_proj"), hb, p["win"], p["bin"], IN_COLS // 3, tm)
        a = fwd("attn_fwd", l, _attn_fwd, z, bias_tab, attn_sinks[l], nex, nblk)
        ccv, cs = fwd("conv_fwd", l, _conv_fwd, z, p["cdw"], conv_dw_b[l], conv_ln_g[l], conv_ln_b[l], nex, tp)
        p["wap"], p["wcp"] = gathered[("ap", l)], gathered[("cp", l)]
        ya, yc, mixed = fwd("mix_fwd", l, _mix_fwd, a, cs, p["wap"], p["wcp"], z, tm)
        p["wout"] = gathered[("out", l)].reshape(D, D)
        r1, h1, h1b = fwd("out_proj_ln", l, functools.partial(_mm_res_ln, "out_proj_ln"), mixed, p["wout"], h, ln1_g[l], ln1_b[l], tm)
        p["wup"] = gathered[("up", l)]
        up3, ug, act = fwd("ffn_up_act", l, _ffn_up_act, h1b, p["wup"], p["fdw"], p["fdwb"], tm, nex, tp)
        p["wdown"] = gathered[("down", l)].reshape(dff, D)
        r2, h2, h2b = _mm_res_ln("ffn_down_ln", act, p["wdown"], h1, ln2_g[l], ln2_b[l], tm)
        saved.append(dict(hb=hb, z=z, a=a, cc=ccv, cs=cs, ya=ya, yc=yc, mixed=mixed, r1=r1, h1b=h1b, up3=up3, ug=ug, act=act, r2=r2))
        lw.append(p)
        h, hb = h2, h2b

    dy, sq = _loss_grad(h, loss_target, nblk)

    grads, swapped, pair_sums, reduced = {}, {}, {}, {}
    cvec, where = core.reshape(1), jnp.stack([chip, core])
    last = [(k, DEPTH - 1) for k in names]
    bwd_plan = {("ln2_bwd", 0): ("swap", last),
                ("ffn_bwd", 0): ("exch", [("up", 1), ("down", 1)]),
                ("dw_down", 0): ("exch", [("ap", 1), ("cp", 1), ("out", 1)]),
                ("dw_up", 0): ("exch", [("in", 1)]),
                ("ln1_bwd", 0): ("swap", [("down", 0), ("up", 0)]),
                ("conv_bwd_b", 0): ("swap", [("out", 0), ("ap", 0), ("cp", 0)]),
                ("attn_bwd", 0): ("exch", [("down", 0), ("up", 0)]),
                ("dw_in_gates", 0): ("exch", [("out", 0), ("ap", 0), ("cp", 0)]),
                ("in_ln_bwd", 0): ("swap", [("in", 0)])}

    def after(kind, keys, outs):
        for key, o in zip(keys, outs):
            if kind == "swap":
                swapped[key] = o
                pair_sums[key] = _add_half("grad_add_sibling", grads[key], o, cvec)
            else:
                reduced[key] = _add_chips("grad_add_chips", grads[key], swapped[key], o, where)

    def bwd(tag, l, fn, *args):
        plan = bwd_plan.get((tag, l))
        if plan is None:
            return fn(*args)
        kind, keys = plan
        job = _swap_job([grads[k] for k in keys]) if kind == "swap" else _exchange_job([pair_sums[k] for k in keys])
        res = fn(*args, side=job)
        after(kind, keys, res[-len(keys):])
        main = res[:-len(keys)]
        return main[0] if len(main) == 1 else main

    small = {}
    prev_a, prev_w, prev_cols = (), None, None
    dprev = dy
    for l in reversed(range(DEPTH)):
        p, s = lw[l], saved[l]
        dr2, dr2b, dg2, db2 = bwd("ln2_bwd", l, functools.partial(_ln_bwd_call, "ln2_bwd"), dprev, s["r2"], ln2_g[l], tm,
                                  prev_a, prev_w, prev_cols)
        dpre3, dfdw, dfdwb = bwd("ffn_bwd", l, _ffn_bwd, dr2b, p["wdown"], s["ug"], s["up3"], p["fdw"], tm, nex, tp)
        grads[("down", l)] = bwd("dw_down", l, functools.partial(_mm_tn, "dw_down"), s["act"], dr2b, ffs,
                                 D // 2).reshape(NCHIP, dff // NCHIP, D)
        grads[("up", l)] = bwd("dw_up", l, functools.partial(_mm_tn, "dw_up"), s["h1b"], dpre3, D, ffs,
                               lambda j: (j // 2, j % 2), True)
        dr1, dr1b, dg1, db1 = bwd("ln1_bwd", l, functools.partial(_ln_bwd_call, "ln1_bwd"), dr2, s["r1"], ln1_g[l], tm // 2,
                                  (dpre3,), p["wup"])
        dya, dyc, dzg, csg = _gate_bwd(dr1b, p["wout"], s["ya"], s["yc"], s["z"], tm)
        grads[("out", l)] = _mm_tn("dw_out", s["mixed"], dr1b, D, D // 2).reshape(NCHIP, D // NCHIP, D)
        grads[("ap", l)] = _mm_tn("dw_attn_proj", s["a"], dya, AW, D // NCHIP, chip_out=True)
        grads[("cp", l)] = _mm_tn("dw_conv_proj", s["cs"], dyc, CW, D // NCHIP, chip_out=True)
        dcc, dclg, dclb, dcwb = _conv_bwd_a(dyc, p["wcp"], s["cc"], conv_ln_g[l], conv_ln_b[l], tm)
        dzc, dcdw, csc = bwd("conv_bwd_b", l, _conv_bwd_b, dcc, s["z"], p["cdw"], nex, tp)
        dzq, csq, dsk, dbias = bwd("attn_bwd", l, _attn_bwd, s["z"], bias_tab, attn_sinks[l], dya, p["wap"], nex, nblk)
        gin = [bwd("dw_in_gates", l, functools.partial(_mm_tn, "dw_in_gates"), s["hb"], dzg, D, D // 2),
               _mm_tn("dw_in_conv", s["hb"], dzc, D, CW),
               _mm_tn("dw_in_qkv", s["hb"], dzq, D, 2 * KVW)]
        grads[("in", l)] = _w_in_to_chips(gin)
        small[l] = dict(
            b_in=_to_old(jnp.concatenate([csg, csc, csq], axis=1)).reshape(IN_COLS), attn_sinks=dsk[0, :NQ],
            conv_dw=dcdw[:CTAPS], conv_dw_b=dcwb.reshape(CW), conv_ln_g=dclg.reshape(CW), conv_ln_b=dclb.reshape(CW),
            ln1_g=dg1.reshape(D), ln1_b=db1.reshape(D),
            ffn_dw=jnp.transpose(dfdw, (1, 0, 2)).reshape(FTAPS, 2 * dff), ffn_dw_b=jnp.transpose(dfdwb, (1, 0, 2)).reshape(2 * dff),
            ln2_g=dg2.reshape(D), ln2_b=db2.reshape(D), bias=dbias)
        dprev = dr1
        prev_a, prev_w, prev_cols = (dzg, dzc, dzq), p["win"], [(C_GATES, C_CONV), (C_CONV, C_QKV), (C_QKV, IN_COLS)]
    draw, _, dg0, db0 = bwd("in_ln_bwd", 0, functools.partial(_ln_bwd_call, "in_ln_bwd"), dprev, raw, in_ln_g, tm,
                            prev_a, prev_w, prev_cols)
    draw3 = draw.reshape(nex, tp, D)
    grad_x = draw3[:, BLK:, :]
    dmeta = _sum0("meta_grad_sum", draw3[:, PAD:BLK, :])

    names_l = ["b_in", "attn_sinks", "conv_dw", "conv_dw_b", "conv_ln_g", "conv_ln_b", "ln1_g", "ln1_b", "ffn_dw", "ffn_dw_b", "ln2_g", "ln2_b"]
    dbias_all = _bias_grad(jnp.concatenate([small[l]["bias"] for l in range(DEPTH)], axis=0))
    part_list = [sq, dmeta, dg0.reshape(D), db0.reshape(D), dbias_all]
    part_list += [jnp.stack([small[0][n], small[1][n]]) for n in names_l]
    shapes_small = [tuple(a.shape) for a in part_list]
    tot = _sum0("small_grad_sum", _allgather_small(_pack(part_list)))
    (sq_all, g_meta_f, g_inlg, g_inlb, g_biasp, g_bin, g_sinks, g_cdw_f, g_cdwb, g_clg, g_clb, g_l1g, g_l1b, g_fdw_f, g_fdwb,
     g_l2g, g_l2b) = _unpack(tot, shapes_small)
    loss = 0.5 / D * jnp.sum(sq_all)
    g_relb = g_biasp
    csh = D // NCHIP
    g_meta_s = lax.dynamic_slice_in_dim(g_meta_f, chip * csh, csh, axis=1)
    g_cdw_s = lax.dynamic_slice_in_dim(g_cdw_f, chip * (CW // NCHIP), CW // NCHIP, axis=2)
    g_fdw_s = lax.dynamic_slice_in_dim(g_fdw_f, chip * ffs, ffs, axis=2)

    tail = [("in", 0)]
    after("exch", tail, _run_job("grad_chip_exchange", _exchange_job([pair_sums[k] for k in tail])))
    joined = _sibling_join([[reduced[(k, l)] for l in range(DEPTH)] for k in names])

    moms = [m_w_in, m_w_attn_proj, m_w_conv_proj, m_w_out, m_ffn_w_up, m_ffn_w_down]
    vels = [v_w_in, v_w_attn_proj, v_w_conv_proj, v_w_out, v_ffn_w_up, v_ffn_w_down]
    big_out = [_adamw_matrix(big[k], f, [reduced[(k, l)] for l in range(DEPTH)], mo, ve, cvec)
               for k, f, mo, ve in zip(names, joined, moms, vels)]

    sm_w = [meta_tokens, in_ln_g, in_ln_b, rel_bias, b_in, attn_sinks, conv_dw, conv_dw_b, conv_ln_g, conv_ln_b, ln1_g, ln1_b,
            ffn_dw, ffn_dw_b, ln2_g, ln2_b]
    sm_m = [m_meta_tokens, m_in_ln_g, m_in_ln_b, m_rel_bias, m_b_in, m_attn_sinks, m_conv_dw, m_conv_dw_b, m_conv_ln_g, m_conv_ln_b,
            m_ln1_g, m_ln1_b, m_ffn_dw, m_ffn_dw_b, m_ln2_g, m_ln2_b]
    sm_v = [v_meta_tokens, v_in_ln_g, v_in_ln_b, v_rel_bias, v_b_in, v_attn_sinks, v_conv_dw, v_conv_dw_b, v_conv_ln_g, v_conv_ln_b,
            v_ln1_g, v_ln1_b, v_ffn_dw, v_ffn_dw_b, v_ln2_g, v_ln2_b]
    sm_g = [g_meta_s, g_inlg, g_inlb, g_relb, g_bin, g_sinks, g_cdw_s, g_cdwb, g_clg, g_clb, g_l1g, g_l1b, g_fdw_s, g_fdwb, g_l2g, g_l2b]
    sm_shapes = [tuple(a.shape) for a in sm_w]
    sd, smn, svn = _adamw("adamw_small", _pack(sm_w), _pack(sm_g), _pack(sm_m), _pack(sm_v))
    sd, smn, svn = _unpack(sd, sm_shapes), _unpack(smn, sm_shapes), _unpack(svn, sm_shapes)

    order = ["meta_tokens", "in_ln_g", "in_ln_b", "rel_bias", "w_in", "b_in", "attn_sinks", "w_attn_proj", "conv_dw", "conv_dw_b",
             "conv_ln_g", "conv_ln_b", "w_conv_proj", "w_out", "ln1_g", "ln1_b", "ffn_w_up", "ffn_dw", "ffn_dw_b", "ffn_w_down",
             "ln2_g", "ln2_b"]
    small_names = ["meta_tokens", "in_ln_g", "in_ln_b", "rel_bias", "b_in", "attn_sinks", "conv_dw", "conv_dw_b", "conv_ln_g",
                   "conv_ln_b", "ln1_g", "ln1_b", "ffn_dw", "ffn_dw_b", "ln2_g", "ln2_b"]
    big_names = ["w_in", "w_attn_proj", "w_conv_proj", "w_out", "ffn_w_up", "ffn_w_down"]
    res = {}
    for i, n in enumerate(small_names):
        res[n] = (sm_g[i], sd[i], smn[i], svn[i])
    for i, n in enumerate(big_names):
        res[n] = big_out[i]
    outs = [loss, grad_x]
    for k in range(4):
        outs += [res[n][k] for n in order]
    return tuple(outs)
```

```python
import functools
import math
from typing import Any, Callable, NamedTuple, Sequence

import numpy as np
import jax
import jax.numpy as jnp
from jax import lax
from jax.experimental import pallas as pl
from jax.experimental.pallas import tpu as pltpu

F32 = jnp.float32
BF16 = jnp.bfloat16
MESH = pl.DeviceIdType.MESH

D = 1024
N_META = 16
BLK = 128
PAD = BLK - N_META
HD = 64
NQ = 8
NKV = 2
GRP = NQ // NKV
AW = NQ * HD
KVW = NKV * HD
CW = D // 2
CTAPS = 31
FTAPS = 3
NBUCKET = 32
MAXDIST = 128
EPS = 1e-5
DEPTH = 2
ALPHA = (2.0 * DEPTH) ** 0.25
NCHIP = 4
NKEY = 3 * BLK
NEG = -1e30
CHALO = 32
FHALO = 8
IN_COLS = AW + 2 * KVW + 2 * CW + 2 * D
_OLD = dict(q=(0, AW), k=(AW, AW + KVW), v=(AW + KVW, AW + 2 * KVW), cv=(AW + 2 * KVW, AW + 2 * KVW + CW),
            cg=(AW + 2 * KVW + CW, AW + 2 * KVW + 2 * CW), ga=(AW + 2 * KVW + 2 * CW, AW + 2 * KVW + 2 * CW + D),
            gc=(AW + 2 * KVW + 2 * CW + D, IN_COLS))
_NEW_ORDER = ("ga", "gc", "cv", "cg", "q", "k", "v")
C_GATES, C_CONV, C_QKV = 0, 2 * D, 2 * D + 2 * CW

ADAM_LR, ADAM_B1, ADAM_B2, ADAM_EPS, ADAM_WD, ADAM_STEP = 0.001, 0.9, 0.999, 1e-08, 0.01, 10


def _to_new(a):
    return jnp.concatenate([a[..., _OLD[n][0]:_OLD[n][1]] for n in _NEW_ORDER], axis=-1)


def _to_old(a):
    offs, o = {}, 0
    for n in _NEW_ORDER:
        w = _OLD[n][1] - _OLD[n][0]
        offs[n] = (o, o + w)
        o += w
    return jnp.concatenate([a[..., offs[n][0]:offs[n][1]] for n in ("q", "k", "v", "cv", "cg", "ga", "gc")], axis=-1)


def _new_starts():
    starts, o = {}, 0
    for n in _NEW_ORDER:
        starts[n] = o
        o += _OLD[n][1] - _OLD[n][0]
    return starts


def _w_in_from_chips(g4):
    cs = IN_COLS // NCHIP
    pieces = []
    for n in _NEW_ORDER:
        lo, hi = _OLD[n]
        while lo < hi:
            chip = lo // cs
            end = min(hi, (chip + 1) * cs)
            pieces.append(g4[chip][:, lo - chip * cs:end - chip * cs])
            lo = end
    return jnp.concatenate(pieces, axis=1)


def _w_in_to_chips(parts):
    cs = IN_COLS // NCHIP
    starts = _new_starts()
    bounds, o = [], 0
    for p in parts:
        bounds.append((o, o + p.shape[1], p))
        o += p.shape[1]

    def new_cols(a, b):
        out = []
        for s, e, p in bounds:
            lo, hi = max(a, s), min(b, e)
            if lo < hi:
                out.append(p[:, lo - s:hi - s])
        return out

    slabs = []
    for chip in range(NCHIP):
        pieces = []
        for n in ("q", "k", "v", "cv", "cg", "ga", "gc"):
            lo, hi = max(_OLD[n][0], chip * cs), min(_OLD[n][1], (chip + 1) * cs)
            if lo < hi:
                pieces += new_cols(starts[n] + lo - _OLD[n][0], starts[n] + hi - _OLD[n][0])
        slabs.append(jnp.concatenate(pieces, axis=1))
    return jnp.stack(slabs)


class _Job(NamedTuple):
    ins: Sequence[Any]
    outs: Sequence[Any]
    sems: tuple
    start: Callable
    finish: Callable


_ANY = pl.BlockSpec(memory_space=pl.ANY)


def _call(body, name, out_shape, grid, in_specs, out_specs, scratch=(), prefetch=0, side=None):
    params = pltpu.CompilerParams(dimension_semantics=("arbitrary",) * len(grid))
    if side is None:
        if prefetch:
            gs = pltpu.PrefetchScalarGridSpec(num_scalar_prefetch=prefetch, grid=grid, in_specs=in_specs,
                                              out_specs=out_specs, scratch_shapes=list(scratch))
            return pl.pallas_call(body, name=name, out_shape=out_shape, grid_spec=gs, compiler_params=params)
        return pl.pallas_call(body, name=name, out_shape=out_shape, grid=grid, in_specs=in_specs, out_specs=out_specs,
                              scratch_shapes=list(scratch), compiler_params=params)
    assert not prefetch
    single = not isinstance(out_shape, (tuple, list))
    main_shapes = (out_shape,) if single else tuple(out_shape)
    main_specs = (out_specs,) if single else tuple(out_specs)
    n_in, n_sin, n_out, n_sout, n_scr = len(in_specs), len(side.ins), len(main_shapes), len(side.outs), len(scratch)

    def wrapped(*refs):
        main_in, sin = refs[:n_in], refs[n_in:n_in + n_sin]
        o0 = n_in + n_sin
        main_out, sout = refs[o0:o0 + n_out], refs[o0 + n_out:o0 + n_out + n_sout]
        s0 = o0 + n_out + n_sout
        main_scr, (send, recv) = refs[s0:s0 + n_scr], refs[s0 + n_scr:]
        first = functools.reduce(lambda a, b: a & b, [pl.program_id(k) == 0 for k in range(len(grid))])
        last = functools.reduce(lambda a, b: a & b, [pl.program_id(k) == grid[k] - 1 for k in range(len(grid))])

        @pl.when(first)
        def _():
            side.start(sin, sout, send, recv)

        body(*main_in, *main_out, *main_scr)

        @pl.when(last)
        def _():
            side.finish(sin, sout, send, recv)

    call = pl.pallas_call(
        wrapped, name=name, out_shape=main_shapes + tuple(side.outs), grid=grid,
        in_specs=list(in_specs) + [_ANY] * n_sin, out_specs=main_specs + tuple([_ANY] * n_sout),
        scratch_shapes=list(scratch) + [pltpu.SemaphoreType.DMA(side.sems), pltpu.SemaphoreType.DMA(side.sems)],
        compiler_params=params)
    return lambda *args: call(*args, *side.ins)


def _row_tile(m):
    best = 32
    for t in range(32, 641, 32):
        if m % t == 0:
            best = t
    return best


def _pad_rows(rows, nex, tp):
    m = rows < PAD
    for b in range(1, nex):
        m = m | ((rows >= b * tp) & (rows < b * tp + PAD))
    return m


def _ln_stats(x):
    mu = jnp.mean(x, axis=-1, keepdims=True)
    xc = x - mu
    var = jnp.mean(xc * xc, axis=-1, keepdims=True)
    rstd = lax.rsqrt(var + EPS)
    return xc * rstd, rstd


def _ln_bwd(dy, xhat, rstd, g):
    dxh = dy * g
    m1 = jnp.mean(dxh, axis=-1, keepdims=True)
    m2 = jnp.mean(dxh * xhat, axis=-1, keepdims=True)
    return rstd * (dxh - m1 - xhat * m2)


def _dot(a, b):
    return jnp.dot(a, b, preferred_element_type=F32)


def _dot_nt(a, b):
    return lax.dot_general(a, b, (((1,), (1,)), ((), ())), preferred_element_type=F32)


def _dot_tn(a, b):
    return lax.dot_general(a, b, (((0,), (0,)), ((), ())), preferred_element_type=F32)


def _sigmoid(x):
    return 1.0 / (1.0 + jnp.exp(-x))


def _bucket_np(d):
    n = np.maximum(d, 0)
    max_exact = NBUCKET // 2
    nf = np.maximum(n, 1).astype(np.float32)
    large = max_exact + (np.log(nf / np.float32(max_exact)) / np.float32(math.log(MAXDIST / max_exact))
                         * np.float32(NBUCKET - max_exact)).astype(np.int32)
    large = np.minimum(large, NBUCKET - 1)
    return np.where(n < max_exact, n, large).astype(np.int32)


def _bias_index():
    i = np.arange(BLK)[:, None]
    j = np.arange(2 * BLK)[None, :]
    d = BLK + i - j
    band_ok = (d >= 0) & (d < BLK)
    band = _bucket_np(d)
    idx = np.full((3, BLK, NKEY), -1, np.int32)
    m = np.arange(N_META)[None, :]
    d0 = (i - PAD) - m
    idx[0, :, 2 * BLK:2 * BLK + N_META] = np.where(d0 >= 0, _bucket_np(d0), -1)
    ok1 = band_ok & (j >= BLK)
    idx[1, :, :2 * BLK] = np.where(ok1, band, -1)
    idx[1, :, 2 * BLK:2 * BLK + N_META] = _bucket_np((N_META + i) - m)
    idx[2, :, :2 * BLK] = np.where(band_ok, band, -1)
    idx[2, :, 2 * BLK:2 * BLK + N_META] = NBUCKET - 1
    return idx


def _bias_build(rel_bias):
    idx = jnp.asarray(_bias_index())

    def body(idx_ref, rb_ref, o_ref):
        ix = idx_ref[...]
        for h in range(NQ):
            acc = jnp.full(ix.shape, NEG, F32)
            for b in range(NBUCKET):
                acc = jnp.where(ix == b, rb_ref[b, h], acc)
            o_ref[:, h, :, :] = acc

    return pl.pallas_call(
        body, name="bias_build", out_shape=jax.ShapeDtypeStruct((3, NQ, BLK, NKEY), F32),
        in_specs=[pl.BlockSpec(memory_space=pltpu.VMEM), pl.BlockSpec(memory_space=pltpu.SMEM)],
        out_specs=pl.BlockSpec(memory_space=pltpu.VMEM))(idx, rel_bias)


def _bias_grad(dbias, side=None):
    idx = jnp.asarray(_bias_index())

    def body(idx_ref, d_ref, o_ref):
        d = jnp.sum(d_ref[...], axis=0)
        for b in range(NBUCKET):
            acc = jnp.zeros((NQ, NKEY), F32)
            for case in range(3):
                hit = (idx_ref[case] == b)[None, :, :]
                acc = acc + jnp.sum(jnp.where(hit, d[case], 0.0), axis=1)
            o_ref[b] = jnp.sum(acc, axis=-1, keepdims=True)

    res = _call(body, "bias_grad", jax.ShapeDtypeStruct((NBUCKET, NQ, 1), F32), (1,),
                [pl.BlockSpec(idx.shape, lambda i: (0, 0, 0)), pl.BlockSpec(dbias.shape, lambda i: (0, 0, 0, 0, 0))],
                pl.BlockSpec((NBUCKET, NQ, 1), lambda i: (0, 0, 0)), side=side)(idx, dbias)
    if side is None:
        return res.reshape(NBUCKET, NQ)
    return (res[0].reshape(NBUCKET, NQ),) + tuple(res[1:])


def _embed_ln(x, meta, g, b, nblk, side=None):
    nex, seq, _ = x.shape
    m = nex * nblk * BLK

    def body(x_ref, meta_ref, g_ref, b_ref, raw_ref, h_ref, hb_ref):
        j = pl.program_id(1)

        @pl.when(j == 0)
        def _():
            raw_ref[0:PAD, :] = jnp.zeros((PAD, D), F32)
            raw_ref[PAD:BLK, :] = meta_ref[...]

        @pl.when(j > 0)
        def _():
            raw_ref[...] = x_ref[...]

        xhat, _ = _ln_stats(raw_ref[...])
        y = xhat * g_ref[...] + b_ref[...]
        h_ref[...] = y
        hb_ref[...] = y.astype(BF16)

    row = lambda bb, j: (bb * nblk + j, 0)
    return _call(
        body, "embed_ln",
        (jax.ShapeDtypeStruct((m, D), F32), jax.ShapeDtypeStruct((m, D), F32), jax.ShapeDtypeStruct((m, D), BF16)),
        (nex, nblk),
        [pl.BlockSpec((None, BLK, D), lambda bb, j: (bb, jnp.maximum(j - 1, 0), 0)),
         pl.BlockSpec((N_META, D), lambda bb, j: (0, 0)),
         pl.BlockSpec((1, D), lambda bb, j: (0, 0)), pl.BlockSpec((1, D), lambda bb, j: (0, 0))],
        (pl.BlockSpec((BLK, D), row), pl.BlockSpec((BLK, D), row), pl.BlockSpec((BLK, D), row)), side=side,
    )(x, meta, g.reshape(1, D), b.reshape(1, D))


def _mm_bias(name, a, w, bias, tn, tm, side=None):
    m, k = a.shape
    n = w.shape[1]

    def body(a_ref, w_ref, b_ref, o_ref):
        o_ref[...] = _dot(a_ref[...], w_ref[...]) + b_ref[...]

    return _call(body, name, jax.ShapeDtypeStruct((m, n), F32), (n // tn, m // tm),
                 [pl.BlockSpec((tm, k), lambda j, i: (i, 0)), pl.BlockSpec((k, tn), lambda j, i: (0, j)),
                  pl.BlockSpec((1, tn), lambda j, i: (0, j))],
                 pl.BlockSpec((tm, tn), lambda j, i: (i, j)), side=side)(a, w, bias)


def _ffn_up_act(a, w4, cw, cb, tm, nex, tp, side=None):
    m, k = a.shape
    ffs = w4.shape[2]

    def body(a_ref, wu_ref, wg_ref, cu_ref, cg_ref, bu_ref, bg_ref, up_ref, ug_ref, act_ref, win):
        i = pl.program_id(1)

        @pl.when(i == 0)
        def _():
            win[:, 0:FHALO, :] = jnp.zeros((2, FHALO, ffs), F32)

        rows = i * tm + lax.broadcasted_iota(jnp.int32, (tm, 1), 0)
        pad = _pad_rows(rows, nex, tp)
        av = a_ref[...]
        for p, w_ref in ((0, wu_ref), (1, wg_ref)):
            x = jnp.where(pad, 0.0, _dot(av, w_ref[...]))
            win[p, FHALO:FHALO + tm, :] = x
            up_ref[p] = x.astype(BF16)
        for r0 in range(0, tm, RCH):
            for c0, c1 in _lane_groups(ffs):
                u = _conv3(win, 0, r0, c0, c1, cu_ref, bu_ref)
                g = _conv3(win, 1, r0, c0, c1, cg_ref, bg_ref)
                ug_ref[0, r0:r0 + RCH, c0:c1] = u.astype(BF16)
                ug_ref[1, r0:r0 + RCH, c0:c1] = g.astype(BF16)
                act_ref[r0:r0 + RCH, c0:c1] = (g * (0.5 * (1.0 + lax.erf(g * (1.0 / math.sqrt(2.0))))) * u).astype(BF16)
        win[:, 0:FHALO, :] = win[:, tm:tm + FHALO, :]

    wide = jax.ShapeDtypeStruct((2, m, 2 * ffs), BF16)
    return _call(body, "ffn_up_act", (wide, wide, jax.ShapeDtypeStruct((m, 2 * ffs), BF16)),
                 (2, m // tm),
                 [pl.BlockSpec((tm, k), lambda c, i: (i, 0)),
                  pl.BlockSpec((None, k, ffs), lambda c, i: (c, 0, 0)), pl.BlockSpec((None, k, ffs), lambda c, i: (c + 2, 0, 0)),
                  pl.BlockSpec((FTAPS, ffs), lambda c, i: (0, c)), pl.BlockSpec((FTAPS, ffs), lambda c, i: (0, c + 2)),
                  pl.BlockSpec((1, ffs), lambda c, i: (0, c)), pl.BlockSpec((1, ffs), lambda c, i: (0, c + 2))],
                 (pl.BlockSpec((2, tm, ffs), lambda c, i: (0, i, c)), pl.BlockSpec((2, tm, ffs), lambda c, i: (0, i, c)),
                  pl.BlockSpec((tm, ffs), lambda c, i: (i, c))),
                 scratch=[pltpu.VMEM((2, FHALO + tm, ffs), F32)], side=side)(a, w4, w4, cw, cw, cb, cb)


def _fill_kv(ks, vs, e, prev_ref, cur_ref, meta_ref):
    for piece, lo, n in ((prev_ref, 0, BLK), (cur_ref, BLK, BLK), (meta_ref, 2 * BLK, N_META)):
        val = piece[e]
        for hk in range(NKV):
            ks[e, hk, lo:lo + n, :] = val[:, hk * HD:(hk + 1) * HD].astype(BF16)
            vs[e, hk, lo:lo + n, :] = val[:, KVW + hk * HD:KVW + (hk + 1) * HD].astype(BF16)
    for hk in range(NKV):
        ks[e, hk, 2 * BLK + N_META:NKEY, :] = jnp.zeros((BLK - N_META, HD), BF16)
        vs[e, hk, 2 * BLK + N_META:NKEY, :] = jnp.zeros((BLK - N_META, HD), BF16)


def _interleave(chains):
    live = list(chains)
    while live:
        for c in list(live):
            try:
                next(c)
            except StopIteration:
                live.remove(c)


def _head_softmax(q, ks_hk, bias_ref, sink_ref, h, out):
    qh = q[:, h * HD:(h + 1) * HD].astype(BF16)
    yield
    s = _dot_nt(qh, ks_hk) * (HD ** -0.5) + bias_ref[h]
    yield
    sink = sink_ref[0, h]
    mx = jnp.maximum(jnp.max(s, axis=-1, keepdims=True), sink)
    yield
    p = jnp.exp(s - mx)
    es = jnp.exp(sink - mx)
    yield
    inv = 1.0 / (jnp.sum(p, axis=-1, keepdims=True) + es)
    yield
    out["q"], out["p"], out["sink"] = qh, p * inv, es * inv
    yield


def _attn_specs(nex, blk_of):
    qcol, kvcol = (C_QKV) // AW, (C_QKV + AW) // (2 * KVW)
    return [
        pl.BlockSpec((nex, BLK, AW), lambda j: (0, blk_of(j), qcol)),
        pl.BlockSpec((nex, BLK, 2 * KVW), lambda j: (0, blk_of(j), kvcol)),
        pl.BlockSpec((nex, BLK, 2 * KVW), lambda j: (0, jnp.maximum(blk_of(j) - 1, 0), kvcol)),
        pl.BlockSpec((nex, N_META, 2 * KVW), lambda j: (0, PAD // N_META, kvcol)),
        pl.BlockSpec((None, NQ, BLK, NKEY), lambda j: (jnp.minimum(blk_of(j), 2), 0, 0, 0)),
        pl.BlockSpec(memory_space=pltpu.SMEM),
    ]


def _attn_fwd(z, bias, sinks, nex, nblk, side=None):
    m = z.shape[0]
    z3 = z.reshape(nex, nblk * BLK, z.shape[1])

    def head(e, h, q, ks, vs, bias_ref, sink_ref, oacc):
        out = {}
        yield from _head_softmax(q, ks[e, h // GRP], bias_ref, sink_ref, h, out)
        o = _dot(out["p"].astype(BF16), vs[e, h // GRP])
        yield
        oacc[e, :, h * HD:(h + 1) * HD] = o
        yield

    def body(q_ref, cur_ref, prev_ref, meta_ref, bias_ref, sink_ref, o_ref, ks, vs, oacc):
        for e in range(nex):
            _fill_kv(ks, vs, e, prev_ref, cur_ref, meta_ref)
        _interleave([head(e, h, q_ref[e], ks, vs, bias_ref, sink_ref, oacc) for e in range(nex) for h in range(NQ)])
        for e in range(nex):
            o_ref[e] = oacc[e].astype(BF16)

    res = _call(body, "attn_fwd", jax.ShapeDtypeStruct((nex, nblk * BLK, AW), BF16), (nblk,),
                _attn_specs(nex, lambda j: j),
                pl.BlockSpec((nex, BLK, AW), lambda j: (0, j, 0)),
                scratch=[pltpu.VMEM((nex, NKV, NKEY, HD), BF16), pltpu.VMEM((nex, NKV, NKEY, HD), BF16),
                         pltpu.VMEM((nex, BLK, AW), F32)], side=side)(z3, z3, z3, z3, bias, sinks.reshape(1, NQ))
    if side is None:
        return res.reshape(m, AW)
    return (res[0].reshape(m, AW),) + tuple(res[1:])


def _cgate(cv, cg, rows, nex, tp):
    return jnp.where(_pad_rows(rows, nex, tp), 0.0, cv * _sigmoid(cg))


CLANES = 256


def _rolled_up(blk, b):
    return blk if b == 0 else pltpu.roll(blk, blk.shape[0] - b, axis=0)


def _conv_fwd(z, w, wb, g, b, nex, tp, side=None):
    m = z.shape[0]
    tm = BLK
    sub = tm // CHALO
    cvc, cgc = C_CONV // CW, C_CONV // CW + 1

    def body(cv_ref, cg_ref, cvh_ref, cgh_ref, w_ref, wb_ref, g_ref, b_ref, cc_ref, cs_ref, win):
        i = pl.program_id(0)
        rows = i * tm + lax.broadcasted_iota(jnp.int32, (tm, 1), 0)
        hrows = i * tm - CHALO + lax.broadcasted_iota(jnp.int32, (CHALO, 1), 0)
        win[0:CHALO, :] = _cgate(cvh_ref[...], cgh_ref[...], hrows, nex, tp)
        win[CHALO:CHALO + tm, :] = _cgate(cv_ref[...], cg_ref[...], rows, nex, tp)
        for sb in range(sub):
            lo = sb * CHALO
            for c0 in range(0, CW, CLANES):
                blk = win[lo:lo + 2 * CHALO, c0:c0 + CLANES]
                acc = jnp.zeros((CHALO, CLANES), F32) + wb_ref[:, c0:c0 + CLANES]
                for b in range(8):
                    rb = _rolled_up(blk, b)
                    for a in range(5):
                        s = 8 * a + b
                        if 2 <= s <= CTAPS + 1:
                            acc = acc + w_ref[s - 2:s - 1, c0:c0 + CLANES] * rb[8 * a:8 * a + CHALO]
                cc_ref[lo:lo + CHALO, c0:c0 + CLANES] = acc
        xhat, _ = _ln_stats(cc_ref[...])
        cl = xhat * g_ref[...] + b_ref[...]
        cs_ref[...] = (cl * _sigmoid(cl)).astype(BF16)

    halo = lambda i: jnp.maximum(i * sub - 1, 0)
    vec = pl.BlockSpec((1, CW), lambda i: (0, 0))
    return _call(body, "conv_fwd", (jax.ShapeDtypeStruct((m, CW), F32), jax.ShapeDtypeStruct((m, CW), BF16)),
                 (m // tm,),
                 [pl.BlockSpec((tm, CW), lambda i: (i, cvc)), pl.BlockSpec((tm, CW), lambda i: (i, cgc)),
                  pl.BlockSpec((CHALO, CW), lambda i: (halo(i), cvc)), pl.BlockSpec((CHALO, CW), lambda i: (halo(i), cgc)),
                  pl.BlockSpec((CTAPS, CW), lambda i: (0, 0)), vec, vec, vec],
                 (pl.BlockSpec((tm, CW), lambda i: (i, 0)), pl.BlockSpec((tm, CW), lambda i: (i, 0))),
                 scratch=[pltpu.VMEM((CHALO + tm, CW), F32)], side=side)(z, z, z, z, w, wb.reshape(1, CW), g.reshape(1, CW), b.reshape(1, CW))


def _mix_fwd(a, cs, wap4, wcp4, z, tm, side=None):
    m = a.shape[0]
    ns = wap4.shape[2]

    def body(a_ref, cs_ref, wa_ref, wc_ref, ga_ref, gc_ref, ya_ref, yc_ref, mix_ref):
        av, cv = a_ref[...], cs_ref[...]
        for j in range(NCHIP):
            cols = slice(j * ns, (j + 1) * ns)
            ya, yc = _dot(av, wa_ref[j]), _dot(cv, wc_ref[j])
            ya_ref[:, cols] = ya.astype(BF16)
            yc_ref[:, cols] = yc.astype(BF16)
            mix_ref[:, cols] = (_sigmoid(ga_ref[:, cols]) * ya + _sigmoid(gc_ref[:, cols]) * yc).astype(BF16)

    wspec = pl.BlockSpec((NCHIP, AW, ns), lambda i: (0, 0, 0))
    row = lambda i: (i, 0)
    return _call(body, "mix_fwd",
                 (jax.ShapeDtypeStruct((m, D), BF16), jax.ShapeDtypeStruct((m, D), BF16), jax.ShapeDtypeStruct((m, D), BF16)),
                 (m // tm,),
                 [pl.BlockSpec((tm, AW), row), pl.BlockSpec((tm, CW), row), wspec, wspec,
                  pl.BlockSpec((tm, D), lambda i: (i, 0)), pl.BlockSpec((tm, D), lambda i: (i, 1))],
                 (pl.BlockSpec((tm, D), row), pl.BlockSpec((tm, D), row), pl.BlockSpec((tm, D), row)), side=side)(a, cs, wap4, wcp4, z, z)


def _mm_res_ln(name, a, w, res, g, b, tm, side=None):
    m, k = a.shape
    nband = 2 if tm % 32 == 0 else 1

    def rows(lo, hi, a_ref, w_ref, res_ref, g_ref, b_ref, r_ref, h_ref, hb_ref):
        acc = _dot(a_ref[lo:hi, :], w_ref[...])
        yield
        r = ALPHA * res_ref[lo:hi, :] + acc
        r_ref[lo:hi, :] = r
        yield
        xhat, _ = _ln_stats(r)
        yield
        y = xhat * g_ref[...] + b_ref[...]
        h_ref[lo:hi, :] = y
        hb_ref[lo:hi, :] = y.astype(BF16)
        yield

    def body(*refs):
        _interleave([rows(lo, lo + tm // nband, *refs) for lo in range(0, tm, tm // nband)])

    row = lambda i: (i, 0)
    vec = pl.BlockSpec((1, D), lambda i: (0, 0))
    return _call(body, name,
                 (jax.ShapeDtypeStruct((m, D), F32), jax.ShapeDtypeStruct((m, D), F32), jax.ShapeDtypeStruct((m, D), BF16)),
                 (m // tm,),
                 [pl.BlockSpec((tm, k), row), pl.BlockSpec((k, D), lambda i: (0, 0)), pl.BlockSpec((tm, D), row), vec, vec],
                 (pl.BlockSpec((tm, D), row), pl.BlockSpec((tm, D), row), pl.BlockSpec((tm, D), row)), side=side)(a, w, res, g.reshape(1, D), b.reshape(1, D))


RCH = 16


def _lane_groups(width, most=768):
    n = -(-width // most)
    step = -(-width // (128 * n)) * 128
    return [(c, min(c + step, width)) for c in range(0, width, step)]


def _conv3(win, p, r0, c0, c1, w_ref, b_ref):
    blk = win[p, r0:r0 + FHALO + RCH, c0:c1]
    x0, x1, x2 = blk[FHALO:], pltpu.roll(blk, 1, axis=0)[FHALO:], pltpu.roll(blk, 2, axis=0)[FHALO:]
    return b_ref[:, c0:c1] + w_ref[0:1, c0:c1] * x2 + w_ref[1:2, c0:c1] * x1 + w_ref[2:3, c0:c1] * x0


def _loss_grad(y, target, nblk):
    nex = target.shape[0]
    m = y.shape[0]

    def body(y_ref, t_ref, dy_ref, acc_ref):
        bb, j = pl.program_id(0), pl.program_id(1)

        @pl.when((bb == 0) & (j == 0))
        def _():
            acc_ref[...] = jnp.zeros_like(acc_ref)

        @pl.when(j == 0)
        def _():
            dy_ref[...] = jnp.zeros_like(dy_ref)

        @pl.when(j > 0)
        def _():
            e = y_ref[...] - t_ref[...]
            dy_ref[...] = e * (1.0 / D)
            acc_ref[...] += jnp.sum((e * e).reshape(BLK // 8, 8, D), axis=0)

    return _call(body, "loss_grad", (jax.ShapeDtypeStruct((m, D), F32), jax.ShapeDtypeStruct((8, D), F32)), (nex, nblk),
                 [pl.BlockSpec((BLK, D), lambda bb, j: (bb * nblk + j, 0)),
                  pl.BlockSpec((None, BLK, D), lambda bb, j: (bb, jnp.maximum(j - 1, 0), 0))],
                 (pl.BlockSpec((BLK, D), lambda bb, j: (bb * nblk + j, 0)), pl.BlockSpec((8, D), lambda bb, j: (0, 0))))(y, target)


def _ln_bwd_call(name, dy, r, g, tm, a_list=(), w=None, cols=None, side=None):
    m = dy.shape[0]
    na = len(a_list)

    nband = 2 if tm % 32 == 0 else 1

    def body(*refs):
        dy_ref, r_ref, g_ref = refs[0:3]
        a_refs = refs[3:3 + na]
        w_ref = refs[3 + na] if na else None
        dr_ref, drb_ref, dg_ref, db_ref = refs[-4:]
        i = pl.program_id(0)
        sums = []

        def band(lo, hi):
            dh = dy_ref[lo:hi, :]
            if na:
                dh = ALPHA * dh
                if cols is None:
                    ns = w.shape[2]
                    for j in range(NCHIP):
                        dh = dh + _dot_nt(a_refs[0][j // 2, lo:hi, (j % 2) * ns:(j % 2 + 1) * ns], w_ref[j])
                        yield
                else:
                    for a_ref, (c0, c1) in zip(a_refs, cols):
                        dh = dh + _dot_nt(a_ref[lo:hi, :], w_ref[:, c0:c1])
                        yield
            xhat, rstd = _ln_stats(r_ref[lo:hi, :])
            yield
            dr = _ln_bwd(dh, xhat, rstd, g_ref[...])
            yield
            dr_ref[lo:hi, :] = dr
            drb_ref[lo:hi, :] = dr.astype(BF16)
            sums.append((jnp.sum(dh * xhat, axis=0, keepdims=True), jnp.sum(dh, axis=0, keepdims=True)))
            yield

        _interleave([band(lo, lo + tm // nband) for lo in range(0, tm, tm // nband)])

        @pl.when(i == 0)
        def _():
            dg_ref[...] = jnp.zeros_like(dg_ref)
            db_ref[...] = jnp.zeros_like(db_ref)

        for sg, sb_ in sums:
            dg_ref[...] += sg
            db_ref[...] += sb_

    row = lambda i: (i, 0)
    vec = pl.BlockSpec((1, D), lambda i: (0, 0))
    in_specs = [pl.BlockSpec((tm, D), row), pl.BlockSpec((tm, D), row), vec]
    for a in a_list:
        in_specs.append(pl.BlockSpec((2, tm, a.shape[2]), lambda i: (0, i, 0)) if a.ndim == 3 else pl.BlockSpec((tm, a.shape[1]), row))
    if na:
        in_specs.append(pl.BlockSpec(w.shape, (lambda i: (0, 0, 0)) if w.ndim == 3 else (lambda i: (0, 0))))
    return _call(body, name,
                 (jax.ShapeDtypeStruct((m, D), F32), jax.ShapeDtypeStruct((m, D), BF16),
                  jax.ShapeDtypeStruct((1, D), F32), jax.ShapeDtypeStruct((1, D), F32)),
                 (m // tm,), in_specs,
                 (pl.BlockSpec((tm, D), row), pl.BlockSpec((tm, D), row), vec, vec),
                 side=side)(dy, r, g.reshape(1, D), *a_list, *([w] if na else []))


def _ffn_bwd(drb, wdown, ug, up3, w, tm, nex, tp, side=None):
    _, m, dff = ug.shape
    ffs = dff // 2
    nt = m // tm

    def body(dr_ref, wd_ref, ug_ref, x_ref, wu_ref, wg_ref, o_ref, dw_ref, db_ref, dact, carry, dwacc, dbacc):
        i = pl.program_id(1)
        tile = nt - 1 - i
        dact[...] = _dot_nt(dr_ref[...], wd_ref[...])
        dwacc[...] = jnp.zeros_like(dwacc)
        dbacc[...] = jnp.zeros_like(dbacc)

        @pl.when(i == 0)
        def _():
            carry[...] = jnp.zeros_like(carry)
            dw_ref[...] = jnp.zeros_like(dw_ref)
            db_ref[...] = jnp.zeros_like(db_ref)

        fold = lambda t: t[0:8, :] + t[8:16, :]
        for r0 in reversed(range(0, tm, RCH)):
            rows = tile * tm + r0 + lax.broadcasted_iota(jnp.int32, (RCH, 1), 0)
            pad = _pad_rows(rows, nex, tp)
            for c0, c1 in _lane_groups(ffs, 384):
                u = ug_ref[0, r0:r0 + RCH, c0:c1].astype(F32)
                g = ug_ref[1, r0:r0 + RCH, c0:c1].astype(F32)
                da = dact[r0:r0 + RCH, c0:c1]
                cdf = 0.5 * (1.0 + lax.erf(g * (1.0 / math.sqrt(2.0))))
                pdf = jnp.exp(-0.5 * g * g) * (1.0 / math.sqrt(2.0 * math.pi))
                for p, w_ref, d0 in ((0, wu_ref, da * (g * cdf)), (1, wg_ref, da * u * (cdf + g * pdf))):
                    dblk = jnp.concatenate([d0, carry[p, :, c0:c1]], axis=0)
                    d1 = pltpu.roll(dblk, RCH + FHALO - 1, axis=0)[:RCH]
                    d2 = pltpu.roll(dblk, RCH + FHALO - 2, axis=0)[:RCH]
                    carry[p, :, c0:c1] = d0[0:FHALO]
                    dpre = w_ref[2:3, c0:c1] * d0 + w_ref[1:2, c0:c1] * d1 + w_ref[0:1, c0:c1] * d2
                    o_ref[p, r0:r0 + RCH, c0:c1] = jnp.where(pad, 0.0, dpre).astype(BF16)
                    x0 = x_ref[p, r0:r0 + RCH, c0:c1].astype(F32)
                    dwacc[p, 2, :, c0:c1] += fold(d0 * x0)
                    dwacc[p, 1, :, c0:c1] += fold(d1 * x0)
                    dwacc[p, 0, :, c0:c1] += fold(d2 * x0)
                    dbacc[p, :, c0:c1] += fold(d0)
        for p in range(2):
            for k in range(FTAPS):
                dw_ref[p, k:k + 1, :] += jnp.sum(dwacc[p, k], axis=0, keepdims=True)
            db_ref[p] += jnp.sum(dbacc[p], axis=0, keepdims=True)

    wide = pl.BlockSpec((2, tm, ffs), lambda c, i: (0, nt - 1 - i, c))
    return _call(body, "ffn_bwd",
                 (jax.ShapeDtypeStruct((2, m, dff), BF16), jax.ShapeDtypeStruct((2, FTAPS, dff), F32),
                  jax.ShapeDtypeStruct((2, 1, dff), F32)),
                 (2, nt),
                 [pl.BlockSpec((tm, D), lambda c, i: (nt - 1 - i, 0)), pl.BlockSpec((ffs, D), lambda c, i: (c, 0)), wide, wide,
                  pl.BlockSpec((FTAPS, ffs), lambda c, i: (0, c)), pl.BlockSpec((FTAPS, ffs), lambda c, i: (0, c + 2))],
                 (wide, pl.BlockSpec((2, FTAPS, ffs), lambda c, i: (0, 0, c)), pl.BlockSpec((2, 1, ffs), lambda c, i: (0, 0, c))),
                 scratch=[pltpu.VMEM((tm, ffs), F32), pltpu.VMEM((2, FHALO, ffs), F32),
                          pltpu.VMEM((2, FTAPS, 8, ffs), F32), pltpu.VMEM((2, 8, ffs), F32)], side=side)(drb, wdown, ug, up3, w, w)


def _mm_tn(name, a, b, tk, tn, b_cols=None, chip_out=False, side=None):
    m, k = a.shape
    n = b.shape[-1] * (2 if b.ndim == 3 else 1)

    def body(a_ref, b_ref, o_ref):
        o_ref[...] = _dot_tn(a_ref[...], b_ref[...])

    if b.ndim == 3:
        bspec = pl.BlockSpec((None, m, tn), lambda kk, j: (b_cols(j)[0], 0, b_cols(j)[1]))
    else:
        bspec = pl.BlockSpec((m, tn), lambda kk, j: (0, j))
    if chip_out:
        oshape, ospec = (n // tn, k, tn), pl.BlockSpec((None, tk, tn), lambda kk, j: (j, kk, 0))
    else:
        oshape, ospec = (k, n), pl.BlockSpec((tk, tn), lambda kk, j: (kk, j))
    return _call(body, name, jax.ShapeDtypeStruct(oshape, F32), (k // tk, n // tn),
                 [pl.BlockSpec((m, tk), lambda kk, j: (0, kk)), bspec], ospec, side=side)(a, b)


def _gate_bwd(drb, wout, ya, yc, z, tm):
    m = drb.shape[0]

    def body(dr_ref, w_ref, ya_ref, yc_ref, ga_ref, gc_ref, dya_ref, dyc_ref, dz_ref, cs_ref):
        i = pl.program_id(0)
        dmix = _dot_nt(dr_ref[...], w_ref[...])
        sa, sc = _sigmoid(ga_ref[...]), _sigmoid(gc_ref[...])
        dya_ref[...] = (dmix * sa).astype(BF16)
        dyc_ref[...] = (dmix * sc).astype(BF16)
        dga = dmix * ya_ref[...].astype(F32) * sa * (1.0 - sa)
        dgc = dmix * yc_ref[...].astype(F32) * sc * (1.0 - sc)
        dz_ref[:, 0:D] = dga.astype(BF16)
        dz_ref[:, D:2 * D] = dgc.astype(BF16)

        @pl.when(i == 0)
        def _():
            cs_ref[...] = jnp.zeros_like(cs_ref)

        cs_ref[:, 0:D] += jnp.sum(dga, axis=0, keepdims=True)
        cs_ref[:, D:2 * D] += jnp.sum(dgc, axis=0, keepdims=True)

    row = lambda i: (i, 0)
    return _call(body, "gate_bwd",
                 (jax.ShapeDtypeStruct((m, D), BF16), jax.ShapeDtypeStruct((m, D), BF16),
                  jax.ShapeDtypeStruct((m, 2 * D), BF16), jax.ShapeDtypeStruct((1, 2 * D), F32)),
                 (m // tm,),
                 [pl.BlockSpec((tm, D), row), pl.BlockSpec((D, D), lambda i: (0, 0)), pl.BlockSpec((tm, D), row),
                  pl.BlockSpec((tm, D), row), pl.BlockSpec((tm, D), lambda i: (i, 0)), pl.BlockSpec((tm, D), lambda i: (i, 1))],
                 (pl.BlockSpec((tm, D), row), pl.BlockSpec((tm, D), row), pl.BlockSpec((tm, 2 * D), row),
                  pl.BlockSpec((1, 2 * D), lambda i: (0, 0))))(drb, wout, ya, yc, z, z)


def _conv_bwd_a(dyc, wcp4, cc, g, b, tm):
    m = cc.shape[0]
    ns = wcp4.shape[2]

    def body(dy_ref, w_ref, cc_ref, g_ref, b_ref, dcc_ref, dg_ref, db_ref, dwb_ref):
        i = pl.program_id(0)
        dcs = jnp.zeros((tm, CW), F32)
        for j in range(NCHIP):
            dcs = dcs + _dot_nt(dy_ref[:, j * ns:(j + 1) * ns], w_ref[j])
        xhat, rstd = _ln_stats(cc_ref[...])
        cl = xhat * g_ref[...] + b_ref[...]
        sg = _sigmoid(cl)
        dcl = dcs * sg * (1.0 + cl * (1.0 - sg))
        dcc = _ln_bwd(dcl, xhat, rstd, g_ref[...])
        dcc_ref[...] = dcc

        @pl.when(i == 0)
        def _():
            dg_ref[...] = jnp.zeros_like(dg_ref)
            db_ref[...] = jnp.zeros_like(db_ref)
            dwb_ref[...] = jnp.zeros_like(dwb_ref)

        dg_ref[...] += jnp.sum(dcl * xhat, axis=0, keepdims=True)
        db_ref[...] += jnp.sum(dcl, axis=0, keepdims=True)
        dwb_ref[...] += jnp.sum(dcc, axis=0, keepdims=True)

    row = lambda i: (i, 0)
    vec = pl.BlockSpec((1, CW), lambda i: (0, 0))
    v = jax.ShapeDtypeStruct((1, CW), F32)
    return _call(body, "conv_bwd_a", (jax.ShapeDtypeStruct((m, CW), F32), v, v, v), (m // tm,),
                 [pl.BlockSpec((tm, D), row), pl.BlockSpec((NCHIP, CW, ns), lambda i: (0, 0, 0)), pl.BlockSpec((tm, CW), row), vec, vec],
                 (pl.BlockSpec((tm, CW), row), vec, vec, vec))(dyc, wcp4, cc, g.reshape(1, CW), b.reshape(1, CW))


def _conv_bwd_b(dcc, z, w, nex, tp, side=None):
    m = dcc.shape[0]
    tm = BLK
    sub = tm // CHALO
    nt = m // tm
    cvc, cgc = C_CONV // CW, C_CONV // CW + 1

    def body(d_ref, dh_ref, cv_ref, cg_ref, w_ref, dz_ref, dw_ref, cs_ref, dwin, dwacc):
        i = pl.program_id(0)
        rows = i * tm + lax.broadcasted_iota(jnp.int32, (tm, 1), 0)
        dwin[0:tm, :] = d_ref[...]
        dwin[tm:tm + CHALO, :] = jnp.where(i == nt - 1, 0.0, dh_ref[...])

        @pl.when(i == 0)
        def _():
            dwacc[...] = jnp.zeros_like(dwacc)
            cs_ref[...] = jnp.zeros_like(cs_ref)

        fold = lambda t: (t[0:8] + t[8:16]) + (t[16:24] + t[24:32])
        for sb in range(sub):
            lo = sb * CHALO
            pad = _pad_rows(rows[lo:lo + CHALO], nex, tp)
            for c0 in range(0, CW, CLANES):
                cs_ = slice(c0, c0 + CLANES)
                cv = cv_ref[lo:lo + CHALO, cs_]
                sg = _sigmoid(cg_ref[lo:lo + CHALO, cs_])
                cgin = jnp.where(pad, 0.0, cv * sg)
                blk = dwin[lo:lo + 2 * CHALO, cs_]
                acc = jnp.zeros((CHALO, CLANES), F32)
                for b in range(8):
                    rb = _rolled_up(blk, b)
                    for a in range(4):
                        s = 8 * a + b
                        if s <= CTAPS - 1:
                            k = CTAPS - 1 - s
                            sh = rb[8 * a:8 * a + CHALO]
                            acc = acc + w_ref[k:k + 1, cs_] * sh
                            dwacc[k, :, cs_] += fold(sh * cgin)
                dcg = jnp.where(pad, 0.0, acc)
                dcv = dcg * sg
                dgt = dcg * cv * sg * (1.0 - sg)
                dz_ref[lo:lo + CHALO, cs_] = dcv.astype(BF16)
                dz_ref[lo:lo + CHALO, CW + c0:CW + c0 + CLANES] = dgt.astype(BF16)
                cs_ref[:, cs_] += jnp.sum(dcv, axis=0, keepdims=True)
                cs_ref[:, CW + c0:CW + c0 + CLANES] += jnp.sum(dgt, axis=0, keepdims=True)

        @pl.when(i == nt - 1)
        def _():
            for k in range(CHALO):
                dw_ref[k:k + 1, :] = jnp.sum(dwacc[k], axis=0, keepdims=True)

    nxt = lambda i: jnp.minimum((i + 1) * sub, m // CHALO - 1)
    return _call(body, "conv_bwd_b",
                 (jax.ShapeDtypeStruct((m, 2 * CW), BF16), jax.ShapeDtypeStruct((CHALO, CW), F32),
                  jax.ShapeDtypeStruct((1, 2 * CW), F32)),
                 (nt,),
                 [pl.BlockSpec((tm, CW), lambda i: (i, 0)), pl.BlockSpec((CHALO, CW), lambda i: (nxt(i), 0)),
                  pl.BlockSpec((tm, CW), lambda i: (i, cvc)), pl.BlockSpec((tm, CW), lambda i: (i, cgc)),
                  pl.BlockSpec((CTAPS, CW), lambda i: (0, 0))],
                 (pl.BlockSpec((tm, 2 * CW), lambda i: (i, 0)), pl.BlockSpec((CHALO, CW), lambda i: (0, 0)),
                  pl.BlockSpec((1, 2 * CW), lambda i: (0, 0))),
                 scratch=[pltpu.VMEM((tm + CHALO, CW), F32), pltpu.VMEM((CHALO, 8, CW), F32)], side=side)(dcc, dcc, z, z, w)


def _attn_bwd(z, bias, sinks, dya, wap4, nex, nblk, side=None):
    m = z.shape[0]
    tp = nblk * BLK
    ns = wap4.shape[2]
    blk_of = lambda j: nblk - 1 - j
    z3 = z.reshape(nex, tp, z.shape[1])
    dya3 = dya.reshape(nex, tp, D)

    def body(q_ref, cur_ref, prev_ref, meta_ref, bias_ref, sink_ref, dy_ref, w_ref,
             dz_ref, cs_ref, dsk_ref, dbias_ref, ks, vs, carry, macc, dqacc, dkv, okv):
        j = pl.program_id(0)
        n = nblk - 1 - j

        @pl.when(j == 0)
        def _():
            carry[...] = jnp.zeros_like(carry)
            macc[...] = jnp.zeros_like(macc)
            cs_ref[...] = jnp.zeros_like(cs_ref)
            dsk_ref[...] = jnp.zeros_like(dsk_ref)

        @pl.when((j == 0) | (n <= 1))
        def _():
            dbias_ref[...] = jnp.zeros_like(dbias_ref)

        lane = lax.broadcasted_iota(jnp.int32, (1, BLK), 1)
        dsk = jnp.zeros((1, BLK), F32)

        def head(e, h, q, da, out):
            hk = h // GRP
            yield from _head_softmax(q, ks[e, hk], bias_ref, sink_ref, h, out)
            pn = out["p"]
            doh = da[:, h * HD:(h + 1) * HD].astype(BF16)
            yield
            dp = _dot_nt(doh, vs[e, hk])
            yield
            dl = jnp.sum(pn * dp, axis=-1, keepdims=True)
            yield
            ds = pn * (dp - dl)
            yield
            dbias_ref[h] += ds
            out["dsink"] = jnp.sum(-out["sink"] * dl)
            yield
            dsb = (ds * (HD ** -0.5)).astype(BF16)
            yield
            dqacc[e, :, h * HD:(h + 1) * HD] = _dot(dsb, ks[e, hk])
            out["ds"], out["pb"], out["do"] = dsb, pn.astype(BF16), doh
            yield

        def kv_head(e, hk, outs):
            rows = lambda key: jnp.concatenate([outs[hk * GRP + g][key] for g in range(GRP)], axis=0)
            dk = _dot_tn(rows("ds"), rows("q"))
            yield
            dv = _dot_tn(rows("pb"), rows("do"))
            yield
            dkv[e, :, hk * HD:(hk + 1) * HD] = dk
            dkv[e, :, KVW + hk * HD:KVW + (hk + 1) * HD] = dv
            yield

        das, outs = [], [[{} for _ in range(NQ)] for _ in range(nex)]
        for e in range(nex):
            _fill_kv(ks, vs, e, prev_ref, cur_ref, meta_ref)
            da = jnp.zeros((BLK, AW), F32)
            for jj in range(NCHIP):
                da = da + _dot_nt(dy_ref[e, :, jj * ns:(jj + 1) * ns], w_ref[jj])
            das.append(da)
        _interleave([head(e, h, q_ref[e], das[e], outs[e][h]) for e in range(nex) for h in range(NQ)])
        _interleave([kv_head(e, hk, outs[e]) for e in range(nex) for hk in range(NKV)])
        for e in range(nex):
            for h in range(NQ):
                dsk = dsk + jnp.where(lane == h, outs[e][h]["dsink"], 0.0)
            macc[e] += dkv[e, 2 * BLK:2 * BLK + N_META, :]
            okv[e] = dkv[e, BLK:2 * BLK, :] + carry[e]
            carry[e] = dkv[e, 0:BLK, :]

            @pl.when(n == 0)
            def _():
                okv[e, PAD:BLK, :] += macc[e]

            dq = dqacc[e]
            ok = okv[e]
            dz_ref[e, :, 0:AW] = dq.astype(BF16)
            dz_ref[e, :, AW:AW + 2 * KVW] = ok.astype(BF16)
            cs_ref[:, 0:AW] += jnp.sum(dq, axis=0, keepdims=True)
            cs_ref[:, AW:AW + 2 * KVW] += jnp.sum(ok, axis=0, keepdims=True)
        dsk_ref[...] += dsk

    wz = AW + 2 * KVW
    specs = _attn_specs(nex, blk_of) + [
        pl.BlockSpec((nex, BLK, D), lambda j: (0, blk_of(j), 0)),
        pl.BlockSpec((NCHIP, AW, ns), lambda j: (0, 0, 0))]
    res = _call(body, "attn_bwd",
                (jax.ShapeDtypeStruct((nex, tp, wz), BF16), jax.ShapeDtypeStruct((1, wz), F32), jax.ShapeDtypeStruct((1, BLK), F32),
                 jax.ShapeDtypeStruct((1, 3, NQ, BLK, NKEY), F32)),
                (nblk,), specs,
                (pl.BlockSpec((nex, BLK, wz), lambda j: (0, blk_of(j), 0)), pl.BlockSpec((1, wz), lambda j: (0, 0)),
                 pl.BlockSpec((1, BLK), lambda j: (0, 0)),
                 pl.BlockSpec((None, None, NQ, BLK, NKEY), lambda j: (0, jnp.minimum(blk_of(j), 2), 0, 0, 0))),
                scratch=[pltpu.VMEM((nex, NKV, NKEY, HD), BF16), pltpu.VMEM((nex, NKV, NKEY, HD), BF16),
                         pltpu.VMEM((nex, BLK, 2 * KVW), F32), pltpu.VMEM((nex, N_META, 2 * KVW), F32),
                         pltpu.VMEM((nex, BLK, AW), F32), pltpu.VMEM((nex, NKEY, 2 * KVW), F32),
                         pltpu.VMEM((nex, BLK, 2 * KVW), F32)],
                side=side)(z3, z3, z3, z3, bias, sinks.reshape(1, NQ), dya3, wap4)
    return (res[0].reshape(m, wz),) + tuple(res[1:])


def _tile_rows(rows, cols, target_bytes=1 << 20):
    best = None
    for t in range(8, rows + 1, 8):
        if rows % t == 0 and t * cols * 4 <= target_bytes:
            best = t
    return best or rows


def _sum0(name, x):
    n, r, c = x.shape
    tr = r if n * r * c * 4 <= (8 << 20) else _tile_rows(r, c * n)

    def body(x_ref, o_ref):
        acc = x_ref[0]
        for k in range(1, n):
            acc = acc + x_ref[k]
        o_ref[...] = acc

    return _call(body, name, jax.ShapeDtypeStruct((r, c), F32), (r // tr,),
                 [pl.BlockSpec((n, tr, c), lambda i: (0, i, 0))], pl.BlockSpec((tr, c), lambda i: (i, 0)))(x)


def _adamw(name, w, g, mom, vel):
    r, c = w.shape
    tr = _tile_rows(r, c)
    c1 = 1.0 / (1.0 - ADAM_B1 ** ADAM_STEP)
    c2 = 1.0 / (1.0 - ADAM_B2 ** ADAM_STEP)

    def body(w_ref, g_ref, m_ref, v_ref, d_ref, mo_ref, vo_ref):
        gg = g_ref[...]
        mn = ADAM_B1 * m_ref[...] + (1.0 - ADAM_B1) * gg
        vn = ADAM_B2 * v_ref[...] + (1.0 - ADAM_B2) * (gg * gg)
        mo_ref[...] = mn
        vo_ref[...] = vn
        d_ref[...] = -ADAM_LR * ((mn * c1) / (jnp.sqrt(vn * c2) + ADAM_EPS) + ADAM_WD * w_ref[...])

    spec = pl.BlockSpec((tr, c), lambda i: (i, 0))
    o = jax.ShapeDtypeStruct((r, c), F32)
    return _call(body, name, (o, o, o), (r // tr,), [spec] * 4, (spec, spec, spec))(w, g, mom, vel)


def _adamw_matrix(w, joined, own, mom, vel, core):
    _, r, c = w.shape
    h = r // 2
    th = _tile_rows(h, c, 2 << 20)
    nt = h // th
    c1 = 1.0 / (1.0 - ADAM_B1 ** ADAM_STEP)
    c2 = 1.0 / (1.0 - ADAM_B2 ** ADAM_STEP)

    def body(core_ref, w_ref, j_ref, o0_ref, o1_ref, m_ref, v_ref, g_ref, d_ref, mo_ref, vo_ref):
        layer, half = pl.program_id(0), pl.program_id(1)
        mine = jnp.where(layer == 0, o0_ref[...], o1_ref[...])
        gg = jnp.where(half == core_ref[0], mine, j_ref[...])
        g_ref[...] = gg
        mn = ADAM_B1 * m_ref[...] + (1.0 - ADAM_B1) * gg
        vn = ADAM_B2 * v_ref[...] + (1.0 - ADAM_B2) * (gg * gg)
        mo_ref[...] = mn
        vo_ref[...] = vn
        d_ref[...] = -ADAM_LR * ((mn * c1) / (jnp.sqrt(vn * c2) + ADAM_EPS) + ADAM_WD * w_ref[...])

    full = pl.BlockSpec((None, th, c), lambda l, hh, i, cr: (l, hh * nt + i, 0))
    other = pl.BlockSpec((None, th, c), lambda l, hh, i, cr: (l, (1 - cr[0]) * nt + i, 0))
    part = pl.BlockSpec((th, c), lambda l, hh, i, cr: (i, 0))
    o = jax.ShapeDtypeStruct(w.shape, F32)
    return _call(body, "adamw_matrix", (o, o, o, o), (2, 2, nt), [full, other, part, part, full, full],
                 (full, full, full, full), prefetch=1)(core, w, joined, own[0], own[1], mom, vel)


def _place():
    x, y, c = lax.axis_index("x"), lax.axis_index("y"), lax.axis_index("c")
    others = [(1 - x, y), (x, 1 - y), (1 - x, 1 - y)]
    return x, y, c, others


def _gather_job(items):
    nw = len(items)

    def views(s_ref, g_ref, layer, c):
        if layer is None:
            return s_ref, s_ref.at[c], lambda chip, cc: g_ref.at[chip, cc]
        hr = s_ref.shape[1] // 2
        return s_ref.at[layer], s_ref.at[layer, pl.ds(c * hr, hr)], lambda chip, cc: g_ref.at[chip, pl.ds(cc * hr, hr)]

    def copies(s_refs, g_refs, send, recv):
        x, y, c, others = _place()
        chip = 2 * x + y
        firsts, arrive, passed, arrive2 = [], [], [], []
        for w, (_, layer) in enumerate(items):
            whole, src, dst = views(s_refs[w], g_refs[w], layer, c)

            def rc(kk, s, d, to, w=w):
                return pltpu.make_async_remote_copy(src_ref=s, dst_ref=d, send_sem=send.at[w, kk], recv_sem=recv.at[w, kk],
                                                    device_id=to, device_id_type=MESH)
            firsts.append(rc(6, whole, g_refs[w].at[chip], (x, y, 1 - c)))
            arrive2.append(rc(6, whole, g_refs[w].at[chip], (x, y, c)))
            for k, (px, py) in enumerate(others):
                got, got2 = dst(2 * px + py, c), dst(2 * px + py, 1 - c)
                firsts.append(rc(k, src, dst(chip, c), (px, py, c)))
                arrive.append(rc(k, got, got, (x, y, c)))
                passed.append(rc(3 + k, got, got, (x, y, 1 - c)))
                arrive2.append(rc(3 + k, got2, got2, (x, y, c)))
        return firsts, arrive, passed, arrive2

    def start(s_refs, g_refs, send, recv):
        for cp in copies(s_refs, g_refs, send, recv)[0]:
            cp.start()

    def finish(s_refs, g_refs, send, recv):
        firsts, arrive, passed, arrive2 = copies(s_refs, g_refs, send, recv)
        for a, p in zip(arrive, passed):
            a.wait_recv()
            p.start()
        for a in arrive2:
            a.wait_recv()
        for cp in firsts + passed:
            cp.wait_send()

    outs = [jax.ShapeDtypeStruct((NCHIP,) + (s.shape if layer is None else s.shape[1:]), s.dtype) for s, layer in items]
    return _Job([s for s, _ in items], outs, (nw, 7), start, finish)


def _swap_job(grads):
    def copies(d_refs, a_refs, send, recv):
        x, y, c, _ = _place()
        cps = []
        for w in range(len(grads)):
            h = d_refs[w].shape[1] // 2
            cps.append(pltpu.make_async_remote_copy(
                src_ref=d_refs[w].at[:, pl.ds((1 - c) * h, h), :], dst_ref=a_refs[w], send_sem=send.at[w], recv_sem=recv.at[w],
                device_id=(x, y, 1 - c), device_id_type=MESH))
        return cps

    def start(*r):
        for cp in copies(*r):
            cp.start()

    def finish(*r):
        for cp in copies(*r):
            cp.wait()

    outs = [jax.ShapeDtypeStruct((NCHIP, g.shape[1] // 2, g.shape[2]), g.dtype) for g in grads]
    return _Job(list(grads), outs, (len(grads),), start, finish)


def _exchange_job(parts):
    def copies(q_refs, b_refs, send, recv):
        x, y, c, others = _place()
        cps = []
        for w in range(len(parts)):
            for k, (px, py) in enumerate(others):
                cps.append(pltpu.make_async_remote_copy(
                    src_ref=q_refs[w].at[2 * px + py], dst_ref=b_refs[w].at[k], send_sem=send.at[w, k], recv_sem=recv.at[w, k],
                    device_id=(px, py, c), device_id_type=MESH))
        return cps

    def start(*r):
        for cp in copies(*r):
            cp.start()

    def finish(*r):
        for cp in copies(*r):
            cp.wait()

    outs = [jax.ShapeDtypeStruct((3,) + p.shape[1:], p.dtype) for p in parts]
    return _Job(list(parts), outs, (len(parts), 3), start, finish)


def _run_job(name, job):
    n_in, n_out = len(job.ins), len(job.outs)

    def body(*refs):
        ins, outs = refs[:n_in], refs[n_in:n_in + n_out]
        send, recv = refs[n_in + n_out:]
        job.start(ins, outs, send, recv)
        job.finish(ins, outs, send, recv)

    return pl.pallas_call(
        body, name=name, out_shape=tuple(job.outs), in_specs=[_ANY] * n_in, out_specs=tuple([_ANY] * n_out),
        scratch_shapes=[pltpu.SemaphoreType.DMA(job.sems), pltpu.SemaphoreType.DMA(job.sems)])(*job.ins)


def _sibling_join(halves):
    nw = len(halves)

    def body(*refs):
        h_refs = refs[:2 * nw]
        f_refs = refs[2 * nw:3 * nw]
        send, recv = refs[3 * nw:]
        x, y, c, _ = _place()
        cps = []
        for w in range(nw):
            for l in range(2):
                src = h_refs[2 * w + l]
                h = src.shape[0]
                dst = f_refs[w].at[l, pl.ds(c * h, h), :]
                cp = pltpu.make_async_remote_copy(src_ref=src, dst_ref=dst, send_sem=send.at[w, l], recv_sem=recv.at[w, l],
                                                  device_id=(x, y, 1 - c), device_id_type=MESH)
                cp.start()
                cps.append(cp)
        for w in range(nw):
            for l in range(2):
                src = h_refs[2 * w + l]
                h = src.shape[0]
                other = f_refs[w].at[l, pl.ds((1 - c) * h, h), :]
                pltpu.make_async_remote_copy(src_ref=src, dst_ref=other, send_sem=send.at[w, l], recv_sem=recv.at[w, l],
                                             device_id=(x, y, c), device_id_type=MESH).wait_recv()
        for cp in cps:
            cp.wait_send()

    flat = [a for pair in halves for a in pair]
    outs = tuple(jax.ShapeDtypeStruct((2, 2 * pair[0].shape[0], pair[0].shape[1]), F32) for pair in halves)
    return pl.pallas_call(
        body, name="grad_sibling_join", out_shape=outs, in_specs=[_ANY] * (2 * nw), out_specs=tuple([_ANY] * nw),
        scratch_shapes=[pltpu.SemaphoreType.DMA((nw, 2)), pltpu.SemaphoreType.DMA((nw, 2))])(*flat)


def _allgather_small(v):
    r = v.shape[0]

    def body(x_ref, out_ref, send_sems, recv_sems, local_sem):
        x, y, c, chips = _place()
        me, sibling = (x, y, c), (x, y, 1 - c)

        def slab(px, py, pc):
            return out_ref.at[4 * px + 2 * py + pc]

        def copy(k, block, to, src=None):
            return pltpu.make_async_remote_copy(src_ref=slab(*block) if src is None else src, dst_ref=slab(*block),
                                                send_sem=send_sems.at[k], recv_sem=recv_sems.at[k],
                                                device_id=to, device_id_type=MESH)

        mine = pltpu.make_async_copy(x_ref, slab(*me), local_sem)
        mine.start()
        first = [copy(0, me, sibling, src=x_ref)]
        first += [copy(1 + j, me, (*chip, c), src=x_ref) for j, chip in enumerate(chips)]
        for cp in first:
            cp.start()
        passed = [copy(4 + j, (*chip, c), sibling) for j, chip in enumerate(chips)]
        for j, chip in enumerate(chips):
            copy(1 + j, (*chip, c), me).wait_recv()
            passed[j].start()
        copy(0, sibling, me).wait_recv()
        for j, chip in enumerate(chips):
            copy(4 + j, (*chip, 1 - c), me).wait_recv()
        for cp in first + passed:
            cp.wait_send()
        mine.wait()

    return pl.pallas_call(
        body, name="allgather_small", out_shape=jax.ShapeDtypeStruct((8, r, 128), F32),
        in_specs=[pl.BlockSpec(memory_space=pltpu.VMEM)], out_specs=pl.BlockSpec(memory_space=pltpu.VMEM),
        scratch_shapes=[pltpu.SemaphoreType.DMA((7,)), pltpu.SemaphoreType.DMA((7,)), pltpu.SemaphoreType.DMA],
    )(v)


def _add_half(name, d, a, c):
    _, h, cols = a.shape
    nt = 1
    th = h // nt

    def body(c_ref, d_ref, a_ref, o_ref):
        o_ref[...] = (d_ref[...] + a_ref[...]).astype(BF16)

    return _call(body, name, jax.ShapeDtypeStruct(a.shape, BF16), (NCHIP, nt),
                 [pl.BlockSpec((None, th, cols), lambda p, i, cr: (p, cr[0] * nt + i, 0)),
                  pl.BlockSpec((None, th, cols), lambda p, i, cr: (p, i, 0))],
                 pl.BlockSpec((None, th, cols), lambda p, i, cr: (p, i, 0)), prefetch=1)(c, d, a)


def _add_chips(name, d, a, b, where):
    _, h, cols = a.shape
    th = h // 4 if (h % 64 == 0) else h
    nt = h // th

    def body(w_ref, d_ref, a_ref, b_ref, o_ref):
        own = d_ref[...] + a_ref[...]
        o_ref[...] = ((own + b_ref[0].astype(F32)) + b_ref[1].astype(F32)) + b_ref[2].astype(F32)

    return _call(body, name, jax.ShapeDtypeStruct((h, cols), F32), (nt,),
                 [pl.BlockSpec((None, th, cols), lambda i, wr: (wr[0], wr[1] * nt + i, 0)),
                  pl.BlockSpec((None, th, cols), lambda i, wr: (wr[0], i, 0)),
                  pl.BlockSpec((3, th, cols), lambda i, wr: (0, i, 0))],
                 pl.BlockSpec((th, cols), lambda i, wr: (i, 0)), prefetch=1)(where, d, a, b)


def _pack(arrs):
    pieces = []
    for a in arrs:
        f = a.reshape(-1)
        n = -(-f.shape[0] // 1024) * 1024
        pieces.append(jnp.pad(f, (0, n - f.shape[0])).reshape(-1, 128))
    return jnp.concatenate(pieces, axis=0)


def _unpack(buf, shapes):
    out, r = [], 0
    for s in shapes:
        n = int(np.prod(s))
        rows = -(-n // 1024) * 8
        out.append(buf[r:r + rows].reshape(-1)[:n].reshape(s))
        r += rows
    return out


def kernel(x, meta_tokens, in_ln_g, in_ln_b, rel_bias, w_in, b_in, attn_sinks, w_attn_proj, conv_dw, conv_dw_b, conv_ln_g, conv_ln_b, w_conv_proj, w_out, ln1_g, ln1_b, ffn_w_up, ffn_dw, ffn_dw_b, ffn_w_down, ln2_g, ln2_b, loss_target, m_meta_tokens, m_in_ln_g, m_in_ln_b, m_rel_bias, m_w_in, m_b_in, m_attn_sinks, m_w_attn_proj, m_conv_dw, m_conv_dw_b, m_conv_ln_g, m_conv_ln_b, m_w_conv_proj, m_w_out, m_ln1_g, m_ln1_b, m_ffn_w_up, m_ffn_dw, m_ffn_dw_b, m_ffn_w_down, m_ln2_g, m_ln2_b, v_meta_tokens, v_in_ln_g, v_in_ln_b, v_rel_bias, v_w_in, v_b_in, v_attn_sinks, v_w_attn_proj, v_conv_dw, v_conv_dw_b, v_conv_ln_g, v_conv_ln_b, v_w_conv_proj, v_w_out, v_ln1_g, v_ln1_b, v_ffn_w_up, v_ffn_dw, v_ffn_dw_b, v_ffn_w_down, v_ln2_g, v_ln2_b):
    nex, seq, _ = x.shape
    nblk = seq // BLK + 1
    tp = nblk * BLK
    m = nex * tp
    tm = _row_tile(m)
    ffs = ffn_w_up.shape[2]
    dff = 2 * ffs
    cx, cy, cc = lax.axis_index("x"), lax.axis_index("y"), lax.axis_index("c")
    chip = (2 * cx + cy).astype(jnp.int32)
    core = cc.astype(jnp.int32)

    names = ("in", "ap", "cp", "out", "up", "down")
    big = dict(zip(names, [w_in, w_attn_proj, w_conv_proj, w_out, ffn_w_up, ffn_w_down]))
    sb = {k: v.astype(BF16) for k, v in big.items()}
    gathered = {}

    first_items = [(meta_tokens.reshape(2, N_META // 2, -1), None), (conv_dw, None), (ffn_dw, None)]
    g_meta, g_cdw, g_fdw = _run_job("gather_small", _gather_job(first_items))
    meta_full = jnp.transpose(g_meta, (1, 2, 0, 3)).reshape(N_META, D)
    bias_tab = _bias_build(rel_bias)

    fwd_plan = {("embed_ln", 0): [("in", 0)],
                ("in_proj", 0): [("ap", 0), ("cp", 0), ("out", 0)], ("attn_fwd", 0): [("up", 0)],
                ("conv_fwd", 0): [("down", 0)], ("mix_fwd", 0): [("in", 1)],
                ("out_proj_ln", 0): [("ap", 1), ("cp", 1), ("out", 1)], ("ffn_up_act", 0): [("up", 1), ("down", 1)]}

    def fwd(tag, l, fn, *args):
        keys = fwd_plan.get((tag, l))
        if not keys:
            return fn(*args)
        items = [(sb[k], kl) for k, kl in keys]
        res = fn(*args, side=_gather_job(items))
        for key, g in zip(keys, res[-len(keys):]):
            gathered[key] = g
        main = res[:-len(keys)]
        return main[0] if len(main) == 1 else main

    def layer_weights(l):
        return dict(
            win=_w_in_from_chips(gathered[("in", l)]), bin=_to_new(b_in[l]).reshape(1, IN_COLS),
            cdw=jnp.transpose(g_cdw[:, l], (1, 0, 2)).reshape(CTAPS, CW),
            fdw=jnp.transpose(g_fdw[:, l], (1, 0, 2)).reshape(FTAPS, 2 * dff),
            fdwb=ffn_dw_b[l].reshape(1, 2 * dff))

    raw, h, hb = fwd("embed_ln", 0, _embed_ln, x, meta_full, in_ln_g, in_ln_b, nblk)
    saved, lw = [], []
    for l in range(DEPTH):
        p = layer_weights(l)
        z = fwd("in_proj", l, functools.partial(_mm_bias, "in_proj"), hb, p["win"], p["bin"], IN_COLS // 3, tm)
        a = fwd("attn_fwd", l, _attn_fwd, z, bias_tab, attn_sinks[l], nex, nblk)
        ccv, cs = fwd("conv_fwd", l, _conv_fwd, z, p["cdw"], conv_dw_b[l], conv_ln_g[l], conv_ln_b[l], nex, tp)
        p["wap"], p["wcp"] = gathered[("ap", l)], gathered[("cp", l)]
        ya, yc, mixed = fwd("mix_fwd", l, _mix_fwd, a, cs, p["wap"], p["wcp"], z, tm)
        p["wout"] = gathered[("out", l)].reshape(D, D)
        r1, h1, h1b = fwd("out_proj_ln", l, functools.partial(_mm_res_ln, "out_proj_ln"), mixed, p["wout"], h, ln1_g[l], ln1_b[l], tm)
        p["wup"] = gathered[("up", l)]
        up3, ug, act = fwd("ffn_up_act", l, _ffn_up_act, h1b, p["wup"], p["fdw"], p["fdwb"], tm, nex, tp)
        p["wdown"] = gathered[("down", l)].reshape(dff, D)
        r2, h2, h2b = _mm_res_ln("ffn_down_ln", act, p["wdown"], h1, ln2_g[l], ln2_b[l], tm)
        saved.append(dict(hb=hb, z=z, a=a, cc=ccv, cs=cs, ya=ya, yc=yc, mixed=mixed, r1=r1, h1b=h1b, up3=up3, ug=ug, act=act, r2=r2))
        lw.append(p)
        h, hb = h2, h2b

    dy, sq = _loss_grad(h, loss_target, nblk)

    grads, swapped, pair_sums, reduced = {}, {}, {}, {}
    cvec, where = core.reshape(1), jnp.stack([chip, core])
    last = [(k, DEPTH - 1) for k in names]
    bwd_plan = {("ln2_bwd", 0): ("swap", last),
                ("ffn_bwd", 0): ("exch", [("up", 1), ("down", 1)]),
                ("dw_down", 0): ("exch", [("ap", 1), ("cp", 1), ("out", 1)]),
                ("dw_up", 0): ("exch", [("in", 1)]),
                ("ln1_bwd", 0): ("swap", [("down", 0), ("up", 0)]),
                ("conv_bwd_b", 0): ("swap", [("out", 0), ("ap", 0), ("cp", 0)]),
                ("attn_bwd", 0): ("exch", [("down", 0), ("up", 0)]),
                ("dw_in_gates", 0): ("exch", [("out", 0), ("ap", 0), ("cp", 0)]),
                ("in_ln_bwd", 0): ("swap", [("in", 0)])}

    def after(kind, keys, outs):
        for key, o in zip(keys, outs):
            if kind == "swap":
                swapped[key] = o
                pair_sums[key] = _add_half("grad_add_sibling", grads[key], o, cvec)
            else:
                reduced[key] = _add_chips("grad_add_chips", grads[key], swapped[key], o, where)

    def bwd(tag, l, fn, *args):
        plan = bwd_plan.get((tag, l))
        if plan is None:
            return fn(*args)
        kind, keys = plan
        job = _swap_job([grads[k] for k in keys]) if kind == "swap" else _exchange_job([pair_sums[k] for k in keys])
        res = fn(*args, side=job)
        after(kind, keys, res[-len(keys):])
        main = res[:-len(keys)]
        return main[0] if len(main) == 1 else main

    small = {}
    prev_a, prev_w, prev_cols = (), None, None
    dprev = dy
    for l in reversed(range(DEPTH)):
        p, s = lw[l], saved[l]
        dr2, dr2b, dg2, db2 = bwd("ln2_bwd", l, functools.partial(_ln_bwd_call, "ln2_bwd"), dprev, s["r2"], ln2_g[l], tm,
                                  prev_a, prev_w, prev_cols)
        dpre3, dfdw, dfdwb = bwd("ffn_bwd", l, _ffn_bwd, dr2b, p["wdown"], s["ug"], s["up3"], p["fdw"], tm, nex, tp)
        grads[("down", l)] = bwd("dw_down", l, functools.partial(_mm_tn, "dw_down"), s["act"], dr2b, ffs,
                                 D // 2).reshape(NCHIP, dff // NCHIP, D)
        grads[("up", l)] = bwd("dw_up", l, functools.partial(_mm_tn, "dw_up"), s["h1b"], dpre3, D, ffs,
                               lambda j: (j // 2, j % 2), True)
        dr1, dr1b, dg1, db1 = bwd("ln1_bwd", l, functools.partial(_ln_bwd_call, "ln1_bwd"), dr2, s["r1"], ln1_g[l], tm // 2,
                                  (dpre3,), p["wup"])
        dya, dyc, dzg, csg = _gate_bwd(dr1b, p["wout"], s["ya"], s["yc"], s["z"], tm)
        grads[("out", l)] = _mm_tn("dw_out", s["mixed"], dr1b, D, D // 2).reshape(NCHIP, D // NCHIP, D)
        grads[("ap", l)] = _mm_tn("dw_attn_proj", s["a"], dya, AW, D // NCHIP, chip_out=True)
        grads[("cp", l)] = _mm_tn("dw_conv_proj", s["cs"], dyc, CW, D // NCHIP, chip_out=True)
        dcc, dclg, dclb, dcwb = _conv_bwd_a(dyc, p["wcp"], s["cc"], conv_ln_g[l], conv_ln_b[l], tm)
        dzc, dcdw, csc = bwd("conv_bwd_b", l, _conv_bwd_b, dcc, s["z"], p["cdw"], nex, tp)
        dzq, csq, dsk, dbias = bwd("attn_bwd", l, _attn_bwd, s["z"], bias_tab, attn_sinks[l], dya, p["wap"], nex, nblk)
        gin = [bwd("dw_in_gates", l, functools.partial(_mm_tn, "dw_in_gates"), s["hb"], dzg, D, D // 2),
               _mm_tn("dw_in_conv", s["hb"], dzc, D, CW),
               _mm_tn("dw_in_qkv", s["hb"], dzq, D, 2 * KVW)]
        grads[("in", l)] = _w_in_to_chips(gin)
        small[l] = dict(
            b_in=_to_old(jnp.concatenate([csg, csc, csq], axis=1)).reshape(IN_COLS), attn_sinks=dsk[0, :NQ],
            conv_dw=dcdw[:CTAPS], conv_dw_b=dcwb.reshape(CW), conv_ln_g=dclg.reshape(CW), conv_ln_b=dclb.reshape(CW),
            ln1_g=dg1.reshape(D), ln1_b=db1.reshape(D),
            ffn_dw=jnp.transpose(dfdw, (1, 0, 2)).reshape(FTAPS, 2 * dff), ffn_dw_b=jnp.transpose(dfdwb, (1, 0, 2)).reshape(2 * dff),
            ln2_g=dg2.reshape(D), ln2_b=db2.reshape(D), bias=dbias)
        dprev = dr1
        prev_a, prev_w, prev_cols = (dzg, dzc, dzq), p["win"], [(C_GATES, C_CONV), (C_CONV, C_QKV), (C_QKV, IN_COLS)]
    draw, _, dg0, db0 = bwd("in_ln_bwd", 0, functools.partial(_ln_bwd_call, "in_ln_bwd"), dprev, raw, in_ln_g, tm,
                            prev_a, prev_w, prev_cols)
    draw3 = draw.reshape(nex, tp, D)
    grad_x = draw3[:, BLK:, :]
    dmeta = _sum0("meta_grad_sum", draw3[:, PAD:BLK, :])

    names_l = ["b_in", "attn_sinks", "conv_dw", "conv_dw_b", "conv_ln_g", "conv_ln_b", "ln1_g", "ln1_b", "ffn_dw", "ffn_dw_b", "ln2_g", "ln2_b"]
    bwd_plan[("bias_grad", 0)] = ("exch", [("in", 0)])
    dbias_all = bwd("bias_grad", 0, _bias_grad, jnp.concatenate([small[l]["bias"] for l in range(DEPTH)], axis=0))
    part_list = [sq, dmeta, dg0.reshape(D), db0.reshape(D), dbias_all]
    part_list += [jnp.stack([small[0][n], small[1][n]]) for n in names_l]
    shapes_small = [tuple(a.shape) for a in part_list]
    tot = _sum0("small_grad_sum", _allgather_small(_pack(part_list)))
    (sq_all, g_meta_f, g_inlg, g_inlb, g_biasp, g_bin, g_sinks, g_cdw_f, g_cdwb, g_clg, g_clb, g_l1g, g_l1b, g_fdw_f, g_fdwb,
     g_l2g, g_l2b) = _unpack(tot, shapes_small)
    loss = 0.5 / D * jnp.sum(sq_all)
    g_relb = g_biasp
    csh = D // NCHIP
    g_meta_s = lax.dynamic_slice_in_dim(g_meta_f, chip * csh, csh, axis=1)
    g_cdw_s = lax.dynamic_slice_in_dim(g_cdw_f, chip * (CW // NCHIP), CW // NCHIP, axis=2)
    g_fdw_s = lax.dynamic_slice_in_dim(g_fdw_f, chip * ffs, ffs, axis=2)

    joined = _sibling_join([[reduced[(k, l)] for l in range(DEPTH)] for k in names])

    moms = [m_w_in, m_w_attn_proj, m_w_conv_proj, m_w_out, m_ffn_w_up, m_ffn_w_down]
    vels = [v_w_in, v_w_attn_proj, v_w_conv_proj, v_w_out, v_ffn_w_up, v_ffn_w_down]
    big_out = [_adamw_matrix(big[k], f, [reduced[(k, l)] for l in range(DEPTH)], mo, ve, cvec)
               for k, f, mo, ve in zip(names, joined, moms, vels)]

    sm_w = [meta_tokens, in_ln_g, in_ln_b, rel_bias, b_in, attn_sinks, conv_dw, conv_dw_b, conv_ln_g, conv_ln_b, ln1_g, ln1_b,
            ffn_dw, ffn_dw_b, ln2_g, ln2_b]
    sm_m = [m_meta_tokens, m_in_ln_g, m_in_ln_b, m_rel_bias, m_b_in, m_attn_sinks, m_conv_dw, m_conv_dw_b, m_conv_ln_g, m_conv_ln_b,
            m_ln1_g, m_ln1_b, m_ffn_dw, m_ffn_dw_b, m_ln2_g, m_ln2_b]
    sm_v = [v_meta_tokens, v_in_ln_g, v_in_ln_b, v_rel_bias, v_b_in, v_attn_sinks, v_conv_dw, v_conv_dw_b, v_conv_ln_g, v_conv_ln_b,
            v_ln1_g, v_ln1_b, v_ffn_dw, v_ffn_dw_b, v_ln2_g, v_ln2_b]
    sm_g = [g_meta_s, g_inlg, g_inlb, g_relb, g_bin, g_sinks, g_cdw_s, g_cdwb, g_clg, g_clb, g_l1g, g_l1b, g_fdw_s, g_fdwb, g_l2g, g_l2b]
    sm_shapes = [tuple(a.shape) for a in sm_w]
    sd, smn, svn = _adamw("adamw_small", _pack(sm_w), _pack(sm_g), _pack(sm_m), _pack(sm_v))
    sd, smn, svn = _unpack(sd, sm_shapes), _unpack(smn, sm_shapes), _unpack(svn, sm_shapes)

    order = ["meta_tokens", "in_ln_g", "in_ln_b", "rel_bias", "w_in", "b_in", "attn_sinks", "w_attn_proj", "conv_dw", "conv_dw_b",
             "conv_ln_g", "conv_ln_b", "w_conv_proj", "w_out", "ln1_g", "ln1_b", "ffn_w_up", "ffn_dw", "ffn_dw_b", "ffn_w_down",
             "ln2_g", "ln2_b"]
    small_names = ["meta_tokens", "in_ln_g", "in_ln_b", "rel_bias", "b_in", "attn_sinks", "conv_dw", "conv_dw_b", "conv_ln_g",
                   "conv_ln_b", "ln1_g", "ln1_b", "ffn_dw", "ffn_dw_b", "ln2_g", "ln2_b"]
    big_names = ["w_in", "w_attn_proj", "w_conv_proj", "w_out", "ffn_w_up", "ffn_w_down"]
    res = {}
    for i, n in enumerate(small_names):
        res[n] = (sm_g[i], sd[i], smn[i], svn[i])
    for i, n in enumerate(big_names):
        res[n] = big_out[i]
    outs = [loss, grad_x]
    for k in range(4):
        outs += [res[n][k] for n in order]
    return tuple(outs)
```

```python
import functools
import math
from typing import Any, Callable, NamedTuple, Sequence

import numpy as np
import jax
import jax.numpy as jnp
from jax import lax
from jax.experimental import pallas as pl
from jax.experimental.pallas import tpu as pltpu

F32 = jnp.float32
BF16 = jnp.bfloat16
MESH = pl.DeviceIdType.MESH

D = 1024
N_META = 16
BLK = 128
PAD = BLK - N_META
HD = 64
NQ = 8
NKV = 2
GRP = NQ // NKV
AW = NQ * HD
KVW = NKV * HD
CW = D // 2
CTAPS = 31
FTAPS = 3
NBUCKET = 32
MAXDIST = 128
EPS = 1e-5
DEPTH = 2
ALPHA = (2.0 * DEPTH) ** 0.25
NCHIP = 4
NKEY = 3 * BLK
NEG = -1e30
CHALO = 32
FHALO = 8
IN_COLS = AW + 2 * KVW + 2 * CW + 2 * D
_OLD = dict(q=(0, AW), k=(AW, AW + KVW), v=(AW + KVW, AW + 2 * KVW), cv=(AW + 2 * KVW, AW + 2 * KVW + CW),
            cg=(AW + 2 * KVW + CW, AW + 2 * KVW + 2 * CW), ga=(AW + 2 * KVW + 2 * CW, AW + 2 * KVW + 2 * CW + D),
            gc=(AW + 2 * KVW + 2 * CW + D, IN_COLS))
_NEW_ORDER = ("ga", "gc", "cv", "cg", "q", "k", "v")
C_GATES, C_CONV, C_QKV = 0, 2 * D, 2 * D + 2 * CW

ADAM_LR, ADAM_B1, ADAM_B2, ADAM_EPS, ADAM_WD, ADAM_STEP = 0.001, 0.9, 0.999, 1e-08, 0.01, 10


def _to_new(a):
    return jnp.concatenate([a[..., _OLD[n][0]:_OLD[n][1]] for n in _NEW_ORDER], axis=-1)


def _to_old(a):
    offs, o = {}, 0
    for n in _NEW_ORDER:
        w = _OLD[n][1] - _OLD[n][0]
        offs[n] = (o, o + w)
        o += w
    return jnp.concatenate([a[..., offs[n][0]:offs[n][1]] for n in ("q", "k", "v", "cv", "cg", "ga", "gc")], axis=-1)


def _new_starts():
    starts, o = {}, 0
    for n in _NEW_ORDER:
        starts[n] = o
        o += _OLD[n][1] - _OLD[n][0]
    return starts


def _w_in_from_chips(g4):
    cs = IN_COLS // NCHIP
    pieces = []
    for n in _NEW_ORDER:
        lo, hi = _OLD[n]
        while lo < hi:
            chip = lo // cs
            end = min(hi, (chip + 1) * cs)
            pieces.append(g4[chip][:, lo - chip * cs:end - chip * cs])
            lo = end
    return jnp.concatenate(pieces, axis=1)


def _w_in_to_chips(parts):
    cs = IN_COLS // NCHIP
    starts = _new_starts()
    bounds, o = [], 0
    for p in parts:
        bounds.append((o, o + p.shape[1], p))
        o += p.shape[1]

    def new_cols(a, b):
        out = []
        for s, e, p in bounds:
            lo, hi = max(a, s), min(b, e)
            if lo < hi:
                out.append(p[:, lo - s:hi - s])
        return out

    slabs = []
    for chip in range(NCHIP):
        pieces = []
        for n in ("q", "k", "v", "cv", "cg", "ga", "gc"):
            lo, hi = max(_OLD[n][0], chip * cs), min(_OLD[n][1], (chip + 1) * cs)
            if lo < hi:
                pieces += new_cols(starts[n] + lo - _OLD[n][0], starts[n] + hi - _OLD[n][0])
        slabs.append(jnp.concatenate(pieces, axis=1))
    return jnp.stack(slabs)


class _Job(NamedTuple):
    ins: Sequence[Any]
    outs: Sequence[Any]
    sems: tuple
    start: Callable
    finish: Callable


_ANY = pl.BlockSpec(memory_space=pl.ANY)


def _call(body, name, out_shape, grid, in_specs, out_specs, scratch=(), prefetch=0, side=None):
    params = pltpu.CompilerParams(dimension_semantics=("arbitrary",) * len(grid))
    if side is None:
        if prefetch:
            gs = pltpu.PrefetchScalarGridSpec(num_scalar_prefetch=prefetch, grid=grid, in_specs=in_specs,
                                              out_specs=out_specs, scratch_shapes=list(scratch))
            return pl.pallas_call(body, name=name, out_shape=out_shape, grid_spec=gs, compiler_params=params)
        return pl.pallas_call(body, name=name, out_shape=out_shape, grid=grid, in_specs=in_specs, out_specs=out_specs,
                              scratch_shapes=list(scratch), compiler_params=params)
    assert not prefetch
    single = not isinstance(out_shape, (tuple, list))
    main_shapes = (out_shape,) if single else tuple(out_shape)
    main_specs = (out_specs,) if single else tuple(out_specs)
    n_in, n_sin, n_out, n_sout, n_scr = len(in_specs), len(side.ins), len(main_shapes), len(side.outs), len(scratch)

    def wrapped(*refs):
        main_in, sin = refs[:n_in], refs[n_in:n_in + n_sin]
        o0 = n_in + n_sin
        main_out, sout = refs[o0:o0 + n_out], refs[o0 + n_out:o0 + n_out + n_sout]
        s0 = o0 + n_out + n_sout
        main_scr, (send, recv) = refs[s0:s0 + n_scr], refs[s0 + n_scr:]
        first = functools.reduce(lambda a, b: a & b, [pl.program_id(k) == 0 for k in range(len(grid))])
        last = functools.reduce(lambda a, b: a & b, [pl.program_id(k) == grid[k] - 1 for k in range(len(grid))])

        @pl.when(first)
        def _():
            side.start(sin, sout, send, recv)

        body(*main_in, *main_out, *main_scr)

        @pl.when(last)
        def _():
            side.finish(sin, sout, send, recv)

    call = pl.pallas_call(
        wrapped, name=name, out_shape=main_shapes + tuple(side.outs), grid=grid,
        in_specs=list(in_specs) + [_ANY] * n_sin, out_specs=main_specs + tuple([_ANY] * n_sout),
        scratch_shapes=list(scratch) + [pltpu.SemaphoreType.DMA(side.sems), pltpu.SemaphoreType.DMA(side.sems)],
        compiler_params=params)
    return lambda *args: call(*args, *side.ins)


def _row_tile(m):
    best = 32
    for t in range(32, 641, 32):
        if m % t == 0:
            best = t
    return best


def _pad_rows(rows, nex, tp):
    m = rows < PAD
    for b in range(1, nex):
        m = m | ((rows >= b * tp) & (rows < b * tp + PAD))
    return m


def _ln_stats(x):
    mu = jnp.mean(x, axis=-1, keepdims=True)
    xc = x - mu
    var = jnp.mean(xc * xc, axis=-1, keepdims=True)
    rstd = lax.rsqrt(var + EPS)
    return xc * rstd, rstd


def _ln_bwd(dy, xhat, rstd, g):
    dxh = dy * g
    m1 = jnp.mean(dxh, axis=-1, keepdims=True)
    m2 = jnp.mean(dxh * xhat, axis=-1, keepdims=True)
    return rstd * (dxh - m1 - xhat * m2)


def _dot(a, b):
    return jnp.dot(a, b, preferred_element_type=F32)


def _dot_nt(a, b):
    return lax.dot_general(a, b, (((1,), (1,)), ((), ())), preferred_element_type=F32)


def _dot_tn(a, b):
    return lax.dot_general(a, b, (((0,), (0,)), ((), ())), preferred_element_type=F32)


def _sigmoid(x):
    return 1.0 / (1.0 + jnp.exp(-x))


def _bucket_np(d):
    n = np.maximum(d, 0)
    max_exact = NBUCKET // 2
    nf = np.maximum(n, 1).astype(np.float32)
    large = max_exact + (np.log(nf / np.float32(max_exact)) / np.float32(math.log(MAXDIST / max_exact))
                         * np.float32(NBUCKET - max_exact)).astype(np.int32)
    large = np.minimum(large, NBUCKET - 1)
    return np.where(n < max_exact, n, large).astype(np.int32)


def _bias_index():
    i = np.arange(BLK)[:, None]
    j = np.arange(2 * BLK)[None, :]
    d = BLK + i - j
    band_ok = (d >= 0) & (d < BLK)
    band = _bucket_np(d)
    idx = np.full((3, BLK, NKEY), -1, np.int32)
    m = np.arange(N_META)[None, :]
    d0 = (i - PAD) - m
    idx[0, :, 2 * BLK:2 * BLK + N_META] = np.where(d0 >= 0, _bucket_np(d0), -1)
    ok1 = band_ok & (j >= BLK)
    idx[1, :, :2 * BLK] = np.where(ok1, band, -1)
    idx[1, :, 2 * BLK:2 * BLK + N_META] = _bucket_np((N_META + i) - m)
    idx[2, :, :2 * BLK] = np.where(band_ok, band, -1)
    idx[2, :, 2 * BLK:2 * BLK + N_META] = NBUCKET - 1
    return idx


def _bias_build(rel_bias):
    idx = jnp.asarray(_bias_index())

    def body(idx_ref, rb_ref, o_ref):
        ix = idx_ref[...]
        for h in range(NQ):
            acc = jnp.full(ix.shape, NEG, F32)
            for b in range(NBUCKET):
                acc = jnp.where(ix == b, rb_ref[b, h], acc)
            o_ref[:, h, :, :] = acc

    return pl.pallas_call(
        body, name="bias_build", out_shape=jax.ShapeDtypeStruct((3, NQ, BLK, NKEY), F32),
        in_specs=[pl.BlockSpec(memory_space=pltpu.VMEM), pl.BlockSpec(memory_space=pltpu.SMEM)],
        out_specs=pl.BlockSpec(memory_space=pltpu.VMEM))(idx, rel_bias)


def _bias_grad(dbias, side=None):
    idx = jnp.asarray(_bias_index())

    def body(idx_ref, d_ref, o_ref):
        d = jnp.sum(d_ref[...], axis=0)
        for b in range(NBUCKET):
            acc = jnp.zeros((NQ, NKEY), F32)
            for case in range(3):
                hit = (idx_ref[case] == b)[None, :, :]
                acc = acc + jnp.sum(jnp.where(hit, d[case], 0.0), axis=1)
            o_ref[b] = jnp.sum(acc, axis=-1, keepdims=True)

    res = _call(body, "bias_grad", jax.ShapeDtypeStruct((NBUCKET, NQ, 1), F32), (1,),
                [pl.BlockSpec(idx.shape, lambda i: (0, 0, 0)), pl.BlockSpec(dbias.shape, lambda i: (0, 0, 0, 0, 0))],
                pl.BlockSpec((NBUCKET, NQ, 1), lambda i: (0, 0, 0)), side=side)(idx, dbias)
    if side is None:
        return res.reshape(NBUCKET, NQ)
    return (res[0].reshape(NBUCKET, NQ),) + tuple(res[1:])


def _embed_ln(x, meta, g, b, nblk, side=None):
    nex, seq, _ = x.shape
    m = nex * nblk * BLK

    def body(x_ref, meta_ref, g_ref, b_ref, raw_ref, h_ref, hb_ref):
        j = pl.program_id(1)

        @pl.when(j == 0)
        def _():
            raw_ref[0:PAD, :] = jnp.zeros((PAD, D), F32)
            raw_ref[PAD:BLK, :] = meta_ref[...]

        @pl.when(j > 0)
        def _():
            raw_ref[...] = x_ref[...]

        xhat, _ = _ln_stats(raw_ref[...])
        y = xhat * g_ref[...] + b_ref[...]
        h_ref[...] = y
        hb_ref[...] = y.astype(BF16)

    row = lambda bb, j: (bb * nblk + j, 0)
    return _call(
        body, "embed_ln",
        (jax.ShapeDtypeStruct((m, D), F32), jax.ShapeDtypeStruct((m, D), F32), jax.ShapeDtypeStruct((m, D), BF16)),
        (nex, nblk),
        [pl.BlockSpec((None, BLK, D), lambda bb, j: (bb, jnp.maximum(j - 1, 0), 0)),
         pl.BlockSpec((N_META, D), lambda bb, j: (0, 0)),
         pl.BlockSpec((1, D), lambda bb, j: (0, 0)), pl.BlockSpec((1, D), lambda bb, j: (0, 0))],
        (pl.BlockSpec((BLK, D), row), pl.BlockSpec((BLK, D), row), pl.BlockSpec((BLK, D), row)), side=side,
    )(x, meta, g.reshape(1, D), b.reshape(1, D))


def _mm_bias(name, a, w, bias, tn, tm, side=None):
    m, k = a.shape
    n = w.shape[1]

    def body(a_ref, w_ref, b_ref, o_ref):
        o_ref[...] = _dot(a_ref[...], w_ref[...]) + b_ref[...]

    return _call(body, name, jax.ShapeDtypeStruct((m, n), F32), (n // tn, m // tm),
                 [pl.BlockSpec((tm, k), lambda j, i: (i, 0)), pl.BlockSpec((k, tn), lambda j, i: (0, j)),
                  pl.BlockSpec((1, tn), lambda j, i: (0, j))],
                 pl.BlockSpec((tm, tn), lambda j, i: (i, j)), side=side)(a, w, bias)


def _ffn_up_act(a, w4, cw, cb, tm, nex, tp, side=None):
    m, k = a.shape
    ffs = w4.shape[2]

    def body(a_ref, wu_ref, wg_ref, cu_ref, cg_ref, bu_ref, bg_ref, up_ref, ug_ref, act_ref, win):
        i = pl.program_id(1)

        @pl.when(i == 0)
        def _():
            win[:, 0:FHALO, :] = jnp.zeros((2, FHALO, ffs), F32)

        rows = i * tm + lax.broadcasted_iota(jnp.int32, (tm, 1), 0)
        pad = _pad_rows(rows, nex, tp)
        av = a_ref[...]
        for p, w_ref in ((0, wu_ref), (1, wg_ref)):
            x = jnp.where(pad, 0.0, _dot(av, w_ref[...]))
            win[p, FHALO:FHALO + tm, :] = x
            up_ref[p] = x.astype(BF16)
        for r0 in range(0, tm, RCH):
            for c0, c1 in _lane_groups(ffs):
                u = _conv3(win, 0, r0, c0, c1, cu_ref, bu_ref)
                g = _conv3(win, 1, r0, c0, c1, cg_ref, bg_ref)
                ug_ref[0, r0:r0 + RCH, c0:c1] = u.astype(BF16)
                ug_ref[1, r0:r0 + RCH, c0:c1] = g.astype(BF16)
                act_ref[r0:r0 + RCH, c0:c1] = (g * (0.5 * (1.0 + lax.erf(g * (1.0 / math.sqrt(2.0))))) * u).astype(BF16)
        win[:, 0:FHALO, :] = win[:, tm:tm + FHALO, :]

    wide = jax.ShapeDtypeStruct((2, m, 2 * ffs), BF16)
    return _call(body, "ffn_up_act", (wide, wide, jax.ShapeDtypeStruct((m, 2 * ffs), BF16)),
                 (2, m // tm),
                 [pl.BlockSpec((tm, k), lambda c, i: (i, 0)),
                  pl.BlockSpec((None, k, ffs), lambda c, i: (c, 0, 0)), pl.BlockSpec((None, k, ffs), lambda c, i: (c + 2, 0, 0)),
                  pl.BlockSpec((FTAPS, ffs), lambda c, i: (0, c)), pl.BlockSpec((FTAPS, ffs), lambda c, i: (0, c + 2)),
                  pl.BlockSpec((1, ffs), lambda c, i: (0, c)), pl.BlockSpec((1, ffs), lambda c, i: (0, c + 2))],
                 (pl.BlockSpec((2, tm, ffs), lambda c, i: (0, i, c)), pl.BlockSpec((2, tm, ffs), lambda c, i: (0, i, c)),
                  pl.BlockSpec((tm, ffs), lambda c, i: (i, c))),
                 scratch=[pltpu.VMEM((2, FHALO + tm, ffs), F32)], side=side)(a, w4, w4, cw, cw, cb, cb)


def _fill_kv(ks, vs, e, prev_ref, cur_ref, meta_ref):
    for piece, lo, n in ((prev_ref, 0, BLK), (cur_ref, BLK, BLK), (meta_ref, 2 * BLK, N_META)):
        val = piece[e]
        for hk in range(NKV):
            ks[e, hk, lo:lo + n, :] = val[:, hk * HD:(hk + 1) * HD].astype(BF16)
            vs[e, hk, lo:lo + n, :] = val[:, KVW + hk * HD:KVW + (hk + 1) * HD].astype(BF16)
    for hk in range(NKV):
        ks[e, hk, 2 * BLK + N_META:NKEY, :] = jnp.zeros((BLK - N_META, HD), BF16)
        vs[e, hk, 2 * BLK + N_META:NKEY, :] = jnp.zeros((BLK - N_META, HD), BF16)


def _interleave(chains):
    live = list(chains)
    while live:
        for c in list(live):
            try:
                next(c)
            except StopIteration:
                live.remove(c)


def _head_softmax(q, ks_hk, bias_ref, sink_ref, h, out):
    qh = q[:, h * HD:(h + 1) * HD].astype(BF16)
    yield
    s = _dot_nt(qh, ks_hk) * (HD ** -0.5) + bias_ref[h]
    yield
    sink = sink_ref[0, h]
    mx = jnp.maximum(jnp.max(s, axis=-1, keepdims=True), sink)
    yield
    p = jnp.exp(s - mx)
    es = jnp.exp(sink - mx)
    yield
    inv = 1.0 / (jnp.sum(p, axis=-1, keepdims=True) + es)
    yield
    out["q"], out["p"], out["sink"] = qh, p * inv, es * inv
    yield


def _attn_specs(nex, blk_of):
    qcol, kvcol = (C_QKV) // AW, (C_QKV + AW) // (2 * KVW)
    return [
        pl.BlockSpec((nex, BLK, AW), lambda j: (0, blk_of(j), qcol)),
        pl.BlockSpec((nex, BLK, 2 * KVW), lambda j: (0, blk_of(j), kvcol)),
        pl.BlockSpec((nex, BLK, 2 * KVW), lambda j: (0, jnp.maximum(blk_of(j) - 1, 0), kvcol)),
        pl.BlockSpec((nex, N_META, 2 * KVW), lambda j: (0, PAD // N_META, kvcol)),
        pl.BlockSpec((None, NQ, BLK, NKEY), lambda j: (jnp.minimum(blk_of(j), 2), 0, 0, 0)),
        pl.BlockSpec(memory_space=pltpu.SMEM),
    ]


def _attn_fwd(z, bias, sinks, nex, nblk, side=None):
    m = z.shape[0]
    z3 = z.reshape(nex, nblk * BLK, z.shape[1])

    def head(e, h, q, ks, vs, bias_ref, sink_ref, oacc):
        out = {}
        yield from _head_softmax(q, ks[e, h // GRP], bias_ref, sink_ref, h, out)
        o = _dot(out["p"].astype(BF16), vs[e, h // GRP])
        yield
        oacc[e, :, h * HD:(h + 1) * HD] = o
        yield

    def body(q_ref, cur_ref, prev_ref, meta_ref, bias_ref, sink_ref, o_ref, ks, vs, oacc):
        for e in range(nex):
            _fill_kv(ks, vs, e, prev_ref, cur_ref, meta_ref)
        _interleave([head(e, h, q_ref[e], ks, vs, bias_ref, sink_ref, oacc) for e in range(nex) for h in range(NQ)])
        for e in range(nex):
            o_ref[e] = oacc[e].astype(BF16)

    res = _call(body, "attn_fwd", jax.ShapeDtypeStruct((nex, nblk * BLK, AW), BF16), (nblk,),
                _attn_specs(nex, lambda j: j),
                pl.BlockSpec((nex, BLK, AW), lambda j: (0, j, 0)),
                scratch=[pltpu.VMEM((nex, NKV, NKEY, HD), BF16), pltpu.VMEM((nex, NKV, NKEY, HD), BF16),
                         pltpu.VMEM((nex, BLK, AW), F32)], side=side)(z3, z3, z3, z3, bias, sinks.reshape(1, NQ))
    if side is None:
        return res.reshape(m, AW)
    return (res[0].reshape(m, AW),) + tuple(res[1:])


def _cgate(cv, cg, rows, nex, tp):
    return jnp.where(_pad_rows(rows, nex, tp), 0.0, cv * _sigmoid(cg))


CLANES = 256


def _rolled_up(blk, b):
    return blk if b == 0 else pltpu.roll(blk, blk.shape[0] - b, axis=0)


def _conv_fwd(z, w, wb, g, b, nex, tp, side=None):
    m = z.shape[0]
    tm = BLK
    sub = tm // CHALO
    cvc, cgc = C_CONV // CW, C_CONV // CW + 1

    def body(cv_ref, cg_ref, cvh_ref, cgh_ref, w_ref, wb_ref, g_ref, b_ref, cc_ref, cs_ref, win):
        i = pl.program_id(0)
        rows = i * tm + lax.broadcasted_iota(jnp.int32, (tm, 1), 0)
        hrows = i * tm - CHALO + lax.broadcasted_iota(jnp.int32, (CHALO, 1), 0)
        win[0:CHALO, :] = _cgate(cvh_ref[...], cgh_ref[...], hrows, nex, tp)
        win[CHALO:CHALO + tm, :] = _cgate(cv_ref[...], cg_ref[...], rows, nex, tp)
        for sb in range(sub):
            lo = sb * CHALO
            for c0 in range(0, CW, CLANES):
                blk = win[lo:lo + 2 * CHALO, c0:c0 + CLANES]
                acc = jnp.zeros((CHALO, CLANES), F32) + wb_ref[:, c0:c0 + CLANES]
                for b in range(8):
                    rb = _rolled_up(blk, b)
                    for a in range(5):
                        s = 8 * a + b
                        if 2 <= s <= CTAPS + 1:
                            acc = acc + w_ref[s - 2:s - 1, c0:c0 + CLANES] * rb[8 * a:8 * a + CHALO]
                cc_ref[lo:lo + CHALO, c0:c0 + CLANES] = acc
        xhat, _ = _ln_stats(cc_ref[...])
        cl = xhat * g_ref[...] + b_ref[...]
        cs_ref[...] = (cl * _sigmoid(cl)).astype(BF16)

    halo = lambda i: jnp.maximum(i * sub - 1, 0)
    vec = pl.BlockSpec((1, CW), lambda i: (0, 0))
    return _call(body, "conv_fwd", (jax.ShapeDtypeStruct((m, CW), F32), jax.ShapeDtypeStruct((m, CW), BF16)),
                 (m // tm,),
                 [pl.BlockSpec((tm, CW), lambda i: (i, cvc)), pl.BlockSpec((tm, CW), lambda i: (i, cgc)),
                  pl.BlockSpec((CHALO, CW), lambda i: (halo(i), cvc)), pl.BlockSpec((CHALO, CW), lambda i: (halo(i), cgc)),
                  pl.BlockSpec((CTAPS, CW), lambda i: (0, 0)), vec, vec, vec],
                 (pl.BlockSpec((tm, CW), lambda i: (i, 0)), pl.BlockSpec((tm, CW), lambda i: (i, 0))),
                 scratch=[pltpu.VMEM((CHALO + tm, CW), F32)], side=side)(z, z, z, z, w, wb.reshape(1, CW), g.reshape(1, CW), b.reshape(1, CW))


def _mix_fwd(a, cs, wap4, wcp4, z, tm, side=None):
    m = a.shape[0]
    ns = wap4.shape[2]

    def body(a_ref, cs_ref, wa_ref, wc_ref, ga_ref, gc_ref, ya_ref, yc_ref, mix_ref):
        av, cv = a_ref[...], cs_ref[...]
        for j in range(NCHIP):
            cols = slice(j * ns, (j + 1) * ns)
            ya, yc = _dot(av, wa_ref[j]), _dot(cv, wc_ref[j])
            ya_ref[:, cols] = ya.astype(BF16)
            yc_ref[:, cols] = yc.astype(BF16)
            mix_ref[:, cols] = (_sigmoid(ga_ref[:, cols]) * ya + _sigmoid(gc_ref[:, cols]) * yc).astype(BF16)

    wspec = pl.BlockSpec((NCHIP, AW, ns), lambda i: (0, 0, 0))
    row = lambda i: (i, 0)
    return _call(body, "mix_fwd",
                 (jax.ShapeDtypeStruct((m, D), BF16), jax.ShapeDtypeStruct((m, D), BF16), jax.ShapeDtypeStruct((m, D), BF16)),
                 (m // tm,),
                 [pl.BlockSpec((tm, AW), row), pl.BlockSpec((tm, CW), row), wspec, wspec,
                  pl.BlockSpec((tm, D), lambda i: (i, 0)), pl.BlockSpec((tm, D), lambda i: (i, 1))],
                 (pl.BlockSpec((tm, D), row), pl.BlockSpec((tm, D), row), pl.BlockSpec((tm, D), row)), side=side)(a, cs, wap4, wcp4, z, z)


def _mm_res_ln(name, a, w, res, g, b, tm, side=None):
    m, k = a.shape
    nband = 2 if tm % 32 == 0 else 1

    def rows(lo, hi, a_ref, w_ref, res_ref, g_ref, b_ref, r_ref, h_ref, hb_ref):
        acc = _dot(a_ref[lo:hi, :], w_ref[...])
        yield
        r = ALPHA * res_ref[lo:hi, :] + acc
        r_ref[lo:hi, :] = r
        yield
        xhat, _ = _ln_stats(r)
        yield
        y = xhat * g_ref[...] + b_ref[...]
        h_ref[lo:hi, :] = y
        hb_ref[lo:hi, :] = y.astype(BF16)
        yield

    def body(*refs):
        _interleave([rows(lo, lo + tm // nband, *refs) for lo in range(0, tm, tm // nband)])

    row = lambda i: (i, 0)
    vec = pl.BlockSpec((1, D), lambda i: (0, 0))
    return _call(body, name,
                 (jax.ShapeDtypeStruct((m, D), F32), jax.ShapeDtypeStruct((m, D), F32), jax.ShapeDtypeStruct((m, D), BF16)),
                 (m // tm,),
                 [pl.BlockSpec((tm, k), row), pl.BlockSpec((k, D), lambda i: (0, 0)), pl.BlockSpec((tm, D), row), vec, vec],
                 (pl.BlockSpec((tm, D), row), pl.BlockSpec((tm, D), row), pl.BlockSpec((tm, D), row)), side=side)(a, w, res, g.reshape(1, D), b.reshape(1, D))


RCH = 16


def _lane_groups(width, most=768):
    n = -(-width // most)
    step = -(-width // (128 * n)) * 128
    return [(c, min(c + step, width)) for c in range(0, width, step)]


def _conv3(win, p, r0, c0, c1, w_ref, b_ref):
    blk = win[p, r0:r0 + FHALO + RCH, c0:c1]
    x0, x1, x2 = blk[FHALO:], pltpu.roll(blk, 1, axis=0)[FHALO:], pltpu.roll(blk, 2, axis=0)[FHALO:]
    return b_ref[:, c0:c1] + w_ref[0:1, c0:c1] * x2 + w_ref[1:2, c0:c1] * x1 + w_ref[2:3, c0:c1] * x0


def _loss_grad(y, target, nblk):
    nex = target.shape[0]
    m = y.shape[0]

    def body(y_ref, t_ref, dy_ref, acc_ref):
        bb, j = pl.program_id(0), pl.program_id(1)

        @pl.when((bb == 0) & (j == 0))
        def _():
            acc_ref[...] = jnp.zeros_like(acc_ref)

        @pl.when(j == 0)
        def _():
            dy_ref[...] = jnp.zeros_like(dy_ref)

        @pl.when(j > 0)
        def _():
            e = y_ref[...] - t_ref[...]
            dy_ref[...] = e * (1.0 / D)
            acc_ref[...] += jnp.sum((e * e).reshape(BLK // 8, 8, D), axis=0)

    return _call(body, "loss_grad", (jax.ShapeDtypeStruct((m, D), F32), jax.ShapeDtypeStruct((8, D), F32)), (nex, nblk),
                 [pl.BlockSpec((BLK, D), lambda bb, j: (bb * nblk + j, 0)),
                  pl.BlockSpec((None, BLK, D), lambda bb, j: (bb, jnp.maximum(j - 1, 0), 0))],
                 (pl.BlockSpec((BLK, D), lambda bb, j: (bb * nblk + j, 0)), pl.BlockSpec((8, D), lambda bb, j: (0, 0))))(y, target)


def _ln_bwd_call(name, dy, r, g, tm, a_list=(), w=None, cols=None, side=None):
    m = dy.shape[0]
    na = len(a_list)

    nband = 2 if tm % 32 == 0 else 1

    def body(*refs):
        dy_ref, r_ref, g_ref = refs[0:3]
        a_refs = refs[3:3 + na]
        w_ref = refs[3 + na] if na else None
        dr_ref, drb_ref, dg_ref, db_ref = refs[-4:]
        i = pl.program_id(0)
        sums = []

        def band(lo, hi):
            dh = dy_ref[lo:hi, :]
            if na:
                dh = ALPHA * dh
                if cols is None:
                    ns = w.shape[2]
                    for j in range(NCHIP):
                        dh = dh + _dot_nt(a_refs[0][j // 2, lo:hi, (j % 2) * ns:(j % 2 + 1) * ns], w_ref[j])
                        yield
                else:
                    for a_ref, (c0, c1) in zip(a_refs, cols):
                        dh = dh + _dot_nt(a_ref[lo:hi, :], w_ref[:, c0:c1])
                        yield
            xhat, rstd = _ln_stats(r_ref[lo:hi, :])
            yield
            dr = _ln_bwd(dh, xhat, rstd, g_ref[...])
            yield
            dr_ref[lo:hi, :] = dr
            drb_ref[lo:hi, :] = dr.astype(BF16)
            sums.append((jnp.sum(dh * xhat, axis=0, keepdims=True), jnp.sum(dh, axis=0, keepdims=True)))
            yield

        _interleave([band(lo, lo + tm // nband) for lo in range(0, tm, tm // nband)])

        @pl.when(i == 0)
        def _():
            dg_ref[...] = jnp.zeros_like(dg_ref)
            db_ref[...] = jnp.zeros_like(db_ref)

        for sg, sb_ in sums:
            dg_ref[...] += sg
            db_ref[...] += sb_

    row = lambda i: (i, 0)
    vec = pl.BlockSpec((1, D), lambda i: (0, 0))
    in_specs = [pl.BlockSpec((tm, D), row), pl.BlockSpec((tm, D), row), vec]
    for a in a_list:
        in_specs.append(pl.BlockSpec((2, tm, a.shape[2]), lambda i: (0, i, 0)) if a.ndim == 3 else pl.BlockSpec((tm, a.shape[1]), row))
    if na:
        in_specs.append(pl.BlockSpec(w.shape, (lambda i: (0, 0, 0)) if w.ndim == 3 else (lambda i: (0, 0))))
    return _call(body, name,
                 (jax.ShapeDtypeStruct((m, D), F32), jax.ShapeDtypeStruct((m, D), BF16),
                  jax.ShapeDtypeStruct((1, D), F32), jax.ShapeDtypeStruct((1, D), F32)),
                 (m // tm,), in_specs,
                 (pl.BlockSpec((tm, D), row), pl.BlockSpec((tm, D), row), vec, vec),
                 side=side)(dy, r, g.reshape(1, D), *a_list, *([w] if na else []))


def _ffn_bwd(drb, wdown, ug, up3, w, tm, nex, tp, side=None):
    _, m, dff = ug.shape
    ffs = dff // 2
    nt = m // tm

    def body(dr_ref, wd_ref, ug_ref, x_ref, wu_ref, wg_ref, o_ref, dw_ref, db_ref, dact, carry, dwacc, dbacc):
        i = pl.program_id(1)
        tile = nt - 1 - i
        dact[...] = _dot_nt(dr_ref[...], wd_ref[...])
        dwacc[...] = jnp.zeros_like(dwacc)
        dbacc[...] = jnp.zeros_like(dbacc)

        @pl.when(i == 0)
        def _():
            carry[...] = jnp.zeros_like(carry)
            dw_ref[...] = jnp.zeros_like(dw_ref)
            db_ref[...] = jnp.zeros_like(db_ref)

        fold = lambda t: t[0:8, :] + t[8:16, :]
        for r0 in reversed(range(0, tm, RCH)):
            rows = tile * tm + r0 + lax.broadcasted_iota(jnp.int32, (RCH, 1), 0)
            pad = _pad_rows(rows, nex, tp)
            for c0, c1 in _lane_groups(ffs, 384):
                u = ug_ref[0, r0:r0 + RCH, c0:c1].astype(F32)
                g = ug_ref[1, r0:r0 + RCH, c0:c1].astype(F32)
                da = dact[r0:r0 + RCH, c0:c1]
                t = g * (1.0 / math.sqrt(2.0))
                one_erf = 1.0 + lax.erf(t)
                dens = jnp.exp(math.log(2.0 / math.sqrt(2.0 * math.pi)) - t * t)
                half = 0.5 * da
                hg = half * g
                for p, w_ref, d0 in ((0, wu_ref, hg * one_erf), (1, wg_ref, u * (half * one_erf + hg * dens))):
                    dblk = jnp.concatenate([d0, carry[p, :, c0:c1]], axis=0)
                    d1 = pltpu.roll(dblk, RCH + FHALO - 1, axis=0)[:RCH]
                    d2 = pltpu.roll(dblk, RCH + FHALO - 2, axis=0)[:RCH]
                    carry[p, :, c0:c1] = d0[0:FHALO]
                    dpre = w_ref[2:3, c0:c1] * d0 + w_ref[1:2, c0:c1] * d1 + w_ref[0:1, c0:c1] * d2
                    o_ref[p, r0:r0 + RCH, c0:c1] = jnp.where(pad, 0.0, dpre).astype(BF16)
                    x0 = x_ref[p, r0:r0 + RCH, c0:c1].astype(F32)
                    dwacc[p, 2, :, c0:c1] += fold(d0 * x0)
                    dwacc[p, 1, :, c0:c1] += fold(d1 * x0)
                    dwacc[p, 0, :, c0:c1] += fold(d2 * x0)
                    dbacc[p, :, c0:c1] += fold(d0)
        for p in range(2):
            for k in range(FTAPS):
                dw_ref[p, k:k + 1, :] += jnp.sum(dwacc[p, k], axis=0, keepdims=True)
            db_ref[p] += jnp.sum(dbacc[p], axis=0, keepdims=True)

    wide = pl.BlockSpec((2, tm, ffs), lambda c, i: (0, nt - 1 - i, c))
    return _call(body, "ffn_bwd",
                 (jax.ShapeDtypeStruct((2, m, dff), BF16), jax.ShapeDtypeStruct((2, FTAPS, dff), F32),
                  jax.ShapeDtypeStruct((2, 1, dff), F32)),
                 (2, nt),
                 [pl.BlockSpec((tm, D), lambda c, i: (nt - 1 - i, 0)), pl.BlockSpec((ffs, D), lambda c, i: (c, 0)), wide, wide,
                  pl.BlockSpec((FTAPS, ffs), lambda c, i: (0, c)), pl.BlockSpec((FTAPS, ffs), lambda c, i: (0, c + 2))],
                 (wide, pl.BlockSpec((2, FTAPS, ffs), lambda c, i: (0, 0, c)), pl.BlockSpec((2, 1, ffs), lambda c, i: (0, 0, c))),
                 scratch=[pltpu.VMEM((tm, ffs), F32), pltpu.VMEM((2, FHALO, ffs), F32),
                          pltpu.VMEM((2, FTAPS, 8, ffs), F32), pltpu.VMEM((2, 8, ffs), F32)], side=side)(drb, wdown, ug, up3, w, w)


def _mm_tn(name, a, b, tk, tn, b_cols=None, chip_out=False, side=None):
    m, k = a.shape
    n = b.shape[-1] * (2 if b.ndim == 3 else 1)

    def body(a_ref, b_ref, o_ref):
        o_ref[...] = _dot_tn(a_ref[...], b_ref[...])

    if b.ndim == 3:
        bspec = pl.BlockSpec((None, m, tn), lambda kk, j: (b_cols(j)[0], 0, b_cols(j)[1]))
    else:
        bspec = pl.BlockSpec((m, tn), lambda kk, j: (0, j))
    if chip_out:
        oshape, ospec = (n // tn, k, tn), pl.BlockSpec((None, tk, tn), lambda kk, j: (j, kk, 0))
    else:
        oshape, ospec = (k, n), pl.BlockSpec((tk, tn), lambda kk, j: (kk, j))
    return _call(body, name, jax.ShapeDtypeStruct(oshape, F32), (k // tk, n // tn),
                 [pl.BlockSpec((m, tk), lambda kk, j: (0, kk)), bspec], ospec, side=side)(a, b)


def _gate_bwd(drb, wout, ya, yc, z, tm):
    m = drb.shape[0]

    def body(dr_ref, w_ref, ya_ref, yc_ref, ga_ref, gc_ref, dya_ref, dyc_ref, dz_ref, cs_ref):
        i = pl.program_id(0)
        dmix = _dot_nt(dr_ref[...], w_ref[...])
        sa, sc = _sigmoid(ga_ref[...]), _sigmoid(gc_ref[...])
        dya_ref[...] = (dmix * sa).astype(BF16)
        dyc_ref[...] = (dmix * sc).astype(BF16)
        dga = dmix * ya_ref[...].astype(F32) * sa * (1.0 - sa)
        dgc = dmix * yc_ref[...].astype(F32) * sc * (1.0 - sc)
        dz_ref[:, 0:D] = dga.astype(BF16)
        dz_ref[:, D:2 * D] = dgc.astype(BF16)

        @pl.when(i == 0)
        def _():
            cs_ref[...] = jnp.zeros_like(cs_ref)

        cs_ref[:, 0:D] += jnp.sum(dga, axis=0, keepdims=True)
        cs_ref[:, D:2 * D] += jnp.sum(dgc, axis=0, keepdims=True)

    row = lambda i: (i, 0)
    return _call(body, "gate_bwd",
                 (jax.ShapeDtypeStruct((m, D), BF16), jax.ShapeDtypeStruct((m, D), BF16),
                  jax.ShapeDtypeStruct((m, 2 * D), BF16), jax.ShapeDtypeStruct((1, 2 * D), F32)),
                 (m // tm,),
                 [pl.BlockSpec((tm, D), row), pl.BlockSpec((D, D), lambda i: (0, 0)), pl.BlockSpec((tm, D), row),
                  pl.BlockSpec((tm, D), row), pl.BlockSpec((tm, D), lambda i: (i, 0)), pl.BlockSpec((tm, D), lambda i: (i, 1))],
                 (pl.BlockSpec((tm, D), row), pl.BlockSpec((tm, D), row), pl.BlockSpec((tm, 2 * D), row),
                  pl.BlockSpec((1, 2 * D), lambda i: (0, 0))))(drb, wout, ya, yc, z, z)


def _conv_bwd_a(dyc, wcp4, cc, g, b, tm):
    m = cc.shape[0]
    ns = wcp4.shape[2]

    def body(dy_ref, w_ref, cc_ref, g_ref, b_ref, dcc_ref, dg_ref, db_ref, dwb_ref):
        i = pl.program_id(0)
        dcs = jnp.zeros((tm, CW), F32)
        for j in range(NCHIP):
            dcs = dcs + _dot_nt(dy_ref[:, j * ns:(j + 1) * ns], w_ref[j])
        xhat, rstd = _ln_stats(cc_ref[...])
        cl = xhat * g_ref[...] + b_ref[...]
        sg = _sigmoid(cl)
        dcl = dcs * sg * (1.0 + cl * (1.0 - sg))
        dcc = _ln_bwd(dcl, xhat, rstd, g_ref[...])
        dcc_ref[...] = dcc

        @pl.when(i == 0)
        def _():
            dg_ref[...] = jnp.zeros_like(dg_ref)
            db_ref[...] = jnp.zeros_like(db_ref)
            dwb_ref[...] = jnp.zeros_like(dwb_ref)

        dg_ref[...] += jnp.sum(dcl * xhat, axis=0, keepdims=True)
        db_ref[...] += jnp.sum(dcl, axis=0, keepdims=True)
        dwb_ref[...] += jnp.sum(dcc, axis=0, keepdims=True)

    row = lambda i: (i, 0)
    vec = pl.BlockSpec((1, CW), lambda i: (0, 0))
    v = jax.ShapeDtypeStruct((1, CW), F32)
    return _call(body, "conv_bwd_a", (jax.ShapeDtypeStruct((m, CW), F32), v, v, v), (m // tm,),
                 [pl.BlockSpec((tm, D), row), pl.BlockSpec((NCHIP, CW, ns), lambda i: (0, 0, 0)), pl.BlockSpec((tm, CW), row), vec, vec],
                 (pl.BlockSpec((tm, CW), row), vec, vec, vec))(dyc, wcp4, cc, g.reshape(1, CW), b.reshape(1, CW))


def _conv_bwd_b(dcc, z, w, nex, tp, side=None):
    m = dcc.shape[0]
    tm = BLK
    sub = tm // CHALO
    nt = m // tm
    cvc, cgc = C_CONV // CW, C_CONV // CW + 1

    def body(d_ref, dh_ref, cv_ref, cg_ref, w_ref, dz_ref, dw_ref, cs_ref, dwin, dwacc):
        i = pl.program_id(0)
        rows = i * tm + lax.broadcasted_iota(jnp.int32, (tm, 1), 0)
        dwin[0:tm, :] = d_ref[...]
        dwin[tm:tm + CHALO, :] = jnp.where(i == nt - 1, 0.0, dh_ref[...])

        @pl.when(i == 0)
        def _():
            dwacc[...] = jnp.zeros_like(dwacc)
            cs_ref[...] = jnp.zeros_like(cs_ref)

        fold = lambda t: (t[0:8] + t[8:16]) + (t[16:24] + t[24:32])
        for sb in range(sub):
            lo = sb * CHALO
            pad = _pad_rows(rows[lo:lo + CHALO], nex, tp)
            for c0 in range(0, CW, CLANES):
                cs_ = slice(c0, c0 + CLANES)
                cv = cv_ref[lo:lo + CHALO, cs_]
                sg = _sigmoid(cg_ref[lo:lo + CHALO, cs_])
                cgin = jnp.where(pad, 0.0, cv * sg)
                blk = dwin[lo:lo + 2 * CHALO, cs_]
                acc = jnp.zeros((CHALO, CLANES), F32)
                for b in range(8):
                    rb = _rolled_up(blk, b)
                    for a in range(4):
                        s = 8 * a + b
                        if s <= CTAPS - 1:
                            k = CTAPS - 1 - s
                            sh = rb[8 * a:8 * a + CHALO]
                            acc = acc + w_ref[k:k + 1, cs_] * sh
                            dwacc[k, :, cs_] += fold(sh * cgin)
                dcg = jnp.where(pad, 0.0, acc)
                dcv = dcg * sg
                dgt = dcg * cv * sg * (1.0 - sg)
                dz_ref[lo:lo + CHALO, cs_] = dcv.astype(BF16)
                dz_ref[lo:lo + CHALO, CW + c0:CW + c0 + CLANES] = dgt.astype(BF16)
                cs_ref[:, cs_] += jnp.sum(dcv, axis=0, keepdims=True)
                cs_ref[:, CW + c0:CW + c0 + CLANES] += jnp.sum(dgt, axis=0, keepdims=True)

        @pl.when(i == nt - 1)
        def _():
            for k in range(CHALO):
                dw_ref[k:k + 1, :] = jnp.sum(dwacc[k], axis=0, keepdims=True)

    nxt = lambda i: jnp.minimum((i + 1) * sub, m // CHALO - 1)
    return _call(body, "conv_bwd_b",
                 (jax.ShapeDtypeStruct((m, 2 * CW), BF16), jax.ShapeDtypeStruct((CHALO, CW), F32),
                  jax.ShapeDtypeStruct((1, 2 * CW), F32)),
                 (nt,),
                 [pl.BlockSpec((tm, CW), lambda i: (i, 0)), pl.BlockSpec((CHALO, CW), lambda i: (nxt(i), 0)),
                  pl.BlockSpec((tm, CW), lambda i: (i, cvc)), pl.BlockSpec((tm, CW), lambda i: (i, cgc)),
                  pl.BlockSpec((CTAPS, CW), lambda i: (0, 0))],
                 (pl.BlockSpec((tm, 2 * CW), lambda i: (i, 0)), pl.BlockSpec((CHALO, CW), lambda i: (0, 0)),
                  pl.BlockSpec((1, 2 * CW), lambda i: (0, 0))),
                 scratch=[pltpu.VMEM((tm + CHALO, CW), F32), pltpu.VMEM((CHALO, 8, CW), F32)], side=side)(dcc, dcc, z, z, w)


def _attn_bwd(z, bias, sinks, dya, wap4, nex, nblk, side=None):
    m = z.shape[0]
    tp = nblk * BLK
    ns = wap4.shape[2]
    blk_of = lambda j: nblk - 1 - j
    z3 = z.reshape(nex, tp, z.shape[1])
    dya3 = dya.reshape(nex, tp, D)

    def body(q_ref, cur_ref, prev_ref, meta_ref, bias_ref, sink_ref, dy_ref, w_ref,
             dz_ref, cs_ref, dsk_ref, dbias_ref, ks, vs, carry, macc, dqacc, dkv, okv):
        j = pl.program_id(0)
        n = nblk - 1 - j

        @pl.when(j == 0)
        def _():
            carry[...] = jnp.zeros_like(carry)
            macc[...] = jnp.zeros_like(macc)
            cs_ref[...] = jnp.zeros_like(cs_ref)
            dsk_ref[...] = jnp.zeros_like(dsk_ref)

        @pl.when((j == 0) | (n <= 1))
        def _():
            dbias_ref[...] = jnp.zeros_like(dbias_ref)

        lane = lax.broadcasted_iota(jnp.int32, (1, BLK), 1)
        dsk = jnp.zeros((1, BLK), F32)

        def head(e, h, q, da, out):
            hk = h // GRP
            yield from _head_softmax(q, ks[e, hk], bias_ref, sink_ref, h, out)
            pn = out["p"]
            doh = da[:, h * HD:(h + 1) * HD].astype(BF16)
            yield
            dp = _dot_nt(doh, vs[e, hk])
            yield
            dl = jnp.sum(pn * dp, axis=-1, keepdims=True)
            yield
            ds = pn * (dp - dl)
            yield
            dbias_ref[h] += ds
            out["dsink"] = jnp.sum(-out["sink"] * dl)
            yield
            dsb = (ds * (HD ** -0.5)).astype(BF16)
            yield
            dqacc[e, :, h * HD:(h + 1) * HD] = _dot(dsb, ks[e, hk])
            out["ds"], out["pb"], out["do"] = dsb, pn.astype(BF16), doh
            yield

        def kv_head(e, hk, outs):
            rows = lambda key: jnp.concatenate([outs[hk * GRP + g][key] for g in range(GRP)], axis=0)
            dk = _dot_tn(rows("ds"), rows("q"))
            yield
            dv = _dot_tn(rows("pb"), rows("do"))
            yield
            dkv[e, :, hk * HD:(hk + 1) * HD] = dk
            dkv[e, :, KVW + hk * HD:KVW + (hk + 1) * HD] = dv
            yield

        das, outs = [], [[{} for _ in range(NQ)] for _ in range(nex)]
        for e in range(nex):
            _fill_kv(ks, vs, e, prev_ref, cur_ref, meta_ref)
            da = jnp.zeros((BLK, AW), F32)
            for jj in range(NCHIP):
                da = da + _dot_nt(dy_ref[e, :, jj * ns:(jj + 1) * ns], w_ref[jj])
            das.append(da)
        _interleave([head(e, h, q_ref[e], das[e], outs[e][h]) for e in range(nex) for h in range(NQ)])
        _interleave([kv_head(e, hk, outs[e]) for e in range(nex) for hk in range(NKV)])
        for e in range(nex):
            for h in range(NQ):
                dsk = dsk + jnp.where(lane == h, outs[e][h]["dsink"], 0.0)
            macc[e] += dkv[e, 2 * BLK:2 * BLK + N_META, :]
            okv[e] = dkv[e, BLK:2 * BLK, :] + carry[e]
            carry[e] = dkv[e, 0:BLK, :]

            @pl.when(n == 0)
            def _():
                okv[e, PAD:BLK, :] += macc[e]

            dq = dqacc[e]
            ok = okv[e]
            dz_ref[e, :, 0:AW] = dq.astype(BF16)
            dz_ref[e, :, AW:AW + 2 * KVW] = ok.astype(BF16)
            cs_ref[:, 0:AW] += jnp.sum(dq, axis=0, keepdims=True)
            cs_ref[:, AW:AW + 2 * KVW] += jnp.sum(ok, axis=0, keepdims=True)
        dsk_ref[...] += dsk

    wz = AW + 2 * KVW
    specs = _attn_specs(nex, blk_of) + [
        pl.BlockSpec((nex, BLK, D), lambda j: (0, blk_of(j), 0)),
        pl.BlockSpec((NCHIP, AW, ns), lambda j: (0, 0, 0))]
    res = _call(body, "attn_bwd",
                (jax.ShapeDtypeStruct((nex, tp, wz), BF16), jax.ShapeDtypeStruct((1, wz), F32), jax.ShapeDtypeStruct((1, BLK), F32),
                 jax.ShapeDtypeStruct((1, 3, NQ, BLK, NKEY), F32)),
                (nblk,), specs,
                (pl.BlockSpec((nex, BLK, wz), lambda j: (0, blk_of(j), 0)), pl.BlockSpec((1, wz), lambda j: (0, 0)),
                 pl.BlockSpec((1, BLK), lambda j: (0, 0)),
                 pl.BlockSpec((None, None, NQ, BLK, NKEY), lambda j: (0, jnp.minimum(blk_of(j), 2), 0, 0, 0))),
                scratch=[pltpu.VMEM((nex, NKV, NKEY, HD), BF16), pltpu.VMEM((nex, NKV, NKEY, HD), BF16),
                         pltpu.VMEM((nex, BLK, 2 * KVW), F32), pltpu.VMEM((nex, N_META, 2 * KVW), F32),
                         pltpu.VMEM((nex, BLK, AW), F32), pltpu.VMEM((nex, NKEY, 2 * KVW), F32),
                         pltpu.VMEM((nex, BLK, 2 * KVW), F32)],
                side=side)(z3, z3, z3, z3, bias, sinks.reshape(1, NQ), dya3, wap4)
    return (res[0].reshape(m, wz),) + tuple(res[1:])


def _tile_rows(rows, cols, target_bytes=1 << 20):
    best = None
    for t in range(8, rows + 1, 8):
        if rows % t == 0 and t * cols * 4 <= target_bytes:
            best = t
    return best or rows


def _sum0(name, x):
    n, r, c = x.shape
    tr = r if n * r * c * 4 <= (8 << 20) else _tile_rows(r, c * n)

    def body(x_ref, o_ref):
        acc = x_ref[0]
        for k in range(1, n):
            acc = acc + x_ref[k]
        o_ref[...] = acc

    return _call(body, name, jax.ShapeDtypeStruct((r, c), F32), (r // tr,),
                 [pl.BlockSpec((n, tr, c), lambda i: (0, i, 0))], pl.BlockSpec((tr, c), lambda i: (i, 0)))(x)


def _adamw(name, w, g, mom, vel):
    r, c = w.shape
    tr = _tile_rows(r, c)
    c1 = 1.0 / (1.0 - ADAM_B1 ** ADAM_STEP)
    c2 = 1.0 / (1.0 - ADAM_B2 ** ADAM_STEP)

    def body(w_ref, g_ref, m_ref, v_ref, d_ref, mo_ref, vo_ref):
        gg = g_ref[...]
        mn = ADAM_B1 * m_ref[...] + (1.0 - ADAM_B1) * gg
        vn = ADAM_B2 * v_ref[...] + (1.0 - ADAM_B2) * (gg * gg)
        mo_ref[...] = mn
        vo_ref[...] = vn
        d_ref[...] = -ADAM_LR * ((mn * c1) / (jnp.sqrt(vn * c2) + ADAM_EPS) + ADAM_WD * w_ref[...])

    spec = pl.BlockSpec((tr, c), lambda i: (i, 0))
    o = jax.ShapeDtypeStruct((r, c), F32)
    return _call(body, name, (o, o, o), (r // tr,), [spec] * 4, (spec, spec, spec))(w, g, mom, vel)


def _adamw_matrix(w, joined, own, mom, vel, core):
    _, r, c = w.shape
    h = r // 2
    th = _tile_rows(h, c, 2 << 20)
    nt = h // th
    c1 = 1.0 / (1.0 - ADAM_B1 ** ADAM_STEP)
    c2 = 1.0 / (1.0 - ADAM_B2 ** ADAM_STEP)

    def body(core_ref, w_ref, j_ref, o0_ref, o1_ref, m_ref, v_ref, g_ref, d_ref, mo_ref, vo_ref):
        layer, half = pl.program_id(0), pl.program_id(1)
        mine = jnp.where(layer == 0, o0_ref[...], o1_ref[...])
        gg = jnp.where(half == core_ref[0], mine, j_ref[...])
        g_ref[...] = gg
        mn = ADAM_B1 * m_ref[...] + (1.0 - ADAM_B1) * gg
        vn = ADAM_B2 * v_ref[...] + (1.0 - ADAM_B2) * (gg * gg)
        mo_ref[...] = mn
        vo_ref[...] = vn
        d_ref[...] = -ADAM_LR * ((mn * c1) / (jnp.sqrt(vn * c2) + ADAM_EPS) + ADAM_WD * w_ref[...])

    full = pl.BlockSpec((None, th, c), lambda l, hh, i, cr: (l, hh * nt + i, 0))
    other = pl.BlockSpec((None, th, c), lambda l, hh, i, cr: (l, (1 - cr[0]) * nt + i, 0))
    part = pl.BlockSpec((th, c), lambda l, hh, i, cr: (i, 0))
    o = jax.ShapeDtypeStruct(w.shape, F32)
    return _call(body, "adamw_matrix", (o, o, o, o), (2, 2, nt), [full, other, part, part, full, full],
                 (full, full, full, full), prefetch=1)(core, w, joined, own[0], own[1], mom, vel)


def _place():
    x, y, c = lax.axis_index("x"), lax.axis_index("y"), lax.axis_index("c")
    others = [(1 - x, y), (x, 1 - y), (1 - x, 1 - y)]
    return x, y, c, others


def _gather_job(items):
    nw = len(items)

    def views(s_ref, g_ref, layer, c):
        if layer is None:
            return s_ref, s_ref.at[c], lambda chip, cc: g_ref.at[chip, cc]
        hr = s_ref.shape[1] // 2
        return s_ref.at[layer], s_ref.at[layer, pl.ds(c * hr, hr)], lambda chip, cc: g_ref.at[chip, pl.ds(cc * hr, hr)]

    def copies(s_refs, g_refs, send, recv):
        x, y, c, others = _place()
        chip = 2 * x + y
        firsts, arrive, passed, arrive2 = [], [], [], []
        for w, (_, layer) in enumerate(items):
            whole, src, dst = views(s_refs[w], g_refs[w], layer, c)

            def rc(kk, s, d, to, w=w):
                return pltpu.make_async_remote_copy(src_ref=s, dst_ref=d, send_sem=send.at[w, kk], recv_sem=recv.at[w, kk],
                                                    device_id=to, device_id_type=MESH)
            firsts.append(rc(6, whole, g_refs[w].at[chip], (x, y, 1 - c)))
            arrive2.append(rc(6, whole, g_refs[w].at[chip], (x, y, c)))
            for k, (px, py) in enumerate(others):
                got, got2 = dst(2 * px + py, c), dst(2 * px + py, 1 - c)
                firsts.append(rc(k, src, dst(chip, c), (px, py, c)))
                arrive.append(rc(k, got, got, (x, y, c)))
                passed.append(rc(3 + k, got, got, (x, y, 1 - c)))
                arrive2.append(rc(3 + k, got2, got2, (x, y, c)))
        return firsts, arrive, passed, arrive2

    def start(s_refs, g_refs, send, recv):
        for cp in copies(s_refs, g_refs, send, recv)[0]:
            cp.start()

    def finish(s_refs, g_refs, send, recv):
        firsts, arrive, passed, arrive2 = copies(s_refs, g_refs, send, recv)
        for a, p in zip(arrive, passed):
            a.wait_recv()
            p.start()
        for a in arrive2:
            a.wait_recv()
        for cp in firsts + passed:
            cp.wait_send()

    outs = [jax.ShapeDtypeStruct((NCHIP,) + (s.shape if layer is None else s.shape[1:]), s.dtype) for s, layer in items]
    return _Job([s for s, _ in items], outs, (nw, 7), start, finish)


def _swap_job(grads):
    def copies(d_refs, a_refs, send, recv):
        x, y, c, _ = _place()
        cps = []
        for w in range(len(grads)):
            h = d_refs[w].shape[1] // 2
            cps.append(pltpu.make_async_remote_copy(
                src_ref=d_refs[w].at[:, pl.ds((1 - c) * h, h), :], dst_ref=a_refs[w], send_sem=send.at[w], recv_sem=recv.at[w],
                device_id=(x, y, 1 - c), device_id_type=MESH))
        return cps

    def start(*r):
        for cp in copies(*r):
            cp.start()

    def finish(*r):
        for cp in copies(*r):
            cp.wait()

    outs = [jax.ShapeDtypeStruct((NCHIP, g.shape[1] // 2, g.shape[2]), g.dtype) for g in grads]
    return _Job(list(grads), outs, (len(grads),), start, finish)


def _exchange_job(parts):
    def copies(q_refs, b_refs, send, recv):
        x, y, c, others = _place()
        cps = []
        for w in range(len(parts)):
            for k, (px, py) in enumerate(others):
                cps.append(pltpu.make_async_remote_copy(
                    src_ref=q_refs[w].at[2 * px + py], dst_ref=b_refs[w].at[k], send_sem=send.at[w, k], recv_sem=recv.at[w, k],
                    device_id=(px, py, c), device_id_type=MESH))
        return cps

    def start(*r):
        for cp in copies(*r):
            cp.start()

    def finish(*r):
        for cp in copies(*r):
            cp.wait()

    outs = [jax.ShapeDtypeStruct((3,) + p.shape[1:], p.dtype) for p in parts]
    return _Job(list(parts), outs, (len(parts), 3), start, finish)


def _run_job(name, job):
    n_in, n_out = len(job.ins), len(job.outs)

    def body(*refs):
        ins, outs = refs[:n_in], refs[n_in:n_in + n_out]
        send, recv = refs[n_in + n_out:]
        job.start(ins, outs, send, recv)
        job.finish(ins, outs, send, recv)

    return pl.pallas_call(
        body, name=name, out_shape=tuple(job.outs), in_specs=[_ANY] * n_in, out_specs=tuple([_ANY] * n_out),
        scratch_shapes=[pltpu.SemaphoreType.DMA(job.sems), pltpu.SemaphoreType.DMA(job.sems)])(*job.ins)


def _sibling_join(halves):
    nw = len(halves)

    def body(*refs):
        h_refs = refs[:2 * nw]
        f_refs = refs[2 * nw:3 * nw]
        send, recv = refs[3 * nw:]
        x, y, c, _ = _place()
        cps = []
        for w in range(nw):
            for l in range(2):
                src = h_refs[2 * w + l]
                h = src.shape[0]
                dst = f_refs[w].at[l, pl.ds(c * h, h), :]
                cp = pltpu.make_async_remote_copy(src_ref=src, dst_ref=dst, send_sem=send.at[w, l], recv_sem=recv.at[w, l],
                                                  device_id=(x, y, 1 - c), device_id_type=MESH)
                cp.start()
                cps.append(cp)
        for w in range(nw):
            for l in range(2):
                src = h_refs[2 * w + l]
                h = src.shape[0]
                other = f_refs[w].at[l, pl.ds((1 - c) * h, h), :]
                pltpu.make_async_remote_copy(src_ref=src, dst_ref=other, send_sem=send.at[w, l], recv_sem=recv.at[w, l],
                                             device_id=(x, y, c), device_id_type=MESH).wait_recv()
        for cp in cps:
            cp.wait_send()

    flat = [a for pair in halves for a in pair]
    outs = tuple(jax.ShapeDtypeStruct((2, 2 * pair[0].shape[0], pair[0].shape[1]), F32) for pair in halves)
    return pl.pallas_call(
        body, name="grad_sibling_join", out_shape=outs, in_specs=[_ANY] * (2 * nw), out_specs=tuple([_ANY] * nw),
        scratch_shapes=[pltpu.SemaphoreType.DMA((nw, 2)), pltpu.SemaphoreType.DMA((nw, 2))])(*flat)


def _allgather_small(v):
    r = v.shape[0]

    def body(x_ref, out_ref, send_sems, recv_sems, local_sem):
        x, y, c, chips = _place()
        me, sibling = (x, y, c), (x, y, 1 - c)

        def slab(px, py, pc):
            return out_ref.at[4 * px + 2 * py + pc]

        def copy(k, block, to, src=None):
            return pltpu.make_async_remote_copy(src_ref=slab(*block) if src is None else src, dst_ref=slab(*block),
                                                send_sem=send_sems.at[k], recv_sem=recv_sems.at[k],
                                                device_id=to, device_id_type=MESH)

        mine = pltpu.make_async_copy(x_ref, slab(*me), local_sem)
        mine.start()
        first = [copy(0, me, sibling, src=x_ref)]
        first += [copy(1 + j, me, (*chip, c), src=x_ref) for j, chip in enumerate(chips)]
        for cp in first:
            cp.start()
        passed = [copy(4 + j, (*chip, c), sibling) for j, chip in enumerate(chips)]
        for j, chip in enumerate(chips):
            copy(1 + j, (*chip, c), me).wait_recv()
            passed[j].start()
        copy(0, sibling, me).wait_recv()
        for j, chip in enumerate(chips):
            copy(4 + j, (*chip, 1 - c), me).wait_recv()
        for cp in first + passed:
            cp.wait_send()
        mine.wait()

    return pl.pallas_call(
        body, name="allgather_small", out_shape=jax.ShapeDtypeStruct((8, r, 128), F32),
        in_specs=[pl.BlockSpec(memory_space=pltpu.VMEM)], out_specs=pl.BlockSpec(memory_space=pltpu.VMEM),
        scratch_shapes=[pltpu.SemaphoreType.DMA((7,)), pltpu.SemaphoreType.DMA((7,)), pltpu.SemaphoreType.DMA],
    )(v)


def _add_half(name, d, a, c):
    _, h, cols = a.shape
    nt = 1
    th = h // nt

    def body(c_ref, d_ref, a_ref, o_ref):
        o_ref[...] = (d_ref[...] + a_ref[...]).astype(BF16)

    return _call(body, name, jax.ShapeDtypeStruct(a.shape, BF16), (NCHIP, nt),
                 [pl.BlockSpec((None, th, cols), lambda p, i, cr: (p, cr[0] * nt + i, 0)),
                  pl.BlockSpec((None, th, cols), lambda p, i, cr: (p, i, 0))],
                 pl.BlockSpec((None, th, cols), lambda p, i, cr: (p, i, 0)), prefetch=1)(c, d, a)


def _add_chips(name, d, a, b, where):
    _, h, cols = a.shape
    th = h // 4 if (h % 64 == 0) else h
    nt = h // th

    def body(w_ref, d_ref, a_ref, b_ref, o_ref):
        own = d_ref[...] + a_ref[...]
        o_ref[...] = ((own + b_ref[0].astype(F32)) + b_ref[1].astype(F32)) + b_ref[2].astype(F32)

    return _call(body, name, jax.ShapeDtypeStruct((h, cols), F32), (nt,),
                 [pl.BlockSpec((None, th, cols), lambda i, wr: (wr[0], wr[1] * nt + i, 0)),
                  pl.BlockSpec((None, th, cols), lambda i, wr: (wr[0], i, 0)),
                  pl.BlockSpec((3, th, cols), lambda i, wr: (0, i, 0))],
                 pl.BlockSpec((th, cols), lambda i, wr: (i, 0)), prefetch=1)(where, d, a, b)


def _pack(arrs):
    pieces = []
    for a in arrs:
        f = a.reshape(-1)
        n = -(-f.shape[0] // 1024) * 1024
        pieces.append(jnp.pad(f, (0, n - f.shape[0])).reshape(-1, 128))
    return jnp.concatenate(pieces, axis=0)


def _unpack(buf, shapes):
    out, r = [], 0
    for s in shapes:
        n = int(np.prod(s))
        rows = -(-n // 1024) * 8
        out.append(buf[r:r + rows].reshape(-1)[:n].reshape(s))
        r += rows
    return out


def kernel(x, meta_tokens, in_ln_g, in_ln_b, rel_bias, w_in, b_in, attn_sinks, w_attn_proj, conv_dw, conv_dw_b, conv_ln_g, conv_ln_b, w_conv_proj, w_out, ln1_g, ln1_b, ffn_w_up, ffn_dw, ffn_dw_b, ffn_w_down, ln2_g, ln2_b, loss_target, m_meta_tokens, m_in_ln_g, m_in_ln_b, m_rel_bias, m_w_in, m_b_in, m_attn_sinks, m_w_attn_proj, m_conv_dw, m_conv_dw_b, m_conv_ln_g, m_conv_ln_b, m_w_conv_proj, m_w_out, m_ln1_g, m_ln1_b, m_ffn_w_up, m_ffn_dw, m_ffn_dw_b, m_ffn_w_down, m_ln2_g, m_ln2_b, v_meta_tokens, v_in_ln_g, v_in_ln_b, v_rel_bias, v_w_in, v_b_in, v_attn_sinks, v_w_attn_proj, v_conv_dw, v_conv_dw_b, v_conv_ln_g, v_conv_ln_b, v_w_conv_proj, v_w_out, v_ln1_g, v_ln1_b, v_ffn_w_up, v_ffn_dw, v_ffn_dw_b, v_ffn_w_down, v_ln2_g, v_ln2_b):
    nex, seq, _ = x.shape
    nblk = seq // BLK + 1
    tp = nblk * BLK
    m = nex * tp
    tm = _row_tile(m)
    ffs = ffn_w_up.shape[2]
    dff = 2 * ffs
    cx, cy, cc = lax.axis_index("x"), lax.axis_index("y"), lax.axis_index("c")
    chip = (2 * cx + cy).astype(jnp.int32)
    core = cc.astype(jnp.int32)

    names = ("in", "ap", "cp", "out", "up", "down")
    big = dict(zip(names, [w_in, w_attn_proj, w_conv_proj, w_out, ffn_w_up, ffn_w_down]))
    sb = {k: v.astype(BF16) for k, v in big.items()}
    gathered = {}

    first_items = [(meta_tokens.reshape(2, N_META // 2, -1), None), (conv_dw, None), (ffn_dw, None)]
    g_meta, g_cdw, g_fdw = _run_job("gather_small", _gather_job(first_items))
    meta_full = jnp.transpose(g_meta, (1, 2, 0, 3)).reshape(N_META, D)
    bias_tab = _bias_build(rel_bias)

    fwd_plan = {("embed_ln", 0): [("in", 0)],
                ("in_proj", 0): [("ap", 0), ("cp", 0), ("out", 0)], ("attn_fwd", 0): [("up", 0)],
                ("conv_fwd", 0): [("down", 0)], ("mix_fwd", 0): [("in", 1)],
                ("out_proj_ln", 0): [("ap", 1), ("cp", 1), ("out", 1)], ("ffn_up_act", 0): [("up", 1), ("down", 1)]}

    def fwd(tag, l, fn, *args):
        keys = fwd_plan.get((tag, l))
        if not keys:
            return fn(*args)
        items = [(sb[k], kl) for k, kl in keys]
        res = fn(*args, side=_gather_job(items))
        for key, g in zip(keys, res[-len(keys):]):
            gathered[key] = g
        main = res[:-len(keys)]
        return main[0] if len(main) == 1 else main

    def layer_weights(l):
        return dict(
            win=_w_in_from_chips(gathered[("in", l)]), bin=_to_new(b_in[l]).reshape(1, IN_COLS),
            cdw=jnp.transpose(g_cdw[:, l], (1, 0, 2)).reshape(CTAPS, CW),
            fdw=jnp.transpose(g_fdw[:, l], (1, 0, 2)).reshape(FTAPS, 2 * dff),
            fdwb=ffn_dw_b[l].reshape(1, 2 * dff))

    raw, h, hb = fwd("embed_ln", 0, _embed_ln, x, meta_full, in_ln_g, in_ln_b, nblk)
    saved, lw = [], []
    for l in range(DEPTH):
        p = layer_weights(l)
        z = fwd("in_proj", l, functools.partial(_mm_bias, "in_proj"), hb, p["win"], p["bin"], IN_COLS // 3, tm)
        a = fwd("attn_fwd", l, _attn_fwd, z, bias_tab, attn_sinks[l], nex, nblk)
        ccv, cs = fwd("conv_fwd", l, _conv_fwd, z, p["cdw"], conv_dw_b[l], conv_ln_g[l], conv_ln_b[l], nex, tp)
        p["wap"], p["wcp"] = gathered[("ap", l)], gathered[("cp", l)]
        ya, yc, mixed = fwd("mix_fwd", l, _mix_fwd, a, cs, p["wap"], p["wcp"], z, tm)
        p["wout"] = gathered[("out", l)].reshape(D, D)
        r1, h1, h1b = fwd("out_proj_ln", l, functools.partial(_mm_res_ln, "out_proj_ln"), mixed, p["wout"], h, ln1_g[l], ln1_b[l], tm)
        p["wup"] = gathered[("up", l)]
        up3, ug, act = fwd("ffn_up_act", l, _ffn_up_act, h1b, p["wup"], p["fdw"], p["fdwb"], tm, nex, tp)
        p["wdown"] = gathered[("down", l)].reshape(dff, D)
        r2, h2, h2b = _mm_res_ln("ffn_down_ln", act, p["wdown"], h1, ln2_g[l], ln2_b[l], tm)
        saved.append(dict(hb=hb, z=z, a=a, cc=ccv, cs=cs, ya=ya, yc=yc, mixed=mixed, r1=r1, h1b=h1b, up3=up3, ug=ug, act=act, r2=r2))
        lw.append(p)
        h, hb = h2, h2b

    dy, sq = _loss_grad(h, loss_target, nblk)

    grads, swapped, pair_sums, reduced = {}, {}, {}, {}
    cvec, where = core.reshape(1), jnp.stack([chip, core])
    last = [(k, DEPTH - 1) for k in names]
    bwd_plan = {("ln2_bwd", 0): ("swap", last),
                ("ffn_bwd", 0): ("exch", [("up", 1), ("down", 1)]),
                ("dw_down", 0): ("exch", [("ap", 1), ("cp", 1), ("out", 1)]),
                ("dw_up", 0): ("exch", [("in", 1)]),
                ("ln1_bwd", 0): ("swap", [("down", 0), ("up", 0)]),
                ("conv_bwd_b", 0): ("swap", [("out", 0), ("ap", 0), ("cp", 0)]),
                ("attn_bwd", 0): ("exch", [("down", 0), ("up", 0)]),
                ("dw_in_gates", 0): ("exch", [("out", 0), ("ap", 0), ("cp", 0)]),
                ("in_ln_bwd", 0): ("swap", [("in", 0)])}

    def after(kind, keys, outs):
        for key, o in zip(keys, outs):
            if kind == "swap":
                swapped[key] = o
                pair_sums[key] = _add_half("grad_add_sibling", grads[key], o, cvec)
            else:
                reduced[key] = _add_chips("grad_add_chips", grads[key], swapped[key], o, where)

    def bwd(tag, l, fn, *args):
        plan = bwd_plan.get((tag, l))
        if plan is None:
            return fn(*args)
        kind, keys = plan
        job = _swap_job([grads[k] for k in keys]) if kind == "swap" else _exchange_job([pair_sums[k] for k in keys])
        res = fn(*args, side=job)
        after(kind, keys, res[-len(keys):])
        main = res[:-len(keys)]
        return main[0] if len(main) == 1 else main

    small = {}
    prev_a, prev_w, prev_cols = (), None, None
    dprev = dy
    for l in reversed(range(DEPTH)):
        p, s = lw[l], saved[l]
        dr2, dr2b, dg2, db2 = bwd("ln2_bwd", l, functools.partial(_ln_bwd_call, "ln2_bwd"), dprev, s["r2"], ln2_g[l], tm,
                                  prev_a, prev_w, prev_cols)
        dpre3, dfdw, dfdwb = bwd("ffn_bwd", l, _ffn_bwd, dr2b, p["wdown"], s["ug"], s["up3"], p["fdw"], tm, nex, tp)
        grads[("down", l)] = bwd("dw_down", l, functools.partial(_mm_tn, "dw_down"), s["act"], dr2b, ffs,
                                 D // 2).reshape(NCHIP, dff // NCHIP, D)
        grads[("up", l)] = bwd("dw_up", l, functools.partial(_mm_tn, "dw_up"), s["h1b"], dpre3, D, ffs,
                               lambda j: (j // 2, j % 2), True)
        dr1, dr1b, dg1, db1 = bwd("ln1_bwd", l, functools.partial(_ln_bwd_call, "ln1_bwd"), dr2, s["r1"], ln1_g[l], tm // 2,
                                  (dpre3,), p["wup"])
        dya, dyc, dzg, csg = _gate_bwd(dr1b, p["wout"], s["ya"], s["yc"], s["z"], tm)
        grads[("out", l)] = _mm_tn("dw_out", s["mixed"], dr1b, D, D // 2).reshape(NCHIP, D // NCHIP, D)
        grads[("ap", l)] = _mm_tn("dw_attn_proj", s["a"], dya, AW, D // NCHIP, chip_out=True)
        grads[("cp", l)] = _mm_tn("dw_conv_proj", s["cs"], dyc, CW, D // NCHIP, chip_out=True)
        dcc, dclg, dclb, dcwb = _conv_bwd_a(dyc, p["wcp"], s["cc"], conv_ln_g[l], conv_ln_b[l], tm)
        dzc, dcdw, csc = bwd("conv_bwd_b", l, _conv_bwd_b, dcc, s["z"], p["cdw"], nex, tp)
        dzq, csq, dsk, dbias = bwd("attn_bwd", l, _attn_bwd, s["z"], bias_tab, attn_sinks[l], dya, p["wap"], nex, nblk)
        gin = [bwd("dw_in_gates", l, functools.partial(_mm_tn, "dw_in_gates"), s["hb"], dzg, D, D // 2),
               _mm_tn("dw_in_conv", s["hb"], dzc, D, CW),
               _mm_tn("dw_in_qkv", s["hb"], dzq, D, 2 * KVW)]
        grads[("in", l)] = _w_in_to_chips(gin)
        small[l] = dict(
            b_in=_to_old(jnp.concatenate([csg, csc, csq], axis=1)).reshape(IN_COLS), attn_sinks=dsk[0, :NQ],
            conv_dw=dcdw[:CTAPS], conv_dw_b=dcwb.reshape(CW), conv_ln_g=dclg.reshape(CW), conv_ln_b=dclb.reshape(CW),
            ln1_g=dg1.reshape(D), ln1_b=db1.reshape(D),
            ffn_dw=jnp.transpose(dfdw, (1, 0, 2)).reshape(FTAPS, 2 * dff), ffn_dw_b=jnp.transpose(dfdwb, (1, 0, 2)).reshape(2 * dff),
            ln2_g=dg2.reshape(D), ln2_b=db2.reshape(D), bias=dbias)
        dprev = dr1
        prev_a, prev_w, prev_cols = (dzg, dzc, dzq), p["win"], [(C_GATES, C_CONV), (C_CONV, C_QKV), (C_QKV, IN_COLS)]
    draw, _, dg0, db0 = bwd("in_ln_bwd", 0, functools.partial(_ln_bwd_call, "in_ln_bwd"), dprev, raw, in_ln_g, tm,
                            prev_a, prev_w, prev_cols)
    draw3 = draw.reshape(nex, tp, D)
    grad_x = draw3[:, BLK:, :]
    dmeta = _sum0("meta_grad_sum", draw3[:, PAD:BLK, :])

    names_l = ["b_in", "attn_sinks", "conv_dw", "conv_dw_b", "conv_ln_g", "conv_ln_b", "ln1_g", "ln1_b", "ffn_dw", "ffn_dw_b", "ln2_g", "ln2_b"]
    bwd_plan[("bias_grad", 0)] = ("exch", [("in", 0)])
    dbias_all = bwd("bias_grad", 0, _bias_grad, jnp.concatenate([small[l]["bias"] for l in range(DEPTH)], axis=0))
    part_list = [sq, dmeta, dg0.reshape(D), db0.reshape(D), dbias_all]
    part_list += [jnp.stack([small[0][n], small[1][n]]) for n in names_l]
    shapes_small = [tuple(a.shape) for a in part_list]
    tot = _sum0("small_grad_sum", _allgather_small(_pack(part_list)))
    (sq_all, g_meta_f, g_inlg, g_inlb, g_biasp, g_bin, g_sinks, g_cdw_f, g_cdwb, g_clg, g_clb, g_l1g, g_l1b, g_fdw_f, g_fdwb,
     g_l2g, g_l2b) = _unpack(tot, shapes_small)
    loss = 0.5 / D * jnp.sum(sq_all)
    g_relb = g_biasp
    csh = D // NCHIP
    g_meta_s = lax.dynamic_slice_in_dim(g_meta_f, chip * csh, csh, axis=1)
    g_cdw_s = lax.dynamic_slice_in_dim(g_cdw_f, chip * (CW // NCHIP), CW // NCHIP, axis=2)
    g_fdw_s = lax.dynamic_slice_in_dim(g_fdw_f, chip * ffs, ffs, axis=2)

    joined = _sibling_join([[reduced[(k, l)] for l in range(DEPTH)] for k in names])

    moms = [m_w_in, m_w_attn_proj, m_w_conv_proj, m_w_out, m_ffn_w_up, m_ffn_w_down]
    vels = [v_w_in, v_w_attn_proj, v_w_conv_proj, v_w_out, v_ffn_w_up, v_ffn_w_down]
    big_out = [_adamw_matrix(big[k], f, [reduced[(k, l)] for l in range(DEPTH)], mo, ve, cvec)
               for k, f, mo, ve in zip(names, joined, moms, vels)]

    sm_w = [meta_tokens, in_ln_g, in_ln_b, rel_bias, b_in, attn_sinks, conv_dw, conv_dw_b, conv_ln_g, conv_ln_b, ln1_g, ln1_b,
            ffn_dw, ffn_dw_b, ln2_g, ln2_b]
    sm_m = [m_meta_tokens, m_in_ln_g, m_in_ln_b, m_rel_bias, m_b_in, m_attn_sinks, m_conv_dw, m_conv_dw_b, m_conv_ln_g, m_conv_ln_b,
            m_ln1_g, m_ln1_b, m_ffn_dw, m_ffn_dw_b, m_ln2_g, m_ln2_b]
    sm_v = [v_meta_tokens, v_in_ln_g, v_in_ln_b, v_rel_bias, v_b_in, v_attn_sinks, v_conv_dw, v_conv_dw_b, v_conv_ln_g, v_conv_ln_b,
            v_ln1_g, v_ln1_b, v_ffn_dw, v_ffn_dw_b, v_ln2_g, v_ln2_b]
    sm_g = [g_meta_s, g_inlg, g_inlb, g_relb, g_bin, g_sinks, g_cdw_s, g_cdwb, g_clg, g_clb, g_l1g, g_l1b, g_fdw_s, g_fdwb, g_l2g, g_l2b]
    sm_shapes = [tuple(a.shape) for a in sm_w]
    sd, smn, svn = _adamw("adamw_small", _pack(sm_w), _pack(sm_g), _pack(sm_m), _pack(sm_v))
    sd, smn, svn = _unpack(sd, sm_shapes), _unpack(smn, sm_shapes), _unpack(svn, sm_shapes)

    order = ["meta_tokens", "in_ln_g", "in_ln_b", "rel_bias", "w_in", "b_in", "attn_sinks", "w_attn_proj", "conv_dw", "conv_dw_b",
             "conv_ln_g", "conv_ln_b", "w_conv_proj", "w_out", "ln1_g", "ln1_b", "ffn_w_up", "ffn_dw", "ffn_dw_b", "ffn_w_down",
             "ln2_g", "ln2_b"]
    small_names = ["meta_tokens", "in_ln_g", "in_ln_b", "rel_bias", "b_in", "attn_sinks", "conv_dw", "conv_dw_b", "conv_ln_g",
                   "conv_ln_b", "ln1_g", "ln1_b", "ffn_dw", "ffn_dw_b", "ln2_g", "ln2_b"]
    big_names = ["w_in", "w_attn_proj", "w_conv_proj", "w_out", "ffn_w_up", "ffn_w_down"]
    res = {}
    for i, n in enumerate(small_names):
        res[n] = (sm_g[i], sd[i], smn[i], svn[i])
    for i, n in enumerate(big_names):
        res[n] = big_out[i]
    outs = [loss, grad_x]
    for k in range(4):
        outs += [res[n][k] for n in order]
    return tuple(outs)
```
